```python
import jax, jax.numpy as jnp
from jax import lax
import numpy as np

D_MODEL = 1024
BATCH = 16
SEQ = 2048
DEPTH = 4

CHUNK = 64
N_MIXERS = 3
EPS = 1e-6

GMLP_BLOCK = 128
A_WIDTH = D_MODEL
A_GROUPS = 8
A_GROUP_DIM = A_WIDTH // A_GROUPS

B_WIDTH = D_MODEL
CONV_WIDTH = 3

C_WIDTH = D_MODEL
POOL_WINDOWS = (2, 4, 8, 16)
C_GROUPS = len(POOL_WINDOWS)
C_GROUP_DIM = C_WIDTH // C_GROUPS

FFN_HIDDEN = ((8 * D_MODEL + 3 * 256 - 1) // (3 * 256)) * 256

N_A = (DEPTH + 2) // 3
N_B = (DEPTH + 1) // 3
N_C = DEPTH // 3

kernel_name = "interleaved_gmlp_shortconv_pool_trunk"


def _rms_norm(x, g):
    x32 = x.astype(jnp.float32)
    y = x32 * lax.rsqrt(jnp.mean(x32 * x32, axis=-1, keepdims=True) + EPS)
    return (y * g.astype(jnp.float32)).astype(x.dtype)


def _chunk_causal_mask(n):
    pos = jnp.arange(n)
    return (pos[None, :] // CHUNK) <= (pos[:, None] // CHUNK)


def _spatial_gating_mixer(h, w_in, v_norm_g, w_s, b_s, w_out):
    b, s, _ = h.shape
    z = jax.nn.gelu(h @ w_in, approximate=False)
    u, v = jnp.split(z, 2, axis=-1)
    v32 = v.astype(jnp.float32)
    mu = jnp.mean(v32, axis=-1, keepdims=True)
    var = jnp.mean(jnp.square(v32 - mu), axis=-1, keepdims=True)
    v = ((v32 - mu) * lax.rsqrt(var + EPS) * v_norm_g.astype(jnp.float32)).astype(h.dtype)
    v = v.reshape(b, s // GMLP_BLOCK, GMLP_BLOCK, A_GROUPS, A_GROUP_DIM)
    w_masked = jnp.where(_chunk_causal_mask(GMLP_BLOCK)[None], w_s, 0)
    sv = jnp.einsum('gij,bnjgc->bnigc', w_masked, v) + b_s.T[None, None, :, :, None]
    y = u * sv.reshape(b, s, A_WIDTH)
    return y @ w_out


def _causal_depthwise_conv(x, conv_w):
    return lax.conv_general_dilated(
        x, conv_w[:, None, :],
        window_strides=(1,),
        padding=[(CONV_WIDTH - 1, 0)],
        dimension_numbers=('NWC', 'WIO', 'NWC'),
        feature_group_count=x.shape[-1])


def _short_conv_mixer(h, w_in, conv_w, w_out):
    gate_b, gate_c, xt = jnp.split(h @ w_in, 3, axis=-1)
    y = _causal_depthwise_conv(gate_c * xt, conv_w)
    return (gate_b * y) @ w_out


def _multiscale_pool_mixer(h, w_in, w_grp, scale, w_out):
    b, s, _ = h.shape
    p = (h @ w_in).reshape(b, s, C_GROUPS, C_GROUP_DIM)
    p32 = p.astype(jnp.float32)
    cs = jnp.cumsum(p32, axis=1)
    t = jnp.arange(1, s + 1, dtype=jnp.float32)
    outs = []
    for g, w in enumerate(POOL_WINDOWS):
        csg = cs[:, :, g]
        shifted = jnp.pad(csg[:, :s - w], ((0, 0), (w, 0), (0, 0)))
        mean = (csg - shifted) / jnp.minimum(t, w)[None, :, None]
        outs.append(mean - p32[:, :, g])
    d = jnp.stack(outs, axis=2).astype(h.dtype)
    y = jnp.einsum('bsgc,gcd->bsgd', d, w_grp).reshape(b, s, C_WIDTH) * scale
    return y @ w_out


def _swiglu(h, w_gate, w_up, w_down):
    return (jax.nn.silu(h @ w_gate) * (h @ w_up)) @ w_down


def _fwd_setup_inputs(seed: int = 0) -> dict:
    key = jax.random.key(seed)
    ks = jax.random.split(key, 24)
    f32 = jnp.float32

    def nrm(k, shape, scale):
        return jax.random.normal(k, shape, f32) * scale

    x = jax.random.normal(ks[0], (BATCH, SEQ, D_MODEL), f32)
    norm_mix_g = 1.0 + nrm(ks[1], (DEPTH, D_MODEL), 0.05)
    norm_ffn_g = 1.0 + nrm(ks[2], (DEPTH, D_MODEL), 0.05)
    final_norm_g = 1.0 + nrm(ks[3], (D_MODEL,), 0.05)

    a_w_in = nrm(ks[4], (N_A, D_MODEL, 2 * A_WIDTH), D_MODEL ** -0.5)
    a_v_norm_g = 1.0 + nrm(ks[5], (N_A, A_WIDTH), 0.05)
    a_w_s = nrm(ks[6], (N_A, A_GROUPS, GMLP_BLOCK, GMLP_BLOCK), GMLP_BLOCK ** -0.5)
    a_b_s = 1.0 + nrm(ks[7], (N_A, A_GROUPS, GMLP_BLOCK), 0.05)
    a_w_out = nrm(ks[8], (N_A, A_WIDTH, D_MODEL), A_WIDTH ** -0.5)

    b_w_in = nrm(ks[9], (N_B, D_MODEL, 3 * B_WIDTH), D_MODEL ** -0.5)
    b_conv_w = nrm(ks[10], (N_B, CONV_WIDTH, B_WIDTH), CONV_WIDTH ** -0.5)
    b_w_out = nrm(ks[11], (N_B, B_WIDTH, D_MODEL), B_WIDTH ** -0.5)

    c_w_in = nrm(ks[12], (N_C, D_MODEL, C_WIDTH), D_MODEL ** -0.5)
    c_w_grp = nrm(ks[13], (N_C, C_GROUPS, C_GROUP_DIM, C_GROUP_DIM), C_GROUP_DIM ** -0.5)
    c_scale = 1.0 + nrm(ks[14], (N_C, C_WIDTH), 0.1)
    c_w_out = nrm(ks[15], (N_C, C_WIDTH, D_MODEL), C_WIDTH ** -0.5)

    f_w_gate = nrm(ks[16], (DEPTH, D_MODEL, FFN_HIDDEN), D_MODEL ** -0.5)
    f_w_up = nrm(ks[17], (DEPTH, D_MODEL, FFN_HIDDEN), D_MODEL ** -0.5)
    f_w_down = nrm(ks[18], (DEPTH, FFN_HIDDEN, D_MODEL), FFN_HIDDEN ** -0.5)

    return {
        "x": x,
        "norm_mix_g": norm_mix_g, "norm_ffn_g": norm_ffn_g, "final_norm_g": final_norm_g,
        "a_w_in": a_w_in, "a_v_norm_g": a_v_norm_g, "a_w_s": a_w_s, "a_b_s": a_b_s,
        "a_w_out": a_w_out,
        "b_w_in": b_w_in, "b_conv_w": b_conv_w, "b_w_out": b_w_out,
        "c_w_in": c_w_in, "c_w_grp": c_w_grp, "c_scale": c_scale, "c_w_out": c_w_out,
        "f_w_gate": f_w_gate, "f_w_up": f_w_up, "f_w_down": f_w_down,
    }


def _fwd_reference(x, norm_mix_g, norm_ffn_g, final_norm_g,
              a_w_in, a_v_norm_g, a_w_s, a_b_s, a_w_out,
              b_w_in, b_conv_w, b_w_out,
              c_w_in, c_w_grp, c_scale, c_w_out,
              f_w_gate, f_w_up, f_w_down):
    for i in range(DEPTH):
        kind = i % N_MIXERS
        j = i // N_MIXERS
        h = _rms_norm(x, norm_mix_g[i])
        if kind == 0:
            m = _spatial_gating_mixer(h, a_w_in[j], a_v_norm_g[j], a_w_s[j], a_b_s[j], a_w_out[j])
        elif kind == 1:
            m = _short_conv_mixer(h, b_w_in[j], b_conv_w[j], b_w_out[j])
        else:
            m = _multiscale_pool_mixer(h, c_w_in[j], c_w_grp[j], c_scale[j], c_w_out[j])
        x = x + m
        h = _rms_norm(x, norm_ffn_g[i])
        x = x + _swiglu(h, f_w_gate[i], f_w_up[i], f_w_down[i])
    return _rms_norm(x, final_norm_g)


import jax as _jax
import jax.numpy as _jnp

TWIN_FORMAT = 'train_step'
FWD_PARAMS = ['x', 'norm_mix_g', 'norm_ffn_g', 'final_norm_g', 'a_w_in', 'a_v_norm_g', 'a_w_s', 'a_b_s', 'a_w_out', 'b_w_in', 'b_conv_w', 'b_w_out', 'c_w_in', 'c_w_grp', 'c_scale', 'c_w_out', 'f_w_gate', 'f_w_up', 'f_w_down']
TWIN_WEIGHTS = ['norm_mix_g', 'norm_ffn_g', 'final_norm_g', 'a_w_in', 'a_v_norm_g', 'a_w_s', 'a_b_s', 'a_w_out', 'b_w_in', 'b_conv_w', 'b_w_out', 'c_w_in', 'c_w_grp', 'c_scale', 'c_w_out', 'f_w_gate', 'f_w_up', 'f_w_down']
TWIN_DIFF_INPUT = 'x'
TWIN_INPUTS = ['x', 'norm_mix_g', 'norm_ffn_g', 'final_norm_g', 'a_w_in', 'a_v_norm_g', 'a_w_s', 'a_b_s', 'a_w_out', 'b_w_in', 'b_conv_w', 'b_w_out', 'c_w_in', 'c_w_grp', 'c_scale', 'c_w_out', 'f_w_gate', 'f_w_up', 'f_w_down', 'loss_target', 'm_norm_mix_g', 'm_norm_ffn_g', 'm_final_norm_g', 'm_a_w_in', 'm_a_v_norm_g', 'm_a_w_s', 'm_a_b_s', 'm_a_w_out', 'm_b_w_in', 'm_b_conv_w', 'm_b_w_out', 'm_c_w_in', 'm_c_w_grp', 'm_c_scale', 'm_c_w_out', 'm_f_w_gate', 'm_f_w_up', 'm_f_w_down', 'v_norm_mix_g', 'v_norm_ffn_g', 'v_final_norm_g', 'v_a_w_in', 'v_a_v_norm_g', 'v_a_w_s', 'v_a_b_s', 'v_a_w_out', 'v_b_w_in', 'v_b_conv_w', 'v_b_w_out', 'v_c_w_in', 'v_c_w_grp', 'v_c_scale', 'v_c_w_out', 'v_f_w_gate', 'v_f_w_up', 'v_f_w_down']
TWIN_OUTPUTS = ['loss', 'grad_x', 'grad_norm_mix_g', 'grad_norm_ffn_g', 'grad_final_norm_g', 'grad_a_w_in', 'grad_a_v_norm_g', 'grad_a_w_s', 'grad_a_b_s', 'grad_a_w_out', 'grad_b_w_in', 'grad_b_conv_w', 'grad_b_w_out', 'grad_c_w_in', 'grad_c_w_grp', 'grad_c_scale', 'grad_c_w_out', 'grad_f_w_gate', 'grad_f_w_up', 'grad_f_w_down', 'delta_norm_mix_g', 'delta_norm_ffn_g', 'delta_final_norm_g', 'delta_a_w_in', 'delta_a_v_norm_g', 'delta_a_w_s', 'delta_a_b_s', 'delta_a_w_out', 'delta_b_w_in', 'delta_b_conv_w', 'delta_b_w_out', 'delta_c_w_in', 'delta_c_w_grp', 'delta_c_scale', 'delta_c_w_out', 'delta_f_w_gate', 'delta_f_w_up', 'delta_f_w_down', 'new_m_norm_mix_g', 'new_m_norm_ffn_g', 'new_m_final_norm_g', 'new_m_a_w_in', 'new_m_a_v_norm_g', 'new_m_a_w_s', 'new_m_a_b_s', 'new_m_a_w_out', 'new_m_b_w_in', 'new_m_b_conv_w', 'new_m_b_w_out', 'new_m_c_w_in', 'new_m_c_w_grp', 'new_m_c_scale', 'new_m_c_w_out', 'new_m_f_w_gate', 'new_m_f_w_up', 'new_m_f_w_down', 'new_v_norm_mix_g', 'new_v_norm_ffn_g', 'new_v_final_norm_g', 'new_v_a_w_in', 'new_v_a_v_norm_g', 'new_v_a_w_s', 'new_v_a_b_s', 'new_v_a_w_out', 'new_v_b_w_in', 'new_v_b_conv_w', 'new_v_b_w_out', 'new_v_c_w_in', 'new_v_c_w_grp', 'new_v_c_scale', 'new_v_c_w_out', 'new_v_f_w_gate', 'new_v_f_w_up', 'new_v_f_w_down']
TWIN_LEAF_KINDS = {'loss': 'loss', 'grad_x': 'grad_x', 'grad_norm_mix_g': 'grad_w', 'grad_norm_ffn_g': 'grad_w', 'grad_final_norm_g': 'grad_w', 'grad_a_w_in': 'grad_w', 'grad_a_v_norm_g': 'grad_w', 'grad_a_w_s': 'grad_w', 'grad_a_b_s': 'grad_w', 'grad_a_w_out': 'grad_w', 'grad_b_w_in': 'grad_w', 'grad_b_conv_w': 'grad_w', 'grad_b_w_out': 'grad_w', 'grad_c_w_in': 'grad_w', 'grad_c_w_grp': 'grad_w', 'grad_c_scale': 'grad_w', 'grad_c_w_out': 'grad_w', 'grad_f_w_gate': 'grad_w', 'grad_f_w_up': 'grad_w', 'grad_f_w_down': 'grad_w', 'delta_norm_mix_g': 'delta_w', 'delta_norm_ffn_g': 'delta_w', 'delta_final_norm_g': 'delta_w', 'delta_a_w_in': 'delta_w', 'delta_a_v_norm_g': 'delta_w', 'delta_a_w_s': 'delta_w', 'delta_a_b_s': 'delta_w', 'delta_a_w_out': 'delta_w', 'delta_b_w_in': 'delta_w', 'delta_b_conv_w': 'delta_w', 'delta_b_w_out': 'delta_w', 'delta_c_w_in': 'delta_w', 'delta_c_w_grp': 'delta_w', 'delta_c_scale': 'delta_w', 'delta_c_w_out': 'delta_w', 'delta_f_w_gate': 'delta_w', 'delta_f_w_up': 'delta_w', 'delta_f_w_down': 'delta_w', 'new_m_norm_mix_g': 'new_m', 'new_m_norm_ffn_g': 'new_m', 'new_m_final_norm_g': 'new_m', 'new_m_a_w_in': 'new_m', 'new_m_a_v_norm_g': 'new_m', 'new_m_a_w_s': 'new_m', 'new_m_a_b_s': 'new_m', 'new_m_a_w_out': 'new_m', 'new_m_b_w_in': 'new_m', 'new_m_b_conv_w': 'new_m', 'new_m_b_w_out': 'new_m', 'new_m_c_w_in': 'new_m', 'new_m_c_w_grp': 'new_m', 'new_m_c_scale': 'new_m', 'new_m_c_w_out': 'new_m', 'new_m_f_w_gate': 'new_m', 'new_m_f_w_up': 'new_m', 'new_m_f_w_down': 'new_m', 'new_v_norm_mix_g': 'new_v', 'new_v_norm_ffn_g': 'new_v', 'new_v_final_norm_g': 'new_v', 'new_v_a_w_in': 'new_v', 'new_v_a_v_norm_g': 'new_v', 'new_v_a_w_s': 'new_v', 'new_v_a_b_s': 'new_v', 'new_v_a_w_out': 'new_v', 'new_v_b_w_in': 'new_v', 'new_v_b_conv_w': 'new_v', 'new_v_b_w_out': 'new_v', 'new_v_c_w_in': 'new_v', 'new_v_c_w_grp': 'new_v', 'new_v_c_scale': 'new_v', 'new_v_c_w_out': 'new_v', 'new_v_f_w_gate': 'new_v', 'new_v_f_w_up': 'new_v', 'new_v_f_w_down': 'new_v'}


def _forward(args):
    return _fwd_reference(*[args[k] for k in FWD_PARAMS])


def _output_shape():
    out = _jax.eval_shape(lambda: _forward(_fwd_setup_inputs(0)))
    return out.shape, out.dtype

N_MICROBATCH = 1
ADAM_LR = 0.001
ADAM_B1 = 0.9
ADAM_B2 = 0.999
ADAM_EPS = 1e-08
ADAM_WD = 0.01
ADAM_STEP = 10
PER_EXAMPLE_BATCH_AXIS = {'x': 0, 'loss_target': 0}
SHARED_INPUTS = []
_WEIGHT_DTYPES = {'norm_mix_g': _jnp.float32, 'norm_ffn_g': _jnp.float32, 'final_norm_g': _jnp.float32, 'a_w_in': _jnp.float32, 'a_v_norm_g': _jnp.float32, 'a_w_s': _jnp.float32, 'a_b_s': _jnp.float32, 'a_w_out': _jnp.float32, 'b_w_in': _jnp.float32, 'b_conv_w': _jnp.float32, 'b_w_out': _jnp.float32, 'c_w_in': _jnp.float32, 'c_w_grp': _jnp.float32, 'c_scale': _jnp.float32, 'c_w_out': _jnp.float32, 'f_w_gate': _jnp.float32, 'f_w_up': _jnp.float32, 'f_w_down': _jnp.float32}
MOMENT_SCALE = {'norm_mix_g': 1.703456e-01, 'norm_ffn_g': 1.071345e-01, 'final_norm_g': 3.207169e+01, 'a_w_in': 1.295819e-01, 'a_v_norm_g': 1.004196e-01, 'a_w_s': 9.361586e-02, 'a_b_s': 1.119388e-01, 'a_w_out': 1.471364e-01, 'b_w_in': 1.195484e-01, 'b_conv_w': 1.223218e-01, 'b_w_out': 1.212977e-01, 'c_w_in': 8.706406e-02, 'c_w_grp': 8.683842e-02, 'c_scale': 9.369078e-02, 'c_w_out': 8.741504e-02, 'f_w_gate': 4.580447e-02, 'f_w_up': 4.451665e-02, 'f_w_down': 7.366888e-02}


def _to_microbatches(a, axis):
    t = _jnp.moveaxis(a, axis, 0)
    t = t.reshape((N_MICROBATCH, t.shape[0] // N_MICROBATCH) + t.shape[1:])
    return _jnp.moveaxis(t, 1, axis + 1)


def setup_inputs(seed: int = 0) -> dict:
    inp = _fwd_setup_inputs(seed)
    key = _jax.random.fold_in(_jax.random.key(seed), 7919)
    shape, _ = _output_shape()
    out = dict(inp)
    out["loss_target"] = _jax.random.normal(_jax.random.fold_in(key, 0), shape, _jnp.float32)
    for i, name in enumerate(TWIN_WEIGHTS):
        w = inp[name].astype(_jnp.float32)
        if MOMENT_SCALE is None:
            s = _jnp.sqrt(_jnp.mean(_jnp.square(w)) + 1e-30)
        else:
            s = MOMENT_SCALE[name]
        km, kv = _jax.random.split(_jax.random.fold_in(key, i + 1))
        out[name] = w
        out["m_" + name] = s * _jax.random.normal(km, w.shape, _jnp.float32)
        out["v_" + name] = (s * s) * _jax.random.uniform(kv, w.shape, _jnp.float32, 0.5, 1.5)
    if N_MICROBATCH > 1:
        for name, axis in PER_EXAMPLE_BATCH_AXIS.items():
            out[name] = _to_microbatches(out[name], axis)
    return {'x': out['x'], 'norm_mix_g': out['norm_mix_g'], 'norm_ffn_g': out['norm_ffn_g'], 'final_norm_g': out['final_norm_g'], 'a_w_in': out['a_w_in'], 'a_v_norm_g': out['a_v_norm_g'], 'a_w_s': out['a_w_s'], 'a_b_s': out['a_b_s'], 'a_w_out': out['a_w_out'], 'b_w_in': out['b_w_in'], 'b_conv_w': out['b_conv_w'], 'b_w_out': out['b_w_out'], 'c_w_in': out['c_w_in'], 'c_w_grp': out['c_w_grp'], 'c_scale': out['c_scale'], 'c_w_out': out['c_w_out'], 'f_w_gate': out['f_w_gate'], 'f_w_up': out['f_w_up'], 'f_w_down': out['f_w_down'], 'loss_target': out['loss_target'], 'm_norm_mix_g': out['m_norm_mix_g'], 'm_norm_ffn_g': out['m_norm_ffn_g'], 'm_final_norm_g': out['m_final_norm_g'], 'm_a_w_in': out['m_a_w_in'], 'm_a_v_norm_g': out['m_a_v_norm_g'], 'm_a_w_s': out['m_a_w_s'], 'm_a_b_s': out['m_a_b_s'], 'm_a_w_out': out['m_a_w_out'], 'm_b_w_in': out['m_b_w_in'], 'm_b_conv_w': out['m_b_conv_w'], 'm_b_w_out': out['m_b_w_out'], 'm_c_w_in': out['m_c_w_in'], 'm_c_w_grp': out['m_c_w_grp'], 'm_c_scale': out['m_c_scale'], 'm_c_w_out': out['m_c_w_out'], 'm_f_w_gate': out['m_f_w_gate'], 'm_f_w_up': out['m_f_w_up'], 'm_f_w_down': out['m_f_w_down'], 'v_norm_mix_g': out['v_norm_mix_g'], 'v_norm_ffn_g': out['v_norm_ffn_g'], 'v_final_norm_g': out['v_final_norm_g'], 'v_a_w_in': out['v_a_w_in'], 'v_a_v_norm_g': out['v_a_v_norm_g'], 'v_a_w_s': out['v_a_w_s'], 'v_a_b_s': out['v_a_b_s'], 'v_a_w_out': out['v_a_w_out'], 'v_b_w_in': out['v_b_w_in'], 'v_b_conv_w': out['v_b_conv_w'], 'v_b_w_out': out['v_b_w_out'], 'v_c_w_in': out['v_c_w_in'], 'v_c_w_grp': out['v_c_w_grp'], 'v_c_scale': out['v_c_scale'], 'v_c_w_out': out['v_c_w_out'], 'v_f_w_gate': out['v_f_w_gate'], 'v_f_w_up': out['v_f_w_up'], 'v_f_w_down': out['v_f_w_down']}


def _loss(weights, diff, rest, loss_target):
    with _jax.named_scope("forward"):
        args = {**rest, TWIN_DIFF_INPUT: diff, **{k: w.astype(_WEIGHT_DTYPES[k]) for k, w in weights.items()}}
        y = _forward(args)
    with _jax.named_scope("loss_head"):
        err = _jnp.square(y.astype(_jnp.float32) - loss_target)
        return 0.5 * _jnp.sum(_jnp.mean(err, axis=-1)) if err.ndim else 0.5 * err


def _adamw(w, g, m, v):
    m = ADAM_B1 * m + (1.0 - ADAM_B1) * g
    v = ADAM_B2 * v + (1.0 - ADAM_B2) * _jnp.square(g)
    m_hat = m / (1.0 - ADAM_B1 ** ADAM_STEP)
    v_hat = v / (1.0 - ADAM_B2 ** ADAM_STEP)
    delta = -ADAM_LR * (m_hat / (_jnp.sqrt(v_hat) + ADAM_EPS) + ADAM_WD * w)
    return delta, m, v


def reference(x, norm_mix_g, norm_ffn_g, final_norm_g, a_w_in, a_v_norm_g, a_w_s, a_b_s, a_w_out, b_w_in, b_conv_w, b_w_out, c_w_in, c_w_grp, c_scale, c_w_out, f_w_gate, f_w_up, f_w_down, loss_target, m_norm_mix_g, m_norm_ffn_g, m_final_norm_g, m_a_w_in, m_a_v_norm_g, m_a_w_s, m_a_b_s, m_a_w_out, m_b_w_in, m_b_conv_w, m_b_w_out, m_c_w_in, m_c_w_grp, m_c_scale, m_c_w_out, m_f_w_gate, m_f_w_up, m_f_w_down, v_norm_mix_g, v_norm_ffn_g, v_final_norm_g, v_a_w_in, v_a_v_norm_g, v_a_w_s, v_a_b_s, v_a_w_out, v_b_w_in, v_b_conv_w, v_b_w_out, v_c_w_in, v_c_w_grp, v_c_scale, v_c_w_out, v_f_w_gate, v_f_w_up, v_f_w_down):
    given = dict(x=x, norm_mix_g=norm_mix_g, norm_ffn_g=norm_ffn_g, final_norm_g=final_norm_g, a_w_in=a_w_in, a_v_norm_g=a_v_norm_g, a_w_s=a_w_s, a_b_s=a_b_s, a_w_out=a_w_out, b_w_in=b_w_in, b_conv_w=b_conv_w, b_w_out=b_w_out, c_w_in=c_w_in, c_w_grp=c_w_grp, c_scale=c_scale, c_w_out=c_w_out, f_w_gate=f_w_gate, f_w_up=f_w_up, f_w_down=f_w_down, loss_target=loss_target, m_norm_mix_g=m_norm_mix_g, m_norm_ffn_g=m_norm_ffn_g, m_final_norm_g=m_final_norm_g, m_a_w_in=m_a_w_in, m_a_v_norm_g=m_a_v_norm_g, m_a_w_s=m_a_w_s, m_a_b_s=m_a_b_s, m_a_w_out=m_a_w_out, m_b_w_in=m_b_w_in, m_b_conv_w=m_b_conv_w, m_b_w_out=m_b_w_out, m_c_w_in=m_c_w_in, m_c_w_grp=m_c_w_grp, m_c_scale=m_c_scale, m_c_w_out=m_c_w_out, m_f_w_gate=m_f_w_gate, m_f_w_up=m_f_w_up, m_f_w_down=m_f_w_down, v_norm_mix_g=v_norm_mix_g, v_norm_ffn_g=v_norm_ffn_g, v_final_norm_g=v_final_norm_g, v_a_w_in=v_a_w_in, v_a_v_norm_g=v_a_v_norm_g, v_a_w_s=v_a_w_s, v_a_b_s=v_a_b_s, v_a_w_out=v_a_w_out, v_b_w_in=v_b_w_in, v_b_conv_w=v_b_conv_w, v_b_w_out=v_b_w_out, v_c_w_in=v_c_w_in, v_c_w_grp=v_c_w_grp, v_c_scale=v_c_scale, v_c_w_out=v_c_w_out, v_f_w_gate=v_f_w_gate, v_f_w_up=v_f_w_up, v_f_w_down=v_f_w_down)
    weights = {n: given[n] for n in TWIN_WEIGHTS}
    shared = {n: given[n] for n in SHARED_INPUTS}
    per_example = {n: given[n] for n in ['x']}
    grad_fn = _jax.value_and_grad(_loss, argnums=(0, 1))

    def one_microbatch(ex, loss_target):
        ex = dict(ex)
        diff = ex.pop(TWIN_DIFF_INPUT)
        return grad_fn(weights, diff, {**shared, **ex}, loss_target)

    if N_MICROBATCH == 1:
        loss, (grad_w, grad_x) = one_microbatch(per_example, given["loss_target"])
    else:
        def body(carry, xs):
            loss_sum, grad_sum = carry
            l_k, (gw_k, gx_k) = one_microbatch(xs[0], xs[1])
            with _jax.named_scope("update"):
                return (loss_sum + l_k, _jax.tree.map(_jnp.add, grad_sum, gw_k)), gx_k

        init = (_jnp.zeros((), _jnp.float32), _jax.tree.map(_jnp.zeros_like, weights))
        (loss, grad_w), grad_x = _jax.lax.scan(body, init, (per_example, given["loss_target"]))
    with _jax.named_scope("update"):
        delta_w, new_m, new_v = {}, {}, {}
        for n in TWIN_WEIGHTS:
            delta_w[n], new_m[n], new_v[n] = _adamw(weights[n], grad_w[n], given["m_" + n], given["v_" + n])
    return (loss, grad_x, *[grad_w[n] for n in TWIN_WEIGHTS], *[delta_w[n] for n in TWIN_WEIGHTS],
            *[new_m[n] for n in TWIN_WEIGHTS], *[new_v[n] for n in TWIN_WEIGHTS])
```

```python
import functools

import jax
import jax.numpy as jnp
from jax import lax
from jax.experimental import pallas as pl
from jax.experimental.pallas import tpu as pltpu

f32 = jnp.float32
bf16 = jnp.bfloat16

NDEV = 8
D = 1024
EPS = 1e-6
CHUNK = 64
GBLK = 128
A_GROUPS = 8
C_GROUPS = 4
C_GDIM = 256
POOL_WINDOWS = (2, 4, 8, 16)
HALO = 16
F_LOC = 352
F_PAD = 384
DEPTH = 4
AXES = ("x", "y", "c")
MESH = pl.DeviceIdType.MESH

ADAM_LR = 0.001
ADAM_B1 = 0.9
ADAM_B2 = 0.999
ADAM_EPS = 1e-08
ADAM_WD = 0.01
ADAM_STEP = 10

VMEM_LIMIT = 56 * 1024 * 1024


def _cparams(sem):
    return pltpu.CompilerParams(dimension_semantics=sem, vmem_limit_bytes=VMEM_LIMIT)


def _gelu(z):
    return 0.5 * z * (1.0 + lax.erf(z * 0.7071067811865476))


def _gelu_grad(z):
    return 0.5 * (1.0 + lax.erf(z * 0.7071067811865476)) + z * jnp.exp(-0.5 * z * z) * 0.3989422804014327


def _dot(a, b):
    return jnp.dot(a, b, preferred_element_type=f32)


def _dot_nt(a, b):
    return lax.dot_general(a, b, (((1,), (1,)), ((), ())), preferred_element_type=f32)


def _dot_tn(a, b):
    return lax.dot_general(a, b, (((0,), (0,)), ((), ())), preferred_element_type=f32)


def norm_mm(x, g, w, l, *, tm, row_sharded=False, swiglu=False, name):
    T = x.shape[0]
    if row_sharded:
        nchunk, n = 1, w.shape[3]
        w_spec = pl.BlockSpec((NDEV, None, w.shape[2], n), lambda i, d: (0, l, 0, 0))
    else:
        nchunk, n = NDEV, w.shape[3]
        w_spec = pl.BlockSpec((None, None, D, n), lambda i, d: (d, l, 0, 0))

    def body(x_ref, g_ref, w_ref, p_ref, h_ref, *rest):
        @pl.when(pl.program_id(1) == 0)
        def _():
            xv = x_ref[...]
            r = lax.rsqrt(jnp.mean(xv * xv, axis=-1, keepdims=True) + EPS)
            h_ref[...] = (xv * r * g_ref[...]).astype(bf16)

        wv = w_ref[...].reshape(D, n) if row_sharded else w_ref[...]
        p = _dot(h_ref[...], wv)
        p_ref[...] = p
        if swiglu:
            gate, up = p[:, : n // 2], p[:, n // 2:]
            rest[0][...] = (gate * jax.nn.sigmoid(gate) * up).astype(bf16)

    out_shape = [jax.ShapeDtypeStruct((T, nchunk * n), f32), jax.ShapeDtypeStruct((T, D), bf16)]
    out_specs = [pl.BlockSpec((tm, n), lambda i, d: (i, d)), pl.BlockSpec((tm, D), lambda i, d: (i, 0))]
    if swiglu:
        out_shape.append(jax.ShapeDtypeStruct((T, nchunk * n // 2), bf16))
        out_specs.append(pl.BlockSpec((tm, n // 2), lambda i, d: (i, d)))
    return pl.pallas_call(
        body, grid=(T // tm, nchunk),
        in_specs=[pl.BlockSpec((tm, D), lambda i, d: (i, 0)), pl.BlockSpec((1, D), lambda i, d: (0, 0)), w_spec],
        out_specs=out_specs, out_shape=out_shape,
        compiler_params=_cparams(("parallel", "arbitrary")), name=name,
    )(x, g, w)


def mm_res(y, w, l, x, *, tm, name):
    T, K = y.shape
    k = w.shape[2]

    def body(y_ref, w_ref, x_ref, o_ref):
        o_ref[...] = x_ref[...] + _dot(y_ref[...], w_ref[...].reshape(K, D))

    return pl.pallas_call(
        body, grid=(T // tm,),
        in_specs=[pl.BlockSpec((tm, K), lambda i: (i, 0)),
                  pl.BlockSpec((NDEV, None, k, D), lambda i: (0, l, 0, 0)),
                  pl.BlockSpec((tm, D), lambda i: (i, 0))],
        out_specs=pl.BlockSpec((tm, D), lambda i: (i, 0)),
        out_shape=jax.ShapeDtypeStruct((T, D), f32),
        compiler_params=_cparams(("parallel",)), name=name,
    )(y, w, x)


def mm_nt(dy, w, l, *, tm, name):
    T = dy.shape[0]
    k = w.shape[2]
    K = NDEV * k

    def body(dy_ref, w_ref, o_ref, dyb_ref):
        dyb = dy_ref[...].astype(bf16)
        dyb_ref[...] = dyb
        o_ref[...] = _dot_nt(dyb, w_ref[...].reshape(K, D))

    return pl.pallas_call(
        body, grid=(T // tm,),
        in_specs=[pl.BlockSpec((tm, D), lambda i: (i, 0)),
                  pl.BlockSpec((NDEV, None, k, D), lambda i: (0, l, 0, 0))],
        out_specs=[pl.BlockSpec((tm, K), lambda i: (i, 0)), pl.BlockSpec((tm, D), lambda i: (i, 0))],
        out_shape=[jax.ShapeDtypeStruct((T, K), f32), jax.ShapeDtypeStruct((T, D), bf16)],
        compiler_params=_cparams(("parallel",)), name=name,
    )(dy, w)


def ffn_bwd_mid(dx, wd, l, pf, *, tm, name):
    T = dx.shape[0]

    def body(dx_ref, w_ref, p_ref, dp_ref, dxb_ref):
        @pl.when(pl.program_id(1) == 0)
        def _():
            dxb_ref[...] = dx_ref[...].astype(bf16)

        da = _dot_nt(dxb_ref[...], w_ref[...])
        gate, up = p_ref[:, :F_PAD], p_ref[:, F_PAD:]
        sg = jax.nn.sigmoid(gate)
        dp_ref[:, :F_PAD] = (da * up * (sg * (1.0 + gate * (1.0 - sg)))).astype(bf16)
        dp_ref[:, F_PAD:] = (da * gate * sg).astype(bf16)

    return pl.pallas_call(
        body, grid=(T // tm, NDEV),
        in_specs=[pl.BlockSpec((tm, D), lambda i, d: (i, 0)),
                  pl.BlockSpec((None, None, F_PAD, D), lambda i, d: (d, l, 0, 0)),
                  pl.BlockSpec((tm, 2 * F_PAD), lambda i, d: (i, d))],
        out_specs=[pl.BlockSpec((tm, 2 * F_PAD), lambda i, d: (i, d)), pl.BlockSpec((tm, D), lambda i, d: (i, 0))],
        out_shape=[jax.ShapeDtypeStruct(pf.shape, bf16), jax.ShapeDtypeStruct((T, D), bf16)],
        compiler_params=_cparams(("parallel", "arbitrary")), name=name,
    )(dx, wd, pf)


def mm_nt_normbwd(dp, w, l, x, g, dres, *, tm, row_sharded=False, name):
    T = x.shape[0]
    if row_sharded:
        nchunk, n = 1, w.shape[3]
        w_spec = pl.BlockSpec((NDEV, None, w.shape[2], n), lambda i, d: (0, l, 0, 0))
    else:
        nchunk, n = NDEV, w.shape[3]
        w_spec = pl.BlockSpec((None, None, D, n), lambda i, d: (d, l, 0, 0))

    def body(dp_ref, w_ref, x_ref, g_ref, dres_ref, dx_ref, dg_ref, acc):
        i, d = pl.program_id(0), pl.program_id(1)
        wv = w_ref[...].reshape(D, n) if row_sharded else w_ref[...]
        part = _dot_nt(dp_ref[...], wv)

        @pl.when(d == 0)
        def _():
            acc[...] = part

        @pl.when(d > 0)
        def _():
            acc[...] += part

        @pl.when(jnp.logical_and(i == 0, d == 0))
        def _():
            dg_ref[...] = jnp.zeros(dg_ref.shape, f32)

        @pl.when(d == nchunk - 1)
        def _():
            xv = x_ref[...]
            r = lax.rsqrt(jnp.mean(xv * xv, axis=-1, keepdims=True) + EPS)
            xhat = xv * r
            dh = acc[...]
            dg_ref[...] += jnp.sum(dh * xhat, axis=0, keepdims=True)
            dxhat = dh * g_ref[...]
            dx_ref[...] = dres_ref[...] + r * (dxhat - xhat * jnp.mean(dxhat * xhat, axis=-1, keepdims=True))

    return pl.pallas_call(
        body, grid=(T // tm, nchunk),
        in_specs=[pl.BlockSpec((tm, n), lambda i, d: (i, d)), w_spec,
                  pl.BlockSpec((tm, D), lambda i, d: (i, 0)), pl.BlockSpec((1, D), lambda i, d: (0, 0)),
                  pl.BlockSpec((tm, D), lambda i, d: (i, 0))],
        out_specs=[pl.BlockSpec((tm, D), lambda i, d: (i, 0)), pl.BlockSpec((1, D), lambda i, d: (0, 0))],
        out_shape=[jax.ShapeDtypeStruct((T, D), f32), jax.ShapeDtypeStruct((1, D), f32)],
        scratch_shapes=[pltpu.VMEM((tm, D), f32)],
        compiler_params=_cparams(("arbitrary", "arbitrary")), name=name,
    )(dp, w, x, g, dres)


def mm_tn(a, b, *, tm, row_sharded, name):
    T = a.shape[0]
    if row_sharded:
        k, n = a.shape[1] // NDEV, b.shape[1]
        a_spec = pl.BlockSpec((tm, k), lambda d, t: (t, d))
        b_spec = pl.BlockSpec((tm, n), lambda d, t: (t, 0))
    else:
        k, n = a.shape[1], b.shape[1] // NDEV
        a_spec = pl.BlockSpec((tm, k), lambda d, t: (t, 0))
        b_spec = pl.BlockSpec((tm, n), lambda d, t: (t, d))
    nt = T // tm

    def body(a_ref, b_ref, o_ref, acc):
        t = pl.program_id(1)
        part = _dot_tn(a_ref[...], b_ref[...])

        @pl.when(t == 0)
        def _():
            acc[...] = part

        @pl.when(t > 0)
        def _():
            acc[...] += part

        @pl.when(t == nt - 1)
        def _():
            o_ref[...] = acc[...].astype(bf16)

    return pl.pallas_call(
        body, grid=(NDEV, nt), in_specs=[a_spec, b_spec],
        out_specs=pl.BlockSpec((None, k, n), lambda d, t: (d, 0, 0)),
        out_shape=jax.ShapeDtypeStruct((NDEV, k, n), bf16),
        scratch_shapes=[pltpu.VMEM((k, n), f32)],
        compiler_params=_cparams(("parallel", "arbitrary")), name=name,
    )(a, b)


def _prev_halo_spec(tm, ncol):
    return pl.BlockSpec((HALO, ncol), lambda i: (jnp.maximum(i * (tm // HALO) - 1, 0), 0))


def _next_halo_spec(tm, ncol, T):
    return pl.BlockSpec((HALO, ncol), lambda i: (jnp.minimum((i + 1) * (tm // HALO), T // HALO - 1), 0))


def a_mid_fwd(p, gv, wm, bsb, *, tm, name):
    T = p.shape[0]

    def body(p_ref, gv_ref, wm_ref, bs_ref, y_ref, vn_sc):
        v = _gelu(p_ref[:, D:])
        vc = v - jnp.mean(v, axis=-1, keepdims=True)
        var = jnp.mean(vc * vc, axis=-1, keepdims=True)
        vn_sc[...] = (vc * lax.rsqrt(var + EPS) * gv_ref[...]).astype(bf16)
        for g in range(A_GROUPS):
            cs = slice(g * GBLK, (g + 1) * GBLK)
            for r in range(tm // GBLK):
                rs = slice(r * GBLK, (r + 1) * GBLK)
                sv = _dot(wm_ref[g], vn_sc[rs, cs]) + bs_ref[g]
                y_ref[rs, cs] = (_gelu(p_ref[rs, cs]) * sv).astype(bf16)

    return pl.pallas_call(
        body, grid=(T // tm,),
        in_specs=[pl.BlockSpec((tm, 2 * D), lambda i: (i, 0)), pl.BlockSpec((1, D), lambda i: (0, 0)),
                  pl.BlockSpec((A_GROUPS, GBLK, GBLK), lambda i: (0, 0, 0)),
                  pl.BlockSpec((A_GROUPS, GBLK, GBLK), lambda i: (0, 0, 0))],
        out_specs=pl.BlockSpec((tm, D), lambda i: (i, 0)),
        out_shape=jax.ShapeDtypeStruct((T, D), bf16),
        scratch_shapes=[pltpu.VMEM((tm, D), bf16)],
        compiler_params=_cparams(("parallel",)), name=name,
    )(p, gv, wm, bsb)


def a_mid_bwd(p, dy, gv, wm, bsb, *, tm, name):
    T = p.shape[0]

    def body(p_ref, dy_ref, gv_ref, wm_ref, bs_ref, dp_ref, dgv_ref, dwm_ref, dbs_ref, vn_sc, dvn_sc):
        @pl.when(pl.program_id(0) == 0)
        def _():
            dgv_ref[...] = jnp.zeros(dgv_ref.shape, f32)
            dwm_ref[...] = jnp.zeros(dwm_ref.shape, f32)
            dbs_ref[...] = jnp.zeros(dbs_ref.shape, f32)

        zv = p_ref[:, D:]
        v = _gelu(zv)
        vc = v - jnp.mean(v, axis=-1, keepdims=True)
        rstd = lax.rsqrt(jnp.mean(vc * vc, axis=-1, keepdims=True) + EPS)
        vhat = vc * rstd
        vn_sc[...] = (vhat * gv_ref[...]).astype(bf16)
        for g in range(A_GROUPS):
            cs = slice(g * GBLK, (g + 1) * GBLK)
            dwm = jnp.zeros((GBLK, GBLK), f32)
            dbs = jnp.zeros((GBLK, 1), f32)
            for r in range(tm // GBLK):
                rs = slice(r * GBLK, (r + 1) * GBLK)
                zu = p_ref[rs, cs]
                vn = vn_sc[rs, cs]
                sv = _dot(wm_ref[g], vn) + bs_ref[g]
                dyb = dy_ref[rs, cs]
                dsv = dyb * _gelu(zu)
                dsvb = dsv.astype(bf16)
                dp_ref[rs, cs] = (dyb * sv * _gelu_grad(zu)).astype(bf16)
                dwm += _dot_nt(dsvb, vn)
                dbs += jnp.sum(dsv, axis=1, keepdims=True)
                dvn_sc[rs, cs] = _dot_tn(wm_ref[g], dsvb)
            dwm_ref[g] += dwm
            dbs_ref[g] += dbs
        dvn = dvn_sc[...]
        dgv_ref[...] += jnp.sum(dvn * vhat, axis=0, keepdims=True)
        dvhat = dvn * gv_ref[...]
        dv = rstd * (dvhat - jnp.mean(dvhat, axis=-1, keepdims=True)
                     - vhat * jnp.mean(dvhat * vhat, axis=-1, keepdims=True))
        dp_ref[:, D:] = (dv * _gelu_grad(zv)).astype(bf16)

    return pl.pallas_call(
        body, grid=(T // tm,),
        in_specs=[pl.BlockSpec((tm, 2 * D), lambda i: (i, 0)), pl.BlockSpec((tm, D), lambda i: (i, 0)),
                  pl.BlockSpec((1, D), lambda i: (0, 0)),
                  pl.BlockSpec((A_GROUPS, GBLK, GBLK), lambda i: (0, 0, 0)),
                  pl.BlockSpec((A_GROUPS, GBLK, GBLK), lambda i: (0, 0, 0))],
        out_specs=[pl.BlockSpec((tm, 2 * D), lambda i: (i, 0)), pl.BlockSpec((1, D), lambda i: (0, 0)),
                   pl.BlockSpec((A_GROUPS, GBLK, GBLK), lambda i: (0, 0, 0)),
                   pl.BlockSpec((A_GROUPS, GBLK, 1), lambda i: (0, 0, 0))],
        out_shape=[jax.ShapeDtypeStruct((T, 2 * D), bf16), jax.ShapeDtypeStruct((1, D), f32),
                   jax.ShapeDtypeStruct((A_GROUPS, GBLK, GBLK), f32), jax.ShapeDtypeStruct((A_GROUPS, GBLK, 1), f32)],
        scratch_shapes=[pltpu.VMEM((tm, D), bf16), pltpu.VMEM((tm, D), f32)],
        compiler_params=_cparams(("arbitrary",)), name=name,
    )(p, dy, gv, wm, bsb)


def b_mid_fwd(p, cw, *, tm, S, name):
    T = p.shape[0]
    nst = S // tm

    def body(p_ref, ph_ref, cw_ref, y_ref, ext):
        first = (pl.program_id(0) % nst) == 0
        q = p_ref[:, D:2 * D] * p_ref[:, 2 * D:]
        qh = ph_ref[:, D:2 * D] * ph_ref[:, 2 * D:]
        ext[0:HALO, :] = jnp.where(first, 0.0, qh)
        ext[HALO:, :] = q
        y = cw_ref[0:1, :] * ext[pl.ds(HALO - 2, tm), :] + cw_ref[1:2, :] * ext[pl.ds(HALO - 1, tm), :] + cw_ref[2:3, :] * q
        y_ref[...] = (p_ref[:, :D] * y).astype(bf16)

    return pl.pallas_call(
        body, grid=(T // tm,),
        in_specs=[pl.BlockSpec((tm, 3 * D), lambda i: (i, 0)), _prev_halo_spec(tm, 3 * D),
                  pl.BlockSpec((3, D), lambda i: (0, 0))],
        out_specs=pl.BlockSpec((tm, D), lambda i: (i, 0)),
        out_shape=jax.ShapeDtypeStruct((T, D), bf16),
        scratch_shapes=[pltpu.VMEM((tm + HALO, D), f32)],
        compiler_params=_cparams(("parallel",)), name=name,
    )(p, p, cw)


def b_mid_bwd(p, dy, cw, *, tm, S, name):
    T = p.shape[0]
    nst = S // tm

    def body(p_ref, ph_ref, pn_ref, dy_ref, dyn_ref, cw_ref, dp_ref, dcw_ref, ext, ext2):
        i = pl.program_id(0)
        first = (i % nst) == 0
        last = (i % nst) == nst - 1

        @pl.when(i == 0)
        def _():
            dcw_ref[...] = jnp.zeros(dcw_ref.shape, f32)

        gb, gc, xt = p_ref[:, :D], p_ref[:, D:2 * D], p_ref[:, 2 * D:]
        q = gc * xt
        ext[0:HALO, :] = jnp.where(first, 0.0, ph_ref[:, D:2 * D] * ph_ref[:, 2 * D:])
        ext[HALO:, :] = q
        q2 = ext[pl.ds(HALO - 2, tm), :]
        q1 = ext[pl.ds(HALO - 1, tm), :]
        y = cw_ref[0:1, :] * q2 + cw_ref[1:2, :] * q1 + cw_ref[2:3, :] * q
        dyo = dy_ref[...]
        dp_ref[:, :D] = (dyo * y).astype(bf16)
        dyc = dyo * gb
        ext2[0:tm, :] = dyc
        ext2[tm:, :] = jnp.where(last, 0.0, dyn_ref[...] * pn_ref[:, :D])
        dq = cw_ref[2:3, :] * dyc + cw_ref[1:2, :] * ext2[pl.ds(1, tm), :] + cw_ref[0:1, :] * ext2[pl.ds(2, tm), :]
        dp_ref[:, D:2 * D] = (dq * xt).astype(bf16)
        dp_ref[:, 2 * D:] = (dq * gc).astype(bf16)
        dcw_ref[0:1, :] += jnp.sum(dyc * q2, axis=0, keepdims=True)
        dcw_ref[1:2, :] += jnp.sum(dyc * q1, axis=0, keepdims=True)
        dcw_ref[2:3, :] += jnp.sum(dyc * q, axis=0, keepdims=True)

    return pl.pallas_call(
        body, grid=(T // tm,),
        in_specs=[pl.BlockSpec((tm, 3 * D), lambda i: (i, 0)), _prev_halo_spec(tm, 3 * D), _next_halo_spec(tm, 3 * D, T),
                  pl.BlockSpec((tm, D), lambda i: (i, 0)), _next_halo_spec(tm, D, T),
                  pl.BlockSpec((3, D), lambda i: (0, 0))],
        out_specs=[pl.BlockSpec((tm, 3 * D), lambda i: (i, 0)), pl.BlockSpec((3, D), lambda i: (0, 0))],
        out_shape=[jax.ShapeDtypeStruct((T, 3 * D), bf16), jax.ShapeDtypeStruct((3, D), f32)],
        scratch_shapes=[pltpu.VMEM((tm + HALO, D), f32), pltpu.VMEM((tm + HALO, D), f32)],
        compiler_params=_cparams(("arbitrary",)), name=name,
    )(p, p, p, dy, dy, cw)


def _pool_counts(i, nst, tm, rows, row0, w):
    t = (i % nst) * tm + row0 + lax.broadcasted_iota(jnp.int32, (rows, 1), 0)
    return jnp.minimum(t + 1, w).astype(f32)


def _pool_diff(p_ref, ext, g, i, nst, tm):
    w = POOL_WINDOWS[g]
    cs = slice(g * C_GDIM, (g + 1) * C_GDIM)
    pg = p_ref[:, cs]
    s = pg
    for k in range(1, w):
        s = s + ext[pl.ds(HALO - k, tm), cs]
    return s / _pool_counts(i, nst, tm, tm, 0, w) - pg


def c_mid_fwd(p, wg, l, scale, *, tm, S, name):
    T = p.shape[0]
    nst = S // tm

    def body(p_ref, ph_ref, wg_ref, sc_ref, y_ref, ext):
        i = pl.program_id(0)
        first = (i % nst) == 0
        ext[0:HALO, :] = jnp.where(first, 0.0, ph_ref[...])
        ext[HALO:, :] = p_ref[...]
        for g in range(C_GROUPS):
            cs = slice(g * C_GDIM, (g + 1) * C_GDIM)
            dg = _pool_diff(p_ref, ext, g, i, nst, tm).astype(bf16)
            wv = wg_ref[:, g].reshape(C_GDIM, C_GDIM)
            y_ref[:, cs] = (_dot(dg, wv) * sc_ref[:, cs]).astype(bf16)

    return pl.pallas_call(
        body, grid=(T // tm,),
        in_specs=[pl.BlockSpec((tm, D), lambda i: (i, 0)), _prev_halo_spec(tm, D),
                  pl.BlockSpec((NDEV, None, C_GROUPS, C_GDIM // NDEV, C_GDIM), lambda i: (0, l, 0, 0, 0)),
                  pl.BlockSpec((1, D), lambda i: (0, 0))],
        out_specs=pl.BlockSpec((tm, D), lambda i: (i, 0)),
        out_shape=jax.ShapeDtypeStruct((T, D), bf16),
        scratch_shapes=[pltpu.VMEM((tm + HALO, D), f32)],
        compiler_params=_cparams(("parallel",)), name=name,
    )(p, p, wg, scale)


def c_mid_bwd(p, dy, wg, l, scale, *, tm, S, name):
    T = p.shape[0]
    nst = S // tm

    def body(p_ref, ph_ref, dy_ref, dyn_ref, wg_ref, sc_ref, dp_ref, dsc_ref, dwg_ref, ext, ext2):
        i = pl.program_id(0)
        first = (i % nst) == 0
        last = (i % nst) == nst - 1

        @pl.when(i == 0)
        def _():
            dsc_ref[...] = jnp.zeros(dsc_ref.shape, f32)
            dwg_ref[...] = jnp.zeros(dwg_ref.shape, f32)

        ext[0:HALO, :] = jnp.where(first, 0.0, ph_ref[...])
        ext[HALO:, :] = p_ref[...]
        for g in range(C_GROUPS):
            w = POOL_WINDOWS[g]
            cs = slice(g * C_GDIM, (g + 1) * C_GDIM)
            dg = _pool_diff(p_ref, ext, g, i, nst, tm).astype(bf16)
            wv = wg_ref[:, g].reshape(C_GDIM, C_GDIM)
            dyo = dy_ref[:, cs]
            dsc_ref[:, cs] += jnp.sum(dyo * _dot(dg, wv), axis=0, keepdims=True)
            dyp = (dyo * sc_ref[:, cs]).astype(bf16)
            dypn = (dyn_ref[:, cs] * sc_ref[:, cs]).astype(bf16)
            dwg_ref[g] += _dot_tn(dg, dyp)
            dd = _dot_nt(dyp, wv)
            ddn = _dot_nt(dypn, wv)
            ext2[0:tm, cs] = dd / _pool_counts(i, nst, tm, tm, 0, w)
            ext2[tm:, cs] = jnp.where(last, 0.0, ddn / _pool_counts(i, nst, tm, HALO, tm, w))
            s = -dd
            for k in range(w):
                s = s + ext2[pl.ds(k, tm), cs]
            dp_ref[:, cs] = s.astype(bf16)

    return pl.pallas_call(
        body, grid=(T // tm,),
        in_specs=[pl.BlockSpec((tm, D), lambda i: (i, 0)), _prev_halo_spec(tm, D),
                  pl.BlockSpec((tm, D), lambda i: (i, 0)), _next_halo_spec(tm, D, T),
                  pl.BlockSpec((NDEV, None, C_GROUPS, C_GDIM // NDEV, C_GDIM), lambda i: (0, l, 0, 0, 0)),
                  pl.BlockSpec((1, D), lambda i: (0, 0))],
        out_specs=[pl.BlockSpec((tm, D), lambda i: (i, 0)), pl.BlockSpec((1, D), lambda i: (0, 0)),
                   pl.BlockSpec((C_GROUPS, C_GDIM, C_GDIM), lambda i: (0, 0, 0))],
        out_shape=[jax.ShapeDtypeStruct((T, D), bf16), jax.ShapeDtypeStruct((1, D), f32),
                   jax.ShapeDtypeStruct((C_GROUPS, C_GDIM, C_GDIM), f32)],
        scratch_shapes=[pltpu.VMEM((tm + HALO, D), f32), pltpu.VMEM((tm + HALO, D), f32)],
        compiler_params=_cparams(("arbitrary",)), name=name,
    )(p, p, dy, dy, wg, scale)


def final_loss(x, g, tgt, *, tm, name):
    T = x.shape[0]

    def body(x_ref, g_ref, t_ref, loss_ref, dx_ref, dg_ref):
        @pl.when(pl.program_id(0) == 0)
        def _():
            loss_ref[...] = jnp.zeros(loss_ref.shape, f32)
            dg_ref[...] = jnp.zeros(dg_ref.shape, f32)

        xv = x_ref[...]
        r = lax.rsqrt(jnp.mean(xv * xv, axis=-1, keepdims=True) + EPS)
        xhat = xv * r
        err = xhat * g_ref[...] - t_ref[...]
        loss_ref[...] += 0.5 * jnp.sum(jnp.mean(err * err, axis=-1, keepdims=True))
        dy = err * (1.0 / D)
        dg_ref[...] += jnp.sum(dy * xhat, axis=0, keepdims=True)
        dxhat = dy * g_ref[...]
        dx_ref[...] = r * (dxhat - xhat * jnp.mean(dxhat * xhat, axis=-1, keepdims=True))

    return pl.pallas_call(
        body, grid=(T // tm,),
        in_specs=[pl.BlockSpec((tm, D), lambda i: (i, 0)), pl.BlockSpec((1, D), lambda i: (0, 0)),
                  pl.BlockSpec((tm, D), lambda i: (i, 0))],
        out_specs=[pl.BlockSpec((8, 128), lambda i: (0, 0)), pl.BlockSpec((tm, D), lambda i: (i, 0)),
                   pl.BlockSpec((1, D), lambda i: (0, 0))],
        out_shape=[jax.ShapeDtypeStruct((8, 128), f32), jax.ShapeDtypeStruct((T, D), f32),
                   jax.ShapeDtypeStruct((1, D), f32)],
        compiler_params=_cparams(("arbitrary",)), name=name,
    )(x, g, tgt)


def _slot(px, py, pc):
    return 4 * px + 2 * py + pc


def all_gather(arrs, *, name):
    n = len(arrs)

    def body(*refs):
        ins, outs = refs[:n], refs[n:2 * n]
        send_sems, recv_sems, local_sems = refs[2 * n:]
        x, y, c = lax.axis_index("x"), lax.axis_index("y"), lax.axis_index("c")
        me, sibling = (x, y, c), (x, y, 1 - c)
        chips = [(1 - x, y), (x, 1 - y), (1 - x, 1 - y)]

        def copy(a, k, block, to, src=None):
            dst = outs[a].at[_slot(*block)]
            return pltpu.make_async_remote_copy(
                src_ref=dst if src is None else src, dst_ref=dst,
                send_sem=send_sems.at[a, k], recv_sem=recv_sems.at[a, k], device_id=to, device_id_type=MESH)

        mine = [pltpu.make_async_copy(ins[a], outs[a].at[_slot(*me)], local_sems.at[a]) for a in range(n)]
        for cp in mine:
            cp.start()
        first = []
        for a in range(n):
            first.append(copy(a, 0, me, sibling, src=ins[a]))
            first += [copy(a, 1 + j, me, (*chip, c), src=ins[a]) for j, chip in enumerate(chips)]
        for cp in first:
            cp.start()
        passed = []
        for j, chip in enumerate(chips):
            for a in range(n):
                copy(a, 1 + j, (*chip, c), me).wait_recv()
                fwd = copy(a, 4 + j, (*chip, c), sibling)
                fwd.start()
                passed.append(fwd)
        for a in range(n):
            copy(a, 0, sibling, me).wait_recv()
        for j, chip in enumerate(chips):
            for a in range(n):
                copy(a, 4 + j, (*chip, 1 - c), me).wait_recv()
        for cp in first + passed:
            cp.wait_send()
        for cp in mine:
            cp.wait()

    any_spec = pl.BlockSpec(memory_space=pl.ANY)
    return pl.pallas_call(
        body,
        in_specs=[any_spec] * n, out_specs=[any_spec] * n,
        out_shape=[jax.ShapeDtypeStruct((NDEV,) + a.shape, a.dtype) for a in arrs],
        scratch_shapes=[pltpu.SemaphoreType.DMA((n, 7)), pltpu.SemaphoreType.DMA((n, 7)), pltpu.SemaphoreType.DMA((n,))],
        compiler_params=pltpu.CompilerParams(has_side_effects=True), name=name,
    )(*arrs)


def exchange_slices(groups, *, name):
    flat = [(gi, li, a) for gi, grp in enumerate(groups) for li, a in enumerate(grp)]
    n = len(flat)

    def body(*refs):
        ins, outs = refs[:n], refs[n:n + len(groups)]
        send_sems, recv_sems, local_sems = refs[n + len(groups):]
        x, y, c = lax.axis_index("x"), lax.axis_index("y"), lax.axis_index("c")
        me = _slot(x, y, c)

        def peer(r):
            return (x ^ ((r >> 2) & 1), y ^ ((r >> 1) & 1), c ^ (r & 1))

        def copy(a, r):
            gi, li, _ = flat[a]
            to = peer(r)
            return pltpu.make_async_remote_copy(
                src_ref=ins[a].at[_slot(*to)], dst_ref=outs[gi].at[me, li],
                send_sem=send_sems.at[a, r - 1], recv_sem=recv_sems.at[a, r - 1], device_id=to, device_id_type=MESH)

        def arrival(a, r):
            gi, li, _ = flat[a]
            frm = _slot(*peer(r))
            return pltpu.make_async_remote_copy(
                src_ref=ins[a].at[frm], dst_ref=outs[gi].at[frm, li],
                send_sem=send_sems.at[a, r - 1], recv_sem=recv_sems.at[a, r - 1], device_id=peer(r), device_id_type=MESH)

        mine = [pltpu.make_async_copy(ins[a].at[me], outs[flat[a][0]].at[me, flat[a][1]], local_sems.at[a]) for a in range(n)]
        for cp in mine:
            cp.start()
        sent = [copy(a, r) for r in (1, 2, 4, 3, 5, 6, 7) for a in range(n)]
        for cp in sent:
            cp.start()
        for r in range(1, NDEV):
            for a in range(n):
                arrival(a, r).wait_recv()
        for cp in sent:
            cp.wait_send()
        for cp in mine:
            cp.wait()

    any_spec = pl.BlockSpec(memory_space=pl.ANY)
    return pl.pallas_call(
        body,
        in_specs=[any_spec] * n, out_specs=[any_spec] * len(groups),
        out_shape=[jax.ShapeDtypeStruct((NDEV, len(grp)) + grp[0].shape[1:], grp[0].dtype) for grp in groups],
        scratch_shapes=[pltpu.SemaphoreType.DMA((n, 7)), pltpu.SemaphoreType.DMA((n, 7)), pltpu.SemaphoreType.DMA((n,))],
        compiler_params=pltpu.CompilerParams(has_side_effects=True), name=name,
    )(*[a for _, _, a in flat])


def sum8(g, *, name):
    _, R, C = g.shape
    tr = R

    def body(g_ref, o_ref):
        s = g_ref[0]
        for k in range(1, NDEV):
            s = s + g_ref[k]
        o_ref[...] = s

    return pl.pallas_call(
        body, grid=(R // tr,),
        in_specs=[pl.BlockSpec((NDEV, tr, C), lambda i: (0, i, 0))],
        out_specs=pl.BlockSpec((tr, C), lambda i: (i, 0)),
        out_shape=jax.ShapeDtypeStruct((R, C), f32),
        compiler_params=_cparams(("parallel",)), name=name,
    )(g)


def _adam_math(w, g, m, v):
    m = ADAM_B1 * m + (1.0 - ADAM_B1) * g
    v = ADAM_B2 * v + (1.0 - ADAM_B2) * (g * g)
    m_hat = m / (1.0 - ADAM_B1 ** ADAM_STEP)
    v_hat = v / (1.0 - ADAM_B2 ** ADAM_STEP)
    delta = -ADAM_LR * (m_hat / (jnp.sqrt(v_hat) + ADAM_EPS) + ADAM_WD * w)
    return delta, m, v


def adamw_slices(w, m, v, recv, *, col0=0, tr=None, tc=None, name):
    L, R, C = w.shape
    Rp, Cp = recv.shape[2], recv.shape[3]
    tr = R if tr is None else tr
    tc = C if tc is None else tc
    assert R % tr == 0 and C % tc == 0
    assert (tr == R or Rp == R) and (tc == C or (Cp == C and col0 == 0))
    rr = Rp if tr == R else tr
    rc = Cp if tc == C else tc

    def body(w_ref, m_ref, v_ref, r_ref, g_ref, d_ref, nm_ref, nv_ref):
        c0 = col0 if tc == C else 0
        g = r_ref[0, 0:tr, c0:c0 + tc].astype(f32)
        for k in range(1, NDEV):
            g = g + r_ref[k, 0:tr, c0:c0 + tc].astype(f32)
        delta, nm, nv = _adam_math(w_ref[...], g, m_ref[...], v_ref[...])
        g_ref[...] = g
        d_ref[...] = delta
        nm_ref[...] = nm
        nv_ref[...] = nv

    wspec = pl.BlockSpec((None, tr, tc), lambda l, i, j: (l, i, j))
    return pl.pallas_call(
        body, grid=(L, R // tr, C // tc),
        in_specs=[wspec, wspec, wspec, pl.BlockSpec((NDEV, None, rr, rc), lambda l, i, j: (0, l, i, j))],
        out_specs=[wspec] * 4, out_shape=[jax.ShapeDtypeStruct((L, R, C), f32)] * 4,
        compiler_params=_cparams(("parallel", "parallel", "parallel")), name=name,
    )(w, m, v, recv)


def adamw_plain(w, m, v, g, *, name):
    R, C = w.shape

    def body(w_ref, m_ref, v_ref, g_ref, d_ref, nm_ref, nv_ref):
        delta, nm, nv = _adam_math(w_ref[...], g_ref[...], m_ref[...], v_ref[...])
        d_ref[...] = delta
        nm_ref[...] = nm
        nv_ref[...] = nv

    return pl.pallas_call(
        body, out_shape=[jax.ShapeDtypeStruct((R, C), f32)] * 3, name=name,
    )(w, m, v, g)


def _chunk_causal_mask():
    pos = jnp.arange(GBLK)
    return (pos[None, :] // CHUNK) <= (pos[:, None] // CHUNK)


def kernel(x, norm_mix_g, norm_ffn_g, final_norm_g, a_w_in, a_v_norm_g, a_w_s, a_b_s, a_w_out, b_w_in, b_conv_w, b_w_out, c_w_in, c_w_grp, c_scale, c_w_out, f_w_gate, f_w_up, f_w_down, loss_target, m_norm_mix_g, m_norm_ffn_g, m_final_norm_g, m_a_w_in, m_a_v_norm_g, m_a_w_s, m_a_b_s, m_a_w_out, m_b_w_in, m_b_conv_w, m_b_w_out, m_c_w_in, m_c_w_grp, m_c_scale, m_c_w_out, m_f_w_gate, m_f_w_up, m_f_w_down, v_norm_mix_g, v_norm_ffn_g, v_final_norm_g, v_a_w_in, v_a_v_norm_g, v_a_w_s, v_a_b_s, v_a_w_out, v_b_w_in, v_b_conv_w, v_b_w_out, v_c_w_in, v_c_w_grp, v_c_scale, v_c_w_out, v_f_w_gate, v_f_w_up, v_f_w_down):
    B, S, _ = x.shape
    T = B * S
    tm = min(512, S)
    tmm = min(1024, T)
    tms = min(512, T)
    me = _slot(lax.axis_index("x"), lax.axis_index("y"), lax.axis_index("c"))

    pad_c = ((0, 0), (0, 0), (0, F_PAD - F_LOC))
    f_gu = jnp.concatenate([jnp.pad(f_w_gate.astype(bf16), pad_c), jnp.pad(f_w_up.astype(bf16), pad_c)], axis=2)
    f_d = jnp.pad(f_w_down.astype(bf16), ((0, 0), (0, F_PAD - F_LOC), (0, 0)))
    rows128 = jnp.concatenate([a_w_out, b_w_out, c_w_in, c_w_out], axis=0).astype(bf16)
    small = jnp.concatenate([a_v_norm_g, b_conv_w[0], c_scale, jnp.zeros((2, GBLK), f32)], axis=0)
    g_ain, g_rows, g_bin, g_cgrp, g_fgu, g_fd, g_small = all_gather(
        [a_w_in.astype(bf16), rows128, b_w_in.astype(bf16), c_w_grp.astype(bf16), f_gu, f_d, small], name="gather_weights")
    R_AOUT, R_BOUT, R_CIN, R_COUT = (0, 1), 2, 3, 4
    small_full = jnp.transpose(g_small, (1, 0, 2)).reshape(8, D)
    gv_full = small_full[0:2]
    cw_full = small_full[2:5]
    cs_full = small_full[5:6]
    wm_masked = jnp.where(_chunk_causal_mask()[None, None], a_w_s, 0.0).astype(bf16)
    bsb = jnp.broadcast_to(a_b_s[:, :, :, None], a_b_s.shape + (GBLK,))

    xc = x.reshape(T, D)
    saved = []
    for i in range(DEPTH):
        kind, j = i % 3, i // 3
        gmix = norm_mix_g[i][None]
        if kind == 0:
            p, h = norm_mm(xc, gmix, g_ain, j, tm=tmm, name=f"a_in_{i}")
            y = a_mid_fwd(p, gv_full[j][None], wm_masked[j], bsb[j], tm=tm, name=f"a_mid_{i}")
            x1 = mm_res(y, g_rows, R_AOUT[j], xc, tm=tms, name=f"a_out_{i}")
        elif kind == 1:
            p, h = norm_mm(xc, gmix, g_bin, j, tm=tmm, name=f"b_in_{i}")
            y = b_mid_fwd(p, cw_full, tm=tm, S=S, name=f"b_mid_{i}")
            x1 = mm_res(y, g_rows, R_BOUT, xc, tm=tms, name=f"b_out_{i}")
        else:
            p, h = norm_mm(xc, gmix, g_rows, R_CIN, tm=tmm, row_sharded=True, name=f"c_in_{i}")
            y = c_mid_fwd(p, g_cgrp, j, cs_full, tm=tm, S=S, name=f"c_mid_{i}")
            x1 = mm_res(y, g_rows, R_COUT, xc, tm=tms, name=f"c_out_{i}")
        pf, h2, a = norm_mm(x1, norm_ffn_g[i][None], g_fgu, i, tm=tmm, swiglu=True, name=f"f_in_{i}")
        x2 = mm_res(a, g_fd, i, x1, tm=tms, name=f"f_out_{i}")
        saved.append((xc, p, h, y, x1, pf, h2, a))
        xc = x2

    loss_blk, dx, d_final = final_loss(xc, final_norm_g[None], loss_target.reshape(T, D), tm=tms, name="final_loss")
    loss = lax.psum(loss_blk[0, 0], AXES)

    d_mix, d_ffn = [None] * DEPTH, [None] * DEPTH
    gw_fgu, gw_fd = [None] * DEPTH, [None] * DEPTH
    gw_ain, gw_aout = [None] * 2, [None] * 2
    d_gv, d_wm, d_bs = [None] * 2, [None] * 2, [None] * 2
    for i in reversed(range(DEPTH)):
        kind, j = i % 3, i // 3
        xin, p, h, y, x1, pf, h2, a = saved[i]
        dpf, dxb = ffn_bwd_mid(dx, g_fd, i, pf, tm=tmm, name=f"f_bwd_mid_{i}")
        dx1, d_ffn[i] = mm_nt_normbwd(dpf, g_fgu, i, x1, norm_ffn_g[i][None], dx, tm=tms, name=f"f_bwd_in_{i}")
        gw_fgu[i] = mm_tn(h2, dpf, tm=tmm, row_sharded=False, name=f"f_dw_in_{i}")
        gw_fd[i] = mm_tn(a, dxb, tm=tmm, row_sharded=True, name=f"f_dw_out_{i}")
        gmix = norm_mix_g[i][None]
        if kind == 0:
            dy, dx1b = mm_nt(dx1, g_rows, R_AOUT[j], tm=tmm, name=f"a_bwd_out_{i}")
            dp, d_gv[j], d_wm[j], d_bs[j] = a_mid_bwd(p, dy, gv_full[j][None], wm_masked[j], bsb[j], tm=tm, name=f"a_bwd_mid_{i}")
            dx, d_mix[i] = mm_nt_normbwd(dp, g_ain, j, xin, gmix, dx1, tm=tms,name=f"a_bwd_in_{i}")
            gw_ain[j] = mm_tn(h, dp, tm=tmm, row_sharded=False, name=f"a_dw_in_{i}")
            gw_aout[j] = mm_tn(y, dx1b, tm=tmm, row_sharded=True, name=f"a_dw_out_{i}")
        elif kind == 1:
            dy, dx1b = mm_nt(dx1, g_rows, R_BOUT, tm=tmm, name=f"b_bwd_out_{i}")
            dp, d_cw = b_mid_bwd(p, dy, cw_full, tm=tm, S=S, name=f"b_bwd_mid_{i}")
            dx, d_mix[i] = mm_nt_normbwd(dp, g_bin, j, xin, gmix, dx1, tm=tms,name=f"b_bwd_in_{i}")
            gw_bin = mm_tn(h, dp, tm=tmm, row_sharded=False, name=f"b_dw_in_{i}")
            gw_bout = mm_tn(y, dx1b, tm=tmm, row_sharded=True, name=f"b_dw_out_{i}")
        else:
            dy, dx1b = mm_nt(dx1, g_rows, R_COUT, tm=tmm, name=f"c_bwd_out_{i}")
            dp, d_cs, d_wgrp = c_mid_bwd(p, dy, g_cgrp, j, cs_full, tm=tm, S=S, name=f"c_bwd_mid_{i}")
            dx, d_mix[i] = mm_nt_normbwd(dp, g_rows, R_CIN, xin, gmix, dx1, tm=tms,row_sharded=True, name=f"c_bwd_in_{i}")
            gw_cin = mm_tn(h, dp, tm=tmm, row_sharded=True, name=f"c_dw_in_{i}")
            gw_cout = mm_tn(y, dx1b, tm=tmm, row_sharded=True, name=f"c_dw_out_{i}")
    grad_x = dx.reshape(B, S, D)

    gw_cgrp = jnp.transpose(d_wgrp.reshape(C_GROUPS, NDEV, C_GDIM // NDEV, C_GDIM), (1, 0, 2, 3)).astype(bf16)
    gw_cgrp = gw_cgrp.reshape(NDEV, C_GROUPS * (C_GDIM // NDEV), C_GDIM)
    r_ain, r_aout, r_bin, r_bout, r_cin, r_cgrp, r_cout, r_fgu, r_fd = exchange_slices(
        [gw_ain, gw_aout, [gw_bin], [gw_bout], [gw_cin], [gw_cgrp], [gw_cout], gw_fgu, gw_fd], name="exchange_grads")

    mask = _chunk_causal_mask()
    small_parts = [
        jnp.concatenate(d_mix, axis=0).reshape(-1, GBLK),
        jnp.concatenate(d_ffn, axis=0).reshape(-1, GBLK),
        d_final.reshape(-1, GBLK),
        jnp.concatenate(d_gv, axis=0).reshape(-1, GBLK),
        jnp.where(mask[None, None], jnp.stack(d_wm), 0.0).reshape(-1, GBLK),
        jnp.stack(d_bs).reshape(-1, GBLK),
        d_cw.reshape(-1, GBLK),
        d_cs.reshape(-1, GBLK),
    ]
    small_rows = [q.shape[0] for q in small_parts]
    (gs_all,) = all_gather([jnp.concatenate(small_parts, axis=0)], name="gather_small_grads")
    gs = sum8(gs_all, name="sum_small_grads")
    offs = [0]
    for r in small_rows:
        offs.append(offs[-1] + r)
    sp = [gs[offs[q]:offs[q + 1]] for q in range(len(small_parts))]
    grad_norm_mix_g = sp[0].reshape(DEPTH, D)
    grad_norm_ffn_g = sp[1].reshape(DEPTH, D)
    grad_final_norm_g = sp[2].reshape(D)
    grad_a_w_s = sp[4].reshape(a_w_s.shape)
    grad_a_b_s = sp[5].reshape(a_b_s.shape)

    def my_cols(full):
        return lax.dynamic_slice_in_dim(full, me * GBLK, GBLK, axis=1)

    grad_a_v_norm_g = my_cols(sp[3].reshape(2, D))
    grad_b_conv_w = my_cols(sp[6].reshape(3, D))[None]
    grad_c_scale = my_cols(sp[7].reshape(1, D))

    def plain(w, m, v, g, name):
        shp = w.shape
        w2, m2, v2, g2 = (t.reshape(-1, shp[-1]) for t in (w, m, v, g))
        return tuple(t.reshape(shp) for t in adamw_plain(w2, m2, v2, g2, name=name))

    def sliced(w, m, v, recv, name, **kw):
        shp = w.shape
        w3, m3, v3 = (t.reshape((shp[0], -1, shp[-1])) for t in (w, m, v))
        return tuple(t.reshape(shp) for t in adamw_slices(w3, m3, v3, recv, name=name, **kw))

    res = {}
    res["norm_mix_g"] = (grad_norm_mix_g,) + plain(norm_mix_g, m_norm_mix_g, v_norm_mix_g, grad_norm_mix_g, "adam_norm_mix")
    res["norm_ffn_g"] = (grad_norm_ffn_g,) + plain(norm_ffn_g, m_norm_ffn_g, v_norm_ffn_g, grad_norm_ffn_g, "adam_norm_ffn")
    res["final_norm_g"] = (grad_final_norm_g,) + tuple(
        t.reshape(D) for t in plain(final_norm_g[None], m_final_norm_g[None], v_final_norm_g[None], grad_final_norm_g[None], "adam_final"))
    res["a_w_in"] = sliced(a_w_in, m_a_w_in, v_a_w_in, r_ain, "adam_a_w_in", tr=256)
    res["a_v_norm_g"] = (grad_a_v_norm_g,) + plain(a_v_norm_g, m_a_v_norm_g, v_a_v_norm_g, grad_a_v_norm_g, "adam_a_v_norm")
    res["a_w_s"] = (grad_a_w_s,) + plain(a_w_s, m_a_w_s, v_a_w_s, grad_a_w_s, "adam_a_w_s")
    res["a_b_s"] = (grad_a_b_s,) + plain(a_b_s, m_a_b_s, v_a_b_s, grad_a_b_s, "adam_a_b_s")
    res["a_w_out"] = sliced(a_w_out, m_a_w_out, v_a_w_out, r_aout, "adam_a_w_out")
    res["b_w_in"] = sliced(b_w_in, m_b_w_in, v_b_w_in, r_bin, "adam_b_w_in", tr=256)
    res["b_conv_w"] = (grad_b_conv_w,) + plain(b_conv_w, m_b_conv_w, v_b_conv_w, grad_b_conv_w, "adam_b_conv")
    res["b_w_out"] = sliced(b_w_out, m_b_w_out, v_b_w_out, r_bout, "adam_b_w_out")
    res["c_w_in"] = sliced(c_w_in, m_c_w_in, v_c_w_in, r_cin, "adam_c_w_in")
    res["c_w_grp"] = sliced(c_w_grp, m_c_w_grp, v_c_w_grp, r_cgrp, "adam_c_w_grp")
    res["c_scale"] = (grad_c_scale,) + plain(c_scale, m_c_scale, v_c_scale, grad_c_scale, "adam_c_scale")
    res["c_w_out"] = sliced(c_w_out, m_c_w_out, v_c_w_out, r_cout, "adam_c_w_out")
    res["f_w_gate"] = sliced(f_w_gate, m_f_w_gate, v_f_w_gate, r_fgu, "adam_f_gate", col0=0, tr=256)
    res["f_w_up"] = sliced(f_w_up, m_f_w_up, v_f_w_up, r_fgu, "adam_f_up", col0=F_PAD, tr=256)
    res["f_w_down"] = sliced(f_w_down, m_f_w_down, v_f_w_down, r_fd, "adam_f_down", tc=256)

    order = ["norm_mix_g", "norm_ffn_g", "final_norm_g", "a_w_in", "a_v_norm_g", "a_w_s", "a_b_s", "a_w_out", "b_w_in",
             "b_conv_w", "b_w_out", "c_w_in", "c_w_grp", "c_scale", "c_w_out", "f_w_gate", "f_w_up", "f_w_down"]
    return (loss, grad_x, *[res[n][0] for n in order], *[res[n][1] for n in order],
            *[res[n][2] for n in order], *[res[n][3] for n in order])
```

```python
import functools

import jax
import jax.numpy as jnp
from jax import lax
from jax.experimental import pallas as pl
from jax.experimental.pallas import tpu as pltpu

f32 = jnp.float32
bf16 = jnp.bfloat16

NDEV = 8
D = 1024
EPS = 1e-6
CHUNK = 64
GBLK = 128
A_GROUPS = 8
C_GROUPS = 4
C_GDIM = 256
POOL_WINDOWS = (2, 4, 8, 16)
HALO = 16
F_LOC = 352
F_PAD = 384
DEPTH = 4
AXES = ("x", "y", "c")
MESH = pl.DeviceIdType.MESH

ADAM_LR = 0.001
ADAM_B1 = 0.9
ADAM_B2 = 0.999
ADAM_EPS = 1e-08
ADAM_WD = 0.01
ADAM_STEP = 10

VMEM_LIMIT = 56 * 1024 * 1024


def _cparams(sem):
    return pltpu.CompilerParams(dimension_semantics=sem, vmem_limit_bytes=VMEM_LIMIT)


def _gelu(z):
    return 0.5 * z * (1.0 + lax.erf(z * 0.7071067811865476))


def _gelu_grad(z):
    return 0.5 * (1.0 + lax.erf(z * 0.7071067811865476)) + z * jnp.exp(-0.5 * z * z) * 0.3989422804014327


def _dot(a, b):
    return jnp.dot(a, b, preferred_element_type=f32)


def _dot_nt(a, b):
    return lax.dot_general(a, b, (((1,), (1,)), ((), ())), preferred_element_type=f32)


def _dot_tn(a, b):
    return lax.dot_general(a, b, (((0,), (0,)), ((), ())), preferred_element_type=f32)


def norm_mm(x, g, w, *, tm, row_sharded=False, swiglu=False, name):
    T = x.shape[0]
    if row_sharded:
        nchunk, n = 1, w.shape[2]
        w_spec = pl.BlockSpec((NDEV, w.shape[1], n), lambda i, d: (0, 0, 0))
    else:
        nchunk, n = NDEV, w.shape[2]
        w_spec = pl.BlockSpec((None, D, n), lambda i, d: (d, 0, 0))

    def body(x_ref, g_ref, w_ref, p_ref, h_ref, *rest):
        @pl.when(pl.program_id(1) == 0)
        def _():
            xv = x_ref[...]
            r = lax.rsqrt(jnp.mean(xv * xv, axis=-1, keepdims=True) + EPS)
            h_ref[...] = (xv * r * g_ref[...]).astype(bf16)

        wv = w_ref[...].reshape(D, n) if row_sharded else w_ref[...]
        p = _dot(h_ref[...], wv)
        p_ref[...] = p
        if swiglu:
            gate, up = p[:, : n // 2], p[:, n // 2:]
            rest[0][...] = (gate * jax.nn.sigmoid(gate) * up).astype(bf16)

    out_shape = [jax.ShapeDtypeStruct((T, nchunk * n), f32), jax.ShapeDtypeStruct((T, D), bf16)]
    out_specs = [pl.BlockSpec((tm, n), lambda i, d: (i, d)), pl.BlockSpec((tm, D), lambda i, d: (i, 0))]
    if swiglu:
        out_shape.append(jax.ShapeDtypeStruct((T, nchunk * n // 2), bf16))
        out_specs.append(pl.BlockSpec((tm, n // 2), lambda i, d: (i, d)))
    return pl.pallas_call(
        body, grid=(T // tm, nchunk),
        in_specs=[pl.BlockSpec((tm, D), lambda i, d: (i, 0)), pl.BlockSpec((1, D), lambda i, d: (0, 0)), w_spec],
        out_specs=out_specs, out_shape=out_shape,
        compiler_params=_cparams(("parallel", "arbitrary")), name=name,
    )(x, g, w)


def mm_res(y, w, x, *, tm, name):
    T, K = y.shape
    k = w.shape[1]

    def body(y_ref, w_ref, x_ref, o_ref):
        o_ref[...] = x_ref[...] + _dot(y_ref[...], w_ref[...].reshape(K, D))

    return pl.pallas_call(
        body, grid=(T // tm,),
        in_specs=[pl.BlockSpec((tm, K), lambda i: (i, 0)),
                  pl.BlockSpec((NDEV, k, D), lambda i: (0, 0, 0)),
                  pl.BlockSpec((tm, D), lambda i: (i, 0))],
        out_specs=pl.BlockSpec((tm, D), lambda i: (i, 0)),
        out_shape=jax.ShapeDtypeStruct((T, D), f32),
        compiler_params=_cparams(("parallel",)), name=name,
    )(y, w, x)


def mm_nt(dy, w, *, tm, name):
    T = dy.shape[0]
    k = w.shape[1]
    K = NDEV * k

    def body(dy_ref, w_ref, o_ref, dyb_ref):
        dyb = dy_ref[...].astype(bf16)
        dyb_ref[...] = dyb
        o_ref[...] = _dot_nt(dyb, w_ref[...].reshape(K, D))

    return pl.pallas_call(
        body, grid=(T // tm,),
        in_specs=[pl.BlockSpec((tm, D), lambda i: (i, 0)),
                  pl.BlockSpec((NDEV, k, D), lambda i: (0, 0, 0))],
        out_specs=[pl.BlockSpec((tm, K), lambda i: (i, 0)), pl.BlockSpec((tm, D), lambda i: (i, 0))],
        out_shape=[jax.ShapeDtypeStruct((T, K), f32), jax.ShapeDtypeStruct((T, D), bf16)],
        compiler_params=_cparams(("parallel",)), name=name,
    )(dy, w)


def ffn_bwd_mid(dx, wd, pf, *, tm, name):
    T = dx.shape[0]

    def body(dx_ref, w_ref, p_ref, dp_ref, dxb_ref):
        @pl.when(pl.program_id(1) == 0)
        def _():
            dxb_ref[...] = dx_ref[...].astype(bf16)

        da = _dot_nt(dxb_ref[...], w_ref[...])
        gate, up = p_ref[:, :F_PAD], p_ref[:, F_PAD:]
        sg = jax.nn.sigmoid(gate)
        dp_ref[:, :F_PAD] = (da * up * (sg * (1.0 + gate * (1.0 - sg)))).astype(bf16)
        dp_ref[:, F_PAD:] = (da * gate * sg).astype(bf16)

    return pl.pallas_call(
        body, grid=(T // tm, NDEV),
        in_specs=[pl.BlockSpec((tm, D), lambda i, d: (i, 0)),
                  pl.BlockSpec((None, F_PAD, D), lambda i, d: (d, 0, 0)),
                  pl.BlockSpec((tm, 2 * F_PAD), lambda i, d: (i, d))],
        out_specs=[pl.BlockSpec((tm, 2 * F_PAD), lambda i, d: (i, d)), pl.BlockSpec((tm, D), lambda i, d: (i, 0))],
        out_shape=[jax.ShapeDtypeStruct(pf.shape, bf16), jax.ShapeDtypeStruct((T, D), bf16)],
        compiler_params=_cparams(("parallel", "arbitrary")), name=name,
    )(dx, wd, pf)


def mm_nt_normbwd(dp, w, x, g, dres, *, tm, row_sharded=False, name):
    T = x.shape[0]
    if row_sharded:
        nchunk, n = 1, w.shape[2]
        w_spec = pl.BlockSpec((NDEV, w.shape[1], n), lambda i, d: (0, 0, 0))
    else:
        nchunk, n = NDEV, w.shape[2]
        w_spec = pl.BlockSpec((None, D, n), lambda i, d: (d, 0, 0))

    def body(dp_ref, w_ref, x_ref, g_ref, dres_ref, dx_ref, dg_ref, acc):
        i, d = pl.program_id(0), pl.program_id(1)
        wv = w_ref[...].reshape(D, n) if row_sharded else w_ref[...]
        part = _dot_nt(dp_ref[...], wv)

        @pl.when(d == 0)
        def _():
            acc[...] = part

        @pl.when(d > 0)
        def _():
            acc[...] += part

        @pl.when(jnp.logical_and(i == 0, d == 0))
        def _():
            dg_ref[...] = jnp.zeros(dg_ref.shape, f32)

        @pl.when(d == nchunk - 1)
        def _():
            xv = x_ref[...]
            r = lax.rsqrt(jnp.mean(xv * xv, axis=-1, keepdims=True) + EPS)
            xhat = xv * r
            dh = acc[...]
            dg_ref[...] += jnp.sum(dh * xhat, axis=0, keepdims=True)
            dxhat = dh * g_ref[...]
            dx_ref[...] = dres_ref[...] + r * (dxhat - xhat * jnp.mean(dxhat * xhat, axis=-1, keepdims=True))

    return pl.pallas_call(
        body, grid=(T // tm, nchunk),
        in_specs=[pl.BlockSpec((tm, n), lambda i, d: (i, d)), w_spec,
                  pl.BlockSpec((tm, D), lambda i, d: (i, 0)), pl.BlockSpec((1, D), lambda i, d: (0, 0)),
                  pl.BlockSpec((tm, D), lambda i, d: (i, 0))],
        out_specs=[pl.BlockSpec((tm, D), lambda i, d: (i, 0)), pl.BlockSpec((1, D), lambda i, d: (0, 0))],
        out_shape=[jax.ShapeDtypeStruct((T, D), f32), jax.ShapeDtypeStruct((1, D), f32)],
        scratch_shapes=[pltpu.VMEM((tm, D), f32)],
        compiler_params=_cparams(("arbitrary", "arbitrary")), name=name,
    )(dp, w, x, g, dres)


def mm_tn(a, b, *, tm, row_sharded, name):
    T = a.shape[0]
    if row_sharded:
        k, n = a.shape[1] // NDEV, b.shape[1]
        a_spec = pl.BlockSpec((tm, k), lambda d, t: (t, d))
        b_spec = pl.BlockSpec((tm, n), lambda d, t: (t, 0))
    else:
        k, n = a.shape[1], b.shape[1] // NDEV
        a_spec = pl.BlockSpec((tm, k), lambda d, t: (t, 0))
        b_spec = pl.BlockSpec((tm, n), lambda d, t: (t, d))
    nt = T // tm

    def body(a_ref, b_ref, o_ref, acc):
        t = pl.program_id(1)
        part = _dot_tn(a_ref[...], b_ref[...])

        @pl.when(t == 0)
        def _():
            acc[...] = part

        @pl.when(t > 0)
        def _():
            acc[...] += part

        @pl.when(t == nt - 1)
        def _():
            o_ref[...] = acc[...].astype(bf16)

    return pl.pallas_call(
        body, grid=(NDEV, nt), in_specs=[a_spec, b_spec],
        out_specs=pl.BlockSpec((None, k, n), lambda d, t: (d, 0, 0)),
        out_shape=jax.ShapeDtypeStruct((NDEV, k, n), bf16),
        scratch_shapes=[pltpu.VMEM((k, n), f32)],
        compiler_params=_cparams(("parallel", "arbitrary")), name=name,
    )(a, b)


def _prev_halo_spec(tm, ncol):
    return pl.BlockSpec((HALO, ncol), lambda i: (jnp.maximum(i * (tm // HALO) - 1, 0), 0))


def _next_halo_spec(tm, ncol, T):
    return pl.BlockSpec((HALO, ncol), lambda i: (jnp.minimum((i + 1) * (tm // HALO), T // HALO - 1), 0))


def a_mid_fwd(p, gv, wm, bsb, *, tm, name):
    T = p.shape[0]

    def body(p_ref, gv_ref, wm_ref, bs_ref, y_ref, vn_sc):
        v = _gelu(p_ref[:, D:])
        vc = v - jnp.mean(v, axis=-1, keepdims=True)
        var = jnp.mean(vc * vc, axis=-1, keepdims=True)
        vn_sc[...] = (vc * lax.rsqrt(var + EPS) * gv_ref[...]).astype(bf16)
        for g in range(A_GROUPS):
            cs = slice(g * GBLK, (g + 1) * GBLK)
            for r in range(tm // GBLK):
                rs = slice(r * GBLK, (r + 1) * GBLK)
                sv = _dot(wm_ref[g], vn_sc[rs, cs]) + bs_ref[g]
                y_ref[rs, cs] = (_gelu(p_ref[rs, cs]) * sv).astype(bf16)

    return pl.pallas_call(
        body, grid=(T // tm,),
        in_specs=[pl.BlockSpec((tm, 2 * D), lambda i: (i, 0)), pl.BlockSpec((1, D), lambda i: (0, 0)),
                  pl.BlockSpec((A_GROUPS, GBLK, GBLK), lambda i: (0, 0, 0)),
                  pl.BlockSpec((A_GROUPS, GBLK, GBLK), lambda i: (0, 0, 0))],
        out_specs=pl.BlockSpec((tm, D), lambda i: (i, 0)),
        out_shape=jax.ShapeDtypeStruct((T, D), bf16),
        scratch_shapes=[pltpu.VMEM((tm, D), bf16)],
        compiler_params=_cparams(("parallel",)), name=name,
    )(p, gv, wm, bsb)


def a_mid_bwd(p, dy, gv, wm, bsb, *, tm, name):
    T = p.shape[0]

    def body(p_ref, dy_ref, gv_ref, wm_ref, bs_ref, dp_ref, dgv_ref, dwm_ref, dbs_ref, vn_sc, dvn_sc):
        @pl.when(pl.program_id(0) == 0)
        def _():
            dgv_ref[...] = jnp.zeros(dgv_ref.shape, f32)
            dwm_ref[...] = jnp.zeros(dwm_ref.shape, f32)
            dbs_ref[...] = jnp.zeros(dbs_ref.shape, f32)

        zv = p_ref[:, D:]
        v = _gelu(zv)
        vc = v - jnp.mean(v, axis=-1, keepdims=True)
        rstd = lax.rsqrt(jnp.mean(vc * vc, axis=-1, keepdims=True) + EPS)
        vhat = vc * rstd
        vn_sc[...] = (vhat * gv_ref[...]).astype(bf16)
        for g in range(A_GROUPS):
            cs = slice(g * GBLK, (g + 1) * GBLK)
            dwm = jnp.zeros((GBLK, GBLK), f32)
            dbs = jnp.zeros((GBLK, 1), f32)
            for r in range(tm // GBLK):
                rs = slice(r * GBLK, (r + 1) * GBLK)
                zu = p_ref[rs, cs]
                vn = vn_sc[rs, cs]
                sv = _dot(wm_ref[g], vn) + bs_ref[g]
                dyb = dy_ref[rs, cs]
                dsv = dyb * _gelu(zu)
                dsvb = dsv.astype(bf16)
                dp_ref[rs, cs] = (dyb * sv * _gelu_grad(zu)).astype(bf16)
                dwm += _dot_nt(dsvb, vn)
                dbs += jnp.sum(dsv, axis=1, keepdims=True)
                dvn_sc[rs, cs] = _dot_tn(wm_ref[g], dsvb)
            dwm_ref[g] += dwm
            dbs_ref[g] += dbs
        dvn = dvn_sc[...]
        dgv_ref[...] += jnp.sum(dvn * vhat, axis=0, keepdims=True)
        dvhat = dvn * gv_ref[...]
        dv = rstd * (dvhat - jnp.mean(dvhat, axis=-1, keepdims=True)
                     - vhat * jnp.mean(dvhat * vhat, axis=-1, keepdims=True))
        dp_ref[:, D:] = (dv * _gelu_grad(zv)).astype(bf16)

    return pl.pallas_call(
        body, grid=(T // tm,),
        in_specs=[pl.BlockSpec((tm, 2 * D), lambda i: (i, 0)), pl.BlockSpec((tm, D), lambda i: (i, 0)),
                  pl.BlockSpec((1, D), lambda i: (0, 0)),
                  pl.BlockSpec((A_GROUPS, GBLK, GBLK), lambda i: (0, 0, 0)),
                  pl.BlockSpec((A_GROUPS, GBLK, GBLK), lambda i: (0, 0, 0))],
        out_specs=[pl.BlockSpec((tm, 2 * D), lambda i: (i, 0)), pl.BlockSpec((1, D), lambda i: (0, 0)),
                   pl.BlockSpec((A_GROUPS, GBLK, GBLK), lambda i: (0, 0, 0)),
                   pl.BlockSpec((A_GROUPS, GBLK, 1), lambda i: (0, 0, 0))],
        out_shape=[jax.ShapeDtypeStruct((T, 2 * D), bf16), jax.ShapeDtypeStruct((1, D), f32),
                   jax.ShapeDtypeStruct((A_GROUPS, GBLK, GBLK), f32), jax.ShapeDtypeStruct((A_GROUPS, GBLK, 1), f32)],
        scratch_shapes=[pltpu.VMEM((tm, D), bf16), pltpu.VMEM((tm, D), f32)],
        compiler_params=_cparams(("arbitrary",)), name=name,
    )(p, dy, gv, wm, bsb)


def b_mid_fwd(p, cw, *, tm, S, name):
    T = p.shape[0]
    nst = S // tm

    def body(p_ref, ph_ref, cw_ref, y_ref, ext):
        first = (pl.program_id(0) % nst) == 0
        q = p_ref[:, D:2 * D] * p_ref[:, 2 * D:]
        qh = ph_ref[:, D:2 * D] * ph_ref[:, 2 * D:]
        ext[0:HALO, :] = jnp.where(first, 0.0, qh)
        ext[HALO:, :] = q
        y = cw_ref[0:1, :] * ext[pl.ds(HALO - 2, tm), :] + cw_ref[1:2, :] * ext[pl.ds(HALO - 1, tm), :] + cw_ref[2:3, :] * q
        y_ref[...] = (p_ref[:, :D] * y).astype(bf16)

    return pl.pallas_call(
        body, grid=(T // tm,),
        in_specs=[pl.BlockSpec((tm, 3 * D), lambda i: (i, 0)), _prev_halo_spec(tm, 3 * D),
                  pl.BlockSpec((3, D), lambda i: (0, 0))],
        out_specs=pl.BlockSpec((tm, D), lambda i: (i, 0)),
        out_shape=jax.ShapeDtypeStruct((T, D), bf16),
        scratch_shapes=[pltpu.VMEM((tm + HALO, D), f32)],
        compiler_params=_cparams(("parallel",)), name=name,
    )(p, p, cw)


def b_mid_bwd(p, dy, cw, *, tm, S, name):
    T = p.shape[0]
    nst = S // tm

    def body(p_ref, ph_ref, pn_ref, dy_ref, dyn_ref, cw_ref, dp_ref, dcw_ref, ext, ext2):
        i = pl.program_id(0)
        first = (i % nst) == 0
        last = (i % nst) == nst - 1

        @pl.when(i == 0)
        def _():
            dcw_ref[...] = jnp.zeros(dcw_ref.shape, f32)

        gb, gc, xt = p_ref[:, :D], p_ref[:, D:2 * D], p_ref[:, 2 * D:]
        q = gc * xt
        ext[0:HALO, :] = jnp.where(first, 0.0, ph_ref[:, D:2 * D] * ph_ref[:, 2 * D:])
        ext[HALO:, :] = q
        q2 = ext[pl.ds(HALO - 2, tm), :]
        q1 = ext[pl.ds(HALO - 1, tm), :]
        y = cw_ref[0:1, :] * q2 + cw_ref[1:2, :] * q1 + cw_ref[2:3, :] * q
        dyo = dy_ref[...]
        dp_ref[:, :D] = (dyo * y).astype(bf16)
        dyc = dyo * gb
        ext2[0:tm, :] = dyc
        ext2[tm:, :] = jnp.where(last, 0.0, dyn_ref[...] * pn_ref[:, :D])
        dq = cw_ref[2:3, :] * dyc + cw_ref[1:2, :] * ext2[pl.ds(1, tm), :] + cw_ref[0:1, :] * ext2[pl.ds(2, tm), :]
        dp_ref[:, D:2 * D] = (dq * xt).astype(bf16)
        dp_ref[:, 2 * D:] = (dq * gc).astype(bf16)
        dcw_ref[0:1, :] += jnp.sum(dyc * q2, axis=0, keepdims=True)
        dcw_ref[1:2, :] += jnp.sum(dyc * q1, axis=0, keepdims=True)
        dcw_ref[2:3, :] += jnp.sum(dyc * q, axis=0, keepdims=True)

    return pl.pallas_call(
        body, grid=(T // tm,),
        in_specs=[pl.BlockSpec((tm, 3 * D), lambda i: (i, 0)), _prev_halo_spec(tm, 3 * D), _next_halo_spec(tm, 3 * D, T),
                  pl.BlockSpec((tm, D), lambda i: (i, 0)), _next_halo_spec(tm, D, T),
                  pl.BlockSpec((3, D), lambda i: (0, 0))],
        out_specs=[pl.BlockSpec((tm, 3 * D), lambda i: (i, 0)), pl.BlockSpec((3, D), lambda i: (0, 0))],
        out_shape=[jax.ShapeDtypeStruct((T, 3 * D), bf16), jax.ShapeDtypeStruct((3, D), f32)],
        scratch_shapes=[pltpu.VMEM((tm + HALO, D), f32), pltpu.VMEM((tm + HALO, D), f32)],
        compiler_params=_cparams(("arbitrary",)), name=name,
    )(p, p, p, dy, dy, cw)


def _pool_counts(i, nst, tm, rows, row0, w):
    t = (i % nst) * tm + row0 + lax.broadcasted_iota(jnp.int32, (rows, 1), 0)
    return jnp.minimum(t + 1, w).astype(f32)


def _pool_diff(p_ref, ext, g, i, nst, tm):
    w = POOL_WINDOWS[g]
    cs = slice(g * C_GDIM, (g + 1) * C_GDIM)
    pg = p_ref[:, cs]
    s = pg
    for k in range(1, w):
        s = s + ext[pl.ds(HALO - k, tm), cs]
    return s / _pool_counts(i, nst, tm, tm, 0, w) - pg


def c_mid_fwd(p, wg, scale, *, tm, S, name):
    T = p.shape[0]
    nst = S // tm

    def body(p_ref, ph_ref, wg_ref, sc_ref, y_ref, ext):
        i = pl.program_id(0)
        first = (i % nst) == 0
        ext[0:HALO, :] = jnp.where(first, 0.0, ph_ref[...])
        ext[HALO:, :] = p_ref[...]
        for g in range(C_GROUPS):
            cs = slice(g * C_GDIM, (g + 1) * C_GDIM)
            dg = _pool_diff(p_ref, ext, g, i, nst, tm).astype(bf16)
            wv = wg_ref[:, g].reshape(C_GDIM, C_GDIM)
            y_ref[:, cs] = (_dot(dg, wv) * sc_ref[:, cs]).astype(bf16)

    return pl.pallas_call(
        body, grid=(T // tm,),
        in_specs=[pl.BlockSpec((tm, D), lambda i: (i, 0)), _prev_halo_spec(tm, D),
                  pl.BlockSpec((NDEV, C_GROUPS, C_GDIM // NDEV, C_GDIM), lambda i: (0, 0, 0, 0)),
                  pl.BlockSpec((1, D), lambda i: (0, 0))],
        out_specs=pl.BlockSpec((tm, D), lambda i: (i, 0)),
        out_shape=jax.ShapeDtypeStruct((T, D), bf16),
        scratch_shapes=[pltpu.VMEM((tm + HALO, D), f32)],
        compiler_params=_cparams(("parallel",)), name=name,
    )(p, p, wg, scale)


def c_mid_bwd(p, dy, wg, scale, *, tm, S, name):
    T = p.shape[0]
    nst = S // tm

    def body(p_ref, ph_ref, dy_ref, dyn_ref, wg_ref, sc_ref, dp_ref, dsc_ref, dwg_ref, ext, ext2):
        i = pl.program_id(0)
        first = (i % nst) == 0
        last = (i % nst) == nst - 1

        @pl.when(i == 0)
        def _():
            dsc_ref[...] = jnp.zeros(dsc_ref.shape, f32)
            dwg_ref[...] = jnp.zeros(dwg_ref.shape, f32)

        ext[0:HALO, :] = jnp.where(first, 0.0, ph_ref[...])
        ext[HALO:, :] = p_ref[...]
        for g in range(C_GROUPS):
            w = POOL_WINDOWS[g]
            cs = slice(g * C_GDIM, (g + 1) * C_GDIM)
            dg = _pool_diff(p_ref, ext, g, i, nst, tm).astype(bf16)
            wv = wg_ref[:, g].reshape(C_GDIM, C_GDIM)
            dyo = dy_ref[:, cs]
            dsc_ref[:, cs] += jnp.sum(dyo * _dot(dg, wv), axis=0, keepdims=True)
            dyp = (dyo * sc_ref[:, cs]).astype(bf16)
            dypn = (dyn_ref[:, cs] * sc_ref[:, cs]).astype(bf16)
            dwg_ref[g] += _dot_tn(dg, dyp)
            dd = _dot_nt(dyp, wv)
            ddn = _dot_nt(dypn, wv)
            ext2[0:tm, cs] = dd / _pool_counts(i, nst, tm, tm, 0, w)
            ext2[tm:, cs] = jnp.where(last, 0.0, ddn / _pool_counts(i, nst, tm, HALO, tm, w))
            s = -dd
            for k in range(w):
                s = s + ext2[pl.ds(k, tm), cs]
            dp_ref[:, cs] = s.astype(bf16)

    return pl.pallas_call(
        body, grid=(T // tm,),
        in_specs=[pl.BlockSpec((tm, D), lambda i: (i, 0)), _prev_halo_spec(tm, D),
                  pl.BlockSpec((tm, D), lambda i: (i, 0)), _next_halo_spec(tm, D, T),
                  pl.BlockSpec((NDEV, C_GROUPS, C_GDIM // NDEV, C_GDIM), lambda i: (0, 0, 0, 0)),
                  pl.BlockSpec((1, D), lambda i: (0, 0))],
        out_specs=[pl.BlockSpec((tm, D), lambda i: (i, 0)), pl.BlockSpec((1, D), lambda i: (0, 0)),
                   pl.BlockSpec((C_GROUPS, C_GDIM, C_GDIM), lambda i: (0, 0, 0))],
        out_shape=[jax.ShapeDtypeStruct((T, D), bf16), jax.ShapeDtypeStruct((1, D), f32),
                   jax.ShapeDtypeStruct((C_GROUPS, C_GDIM, C_GDIM), f32)],
        scratch_shapes=[pltpu.VMEM((tm + HALO, D), f32), pltpu.VMEM((tm + HALO, D), f32)],
        compiler_params=_cparams(("arbitrary",)), name=name,
    )(p, p, dy, dy, wg, scale)


def final_loss(x, g, tgt, *, tm, name):
    T = x.shape[0]

    def body(x_ref, g_ref, t_ref, loss_ref, dx_ref, dg_ref):
        @pl.when(pl.program_id(0) == 0)
        def _():
            loss_ref[...] = jnp.zeros(loss_ref.shape, f32)
            dg_ref[...] = jnp.zeros(dg_ref.shape, f32)

        xv = x_ref[...]
        r = lax.rsqrt(jnp.mean(xv * xv, axis=-1, keepdims=True) + EPS)
        xhat = xv * r
        err = xhat * g_ref[...] - t_ref[...]
        loss_ref[...] += 0.5 * jnp.sum(jnp.mean(err * err, axis=-1, keepdims=True))
        dy = err * (1.0 / D)
        dg_ref[...] += jnp.sum(dy * xhat, axis=0, keepdims=True)
        dxhat = dy * g_ref[...]
        dx_ref[...] = r * (dxhat - xhat * jnp.mean(dxhat * xhat, axis=-1, keepdims=True))

    return pl.pallas_call(
        body, grid=(T // tm,),
        in_specs=[pl.BlockSpec((tm, D), lambda i: (i, 0)), pl.BlockSpec((1, D), lambda i: (0, 0)),
                  pl.BlockSpec((tm, D), lambda i: (i, 0))],
        out_specs=[pl.BlockSpec((8, 128), lambda i: (0, 0)), pl.BlockSpec((tm, D), lambda i: (i, 0)),
                   pl.BlockSpec((1, D), lambda i: (0, 0))],
        out_shape=[jax.ShapeDtypeStruct((8, 128), f32), jax.ShapeDtypeStruct((T, D), f32),
                   jax.ShapeDtypeStruct((1, D), f32)],
        compiler_params=_cparams(("arbitrary",)), name=name,
    )(x, g, tgt)


def _slot(px, py, pc):
    return 4 * px + 2 * py + pc


def all_gather(arrs, *, name):
    n = len(arrs)

    def body(*refs):
        ins, outs = refs[:n], refs[n:2 * n]
        send_sems, recv_sems, local_sems = refs[2 * n:]
        x, y, c = lax.axis_index("x"), lax.axis_index("y"), lax.axis_index("c")
        me, sibling = (x, y, c), (x, y, 1 - c)
        chips = [(1 - x, y), (x, 1 - y), (1 - x, 1 - y)]

        def copy(a, k, block, to, src=None):
            dst = outs[a].at[_slot(*block)]
            return pltpu.make_async_remote_copy(
                src_ref=dst if src is None else src, dst_ref=dst,
                send_sem=send_sems.at[a, k], recv_sem=recv_sems.at[a, k], device_id=to, device_id_type=MESH)

        mine = [pltpu.make_async_copy(ins[a], outs[a].at[_slot(*me)], local_sems.at[a]) for a in range(n)]
        for cp in mine:
            cp.start()
        first = []
        for a in range(n):
            first.append(copy(a, 0, me, sibling, src=ins[a]))
            first += [copy(a, 1 + j, me, (*chip, c), src=ins[a]) for j, chip in enumerate(chips)]
        for cp in first:
            cp.start()
        passed = []
        for j, chip in enumerate(chips):
            for a in range(n):
                copy(a, 1 + j, (*chip, c), me).wait_recv()
                fwd = copy(a, 4 + j, (*chip, c), sibling)
                fwd.start()
                passed.append(fwd)
        for a in range(n):
            copy(a, 0, sibling, me).wait_recv()
        for j, chip in enumerate(chips):
            for a in range(n):
                copy(a, 4 + j, (*chip, 1 - c), me).wait_recv()
        for cp in first + passed:
            cp.wait_send()
        for cp in mine:
            cp.wait()

    any_spec = pl.BlockSpec(memory_space=pl.ANY)
    return pl.pallas_call(
        body,
        in_specs=[any_spec] * n, out_specs=[any_spec] * n,
        out_shape=[jax.ShapeDtypeStruct((NDEV,) + a.shape, a.dtype) for a in arrs],
        scratch_shapes=[pltpu.SemaphoreType.DMA((n, 7)), pltpu.SemaphoreType.DMA((n, 7)), pltpu.SemaphoreType.DMA((n,))],
        compiler_params=pltpu.CompilerParams(has_side_effects=True), name=name,
    )(*arrs)


HBM_SPEC = pl.BlockSpec(memory_space=pltpu.HBM)
SEM_SPEC = pl.BlockSpec(memory_space=pltpu.SEMAPHORE)
ANY_SPEC = pl.BlockSpec(memory_space=pl.ANY)
TOKEN_SHAPE = jax.ShapeDtypeStruct((8, 128), f32)
DATAFLOW_EFFECT = pltpu.SideEffectType.DATAFLOW_SIDE_EFFECTING


def _in_hbm(a):
    return pltpu.with_memory_space_constraint(a, pltpu.HBM)


def _hbm_like(a):
    return pltpu.HBM(a.shape, a.dtype)


def _mesh_pos():
    return lax.axis_index("x"), lax.axis_index("y"), lax.axis_index("c")


def _gather_targets(x, y, c):
    return [(x, y, 1 - c), (1 - x, y, c), (x, 1 - y, c), (1 - x, 1 - y, c)]


def gather_start(shards, after, *, name):
    n = len(shards)
    extra = [] if after is None else [after]

    def body(*refs):
        srcs, lands = refs[:n], refs[n:2 * n]
        send_sems, recv_sems = refs[2 * n + len(extra)], refs[2 * n + len(extra) + 1]
        token = refs[-1]
        x, y, c = _mesh_pos()
        me = _slot(x, y, c)
        for a in range(n):
            for k, to in enumerate(_gather_targets(x, y, c)):
                pltpu.make_async_remote_copy(
                    src_ref=srcs[a], dst_ref=lands[a].at[me], send_sem=send_sems.at[4 * a + k], recv_sem=recv_sems.at[4 * a + k],
                    device_id=to, device_id_type=MESH).start()
        token[...] = jnp.zeros(token.shape, f32)

    lands = [lax.empty((NDEV,) + s.shape, s.dtype) for s in shards]
    sems = pltpu.SemaphoreType.DMA((4 * n,))
    out = pl.pallas_call(
        body, name=name,
        in_specs=[HBM_SPEC] * (2 * n) + [ANY_SPEC] * len(extra),
        out_specs=[SEM_SPEC, SEM_SPEC] + [HBM_SPEC] * (2 * n) + [pl.BlockSpec(memory_space=pltpu.VMEM)],
        out_shape=[sems, sems] + [_hbm_like(s) for s in shards] + [_hbm_like(l) for l in lands] + [TOKEN_SHAPE],
        input_output_aliases={i: 2 + i for i in range(2 * n)},
        compiler_params=pltpu.CompilerParams(has_side_effects=DATAFLOW_EFFECT),
    )(*[_in_hbm(s) for s in shards], *[_in_hbm(l) for l in lands], *extra)
    return out[0], out[1], out[2:2 + n], out[2 + n:2 + 2 * n], out[-1]


def gather_wait(send_sems, recv_sems, shards, lands, after, *, name):
    n = len(shards)

    def body(*refs):
        srcs, lands_in = refs[:n], refs[n:2 * n]
        send_sems, recv_sems = refs[2 * n], refs[2 * n + 1]
        x, y, c = _mesh_pos()
        for a in range(n):
            for k, frm in enumerate(_gather_targets(x, y, c)):
                cp = pltpu.make_async_remote_copy(
                    src_ref=srcs[a], dst_ref=lands_in[a].at[_slot(*frm)], send_sem=send_sems.at[4 * a + k],
                    recv_sem=recv_sems.at[4 * a + k], device_id=frm, device_id_type=MESH)
                cp.wait_send()
                cp.wait_recv()

    out = pl.pallas_call(
        body, name=name,
        in_specs=[HBM_SPEC] * (2 * n) + [SEM_SPEC, SEM_SPEC, ANY_SPEC],
        out_specs=[HBM_SPEC] * (2 * n),
        out_shape=[_hbm_like(s) for s in shards] + [_hbm_like(l) for l in lands],
        input_output_aliases={i: i for i in range(2 * n)},
        compiler_params=pltpu.CompilerParams(has_side_effects=DATAFLOW_EFFECT),
    )(*shards, *lands, send_sems, recv_sems, after)
    return out[:n], out[n:]


def gather_finish(shards, lands, *, name):
    n = len(shards)

    def body(*refs):
        srcs, lands_in, lands_out = refs[:n], refs[n:2 * n], refs[2 * n:3 * n]
        send_sems, recv_sems, local_sems = refs[3 * n:]
        x, y, c = _mesh_pos()
        sibling = (x, y, 1 - c)
        chips = [(1 - x, y), (x, 1 - y), (1 - x, 1 - y)]
        mine = [pltpu.make_async_copy(srcs[a], lands_out[a].at[_slot(x, y, c)], local_sems.at[a]) for a in range(n)]
        for cp in mine:
            cp.start()
        sent = []
        for a in range(n):
            for j, chip in enumerate(chips):
                s = _slot(*chip, c)
                cp = pltpu.make_async_remote_copy(
                    src_ref=lands_in[a].at[s], dst_ref=lands_out[a].at[s], send_sem=send_sems.at[a, j],
                    recv_sem=recv_sems.at[a, j], device_id=sibling, device_id_type=MESH)
                cp.start()
                sent.append(cp)
        for a in range(n):
            for j, chip in enumerate(chips):
                s = _slot(*chip, 1 - c)
                pltpu.make_async_remote_copy(
                    src_ref=lands_in[a].at[s], dst_ref=lands_out[a].at[s], send_sem=send_sems.at[a, j],
                    recv_sem=recv_sems.at[a, j], device_id=sibling, device_id_type=MESH).wait_recv()
        for cp in sent:
            cp.wait_send()
        for cp in mine:
            cp.wait()

    return pl.pallas_call(
        body, name=name,
        in_specs=[ANY_SPEC] * (2 * n), out_specs=[ANY_SPEC] * n,
        out_shape=[jax.ShapeDtypeStruct(l.shape, l.dtype) for l in lands],
        input_output_aliases={n + i: i for i in range(n)},
        scratch_shapes=[pltpu.SemaphoreType.DMA((n, 3)), pltpu.SemaphoreType.DMA((n, 3)), pltpu.SemaphoreType.DMA((n,))],
        compiler_params=pltpu.CompilerParams(has_side_effects=True),
    )(*shards, *lands)


def _peer(x, y, c, r):
    return (x ^ ((r >> 2) & 1), y ^ ((r >> 1) & 1), c ^ (r & 1))


def exchange_start(items, lands, after, *, name):
    n, m = len(items), len(lands)
    extra = [] if after is None else [after]

    def body(*refs):
        gs, zones = refs[:n], refs[n:n + m]
        send_sems, recv_sems = refs[n + m + len(extra)], refs[n + m + len(extra) + 1]
        token = refs[-1]
        x, y, c = _mesh_pos()
        me = _slot(x, y, c)
        for r in (1, 2, 4, 3, 5, 6, 7):
            to = _peer(x, y, c, r)
            for a, (_, zi, l) in enumerate(items):
                pltpu.make_async_remote_copy(
                    src_ref=gs[a].at[_slot(*to)], dst_ref=zones[zi].at[me, l], send_sem=send_sems.at[7 * a + r - 1],
                    recv_sem=recv_sems.at[7 * a + r - 1], device_id=to, device_id_type=MESH).start()
        token[...] = jnp.zeros(token.shape, f32)

    gs = [g for g, _, _ in items]
    sems = pltpu.SemaphoreType.DMA((7 * n,))
    out = pl.pallas_call(
        body, name=name,
        in_specs=[HBM_SPEC] * (n + m) + [ANY_SPEC] * len(extra),
        out_specs=[SEM_SPEC, SEM_SPEC] + [HBM_SPEC] * (n + m) + [pl.BlockSpec(memory_space=pltpu.VMEM)],
        out_shape=[sems, sems] + [_hbm_like(g) for g in gs] + [_hbm_like(z) for z in lands] + [TOKEN_SHAPE],
        input_output_aliases={i: 2 + i for i in range(n + m)},
        compiler_params=pltpu.CompilerParams(has_side_effects=DATAFLOW_EFFECT),
    )(*[_in_hbm(g) for g in gs], *[_in_hbm(z) for z in lands], *extra)
    return out[0], out[1], out[2:2 + n], out[2 + n:2 + n + m], out[-1]


def exchange_own(items, lands, *, name):
    n, m = len(items), len(lands)

    def body(*refs):
        gs, zones_out = refs[:n], refs[n + m:n + 2 * m]
        sems = refs[n + 2 * m]
        me = _slot(*_mesh_pos())
        cps = [pltpu.make_async_copy(gs[a].at[me], zones_out[zi].at[me, l], sems.at[a]) for a, (_, zi, l) in enumerate(items)]
        for cp in cps:
            cp.start()
        for cp in cps:
            cp.wait()

    return pl.pallas_call(
        body, name=name,
        in_specs=[ANY_SPEC] * (n + m), out_specs=[ANY_SPEC] * m,
        out_shape=[jax.ShapeDtypeStruct(z.shape, z.dtype) for z in lands],
        input_output_aliases={n + i: i for i in range(m)},
        scratch_shapes=[pltpu.SemaphoreType.DMA((n,))],
        compiler_params=pltpu.CompilerParams(has_side_effects=True),
    )(*[g for g, _, _ in items], *lands)


def exchange_wait(parts, lands, after, *, name):
    m = len(lands)
    flat_gs = [g for _, _, _, gs in parts for g in gs]
    ng = len(flat_gs)

    def body(*refs):
        gs, zones = refs[:ng], refs[ng:ng + m]
        sem_refs = refs[ng + m:ng + m + 2 * len(parts)]
        x, y, c = _mesh_pos()
        base = 0
        for pi, (items, _, _, _) in enumerate(parts):
            send_sems, recv_sems = sem_refs[2 * pi], sem_refs[2 * pi + 1]
            for r in range(1, NDEV):
                frm = _peer(x, y, c, r)
                for a, (_, zi, l) in enumerate(items):
                    cp = pltpu.make_async_remote_copy(
                        src_ref=gs[base + a].at[_slot(*frm)], dst_ref=zones[zi].at[_slot(*frm), l],
                        send_sem=send_sems.at[7 * a + r - 1], recv_sem=recv_sems.at[7 * a + r - 1],
                        device_id=frm, device_id_type=MESH)
                    cp.wait_send()
                    cp.wait_recv()
            base += len(items)

    sem_args = [s for _, ss, rs, _ in parts for s in (ss, rs)]
    out = pl.pallas_call(
        body, name=name,
        in_specs=[HBM_SPEC] * (ng + m) + [SEM_SPEC] * len(sem_args) + [ANY_SPEC],
        out_specs=[HBM_SPEC] * (ng + m),
        out_shape=[_hbm_like(g) for g in flat_gs] + [_hbm_like(z) for z in lands],
        input_output_aliases={i: i for i in range(ng + m)},
        compiler_params=pltpu.CompilerParams(has_side_effects=DATAFLOW_EFFECT),
    )(*flat_gs, *lands, *sem_args, after)
    return out[ng:]


def sum8(g, *, name):
    _, R, C = g.shape
    tr = R

    def body(g_ref, o_ref):
        s = g_ref[0]
        for k in range(1, NDEV):
            s = s + g_ref[k]
        o_ref[...] = s

    return pl.pallas_call(
        body, grid=(R // tr,),
        in_specs=[pl.BlockSpec((NDEV, tr, C), lambda i: (0, i, 0))],
        out_specs=pl.BlockSpec((tr, C), lambda i: (i, 0)),
        out_shape=jax.ShapeDtypeStruct((R, C), f32),
        compiler_params=_cparams(("parallel",)), name=name,
    )(g)


def _adam_math(w, g, m, v):
    m = ADAM_B1 * m + (1.0 - ADAM_B1) * g
    v = ADAM_B2 * v + (1.0 - ADAM_B2) * (g * g)
    m_hat = m / (1.0 - ADAM_B1 ** ADAM_STEP)
    v_hat = v / (1.0 - ADAM_B2 ** ADAM_STEP)
    delta = -ADAM_LR * (m_hat / (jnp.sqrt(v_hat) + ADAM_EPS) + ADAM_WD * w)
    return delta, m, v


def adamw_slices(w, m, v, recv, *, col0=0, tr=None, tc=None, name):
    L, R, C = w.shape
    Rp, Cp = recv.shape[2], recv.shape[3]
    tr = R if tr is None else tr
    tc = C if tc is None else tc
    assert R % tr == 0 and C % tc == 0
    assert (tr == R or Rp == R) and (tc == C or (Cp == C and col0 == 0))
    rr = Rp if tr == R else tr
    rc = Cp if tc == C else tc

    def body(w_ref, m_ref, v_ref, r_ref, g_ref, d_ref, nm_ref, nv_ref):
        c0 = col0 if tc == C else 0
        g = r_ref[0, 0:tr, c0:c0 + tc].astype(f32)
        for k in range(1, NDEV):
            g = g + r_ref[k, 0:tr, c0:c0 + tc].astype(f32)
        delta, nm, nv = _adam_math(w_ref[...], g, m_ref[...], v_ref[...])
        g_ref[...] = g
        d_ref[...] = delta
        nm_ref[...] = nm
        nv_ref[...] = nv

    wspec = pl.BlockSpec((None, tr, tc), lambda l, i, j: (l, i, j))
    return pl.pallas_call(
        body, grid=(L, R // tr, C // tc),
        in_specs=[wspec, wspec, wspec, pl.BlockSpec((NDEV, None, rr, rc), lambda l, i, j: (0, l, i, j))],
        out_specs=[wspec] * 4, out_shape=[jax.ShapeDtypeStruct((L, R, C), f32)] * 4,
        compiler_params=_cparams(("parallel", "parallel", "parallel")), name=name,
    )(w, m, v, recv)


def adamw_plain(w, m, v, g, *, name):
    R, C = w.shape

    def body(w_ref, m_ref, v_ref, g_ref, d_ref, nm_ref, nv_ref):
        delta, nm, nv = _adam_math(w_ref[...], g_ref[...], m_ref[...], v_ref[...])
        d_ref[...] = delta
        nm_ref[...] = nm
        nv_ref[...] = nv

    return pl.pallas_call(
        body, out_shape=[jax.ShapeDtypeStruct((R, C), f32)] * 3, name=name,
    )(w, m, v, g)


def _chunk_causal_mask():
    pos = jnp.arange(GBLK)
    return (pos[None, :] // CHUNK) <= (pos[:, None] // CHUNK)


def kernel(x, norm_mix_g, norm_ffn_g, final_norm_g, a_w_in, a_v_norm_g, a_w_s, a_b_s, a_w_out, b_w_in, b_conv_w, b_w_out, c_w_in, c_w_grp, c_scale, c_w_out, f_w_gate, f_w_up, f_w_down, loss_target, m_norm_mix_g, m_norm_ffn_g, m_final_norm_g, m_a_w_in, m_a_v_norm_g, m_a_w_s, m_a_b_s, m_a_w_out, m_b_w_in, m_b_conv_w, m_b_w_out, m_c_w_in, m_c_w_grp, m_c_scale, m_c_w_out, m_f_w_gate, m_f_w_up, m_f_w_down, v_norm_mix_g, v_norm_ffn_g, v_final_norm_g, v_a_w_in, v_a_v_norm_g, v_a_w_s, v_a_b_s, v_a_w_out, v_b_w_in, v_b_conv_w, v_b_w_out, v_c_w_in, v_c_w_grp, v_c_scale, v_c_w_out, v_f_w_gate, v_f_w_up, v_f_w_down):
    B, S, _ = x.shape
    T = B * S
    tm = min(512, S)
    tmm = min(1024, T)
    tms = min(512, T)
    me = _slot(lax.axis_index("x"), lax.axis_index("y"), lax.axis_index("c"))

    pad_c = ((0, 0), (0, 0), (0, F_PAD - F_LOC))
    f_gu = jnp.concatenate([jnp.pad(f_w_gate.astype(bf16), pad_c), jnp.pad(f_w_up.astype(bf16), pad_c)], axis=2)
    f_d = jnp.pad(f_w_down.astype(bf16), ((0, 0), (0, F_PAD - F_LOC), (0, 0)))
    small = jnp.concatenate([a_v_norm_g, b_conv_w[0], c_scale, jnp.zeros((2, GBLK), f32)], axis=0)
    a_in_b, a_out_b = a_w_in.astype(bf16), a_w_out.astype(bf16)
    layer_shards = [
        [a_in_b[0], a_out_b[0], f_gu[0], f_d[0], small],
        [b_w_in[0].astype(bf16), b_w_out[0].astype(bf16), f_gu[1], f_d[1]],
        [c_w_in[0].astype(bf16), c_w_grp[0].astype(bf16), c_w_out[0].astype(bf16), f_gu[2], f_d[2]],
        [a_in_b[1], a_out_b[1], f_gu[3], f_d[3]],
    ]
    started, tok = [], None
    for i in range(DEPTH):
        started.append(gather_start(layer_shards[i], tok, name=f"gather_start_{i}"))
        tok = started[-1][4]

    def layer_weights(i, after):
        send_sems, recv_sems, shards, lands, _ = started[i]
        shards, lands = gather_wait(send_sems, recv_sems, shards, lands, after, name=f"gather_wait_{i}")
        return gather_finish(shards, lands, name=f"gather_finish_{i}")

    wm_masked = jnp.where(_chunk_causal_mask()[None, None], a_w_s, 0.0).astype(bf16)
    bsb = jnp.broadcast_to(a_b_s[:, :, :, None], a_b_s.shape + (GBLK,))

    xc = x.reshape(T, D)
    saved, weights = [], []
    for i in range(DEPTH):
        kind, j = i % 3, i // 3
        gmix = norm_mix_g[i][None]
        w = layer_weights(i, tok if i == 0 else xc)
        if i == 0:
            small_full = jnp.transpose(w[4], (1, 0, 2)).reshape(8, D)
            gv_full, cw_full, cs_full = small_full[0:2], small_full[2:5], small_full[5:6]
            w = w[:4]
        weights.append(w)
        if kind == 0:
            w_in, w_out, w_fgu, w_fd = w
            p, h = norm_mm(xc, gmix, w_in, tm=tmm, name=f"a_in_{i}")
            y = a_mid_fwd(p, gv_full[j][None], wm_masked[j], bsb[j], tm=tm, name=f"a_mid_{i}")
        elif kind == 1:
            w_in, w_out, w_fgu, w_fd = w
            p, h = norm_mm(xc, gmix, w_in, tm=tmm, name=f"b_in_{i}")
            y = b_mid_fwd(p, cw_full, tm=tm, S=S, name=f"b_mid_{i}")
        else:
            w_in, w_grp, w_out, w_fgu, w_fd = w
            p, h = norm_mm(xc, gmix, w_in, tm=tmm, row_sharded=True, name=f"c_in_{i}")
            y = c_mid_fwd(p, w_grp, cs_full, tm=tm, S=S, name=f"c_mid_{i}")
        x1 = mm_res(y, w_out, xc, tm=tms, name=f"mix_out_{i}")
        pf, h2, a = norm_mm(x1, norm_ffn_g[i][None], w_fgu, tm=tmm, swiglu=True, name=f"f_in_{i}")
        x2 = mm_res(a, w_fd, x1, tm=tms, name=f"f_out_{i}")
        saved.append((xc, p, h, y, x1, pf, h2, a))
        xc = x2

    loss_blk, dx, d_final = final_loss(xc, final_norm_g[None], loss_target.reshape(T, D), tm=tms, name="final_loss")
    loss = lax.psum(loss_blk[0, 0], AXES)

    zone_shapes = [(2, D, 256), (2, GBLK, D), (1, D, 384), (1, GBLK, D), (1, GBLK, D), (1, GBLK, C_GDIM), (1, GBLK, D),
                   (DEPTH, D, 2 * F_PAD), (DEPTH, F_PAD, D)]
    Z_AIN, Z_AOUT, Z_BIN, Z_BOUT, Z_CIN, Z_CGRP, Z_COUT, Z_FGU, Z_FD = range(9)
    lands = [lax.empty((NDEV,) + s, bf16) for s in zone_shapes]
    parts, all_items = [], []

    def exchange(items, after, name):
        nonlocal lands
        idx = sorted({zi for _, zi, _ in items})
        local = [(g, idx.index(zi), l) for g, zi, l in items]
        send_sems, recv_sems, gs, zs, token = exchange_start(local, [lands[zi] for zi in idx], after, name=name)
        for q, zi in enumerate(idx):
            lands[zi] = zs[q]
        parts.append(([(None, zi, l) for _, zi, l in items], send_sems, recv_sems, gs))
        all_items.extend((gs[q], zi, l) for q, (_, zi, l) in enumerate(items))
        return token

    d_mix, d_ffn = [None] * DEPTH, [None] * DEPTH
    d_gv, d_wm, d_bs = [None] * 2, [None] * 2, [None] * 2
    tok = None
    for i in reversed(range(DEPTH)):
        kind, j = i % 3, i // 3
        xin, p, h, y, x1, pf, h2, a = saved[i]
        w = weights[i]
        w_in, w_out, w_fgu, w_fd = w[0], w[-3], w[-2], w[-1]
        gffn = norm_ffn_g[i][None] if tok is None else norm_ffn_g[i][None] + tok[0:1, 0:1]
        dpf, dxb = ffn_bwd_mid(dx, w_fd, pf, tm=tmm, name=f"f_bwd_mid_{i}")
        dx1, d_ffn[i] = mm_nt_normbwd(dpf, w_fgu, x1, gffn, dx, tm=tms, name=f"f_bwd_in_{i}")
        items = [(mm_tn(h2, dpf, tm=tmm, row_sharded=False, name=f"f_dw_in_{i}"), Z_FGU, i),
                 (mm_tn(a, dxb, tm=tmm, row_sharded=True, name=f"f_dw_out_{i}"), Z_FD, i)]
        gmix = norm_mix_g[i][None]
        if i == 0:
            gmix = gmix + exchange(items, tok, "exchange_start_ffn0")[0:1, 0:1]
            items = []
        dy, dx1b = mm_nt(dx1, w_out, tm=tmm, name=f"mix_bwd_out_{i}")
        if kind == 0:
            dp, d_gv[j], d_wm[j], d_bs[j] = a_mid_bwd(p, dy, gv_full[j][None], wm_masked[j], bsb[j], tm=tm, name=f"a_bwd_mid_{i}")
            dx, d_mix[i] = mm_nt_normbwd(dp, w_in, xin, gmix, dx1, tm=tms, name=f"a_bwd_in_{i}")
            items += [(mm_tn(h, dp, tm=tmm, row_sharded=False, name=f"a_dw_in_{i}"), Z_AIN, j),
                      (mm_tn(y, dx1b, tm=tmm, row_sharded=True, name=f"a_dw_out_{i}"), Z_AOUT, j)]
        elif kind == 1:
            dp, d_cw = b_mid_bwd(p, dy, cw_full, tm=tm, S=S, name=f"b_bwd_mid_{i}")
            dx, d_mix[i] = mm_nt_normbwd(dp, w_in, xin, gmix, dx1, tm=tms, name=f"b_bwd_in_{i}")
            items += [(mm_tn(h, dp, tm=tmm, row_sharded=False, name=f"b_dw_in_{i}"), Z_BIN, 0),
                      (mm_tn(y, dx1b, tm=tmm, row_sharded=True, name=f"b_dw_out_{i}"), Z_BOUT, 0)]
        else:
            dp, d_cs, d_wgrp = c_mid_bwd(p, dy, w[1], cs_full, tm=tm, S=S, name=f"c_bwd_mid_{i}")
            dx, d_mix[i] = mm_nt_normbwd(dp, w_in, xin, gmix, dx1, tm=tms, row_sharded=True, name=f"c_bwd_in_{i}")
            gw_cgrp = jnp.transpose(d_wgrp.reshape(C_GROUPS, NDEV, C_GDIM // NDEV, C_GDIM), (1, 0, 2, 3)).astype(bf16)
            items += [(mm_tn(h, dp, tm=tmm, row_sharded=True, name=f"c_dw_in_{i}"), Z_CIN, 0),
                      (gw_cgrp.reshape(NDEV, GBLK, C_GDIM), Z_CGRP, 0),
                      (mm_tn(y, dx1b, tm=tmm, row_sharded=True, name=f"c_dw_out_{i}"), Z_COUT, 0)]
        tok = exchange(items, tok, f"exchange_start_{i}")
    grad_x = dx.reshape(B, S, D)

    lands = exchange_own(all_items, lands, name="exchange_own")
    lands = exchange_wait(parts, lands, dx, name="exchange_wait")
    r_ain, r_aout, r_bin, r_bout, r_cin, r_cgrp, r_cout, r_fgu, r_fd = lands

    mask = _chunk_causal_mask()
    small_parts = [
        jnp.concatenate(d_mix, axis=0).reshape(-1, GBLK),
        jnp.concatenate(d_ffn, axis=0).reshape(-1, GBLK),
        d_final.reshape(-1, GBLK),
        jnp.concatenate(d_gv, axis=0).reshape(-1, GBLK),
        jnp.where(mask[None, None], jnp.stack(d_wm), 0.0).reshape(-1, GBLK),
        jnp.stack(d_bs).reshape(-1, GBLK),
        d_cw.reshape(-1, GBLK),
        d_cs.reshape(-1, GBLK),
    ]
    small_rows = [q.shape[0] for q in small_parts]
    (gs_all,) = all_gather([jnp.concatenate(small_parts, axis=0)], name="gather_small_grads")
    gs = sum8(gs_all, name="sum_small_grads")
    offs = [0]
    for r in small_rows:
        offs.append(offs[-1] + r)
    sp = [gs[offs[q]:offs[q + 1]] for q in range(len(small_parts))]
    grad_norm_mix_g = sp[0].reshape(DEPTH, D)
    grad_norm_ffn_g = sp[1].reshape(DEPTH, D)
    grad_final_norm_g = sp[2].reshape(D)
    grad_a_w_s = sp[4].reshape(a_w_s.shape)
    grad_a_b_s = sp[5].reshape(a_b_s.shape)

    def my_cols(full):
        return lax.dynamic_slice_in_dim(full, me * GBLK, GBLK, axis=1)

    grad_a_v_norm_g = my_cols(sp[3].reshape(2, D))
    grad_b_conv_w = my_cols(sp[6].reshape(3, D))[None]
    grad_c_scale = my_cols(sp[7].reshape(1, D))

    def plain(w, m, v, g, name):
        shp = w.shape
        w2, m2, v2, g2 = (t.reshape(-1, shp[-1]) for t in (w, m, v, g))
        return tuple(t.reshape(shp) for t in adamw_plain(w2, m2, v2, g2, name=name))

    def sliced(w, m, v, recv, name, **kw):
        shp = w.shape
        w3, m3, v3 = (t.reshape((shp[0], -1, shp[-1])) for t in (w, m, v))
        return tuple(t.reshape(shp) for t in adamw_slices(w3, m3, v3, recv, name=name, **kw))

    res = {}
    res["norm_mix_g"] = (grad_norm_mix_g,) + plain(norm_mix_g, m_norm_mix_g, v_norm_mix_g, grad_norm_mix_g, "adam_norm_mix")
    res["norm_ffn_g"] = (grad_norm_ffn_g,) + plain(norm_ffn_g, m_norm_ffn_g, v_norm_ffn_g, grad_norm_ffn_g, "adam_norm_ffn")
    res["final_norm_g"] = (grad_final_norm_g,) + tuple(
        t.reshape(D) for t in plain(final_norm_g[None], m_final_norm_g[None], v_final_norm_g[None], grad_final_norm_g[None], "adam_final"))
    res["a_w_in"] = sliced(a_w_in, m_a_w_in, v_a_w_in, r_ain, "adam_a_w_in", tr=256)
    res["a_v_norm_g"] = (grad_a_v_norm_g,) + plain(a_v_norm_g, m_a_v_norm_g, v_a_v_norm_g, grad_a_v_norm_g, "adam_a_v_norm")
    res["a_w_s"] = (grad_a_w_s,) + plain(a_w_s, m_a_w_s, v_a_w_s, grad_a_w_s, "adam_a_w_s")
    res["a_b_s"] = (grad_a_b_s,) + plain(a_b_s, m_a_b_s, v_a_b_s, grad_a_b_s, "adam_a_b_s")
    res["a_w_out"] = sliced(a_w_out, m_a_w_out, v_a_w_out, r_aout, "adam_a_w_out")
    res["b_w_in"] = sliced(b_w_in, m_b_w_in, v_b_w_in, r_bin, "adam_b_w_in", tr=256)
    res["b_conv_w"] = (grad_b_conv_w,) + plain(b_conv_w, m_b_conv_w, v_b_conv_w, grad_b_conv_w, "adam_b_conv")
    res["b_w_out"] = sliced(b_w_out, m_b_w_out, v_b_w_out, r_bout, "adam_b_w_out")
    res["c_w_in"] = sliced(c_w_in, m_c_w_in, v_c_w_in, r_cin, "adam_c_w_in")
    res["c_w_grp"] = sliced(c_w_grp, m_c_w_grp, v_c_w_grp, r_cgrp, "adam_c_w_grp")
    res["c_scale"] = (grad_c_scale,) + plain(c_scale, m_c_scale, v_c_scale, grad_c_scale, "adam_c_scale")
    res["c_w_out"] = sliced(c_w_out, m_c_w_out, v_c_w_out, r_cout, "adam_c_w_out")
    res["f_w_gate"] = sliced(f_w_gate, m_f_w_gate, v_f_w_gate, r_fgu, "adam_f_gate", col0=0, tr=256)
    res["f_w_up"] = sliced(f_w_up, m_f_w_up, v_f_w_up, r_fgu, "adam_f_up", col0=F_PAD, tr=256)
    res["f_w_down"] = sliced(f_w_down, m_f_w_down, v_f_w_down, r_fd, "adam_f_down", tc=256)

    order = ["norm_mix_g", "norm_ffn_g", "final_norm_g", "a_w_in", "a_v_norm_g", "a_w_s", "a_b_s", "a_w_out", "b_w_in",
             "b_conv_w", "b_w_out", "c_w_in", "c_w_grp", "c_scale", "c_w_out", "f_w_gate", "f_w_up", "f_w_down"]
    return (loss, grad_x, *[res[n][0] for n in order], *[res[n][1] for n in order],
            *[res[n][2] for n in order], *[res[n][3] for n in order])
```

```python
import functools

import jax
import jax.numpy as jnp
from jax import lax
from jax.experimental import pallas as pl
from jax.experimental.pallas import tpu as pltpu

f32 = jnp.float32
bf16 = jnp.bfloat16

NDEV = 8
D = 1024
EPS = 1e-6
CHUNK = 64
GBLK = 128
A_GROUPS = 8
C_GROUPS = 4
C_GDIM = 256
POOL_WINDOWS = (2, 4, 8, 16)
HALO = 16
F_LOC = 352
F_PAD = 384
DEPTH = 4
AXES = ("x", "y", "c")
MESH = pl.DeviceIdType.MESH

ADAM_LR = 0.001
ADAM_B1 = 0.9
ADAM_B2 = 0.999
ADAM_EPS = 1e-08
ADAM_WD = 0.01
ADAM_STEP = 10

VMEM_LIMIT = 56 * 1024 * 1024


def _cparams(sem):
    return pltpu.CompilerParams(dimension_semantics=sem, vmem_limit_bytes=VMEM_LIMIT)


def _gelu(z):
    return 0.5 * z * (1.0 + lax.erf(z * 0.7071067811865476))


def _gelu_grad(z):
    return 0.5 * (1.0 + lax.erf(z * 0.7071067811865476)) + z * jnp.exp(-0.5 * z * z) * 0.3989422804014327


def _dot(a, b):
    return jnp.dot(a, b, preferred_element_type=f32)


def _dot_nt(a, b):
    return lax.dot_general(a, b, (((1,), (1,)), ((), ())), preferred_element_type=f32)


def _dot_tn(a, b):
    return lax.dot_general(a, b, (((0,), (0,)), ((), ())), preferred_element_type=f32)


def norm_mm(x, g, w, *, tm, row_sharded=False, swiglu=False, name):
    T = x.shape[0]
    if row_sharded:
        nchunk, n = 1, w.shape[2]
        w_spec = pl.BlockSpec((NDEV, w.shape[1], n), lambda i, d: (0, 0, 0))
    else:
        nchunk, n = NDEV, w.shape[2]
        w_spec = pl.BlockSpec((None, D, n), lambda i, d: (d, 0, 0))

    def body(x_ref, g_ref, w_ref, p_ref, h_ref, *rest):
        @pl.when(pl.program_id(1) == 0)
        def _():
            xv = x_ref[...]
            r = lax.rsqrt(jnp.mean(xv * xv, axis=-1, keepdims=True) + EPS)
            h_ref[...] = (xv * r * g_ref[...]).astype(bf16)

        wv = w_ref[...].reshape(D, n) if row_sharded else w_ref[...]
        p = _dot(h_ref[...], wv)
        p_ref[...] = p
        if swiglu:
            gate, up = p[:, : n // 2], p[:, n // 2:]
            rest[0][...] = (gate * jax.nn.sigmoid(gate) * up).astype(bf16)

    out_shape = [jax.ShapeDtypeStruct((T, nchunk * n), f32), jax.ShapeDtypeStruct((T, D), bf16)]
    out_specs = [pl.BlockSpec((tm, n), lambda i, d: (i, d)), pl.BlockSpec((tm, D), lambda i, d: (i, 0))]
    if swiglu:
        out_shape.append(jax.ShapeDtypeStruct((T, nchunk * n // 2), bf16))
        out_specs.append(pl.BlockSpec((tm, n // 2), lambda i, d: (i, d)))
    return pl.pallas_call(
        body, grid=(T // tm, nchunk),
        in_specs=[pl.BlockSpec((tm, D), lambda i, d: (i, 0)), pl.BlockSpec((1, D), lambda i, d: (0, 0)), w_spec],
        out_specs=out_specs, out_shape=out_shape,
        compiler_params=_cparams(("parallel", "arbitrary")), name=name,
    )(x, g, w)


def mm_res(y, w, x, *, tm, name):
    T, K = y.shape
    k = w.shape[1]

    def body(y_ref, w_ref, x_ref, o_ref):
        o_ref[...] = x_ref[...] + _dot(y_ref[...], w_ref[...].reshape(K, D))

    return pl.pallas_call(
        body, grid=(T // tm,),
        in_specs=[pl.BlockSpec((tm, K), lambda i: (i, 0)),
                  pl.BlockSpec((NDEV, k, D), lambda i: (0, 0, 0)),
                  pl.BlockSpec((tm, D), lambda i: (i, 0))],
        out_specs=pl.BlockSpec((tm, D), lambda i: (i, 0)),
        out_shape=jax.ShapeDtypeStruct((T, D), f32),
        compiler_params=_cparams(("parallel",)), name=name,
    )(y, w, x)


def mm_nt(dy, w, *, tm, name):
    T = dy.shape[0]
    k = w.shape[1]
    K = NDEV * k

    def body(dy_ref, w_ref, o_ref, dyb_ref):
        dyb = dy_ref[...].astype(bf16)
        dyb_ref[...] = dyb
        o_ref[...] = _dot_nt(dyb, w_ref[...].reshape(K, D))

    return pl.pallas_call(
        body, grid=(T // tm,),
        in_specs=[pl.BlockSpec((tm, D), lambda i: (i, 0)),
                  pl.BlockSpec((NDEV, k, D), lambda i: (0, 0, 0))],
        out_specs=[pl.BlockSpec((tm, K), lambda i: (i, 0)), pl.BlockSpec((tm, D), lambda i: (i, 0))],
        out_shape=[jax.ShapeDtypeStruct((T, K), f32), jax.ShapeDtypeStruct((T, D), bf16)],
        compiler_params=_cparams(("parallel",)), name=name,
    )(dy, w)


def ffn_bwd_mid(dx, wd, pf, *, tm, name):
    T = dx.shape[0]

    def body(dx_ref, w_ref, p_ref, dp_ref, dxb_ref):
        @pl.when(pl.program_id(1) == 0)
        def _():
            dxb_ref[...] = dx_ref[...].astype(bf16)

        da = _dot_nt(dxb_ref[...], w_ref[...])
        gate, up = p_ref[:, :F_PAD], p_ref[:, F_PAD:]
        sg = jax.nn.sigmoid(gate)
        dp_ref[:, :F_PAD] = (da * up * (sg * (1.0 + gate * (1.0 - sg)))).astype(bf16)
        dp_ref[:, F_PAD:] = (da * gate * sg).astype(bf16)

    return pl.pallas_call(
        body, grid=(T // tm, NDEV),
        in_specs=[pl.BlockSpec((tm, D), lambda i, d: (i, 0)),
                  pl.BlockSpec((None, F_PAD, D), lambda i, d: (d, 0, 0)),
                  pl.BlockSpec((tm, 2 * F_PAD), lambda i, d: (i, d))],
        out_specs=[pl.BlockSpec((tm, 2 * F_PAD), lambda i, d: (i, d)), pl.BlockSpec((tm, D), lambda i, d: (i, 0))],
        out_shape=[jax.ShapeDtypeStruct(pf.shape, bf16), jax.ShapeDtypeStruct((T, D), bf16)],
        compiler_params=_cparams(("parallel", "arbitrary")), name=name,
    )(dx, wd, pf)


def mm_nt_normbwd(dp, w, x, g, dres, *, tm, row_sharded=False, name):
    T = x.shape[0]
    if row_sharded:
        nchunk, n = 1, w.shape[2]
        w_spec = pl.BlockSpec((NDEV, w.shape[1], n), lambda i, d: (0, 0, 0))
    else:
        nchunk, n = NDEV, w.shape[2]
        w_spec = pl.BlockSpec((None, D, n), lambda i, d: (d, 0, 0))

    def body(dp_ref, w_ref, x_ref, g_ref, dres_ref, dx_ref, dg_ref, acc):
        i, d = pl.program_id(0), pl.program_id(1)
        wv = w_ref[...].reshape(D, n) if row_sharded else w_ref[...]
        part = _dot_nt(dp_ref[...], wv)

        @pl.when(d == 0)
        def _():
            acc[...] = part

        @pl.when(d > 0)
        def _():
            acc[...] += part

        @pl.when(jnp.logical_and(i == 0, d == 0))
        def _():
            dg_ref[...] = jnp.zeros(dg_ref.shape, f32)

        @pl.when(d == nchunk - 1)
        def _():
            xv = x_ref[...]
            r = lax.rsqrt(jnp.mean(xv * xv, axis=-1, keepdims=True) + EPS)
            xhat = xv * r
            dh = acc[...]
            dg_ref[...] += jnp.sum(dh * xhat, axis=0, keepdims=True)
            dxhat = dh * g_ref[...]
            dx_ref[...] = dres_ref[...] + r * (dxhat - xhat * jnp.mean(dxhat * xhat, axis=-1, keepdims=True))

    return pl.pallas_call(
        body, grid=(T // tm, nchunk),
        in_specs=[pl.BlockSpec((tm, n), lambda i, d: (i, d)), w_spec,
                  pl.BlockSpec((tm, D), lambda i, d: (i, 0)), pl.BlockSpec((1, D), lambda i, d: (0, 0)),
                  pl.BlockSpec((tm, D), lambda i, d: (i, 0))],
        out_specs=[pl.BlockSpec((tm, D), lambda i, d: (i, 0)), pl.BlockSpec((1, D), lambda i, d: (0, 0))],
        out_shape=[jax.ShapeDtypeStruct((T, D), f32), jax.ShapeDtypeStruct((1, D), f32)],
        scratch_shapes=[pltpu.VMEM((tm, D), f32)],
        compiler_params=_cparams(("arbitrary", "arbitrary")), name=name,
    )(dp, w, x, g, dres)


def mm_tn(a, b, *, tm, row_sharded, name):
    T = a.shape[0]
    if row_sharded:
        k, n = a.shape[1] // NDEV, b.shape[1]
        a_spec = pl.BlockSpec((tm, k), lambda d, t: (t, d))
        b_spec = pl.BlockSpec((tm, n), lambda d, t: (t, 0))
    else:
        k, n = a.shape[1], b.shape[1] // NDEV
        a_spec = pl.BlockSpec((tm, k), lambda d, t: (t, 0))
        b_spec = pl.BlockSpec((tm, n), lambda d, t: (t, d))
    nt = T // tm

    def body(a_ref, b_ref, o_ref, acc):
        t = pl.program_id(1)
        part = _dot_tn(a_ref[...], b_ref[...])

        @pl.when(t == 0)
        def _():
            acc[...] = part

        @pl.when(t > 0)
        def _():
            acc[...] += part

        @pl.when(t == nt - 1)
        def _():
            o_ref[...] = acc[...].astype(bf16)

    return pl.pallas_call(
        body, grid=(NDEV, nt), in_specs=[a_spec, b_spec],
        out_specs=pl.BlockSpec((None, k, n), lambda d, t: (d, 0, 0)),
        out_shape=jax.ShapeDtypeStruct((NDEV, k, n), bf16),
        scratch_shapes=[pltpu.VMEM((k, n), f32)],
        compiler_params=_cparams(("parallel", "arbitrary")), name=name,
    )(a, b)


def _prev_halo_spec(tm, ncol):
    return pl.BlockSpec((HALO, ncol), lambda i: (jnp.maximum(i * (tm // HALO) - 1, 0), 0))


def _next_halo_spec(tm, ncol, T):
    return pl.BlockSpec((HALO, ncol), lambda i: (jnp.minimum((i + 1) * (tm // HALO), T // HALO - 1), 0))


def a_mid_fwd(p, gv, wm, bsb, *, tm, name):
    T = p.shape[0]

    def body(p_ref, gv_ref, wm_ref, bs_ref, y_ref, vn_sc):
        v = _gelu(p_ref[:, D:])
        vc = v - jnp.mean(v, axis=-1, keepdims=True)
        var = jnp.mean(vc * vc, axis=-1, keepdims=True)
        vn_sc[...] = (vc * lax.rsqrt(var + EPS) * gv_ref[...]).astype(bf16)
        for g in range(A_GROUPS):
            cs = slice(g * GBLK, (g + 1) * GBLK)
            for r in range(tm // GBLK):
                rs = slice(r * GBLK, (r + 1) * GBLK)
                sv = _dot(wm_ref[g], vn_sc[rs, cs]) + bs_ref[g]
                y_ref[rs, cs] = (_gelu(p_ref[rs, cs]) * sv).astype(bf16)

    return pl.pallas_call(
        body, grid=(T // tm,),
        in_specs=[pl.BlockSpec((tm, 2 * D), lambda i: (i, 0)), pl.BlockSpec((1, D), lambda i: (0, 0)),
                  pl.BlockSpec((A_GROUPS, GBLK, GBLK), lambda i: (0, 0, 0)),
                  pl.BlockSpec((A_GROUPS, GBLK, GBLK), lambda i: (0, 0, 0))],
        out_specs=pl.BlockSpec((tm, D), lambda i: (i, 0)),
        out_shape=jax.ShapeDtypeStruct((T, D), bf16),
        scratch_shapes=[pltpu.VMEM((tm, D), bf16)],
        compiler_params=_cparams(("parallel",)), name=name,
    )(p, gv, wm, bsb)


def a_mid_bwd(p, dy, gv, wm, bsb, *, tm, name):
    T = p.shape[0]

    def body(p_ref, dy_ref, gv_ref, wm_ref, bs_ref, dp_ref, dgv_ref, dwm_ref, dbs_ref, vn_sc, dvn_sc):
        @pl.when(pl.program_id(0) == 0)
        def _():
            dgv_ref[...] = jnp.zeros(dgv_ref.shape, f32)
            dwm_ref[...] = jnp.zeros(dwm_ref.shape, f32)
            dbs_ref[...] = jnp.zeros(dbs_ref.shape, f32)

        zv = p_ref[:, D:]
        v = _gelu(zv)
        vc = v - jnp.mean(v, axis=-1, keepdims=True)
        rstd = lax.rsqrt(jnp.mean(vc * vc, axis=-1, keepdims=True) + EPS)
        vhat = vc * rstd
        vn_sc[...] = (vhat * gv_ref[...]).astype(bf16)
        for g in range(A_GROUPS):
            cs = slice(g * GBLK, (g + 1) * GBLK)
            dwm = jnp.zeros((GBLK, GBLK), f32)
            dbs = jnp.zeros((GBLK, 1), f32)
            for r in range(tm // GBLK):
                rs = slice(r * GBLK, (r + 1) * GBLK)
                zu = p_ref[rs, cs]
                vn = vn_sc[rs, cs]
                sv = _dot(wm_ref[g], vn) + bs_ref[g]
                dyb = dy_ref[rs, cs]
                dsv = dyb * _gelu(zu)
                dsvb = dsv.astype(bf16)
                dp_ref[rs, cs] = (dyb * sv * _gelu_grad(zu)).astype(bf16)
                dwm += _dot_nt(dsvb, vn)
                dbs += jnp.sum(dsv, axis=1, keepdims=True)
                dvn_sc[rs, cs] = _dot_tn(wm_ref[g], dsvb)
            dwm_ref[g] += dwm
            dbs_ref[g] += dbs
        dvn = dvn_sc[...]
        dgv_ref[...] += jnp.sum(dvn * vhat, axis=0, keepdims=True)
        dvhat = dvn * gv_ref[...]
        dv = rstd * (dvhat - jnp.mean(dvhat, axis=-1, keepdims=True)
                     - vhat * jnp.mean(dvhat * vhat, axis=-1, keepdims=True))
        dp_ref[:, D:] = (dv * _gelu_grad(zv)).astype(bf16)

    return pl.pallas_call(
        body, grid=(T // tm,),
        in_specs=[pl.BlockSpec((tm, 2 * D), lambda i: (i, 0)), pl.BlockSpec((tm, D), lambda i: (i, 0)),
                  pl.BlockSpec((1, D), lambda i: (0, 0)),
                  pl.BlockSpec((A_GROUPS, GBLK, GBLK), lambda i: (0, 0, 0)),
                  pl.BlockSpec((A_GROUPS, GBLK, GBLK), lambda i: (0, 0, 0))],
        out_specs=[pl.BlockSpec((tm, 2 * D), lambda i: (i, 0)), pl.BlockSpec((1, D), lambda i: (0, 0)),
                   pl.BlockSpec((A_GROUPS, GBLK, GBLK), lambda i: (0, 0, 0)),
                   pl.BlockSpec((A_GROUPS, GBLK, 1), lambda i: (0, 0, 0))],
        out_shape=[jax.ShapeDtypeStruct((T, 2 * D), bf16), jax.ShapeDtypeStruct((1, D), f32),
                   jax.ShapeDtypeStruct((A_GROUPS, GBLK, GBLK), f32), jax.ShapeDtypeStruct((A_GROUPS, GBLK, 1), f32)],
        scratch_shapes=[pltpu.VMEM((tm, D), bf16), pltpu.VMEM((tm, D), f32)],
        compiler_params=_cparams(("arbitrary",)), name=name,
    )(p, dy, gv, wm, bsb)


def b_mid_fwd(p, cw, *, tm, S, name):
    T = p.shape[0]
    nst = S // tm

    def body(p_ref, ph_ref, cw_ref, y_ref, ext):
        first = (pl.program_id(0) % nst) == 0
        q = p_ref[:, D:2 * D] * p_ref[:, 2 * D:]
        qh = ph_ref[:, D:2 * D] * ph_ref[:, 2 * D:]
        ext[0:HALO, :] = jnp.where(first, 0.0, qh)
        ext[HALO:, :] = q
        y = cw_ref[0:1, :] * ext[pl.ds(HALO - 2, tm), :] + cw_ref[1:2, :] * ext[pl.ds(HALO - 1, tm), :] + cw_ref[2:3, :] * q
        y_ref[...] = (p_ref[:, :D] * y).astype(bf16)

    return pl.pallas_call(
        body, grid=(T // tm,),
        in_specs=[pl.BlockSpec((tm, 3 * D), lambda i: (i, 0)), _prev_halo_spec(tm, 3 * D),
                  pl.BlockSpec((3, D), lambda i: (0, 0))],
        out_specs=pl.BlockSpec((tm, D), lambda i: (i, 0)),
        out_shape=jax.ShapeDtypeStruct((T, D), bf16),
        scratch_shapes=[pltpu.VMEM((tm + HALO, D), f32)],
        compiler_params=_cparams(("parallel",)), name=name,
    )(p, p, cw)


def b_mid_bwd(p, dy, cw, *, tm, S, name):
    T = p.shape[0]
    nst = S // tm

    def body(p_ref, ph_ref, pn_ref, dy_ref, dyn_ref, cw_ref, dp_ref, dcw_ref, ext, ext2):
        i = pl.program_id(0)
        first = (i % nst) == 0
        last = (i % nst) == nst - 1

        @pl.when(i == 0)
        def _():
            dcw_ref[...] = jnp.zeros(dcw_ref.shape, f32)

        gb, gc, xt = p_ref[:, :D], p_ref[:, D:2 * D], p_ref[:, 2 * D:]
        q = gc * xt
        ext[0:HALO, :] = jnp.where(first, 0.0, ph_ref[:, D:2 * D] * ph_ref[:, 2 * D:])
        ext[HALO:, :] = q
        q2 = ext[pl.ds(HALO - 2, tm), :]
        q1 = ext[pl.ds(HALO - 1, tm), :]
        y = cw_ref[0:1, :] * q2 + cw_ref[1:2, :] * q1 + cw_ref[2:3, :] * q
        dyo = dy_ref[...]
        dp_ref[:, :D] = (dyo * y).astype(bf16)
        dyc = dyo * gb
        ext2[0:tm, :] = dyc
        ext2[tm:, :] = jnp.where(last, 0.0, dyn_ref[...] * pn_ref[:, :D])
        dq = cw_ref[2:3, :] * dyc + cw_ref[1:2, :] * ext2[pl.ds(1, tm), :] + cw_ref[0:1, :] * ext2[pl.ds(2, tm), :]
        dp_ref[:, D:2 * D] = (dq * xt).astype(bf16)
        dp_ref[:, 2 * D:] = (dq * gc).astype(bf16)
        dcw_ref[0:1, :] += jnp.sum(dyc * q2, axis=0, keepdims=True)
        dcw_ref[1:2, :] += jnp.sum(dyc * q1, axis=0, keepdims=True)
        dcw_ref[2:3, :] += jnp.sum(dyc * q, axis=0, keepdims=True)

    return pl.pallas_call(
        body, grid=(T // tm,),
        in_specs=[pl.BlockSpec((tm, 3 * D), lambda i: (i, 0)), _prev_halo_spec(tm, 3 * D), _next_halo_spec(tm, 3 * D, T),
                  pl.BlockSpec((tm, D), lambda i: (i, 0)), _next_halo_spec(tm, D, T),
                  pl.BlockSpec((3, D), lambda i: (0, 0))],
        out_specs=[pl.BlockSpec((tm, 3 * D), lambda i: (i, 0)), pl.BlockSpec((3, D), lambda i: (0, 0))],
        out_shape=[jax.ShapeDtypeStruct((T, 3 * D), bf16), jax.ShapeDtypeStruct((3, D), f32)],
        scratch_shapes=[pltpu.VMEM((tm + HALO, D), f32), pltpu.VMEM((tm + HALO, D), f32)],
        compiler_params=_cparams(("arbitrary",)), name=name,
    )(p, p, p, dy, dy, cw)


def _pool_counts(i, nst, tm, rows, row0, w):
    t = (i % nst) * tm + row0 + lax.broadcasted_iota(jnp.int32, (rows, 1), 0)
    return jnp.minimum(t + 1, w).astype(f32)


def _pool_diff(p_ref, ext, g, i, nst, tm):
    w = POOL_WINDOWS[g]
    cs = slice(g * C_GDIM, (g + 1) * C_GDIM)
    pg = p_ref[:, cs]
    s = pg
    for k in range(1, w):
        s = s + ext[pl.ds(HALO - k, tm), cs]
    return s / _pool_counts(i, nst, tm, tm, 0, w) - pg


def c_mid_fwd(p, wg, scale, *, tm, S, name):
    T = p.shape[0]
    nst = S // tm

    def body(p_ref, ph_ref, wg_ref, sc_ref, y_ref, ext):
        i = pl.program_id(0)
        first = (i % nst) == 0
        ext[0:HALO, :] = jnp.where(first, 0.0, ph_ref[...])
        ext[HALO:, :] = p_ref[...]
        for g in range(C_GROUPS):
            cs = slice(g * C_GDIM, (g + 1) * C_GDIM)
            dg = _pool_diff(p_ref, ext, g, i, nst, tm).astype(bf16)
            wv = wg_ref[:, g].reshape(C_GDIM, C_GDIM)
            y_ref[:, cs] = (_dot(dg, wv) * sc_ref[:, cs]).astype(bf16)

    return pl.pallas_call(
        body, grid=(T // tm,),
        in_specs=[pl.BlockSpec((tm, D), lambda i: (i, 0)), _prev_halo_spec(tm, D),
                  pl.BlockSpec((NDEV, C_GROUPS, C_GDIM // NDEV, C_GDIM), lambda i: (0, 0, 0, 0)),
                  pl.BlockSpec((1, D), lambda i: (0, 0))],
        out_specs=pl.BlockSpec((tm, D), lambda i: (i, 0)),
        out_shape=jax.ShapeDtypeStruct((T, D), bf16),
        scratch_shapes=[pltpu.VMEM((tm + HALO, D), f32)],
        compiler_params=_cparams(("parallel",)), name=name,
    )(p, p, wg, scale)


def c_mid_bwd(p, dy, wg, scale, *, tm, S, name):
    T = p.shape[0]
    nst = S // tm

    def body(p_ref, ph_ref, dy_ref, dyn_ref, wg_ref, sc_ref, dp_ref, dsc_ref, dwg_ref, ext, ext2):
        i = pl.program_id(0)
        first = (i % nst) == 0
        last = (i % nst) == nst - 1

        @pl.when(i == 0)
        def _():
            dsc_ref[...] = jnp.zeros(dsc_ref.shape, f32)
            dwg_ref[...] = jnp.zeros(dwg_ref.shape, f32)

        ext[0:HALO, :] = jnp.where(first, 0.0, ph_ref[...])
        ext[HALO:, :] = p_ref[...]
        for g in range(C_GROUPS):
            w = POOL_WINDOWS[g]
            cs = slice(g * C_GDIM, (g + 1) * C_GDIM)
            dg = _pool_diff(p_ref, ext, g, i, nst, tm).astype(bf16)
            wv = wg_ref[:, g].reshape(C_GDIM, C_GDIM)
            dyo = dy_ref[:, cs]
            dsc_ref[:, cs] += jnp.sum(dyo * _dot(dg, wv), axis=0, keepdims=True)
            dyp = (dyo * sc_ref[:, cs]).astype(bf16)
            dypn = (dyn_ref[:, cs] * sc_ref[:, cs]).astype(bf16)
            dwg_ref[g] += _dot_tn(dg, dyp)
            dd = _dot_nt(dyp, wv)
            ddn = _dot_nt(dypn, wv)
            ext2[0:tm, cs] = dd / _pool_counts(i, nst, tm, tm, 0, w)
            ext2[tm:, cs] = jnp.where(last, 0.0, ddn / _pool_counts(i, nst, tm, HALO, tm, w))
            s = -dd
            for k in range(w):
                s = s + ext2[pl.ds(k, tm), cs]
            dp_ref[:, cs] = s.astype(bf16)

    return pl.pallas_call(
        body, grid=(T // tm,),
        in_specs=[pl.BlockSpec((tm, D), lambda i: (i, 0)), _prev_halo_spec(tm, D),
                  pl.BlockSpec((tm, D), lambda i: (i, 0)), _next_halo_spec(tm, D, T),
                  pl.BlockSpec((NDEV, C_GROUPS, C_GDIM // NDEV, C_GDIM), lambda i: (0, 0, 0, 0)),
                  pl.BlockSpec((1, D), lambda i: (0, 0))],
        out_specs=[pl.BlockSpec((tm, D), lambda i: (i, 0)), pl.BlockSpec((1, D), lambda i: (0, 0)),
                   pl.BlockSpec((C_GROUPS, C_GDIM, C_GDIM), lambda i: (0, 0, 0))],
        out_shape=[jax.ShapeDtypeStruct((T, D), bf16), jax.ShapeDtypeStruct((1, D), f32),
                   jax.ShapeDtypeStruct((C_GROUPS, C_GDIM, C_GDIM), f32)],
        scratch_shapes=[pltpu.VMEM((tm + HALO, D), f32), pltpu.VMEM((tm + HALO, D), f32)],
        compiler_params=_cparams(("arbitrary",)), name=name,
    )(p, p, dy, dy, wg, scale)


def final_loss(x, g, tgt, *, tm, name):
    T = x.shape[0]

    def body(x_ref, g_ref, t_ref, loss_ref, dx_ref, dg_ref):
        @pl.when(pl.program_id(0) == 0)
        def _():
            loss_ref[...] = jnp.zeros(loss_ref.shape, f32)
            dg_ref[...] = jnp.zeros(dg_ref.shape, f32)

        xv = x_ref[...]
        r = lax.rsqrt(jnp.mean(xv * xv, axis=-1, keepdims=True) + EPS)
        xhat = xv * r
        err = xhat * g_ref[...] - t_ref[...]
        loss_ref[...] += 0.5 * jnp.sum(jnp.mean(err * err, axis=-1, keepdims=True))
        dy = err * (1.0 / D)
        dg_ref[...] += jnp.sum(dy * xhat, axis=0, keepdims=True)
        dxhat = dy * g_ref[...]
        dx_ref[...] = r * (dxhat - xhat * jnp.mean(dxhat * xhat, axis=-1, keepdims=True))

    return pl.pallas_call(
        body, grid=(T // tm,),
        in_specs=[pl.BlockSpec((tm, D), lambda i: (i, 0)), pl.BlockSpec((1, D), lambda i: (0, 0)),
                  pl.BlockSpec((tm, D), lambda i: (i, 0))],
        out_specs=[pl.BlockSpec((8, 128), lambda i: (0, 0)), pl.BlockSpec((tm, D), lambda i: (i, 0)),
                   pl.BlockSpec((1, D), lambda i: (0, 0))],
        out_shape=[jax.ShapeDtypeStruct((8, 128), f32), jax.ShapeDtypeStruct((T, D), f32),
                   jax.ShapeDtypeStruct((1, D), f32)],
        compiler_params=_cparams(("arbitrary",)), name=name,
    )(x, g, tgt)


def _slot(px, py, pc):
    return 4 * px + 2 * py + pc


def _with_own_block(s, me):
    zone = lax.empty((NDEV,) + s.shape, s.dtype)
    return lax.dynamic_update_slice(zone, s[None], (me,) + (0,) * s.ndim)


def all_gather(arrs, me, *, name):
    n = len(arrs)

    def body(*refs):
        ins, outs = refs[:n], refs[2 * n:3 * n]
        send_sems, recv_sems = refs[3 * n:]
        x, y, c = lax.axis_index("x"), lax.axis_index("y"), lax.axis_index("c")
        me, sibling = (x, y, c), (x, y, 1 - c)
        chips = [(1 - x, y), (x, 1 - y), (1 - x, 1 - y)]

        def copy(a, k, block, to, src=None):
            dst = outs[a].at[_slot(*block)]
            return pltpu.make_async_remote_copy(
                src_ref=dst if src is None else src, dst_ref=dst,
                send_sem=send_sems.at[a, k], recv_sem=recv_sems.at[a, k], device_id=to, device_id_type=MESH)

        first = []
        for a in range(n):
            first.append(copy(a, 0, me, sibling, src=ins[a]))
            first += [copy(a, 1 + j, me, (*chip, c), src=ins[a]) for j, chip in enumerate(chips)]
        for cp in first:
            cp.start()
        passed = []
        for j, chip in enumerate(chips):
            for a in range(n):
                copy(a, 1 + j, (*chip, c), me).wait_recv()
                fwd = copy(a, 4 + j, (*chip, c), sibling)
                fwd.start()
                passed.append(fwd)
        for a in range(n):
            copy(a, 0, sibling, me).wait_recv()
        for j, chip in enumerate(chips):
            for a in range(n):
                copy(a, 4 + j, (*chip, 1 - c), me).wait_recv()
        for cp in first + passed:
            cp.wait_send()

    any_spec = pl.BlockSpec(memory_space=pl.ANY)
    return pl.pallas_call(
        body,
        in_specs=[any_spec] * (2 * n), out_specs=[any_spec] * n,
        out_shape=[jax.ShapeDtypeStruct((NDEV,) + a.shape, a.dtype) for a in arrs],
        input_output_aliases={n + i: i for i in range(n)},
        scratch_shapes=[pltpu.SemaphoreType.DMA((n, 7)), pltpu.SemaphoreType.DMA((n, 7))],
        compiler_params=pltpu.CompilerParams(has_side_effects=True), name=name,
    )(*arrs, *[_with_own_block(a, me) for a in arrs])


HBM_SPEC = pl.BlockSpec(memory_space=pltpu.HBM)
SEM_SPEC = pl.BlockSpec(memory_space=pltpu.SEMAPHORE)
ANY_SPEC = pl.BlockSpec(memory_space=pl.ANY)
TOKEN_SHAPE = jax.ShapeDtypeStruct((8, 128), f32)
DATAFLOW_EFFECT = pltpu.SideEffectType.DATAFLOW_SIDE_EFFECTING


def _in_hbm(a):
    return pltpu.with_memory_space_constraint(a, pltpu.HBM)


def _hbm_like(a):
    return pltpu.HBM(a.shape, a.dtype)


def _mesh_pos():
    return lax.axis_index("x"), lax.axis_index("y"), lax.axis_index("c")


def _gather_targets(x, y, c):
    return [(x, y, 1 - c), (1 - x, y, c), (x, 1 - y, c), (1 - x, 1 - y, c)]


def gather_start(shards, me, after, *, name):
    n = len(shards)
    extra = [] if after is None else [after]

    def body(*refs):
        srcs, lands = refs[:n], refs[n:2 * n]
        send_sems, recv_sems = refs[2 * n + len(extra)], refs[2 * n + len(extra) + 1]
        token = refs[-1]
        x, y, c = _mesh_pos()
        me = _slot(x, y, c)
        for a in range(n):
            for k, to in enumerate(_gather_targets(x, y, c)):
                pltpu.make_async_remote_copy(
                    src_ref=srcs[a], dst_ref=lands[a].at[me], send_sem=send_sems.at[4 * a + k], recv_sem=recv_sems.at[4 * a + k],
                    device_id=to, device_id_type=MESH).start()
        token[...] = jnp.zeros(token.shape, f32)

    lands = [_with_own_block(s, me) for s in shards]
    sems = pltpu.SemaphoreType.DMA((4 * n,))
    out = pl.pallas_call(
        body, name=name,
        in_specs=[HBM_SPEC] * (2 * n) + [ANY_SPEC] * len(extra),
        out_specs=[SEM_SPEC, SEM_SPEC] + [HBM_SPEC] * (2 * n) + [pl.BlockSpec(memory_space=pltpu.VMEM)],
        out_shape=[sems, sems] + [_hbm_like(s) for s in shards] + [_hbm_like(l) for l in lands] + [TOKEN_SHAPE],
        input_output_aliases={i: 2 + i for i in range(2 * n)},
        compiler_params=pltpu.CompilerParams(has_side_effects=DATAFLOW_EFFECT),
    )(*[_in_hbm(s) for s in shards], *[_in_hbm(l) for l in lands], *extra)
    return out[0], out[1], out[2:2 + n], out[2 + n:2 + 2 * n], out[-1]


def gather_wait(send_sems, recv_sems, shards, lands, after, *, name):
    n = len(shards)

    def body(*refs):
        srcs, lands_in = refs[:n], refs[n:2 * n]
        send_sems, recv_sems = refs[2 * n], refs[2 * n + 1]
        x, y, c = _mesh_pos()
        for a in range(n):
            for k, frm in enumerate(_gather_targets(x, y, c)):
                cp = pltpu.make_async_remote_copy(
                    src_ref=srcs[a], dst_ref=lands_in[a].at[_slot(*frm)], send_sem=send_sems.at[4 * a + k],
                    recv_sem=recv_sems.at[4 * a + k], device_id=frm, device_id_type=MESH)
                cp.wait_send()
                cp.wait_recv()

    out = pl.pallas_call(
        body, name=name,
        in_specs=[HBM_SPEC] * (2 * n) + [SEM_SPEC, SEM_SPEC, ANY_SPEC],
        out_specs=[HBM_SPEC] * (2 * n),
        out_shape=[_hbm_like(s) for s in shards] + [_hbm_like(l) for l in lands],
        input_output_aliases={i: i for i in range(2 * n)},
        compiler_params=pltpu.CompilerParams(has_side_effects=DATAFLOW_EFFECT),
    )(*shards, *lands, send_sems, recv_sems, after)
    return out[:n], out[n:]


def gather_finish(lands, *, name):
    n = len(lands)

    def body(*refs):
        lands_in, lands_out = refs[:n], refs[n:2 * n]
        send_sems, recv_sems = refs[2 * n:]
        x, y, c = _mesh_pos()
        sibling = (x, y, 1 - c)
        chips = [(1 - x, y), (x, 1 - y), (1 - x, 1 - y)]
        sent = []
        for a in range(n):
            for j, chip in enumerate(chips):
                s = _slot(*chip, c)
                cp = pltpu.make_async_remote_copy(
                    src_ref=lands_in[a].at[s], dst_ref=lands_out[a].at[s], send_sem=send_sems.at[a, j],
                    recv_sem=recv_sems.at[a, j], device_id=sibling, device_id_type=MESH)
                cp.start()
                sent.append(cp)
        for a in range(n):
            for j, chip in enumerate(chips):
                s = _slot(*chip, 1 - c)
                pltpu.make_async_remote_copy(
                    src_ref=lands_in[a].at[s], dst_ref=lands_out[a].at[s], send_sem=send_sems.at[a, j],
                    recv_sem=recv_sems.at[a, j], device_id=sibling, device_id_type=MESH).wait_recv()
        for cp in sent:
            cp.wait_send()

    return pl.pallas_call(
        body, name=name,
        in_specs=[ANY_SPEC] * n, out_specs=[ANY_SPEC] * n,
        out_shape=[jax.ShapeDtypeStruct(l.shape, l.dtype) for l in lands],
        input_output_aliases={i: i for i in range(n)},
        scratch_shapes=[pltpu.SemaphoreType.DMA((n, 3)), pltpu.SemaphoreType.DMA((n, 3))],
        compiler_params=pltpu.CompilerParams(has_side_effects=True),
    )(*lands)


def _peer(x, y, c, r):
    return (x ^ ((r >> 2) & 1), y ^ ((r >> 1) & 1), c ^ (r & 1))


def exchange_start(items, lands, after, *, name):
    n, m = len(items), len(lands)
    extra = [] if after is None else [after]

    def body(*refs):
        gs, zones = refs[:n], refs[n:n + m]
        send_sems, recv_sems = refs[n + m + len(extra)], refs[n + m + len(extra) + 1]
        token = refs[-1]
        x, y, c = _mesh_pos()
        me = _slot(x, y, c)
        for r in (1, 2, 4, 3, 5, 6, 7):
            to = _peer(x, y, c, r)
            for a, (_, zi, l) in enumerate(items):
                pltpu.make_async_remote_copy(
                    src_ref=gs[a].at[_slot(*to)], dst_ref=zones[zi].at[me, l], send_sem=send_sems.at[7 * a + r - 1],
                    recv_sem=recv_sems.at[7 * a + r - 1], device_id=to, device_id_type=MESH).start()
        token[...] = jnp.zeros(token.shape, f32)

    gs = [g for g, _, _ in items]
    sems = pltpu.SemaphoreType.DMA((7 * n,))
    out = pl.pallas_call(
        body, name=name,
        in_specs=[HBM_SPEC] * (n + m) + [ANY_SPEC] * len(extra),
        out_specs=[SEM_SPEC, SEM_SPEC] + [HBM_SPEC] * (n + m) + [pl.BlockSpec(memory_space=pltpu.VMEM)],
        out_shape=[sems, sems] + [_hbm_like(g) for g in gs] + [_hbm_like(z) for z in lands] + [TOKEN_SHAPE],
        input_output_aliases={i: 2 + i for i in range(n + m)},
        compiler_params=pltpu.CompilerParams(has_side_effects=DATAFLOW_EFFECT),
    )(*[_in_hbm(g) for g in gs], *[_in_hbm(z) for z in lands], *extra)
    return out[0], out[1], out[2:2 + n], out[2 + n:2 + n + m], out[-1]


def exchange_wait(parts, lands, after, *, name):
    m = len(lands)
    flat_gs = [g for _, _, _, gs in parts for g in gs]
    ng = len(flat_gs)

    def body(*refs):
        gs, zones = refs[:ng], refs[ng:ng + m]
        sem_refs = refs[ng + m:ng + m + 2 * len(parts)]
        x, y, c = _mesh_pos()
        base = 0
        for pi, (items, _, _, _) in enumerate(parts):
            send_sems, recv_sems = sem_refs[2 * pi], sem_refs[2 * pi + 1]
            for r in range(1, NDEV):
                frm = _peer(x, y, c, r)
                for a, (_, zi, l) in enumerate(items):
                    cp = pltpu.make_async_remote_copy(
                        src_ref=gs[base + a].at[_slot(*frm)], dst_ref=zones[zi].at[_slot(*frm), l],
                        send_sem=send_sems.at[7 * a + r - 1], recv_sem=recv_sems.at[7 * a + r - 1],
                        device_id=frm, device_id_type=MESH)
                    cp.wait_send()
                    cp.wait_recv()
            base += len(items)

    sem_args = [s for _, ss, rs, _ in parts for s in (ss, rs)]
    out = pl.pallas_call(
        body, name=name,
        in_specs=[HBM_SPEC] * (ng + m) + [SEM_SPEC] * len(sem_args) + [ANY_SPEC],
        out_specs=[HBM_SPEC] * (ng + m),
        out_shape=[_hbm_like(g) for g in flat_gs] + [_hbm_like(z) for z in lands],
        input_output_aliases={i: i for i in range(ng + m)},
        compiler_params=pltpu.CompilerParams(has_side_effects=DATAFLOW_EFFECT),
    )(*flat_gs, *lands, *sem_args, after)
    return out[ng:]


def sum8(g, *, name):
    _, R, C = g.shape
    tr = R

    def body(g_ref, o_ref):
        s = g_ref[0]
        for k in range(1, NDEV):
            s = s + g_ref[k]
        o_ref[...] = s

    return pl.pallas_call(
        body, grid=(R // tr,),
        in_specs=[pl.BlockSpec((NDEV, tr, C), lambda i: (0, i, 0))],
        out_specs=pl.BlockSpec((tr, C), lambda i: (i, 0)),
        out_shape=jax.ShapeDtypeStruct((R, C), f32),
        compiler_params=_cparams(("parallel",)), name=name,
    )(g)


def _adam_math(w, g, m, v):
    m = ADAM_B1 * m + (1.0 - ADAM_B1) * g
    v = ADAM_B2 * v + (1.0 - ADAM_B2) * (g * g)
    m_hat = m / (1.0 - ADAM_B1 ** ADAM_STEP)
    v_hat = v / (1.0 - ADAM_B2 ** ADAM_STEP)
    delta = -ADAM_LR * (m_hat / (jnp.sqrt(v_hat) + ADAM_EPS) + ADAM_WD * w)
    return delta, m, v


def adamw_slices(w, m, v, recv, *, row0=0, col0=0, tr=None, tc=None, name):
    L, R, C = w.shape
    Rp, Cp = recv.shape[2], recv.shape[3]
    tr = R if tr is None else tr
    tc = C if tc is None else tc
    assert R % tr == 0 and C % tc == 0
    assert (tr == R or (Rp == R and row0 == 0)) and (tc == C or (Cp == C and col0 == 0))
    rr = Rp if tr == R else tr
    rc = Cp if tc == C else tc

    def body(w_ref, m_ref, v_ref, r_ref, g_ref, d_ref, nm_ref, nv_ref):
        c0 = col0 if tc == C else 0
        r0 = row0 if tr == R else 0
        g = r_ref[0, r0:r0 + tr, c0:c0 + tc].astype(f32)
        for k in range(1, NDEV):
            g = g + r_ref[k, r0:r0 + tr, c0:c0 + tc].astype(f32)
        delta, nm, nv = _adam_math(w_ref[...], g, m_ref[...], v_ref[...])
        g_ref[...] = g
        d_ref[...] = delta
        nm_ref[...] = nm
        nv_ref[...] = nv

    wspec = pl.BlockSpec((None, tr, tc), lambda l, i, j: (l, i, j))
    return pl.pallas_call(
        body, grid=(L, R // tr, C // tc),
        in_specs=[wspec, wspec, wspec, pl.BlockSpec((NDEV, None, rr, rc), lambda l, i, j: (0, l, i, j))],
        out_specs=[wspec] * 4, out_shape=[jax.ShapeDtypeStruct((L, R, C), f32)] * 4,
        compiler_params=_cparams(("parallel", "parallel", "parallel")), name=name,
    )(w, m, v, recv)


def adamw_plain(w, m, v, g, *, name):
    R, C = w.shape

    def body(w_ref, m_ref, v_ref, g_ref, d_ref, nm_ref, nv_ref):
        delta, nm, nv = _adam_math(w_ref[...], g_ref[...], m_ref[...], v_ref[...])
        d_ref[...] = delta
        nm_ref[...] = nm
        nv_ref[...] = nv

    return pl.pallas_call(
        body, out_shape=[jax.ShapeDtypeStruct((R, C), f32)] * 3, name=name,
    )(w, m, v, g)


def _chunk_causal_mask():
    pos = jnp.arange(GBLK)
    return (pos[None, :] // CHUNK) <= (pos[:, None] // CHUNK)


def kernel(x, norm_mix_g, norm_ffn_g, final_norm_g, a_w_in, a_v_norm_g, a_w_s, a_b_s, a_w_out, b_w_in, b_conv_w, b_w_out, c_w_in, c_w_grp, c_scale, c_w_out, f_w_gate, f_w_up, f_w_down, loss_target, m_norm_mix_g, m_norm_ffn_g, m_final_norm_g, m_a_w_in, m_a_v_norm_g, m_a_w_s, m_a_b_s, m_a_w_out, m_b_w_in, m_b_conv_w, m_b_w_out, m_c_w_in, m_c_w_grp, m_c_scale, m_c_w_out, m_f_w_gate, m_f_w_up, m_f_w_down, v_norm_mix_g, v_norm_ffn_g, v_final_norm_g, v_a_w_in, v_a_v_norm_g, v_a_w_s, v_a_b_s, v_a_w_out, v_b_w_in, v_b_conv_w, v_b_w_out, v_c_w_in, v_c_w_grp, v_c_scale, v_c_w_out, v_f_w_gate, v_f_w_up, v_f_w_down):
    B, S, _ = x.shape
    T = B * S
    tm = min(512, S)
    tmm = min(1024, T)
    tms = min(512, T)
    me = _slot(lax.axis_index("x"), lax.axis_index("y"), lax.axis_index("c"))

    pad_c = ((0, 0), (0, 0), (0, F_PAD - F_LOC))
    f_gu = jnp.concatenate([jnp.pad(f_w_gate.astype(bf16), pad_c), jnp.pad(f_w_up.astype(bf16), pad_c)], axis=2)
    f_d = jnp.pad(f_w_down.astype(bf16), ((0, 0), (0, F_PAD - F_LOC), (0, 0)))
    small = jnp.concatenate([a_v_norm_g, b_conv_w[0], c_scale, jnp.zeros((2, GBLK), f32)], axis=0)
    a_in_b, a_out_b = a_w_in.astype(bf16), a_w_out.astype(bf16)
    stage_shards = [
        [a_in_b[0], a_out_b[0], small],
        [f_gu[0], f_d[0]],
        [b_w_in[0].astype(bf16), b_w_out[0].astype(bf16), f_gu[1], f_d[1]],
        [c_w_in[0].astype(bf16), c_w_grp[0].astype(bf16), c_w_out[0].astype(bf16), f_gu[2], f_d[2]],
        [a_in_b[1], a_out_b[1], f_gu[3], f_d[3]],
    ]
    started, tok = [], None
    for s in range(len(stage_shards)):
        started.append(gather_start(stage_shards[s], me, tok, name=f"gather_start_{s}"))
        tok = started[-1][4]

    def stage_weights(s, after):
        send_sems, recv_sems, shards, lands, _ = started[s]
        _, lands = gather_wait(send_sems, recv_sems, shards, lands, after, name=f"gather_wait_{s}")
        return gather_finish(lands, name=f"gather_finish_{s}")

    wm_masked = jnp.where(_chunk_causal_mask()[None, None], a_w_s, 0.0).astype(bf16)
    bsb = jnp.broadcast_to(a_b_s[:, :, :, None], a_b_s.shape + (GBLK,))

    xc = x.reshape(T, D)
    saved, weights = [], []
    for i in range(DEPTH):
        kind, j = i % 3, i // 3
        gmix = norm_mix_g[i][None]
        if i == 0:
            w_in, w_out, small_g = stage_weights(0, tok)
            small_full = jnp.transpose(small_g, (1, 0, 2)).reshape(8, D)
            gv_full, cw_full, cs_full = small_full[0:2], small_full[2:5], small_full[5:6]
            w = [w_in, w_out]
        else:
            w = stage_weights(i + 1, xc)
        if kind == 0:
            w_in, w_out = w[:2]
            p, h = norm_mm(xc, gmix, w_in, tm=tmm, name=f"a_in_{i}")
            y = a_mid_fwd(p, gv_full[j][None], wm_masked[j], bsb[j], tm=tm, name=f"a_mid_{i}")
        elif kind == 1:
            w_in, w_out = w[:2]
            p, h = norm_mm(xc, gmix, w_in, tm=tmm, name=f"b_in_{i}")
            y = b_mid_fwd(p, cw_full, tm=tm, S=S, name=f"b_mid_{i}")
        else:
            w_in, w_grp, w_out = w[:3]
            p, h = norm_mm(xc, gmix, w_in, tm=tmm, row_sharded=True, name=f"c_in_{i}")
            y = c_mid_fwd(p, w_grp, cs_full, tm=tm, S=S, name=f"c_mid_{i}")
        x1 = mm_res(y, w_out, xc, tm=tms, name=f"mix_out_{i}")
        if i == 0:
            w = w + list(stage_weights(1, x1))
        weights.append(w)
        w_fgu, w_fd = w[-2], w[-1]
        pf, h2, a = norm_mm(x1, norm_ffn_g[i][None], w_fgu, tm=tmm, swiglu=True, name=f"f_in_{i}")
        x2 = mm_res(a, w_fd, x1, tm=tms, name=f"f_out_{i}")
        saved.append((xc, p, h, y, x1, pf, h2, a))
        xc = x2

    loss_blk, dx, d_final = final_loss(xc, final_norm_g[None], loss_target.reshape(T, D), tm=tms, name="final_loss")
    loss = lax.psum(loss_blk[0, 0], AXES)

    zone_shapes = [(2, D, 256), (2, GBLK, D), (1, D, 384), (1, GBLK, D), (1, GBLK, D), (1, GBLK, C_GDIM), (1, GBLK, D),
                   (DEPTH, 2 * F_PAD, D), (DEPTH, F_PAD, D)]
    Z_AIN, Z_AOUT, Z_BIN, Z_BOUT, Z_CIN, Z_CGRP, Z_COUT, Z_FGU, Z_FD = range(9)
    lands = [lax.empty((NDEV,) + s, bf16) for s in zone_shapes]
    parts = []

    def exchange(items, after, name):
        nonlocal lands
        for g, zi, l in items:
            own = lax.dynamic_index_in_dim(g, me, axis=0, keepdims=True)[None]
            lands[zi] = lax.dynamic_update_slice(lands[zi], own, (me, l) + (0,) * (g.ndim - 1))
        idx = sorted({zi for _, zi, _ in items})
        local = [(g, idx.index(zi), l) for g, zi, l in items]
        send_sems, recv_sems, gs, zs, token = exchange_start(local, [lands[zi] for zi in idx], after, name=name)
        for q, zi in enumerate(idx):
            lands[zi] = zs[q]
        parts.append(([(None, zi, l) for _, zi, l in items], send_sems, recv_sems, gs))
        return token

    d_mix, d_ffn = [None] * DEPTH, [None] * DEPTH
    d_gv, d_wm, d_bs = [None] * 2, [None] * 2, [None] * 2
    tok = None
    for i in reversed(range(DEPTH)):
        kind, j = i % 3, i // 3
        xin, p, h, y, x1, pf, h2, a = saved[i]
        w = weights[i]
        w_in, w_out, w_fgu, w_fd = w[0], w[-3], w[-2], w[-1]
        gffn = norm_ffn_g[i][None] if tok is None else norm_ffn_g[i][None] + tok[0:1, 0:1]
        dpf, dxb = ffn_bwd_mid(dx, w_fd, pf, tm=tmm, name=f"f_bwd_mid_{i}")
        dx1, d_ffn[i] = mm_nt_normbwd(dpf, w_fgu, x1, gffn, dx, tm=tms, name=f"f_bwd_in_{i}")
        items = [(mm_tn(dpf, h2, tm=tmm, row_sharded=True, name=f"f_dw_in_{i}"), Z_FGU, i),
                 (mm_tn(a, dxb, tm=tmm, row_sharded=True, name=f"f_dw_out_{i}"), Z_FD, i)]
        gmix = norm_mix_g[i][None]
        if i == 0:
            gmix = gmix + exchange(items, tok, "exchange_start_ffn0")[0:1, 0:1]
            items = []
        dy, dx1b = mm_nt(dx1, w_out, tm=tmm, name=f"mix_bwd_out_{i}")
        if kind == 0:
            dp, d_gv[j], d_wm[j], d_bs[j] = a_mid_bwd(p, dy, gv_full[j][None], wm_masked[j], bsb[j], tm=tm, name=f"a_bwd_mid_{i}")
            dx, d_mix[i] = mm_nt_normbwd(dp, w_in, xin, gmix, dx1, tm=tms, name=f"a_bwd_in_{i}")
            items += [(mm_tn(h, dp, tm=tmm, row_sharded=False, name=f"a_dw_in_{i}"), Z_AIN, j),
                      (mm_tn(y, dx1b, tm=tmm, row_sharded=True, name=f"a_dw_out_{i}"), Z_AOUT, j)]
        elif kind == 1:
            dp, d_cw = b_mid_bwd(p, dy, cw_full, tm=tm, S=S, name=f"b_bwd_mid_{i}")
            dx, d_mix[i] = mm_nt_normbwd(dp, w_in, xin, gmix, dx1, tm=tms, name=f"b_bwd_in_{i}")
            items += [(mm_tn(h, dp, tm=tmm, row_sharded=False, name=f"b_dw_in_{i}"), Z_BIN, 0),
                      (mm_tn(y, dx1b, tm=tmm, row_sharded=True, name=f"b_dw_out_{i}"), Z_BOUT, 0)]
        else:
            dp, d_cs, d_wgrp = c_mid_bwd(p, dy, w[1], cs_full, tm=tm, S=S, name=f"c_bwd_mid_{i}")
            dx, d_mix[i] = mm_nt_normbwd(dp, w_in, xin, gmix, dx1, tm=tms, row_sharded=True, name=f"c_bwd_in_{i}")
            gw_cgrp = jnp.transpose(d_wgrp.reshape(C_GROUPS, NDEV, C_GDIM // NDEV, C_GDIM), (1, 0, 2, 3)).astype(bf16)
            items += [(mm_tn(h, dp, tm=tmm, row_sharded=True, name=f"c_dw_in_{i}"), Z_CIN, 0),
                      (gw_cgrp.reshape(NDEV, GBLK, C_GDIM), Z_CGRP, 0),
                      (mm_tn(y, dx1b, tm=tmm, row_sharded=True, name=f"c_dw_out_{i}"), Z_COUT, 0)]
        tok = exchange(items, tok, f"exchange_start_{i}")
    grad_x = dx.reshape(B, S, D)

    lands = exchange_wait(parts, lands, dx, name="exchange_wait")
    r_ain, r_aout, r_bin, r_bout, r_cin, r_cgrp, r_cout, r_fgu, r_fd = lands

    mask = _chunk_causal_mask()
    small_parts = [
        jnp.concatenate(d_mix, axis=0).reshape(-1, GBLK),
        jnp.concatenate(d_ffn, axis=0).reshape(-1, GBLK),
        d_final.reshape(-1, GBLK),
        jnp.concatenate(d_gv, axis=0).reshape(-1, GBLK),
        jnp.where(mask[None, None], jnp.stack(d_wm), 0.0).reshape(-1, GBLK),
        jnp.stack(d_bs).reshape(-1, GBLK),
        d_cw.reshape(-1, GBLK),
        d_cs.reshape(-1, GBLK),
    ]
    small_rows = [q.shape[0] for q in small_parts]
    (gs_all,) = all_gather([jnp.concatenate(small_parts, axis=0)], me, name="gather_small_grads")
    gs = sum8(gs_all, name="sum_small_grads")
    offs = [0]
    for r in small_rows:
        offs.append(offs[-1] + r)
    sp = [gs[offs[q]:offs[q + 1]] for q in range(len(small_parts))]
    grad_norm_mix_g = sp[0].reshape(DEPTH, D)
    grad_norm_ffn_g = sp[1].reshape(DEPTH, D)
    grad_final_norm_g = sp[2].reshape(D)
    grad_a_w_s = sp[4].reshape(a_w_s.shape)
    grad_a_b_s = sp[5].reshape(a_b_s.shape)

    def my_cols(full):
        return lax.dynamic_slice_in_dim(full, me * GBLK, GBLK, axis=1)

    grad_a_v_norm_g = my_cols(sp[3].reshape(2, D))
    grad_b_conv_w = my_cols(sp[6].reshape(3, D))[None]
    grad_c_scale = my_cols(sp[7].reshape(1, D))

    def plain(w, m, v, g, name):
        shp = w.shape
        w2, m2, v2, g2 = (t.reshape(-1, shp[-1]) for t in (w, m, v, g))
        return tuple(t.reshape(shp) for t in adamw_plain(w2, m2, v2, g2, name=name))

    def sliced(w, m, v, recv, name, **kw):
        shp = w.shape
        w3, m3, v3 = (t.reshape((shp[0], -1, shp[-1])) for t in (w, m, v))
        return tuple(t.reshape(shp) for t in adamw_slices(w3, m3, v3, recv, name=name, **kw))

    res = {}
    res["norm_mix_g"] = (grad_norm_mix_g,) + plain(norm_mix_g, m_norm_mix_g, v_norm_mix_g, grad_norm_mix_g, "adam_norm_mix")
    res["norm_ffn_g"] = (grad_norm_ffn_g,) + plain(norm_ffn_g, m_norm_ffn_g, v_norm_ffn_g, grad_norm_ffn_g, "adam_norm_ffn")
    res["final_norm_g"] = (grad_final_norm_g,) + tuple(
        t.reshape(D) for t in plain(final_norm_g[None], m_final_norm_g[None], v_final_norm_g[None], grad_final_norm_g[None], "adam_final"))
    res["a_w_in"] = sliced(a_w_in, m_a_w_in, v_a_w_in, r_ain, "adam_a_w_in", tr=256)
    res["a_v_norm_g"] = (grad_a_v_norm_g,) + plain(a_v_norm_g, m_a_v_norm_g, v_a_v_norm_g, grad_a_v_norm_g, "adam_a_v_norm")
    res["a_w_s"] = (grad_a_w_s,) + plain(a_w_s, m_a_w_s, v_a_w_s, grad_a_w_s, "adam_a_w_s")
    res["a_b_s"] = (grad_a_b_s,) + plain(a_b_s, m_a_b_s, v_a_b_s, grad_a_b_s, "adam_a_b_s")
    res["a_w_out"] = sliced(a_w_out, m_a_w_out, v_a_w_out, r_aout, "adam_a_w_out")
    res["b_w_in"] = sliced(b_w_in, m_b_w_in, v_b_w_in, r_bin, "adam_b_w_in", tr=256)
    res["b_conv_w"] = (grad_b_conv_w,) + plain(b_conv_w, m_b_conv_w, v_b_conv_w, grad_b_conv_w, "adam_b_conv")
    res["b_w_out"] = sliced(b_w_out, m_b_w_out, v_b_w_out, r_bout, "adam_b_w_out")
    res["c_w_in"] = sliced(c_w_in, m_c_w_in, v_c_w_in, r_cin, "adam_c_w_in")
    res["c_w_grp"] = sliced(c_w_grp, m_c_w_grp, v_c_w_grp, r_cgrp, "adam_c_w_grp")
    res["c_scale"] = (grad_c_scale,) + plain(c_scale, m_c_scale, v_c_scale, grad_c_scale, "adam_c_scale")
    res["c_w_out"] = sliced(c_w_out, m_c_w_out, v_c_w_out, r_cout, "adam_c_w_out")
    def sliced_t(w, m, v, recv, name, **kw):
        wt, mt, vt = (jnp.transpose(t, (0, 2, 1)) for t in (w, m, v))
        return tuple(jnp.transpose(t, (0, 2, 1)) for t in adamw_slices(wt, mt, vt, recv, name=name, **kw))

    res["f_w_gate"] = sliced_t(f_w_gate, m_f_w_gate, v_f_w_gate, r_fgu, "adam_f_gate", row0=0, tc=256)
    res["f_w_up"] = sliced_t(f_w_up, m_f_w_up, v_f_w_up, r_fgu, "adam_f_up", row0=F_PAD, tc=256)
    res["f_w_down"] = sliced(f_w_down, m_f_w_down, v_f_w_down, r_fd, "adam_f_down", tc=256)

    order = ["norm_mix_g", "norm_ffn_g", "final_norm_g", "a_w_in", "a_v_norm_g", "a_w_s", "a_b_s", "a_w_out", "b_w_in",
             "b_conv_w", "b_w_out", "c_w_in", "c_w_grp", "c_scale", "c_w_out", "f_w_gate", "f_w_up", "f_w_down"]
    return (loss, grad_x, *[res[n][0] for n in order], *[res[n][1] for n in order],
            *[res[n][2] for n in order], *[res[n][3] for n in order])
```

```python
import functools

import jax
import jax.numpy as jnp
from jax import lax
from jax.experimental import pallas as pl
from jax.experimental.pallas import tpu as pltpu

f32 = jnp.float32
bf16 = jnp.bfloat16

NDEV = 8
D = 1024
EPS = 1e-6
CHUNK = 64
GBLK = 128
A_GROUPS = 8
C_GROUPS = 4
C_GDIM = 256
POOL_WINDOWS = (2, 4, 8, 16)
HALO = 16
F_LOC = 352
F_PAD = 384
DEPTH = 4
AXES = ("x", "y", "c")
MESH = pl.DeviceIdType.MESH

ADAM_LR = 0.001
ADAM_B1 = 0.9
ADAM_B2 = 0.999
ADAM_EPS = 1e-08
ADAM_WD = 0.01
ADAM_STEP = 10

VMEM_LIMIT = 56 * 1024 * 1024


def _cparams(sem):
    return pltpu.CompilerParams(dimension_semantics=sem, vmem_limit_bytes=VMEM_LIMIT)


def _gelu(z):
    return 0.5 * z * (1.0 + lax.erf(z * 0.7071067811865476))


def _gelu_grad(z):
    return 0.5 * (1.0 + lax.erf(z * 0.7071067811865476)) + z * jnp.exp(-0.5 * z * z) * 0.3989422804014327


def _dot(a, b):
    return jnp.dot(a, b, preferred_element_type=f32)


def _dot_nt(a, b):
    return lax.dot_general(a, b, (((1,), (1,)), ((), ())), preferred_element_type=f32)


def _dot_tn(a, b):
    return lax.dot_general(a, b, (((0,), (0,)), ((), ())), preferred_element_type=f32)


def norm_mm(x, g, w, *, tm, transposed, swiglu=False, name):
    T = x.shape[0]
    if swiglu:
        nchunk, n = NDEV, w.shape[1]
        w_spec = pl.BlockSpec((None, n, D), lambda i, d: (d, 0, 0))
    else:
        nchunk, n = 1, (NDEV * w.shape[1] if transposed else w.shape[2])
        w_spec = pl.BlockSpec(w.shape, lambda i, d: (0, 0, 0))

    def body(x_ref, g_ref, w_ref, p_ref, h_ref, *rest):
        @pl.when(pl.program_id(1) == 0)
        def _():
            xv = x_ref[...]
            r = lax.rsqrt(jnp.mean(xv * xv, axis=-1, keepdims=True) + EPS)
            h_ref[...] = (xv * r * g_ref[...]).astype(bf16)

        if swiglu:
            p = _dot_nt(h_ref[...], w_ref[...])
            gate, up = p[:, : n // 2], p[:, n // 2:]
            rest[0][...] = (gate * jax.nn.sigmoid(gate) * up).astype(bf16)
        elif transposed:
            p = _dot_nt(h_ref[...], w_ref[...].reshape(n, D))
        else:
            p = _dot(h_ref[...], w_ref[...].reshape(D, n))
        p_ref[...] = p.astype(p_ref.dtype)

    out_shape = [jax.ShapeDtypeStruct((T, nchunk * n), bf16 if swiglu else f32), jax.ShapeDtypeStruct((T, D), bf16)]
    out_specs = [pl.BlockSpec((tm, n), lambda i, d: (i, d)), pl.BlockSpec((tm, D), lambda i, d: (i, 0))]
    if swiglu:
        out_shape.append(jax.ShapeDtypeStruct((T, nchunk * n // 2), bf16))
        out_specs.append(pl.BlockSpec((tm, n // 2), lambda i, d: (i, d)))
    return pl.pallas_call(
        body, grid=(T // tm, nchunk),
        in_specs=[pl.BlockSpec((tm, D), lambda i, d: (i, 0)), pl.BlockSpec((1, D), lambda i, d: (0, 0)), w_spec],
        out_specs=out_specs, out_shape=out_shape,
        compiler_params=_cparams(("parallel", "arbitrary")), name=name,
    )(x, g, w)


def mm_res(y, w, x, *, tm, name):
    T, K = y.shape
    k = w.shape[1]

    def body(y_ref, w_ref, x_ref, o_ref):
        o_ref[...] = x_ref[...] + _dot(y_ref[...], w_ref[...].reshape(K, D))

    return pl.pallas_call(
        body, grid=(T // tm,),
        in_specs=[pl.BlockSpec((tm, K), lambda i: (i, 0)),
                  pl.BlockSpec((NDEV, k, D), lambda i: (0, 0, 0)),
                  pl.BlockSpec((tm, D), lambda i: (i, 0))],
        out_specs=pl.BlockSpec((tm, D), lambda i: (i, 0)),
        out_shape=jax.ShapeDtypeStruct((T, D), f32),
        compiler_params=_cparams(("parallel",)), name=name,
    )(y, w, x)


def mm_nt(dy, w, *, tm, name):
    T = dy.shape[0]
    k = w.shape[1]
    K = NDEV * k

    def body(dy_ref, w_ref, o_ref, dyb_ref):
        dyb = dy_ref[...].astype(bf16)
        dyb_ref[...] = dyb
        o_ref[...] = _dot_nt(dyb, w_ref[...].reshape(K, D))

    return pl.pallas_call(
        body, grid=(T // tm,),
        in_specs=[pl.BlockSpec((tm, D), lambda i: (i, 0)),
                  pl.BlockSpec((NDEV, k, D), lambda i: (0, 0, 0))],
        out_specs=[pl.BlockSpec((tm, K), lambda i: (i, 0)), pl.BlockSpec((tm, D), lambda i: (i, 0))],
        out_shape=[jax.ShapeDtypeStruct((T, K), f32), jax.ShapeDtypeStruct((T, D), bf16)],
        compiler_params=_cparams(("parallel",)), name=name,
    )(dy, w)


def ffn_bwd_mid(dx, wd, pf, *, tm, name):
    T = dx.shape[0]

    def body(dx_ref, w_ref, p_ref, dp_ref, dxb_ref):
        @pl.when(pl.program_id(1) == 0)
        def _():
            dxb_ref[...] = dx_ref[...].astype(bf16)

        da = _dot_nt(dxb_ref[...], w_ref[...])
        gate, up = p_ref[:, :F_PAD].astype(f32), p_ref[:, F_PAD:].astype(f32)
        sg = jax.nn.sigmoid(gate)
        dp_ref[:, :F_PAD] = (da * up * (sg * (1.0 + gate * (1.0 - sg)))).astype(bf16)
        dp_ref[:, F_PAD:] = (da * gate * sg).astype(bf16)

    return pl.pallas_call(
        body, grid=(T // tm, NDEV),
        in_specs=[pl.BlockSpec((tm, D), lambda i, d: (i, 0)),
                  pl.BlockSpec((None, F_PAD, D), lambda i, d: (d, 0, 0)),
                  pl.BlockSpec((tm, 2 * F_PAD), lambda i, d: (i, d))],
        out_specs=[pl.BlockSpec((tm, 2 * F_PAD), lambda i, d: (i, d)), pl.BlockSpec((tm, D), lambda i, d: (i, 0))],
        out_shape=[jax.ShapeDtypeStruct(pf.shape, bf16), jax.ShapeDtypeStruct((T, D), bf16)],
        compiler_params=_cparams(("parallel", "arbitrary")), name=name,
    )(dx, wd, pf)


def mm_normbwd(dp, w, x, g, dres, *, tm, transposed, name):
    T, N = dp.shape

    def body(dp_ref, w_ref, x_ref, g_ref, dres_ref, dx_ref, dg_ref):
        @pl.when(pl.program_id(0) == 0)
        def _():
            dg_ref[...] = jnp.zeros(dg_ref.shape, f32)

        if transposed:
            dh = _dot(dp_ref[...], w_ref[...].reshape(N, D))
        else:
            dh = _dot_nt(dp_ref[...], w_ref[...].reshape(D, N))
        xv = x_ref[...]
        r = lax.rsqrt(jnp.mean(xv * xv, axis=-1, keepdims=True) + EPS)
        xhat = xv * r
        dg_ref[...] += jnp.sum(dh * xhat, axis=0, keepdims=True)
        dxhat = dh * g_ref[...]
        dx_ref[...] = dres_ref[...] + r * (dxhat - xhat * jnp.mean(dxhat * xhat, axis=-1, keepdims=True))

    return pl.pallas_call(
        body, grid=(T // tm,),
        in_specs=[pl.BlockSpec((tm, N), lambda i: (i, 0)),
                  pl.BlockSpec(w.shape, lambda i: (0, 0, 0), pipeline_mode=pl.Buffered(1)),
                  pl.BlockSpec((tm, D), lambda i: (i, 0)), pl.BlockSpec((1, D), lambda i: (0, 0)),
                  pl.BlockSpec((tm, D), lambda i: (i, 0))],
        out_specs=[pl.BlockSpec((tm, D), lambda i: (i, 0)), pl.BlockSpec((1, D), lambda i: (0, 0))],
        out_shape=[jax.ShapeDtypeStruct((T, D), f32), jax.ShapeDtypeStruct((1, D), f32)],
        compiler_params=_cparams(("arbitrary",)), name=name,
    )(dp, w, x, g, dres)


def mm_tn(a, b, *, tm, row_sharded, name):
    T = a.shape[0]
    if row_sharded:
        k, n = a.shape[1] // NDEV, b.shape[1]
        a_spec = pl.BlockSpec((tm, k), lambda d, t: (t, d))
        b_spec = pl.BlockSpec((tm, n), lambda d, t: (t, 0))
    else:
        k, n = a.shape[1], b.shape[1] // NDEV
        a_spec = pl.BlockSpec((tm, k), lambda d, t: (t, 0))
        b_spec = pl.BlockSpec((tm, n), lambda d, t: (t, d))
    nt = T // tm

    def body(a_ref, b_ref, o_ref, acc):
        t = pl.program_id(1)
        part = _dot_tn(a_ref[...], b_ref[...])

        @pl.when(t == 0)
        def _():
            acc[...] = part

        @pl.when(t > 0)
        def _():
            acc[...] += part

        @pl.when(t == nt - 1)
        def _():
            o_ref[...] = acc[...].astype(bf16)

    return pl.pallas_call(
        body, grid=(NDEV, nt), in_specs=[a_spec, b_spec],
        out_specs=pl.BlockSpec((None, k, n), lambda d, t: (d, 0, 0)),
        out_shape=jax.ShapeDtypeStruct((NDEV, k, n), bf16),
        scratch_shapes=[pltpu.VMEM((k, n), f32)],
        compiler_params=_cparams(("parallel", "arbitrary")), name=name,
    )(a, b)


def _prev_halo_spec(tm, ncol):
    return pl.BlockSpec((HALO, ncol), lambda i: (jnp.maximum(i * (tm // HALO) - 1, 0), 0))


def _next_halo_spec(tm, ncol, T):
    return pl.BlockSpec((HALO, ncol), lambda i: (jnp.minimum((i + 1) * (tm // HALO), T // HALO - 1), 0))


def a_mid_fwd(p, gv, wm, bsb, *, tm, name):
    T = p.shape[0]

    def body(p_ref, gv_ref, wm_ref, bs_ref, y_ref, vn_sc):
        v = _gelu(p_ref[:, D:])
        vc = v - jnp.mean(v, axis=-1, keepdims=True)
        var = jnp.mean(vc * vc, axis=-1, keepdims=True)
        vn_sc[...] = (vc * lax.rsqrt(var + EPS) * gv_ref[...]).astype(bf16)
        for g in range(A_GROUPS):
            cs = slice(g * GBLK, (g + 1) * GBLK)
            for r in range(tm // GBLK):
                rs = slice(r * GBLK, (r + 1) * GBLK)
                sv = _dot(wm_ref[g], vn_sc[rs, cs]) + bs_ref[g]
                y_ref[rs, cs] = (_gelu(p_ref[rs, cs]) * sv).astype(bf16)

    return pl.pallas_call(
        body, grid=(T // tm,),
        in_specs=[pl.BlockSpec((tm, 2 * D), lambda i: (i, 0)), pl.BlockSpec((1, D), lambda i: (0, 0)),
                  pl.BlockSpec((A_GROUPS, GBLK, GBLK), lambda i: (0, 0, 0)),
                  pl.BlockSpec((A_GROUPS, GBLK, GBLK), lambda i: (0, 0, 0))],
        out_specs=pl.BlockSpec((tm, D), lambda i: (i, 0)),
        out_shape=jax.ShapeDtypeStruct((T, D), bf16),
        scratch_shapes=[pltpu.VMEM((tm, D), bf16)],
        compiler_params=_cparams(("parallel",)), name=name,
    )(p, gv, wm, bsb)


def a_mid_bwd(p, dy, gv, wm, bsb, *, tm, name):
    T = p.shape[0]

    def body(p_ref, dy_ref, gv_ref, wm_ref, bs_ref, dp_ref, dgv_ref, dwm_ref, dbs_ref, vn_sc, dvn_sc):
        @pl.when(pl.program_id(0) == 0)
        def _():
            dgv_ref[...] = jnp.zeros(dgv_ref.shape, f32)
            dwm_ref[...] = jnp.zeros(dwm_ref.shape, f32)
            dbs_ref[...] = jnp.zeros(dbs_ref.shape, f32)

        zv = p_ref[:, D:]
        v = _gelu(zv)
        vc = v - jnp.mean(v, axis=-1, keepdims=True)
        rstd = lax.rsqrt(jnp.mean(vc * vc, axis=-1, keepdims=True) + EPS)
        vhat = vc * rstd
        vn_sc[...] = (vhat * gv_ref[...]).astype(bf16)
        for g in range(A_GROUPS):
            cs = slice(g * GBLK, (g + 1) * GBLK)
            dwm = jnp.zeros((GBLK, GBLK), f32)
            dbs = jnp.zeros((GBLK, 1), f32)
            for r in range(tm // GBLK):
                rs = slice(r * GBLK, (r + 1) * GBLK)
                zu = p_ref[rs, cs]
                vn = vn_sc[rs, cs]
                sv = _dot(wm_ref[g], vn) + bs_ref[g]
                dyb = dy_ref[rs, cs]
                dsv = dyb * _gelu(zu)
                dsvb = dsv.astype(bf16)
                dp_ref[rs, cs] = (dyb * sv * _gelu_grad(zu)).astype(bf16)
                dwm += _dot_nt(dsvb, vn)
                dbs += jnp.sum(dsv, axis=1, keepdims=True)
                dvn_sc[rs, cs] = _dot_tn(wm_ref[g], dsvb)
            dwm_ref[g] += dwm
            dbs_ref[g] += dbs
        dvn = dvn_sc[...]
        dgv_ref[...] += jnp.sum(dvn * vhat, axis=0, keepdims=True)
        dvhat = dvn * gv_ref[...]
        dv = rstd * (dvhat - jnp.mean(dvhat, axis=-1, keepdims=True)
                     - vhat * jnp.mean(dvhat * vhat, axis=-1, keepdims=True))
        dp_ref[:, D:] = (dv * _gelu_grad(zv)).astype(bf16)

    return pl.pallas_call(
        body, grid=(T // tm,),
        in_specs=[pl.BlockSpec((tm, 2 * D), lambda i: (i, 0)), pl.BlockSpec((tm, D), lambda i: (i, 0)),
                  pl.BlockSpec((1, D), lambda i: (0, 0)),
                  pl.BlockSpec((A_GROUPS, GBLK, GBLK), lambda i: (0, 0, 0)),
                  pl.BlockSpec((A_GROUPS, GBLK, GBLK), lambda i: (0, 0, 0))],
        out_specs=[pl.BlockSpec((tm, 2 * D), lambda i: (i, 0)), pl.BlockSpec((1, D), lambda i: (0, 0)),
                   pl.BlockSpec((A_GROUPS, GBLK, GBLK), lambda i: (0, 0, 0)),
                   pl.BlockSpec((A_GROUPS, GBLK, 1), lambda i: (0, 0, 0))],
        out_shape=[jax.ShapeDtypeStruct((T, 2 * D), bf16), jax.ShapeDtypeStruct((1, D), f32),
                   jax.ShapeDtypeStruct((A_GROUPS, GBLK, GBLK), f32), jax.ShapeDtypeStruct((A_GROUPS, GBLK, 1), f32)],
        scratch_shapes=[pltpu.VMEM((tm, D), bf16), pltpu.VMEM((tm, D), f32)],
        compiler_params=_cparams(("arbitrary",)), name=name,
    )(p, dy, gv, wm, bsb)


def b_mid_fwd(p, cw, *, tm, S, name):
    T = p.shape[0]
    nst = S // tm

    def body(p_ref, ph_ref, cw_ref, y_ref, ext):
        first = (pl.program_id(0) % nst) == 0
        q = p_ref[:, D:2 * D] * p_ref[:, 2 * D:]
        qh = ph_ref[:, D:2 * D] * ph_ref[:, 2 * D:]
        ext[0:HALO, :] = jnp.where(first, 0.0, qh)
        ext[HALO:, :] = q
        y = cw_ref[0:1, :] * ext[pl.ds(HALO - 2, tm), :] + cw_ref[1:2, :] * ext[pl.ds(HALO - 1, tm), :] + cw_ref[2:3, :] * q
        y_ref[...] = (p_ref[:, :D] * y).astype(bf16)

    return pl.pallas_call(
        body, grid=(T // tm,),
        in_specs=[pl.BlockSpec((tm, 3 * D), lambda i: (i, 0)), _prev_halo_spec(tm, 3 * D),
                  pl.BlockSpec((3, D), lambda i: (0, 0))],
        out_specs=pl.BlockSpec((tm, D), lambda i: (i, 0)),
        out_shape=jax.ShapeDtypeStruct((T, D), bf16),
        scratch_shapes=[pltpu.VMEM((tm + HALO, D), f32)],
        compiler_params=_cparams(("parallel",)), name=name,
    )(p, p, cw)


def b_mid_bwd(p, dy, cw, *, tm, S, name):
    T = p.shape[0]
    nst = S // tm

    def body(p_ref, ph_ref, pn_ref, dy_ref, dyn_ref, cw_ref, dp_ref, dcw_ref, ext, ext2):
        i = pl.program_id(0)
        first = (i % nst) == 0
        last = (i % nst) == nst - 1

        @pl.when(i == 0)
        def _():
            dcw_ref[...] = jnp.zeros(dcw_ref.shape, f32)

        gb, gc, xt = p_ref[:, :D], p_ref[:, D:2 * D], p_ref[:, 2 * D:]
        q = gc * xt
        ext[0:HALO, :] = jnp.where(first, 0.0, ph_ref[:, D:2 * D] * ph_ref[:, 2 * D:])
        ext[HALO:, :] = q
        q2 = ext[pl.ds(HALO - 2, tm), :]
        q1 = ext[pl.ds(HALO - 1, tm), :]
        y = cw_ref[0:1, :] * q2 + cw_ref[1:2, :] * q1 + cw_ref[2:3, :] * q
        dyo = dy_ref[...]
        dp_ref[:, :D] = (dyo * y).astype(bf16)
        dyc = dyo * gb
        ext2[0:tm, :] = dyc
        ext2[tm:, :] = jnp.where(last, 0.0, dyn_ref[...] * pn_ref[:, :D])
        dq = cw_ref[2:3, :] * dyc + cw_ref[1:2, :] * ext2[pl.ds(1, tm), :] + cw_ref[0:1, :] * ext2[pl.ds(2, tm), :]
        dp_ref[:, D:2 * D] = (dq * xt).astype(bf16)
        dp_ref[:, 2 * D:] = (dq * gc).astype(bf16)
        dcw_ref[0:1, :] += jnp.sum(dyc * q2, axis=0, keepdims=True)
        dcw_ref[1:2, :] += jnp.sum(dyc * q1, axis=0, keepdims=True)
        dcw_ref[2:3, :] += jnp.sum(dyc * q, axis=0, keepdims=True)

    return pl.pallas_call(
        body, grid=(T // tm,),
        in_specs=[pl.BlockSpec((tm, 3 * D), lambda i: (i, 0)), _prev_halo_spec(tm, 3 * D), _next_halo_spec(tm, 3 * D, T),
                  pl.BlockSpec((tm, D), lambda i: (i, 0)), _next_halo_spec(tm, D, T),
                  pl.BlockSpec((3, D), lambda i: (0, 0))],
        out_specs=[pl.BlockSpec((tm, 3 * D), lambda i: (i, 0)), pl.BlockSpec((3, D), lambda i: (0, 0))],
        out_shape=[jax.ShapeDtypeStruct((T, 3 * D), bf16), jax.ShapeDtypeStruct((3, D), f32)],
        scratch_shapes=[pltpu.VMEM((tm + HALO, D), f32), pltpu.VMEM((tm + HALO, D), f32)],
        compiler_params=_cparams(("arbitrary",)), name=name,
    )(p, p, p, dy, dy, cw)


def _pool_counts(i, nst, tm, rows, row0, w):
    t = (i % nst) * tm + row0 + lax.broadcasted_iota(jnp.int32, (rows, 1), 0)
    return jnp.minimum(t + 1, w).astype(f32)


def _pool_diff(p_ref, ext, g, i, nst, tm):
    w = POOL_WINDOWS[g]
    cs = slice(g * C_GDIM, (g + 1) * C_GDIM)
    pg = p_ref[:, cs]
    s = pg
    for k in range(1, w):
        s = s + ext[pl.ds(HALO - k, tm), cs]
    return s / _pool_counts(i, nst, tm, tm, 0, w) - pg


def c_mid_fwd(p, wg, scale, *, tm, S, name):
    T = p.shape[0]
    nst = S // tm

    def body(p_ref, ph_ref, wg_ref, sc_ref, y_ref, ext):
        i = pl.program_id(0)
        first = (i % nst) == 0
        ext[0:HALO, :] = jnp.where(first, 0.0, ph_ref[...])
        ext[HALO:, :] = p_ref[...]
        for g in range(C_GROUPS):
            cs = slice(g * C_GDIM, (g + 1) * C_GDIM)
            dg = _pool_diff(p_ref, ext, g, i, nst, tm).astype(bf16)
            wv = wg_ref[:, g].reshape(C_GDIM, C_GDIM)
            y_ref[:, cs] = (_dot(dg, wv) * sc_ref[:, cs]).astype(bf16)

    return pl.pallas_call(
        body, grid=(T // tm,),
        in_specs=[pl.BlockSpec((tm, D), lambda i: (i, 0)), _prev_halo_spec(tm, D),
                  pl.BlockSpec((NDEV, C_GROUPS, C_GDIM // NDEV, C_GDIM), lambda i: (0, 0, 0, 0)),
                  pl.BlockSpec((1, D), lambda i: (0, 0))],
        out_specs=pl.BlockSpec((tm, D), lambda i: (i, 0)),
        out_shape=jax.ShapeDtypeStruct((T, D), bf16),
        scratch_shapes=[pltpu.VMEM((tm + HALO, D), f32)],
        compiler_params=_cparams(("parallel",)), name=name,
    )(p, p, wg, scale)


def c_mid_bwd(p, dy, wg, scale, *, tm, S, name):
    T = p.shape[0]
    nst = S // tm

    def body(p_ref, ph_ref, dy_ref, dyn_ref, wg_ref, sc_ref, dp_ref, dsc_ref, dwg_ref, ext, ext2):
        i = pl.program_id(0)
        first = (i % nst) == 0
        last = (i % nst) == nst - 1

        @pl.when(i == 0)
        def _():
            dsc_ref[...] = jnp.zeros(dsc_ref.shape, f32)
            dwg_ref[...] = jnp.zeros(dwg_ref.shape, f32)

        ext[0:HALO, :] = jnp.where(first, 0.0, ph_ref[...])
        ext[HALO:, :] = p_ref[...]
        for g in range(C_GROUPS):
            w = POOL_WINDOWS[g]
            cs = slice(g * C_GDIM, (g + 1) * C_GDIM)
            dg = _pool_diff(p_ref, ext, g, i, nst, tm).astype(bf16)
            wv = wg_ref[:, g].reshape(C_GDIM, C_GDIM)
            dyo = dy_ref[:, cs]
            dsc_ref[:, cs] += jnp.sum(dyo * _dot(dg, wv), axis=0, keepdims=True)
            dyp = (dyo * sc_ref[:, cs]).astype(bf16)
            dypn = (dyn_ref[:, cs] * sc_ref[:, cs]).astype(bf16)
            dwg_ref[g] += _dot_tn(dg, dyp)
            dd = _dot_nt(dyp, wv)
            ddn = _dot_nt(dypn, wv)
            ext2[0:tm, cs] = dd / _pool_counts(i, nst, tm, tm, 0, w)
            ext2[tm:, cs] = jnp.where(last, 0.0, ddn / _pool_counts(i, nst, tm, HALO, tm, w))
            s = -dd
            for k in range(w):
                s = s + ext2[pl.ds(k, tm), cs]
            dp_ref[:, cs] = s.astype(bf16)

    return pl.pallas_call(
        body, grid=(T // tm,),
        in_specs=[pl.BlockSpec((tm, D), lambda i: (i, 0)), _prev_halo_spec(tm, D),
                  pl.BlockSpec((tm, D), lambda i: (i, 0)), _next_halo_spec(tm, D, T),
                  pl.BlockSpec((NDEV, C_GROUPS, C_GDIM // NDEV, C_GDIM), lambda i: (0, 0, 0, 0)),
                  pl.BlockSpec((1, D), lambda i: (0, 0))],
        out_specs=[pl.BlockSpec((tm, D), lambda i: (i, 0)), pl.BlockSpec((1, D), lambda i: (0, 0)),
                   pl.BlockSpec((C_GROUPS, C_GDIM, C_GDIM), lambda i: (0, 0, 0))],
        out_shape=[jax.ShapeDtypeStruct((T, D), bf16), jax.ShapeDtypeStruct((1, D), f32),
                   jax.ShapeDtypeStruct((C_GROUPS, C_GDIM, C_GDIM), f32)],
        scratch_shapes=[pltpu.VMEM((tm + HALO, D), f32), pltpu.VMEM((tm + HALO, D), f32)],
        compiler_params=_cparams(("arbitrary",)), name=name,
    )(p, p, dy, dy, wg, scale)


def final_loss(x, g, tgt, *, tm, name):
    T = x.shape[0]

    def body(x_ref, g_ref, t_ref, loss_ref, dx_ref, dg_ref):
        @pl.when(pl.program_id(0) == 0)
        def _():
            loss_ref[...] = jnp.zeros(loss_ref.shape, f32)
            dg_ref[...] = jnp.zeros(dg_ref.shape, f32)

        xv = x_ref[...]
        r = lax.rsqrt(jnp.mean(xv * xv, axis=-1, keepdims=True) + EPS)
        xhat = xv * r
        err = xhat * g_ref[...] - t_ref[...]
        loss_ref[...] += 0.5 * jnp.sum(jnp.mean(err * err, axis=-1, keepdims=True))
        dy = err * (1.0 / D)
        dg_ref[...] += jnp.sum(dy * xhat, axis=0, keepdims=True)
        dxhat = dy * g_ref[...]
        dx_ref[...] = r * (dxhat - xhat * jnp.mean(dxhat * xhat, axis=-1, keepdims=True))

    return pl.pallas_call(
        body, grid=(T // tm,),
        in_specs=[pl.BlockSpec((tm, D), lambda i: (i, 0)), pl.BlockSpec((1, D), lambda i: (0, 0)),
                  pl.BlockSpec((tm, D), lambda i: (i, 0))],
        out_specs=[pl.BlockSpec((8, 128), lambda i: (0, 0)), pl.BlockSpec((tm, D), lambda i: (i, 0)),
                   pl.BlockSpec((1, D), lambda i: (0, 0))],
        out_shape=[jax.ShapeDtypeStruct((8, 128), f32), jax.ShapeDtypeStruct((T, D), f32),
                   jax.ShapeDtypeStruct((1, D), f32)],
        compiler_params=_cparams(("arbitrary",)), name=name,
    )(x, g, tgt)


def _slot(px, py, pc):
    return 4 * px + 2 * py + pc


def _with_own_block(s, me):
    zone = lax.empty((NDEV,) + s.shape, s.dtype)
    return lax.dynamic_update_slice(zone, s[None], (me,) + (0,) * s.ndim)


def all_gather(arrs, me, *, name):
    n = len(arrs)

    def body(*refs):
        ins, outs = refs[:n], refs[2 * n:3 * n]
        send_sems, recv_sems = refs[3 * n:]
        x, y, c = lax.axis_index("x"), lax.axis_index("y"), lax.axis_index("c")
        me, sibling = (x, y, c), (x, y, 1 - c)
        chips = [(1 - x, y), (x, 1 - y), (1 - x, 1 - y)]

        def copy(a, k, block, to, src=None):
            dst = outs[a].at[_slot(*block)]
            return pltpu.make_async_remote_copy(
                src_ref=dst if src is None else src, dst_ref=dst,
                send_sem=send_sems.at[a, k], recv_sem=recv_sems.at[a, k], device_id=to, device_id_type=MESH)

        first = []
        for a in range(n):
            first.append(copy(a, 0, me, sibling, src=ins[a]))
            first += [copy(a, 1 + j, me, (*chip, c), src=ins[a]) for j, chip in enumerate(chips)]
        for cp in first:
            cp.start()
        passed = []
        for j, chip in enumerate(chips):
            for a in range(n):
                copy(a, 1 + j, (*chip, c), me).wait_recv()
                fwd = copy(a, 4 + j, (*chip, c), sibling)
                fwd.start()
                passed.append(fwd)
        for a in range(n):
            copy(a, 0, sibling, me).wait_recv()
        for j, chip in enumerate(chips):
            for a in range(n):
                copy(a, 4 + j, (*chip, 1 - c), me).wait_recv()
        for cp in first + passed:
            cp.wait_send()

    any_spec = pl.BlockSpec(memory_space=pl.ANY)
    return pl.pallas_call(
        body,
        in_specs=[any_spec] * (2 * n), out_specs=[any_spec] * n,
        out_shape=[jax.ShapeDtypeStruct((NDEV,) + a.shape, a.dtype) for a in arrs],
        input_output_aliases={n + i: i for i in range(n)},
        scratch_shapes=[pltpu.SemaphoreType.DMA((n, 7)), pltpu.SemaphoreType.DMA((n, 7))],
        compiler_params=pltpu.CompilerParams(has_side_effects=True), name=name,
    )(*arrs, *[_with_own_block(a, me) for a in arrs])


HBM_SPEC = pl.BlockSpec(memory_space=pltpu.HBM)
SEM_SPEC = pl.BlockSpec(memory_space=pltpu.SEMAPHORE)
ANY_SPEC = pl.BlockSpec(memory_space=pl.ANY)
TOKEN_SHAPE = jax.ShapeDtypeStruct((8, 128), f32)
DATAFLOW_EFFECT = pltpu.SideEffectType.DATAFLOW_SIDE_EFFECTING


def _in_hbm(a):
    return pltpu.with_memory_space_constraint(a, pltpu.HBM)


def _hbm_like(a):
    return pltpu.HBM(a.shape, a.dtype)


def _mesh_pos():
    return lax.axis_index("x"), lax.axis_index("y"), lax.axis_index("c")


def _gather_targets(x, y, c):
    return [(x, y, 1 - c), (1 - x, y, c), (x, 1 - y, c), (1 - x, 1 - y, c)]


def gather_start(shards, me, after, *, name):
    n = len(shards)
    extra = [] if after is None else [after]

    def body(*refs):
        srcs, lands = refs[:n], refs[n:2 * n]
        send_sems, recv_sems = refs[2 * n + len(extra)], refs[2 * n + len(extra) + 1]
        token = refs[-1]
        x, y, c = _mesh_pos()
        me = _slot(x, y, c)
        for a in range(n):
            for k, to in enumerate(_gather_targets(x, y, c)):
                pltpu.make_async_remote_copy(
                    src_ref=srcs[a], dst_ref=lands[a].at[me], send_sem=send_sems.at[4 * a + k], recv_sem=recv_sems.at[4 * a + k],
                    device_id=to, device_id_type=MESH).start()
        token[...] = jnp.zeros(token.shape, f32)

    lands = [_with_own_block(s, me) for s in shards]
    sems = pltpu.SemaphoreType.DMA((4 * n,))
    out = pl.pallas_call(
        body, name=name,
        in_specs=[HBM_SPEC] * (2 * n) + [ANY_SPEC] * len(extra),
        out_specs=[SEM_SPEC, SEM_SPEC] + [HBM_SPEC] * (2 * n) + [pl.BlockSpec(memory_space=pltpu.VMEM)],
        out_shape=[sems, sems] + [_hbm_like(s) for s in shards] + [_hbm_like(l) for l in lands] + [TOKEN_SHAPE],
        input_output_aliases={i: 2 + i for i in range(2 * n)},
        compiler_params=pltpu.CompilerParams(has_side_effects=DATAFLOW_EFFECT),
    )(*[_in_hbm(s) for s in shards], *[_in_hbm(l) for l in lands], *extra)
    return out[0], out[1], out[2:2 + n], out[2 + n:2 + 2 * n], out[-1]


def gather_wait(send_sems, recv_sems, shards, lands, after, *, name):
    n = len(shards)

    def body(*refs):
        srcs, lands_in = refs[:n], refs[n:2 * n]
        send_sems, recv_sems = refs[2 * n], refs[2 * n + 1]
        x, y, c = _mesh_pos()
        for a in range(n):
            for k, frm in enumerate(_gather_targets(x, y, c)):
                cp = pltpu.make_async_remote_copy(
                    src_ref=srcs[a], dst_ref=lands_in[a].at[_slot(*frm)], send_sem=send_sems.at[4 * a + k],
                    recv_sem=recv_sems.at[4 * a + k], device_id=frm, device_id_type=MESH)
                cp.wait_send()
                cp.wait_recv()

    out = pl.pallas_call(
        body, name=name,
        in_specs=[HBM_SPEC] * (2 * n) + [SEM_SPEC, SEM_SPEC, ANY_SPEC],
        out_specs=[HBM_SPEC] * (2 * n),
        out_shape=[_hbm_like(s) for s in shards] + [_hbm_like(l) for l in lands],
        input_output_aliases={i: i for i in range(2 * n)},
        compiler_params=pltpu.CompilerParams(has_side_effects=DATAFLOW_EFFECT),
    )(*shards, *lands, send_sems, recv_sems, after)
    return out[:n], out[n:]


def gather_finish(lands, *, name):
    n = len(lands)

    def body(*refs):
        lands_in, lands_out = refs[:n], refs[n:2 * n]
        send_sems, recv_sems = refs[2 * n:]
        x, y, c = _mesh_pos()
        sibling = (x, y, 1 - c)
        chips = [(1 - x, y), (x, 1 - y), (1 - x, 1 - y)]
        sent = []
        for a in range(n):
            for j, chip in enumerate(chips):
                s = _slot(*chip, c)
                cp = pltpu.make_async_remote_copy(
                    src_ref=lands_in[a].at[s], dst_ref=lands_out[a].at[s], send_sem=send_sems.at[a, j],
                    recv_sem=recv_sems.at[a, j], device_id=sibling, device_id_type=MESH)
                cp.start()
                sent.append(cp)
        for a in range(n):
            for j, chip in enumerate(chips):
                s = _slot(*chip, 1 - c)
                pltpu.make_async_remote_copy(
                    src_ref=lands_in[a].at[s], dst_ref=lands_out[a].at[s], send_sem=send_sems.at[a, j],
                    recv_sem=recv_sems.at[a, j], device_id=sibling, device_id_type=MESH).wait_recv()
        for cp in sent:
            cp.wait_send()

    return pl.pallas_call(
        body, name=name,
        in_specs=[ANY_SPEC] * n, out_specs=[ANY_SPEC] * n,
        out_shape=[jax.ShapeDtypeStruct(l.shape, l.dtype) for l in lands],
        input_output_aliases={i: i for i in range(n)},
        scratch_shapes=[pltpu.SemaphoreType.DMA((n, 3)), pltpu.SemaphoreType.DMA((n, 3))],
        compiler_params=pltpu.CompilerParams(has_side_effects=True),
    )(*lands)


def _peer(x, y, c, r):
    return (x ^ ((r >> 2) & 1), y ^ ((r >> 1) & 1), c ^ (r & 1))


def exchange_start(items, lands, after, *, name):
    n, m = len(items), len(lands)
    extra = [] if after is None else [after]

    def body(*refs):
        gs, zones = refs[:n], refs[n:n + m]
        send_sems, recv_sems = refs[n + m + len(extra)], refs[n + m + len(extra) + 1]
        token = refs[-1]
        x, y, c = _mesh_pos()
        me = _slot(x, y, c)
        for r in (1, 2, 4, 3, 5, 6, 7):
            to = _peer(x, y, c, r)
            for a, (_, zi, l) in enumerate(items):
                pltpu.make_async_remote_copy(
                    src_ref=gs[a].at[_slot(*to)], dst_ref=zones[zi].at[me, l], send_sem=send_sems.at[7 * a + r - 1],
                    recv_sem=recv_sems.at[7 * a + r - 1], device_id=to, device_id_type=MESH).start()
        token[...] = jnp.zeros(token.shape, f32)

    gs = [g for g, _, _ in items]
    sems = pltpu.SemaphoreType.DMA((7 * n,))
    out = pl.pallas_call(
        body, name=name,
        in_specs=[HBM_SPEC] * (n + m) + [ANY_SPEC] * len(extra),
        out_specs=[SEM_SPEC, SEM_SPEC] + [HBM_SPEC] * (n + m) + [pl.BlockSpec(memory_space=pltpu.VMEM)],
        out_shape=[sems, sems] + [_hbm_like(g) for g in gs] + [_hbm_like(z) for z in lands] + [TOKEN_SHAPE],
        input_output_aliases={i: 2 + i for i in range(n + m)},
        compiler_params=pltpu.CompilerParams(has_side_effects=DATAFLOW_EFFECT),
    )(*[_in_hbm(g) for g in gs], *[_in_hbm(z) for z in lands], *extra)
    return out[0], out[1], out[2:2 + n], out[2 + n:2 + n + m], out[-1]


def exchange_wait(parts, lands, after, *, name):
    m = len(lands)
    flat_gs = [g for _, _, _, gs in parts for g in gs]
    ng = len(flat_gs)

    def body(*refs):
        gs, zones = refs[:ng], refs[ng:ng + m]
        sem_refs = refs[ng + m:ng + m + 2 * len(parts)]
        x, y, c = _mesh_pos()
        base = 0
        for pi, (items, _, _, _) in enumerate(parts):
            send_sems, recv_sems = sem_refs[2 * pi], sem_refs[2 * pi + 1]
            for r in range(1, NDEV):
                frm = _peer(x, y, c, r)
                for a, (_, zi, l) in enumerate(items):
                    cp = pltpu.make_async_remote_copy(
                        src_ref=gs[base + a].at[_slot(*frm)], dst_ref=zones[zi].at[_slot(*frm), l],
                        send_sem=send_sems.at[7 * a + r - 1], recv_sem=recv_sems.at[7 * a + r - 1],
                        device_id=frm, device_id_type=MESH)
                    cp.wait_send()
                    cp.wait_recv()
            base += len(items)

    sem_args = [s for _, ss, rs, _ in parts for s in (ss, rs)]
    out = pl.pallas_call(
        body, name=name,
        in_specs=[HBM_SPEC] * (ng + m) + [SEM_SPEC] * len(sem_args) + [ANY_SPEC],
        out_specs=[HBM_SPEC] * (ng + m),
        out_shape=[_hbm_like(g) for g in flat_gs] + [_hbm_like(z) for z in lands],
        input_output_aliases={i: i for i in range(ng + m)},
        compiler_params=pltpu.CompilerParams(has_side_effects=DATAFLOW_EFFECT),
    )(*flat_gs, *lands, *sem_args, after)
    return out[ng:]


def sum8(g, *, name):
    _, R, C = g.shape
    tr = R

    def body(g_ref, o_ref):
        s = g_ref[0]
        for k in range(1, NDEV):
            s = s + g_ref[k]
        o_ref[...] = s

    return pl.pallas_call(
        body, grid=(R // tr,),
        in_specs=[pl.BlockSpec((NDEV, tr, C), lambda i: (0, i, 0))],
        out_specs=pl.BlockSpec((tr, C), lambda i: (i, 0)),
        out_shape=jax.ShapeDtypeStruct((R, C), f32),
        compiler_params=_cparams(("parallel",)), name=name,
    )(g)


def _adam_math(w, g, m, v):
    m = ADAM_B1 * m + (1.0 - ADAM_B1) * g
    v = ADAM_B2 * v + (1.0 - ADAM_B2) * (g * g)
    m_hat = m / (1.0 - ADAM_B1 ** ADAM_STEP)
    v_hat = v / (1.0 - ADAM_B2 ** ADAM_STEP)
    delta = -ADAM_LR * (m_hat / (jnp.sqrt(v_hat) + ADAM_EPS) + ADAM_WD * w)
    return delta, m, v


def adamw_slices(w, m, v, recv, *, row0=0, col0=0, tr=None, tc=None, name):
    L, R, C = w.shape
    Rp, Cp = recv.shape[2], recv.shape[3]
    tr = R if tr is None else tr
    tc = C if tc is None else tc
    assert R % tr == 0 and C % tc == 0
    assert (tr == R or (Rp == R and row0 == 0)) and (tc == C or (Cp == C and col0 == 0))
    rr = Rp if tr == R else tr
    rc = Cp if tc == C else tc

    def body(w_ref, m_ref, v_ref, r_ref, g_ref, d_ref, nm_ref, nv_ref):
        c0 = col0 if tc == C else 0
        r0 = row0 if tr == R else 0
        g = r_ref[0, r0:r0 + tr, c0:c0 + tc].astype(f32)
        for k in range(1, NDEV):
            g = g + r_ref[k, r0:r0 + tr, c0:c0 + tc].astype(f32)
        delta, nm, nv = _adam_math(w_ref[...], g, m_ref[...], v_ref[...])
        g_ref[...] = g
        d_ref[...] = delta
        nm_ref[...] = nm
        nv_ref[...] = nv

    wspec = pl.BlockSpec((None, tr, tc), lambda l, i, j: (l, i, j))
    return pl.pallas_call(
        body, grid=(L, R // tr, C // tc),
        in_specs=[wspec, wspec, wspec, pl.BlockSpec((NDEV, None, rr, rc), lambda l, i, j: (0, l, i, j))],
        out_specs=[wspec] * 4, out_shape=[jax.ShapeDtypeStruct((L, R, C), f32)] * 4,
        compiler_params=_cparams(("parallel", "parallel", "parallel")), name=name,
    )(w, m, v, recv)


def adamw_slices_t(w, m, v, recv, *, name):
    L, R, C = w.shape

    def body(w_ref, m_ref, v_ref, r_ref, g_ref, d_ref, nm_ref, nv_ref):
        gt = r_ref[0].astype(f32)
        for k in range(1, NDEV):
            gt = gt + r_ref[k].astype(f32)
        g = gt.T
        delta, nm, nv = _adam_math(w_ref[...], g, m_ref[...], v_ref[...])
        g_ref[...] = g
        d_ref[...] = delta
        nm_ref[...] = nm
        nv_ref[...] = nv

    wspec = pl.BlockSpec((None, R, C), lambda l: (l, 0, 0))
    return pl.pallas_call(
        body, grid=(L,),
        in_specs=[wspec, wspec, wspec, pl.BlockSpec((NDEV, None, C, R), lambda l: (0, l, 0, 0))],
        out_specs=[wspec] * 4, out_shape=[jax.ShapeDtypeStruct((L, R, C), f32)] * 4,
        compiler_params=_cparams(("parallel",)), name=name,
    )(w, m, v, recv)


def adamw_plain(w, m, v, g, *, name):
    R, C = w.shape

    def body(w_ref, m_ref, v_ref, g_ref, d_ref, nm_ref, nv_ref):
        delta, nm, nv = _adam_math(w_ref[...], g_ref[...], m_ref[...], v_ref[...])
        d_ref[...] = delta
        nm_ref[...] = nm
        nv_ref[...] = nv

    return pl.pallas_call(
        body, out_shape=[jax.ShapeDtypeStruct((R, C), f32)] * 3, name=name,
    )(w, m, v, g)


def _chunk_causal_mask():
    pos = jnp.arange(GBLK)
    return (pos[None, :] // CHUNK) <= (pos[:, None] // CHUNK)


def kernel(x, norm_mix_g, norm_ffn_g, final_norm_g, a_w_in, a_v_norm_g, a_w_s, a_b_s, a_w_out, b_w_in, b_conv_w, b_w_out, c_w_in, c_w_grp, c_scale, c_w_out, f_w_gate, f_w_up, f_w_down, loss_target, m_norm_mix_g, m_norm_ffn_g, m_final_norm_g, m_a_w_in, m_a_v_norm_g, m_a_w_s, m_a_b_s, m_a_w_out, m_b_w_in, m_b_conv_w, m_b_w_out, m_c_w_in, m_c_w_grp, m_c_scale, m_c_w_out, m_f_w_gate, m_f_w_up, m_f_w_down, v_norm_mix_g, v_norm_ffn_g, v_final_norm_g, v_a_w_in, v_a_v_norm_g, v_a_w_s, v_a_b_s, v_a_w_out, v_b_w_in, v_b_conv_w, v_b_w_out, v_c_w_in, v_c_w_grp, v_c_scale, v_c_w_out, v_f_w_gate, v_f_w_up, v_f_w_down):
    B, S, _ = x.shape
    T = B * S
    tm = min(512, S)
    tmm = min(1024, T)
    tms = min(512, T)
    me = _slot(lax.axis_index("x"), lax.axis_index("y"), lax.axis_index("c"))

    pad_r = ((0, 0), (0, F_PAD - F_LOC), (0, 0))

    def tb(w):
        return jnp.transpose(w, (0, 2, 1)).astype(bf16)

    f_gu = jnp.concatenate([jnp.pad(tb(f_w_gate), pad_r), jnp.pad(tb(f_w_up), pad_r)], axis=1)
    f_d = jnp.pad(f_w_down.astype(bf16), pad_r)
    small = jnp.concatenate([a_v_norm_g, b_conv_w[0], c_scale, jnp.zeros((2, GBLK), f32)], axis=0)
    a_in_b, a_out_b = tb(a_w_in), a_w_out.astype(bf16)
    stage_shards = [
        [a_in_b[0], a_out_b[0], small],
        [f_gu[0], f_d[0]],
        [tb(b_w_in)[0], b_w_out[0].astype(bf16), f_gu[1], f_d[1]],
        [c_w_in[0].astype(bf16), c_w_grp[0].astype(bf16), c_w_out[0].astype(bf16), f_gu[2], f_d[2]],
        [a_in_b[1], a_out_b[1], f_gu[3], f_d[3]],
    ]
    started, tok = [], None
    for s in range(len(stage_shards)):
        started.append(gather_start(stage_shards[s], me, tok, name=f"gather_start_{s}"))
        tok = started[-1][4]

    def stage_weights(s, after):
        send_sems, recv_sems, shards, lands, _ = started[s]
        _, lands = gather_wait(send_sems, recv_sems, shards, lands, after, name=f"gather_wait_{s}")
        return gather_finish(lands, name=f"gather_finish_{s}")

    wm_masked = jnp.where(_chunk_causal_mask()[None, None], a_w_s, 0.0).astype(bf16)
    bsb = jnp.broadcast_to(a_b_s[:, :, :, None], a_b_s.shape + (GBLK,))

    xc = x.reshape(T, D)
    saved, weights = [], []
    for i in range(DEPTH):
        kind, j = i % 3, i // 3
        gmix = norm_mix_g[i][None]
        if i == 0:
            w_in, w_out, small_g = stage_weights(0, tok)
            small_full = jnp.transpose(small_g, (1, 0, 2)).reshape(8, D)
            gv_full, cw_full, cs_full = small_full[0:2], small_full[2:5], small_full[5:6]
            w = [w_in, w_out]
        else:
            w = stage_weights(i + 1, xc)
        if kind == 0:
            w_in, w_out = w[:2]
            p, h = norm_mm(xc, gmix, w_in, tm=tms, transposed=True, name=f"a_in_{i}")
            y = a_mid_fwd(p, gv_full[j][None], wm_masked[j], bsb[j], tm=tm, name=f"a_mid_{i}")
        elif kind == 1:
            w_in, w_out = w[:2]
            p, h = norm_mm(xc, gmix, w_in, tm=tms, transposed=True, name=f"b_in_{i}")
            y = b_mid_fwd(p, cw_full, tm=tm, S=S, name=f"b_mid_{i}")
        else:
            w_in, w_grp, w_out = w[:3]
            p, h = norm_mm(xc, gmix, w_in, tm=tms, transposed=False, name=f"c_in_{i}")
            y = c_mid_fwd(p, w_grp, cs_full, tm=tm, S=S, name=f"c_mid_{i}")
        x1 = mm_res(y, w_out, xc, tm=tms, name=f"mix_out_{i}")
        if i == 0:
            w = w + list(stage_weights(1, x1))
        weights.append(w)
        w_fgu, w_fd = w[-2], w[-1]
        pf, h2, a = norm_mm(x1, norm_ffn_g[i][None], w_fgu, tm=tmm, transposed=True, swiglu=True, name=f"f_in_{i}")
        x2 = mm_res(a, w_fd, x1, tm=tms, name=f"f_out_{i}")
        saved.append((xc, p, h, y, x1, pf, h2, a))
        xc = x2

    loss_blk, dx, d_final = final_loss(xc, final_norm_g[None], loss_target.reshape(T, D), tm=tms, name="final_loss")
    loss = lax.psum(loss_blk[0, 0], AXES)

    zone_shapes = [(2, 256, D), (2, GBLK, D), (1, 384, D), (1, GBLK, D), (1, GBLK, D), (1, GBLK, C_GDIM), (1, GBLK, D),
                   (DEPTH, 2 * F_PAD, D), (DEPTH, F_PAD, D)]
    Z_AIN, Z_AOUT, Z_BIN, Z_BOUT, Z_CIN, Z_CGRP, Z_COUT, Z_FGU, Z_FD = range(9)
    lands = [lax.empty((NDEV,) + s, bf16) for s in zone_shapes]
    parts = []

    def exchange(items, after, name):
        nonlocal lands
        for g, zi, l in items:
            own = lax.dynamic_index_in_dim(g, me, axis=0, keepdims=True)[None]
            lands[zi] = lax.dynamic_update_slice(lands[zi], own, (me, l) + (0,) * (g.ndim - 1))
        idx = sorted({zi for _, zi, _ in items})
        local = [(g, idx.index(zi), l) for g, zi, l in items]
        send_sems, recv_sems, gs, zs, token = exchange_start(local, [lands[zi] for zi in idx], after, name=name)
        for q, zi in enumerate(idx):
            lands[zi] = zs[q]
        parts.append(([(None, zi, l) for _, zi, l in items], send_sems, recv_sems, gs))
        return token

    d_mix, d_ffn = [None] * DEPTH, [None] * DEPTH
    d_gv, d_wm, d_bs = [None] * 2, [None] * 2, [None] * 2
    tok = None
    for i in reversed(range(DEPTH)):
        kind, j = i % 3, i // 3
        xin, p, h, y, x1, pf, h2, a = saved[i]
        w = weights[i]
        w_in, w_out, w_fgu, w_fd = w[0], w[-3], w[-2], w[-1]
        gffn = norm_ffn_g[i][None] if tok is None else norm_ffn_g[i][None] + tok[0:1, 0:1]
        dpf, dxb = ffn_bwd_mid(dx, w_fd, pf, tm=tmm, name=f"f_bwd_mid_{i}")
        dx1, d_ffn[i] = mm_normbwd(dpf, w_fgu, x1, gffn, dx, tm=tms, transposed=True, name=f"f_bwd_in_{i}")
        items = [(mm_tn(dpf, h2, tm=tmm, row_sharded=True, name=f"f_dw_in_{i}"), Z_FGU, i),
                 (mm_tn(a, dxb, tm=tmm, row_sharded=True, name=f"f_dw_out_{i}"), Z_FD, i)]
        gmix = norm_mix_g[i][None]
        if i == 0:
            gmix = gmix + exchange(items, tok, "exchange_start_ffn0")[0:1, 0:1]
            items = []
        dy, dx1b = mm_nt(dx1, w_out, tm=tmm, name=f"mix_bwd_out_{i}")
        if kind == 0:
            dp, d_gv[j], d_wm[j], d_bs[j] = a_mid_bwd(p, dy, gv_full[j][None], wm_masked[j], bsb[j], tm=tm, name=f"a_bwd_mid_{i}")
            dx, d_mix[i] = mm_normbwd(dp, w_in, xin, gmix, dx1, tm=tms, transposed=True, name=f"a_bwd_in_{i}")
            items += [(mm_tn(dp, h, tm=tmm, row_sharded=True, name=f"a_dw_in_{i}"), Z_AIN, j),
                      (mm_tn(y, dx1b, tm=tmm, row_sharded=True, name=f"a_dw_out_{i}"), Z_AOUT, j)]
        elif kind == 1:
            dp, d_cw = b_mid_bwd(p, dy, cw_full, tm=tm, S=S, name=f"b_bwd_mid_{i}")
            dx, d_mix[i] = mm_normbwd(dp, w_in, xin, gmix, dx1, tm=tms, transposed=True, name=f"b_bwd_in_{i}")
            items += [(mm_tn(dp, h, tm=tmm, row_sharded=True, name=f"b_dw_in_{i}"), Z_BIN, 0),
                      (mm_tn(y, dx1b, tm=tmm, row_sharded=True, name=f"b_dw_out_{i}"), Z_BOUT, 0)]
        else:
            dp, d_cs, d_wgrp = c_mid_bwd(p, dy, w[1], cs_full, tm=tm, S=S, name=f"c_bwd_mid_{i}")
            dx, d_mix[i] = mm_normbwd(dp, w_in, xin, gmix, dx1, tm=tms, transposed=False, name=f"c_bwd_in_{i}")
            gw_cgrp = jnp.transpose(d_wgrp.reshape(C_GROUPS, NDEV, C_GDIM // NDEV, C_GDIM), (1, 0, 2, 3)).astype(bf16)
            items += [(mm_tn(h, dp, tm=tmm, row_sharded=True, name=f"c_dw_in_{i}"), Z_CIN, 0),
                      (gw_cgrp.reshape(NDEV, GBLK, C_GDIM), Z_CGRP, 0),
                      (mm_tn(y, dx1b, tm=tmm, row_sharded=True, name=f"c_dw_out_{i}"), Z_COUT, 0)]
        tok = exchange(items, tok, f"exchange_start_{i}")
    grad_x = dx.reshape(B, S, D)

    mask = _chunk_causal_mask()
    small_parts = [
        jnp.concatenate(d_mix, axis=0).reshape(-1, GBLK),
        jnp.concatenate(d_ffn, axis=0).reshape(-1, GBLK),
        d_final.reshape(-1, GBLK),
        jnp.concatenate(d_gv, axis=0).reshape(-1, GBLK),
        jnp.where(mask[None, None], jnp.stack(d_wm), 0.0).reshape(-1, GBLK),
        jnp.stack(d_bs).reshape(-1, GBLK),
        d_cw.reshape(-1, GBLK),
        d_cs.reshape(-1, GBLK),
    ]
    small_rows = [q.shape[0] for q in small_parts]
    sg_send, sg_recv, sg_shards, sg_lands, _ = gather_start(
        [jnp.concatenate(small_parts, axis=0)], me, tok, name="gather_small_start")

    def plain(w, m, v, g, name):
        shp = w.shape
        w2, m2, v2, g2 = (t.reshape(-1, shp[-1]) for t in (w, m, v, g))
        return tuple(t.reshape(shp) for t in adamw_plain(w2, m2, v2, g2, name=name))

    def sliced(w, m, v, recv, name, **kw):
        shp = w.shape
        w3, m3, v3 = (t.reshape((shp[0], -1, shp[-1])) for t in (w, m, v))
        return tuple(t.reshape(shp) for t in adamw_slices(w3, m3, v3, recv, name=name, **kw))

    def sliced_t(w, m, v, recv, name, **kw):
        wt, mt, vt = (jnp.transpose(t, (0, 2, 1)) for t in (w, m, v))
        return tuple(jnp.transpose(t, (0, 2, 1)) for t in adamw_slices(wt, mt, vt, recv, name=name, **kw))

    lands = list(exchange_wait(parts[:-1], lands, dx, name="exchange_wait"))
    _, _, r_bin, r_bout, r_cin, r_cgrp, r_cout, r_fgu, r_fd = lands
    res = {}
    res["b_w_in"] = adamw_slices_t(b_w_in, m_b_w_in, v_b_w_in, r_bin, name="adam_b_w_in")
    res["b_w_out"] = sliced(b_w_out, m_b_w_out, v_b_w_out, r_bout, "adam_b_w_out")
    res["c_w_in"] = sliced(c_w_in, m_c_w_in, v_c_w_in, r_cin, "adam_c_w_in")
    res["c_w_grp"] = sliced(c_w_grp, m_c_w_grp, v_c_w_grp, r_cgrp, "adam_c_w_grp")
    res["c_w_out"] = sliced(c_w_out, m_c_w_out, v_c_w_out, r_cout, "adam_c_w_out")
    res["f_w_gate"] = sliced_t(f_w_gate, m_f_w_gate, v_f_w_gate, r_fgu, "adam_f_gate", row0=0, tc=256)
    res["f_w_up"] = sliced_t(f_w_up, m_f_w_up, v_f_w_up, r_fgu, "adam_f_up", row0=F_PAD, tc=256)
    res["f_w_down"] = sliced(f_w_down, m_f_w_down, v_f_w_down, r_fd, "adam_f_down", tc=256)

    last_items, last_send, last_recv, last_gs = parts[-1]
    last_zones = [Z_AIN, Z_AOUT]
    last_part = ([(None, last_zones.index(zi), l) for _, zi, l in last_items], last_send, last_recv, last_gs)
    r_ain, r_aout = exchange_wait([last_part], [lands[zi] for zi in last_zones], res["f_w_down"][1], name="exchange_wait_last")
    res["a_w_in"] = adamw_slices_t(a_w_in, m_a_w_in, v_a_w_in, r_ain, name="adam_a_w_in")
    res["a_w_out"] = sliced(a_w_out, m_a_w_out, v_a_w_out, r_aout, "adam_a_w_out")

    _, sg_lands = gather_wait(sg_send, sg_recv, sg_shards, sg_lands, res["a_w_out"][1], name="gather_small_wait")
    (gs_all,) = gather_finish(sg_lands, name="gather_small_finish")
    gs = sum8(gs_all, name="sum_small_grads")
    offs = [0]
    for r in small_rows:
        offs.append(offs[-1] + r)
    sp = [gs[offs[q]:offs[q + 1]] for q in range(len(small_parts))]
    grad_norm_mix_g = sp[0].reshape(DEPTH, D)
    grad_norm_ffn_g = sp[1].reshape(DEPTH, D)
    grad_final_norm_g = sp[2].reshape(D)
    grad_a_w_s = sp[4].reshape(a_w_s.shape)
    grad_a_b_s = sp[5].reshape(a_b_s.shape)

    def my_cols(full):
        return lax.dynamic_slice_in_dim(full, me * GBLK, GBLK, axis=1)

    grad_a_v_norm_g = my_cols(sp[3].reshape(2, D))
    grad_b_conv_w = my_cols(sp[6].reshape(3, D))[None]
    grad_c_scale = my_cols(sp[7].reshape(1, D))
    res["norm_mix_g"] = (grad_norm_mix_g,) + plain(norm_mix_g, m_norm_mix_g, v_norm_mix_g, grad_norm_mix_g, "adam_norm_mix")
    res["norm_ffn_g"] = (grad_norm_ffn_g,) + plain(norm_ffn_g, m_norm_ffn_g, v_norm_ffn_g, grad_norm_ffn_g, "adam_norm_ffn")
    res["final_norm_g"] = (grad_final_norm_g,) + tuple(
        t.reshape(D) for t in plain(final_norm_g[None], m_final_norm_g[None], v_final_norm_g[None], grad_final_norm_g[None], "adam_final"))
    res["a_v_norm_g"] = (grad_a_v_norm_g,) + plain(a_v_norm_g, m_a_v_norm_g, v_a_v_norm_g, grad_a_v_norm_g, "adam_a_v_norm")
    res["a_w_s"] = (grad_a_w_s,) + plain(a_w_s, m_a_w_s, v_a_w_s, grad_a_w_s, "adam_a_w_s")
    res["a_b_s"] = (grad_a_b_s,) + plain(a_b_s, m_a_b_s, v_a_b_s, grad_a_b_s, "adam_a_b_s")
    res["b_conv_w"] = (grad_b_conv_w,) + plain(b_conv_w, m_b_conv_w, v_b_conv_w, grad_b_conv_w, "adam_b_conv")
    res["c_scale"] = (grad_c_scale,) + plain(c_scale, m_c_scale, v_c_scale, grad_c_scale, "adam_c_scale")

    order = ["norm_mix_g", "norm_ffn_g", "final_norm_g", "a_w_in", "a_v_norm_g", "a_w_s", "a_b_s", "a_w_out", "b_w_in",
             "b_conv_w", "b_w_out", "c_w_in", "c_w_grp", "c_scale", "c_w_out", "f_w_gate", "f_w_up", "f_w_down"]
    return (loss, grad_x, *[res[n][0] for n in order], *[res[n][1] for n in order],
            *[res[n][2] for n in order], *[res[n][3] for n in order])
```

```python
import functools

import jax
import jax.numpy as jnp
from jax import lax
from jax.experimental import pallas as pl
from jax.experimental.pallas import tpu as pltpu

f32 = jnp.float32
bf16 = jnp.bfloat16

NDEV = 8
D = 1024
EPS = 1e-6
CHUNK = 64
GBLK = 128
A_GROUPS = 8
C_GROUPS = 4
C_GDIM = 256
POOL_WINDOWS = (2, 4, 8, 16)
HALO = 16
F_LOC = 352
F_PAD = 384
DEPTH = 4
AXES = ("x", "y", "c")
MESH = pl.DeviceIdType.MESH

ADAM_LR = 0.001
ADAM_B1 = 0.9
ADAM_B2 = 0.999
ADAM_EPS = 1e-08
ADAM_WD = 0.01
ADAM_STEP = 10

VMEM_LIMIT = 56 * 1024 * 1024


def _cparams(sem):
    return pltpu.CompilerParams(dimension_semantics=sem, vmem_limit_bytes=VMEM_LIMIT)


def _gelu(z):
    return 0.5 * z * (1.0 + lax.erf(z * 0.7071067811865476))


def _gelu_grad(z):
    return 0.5 * (1.0 + lax.erf(z * 0.7071067811865476)) + z * jnp.exp(-0.5 * z * z) * 0.3989422804014327


def _dot(a, b):
    return jnp.dot(a, b, preferred_element_type=f32)


def _dot_nt(a, b):
    return lax.dot_general(a, b, (((1,), (1,)), ((), ())), preferred_element_type=f32)


def _dot_tn(a, b):
    return lax.dot_general(a, b, (((0,), (0,)), ((), ())), preferred_element_type=f32)


def norm_mm(x, g, w, *, tm, transposed, swiglu=False, name):
    T = x.shape[0]
    if swiglu:
        nchunk, n = NDEV, w.shape[1]
        w_spec = pl.BlockSpec((None, n, D), lambda i, d: (d, 0, 0))
    else:
        nchunk, n = 1, (NDEV * w.shape[1] if transposed else w.shape[2])
        w_spec = pl.BlockSpec(w.shape, lambda i, d: (0, 0, 0))

    def body(x_ref, g_ref, w_ref, p_ref, h_ref, *rest):
        @pl.when(pl.program_id(1) == 0)
        def _():
            xv = x_ref[...]
            r = lax.rsqrt(jnp.mean(xv * xv, axis=-1, keepdims=True) + EPS)
            h_ref[...] = (xv * r * g_ref[...]).astype(bf16)

        if swiglu:
            p = _dot_nt(h_ref[...], w_ref[...])
            gate, up = p[:, : n // 2], p[:, n // 2:]
            rest[0][...] = (gate * jax.nn.sigmoid(gate) * up).astype(bf16)
        elif transposed:
            p = _dot_nt(h_ref[...], w_ref[...].reshape(n, D))
        else:
            p = _dot(h_ref[...], w_ref[...].reshape(D, n))
        p_ref[...] = p.astype(p_ref.dtype)

    out_shape = [jax.ShapeDtypeStruct((T, nchunk * n), bf16 if swiglu else f32), jax.ShapeDtypeStruct((T, D), bf16)]
    out_specs = [pl.BlockSpec((tm, n), lambda i, d: (i, d)), pl.BlockSpec((tm, D), lambda i, d: (i, 0))]
    if swiglu:
        out_shape.append(jax.ShapeDtypeStruct((T, nchunk * n // 2), bf16))
        out_specs.append(pl.BlockSpec((tm, n // 2), lambda i, d: (i, d)))
    return pl.pallas_call(
        body, grid=(T // tm, nchunk),
        in_specs=[pl.BlockSpec((tm, D), lambda i, d: (i, 0)), pl.BlockSpec((1, D), lambda i, d: (0, 0)), w_spec],
        out_specs=out_specs, out_shape=out_shape,
        compiler_params=_cparams(("parallel", "arbitrary")), name=name,
    )(x, g, w)


def mm_res(y, w, x, *, tm, name):
    T, K = y.shape
    k = w.shape[1]

    def body(y_ref, w_ref, x_ref, o_ref):
        o_ref[...] = x_ref[...] + _dot(y_ref[...], w_ref[...].reshape(K, D))

    return pl.pallas_call(
        body, grid=(T // tm,),
        in_specs=[pl.BlockSpec((tm, K), lambda i: (i, 0)),
                  pl.BlockSpec((NDEV, k, D), lambda i: (0, 0, 0)),
                  pl.BlockSpec((tm, D), lambda i: (i, 0))],
        out_specs=pl.BlockSpec((tm, D), lambda i: (i, 0)),
        out_shape=jax.ShapeDtypeStruct((T, D), f32),
        compiler_params=_cparams(("parallel",)), name=name,
    )(y, w, x)


def mm_nt(dy, w, *, tm, name):
    T = dy.shape[0]
    k = w.shape[1]
    K = NDEV * k

    def body(dy_ref, w_ref, o_ref, dyb_ref):
        dyb = dy_ref[...].astype(bf16)
        dyb_ref[...] = dyb
        o_ref[...] = _dot_nt(dyb, w_ref[...].reshape(K, D))

    return pl.pallas_call(
        body, grid=(T // tm,),
        in_specs=[pl.BlockSpec((tm, D), lambda i: (i, 0)),
                  pl.BlockSpec((NDEV, k, D), lambda i: (0, 0, 0))],
        out_specs=[pl.BlockSpec((tm, K), lambda i: (i, 0)), pl.BlockSpec((tm, D), lambda i: (i, 0))],
        out_shape=[jax.ShapeDtypeStruct((T, K), f32), jax.ShapeDtypeStruct((T, D), bf16)],
        compiler_params=_cparams(("parallel",)), name=name,
    )(dy, w)


def ffn_bwd_mid(dx, wd, pf, *, tm, name):
    T = dx.shape[0]
    H = NDEV * F_PAD

    def body(dx_ref, w_ref, p_ref, dp_ref, dxb_ref):
        dxb = dx_ref[...].astype(bf16)
        dxb_ref[...] = dxb
        da_all = _dot_nt(dxb, w_ref[...].reshape(H, D))
        for d in range(NDEV):
            c0 = 2 * F_PAD * d
            da = da_all[:, F_PAD * d:F_PAD * (d + 1)]
            gate = p_ref[:, c0:c0 + F_PAD].astype(f32)
            up = p_ref[:, c0 + F_PAD:c0 + 2 * F_PAD].astype(f32)
            sg = jax.nn.sigmoid(gate)
            dp_ref[:, c0:c0 + F_PAD] = (da * up * (sg * (1.0 + gate * (1.0 - sg)))).astype(bf16)
            dp_ref[:, c0 + F_PAD:c0 + 2 * F_PAD] = (da * gate * sg).astype(bf16)

    return pl.pallas_call(
        body, grid=(T // tm,),
        in_specs=[pl.BlockSpec((tm, D), lambda i: (i, 0)),
                  pl.BlockSpec((NDEV, F_PAD, D), lambda i: (0, 0, 0), pipeline_mode=pl.Buffered(1)),
                  pl.BlockSpec((tm, 2 * H), lambda i: (i, 0))],
        out_specs=[pl.BlockSpec((tm, 2 * H), lambda i: (i, 0)), pl.BlockSpec((tm, D), lambda i: (i, 0))],
        out_shape=[jax.ShapeDtypeStruct(pf.shape, bf16), jax.ShapeDtypeStruct((T, D), bf16)],
        compiler_params=_cparams(("parallel",)), name=name,
    )(dx, wd, pf)


def mm_normbwd(dp, w, x, g, dres, *, tm, transposed, name):
    T, N = dp.shape

    def body(dp_ref, w_ref, x_ref, g_ref, dres_ref, dx_ref, dg_ref):
        @pl.when(pl.program_id(0) == 0)
        def _():
            dg_ref[...] = jnp.zeros(dg_ref.shape, f32)

        if transposed:
            dh = _dot(dp_ref[...], w_ref[...].reshape(N, D))
        else:
            dh = _dot_nt(dp_ref[...], w_ref[...].reshape(D, N))
        xv = x_ref[...]
        r = lax.rsqrt(jnp.mean(xv * xv, axis=-1, keepdims=True) + EPS)
        xhat = xv * r
        dg_ref[...] += jnp.sum(dh * xhat, axis=0, keepdims=True)
        dxhat = dh * g_ref[...]
        dx_ref[...] = dres_ref[...] + r * (dxhat - xhat * jnp.mean(dxhat * xhat, axis=-1, keepdims=True))

    return pl.pallas_call(
        body, grid=(T // tm,),
        in_specs=[pl.BlockSpec((tm, N), lambda i: (i, 0)),
                  pl.BlockSpec(w.shape, lambda i: (0, 0, 0), pipeline_mode=pl.Buffered(1)),
                  pl.BlockSpec((tm, D), lambda i: (i, 0)), pl.BlockSpec((1, D), lambda i: (0, 0)),
                  pl.BlockSpec((tm, D), lambda i: (i, 0))],
        out_specs=[pl.BlockSpec((tm, D), lambda i: (i, 0)), pl.BlockSpec((1, D), lambda i: (0, 0))],
        out_shape=[jax.ShapeDtypeStruct((T, D), f32), jax.ShapeDtypeStruct((1, D), f32)],
        compiler_params=_cparams(("arbitrary",)), name=name,
    )(dp, w, x, g, dres)


def mm_tn(a, b, *, name):
    T = a.shape[0]
    k, n = a.shape[1] // NDEV, b.shape[1]

    def body(a_ref, b_ref, o_ref):
        o_ref[...] = _dot_tn(a_ref[...], b_ref[...]).astype(bf16)

    return pl.pallas_call(
        body, grid=(NDEV,),
        in_specs=[pl.BlockSpec((T, k), lambda d: (0, d)),
                  pl.BlockSpec((T, n), lambda d: (0, 0), pipeline_mode=pl.Buffered(1))],
        out_specs=pl.BlockSpec((None, k, n), lambda d: (d, 0, 0)),
        out_shape=jax.ShapeDtypeStruct((NDEV, k, n), bf16),
        compiler_params=_cparams(("parallel",)), name=name,
    )(a, b)


def _prev_halo_spec(tm, ncol):
    return pl.BlockSpec((HALO, ncol), lambda i: (jnp.maximum(i * (tm // HALO) - 1, 0), 0))


def _next_halo_spec(tm, ncol, T):
    return pl.BlockSpec((HALO, ncol), lambda i: (jnp.minimum((i + 1) * (tm // HALO), T // HALO - 1), 0))


def a_mid_fwd(p, gv, wm, bsb, *, tm, name):
    T = p.shape[0]

    def body(p_ref, gv_ref, wm_ref, bs_ref, y_ref, vn_sc):
        v = _gelu(p_ref[:, D:])
        vc = v - jnp.mean(v, axis=-1, keepdims=True)
        var = jnp.mean(vc * vc, axis=-1, keepdims=True)
        vn_sc[...] = (vc * lax.rsqrt(var + EPS) * gv_ref[...]).astype(bf16)
        for g in range(A_GROUPS):
            cs = slice(g * GBLK, (g + 1) * GBLK)
            for r in range(tm // GBLK):
                rs = slice(r * GBLK, (r + 1) * GBLK)
                sv = _dot(wm_ref[g], vn_sc[rs, cs]) + bs_ref[g]
                y_ref[rs, cs] = (_gelu(p_ref[rs, cs]) * sv).astype(bf16)

    return pl.pallas_call(
        body, grid=(T // tm,),
        in_specs=[pl.BlockSpec((tm, 2 * D), lambda i: (i, 0)), pl.BlockSpec((1, D), lambda i: (0, 0)),
                  pl.BlockSpec((A_GROUPS, GBLK, GBLK), lambda i: (0, 0, 0)),
                  pl.BlockSpec((A_GROUPS, GBLK, GBLK), lambda i: (0, 0, 0))],
        out_specs=pl.BlockSpec((tm, D), lambda i: (i, 0)),
        out_shape=jax.ShapeDtypeStruct((T, D), bf16),
        scratch_shapes=[pltpu.VMEM((tm, D), bf16)],
        compiler_params=_cparams(("parallel",)), name=name,
    )(p, gv, wm, bsb)


def a_mid_bwd(p, dy, gv, wm, bsb, *, tm, name):
    T = p.shape[0]

    def body(p_ref, dy_ref, gv_ref, wm_ref, bs_ref, dp_ref, dgv_ref, dwm_ref, dbs_ref, vn_sc, dvn_sc):
        @pl.when(pl.program_id(0) == 0)
        def _():
            dgv_ref[...] = jnp.zeros(dgv_ref.shape, f32)
            dwm_ref[...] = jnp.zeros(dwm_ref.shape, f32)
            dbs_ref[...] = jnp.zeros(dbs_ref.shape, f32)

        zv = p_ref[:, D:]
        v = _gelu(zv)
        vc = v - jnp.mean(v, axis=-1, keepdims=True)
        rstd = lax.rsqrt(jnp.mean(vc * vc, axis=-1, keepdims=True) + EPS)
        vhat = vc * rstd
        vn_sc[...] = (vhat * gv_ref[...]).astype(bf16)
        for g in range(A_GROUPS):
            cs = slice(g * GBLK, (g + 1) * GBLK)
            dwm = jnp.zeros((GBLK, GBLK), f32)
            dbs = jnp.zeros((GBLK, 1), f32)
            for r in range(tm // GBLK):
                rs = slice(r * GBLK, (r + 1) * GBLK)
                zu = p_ref[rs, cs]
                vn = vn_sc[rs, cs]
                sv = _dot(wm_ref[g], vn) + bs_ref[g]
                dyb = dy_ref[rs, cs]
                dsv = dyb * _gelu(zu)
                dsvb = dsv.astype(bf16)
                dp_ref[rs, cs] = (dyb * sv * _gelu_grad(zu)).astype(bf16)
                dwm += _dot_nt(dsvb, vn)
                dbs += jnp.sum(dsv, axis=1, keepdims=True)
                dvn_sc[rs, cs] = _dot_tn(wm_ref[g], dsvb)
            dwm_ref[g] += dwm
            dbs_ref[g] += dbs
        dvn = dvn_sc[...]
        dgv_ref[...] += jnp.sum(dvn * vhat, axis=0, keepdims=True)
        dvhat = dvn * gv_ref[...]
        dv = rstd * (dvhat - jnp.mean(dvhat, axis=-1, keepdims=True)
                     - vhat * jnp.mean(dvhat * vhat, axis=-1, keepdims=True))
        dp_ref[:, D:] = (dv * _gelu_grad(zv)).astype(bf16)

    return pl.pallas_call(
        body, grid=(T // tm,),
        in_specs=[pl.BlockSpec((tm, 2 * D), lambda i: (i, 0)), pl.BlockSpec((tm, D), lambda i: (i, 0)),
                  pl.BlockSpec((1, D), lambda i: (0, 0)),
                  pl.BlockSpec((A_GROUPS, GBLK, GBLK), lambda i: (0, 0, 0)),
                  pl.BlockSpec((A_GROUPS, GBLK, GBLK), lambda i: (0, 0, 0))],
        out_specs=[pl.BlockSpec((tm, 2 * D), lambda i: (i, 0)), pl.BlockSpec((1, D), lambda i: (0, 0)),
                   pl.BlockSpec((A_GROUPS, GBLK, GBLK), lambda i: (0, 0, 0)),
                   pl.BlockSpec((A_GROUPS, GBLK, 1), lambda i: (0, 0, 0))],
        out_shape=[jax.ShapeDtypeStruct((T, 2 * D), bf16), jax.ShapeDtypeStruct((1, D), f32),
                   jax.ShapeDtypeStruct((A_GROUPS, GBLK, GBLK), f32), jax.ShapeDtypeStruct((A_GROUPS, GBLK, 1), f32)],
        scratch_shapes=[pltpu.VMEM((tm, D), bf16), pltpu.VMEM((tm, D), f32)],
        compiler_params=_cparams(("arbitrary",)), name=name,
    )(p, dy, gv, wm, bsb)


def b_mid_fwd(p, cw, *, tm, S, name):
    T = p.shape[0]
    nst = S // tm

    def body(p_ref, ph_ref, cw_ref, y_ref, ext):
        first = (pl.program_id(0) % nst) == 0
        q = p_ref[:, D:2 * D] * p_ref[:, 2 * D:]
        qh = ph_ref[:, D:2 * D] * ph_ref[:, 2 * D:]
        ext[0:HALO, :] = jnp.where(first, 0.0, qh)
        ext[HALO:, :] = q
        y = cw_ref[0:1, :] * ext[pl.ds(HALO - 2, tm), :] + cw_ref[1:2, :] * ext[pl.ds(HALO - 1, tm), :] + cw_ref[2:3, :] * q
        y_ref[...] = (p_ref[:, :D] * y).astype(bf16)

    return pl.pallas_call(
        body, grid=(T // tm,),
        in_specs=[pl.BlockSpec((tm, 3 * D), lambda i: (i, 0)), _prev_halo_spec(tm, 3 * D),
                  pl.BlockSpec((3, D), lambda i: (0, 0))],
        out_specs=pl.BlockSpec((tm, D), lambda i: (i, 0)),
        out_shape=jax.ShapeDtypeStruct((T, D), bf16),
        scratch_shapes=[pltpu.VMEM((tm + HALO, D), f32)],
        compiler_params=_cparams(("parallel",)), name=name,
    )(p, p, cw)


def b_mid_bwd(p, dy, cw, *, tm, S, name):
    T = p.shape[0]
    nst = S // tm

    def body(p_ref, ph_ref, pn_ref, dy_ref, dyn_ref, cw_ref, dp_ref, dcw_ref, ext, ext2):
        i = pl.program_id(0)
        first = (i % nst) == 0
        last = (i % nst) == nst - 1

        @pl.when(i == 0)
        def _():
            dcw_ref[...] = jnp.zeros(dcw_ref.shape, f32)

        gb, gc, xt = p_ref[:, :D], p_ref[:, D:2 * D], p_ref[:, 2 * D:]
        q = gc * xt
        ext[0:HALO, :] = jnp.where(first, 0.0, ph_ref[:, D:2 * D] * ph_ref[:, 2 * D:])
        ext[HALO:, :] = q
        q2 = ext[pl.ds(HALO - 2, tm), :]
        q1 = ext[pl.ds(HALO - 1, tm), :]
        y = cw_ref[0:1, :] * q2 + cw_ref[1:2, :] * q1 + cw_ref[2:3, :] * q
        dyo = dy_ref[...]
        dp_ref[:, :D] = (dyo * y).astype(bf16)
        dyc = dyo * gb
        ext2[0:tm, :] = dyc
        ext2[tm:, :] = jnp.where(last, 0.0, dyn_ref[...] * pn_ref[:, :D])
        dq = cw_ref[2:3, :] * dyc + cw_ref[1:2, :] * ext2[pl.ds(1, tm), :] + cw_ref[0:1, :] * ext2[pl.ds(2, tm), :]
        dp_ref[:, D:2 * D] = (dq * xt).astype(bf16)
        dp_ref[:, 2 * D:] = (dq * gc).astype(bf16)
        dcw_ref[0:1, :] += jnp.sum(dyc * q2, axis=0, keepdims=True)
        dcw_ref[1:2, :] += jnp.sum(dyc * q1, axis=0, keepdims=True)
        dcw_ref[2:3, :] += jnp.sum(dyc * q, axis=0, keepdims=True)

    return pl.pallas_call(
        body, grid=(T // tm,),
        in_specs=[pl.BlockSpec((tm, 3 * D), lambda i: (i, 0)), _prev_halo_spec(tm, 3 * D), _next_halo_spec(tm, 3 * D, T),
                  pl.BlockSpec((tm, D), lambda i: (i, 0)), _next_halo_spec(tm, D, T),
                  pl.BlockSpec((3, D), lambda i: (0, 0))],
        out_specs=[pl.BlockSpec((tm, 3 * D), lambda i: (i, 0)), pl.BlockSpec((3, D), lambda i: (0, 0))],
        out_shape=[jax.ShapeDtypeStruct((T, 3 * D), bf16), jax.ShapeDtypeStruct((3, D), f32)],
        scratch_shapes=[pltpu.VMEM((tm + HALO, D), f32), pltpu.VMEM((tm + HALO, D), f32)],
        compiler_params=_cparams(("arbitrary",)), name=name,
    )(p, p, p, dy, dy, cw)


def _pool_counts(i, nst, tm, rows, row0, w):
    t = (i % nst) * tm + row0 + lax.broadcasted_iota(jnp.int32, (rows, 1), 0)
    return jnp.minimum(t + 1, w).astype(f32)


def _pool_diff(p_ref, ext, g, i, nst, tm):
    w = POOL_WINDOWS[g]
    cs = slice(g * C_GDIM, (g + 1) * C_GDIM)
    pg = p_ref[:, cs]
    s = pg
    for k in range(1, w):
        s = s + ext[pl.ds(HALO - k, tm), cs]
    return s / _pool_counts(i, nst, tm, tm, 0, w) - pg


def c_mid_fwd(p, wg, scale, *, tm, S, name):
    T = p.shape[0]
    nst = S // tm

    def body(p_ref, ph_ref, wg_ref, sc_ref, y_ref, ext):
        i = pl.program_id(0)
        first = (i % nst) == 0
        ext[0:HALO, :] = jnp.where(first, 0.0, ph_ref[...])
        ext[HALO:, :] = p_ref[...]
        for g in range(C_GROUPS):
            cs = slice(g * C_GDIM, (g + 1) * C_GDIM)
            dg = _pool_diff(p_ref, ext, g, i, nst, tm).astype(bf16)
            wv = wg_ref[:, g].reshape(C_GDIM, C_GDIM)
            y_ref[:, cs] = (_dot(dg, wv) * sc_ref[:, cs]).astype(bf16)

    return pl.pallas_call(
        body, grid=(T // tm,),
        in_specs=[pl.BlockSpec((tm, D), lambda i: (i, 0)), _prev_halo_spec(tm, D),
                  pl.BlockSpec((NDEV, C_GROUPS, C_GDIM // NDEV, C_GDIM), lambda i: (0, 0, 0, 0)),
                  pl.BlockSpec((1, D), lambda i: (0, 0))],
        out_specs=pl.BlockSpec((tm, D), lambda i: (i, 0)),
        out_shape=jax.ShapeDtypeStruct((T, D), bf16),
        scratch_shapes=[pltpu.VMEM((tm + HALO, D), f32)],
        compiler_params=_cparams(("parallel",)), name=name,
    )(p, p, wg, scale)


def c_mid_bwd(p, dy, wg, scale, *, tm, S, name):
    T = p.shape[0]
    nst = S // tm

    def body(p_ref, ph_ref, dy_ref, dyn_ref, wg_ref, sc_ref, dp_ref, dsc_ref, dwg_ref, ext, ext2):
        i = pl.program_id(0)
        first = (i % nst) == 0
        last = (i % nst) == nst - 1

        @pl.when(i == 0)
        def _():
            dsc_ref[...] = jnp.zeros(dsc_ref.shape, f32)
            dwg_ref[...] = jnp.zeros(dwg_ref.shape, f32)

        ext[0:HALO, :] = jnp.where(first, 0.0, ph_ref[...])
        ext[HALO:, :] = p_ref[...]
        for g in range(C_GROUPS):
            w = POOL_WINDOWS[g]
            cs = slice(g * C_GDIM, (g + 1) * C_GDIM)
            dg = _pool_diff(p_ref, ext, g, i, nst, tm).astype(bf16)
            wv = wg_ref[:, g].reshape(C_GDIM, C_GDIM)
            dyo = dy_ref[:, cs]
            dsc_ref[:, cs] += jnp.sum(dyo * _dot(dg, wv), axis=0, keepdims=True)
            dyp = (dyo * sc_ref[:, cs]).astype(bf16)
            dypn = (dyn_ref[:, cs] * sc_ref[:, cs]).astype(bf16)
            dwg_ref[g] += _dot_tn(dg, dyp)
            dd = _dot_nt(dyp, wv)
            ddn = _dot_nt(dypn, wv)
            ext2[0:tm, cs] = dd / _pool_counts(i, nst, tm, tm, 0, w)
            ext2[tm:, cs] = jnp.where(last, 0.0, ddn / _pool_counts(i, nst, tm, HALO, tm, w))
            s = -dd
            for k in range(w):
                s = s + ext2[pl.ds(k, tm), cs]
            dp_ref[:, cs] = s.astype(bf16)

    return pl.pallas_call(
        body, grid=(T // tm,),
        in_specs=[pl.BlockSpec((tm, D), lambda i: (i, 0)), _prev_halo_spec(tm, D),
                  pl.BlockSpec((tm, D), lambda i: (i, 0)), _next_halo_spec(tm, D, T),
                  pl.BlockSpec((NDEV, C_GROUPS, C_GDIM // NDEV, C_GDIM), lambda i: (0, 0, 0, 0)),
                  pl.BlockSpec((1, D), lambda i: (0, 0))],
        out_specs=[pl.BlockSpec((tm, D), lambda i: (i, 0)), pl.BlockSpec((1, D), lambda i: (0, 0)),
                   pl.BlockSpec((C_GROUPS, C_GDIM, C_GDIM), lambda i: (0, 0, 0))],
        out_shape=[jax.ShapeDtypeStruct((T, D), bf16), jax.ShapeDtypeStruct((1, D), f32),
                   jax.ShapeDtypeStruct((C_GROUPS, C_GDIM, C_GDIM), f32)],
        scratch_shapes=[pltpu.VMEM((tm + HALO, D), f32), pltpu.VMEM((tm + HALO, D), f32)],
        compiler_params=_cparams(("arbitrary",)), name=name,
    )(p, p, dy, dy, wg, scale)


def final_loss(x, g, tgt, *, tm, name):
    T = x.shape[0]

    def body(x_ref, g_ref, t_ref, loss_ref, dx_ref, dg_ref):
        @pl.when(pl.program_id(0) == 0)
        def _():
            loss_ref[...] = jnp.zeros(loss_ref.shape, f32)
            dg_ref[...] = jnp.zeros(dg_ref.shape, f32)

        xv = x_ref[...]
        r = lax.rsqrt(jnp.mean(xv * xv, axis=-1, keepdims=True) + EPS)
        xhat = xv * r
        err = xhat * g_ref[...] - t_ref[...]
        loss_ref[...] += 0.5 * jnp.sum(jnp.mean(err * err, axis=-1, keepdims=True))
        dy = err * (1.0 / D)
        dg_ref[...] += jnp.sum(dy * xhat, axis=0, keepdims=True)
        dxhat = dy * g_ref[...]
        dx_ref[...] = r * (dxhat - xhat * jnp.mean(dxhat * xhat, axis=-1, keepdims=True))

    return pl.pallas_call(
        body, grid=(T // tm,),
        in_specs=[pl.BlockSpec((tm, D), lambda i: (i, 0)), pl.BlockSpec((1, D), lambda i: (0, 0)),
                  pl.BlockSpec((tm, D), lambda i: (i, 0))],
        out_specs=[pl.BlockSpec((8, 128), lambda i: (0, 0)), pl.BlockSpec((tm, D), lambda i: (i, 0)),
                   pl.BlockSpec((1, D), lambda i: (0, 0))],
        out_shape=[jax.ShapeDtypeStruct((8, 128), f32), jax.ShapeDtypeStruct((T, D), f32),
                   jax.ShapeDtypeStruct((1, D), f32)],
        compiler_params=_cparams(("arbitrary",)), name=name,
    )(x, g, tgt)


def _slot(px, py, pc):
    return 4 * px + 2 * py + pc


def _with_own_block(s, me):
    zone = lax.empty((NDEV,) + s.shape, s.dtype)
    return lax.dynamic_update_slice(zone, s[None], (me,) + (0,) * s.ndim)


def all_gather(arrs, me, *, name):
    n = len(arrs)

    def body(*refs):
        ins, outs = refs[:n], refs[2 * n:3 * n]
        send_sems, recv_sems = refs[3 * n:]
        x, y, c = lax.axis_index("x"), lax.axis_index("y"), lax.axis_index("c")
        me, sibling = (x, y, c), (x, y, 1 - c)
        chips = [(1 - x, y), (x, 1 - y), (1 - x, 1 - y)]

        def copy(a, k, block, to, src=None):
            dst = outs[a].at[_slot(*block)]
            return pltpu.make_async_remote_copy(
                src_ref=dst if src is None else src, dst_ref=dst,
                send_sem=send_sems.at[a, k], recv_sem=recv_sems.at[a, k], device_id=to, device_id_type=MESH)

        first = []
        for a in range(n):
            first.append(copy(a, 0, me, sibling, src=ins[a]))
            first += [copy(a, 1 + j, me, (*chip, c), src=ins[a]) for j, chip in enumerate(chips)]
        for cp in first:
            cp.start()
        passed = []
        for j, chip in enumerate(chips):
            for a in range(n):
                copy(a, 1 + j, (*chip, c), me).wait_recv()
                fwd = copy(a, 4 + j, (*chip, c), sibling)
                fwd.start()
                passed.append(fwd)
        for a in range(n):
            copy(a, 0, sibling, me).wait_recv()
        for j, chip in enumerate(chips):
            for a in range(n):
                copy(a, 4 + j, (*chip, 1 - c), me).wait_recv()
        for cp in first + passed:
            cp.wait_send()

    any_spec = pl.BlockSpec(memory_space=pl.ANY)
    return pl.pallas_call(
        body,
        in_specs=[any_spec] * (2 * n), out_specs=[any_spec] * n,
        out_shape=[jax.ShapeDtypeStruct((NDEV,) + a.shape, a.dtype) for a in arrs],
        input_output_aliases={n + i: i for i in range(n)},
        scratch_shapes=[pltpu.SemaphoreType.DMA((n, 7)), pltpu.SemaphoreType.DMA((n, 7))],
        compiler_params=pltpu.CompilerParams(has_side_effects=True), name=name,
    )(*arrs, *[_with_own_block(a, me) for a in arrs])


HBM_SPEC = pl.BlockSpec(memory_space=pltpu.HBM)
SEM_SPEC = pl.BlockSpec(memory_space=pltpu.SEMAPHORE)
ANY_SPEC = pl.BlockSpec(memory_space=pl.ANY)
TOKEN_SHAPE = jax.ShapeDtypeStruct((8, 128), f32)
DATAFLOW_EFFECT = pltpu.SideEffectType.DATAFLOW_SIDE_EFFECTING


def _in_hbm(a):
    return pltpu.with_memory_space_constraint(a, pltpu.HBM)


def _hbm_like(a):
    return pltpu.HBM(a.shape, a.dtype)


def _mesh_pos():
    return lax.axis_index("x"), lax.axis_index("y"), lax.axis_index("c")


def _gather_targets(x, y, c):
    return [(x, y, 1 - c), (1 - x, y, c), (x, 1 - y, c), (1 - x, 1 - y, c)]


def gather_start(shards, me, after, *, name):
    n = len(shards)
    extra = [] if after is None else [after]

    def body(*refs):
        srcs, lands = refs[:n], refs[n:2 * n]
        send_sems, recv_sems = refs[2 * n + len(extra)], refs[2 * n + len(extra) + 1]
        token = refs[-1]
        x, y, c = _mesh_pos()
        me = _slot(x, y, c)
        for a in range(n):
            for k, to in enumerate(_gather_targets(x, y, c)):
                pltpu.make_async_remote_copy(
                    src_ref=srcs[a], dst_ref=lands[a].at[me], send_sem=send_sems.at[4 * a + k], recv_sem=recv_sems.at[4 * a + k],
                    device_id=to, device_id_type=MESH).start()
        token[...] = jnp.zeros(token.shape, f32)

    lands = [_with_own_block(s, me) for s in shards]
    sems = pltpu.SemaphoreType.DMA((4 * n,))
    out = pl.pallas_call(
        body, name=name,
        in_specs=[HBM_SPEC] * (2 * n) + [ANY_SPEC] * len(extra),
        out_specs=[SEM_SPEC, SEM_SPEC] + [HBM_SPEC] * (2 * n) + [pl.BlockSpec(memory_space=pltpu.VMEM)],
        out_shape=[sems, sems] + [_hbm_like(s) for s in shards] + [_hbm_like(l) for l in lands] + [TOKEN_SHAPE],
        input_output_aliases={i: 2 + i for i in range(2 * n)},
        compiler_params=pltpu.CompilerParams(has_side_effects=DATAFLOW_EFFECT),
    )(*[_in_hbm(s) for s in shards], *[_in_hbm(l) for l in lands], *extra)
    return out[0], out[1], out[2:2 + n], out[2 + n:2 + 2 * n], out[-1]


def gather_wait(send_sems, recv_sems, shards, lands, after, *, name):
    n = len(shards)
    after = list(after) if isinstance(after, (list, tuple)) else [after]

    def body(*refs):
        srcs, lands_in = refs[:n], refs[n:2 * n]
        send_sems, recv_sems = refs[2 * n], refs[2 * n + 1]
        x, y, c = _mesh_pos()
        for a in range(n):
            for k, frm in enumerate(_gather_targets(x, y, c)):
                cp = pltpu.make_async_remote_copy(
                    src_ref=srcs[a], dst_ref=lands_in[a].at[_slot(*frm)], send_sem=send_sems.at[4 * a + k],
                    recv_sem=recv_sems.at[4 * a + k], device_id=frm, device_id_type=MESH)
                cp.wait_send()
                cp.wait_recv()

    out = pl.pallas_call(
        body, name=name,
        in_specs=[HBM_SPEC] * (2 * n) + [SEM_SPEC, SEM_SPEC] + [ANY_SPEC] * len(after),
        out_specs=[HBM_SPEC] * (2 * n),
        out_shape=[_hbm_like(s) for s in shards] + [_hbm_like(l) for l in lands],
        input_output_aliases={i: i for i in range(2 * n)},
        compiler_params=pltpu.CompilerParams(has_side_effects=DATAFLOW_EFFECT),
    )(*shards, *lands, send_sems, recv_sems, *after)
    return out[:n], out[n:]


def gather_finish(lands, *, name):
    n = len(lands)

    def body(*refs):
        lands_in, lands_out = refs[:n], refs[n:2 * n]
        send_sems, recv_sems = refs[2 * n:]
        x, y, c = _mesh_pos()
        sibling = (x, y, 1 - c)
        chips = [(1 - x, y), (x, 1 - y), (1 - x, 1 - y)]
        sent = []
        for a in range(n):
            for j, chip in enumerate(chips):
                s = _slot(*chip, c)
                cp = pltpu.make_async_remote_copy(
                    src_ref=lands_in[a].at[s], dst_ref=lands_out[a].at[s], send_sem=send_sems.at[a, j],
                    recv_sem=recv_sems.at[a, j], device_id=sibling, device_id_type=MESH)
                cp.start()
                sent.append(cp)
        for a in range(n):
            for j, chip in enumerate(chips):
                s = _slot(*chip, 1 - c)
                pltpu.make_async_remote_copy(
                    src_ref=lands_in[a].at[s], dst_ref=lands_out[a].at[s], send_sem=send_sems.at[a, j],
                    recv_sem=recv_sems.at[a, j], device_id=sibling, device_id_type=MESH).wait_recv()
        for cp in sent:
            cp.wait_send()

    return pl.pallas_call(
        body, name=name,
        in_specs=[ANY_SPEC] * n, out_specs=[ANY_SPEC] * n,
        out_shape=[jax.ShapeDtypeStruct(l.shape, l.dtype) for l in lands],
        input_output_aliases={i: i for i in range(n)},
        scratch_shapes=[pltpu.SemaphoreType.DMA((n, 3)), pltpu.SemaphoreType.DMA((n, 3))],
        compiler_params=pltpu.CompilerParams(has_side_effects=True),
    )(*lands)


def _peer(x, y, c, r):
    return (x ^ ((r >> 2) & 1), y ^ ((r >> 1) & 1), c ^ (r & 1))


def exchange_start(items, lands, after, *, name):
    n, m = len(items), len(lands)
    extra = [] if after is None else [after]

    def body(*refs):
        gs, zones = refs[:n], refs[n:n + m]
        send_sems, recv_sems = refs[n + m + len(extra)], refs[n + m + len(extra) + 1]
        token = refs[-1]
        x, y, c = _mesh_pos()
        me = _slot(x, y, c)
        for r in (1, 2, 4, 3, 5, 6, 7):
            to = _peer(x, y, c, r)
            for a, (_, zi, l) in enumerate(items):
                pltpu.make_async_remote_copy(
                    src_ref=gs[a].at[_slot(*to)], dst_ref=zones[zi].at[me, l], send_sem=send_sems.at[7 * a + r - 1],
                    recv_sem=recv_sems.at[7 * a + r - 1], device_id=to, device_id_type=MESH).start()
        token[...] = jnp.zeros(token.shape, f32)

    gs = [g for g, _, _ in items]
    sems = pltpu.SemaphoreType.DMA((7 * n,))
    out = pl.pallas_call(
        body, name=name,
        in_specs=[HBM_SPEC] * (n + m) + [ANY_SPEC] * len(extra),
        out_specs=[SEM_SPEC, SEM_SPEC] + [HBM_SPEC] * (n + m) + [pl.BlockSpec(memory_space=pltpu.VMEM)],
        out_shape=[sems, sems] + [_hbm_like(g) for g in gs] + [_hbm_like(z) for z in lands] + [TOKEN_SHAPE],
        input_output_aliases={i: 2 + i for i in range(n + m)},
        compiler_params=pltpu.CompilerParams(has_side_effects=DATAFLOW_EFFECT),
    )(*[_in_hbm(g) for g in gs], *[_in_hbm(z) for z in lands], *extra)
    return out[0], out[1], out[2:2 + n], out[2 + n:2 + n + m], out[-1]


def exchange_wait(parts, lands, after, *, name):
    m = len(lands)
    flat_gs = [g for _, _, _, gs in parts for g in gs]
    ng = len(flat_gs)
    after = list(after) if isinstance(after, (list, tuple)) else [after]

    def body(*refs):
        gs, zones = refs[:ng], refs[ng:ng + m]
        sem_refs = refs[ng + m:ng + m + 2 * len(parts)]
        x, y, c = _mesh_pos()
        base = 0
        for pi, (items, _, _, _) in enumerate(parts):
            send_sems, recv_sems = sem_refs[2 * pi], sem_refs[2 * pi + 1]
            for r in range(1, NDEV):
                frm = _peer(x, y, c, r)
                for a, (_, zi, l) in enumerate(items):
                    cp = pltpu.make_async_remote_copy(
                        src_ref=gs[base + a].at[_slot(*frm)], dst_ref=zones[zi].at[_slot(*frm), l],
                        send_sem=send_sems.at[7 * a + r - 1], recv_sem=recv_sems.at[7 * a + r - 1],
                        device_id=frm, device_id_type=MESH)
                    cp.wait_send()
                    cp.wait_recv()
            base += len(items)

    sem_args = [s for _, ss, rs, _ in parts for s in (ss, rs)]
    out = pl.pallas_call(
        body, name=name,
        in_specs=[HBM_SPEC] * (ng + m) + [SEM_SPEC] * len(sem_args) + [ANY_SPEC] * len(after),
        out_specs=[HBM_SPEC] * (ng + m),
        out_shape=[_hbm_like(g) for g in flat_gs] + [_hbm_like(z) for z in lands],
        input_output_aliases={i: i for i in range(ng + m)},
        compiler_params=pltpu.CompilerParams(has_side_effects=DATAFLOW_EFFECT),
    )(*flat_gs, *lands, *sem_args, *after)
    return out[ng:]


def sum8(g, *, name):
    _, R, C = g.shape
    tr = R

    def body(g_ref, o_ref):
        s = g_ref[0]
        for k in range(1, NDEV):
            s = s + g_ref[k]
        o_ref[...] = s

    return pl.pallas_call(
        body, grid=(R // tr,),
        in_specs=[pl.BlockSpec((NDEV, tr, C), lambda i: (0, i, 0))],
        out_specs=pl.BlockSpec((tr, C), lambda i: (i, 0)),
        out_shape=jax.ShapeDtypeStruct((R, C), f32),
        compiler_params=_cparams(("parallel",)), name=name,
    )(g)


def _adam_math(w, g, m, v):
    m = ADAM_B1 * m + (1.0 - ADAM_B1) * g
    v = ADAM_B2 * v + (1.0 - ADAM_B2) * (g * g)
    m_hat = m / (1.0 - ADAM_B1 ** ADAM_STEP)
    v_hat = v / (1.0 - ADAM_B2 ** ADAM_STEP)
    delta = -ADAM_LR * (m_hat / (jnp.sqrt(v_hat) + ADAM_EPS) + ADAM_WD * w)
    return delta, m, v


def adamw_slices(w, m, v, recv, *, row0=0, col0=0, tr=None, tc=None, name):
    L, R, C = w.shape
    Rp, Cp = recv.shape[2], recv.shape[3]
    tr = R if tr is None else tr
    tc = C if tc is None else tc
    assert R % tr == 0 and C % tc == 0
    assert (tr == R or (Rp == R and row0 == 0)) and (tc == C or (Cp == C and col0 == 0))
    rr = Rp if tr == R else tr
    rc = Cp if tc == C else tc

    def body(w_ref, m_ref, v_ref, r_ref, g_ref, d_ref, nm_ref, nv_ref):
        c0 = col0 if tc == C else 0
        r0 = row0 if tr == R else 0
        g = r_ref[0, r0:r0 + tr, c0:c0 + tc].astype(f32)
        for k in range(1, NDEV):
            g = g + r_ref[k, r0:r0 + tr, c0:c0 + tc].astype(f32)
        delta, nm, nv = _adam_math(w_ref[...], g, m_ref[...], v_ref[...])
        g_ref[...] = g
        d_ref[...] = delta
        nm_ref[...] = nm
        nv_ref[...] = nv

    wspec = pl.BlockSpec((None, tr, tc), lambda l, i, j: (l, i, j))
    return pl.pallas_call(
        body, grid=(L, R // tr, C // tc),
        in_specs=[wspec, wspec, wspec, pl.BlockSpec((NDEV, None, rr, rc), lambda l, i, j: (0, l, i, j))],
        out_specs=[wspec] * 4, out_shape=[jax.ShapeDtypeStruct((L, R, C), f32)] * 4,
        compiler_params=_cparams(("parallel", "parallel", "parallel")), name=name,
    )(w, m, v, recv)


def adamw_slices_t(w, m, v, recv, *, name):
    L, R, C = w.shape

    def body(w_ref, m_ref, v_ref, r_ref, g_ref, d_ref, nm_ref, nv_ref):
        gt = r_ref[0].astype(f32)
        for k in range(1, NDEV):
            gt = gt + r_ref[k].astype(f32)
        g = gt.T
        delta, nm, nv = _adam_math(w_ref[...], g, m_ref[...], v_ref[...])
        g_ref[...] = g
        d_ref[...] = delta
        nm_ref[...] = nm
        nv_ref[...] = nv

    wspec = pl.BlockSpec((None, R, C), lambda l: (l, 0, 0))
    return pl.pallas_call(
        body, grid=(L,),
        in_specs=[wspec, wspec, wspec, pl.BlockSpec((NDEV, None, C, R), lambda l: (0, l, 0, 0))],
        out_specs=[wspec] * 4, out_shape=[jax.ShapeDtypeStruct((L, R, C), f32)] * 4,
        compiler_params=_cparams(("parallel",)), name=name,
    )(w, m, v, recv)


def adamw_plain(w, m, v, g, *, name):
    R, C = w.shape

    def body(w_ref, m_ref, v_ref, g_ref, d_ref, nm_ref, nv_ref):
        delta, nm, nv = _adam_math(w_ref[...], g_ref[...], m_ref[...], v_ref[...])
        d_ref[...] = delta
        nm_ref[...] = nm
        nv_ref[...] = nv

    return pl.pallas_call(
        body, out_shape=[jax.ShapeDtypeStruct((R, C), f32)] * 3, name=name,
    )(w, m, v, g)


def _chunk_causal_mask():
    pos = jnp.arange(GBLK)
    return (pos[None, :] // CHUNK) <= (pos[:, None] // CHUNK)


def kernel(x, norm_mix_g, norm_ffn_g, final_norm_g, a_w_in, a_v_norm_g, a_w_s, a_b_s, a_w_out, b_w_in, b_conv_w, b_w_out, c_w_in, c_w_grp, c_scale, c_w_out, f_w_gate, f_w_up, f_w_down, loss_target, m_norm_mix_g, m_norm_ffn_g, m_final_norm_g, m_a_w_in, m_a_v_norm_g, m_a_w_s, m_a_b_s, m_a_w_out, m_b_w_in, m_b_conv_w, m_b_w_out, m_c_w_in, m_c_w_grp, m_c_scale, m_c_w_out, m_f_w_gate, m_f_w_up, m_f_w_down, v_norm_mix_g, v_norm_ffn_g, v_final_norm_g, v_a_w_in, v_a_v_norm_g, v_a_w_s, v_a_b_s, v_a_w_out, v_b_w_in, v_b_conv_w, v_b_w_out, v_c_w_in, v_c_w_grp, v_c_scale, v_c_w_out, v_f_w_gate, v_f_w_up, v_f_w_down):
    B, S, _ = x.shape
    T = B * S
    tm = min(512, S)
    tmm = min(1024, T)
    tms = min(512, T)
    me = _slot(lax.axis_index("x"), lax.axis_index("y"), lax.axis_index("c"))

    pad_r = ((0, 0), (0, F_PAD - F_LOC), (0, 0))

    def tb(w):
        return jnp.transpose(w, (0, 2, 1)).astype(bf16)

    f_gu = jnp.concatenate([jnp.pad(tb(f_w_gate), pad_r), jnp.pad(tb(f_w_up), pad_r)], axis=1)
    f_d = jnp.pad(f_w_down.astype(bf16), pad_r)
    small = jnp.concatenate([a_v_norm_g, b_conv_w[0], c_scale, jnp.zeros((2, GBLK), f32)], axis=0)
    a_in_b, a_out_b = tb(a_w_in), a_w_out.astype(bf16)
    stage_shards = [
        [a_in_b[0], a_out_b[0], small],
        [f_gu[0], f_d[0]],
        [tb(b_w_in)[0], b_w_out[0].astype(bf16), f_gu[1], f_d[1]],
        [c_w_in[0].astype(bf16), c_w_grp[0].astype(bf16), c_w_out[0].astype(bf16), f_gu[2], f_d[2]],
        [a_in_b[1], a_out_b[1], f_gu[3], f_d[3]],
    ]
    started, tok = [], None
    for s in range(len(stage_shards)):
        started.append(gather_start(stage_shards[s], me, tok, name=f"gather_start_{s}"))
        tok = started[-1][4]

    def stage_weights(s, after):
        send_sems, recv_sems, shards, lands, _ = started[s]
        _, lands = gather_wait(send_sems, recv_sems, shards, lands, after, name=f"gather_wait_{s}")
        return gather_finish(lands, name=f"gather_finish_{s}")

    wm_masked = jnp.where(_chunk_causal_mask()[None, None], a_w_s, 0.0).astype(bf16)
    bsb = jnp.broadcast_to(a_b_s[:, :, :, None], a_b_s.shape + (GBLK,))

    xc = x.reshape(T, D)
    saved, weights = [], []
    for i in range(DEPTH):
        kind, j = i % 3, i // 3
        gmix = norm_mix_g[i][None]
        if i == 0:
            w_in, w_out, small_g = stage_weights(0, tok)
            small_full = jnp.transpose(small_g, (1, 0, 2)).reshape(8, D)
            gv_full, cw_full, cs_full = small_full[0:2], small_full[2:5], small_full[5:6]
            w = [w_in, w_out]
        else:
            w = stage_weights(i + 1, xc)
        if kind == 0:
            w_in, w_out = w[:2]
            p, h = norm_mm(xc, gmix, w_in, tm=tms, transposed=True, name=f"a_in_{i}")
            y = a_mid_fwd(p, gv_full[j][None], wm_masked[j], bsb[j], tm=tm, name=f"a_mid_{i}")
        elif kind == 1:
            w_in, w_out = w[:2]
            p, h = norm_mm(xc, gmix, w_in, tm=tms, transposed=True, name=f"b_in_{i}")
            y = b_mid_fwd(p, cw_full, tm=tm, S=S, name=f"b_mid_{i}")
        else:
            w_in, w_grp, w_out = w[:3]
            p, h = norm_mm(xc, gmix, w_in, tm=tms, transposed=False, name=f"c_in_{i}")
            y = c_mid_fwd(p, w_grp, cs_full, tm=tm, S=S, name=f"c_mid_{i}")
        x1 = mm_res(y, w_out, xc, tm=tms, name=f"mix_out_{i}")
        if i == 0:
            w = w + list(stage_weights(1, x1))
        weights.append(w)
        w_fgu, w_fd = w[-2], w[-1]
        pf, h2, a = norm_mm(x1, norm_ffn_g[i][None], w_fgu, tm=min(2048, T), transposed=True, swiglu=True, name=f"f_in_{i}")
        x2 = mm_res(a, w_fd, x1, tm=tms, name=f"f_out_{i}")
        saved.append((xc, p, h, y, x1, pf, h2, a))
        xc = x2

    loss_blk, dx, d_final = final_loss(xc, final_norm_g[None], loss_target.reshape(T, D), tm=tms, name="final_loss")
    loss = lax.psum(loss_blk[0, 0], AXES)

    zone_shapes = [(2, 256, D), (2, GBLK, D), (1, 384, D), (1, GBLK, D), (1, GBLK, D), (1, GBLK, C_GDIM), (1, GBLK, D),
                   (DEPTH, 2 * F_PAD, D), (DEPTH, F_PAD, D)]
    Z_AIN, Z_AOUT, Z_BIN, Z_BOUT, Z_CIN, Z_CGRP, Z_COUT, Z_FGU, Z_FD = range(9)
    lands = [lax.empty((NDEV,) + s, bf16) for s in zone_shapes]
    parts = []

    def exchange(items, after, name):
        nonlocal lands
        for g, zi, l in items:
            own = lax.dynamic_index_in_dim(g, me, axis=0, keepdims=True)[None]
            lands[zi] = lax.dynamic_update_slice(lands[zi], own, (me, l) + (0,) * (g.ndim - 1))
        idx = sorted({zi for _, zi, _ in items})
        local = [(g, idx.index(zi), l) for g, zi, l in items]
        send_sems, recv_sems, gs, zs, token = exchange_start(local, [lands[zi] for zi in idx], after, name=name)
        for q, zi in enumerate(idx):
            lands[zi] = zs[q]
        parts.append(([(None, zi, l) for _, zi, l in items], send_sems, recv_sems, gs))
        return token

    d_mix, d_ffn = [None] * DEPTH, [None] * DEPTH
    d_gv, d_wm, d_bs = [None] * 2, [None] * 2, [None] * 2
    tok = None
    for i in reversed(range(DEPTH)):
        kind, j = i % 3, i // 3
        xin, p, h, y, x1, pf, h2, a = saved[i]
        w = weights[i]
        w_in, w_out, w_fgu, w_fd = w[0], w[-3], w[-2], w[-1]
        gffn = norm_ffn_g[i][None] if tok is None else norm_ffn_g[i][None] + tok[0:1, 0:1]
        dpf, dxb = ffn_bwd_mid(dx, w_fd, pf, tm=tms, name=f"f_bwd_mid_{i}")
        items = [(mm_tn(dpf, h2, name=f"f_dw_in_{i}"), Z_FGU, i),
                 (mm_tn(a, dxb, name=f"f_dw_out_{i}"), Z_FD, i)]
        if i == 0:
            tok = exchange(items, tok, "exchange_start_ffn0")
            gffn = gffn + tok[0:1, 0:1]
            items = []
        dx1, d_ffn[i] = mm_normbwd(dpf, w_fgu, x1, gffn, dx, tm=tms, transposed=True, name=f"f_bwd_in_{i}")
        dy, dx1b = mm_nt(dx1, w_out, tm=tmm, name=f"mix_bwd_out_{i}")
        if kind == 0:
            dp, d_gv[j], d_wm[j], d_bs[j] = a_mid_bwd(p, dy, gv_full[j][None], wm_masked[j], bsb[j], tm=tm, name=f"a_bwd_mid_{i}")
            items += [(mm_tn(dp, h, name=f"a_dw_in_{i}"), Z_AIN, j),
                      (mm_tn(y, dx1b, name=f"a_dw_out_{i}"), Z_AOUT, j)]
        elif kind == 1:
            dp, d_cw = b_mid_bwd(p, dy, cw_full, tm=tm, S=S, name=f"b_bwd_mid_{i}")
            items += [(mm_tn(dp, h, name=f"b_dw_in_{i}"), Z_BIN, 0),
                      (mm_tn(y, dx1b, name=f"b_dw_out_{i}"), Z_BOUT, 0)]
        else:
            dp, d_cs, d_wgrp = c_mid_bwd(p, dy, w[1], cs_full, tm=tm, S=S, name=f"c_bwd_mid_{i}")
            gw_cgrp = jnp.transpose(d_wgrp.reshape(C_GROUPS, NDEV, C_GDIM // NDEV, C_GDIM), (1, 0, 2, 3)).astype(bf16)
            items += [(mm_tn(h, dp, name=f"c_dw_in_{i}"), Z_CIN, 0),
                      (gw_cgrp.reshape(NDEV, GBLK, C_GDIM), Z_CGRP, 0),
                      (mm_tn(y, dx1b, name=f"c_dw_out_{i}"), Z_COUT, 0)]
        tok = exchange(items, tok, f"exchange_start_{i}")
        gmix = norm_mix_g[i][None] + tok[0:1, 0:1]
        dx, d_mix[i] = mm_normbwd(dp, w_in, xin, gmix, dx1, tm=tms, transposed=(kind != 2), name=f"mix_bwd_in_{i}")
    grad_x = dx.reshape(B, S, D)

    mask = _chunk_causal_mask()
    small_parts = [
        jnp.concatenate(d_mix, axis=0).reshape(-1, GBLK),
        jnp.concatenate(d_ffn, axis=0).reshape(-1, GBLK),
        d_final.reshape(-1, GBLK),
        jnp.concatenate(d_gv, axis=0).reshape(-1, GBLK),
        jnp.where(mask[None, None], jnp.stack(d_wm), 0.0).reshape(-1, GBLK),
        jnp.stack(d_bs).reshape(-1, GBLK),
        d_cw.reshape(-1, GBLK),
        d_cs.reshape(-1, GBLK),
    ]
    small_rows = [q.shape[0] for q in small_parts]
    sg_send, sg_recv, sg_shards, sg_lands, _ = gather_start(
        [jnp.concatenate(small_parts, axis=0)], me, tok, name="gather_small_start")

    def plain(w, m, v, g, name):
        shp = w.shape
        w2, m2, v2, g2 = (t.reshape(-1, shp[-1]) for t in (w, m, v, g))
        return tuple(t.reshape(shp) for t in adamw_plain(w2, m2, v2, g2, name=name))

    def sliced(w, m, v, recv, name, **kw):
        shp = w.shape
        w3, m3, v3 = (t.reshape((shp[0], -1, shp[-1])) for t in (w, m, v))
        return tuple(t.reshape(shp) for t in adamw_slices(w3, m3, v3, recv, name=name, **kw))

    def sliced_t(w, m, v, recv, name, **kw):
        wt, mt, vt = (jnp.transpose(t, (0, 2, 1)) for t in (w, m, v))
        return tuple(jnp.transpose(t, (0, 2, 1)) for t in adamw_slices(wt, mt, vt, recv, name=name, **kw))

    lands = list(exchange_wait(parts[:-1], lands, dx, name="exchange_wait"))
    _, _, r_bin, r_bout, r_cin, r_cgrp, r_cout, r_fgu, r_fd = lands
    res = {}
    res["b_w_in"] = adamw_slices_t(b_w_in, m_b_w_in, v_b_w_in, r_bin, name="adam_b_w_in")
    res["b_w_out"] = sliced(b_w_out, m_b_w_out, v_b_w_out, r_bout, "adam_b_w_out")
    res["c_w_in"] = sliced(c_w_in, m_c_w_in, v_c_w_in, r_cin, "adam_c_w_in")
    res["c_w_grp"] = sliced(c_w_grp, m_c_w_grp, v_c_w_grp, r_cgrp, "adam_c_w_grp")
    res["c_w_out"] = sliced(c_w_out, m_c_w_out, v_c_w_out, r_cout, "adam_c_w_out")
    res["f_w_gate"] = sliced_t(f_w_gate, m_f_w_gate, v_f_w_gate, r_fgu, "adam_f_gate", row0=0, tc=256)
    res["f_w_up"] = sliced_t(f_w_up, m_f_w_up, v_f_w_up, r_fgu, "adam_f_up", row0=F_PAD, tc=256)
    res["f_w_down"] = sliced(f_w_down, m_f_w_down, v_f_w_down, r_fd, "adam_f_down", tc=256)

    last_items, last_send, last_recv, last_gs = parts[-1]
    last_zones = [Z_AIN, Z_AOUT]
    last_part = ([(None, last_zones.index(zi), l) for _, zi, l in last_items], last_send, last_recv, last_gs)
    big = [res[n][1] for n in ("b_w_in", "b_w_out", "c_w_in", "c_w_grp", "c_w_out", "f_w_gate", "f_w_up", "f_w_down")]
    r_ain, r_aout = exchange_wait([last_part], [lands[zi] for zi in last_zones], big, name="exchange_wait_last")
    res["a_w_in"] = adamw_slices_t(a_w_in, m_a_w_in, v_a_w_in, r_ain, name="adam_a_w_in")
    res["a_w_out"] = sliced(a_w_out, m_a_w_out, v_a_w_out, r_aout, "adam_a_w_out")

    _, sg_lands = gather_wait(sg_send, sg_recv, sg_shards, sg_lands, [res["a_w_in"][1], res["a_w_out"][1]], name="gather_small_wait")
    (gs_all,) = gather_finish(sg_lands, name="gather_small_finish")
    gs = sum8(gs_all, name="sum_small_grads")
    offs = [0]
    for r in small_rows:
        offs.append(offs[-1] + r)
    sp = [gs[offs[q]:offs[q + 1]] for q in range(len(small_parts))]
    grad_norm_mix_g = sp[0].reshape(DEPTH, D)
    grad_norm_ffn_g = sp[1].reshape(DEPTH, D)
    grad_final_norm_g = sp[2].reshape(D)
    grad_a_w_s = sp[4].reshape(a_w_s.shape)
    grad_a_b_s = sp[5].reshape(a_b_s.shape)

    def my_cols(full):
        return lax.dynamic_slice_in_dim(full, me * GBLK, GBLK, axis=1)

    grad_a_v_norm_g = my_cols(sp[3].reshape(2, D))
    grad_b_conv_w = my_cols(sp[6].reshape(3, D))[None]
    grad_c_scale = my_cols(sp[7].reshape(1, D))
    res["norm_mix_g"] = (grad_norm_mix_g,) + plain(norm_mix_g, m_norm_mix_g, v_norm_mix_g, grad_norm_mix_g, "adam_norm_mix")
    res["norm_ffn_g"] = (grad_norm_ffn_g,) + plain(norm_ffn_g, m_norm_ffn_g, v_norm_ffn_g, grad_norm_ffn_g, "adam_norm_ffn")
    res["final_norm_g"] = (grad_final_norm_g,) + tuple(
        t.reshape(D) for t in plain(final_norm_g[None], m_final_norm_g[None], v_final_norm_g[None], grad_final_norm_g[None], "adam_final"))
    res["a_v_norm_g"] = (grad_a_v_norm_g,) + plain(a_v_norm_g, m_a_v_norm_g, v_a_v_norm_g, grad_a_v_norm_g, "adam_a_v_norm")
    res["a_w_s"] = (grad_a_w_s,) + plain(a_w_s, m_a_w_s, v_a_w_s, grad_a_w_s, "adam_a_w_s")
    res["a_b_s"] = (grad_a_b_s,) + plain(a_b_s, m_a_b_s, v_a_b_s, grad_a_b_s, "adam_a_b_s")
    res["b_conv_w"] = (grad_b_conv_w,) + plain(b_conv_w, m_b_conv_w, v_b_conv_w, grad_b_conv_w, "adam_b_conv")
    res["c_scale"] = (grad_c_scale,) + plain(c_scale, m_c_scale, v_c_scale, grad_c_scale, "adam_c_scale")

    order = ["norm_mix_g", "norm_ffn_g", "final_norm_g", "a_w_in", "a_v_norm_g", "a_w_s", "a_b_s", "a_w_out", "b_w_in",
             "b_conv_w", "b_w_out", "c_w_in", "c_w_grp", "c_scale", "c_w_out", "f_w_gate", "f_w_up", "f_w_down"]
    return (loss, grad_x, *[res[n][0] for n in order], *[res[n][1] for n in order],
            *[res[n][2] for n in order], *[res[n][3] for n in order])
```

```python
import functools

import jax
import jax.numpy as jnp
from jax import lax
from jax.experimental import pallas as pl
from jax.experimental.pallas import tpu as pltpu

f32 = jnp.float32
bf16 = jnp.bfloat16

NDEV = 8
D = 1024
EPS = 1e-6
CHUNK = 64
GBLK = 128
A_GROUPS = 8
C_GROUPS = 4
C_GDIM = 256
POOL_WINDOWS = (2, 4, 8, 16)
HALO = 16
F_LOC = 352
F_PAD = 384
DEPTH = 4
AXES = ("x", "y", "c")
MESH = pl.DeviceIdType.MESH

ADAM_LR = 0.001
ADAM_B1 = 0.9
ADAM_B2 = 0.999
ADAM_EPS = 1e-08
ADAM_WD = 0.01
ADAM_STEP = 10

VMEM_LIMIT = 56 * 1024 * 1024


def _cparams(sem):
    return pltpu.CompilerParams(dimension_semantics=sem, vmem_limit_bytes=VMEM_LIMIT)


def _gelu(z):
    return 0.5 * z * (1.0 + lax.erf(z * 0.7071067811865476))


def _gelu_cdf(z):
    return 0.5 * (1.0 + lax.erf(z * 0.7071067811865476))


def _gelu_grad(z, cdf):
    return cdf + z * jnp.exp(-0.5 * z * z) * 0.3989422804014327


def _dot(a, b):
    return jnp.dot(a, b, preferred_element_type=f32)


def _dot_nt(a, b):
    return lax.dot_general(a, b, (((1,), (1,)), ((), ())), preferred_element_type=f32)


def _dot_tn(a, b):
    return lax.dot_general(a, b, (((0,), (0,)), ((), ())), preferred_element_type=f32)


def norm_mm(x, g, w, *, tm, transposed, swiglu=False, name):
    T = x.shape[0]
    if swiglu:
        nchunk, n = NDEV, w.shape[1]
        w_spec = pl.BlockSpec((None, n, D), lambda i, d: (d, 0, 0))
    else:
        nchunk, n = 1, (NDEV * w.shape[1] if transposed else w.shape[2])
        w_spec = pl.BlockSpec(w.shape, lambda i, d: (0, 0, 0))

    def body(x_ref, g_ref, w_ref, p_ref, h_ref, *rest):
        @pl.when(pl.program_id(1) == 0)
        def _():
            xv = x_ref[...]
            r = lax.rsqrt(jnp.mean(xv * xv, axis=-1, keepdims=True) + EPS)
            h_ref[...] = (xv * r * g_ref[...]).astype(bf16)

        if swiglu:
            p = _dot_nt(h_ref[...], w_ref[...])
            gate, up = p[:, : n // 2], p[:, n // 2:]
            rest[0][...] = (gate * jax.nn.sigmoid(gate) * up).astype(bf16)
        elif transposed:
            p = _dot_nt(h_ref[...], w_ref[...].reshape(n, D))
        else:
            p = _dot(h_ref[...], w_ref[...].reshape(D, n))
        p_ref[...] = p.astype(p_ref.dtype)

    out_shape = [jax.ShapeDtypeStruct((T, nchunk * n), bf16), jax.ShapeDtypeStruct((T, D), bf16)]
    out_specs = [pl.BlockSpec((tm, n), lambda i, d: (i, d)), pl.BlockSpec((tm, D), lambda i, d: (i, 0))]
    if swiglu:
        out_shape.append(jax.ShapeDtypeStruct((T, nchunk * n // 2), bf16))
        out_specs.append(pl.BlockSpec((tm, n // 2), lambda i, d: (i, d)))
    return pl.pallas_call(
        body, grid=(T // tm, nchunk),
        in_specs=[pl.BlockSpec((tm, D), lambda i, d: (i, 0)), pl.BlockSpec((1, D), lambda i, d: (0, 0)), w_spec],
        out_specs=out_specs, out_shape=out_shape,
        compiler_params=_cparams(("parallel", "arbitrary")), name=name,
    )(x, g, w)


def mm_res(y, w, x, *, tm, name):
    T, K = y.shape
    k = w.shape[1]

    def body(y_ref, w_ref, x_ref, o_ref):
        o_ref[...] = x_ref[...] + _dot(y_ref[...], w_ref[...].reshape(K, D))

    return pl.pallas_call(
        body, grid=(T // tm,),
        in_specs=[pl.BlockSpec((tm, K), lambda i: (i, 0)),
                  pl.BlockSpec((NDEV, k, D), lambda i: (0, 0, 0)),
                  pl.BlockSpec((tm, D), lambda i: (i, 0))],
        out_specs=pl.BlockSpec((tm, D), lambda i: (i, 0)),
        out_shape=jax.ShapeDtypeStruct((T, D), f32),
        compiler_params=_cparams(("parallel",)), name=name,
    )(y, w, x)


def mm_nt(dy, w, *, tm, name):
    T = dy.shape[0]
    k = w.shape[1]
    K = NDEV * k

    def body(dy_ref, w_ref, o_ref, dyb_ref):
        dyb = dy_ref[...].astype(bf16)
        dyb_ref[...] = dyb
        o_ref[...] = _dot_nt(dyb, w_ref[...].reshape(K, D)).astype(bf16)

    return pl.pallas_call(
        body, grid=(T // tm,),
        in_specs=[pl.BlockSpec((tm, D), lambda i: (i, 0)),
                  pl.BlockSpec((NDEV, k, D), lambda i: (0, 0, 0))],
        out_specs=[pl.BlockSpec((tm, K), lambda i: (i, 0)), pl.BlockSpec((tm, D), lambda i: (i, 0))],
        out_shape=[jax.ShapeDtypeStruct((T, K), bf16), jax.ShapeDtypeStruct((T, D), bf16)],
        compiler_params=_cparams(("parallel",)), name=name,
    )(dy, w)


def ffn_bwd_mid(dx, wd, pf, *, tm, name):
    T = dx.shape[0]
    H = NDEV * F_PAD

    def body(dx_ref, w_ref, p_ref, dp_ref, dxb_ref):
        dxb = dx_ref[...].astype(bf16)
        dxb_ref[...] = dxb
        da_all = _dot_nt(dxb, w_ref[...].reshape(H, D))
        for d in range(NDEV):
            c0 = 2 * F_PAD * d
            da = da_all[:, F_PAD * d:F_PAD * (d + 1)]
            gate = p_ref[:, c0:c0 + F_PAD].astype(f32)
            up = p_ref[:, c0 + F_PAD:c0 + 2 * F_PAD].astype(f32)
            sg = jax.nn.sigmoid(gate)
            dp_ref[:, c0:c0 + F_PAD] = (da * up * (sg * (1.0 + gate * (1.0 - sg)))).astype(bf16)
            dp_ref[:, c0 + F_PAD:c0 + 2 * F_PAD] = (da * gate * sg).astype(bf16)

    return pl.pallas_call(
        body, grid=(T // tm,),
        in_specs=[pl.BlockSpec((tm, D), lambda i: (i, 0)),
                  pl.BlockSpec((NDEV, F_PAD, D), lambda i: (0, 0, 0), pipeline_mode=pl.Buffered(1)),
                  pl.BlockSpec((tm, 2 * H), lambda i: (i, 0))],
        out_specs=[pl.BlockSpec((tm, 2 * H), lambda i: (i, 0)), pl.BlockSpec((tm, D), lambda i: (i, 0))],
        out_shape=[jax.ShapeDtypeStruct(pf.shape, bf16), jax.ShapeDtypeStruct((T, D), bf16)],
        compiler_params=_cparams(("parallel",)), name=name,
    )(dx, wd, pf)


def mm_normbwd(dp, w, x, g, dres, tok, *, tm, transposed, name):
    T, N = dp.shape
    toks = [] if tok is None else [tok]

    def body(dp_ref, w_ref, x_ref, g_ref, dres_ref, *rest):
        dx_ref, dg_ref = rest[len(toks):]

        @pl.when(pl.program_id(0) == 0)
        def _():
            dg_ref[...] = jnp.zeros(dg_ref.shape, f32)

        if transposed:
            dh = _dot(dp_ref[...], w_ref[...].reshape(N, D))
        else:
            dh = _dot_nt(dp_ref[...], w_ref[...].reshape(D, N))
        gv = g_ref[...] + rest[0][0:1, 0:1] if toks else g_ref[...]
        xv = x_ref[...]
        r = lax.rsqrt(jnp.mean(xv * xv, axis=-1, keepdims=True) + EPS)
        xhat = xv * r
        dg_ref[...] += jnp.sum(dh * xhat, axis=0, keepdims=True)
        dxhat = dh * gv
        dx_ref[...] = dres_ref[...] + r * (dxhat - xhat * jnp.mean(dxhat * xhat, axis=-1, keepdims=True))

    return pl.pallas_call(
        body, grid=(T // tm,),
        in_specs=[pl.BlockSpec((tm, N), lambda i: (i, 0)),
                  pl.BlockSpec(w.shape, lambda i: (0, 0, 0), pipeline_mode=pl.Buffered(1)),
                  pl.BlockSpec((tm, D), lambda i: (i, 0)), pl.BlockSpec((1, D), lambda i: (0, 0)),
                  pl.BlockSpec((tm, D), lambda i: (i, 0))] + [pl.BlockSpec((8, 128), lambda i: (0, 0))] * len(toks),
        out_specs=[pl.BlockSpec((tm, D), lambda i: (i, 0)), pl.BlockSpec((1, D), lambda i: (0, 0))],
        out_shape=[jax.ShapeDtypeStruct((T, D), f32), jax.ShapeDtypeStruct((1, D), f32)],
        compiler_params=_cparams(("arbitrary",)), name=name,
    )(dp, w, x, g, dres, *toks)


def mm_tn(a, b, *, name):
    T = a.shape[0]
    k, n = a.shape[1] // NDEV, b.shape[1]

    def body(a_ref, b_ref, o_ref):
        o_ref[...] = _dot_tn(a_ref[...], b_ref[...]).astype(bf16)

    return pl.pallas_call(
        body, grid=(NDEV,),
        in_specs=[pl.BlockSpec((T, k), lambda d: (0, d)),
                  pl.BlockSpec((T, n), lambda d: (0, 0), pipeline_mode=pl.Buffered(1))],
        out_specs=pl.BlockSpec((None, k, n), lambda d: (d, 0, 0)),
        out_shape=jax.ShapeDtypeStruct((NDEV, k, n), bf16),
        compiler_params=_cparams(("parallel",)), name=name,
    )(a, b)


def _prev_halo_spec(tm, ncol):
    return pl.BlockSpec((HALO, ncol), lambda i: (jnp.maximum(i * (tm // HALO) - 1, 0), 0))


def _next_halo_spec(tm, ncol, T):
    return pl.BlockSpec((HALO, ncol), lambda i: (jnp.minimum((i + 1) * (tm // HALO), T // HALO - 1), 0))


def a_mid_fwd(p, gv, wm, bsb, *, tm, name):
    T = p.shape[0]

    def body(p_ref, gv_ref, wm_ref, bs_ref, y_ref, vn_sc):
        v = _gelu(p_ref[:, D:].astype(f32))
        vc = v - jnp.mean(v, axis=-1, keepdims=True)
        var = jnp.mean(vc * vc, axis=-1, keepdims=True)
        vn_sc[...] = (vc * lax.rsqrt(var + EPS) * gv_ref[...]).astype(bf16)
        for g in range(A_GROUPS):
            cs = slice(g * GBLK, (g + 1) * GBLK)
            for r in range(tm // GBLK):
                rs = slice(r * GBLK, (r + 1) * GBLK)
                sv = _dot(wm_ref[g], vn_sc[rs, cs]) + bs_ref[g]
                y_ref[rs, cs] = (_gelu(p_ref[rs, cs].astype(f32)) * sv).astype(bf16)

    return pl.pallas_call(
        body, grid=(T // tm,),
        in_specs=[pl.BlockSpec((tm, 2 * D), lambda i: (i, 0)), pl.BlockSpec((1, D), lambda i: (0, 0)),
                  pl.BlockSpec((A_GROUPS, GBLK, GBLK), lambda i: (0, 0, 0)),
                  pl.BlockSpec((A_GROUPS, GBLK, GBLK), lambda i: (0, 0, 0))],
        out_specs=pl.BlockSpec((tm, D), lambda i: (i, 0)),
        out_shape=jax.ShapeDtypeStruct((T, D), bf16),
        scratch_shapes=[pltpu.VMEM((tm, D), bf16)],
        compiler_params=_cparams(("parallel",)), name=name,
    )(p, gv, wm, bsb)


def a_mid_bwd(p, dy, gv, wm, bsb, *, tm, name):
    T = p.shape[0]

    def body(p_ref, dy_ref, gv_ref, wm_ref, bs_ref, dp_ref, dgv_ref, dwm_ref, dbs_ref, vn_sc, dvn_sc):
        @pl.when(pl.program_id(0) == 0)
        def _():
            dgv_ref[...] = jnp.zeros(dgv_ref.shape, f32)
            dwm_ref[...] = jnp.zeros(dwm_ref.shape, f32)
            dbs_ref[...] = jnp.zeros(dbs_ref.shape, f32)

        zv = p_ref[:, D:].astype(f32)
        cdf_v = _gelu_cdf(zv)
        v = zv * cdf_v
        vc = v - jnp.mean(v, axis=-1, keepdims=True)
        rstd = lax.rsqrt(jnp.mean(vc * vc, axis=-1, keepdims=True) + EPS)
        vhat = vc * rstd
        vn_sc[...] = (vhat * gv_ref[...]).astype(bf16)
        for g in range(A_GROUPS):
            cs = slice(g * GBLK, (g + 1) * GBLK)
            dwm = jnp.zeros((GBLK, GBLK), f32)
            dbs = jnp.zeros((GBLK, 1), f32)
            for r in range(tm // GBLK):
                rs = slice(r * GBLK, (r + 1) * GBLK)
                zu = p_ref[rs, cs].astype(f32)
                cdf_u = _gelu_cdf(zu)
                vn = vn_sc[rs, cs]
                sv = _dot(wm_ref[g], vn) + bs_ref[g]
                dyb = dy_ref[rs, cs].astype(f32)
                dsv = dyb * (zu * cdf_u)
                dsvb = dsv.astype(bf16)
                dp_ref[rs, cs] = (dyb * sv * _gelu_grad(zu, cdf_u)).astype(bf16)
                dwm += _dot_nt(dsvb, vn)
                dbs += jnp.sum(dsv, axis=1, keepdims=True)
                dvn_sc[rs, cs] = _dot_tn(wm_ref[g], dsvb)
            dwm_ref[g] += dwm
            dbs_ref[g] += dbs
        dvn = dvn_sc[...]
        dgv_ref[...] += jnp.sum(dvn * vhat, axis=0, keepdims=True)
        dvhat = dvn * gv_ref[...]
        dv = rstd * (dvhat - jnp.mean(dvhat, axis=-1, keepdims=True)
                     - vhat * jnp.mean(dvhat * vhat, axis=-1, keepdims=True))
        dp_ref[:, D:] = (dv * _gelu_grad(zv, cdf_v)).astype(bf16)

    return pl.pallas_call(
        body, grid=(T // tm,),
        in_specs=[pl.BlockSpec((tm, 2 * D), lambda i: (i, 0)), pl.BlockSpec((tm, D), lambda i: (i, 0)),
                  pl.BlockSpec((1, D), lambda i: (0, 0)),
                  pl.BlockSpec((A_GROUPS, GBLK, GBLK), lambda i: (0, 0, 0)),
                  pl.BlockSpec((A_GROUPS, GBLK, GBLK), lambda i: (0, 0, 0))],
        out_specs=[pl.BlockSpec((tm, 2 * D), lambda i: (i, 0)), pl.BlockSpec((1, D), lambda i: (0, 0)),
                   pl.BlockSpec((A_GROUPS, GBLK, GBLK), lambda i: (0, 0, 0)),
                   pl.BlockSpec((A_GROUPS, GBLK, 1), lambda i: (0, 0, 0))],
        out_shape=[jax.ShapeDtypeStruct((T, 2 * D), bf16), jax.ShapeDtypeStruct((1, D), f32),
                   jax.ShapeDtypeStruct((A_GROUPS, GBLK, GBLK), f32), jax.ShapeDtypeStruct((A_GROUPS, GBLK, 1), f32)],
        scratch_shapes=[pltpu.VMEM((tm, D), bf16), pltpu.VMEM((tm, D), f32)],
        compiler_params=_cparams(("arbitrary",)), name=name,
    )(p, dy, gv, wm, bsb)


def b_mid_fwd(p, cw, *, tm, S, name):
    T = p.shape[0]
    nst = S // tm

    def body(p_ref, ph_ref, cw_ref, y_ref, ext):
        first = (pl.program_id(0) % nst) == 0
        q = p_ref[:, D:2 * D].astype(f32) * p_ref[:, 2 * D:].astype(f32)
        qh = ph_ref[:, D:2 * D].astype(f32) * ph_ref[:, 2 * D:].astype(f32)
        ext[0:HALO, :] = jnp.where(first, 0.0, qh)
        ext[HALO:, :] = q
        y = cw_ref[0:1, :] * ext[pl.ds(HALO - 2, tm), :] + cw_ref[1:2, :] * ext[pl.ds(HALO - 1, tm), :] + cw_ref[2:3, :] * q
        y_ref[...] = (p_ref[:, :D].astype(f32) * y).astype(bf16)

    return pl.pallas_call(
        body, grid=(T // tm,),
        in_specs=[pl.BlockSpec((tm, 3 * D), lambda i: (i, 0)), _prev_halo_spec(tm, 3 * D),
                  pl.BlockSpec((3, D), lambda i: (0, 0))],
        out_specs=pl.BlockSpec((tm, D), lambda i: (i, 0)),
        out_shape=jax.ShapeDtypeStruct((T, D), bf16),
        scratch_shapes=[pltpu.VMEM((tm + HALO, D), f32)],
        compiler_params=_cparams(("parallel",)), name=name,
    )(p, p, cw)


def b_mid_bwd(p, dy, cw, *, tm, S, name):
    T = p.shape[0]
    nst = S // tm

    def body(p_ref, ph_ref, pn_ref, dy_ref, dyn_ref, cw_ref, dp_ref, dcw_ref, ext, ext2):
        i = pl.program_id(0)
        first = (i % nst) == 0
        last = (i % nst) == nst - 1

        @pl.when(i == 0)
        def _():
            dcw_ref[...] = jnp.zeros(dcw_ref.shape, f32)

        gb, gc, xt = p_ref[:, :D].astype(f32), p_ref[:, D:2 * D].astype(f32), p_ref[:, 2 * D:].astype(f32)
        q = gc * xt
        ext[0:HALO, :] = jnp.where(first, 0.0, ph_ref[:, D:2 * D].astype(f32) * ph_ref[:, 2 * D:].astype(f32))
        ext[HALO:, :] = q
        q2 = ext[pl.ds(HALO - 2, tm), :]
        q1 = ext[pl.ds(HALO - 1, tm), :]
        y = cw_ref[0:1, :] * q2 + cw_ref[1:2, :] * q1 + cw_ref[2:3, :] * q
        dyo = dy_ref[...].astype(f32)
        dp_ref[:, :D] = (dyo * y).astype(bf16)
        dyc = dyo * gb
        ext2[0:tm, :] = dyc
        ext2[tm:, :] = jnp.where(last, 0.0, dyn_ref[...].astype(f32) * pn_ref[:, :D].astype(f32))
        dq = cw_ref[2:3, :] * dyc + cw_ref[1:2, :] * ext2[pl.ds(1, tm), :] + cw_ref[0:1, :] * ext2[pl.ds(2, tm), :]
        dp_ref[:, D:2 * D] = (dq * xt).astype(bf16)
        dp_ref[:, 2 * D:] = (dq * gc).astype(bf16)
        dcw_ref[0:1, :] += jnp.sum(dyc * q2, axis=0, keepdims=True)
        dcw_ref[1:2, :] += jnp.sum(dyc * q1, axis=0, keepdims=True)
        dcw_ref[2:3, :] += jnp.sum(dyc * q, axis=0, keepdims=True)

    return pl.pallas_call(
        body, grid=(T // tm,),
        in_specs=[pl.BlockSpec((tm, 3 * D), lambda i: (i, 0)), _prev_halo_spec(tm, 3 * D), _next_halo_spec(tm, 3 * D, T),
                  pl.BlockSpec((tm, D), lambda i: (i, 0)), _next_halo_spec(tm, D, T),
                  pl.BlockSpec((3, D), lambda i: (0, 0))],
        out_specs=[pl.BlockSpec((tm, 3 * D), lambda i: (i, 0)), pl.BlockSpec((3, D), lambda i: (0, 0))],
        out_shape=[jax.ShapeDtypeStruct((T, 3 * D), bf16), jax.ShapeDtypeStruct((3, D), f32)],
        scratch_shapes=[pltpu.VMEM((tm + HALO, D), f32), pltpu.VMEM((tm + HALO, D), f32)],
        compiler_params=_cparams(("arbitrary",)), name=name,
    )(p, p, p, dy, dy, cw)


def _pool_counts(i, nst, tm, rows, row0, w):
    t = (i % nst) * tm + row0 + lax.broadcasted_iota(jnp.int32, (rows, 1), 0)
    return jnp.minimum(t + 1, w).astype(f32)


def _pool_diff(p_ref, ext, g, i, nst, tm):
    w = POOL_WINDOWS[g]
    cs = slice(g * C_GDIM, (g + 1) * C_GDIM)
    pg = p_ref[:, cs].astype(f32)
    s = pg
    for k in range(1, w):
        s = s + ext[pl.ds(HALO - k, tm), cs]
    return s / _pool_counts(i, nst, tm, tm, 0, w) - pg


def c_mid_fwd(p, wg, scale, *, tm, S, name):
    T = p.shape[0]
    nst = S // tm

    def body(p_ref, ph_ref, wg_ref, sc_ref, y_ref, ext):
        i = pl.program_id(0)
        first = (i % nst) == 0
        ext[0:HALO, :] = jnp.where(first, 0.0, ph_ref[...].astype(f32))
        ext[HALO:, :] = p_ref[...].astype(f32)
        for g in range(C_GROUPS):
            cs = slice(g * C_GDIM, (g + 1) * C_GDIM)
            dg = _pool_diff(p_ref, ext, g, i, nst, tm).astype(bf16)
            wv = wg_ref[:, g].reshape(C_GDIM, C_GDIM)
            y_ref[:, cs] = (_dot(dg, wv) * sc_ref[:, cs]).astype(bf16)

    return pl.pallas_call(
        body, grid=(T // tm,),
        in_specs=[pl.BlockSpec((tm, D), lambda i: (i, 0)), _prev_halo_spec(tm, D),
                  pl.BlockSpec((NDEV, C_GROUPS, C_GDIM // NDEV, C_GDIM), lambda i: (0, 0, 0, 0)),
                  pl.BlockSpec((1, D), lambda i: (0, 0))],
        out_specs=pl.BlockSpec((tm, D), lambda i: (i, 0)),
        out_shape=jax.ShapeDtypeStruct((T, D), bf16),
        scratch_shapes=[pltpu.VMEM((tm + HALO, D), f32)],
        compiler_params=_cparams(("parallel",)), name=name,
    )(p, p, wg, scale)


def c_mid_bwd(p, dy, wg, scale, *, tm, S, name):
    T = p.shape[0]
    nst = S // tm

    def body(p_ref, ph_ref, dy_ref, dyn_ref, wg_ref, sc_ref, dp_ref, dsc_ref, dwg_ref, ext, ext2):
        i = pl.program_id(0)
        first = (i % nst) == 0
        last = (i % nst) == nst - 1

        @pl.when(i == 0)
        def _():
            dsc_ref[...] = jnp.zeros(dsc_ref.shape, f32)
            dwg_ref[...] = jnp.zeros(dwg_ref.shape, f32)

        ext[0:HALO, :] = jnp.where(first, 0.0, ph_ref[...].astype(f32))
        ext[HALO:, :] = p_ref[...].astype(f32)
        for g in range(C_GROUPS):
            w = POOL_WINDOWS[g]
            cs = slice(g * C_GDIM, (g + 1) * C_GDIM)
            dg = _pool_diff(p_ref, ext, g, i, nst, tm).astype(bf16)
            wv = wg_ref[:, g].reshape(C_GDIM, C_GDIM)
            dyo = dy_ref[:, cs].astype(f32)
            dsc_ref[:, cs] += jnp.sum(dyo * _dot(dg, wv), axis=0, keepdims=True)
            dyp = (dyo * sc_ref[:, cs]).astype(bf16)
            dypn = (dyn_ref[:, cs].astype(f32) * sc_ref[:, cs]).astype(bf16)
            dwg_ref[g] += _dot_tn(dg, dyp)
            dd = _dot_nt(dyp, wv)
            ddn = _dot_nt(dypn, wv)
            ext2[0:tm, cs] = dd / _pool_counts(i, nst, tm, tm, 0, w)
            ext2[tm:, cs] = jnp.where(last, 0.0, ddn / _pool_counts(i, nst, tm, HALO, tm, w))
            s = -dd
            for k in range(w):
                s = s + ext2[pl.ds(k, tm), cs]
            dp_ref[:, cs] = s.astype(bf16)

    return pl.pallas_call(
        body, grid=(T // tm,),
        in_specs=[pl.BlockSpec((tm, D), lambda i: (i, 0)), _prev_halo_spec(tm, D),
                  pl.BlockSpec((tm, D), lambda i: (i, 0)), _next_halo_spec(tm, D, T),
                  pl.BlockSpec((NDEV, C_GROUPS, C_GDIM // NDEV, C_GDIM), lambda i: (0, 0, 0, 0)),
                  pl.BlockSpec((1, D), lambda i: (0, 0))],
        out_specs=[pl.BlockSpec((tm, D), lambda i: (i, 0)), pl.BlockSpec((1, D), lambda i: (0, 0)),
                   pl.BlockSpec((C_GROUPS, C_GDIM, C_GDIM), lambda i: (0, 0, 0))],
        out_shape=[jax.ShapeDtypeStruct((T, D), bf16), jax.ShapeDtypeStruct((1, D), f32),
                   jax.ShapeDtypeStruct((C_GROUPS, C_GDIM, C_GDIM), f32)],
        scratch_shapes=[pltpu.VMEM((tm + HALO, D), f32), pltpu.VMEM((tm + HALO, D), f32)],
        compiler_params=_cparams(("arbitrary",)), name=name,
    )(p, p, dy, dy, wg, scale)


def final_loss(x, g, tgt, *, tm, name):
    T = x.shape[0]

    def body(x_ref, g_ref, t_ref, loss_ref, dx_ref, dg_ref):
        @pl.when(pl.program_id(0) == 0)
        def _():
            loss_ref[...] = jnp.zeros(loss_ref.shape, f32)
            dg_ref[...] = jnp.zeros(dg_ref.shape, f32)

        xv = x_ref[...]
        r = lax.rsqrt(jnp.mean(xv * xv, axis=-1, keepdims=True) + EPS)
        xhat = xv * r
        err = xhat * g_ref[...] - t_ref[...]
        loss_ref[...] += 0.5 * jnp.sum(jnp.mean(err * err, axis=-1, keepdims=True))
        dy = err * (1.0 / D)
        dg_ref[...] += jnp.sum(dy * xhat, axis=0, keepdims=True)
        dxhat = dy * g_ref[...]
        dx_ref[...] = r * (dxhat - xhat * jnp.mean(dxhat * xhat, axis=-1, keepdims=True))

    return pl.pallas_call(
        body, grid=(T // tm,),
        in_specs=[pl.BlockSpec((tm, D), lambda i: (i, 0)), pl.BlockSpec((1, D), lambda i: (0, 0)),
                  pl.BlockSpec((tm, D), lambda i: (i, 0))],
        out_specs=[pl.BlockSpec((8, 128), lambda i: (0, 0)), pl.BlockSpec((tm, D), lambda i: (i, 0)),
                   pl.BlockSpec((1, D), lambda i: (0, 0))],
        out_shape=[jax.ShapeDtypeStruct((8, 128), f32), jax.ShapeDtypeStruct((T, D), f32),
                   jax.ShapeDtypeStruct((1, D), f32)],
        compiler_params=_cparams(("arbitrary",)), name=name,
    )(x, g, tgt)


def _slot(px, py, pc):
    return 4 * px + 2 * py + pc


def _with_own_block(s, me):
    zone = lax.empty((NDEV,) + s.shape, s.dtype)
    return lax.dynamic_update_slice(zone, s[None], (me,) + (0,) * s.ndim)


def all_gather(arrs, me, *, name):
    n = len(arrs)

    def body(*refs):
        ins, outs = refs[:n], refs[2 * n:3 * n]
        send_sems, recv_sems = refs[3 * n:]
        x, y, c = lax.axis_index("x"), lax.axis_index("y"), lax.axis_index("c")
        me, sibling = (x, y, c), (x, y, 1 - c)
        chips = [(1 - x, y), (x, 1 - y), (1 - x, 1 - y)]

        def copy(a, k, block, to, src=None):
            dst = outs[a].at[_slot(*block)]
            return pltpu.make_async_remote_copy(
                src_ref=dst if src is None else src, dst_ref=dst,
                send_sem=send_sems.at[a, k], recv_sem=recv_sems.at[a, k], device_id=to, device_id_type=MESH)

        first = []
        for a in range(n):
            first.append(copy(a, 0, me, sibling, src=ins[a]))
            first += [copy(a, 1 + j, me, (*chip, c), src=ins[a]) for j, chip in enumerate(chips)]
        for cp in first:
            cp.start()
        passed = []
        for j, chip in enumerate(chips):
            for a in range(n):
                copy(a, 1 + j, (*chip, c), me).wait_recv()
                fwd = copy(a, 4 + j, (*chip, c), sibling)
                fwd.start()
                passed.append(fwd)
        for a in range(n):
            copy(a, 0, sibling, me).wait_recv()
        for j, chip in enumerate(chips):
            for a in range(n):
                copy(a, 4 + j, (*chip, 1 - c), me).wait_recv()
        for cp in first + passed:
            cp.wait_send()

    any_spec = pl.BlockSpec(memory_space=pl.ANY)
    return pl.pallas_call(
        body,
        in_specs=[any_spec] * (2 * n), out_specs=[any_spec] * n,
        out_shape=[jax.ShapeDtypeStruct((NDEV,) + a.shape, a.dtype) for a in arrs],
        input_output_aliases={n + i: i for i in range(n)},
        scratch_shapes=[pltpu.SemaphoreType.DMA((n, 7)), pltpu.SemaphoreType.DMA((n, 7))],
        compiler_params=pltpu.CompilerParams(has_side_effects=True), name=name,
    )(*arrs, *[_with_own_block(a, me) for a in arrs])


HBM_SPEC = pl.BlockSpec(memory_space=pltpu.HBM)
SEM_SPEC = pl.BlockSpec(memory_space=pltpu.SEMAPHORE)
ANY_SPEC = pl.BlockSpec(memory_space=pl.ANY)
TOKEN_SHAPE = jax.ShapeDtypeStruct((8, 128), f32)
DATAFLOW_EFFECT = pltpu.SideEffectType.DATAFLOW_SIDE_EFFECTING


def _in_hbm(a):
    return pltpu.with_memory_space_constraint(a, pltpu.HBM)


def _hbm_like(a):
    return pltpu.HBM(a.shape, a.dtype)


def _mesh_pos():
    return lax.axis_index("x"), lax.axis_index("y"), lax.axis_index("c")


def _gather_targets(x, y, c):
    return [(x, y, 1 - c), (1 - x, y, c), (x, 1 - y, c), (1 - x, 1 - y, c)]


def gather_start(shards, me, after, *, name):
    n = len(shards)
    extra = [] if after is None else [after]

    def body(*refs):
        srcs, lands = refs[:n], refs[n:2 * n]
        send_sems, recv_sems = refs[2 * n + len(extra)], refs[2 * n + len(extra) + 1]
        token = refs[-1]
        x, y, c = _mesh_pos()
        me = _slot(x, y, c)
        for a in range(n):
            for k, to in enumerate(_gather_targets(x, y, c)):
                pltpu.make_async_remote_copy(
                    src_ref=srcs[a], dst_ref=lands[a].at[me], send_sem=send_sems.at[4 * a + k], recv_sem=recv_sems.at[4 * a + k],
                    device_id=to, device_id_type=MESH).start()
        token[...] = jnp.zeros(token.shape, f32)

    lands = [_with_own_block(s, me) for s in shards]
    sems = pltpu.SemaphoreType.DMA((4 * n,))
    out = pl.pallas_call(
        body, name=name,
        in_specs=[HBM_SPEC] * (2 * n) + [ANY_SPEC] * len(extra),
        out_specs=[SEM_SPEC, SEM_SPEC] + [HBM_SPEC] * (2 * n) + [pl.BlockSpec(memory_space=pltpu.VMEM)],
        out_shape=[sems, sems] + [_hbm_like(s) for s in shards] + [_hbm_like(l) for l in lands] + [TOKEN_SHAPE],
        input_output_aliases={i: 2 + i for i in range(2 * n)},
        compiler_params=pltpu.CompilerParams(has_side_effects=DATAFLOW_EFFECT),
    )(*[_in_hbm(s) for s in shards], *[_in_hbm(l) for l in lands], *extra)
    return out[0], out[1], out[2:2 + n], out[2 + n:2 + 2 * n], out[-1]


def gather_wait(send_sems, recv_sems, shards, lands, after, *, name):
    n = len(shards)
    after = list(after) if isinstance(after, (list, tuple)) else [after]

    def body(*refs):
        srcs, lands_in = refs[:n], refs[n:2 * n]
        send_sems, recv_sems = refs[2 * n], refs[2 * n + 1]
        x, y, c = _mesh_pos()
        for a in range(n):
            for k, frm in enumerate(_gather_targets(x, y, c)):
                cp = pltpu.make_async_remote_copy(
                    src_ref=srcs[a], dst_ref=lands_in[a].at[_slot(*frm)], send_sem=send_sems.at[4 * a + k],
                    recv_sem=recv_sems.at[4 * a + k], device_id=frm, device_id_type=MESH)
                cp.wait_send()
                cp.wait_recv()

    out = pl.pallas_call(
        body, name=name,
        in_specs=[HBM_SPEC] * (2 * n) + [SEM_SPEC, SEM_SPEC] + [ANY_SPEC] * len(after),
        out_specs=[HBM_SPEC] * (2 * n),
        out_shape=[_hbm_like(s) for s in shards] + [_hbm_like(l) for l in lands],
        input_output_aliases={i: i for i in range(2 * n)},
        compiler_params=pltpu.CompilerParams(has_side_effects=DATAFLOW_EFFECT),
    )(*shards, *lands, send_sems, recv_sems, *after)
    return out[:n], out[n:]


def gather_finish(lands, *, name):
    n = len(lands)

    def body(*refs):
        lands_in, lands_out = refs[:n], refs[n:2 * n]
        send_sems, recv_sems = refs[2 * n:]
        x, y, c = _mesh_pos()
        sibling = (x, y, 1 - c)
        chips = [(1 - x, y), (x, 1 - y), (1 - x, 1 - y)]
        sent = []
        for a in range(n):
            for j, chip in enumerate(chips):
                s = _slot(*chip, c)
                cp = pltpu.make_async_remote_copy(
                    src_ref=lands_in[a].at[s], dst_ref=lands_out[a].at[s], send_sem=send_sems.at[a, j],
                    recv_sem=recv_sems.at[a, j], device_id=sibling, device_id_type=MESH)
                cp.start()
                sent.append(cp)
        for a in range(n):
            for j, chip in enumerate(chips):
                s = _slot(*chip, 1 - c)
                pltpu.make_async_remote_copy(
                    src_ref=lands_in[a].at[s], dst_ref=lands_out[a].at[s], send_sem=send_sems.at[a, j],
                    recv_sem=recv_sems.at[a, j], device_id=sibling, device_id_type=MESH).wait_recv()
        for cp in sent:
            cp.wait_send()

    return pl.pallas_call(
        body, name=name,
        in_specs=[ANY_SPEC] * n, out_specs=[ANY_SPEC] * n,
        out_shape=[jax.ShapeDtypeStruct(l.shape, l.dtype) for l in lands],
        input_output_aliases={i: i for i in range(n)},
        scratch_shapes=[pltpu.SemaphoreType.DMA((n, 3)), pltpu.SemaphoreType.DMA((n, 3))],
        compiler_params=pltpu.CompilerParams(has_side_effects=True),
    )(*lands)


def _peer(x, y, c, r):
    return (x ^ ((r >> 2) & 1), y ^ ((r >> 1) & 1), c ^ (r & 1))


def exchange_start(items, lands, after, *, name):
    n, m = len(items), len(lands)
    extra = [] if after is None else [after]

    def body(*refs):
        gs, zones = refs[:n], refs[n:n + m]
        send_sems, recv_sems = refs[n + m + len(extra)], refs[n + m + len(extra) + 1]
        token = refs[-1]
        x, y, c = _mesh_pos()
        me = _slot(x, y, c)
        for r in (1, 2, 4, 3, 5, 6, 7):
            to = _peer(x, y, c, r)
            for a, (_, zi, l) in enumerate(items):
                pltpu.make_async_remote_copy(
                    src_ref=gs[a].at[_slot(*to)], dst_ref=zones[zi].at[me, l], send_sem=send_sems.at[7 * a + r - 1],
                    recv_sem=recv_sems.at[7 * a + r - 1], device_id=to, device_id_type=MESH).start()
        token[...] = jnp.zeros(token.shape, f32)

    gs = [g for g, _, _ in items]
    sems = pltpu.SemaphoreType.DMA((7 * n,))
    out = pl.pallas_call(
        body, name=name,
        in_specs=[HBM_SPEC] * (n + m) + [ANY_SPEC] * len(extra),
        out_specs=[SEM_SPEC, SEM_SPEC] + [HBM_SPEC] * (n + m) + [pl.BlockSpec(memory_space=pltpu.VMEM)],
        out_shape=[sems, sems] + [_hbm_like(g) for g in gs] + [_hbm_like(z) for z in lands] + [TOKEN_SHAPE],
        input_output_aliases={i: 2 + i for i in range(n + m)},
        compiler_params=pltpu.CompilerParams(has_side_effects=DATAFLOW_EFFECT),
    )(*[_in_hbm(g) for g in gs], *[_in_hbm(z) for z in lands], *extra)
    return out[0], out[1], out[2:2 + n], out[2 + n:2 + n + m], out[-1]


def exchange_wait(parts, lands, after, *, name):
    m = len(lands)
    flat_gs = [g for _, _, _, gs in parts for g in gs]
    ng = len(flat_gs)
    after = list(after) if isinstance(after, (list, tuple)) else [after]

    def body(*refs):
        gs, zones = refs[:ng], refs[ng:ng + m]
        sem_refs = refs[ng + m:ng + m + 2 * len(parts)]
        x, y, c = _mesh_pos()
        base = 0
        for pi, (items, _, _, _) in enumerate(parts):
            send_sems, recv_sems = sem_refs[2 * pi], sem_refs[2 * pi + 1]
            for r in range(1, NDEV):
                frm = _peer(x, y, c, r)
                for a, (_, zi, l) in enumerate(items):
                    cp = pltpu.make_async_remote_copy(
                        src_ref=gs[base + a].at[_slot(*frm)], dst_ref=zones[zi].at[_slot(*frm), l],
                        send_sem=send_sems.at[7 * a + r - 1], recv_sem=recv_sems.at[7 * a + r - 1],
                        device_id=frm, device_id_type=MESH)
                    cp.wait_send()
                    cp.wait_recv()
            base += len(items)

    sem_args = [s for _, ss, rs, _ in parts for s in (ss, rs)]
    out = pl.pallas_call(
        body, name=name,
        in_specs=[HBM_SPEC] * (ng + m) + [SEM_SPEC] * len(sem_args) + [ANY_SPEC] * len(after),
        out_specs=[HBM_SPEC] * (ng + m),
        out_shape=[_hbm_like(g) for g in flat_gs] + [_hbm_like(z) for z in lands],
        input_output_aliases={i: i for i in range(ng + m)},
        compiler_params=pltpu.CompilerParams(has_side_effects=DATAFLOW_EFFECT),
    )(*flat_gs, *lands, *sem_args, *after)
    return out[ng:]


def sum8(g, *, name):
    _, R, C = g.shape
    tr = R

    def body(g_ref, o_ref):
        s = g_ref[0]
        for k in range(1, NDEV):
            s = s + g_ref[k]
        o_ref[...] = s

    return pl.pallas_call(
        body, grid=(R // tr,),
        in_specs=[pl.BlockSpec((NDEV, tr, C), lambda i: (0, i, 0))],
        out_specs=pl.BlockSpec((tr, C), lambda i: (i, 0)),
        out_shape=jax.ShapeDtypeStruct((R, C), f32),
        compiler_params=_cparams(("parallel",)), name=name,
    )(g)


def _adam_math(w, g, m, v):
    m = ADAM_B1 * m + (1.0 - ADAM_B1) * g
    v = ADAM_B2 * v + (1.0 - ADAM_B2) * (g * g)
    m_hat = m / (1.0 - ADAM_B1 ** ADAM_STEP)
    v_hat = v / (1.0 - ADAM_B2 ** ADAM_STEP)
    delta = -ADAM_LR * (m_hat / (jnp.sqrt(v_hat) + ADAM_EPS) + ADAM_WD * w)
    return delta, m, v


def adamw_slices(w, m, v, recv, *, row0=0, col0=0, rrows=None, tr=None, tc=None, l0=0, nl=None, prev=None, name):
    L, R, C = w.shape
    Rp, Cp = recv.shape[2], recv.shape[3]
    nl = L - l0 if nl is None else nl
    tr = R if tr is None else tr
    tc = C if tc is None else tc
    rrows = Rp if rrows is None else rrows
    assert R % tr == 0 and C % tc == 0 and row0 % rrows == 0 and R <= rrows
    assert (tr == R or (Rp == R and row0 == 0)) and (tc == C or (Cp == C and col0 == 0))
    rr = rrows if tr == R else tr
    rc = Cp if tc == C else tc
    rb = row0 // rrows if tr == R else 0
    prevs = [] if prev is None else list(prev)

    def body(w_ref, m_ref, v_ref, r_ref, *rest):
        g_ref, d_ref, nm_ref, nv_ref = rest[len(prevs):]
        c0 = col0 if tc == C else 0
        g = r_ref[0, 0:tr, c0:c0 + tc].astype(f32)
        for k in range(1, NDEV):
            g = g + r_ref[k, 0:tr, c0:c0 + tc].astype(f32)
        delta, nm, nv = _adam_math(w_ref[...], g, m_ref[...], v_ref[...])
        g_ref[...] = g
        d_ref[...] = delta
        nm_ref[...] = nm
        nv_ref[...] = nv

    wspec = pl.BlockSpec((None, tr, tc), lambda l, i, j: (l0 + l, i, j))
    return pl.pallas_call(
        body, grid=(nl, R // tr, C // tc),
        in_specs=[wspec, wspec, wspec, pl.BlockSpec((NDEV, None, rr, rc), lambda l, i, j: (0, l0 + l, rb + i, j))]
        + [pl.BlockSpec(memory_space=pl.ANY)] * len(prevs),
        out_specs=[wspec] * 4, out_shape=[jax.ShapeDtypeStruct((L, R, C), f32)] * 4,
        input_output_aliases={4 + q: q for q in range(len(prevs))},
        compiler_params=_cparams(("parallel", "parallel", "parallel")), name=name,
    )(w, m, v, recv, *prevs)


def adamw_slices_t(w, m, v, recv, *, name):
    L, R, C = w.shape

    def body(w_ref, m_ref, v_ref, r_ref, g_ref, d_ref, nm_ref, nv_ref):
        gt = r_ref[0].astype(f32)
        for k in range(1, NDEV):
            gt = gt + r_ref[k].astype(f32)
        g = gt.T
        delta, nm, nv = _adam_math(w_ref[...], g, m_ref[...], v_ref[...])
        g_ref[...] = g
        d_ref[...] = delta
        nm_ref[...] = nm
        nv_ref[...] = nv

    wspec = pl.BlockSpec((None, R, C), lambda l: (l, 0, 0))
    return pl.pallas_call(
        body, grid=(L,),
        in_specs=[wspec, wspec, wspec, pl.BlockSpec((NDEV, None, C, R), lambda l: (0, l, 0, 0))],
        out_specs=[wspec] * 4, out_shape=[jax.ShapeDtypeStruct((L, R, C), f32)] * 4,
        compiler_params=_cparams(("parallel",)), name=name,
    )(w, m, v, recv)


def adamw_plain(w, m, v, g, *, name):
    R, C = w.shape

    def body(w_ref, m_ref, v_ref, g_ref, d_ref, nm_ref, nv_ref):
        delta, nm, nv = _adam_math(w_ref[...], g_ref[...], m_ref[...], v_ref[...])
        d_ref[...] = delta
        nm_ref[...] = nm
        nv_ref[...] = nv

    return pl.pallas_call(
        body, out_shape=[jax.ShapeDtypeStruct((R, C), f32)] * 3, name=name,
    )(w, m, v, g)


def _chunk_causal_mask():
    pos = jnp.arange(GBLK)
    return (pos[None, :] // CHUNK) <= (pos[:, None] // CHUNK)


def kernel(x, norm_mix_g, norm_ffn_g, final_norm_g, a_w_in, a_v_norm_g, a_w_s, a_b_s, a_w_out, b_w_in, b_conv_w, b_w_out, c_w_in, c_w_grp, c_scale, c_w_out, f_w_gate, f_w_up, f_w_down, loss_target, m_norm_mix_g, m_norm_ffn_g, m_final_norm_g, m_a_w_in, m_a_v_norm_g, m_a_w_s, m_a_b_s, m_a_w_out, m_b_w_in, m_b_conv_w, m_b_w_out, m_c_w_in, m_c_w_grp, m_c_scale, m_c_w_out, m_f_w_gate, m_f_w_up, m_f_w_down, v_norm_mix_g, v_norm_ffn_g, v_final_norm_g, v_a_w_in, v_a_v_norm_g, v_a_w_s, v_a_b_s, v_a_w_out, v_b_w_in, v_b_conv_w, v_b_w_out, v_c_w_in, v_c_w_grp, v_c_scale, v_c_w_out, v_f_w_gate, v_f_w_up, v_f_w_down):
    B, S, _ = x.shape
    T = B * S
    tm = min(512, S)
    tmm = min(1024, T)
    tms = min(512, T)
    me = _slot(lax.axis_index("x"), lax.axis_index("y"), lax.axis_index("c"))

    pad_r = ((0, 0), (0, F_PAD - F_LOC), (0, 0))

    def tb(w):
        return jnp.transpose(w, (0, 2, 1)).astype(bf16)

    f_gu = jnp.concatenate([jnp.pad(tb(f_w_gate), pad_r), jnp.pad(tb(f_w_up), pad_r)], axis=1)
    f_d = jnp.pad(f_w_down.astype(bf16), pad_r)
    small = jnp.concatenate([a_v_norm_g, b_conv_w[0], c_scale, jnp.zeros((2, GBLK), f32)], axis=0)
    a_in_b, a_out_b = tb(a_w_in), a_w_out.astype(bf16)
    stage_shards = [
        [a_in_b[0], a_out_b[0], small],
        [f_gu[0], f_d[0]],
        [tb(b_w_in)[0], b_w_out[0].astype(bf16), f_gu[1], f_d[1]],
        [c_w_in[0].astype(bf16), c_w_grp[0].astype(bf16), c_w_out[0].astype(bf16), f_gu[2], f_d[2]],
        [a_in_b[1], a_out_b[1], f_gu[3], f_d[3]],
    ]
    started, tok = [], None
    for s in range(len(stage_shards)):
        started.append(gather_start(stage_shards[s], me, tok, name=f"gather_start_{s}"))
        tok = started[-1][4]

    def stage_weights(s, after):
        send_sems, recv_sems, shards, lands, _ = started[s]
        _, lands = gather_wait(send_sems, recv_sems, shards, lands, after, name=f"gather_wait_{s}")
        return gather_finish(lands, name=f"gather_finish_{s}")

    wm_masked = jnp.where(_chunk_causal_mask()[None, None], a_w_s, 0.0).astype(bf16)
    bsb = jnp.broadcast_to(a_b_s[:, :, :, None], a_b_s.shape + (GBLK,))

    xc = x.reshape(T, D)
    saved, weights = [], []
    for i in range(DEPTH):
        kind, j = i % 3, i // 3
        gmix = norm_mix_g[i][None]
        if i == 0:
            w_in, w_out, small_g = stage_weights(0, tok)
            small_full = jnp.transpose(small_g, (1, 0, 2)).reshape(8, D)
            gv_full, cw_full, cs_full = small_full[0:2], small_full[2:5], small_full[5:6]
            w = [w_in, w_out]
        else:
            w = stage_weights(i + 1, xc)
        if kind == 0:
            w_in, w_out = w[:2]
            p, h = norm_mm(xc, gmix, w_in, tm=tms, transposed=True, name=f"a_in_{i}")
            y = a_mid_fwd(p, gv_full[j][None], wm_masked[j], bsb[j], tm=tm, name=f"a_mid_{i}")
        elif kind == 1:
            w_in, w_out = w[:2]
            p, h = norm_mm(xc, gmix, w_in, tm=tms, transposed=True, name=f"b_in_{i}")
            y = b_mid_fwd(p, cw_full, tm=tm, S=S, name=f"b_mid_{i}")
        else:
            w_in, w_grp, w_out = w[:3]
            p, h = norm_mm(xc, gmix, w_in, tm=tms, transposed=False, name=f"c_in_{i}")
            y = c_mid_fwd(p, w_grp, cs_full, tm=tm, S=S, name=f"c_mid_{i}")
        x1 = mm_res(y, w_out, xc, tm=tms, name=f"mix_out_{i}")
        if i == 0:
            w = w + list(stage_weights(1, x1))
        weights.append(w)
        w_fgu, w_fd = w[-2], w[-1]
        pf, h2, a = norm_mm(x1, norm_ffn_g[i][None], w_fgu, tm=min(2048, T), transposed=True, swiglu=True, name=f"f_in_{i}")
        x2 = mm_res(a, w_fd, x1, tm=tms, name=f"f_out_{i}")
        saved.append((xc, p, h, y, x1, pf, h2, a))
        xc = x2

    loss_blk, dx, d_final = final_loss(xc, final_norm_g[None], loss_target.reshape(T, D), tm=tms, name="final_loss")
    loss = lax.psum(loss_blk[0, 0], AXES)

    zone_shapes = [(2, 256, D), (2, GBLK, D), (1, 384, D), (1, GBLK, D), (1, GBLK, D), (1, GBLK, C_GDIM), (1, GBLK, D),
                   (DEPTH, 2 * F_PAD, D), (DEPTH, F_PAD, D)]
    Z_AIN, Z_AOUT, Z_BIN, Z_BOUT, Z_CIN, Z_CGRP, Z_COUT, Z_FGU, Z_FD = range(9)
    lands = [lax.empty((NDEV,) + s, bf16) for s in zone_shapes]
    parts = []

    def exchange(items, after, name):
        nonlocal lands
        for g, zi, l in items:
            own = lax.dynamic_index_in_dim(g, me, axis=0, keepdims=True)[None]
            lands[zi] = lax.dynamic_update_slice(lands[zi], own, (me, l) + (0,) * (g.ndim - 1))
        idx = sorted({zi for _, zi, _ in items})
        local = [(g, idx.index(zi), l) for g, zi, l in items]
        send_sems, recv_sems, gs, zs, token = exchange_start(local, [lands[zi] for zi in idx], after, name=name)
        for q, zi in enumerate(idx):
            lands[zi] = zs[q]
        parts.append(([(None, zi, l) for _, zi, l in items], send_sems, recv_sems, gs))
        return token

    d_mix, d_ffn = [None] * DEPTH, [None] * DEPTH
    d_gv, d_wm, d_bs = [None] * 2, [None] * 2, [None] * 2
    tok = None
    for i in reversed(range(DEPTH)):
        kind, j = i % 3, i // 3
        xin, p, h, y, x1, pf, h2, a = saved[i]
        w = weights[i]
        w_in, w_out, w_fgu, w_fd = w[0], w[-3], w[-2], w[-1]
        dpf, dxb = ffn_bwd_mid(dx, w_fd, pf, tm=tms, name=f"f_bwd_mid_{i}")
        items = [(mm_tn(dpf, h2, name=f"f_dw_in_{i}"), Z_FGU, i),
                 (mm_tn(a, dxb, name=f"f_dw_out_{i}"), Z_FD, i)]
        if i == 0:
            tok = exchange(items, tok, "exchange_start_ffn0")
            items = []
        dx1, d_ffn[i] = mm_normbwd(dpf, w_fgu, x1, norm_ffn_g[i][None], dx, tok, tm=tms, transposed=True, name=f"f_bwd_in_{i}")
        dy, dx1b = mm_nt(dx1, w_out, tm=tmm, name=f"mix_bwd_out_{i}")
        if kind == 0:
            dp, d_gv[j], d_wm[j], d_bs[j] = a_mid_bwd(p, dy, gv_full[j][None], wm_masked[j], bsb[j], tm=tm, name=f"a_bwd_mid_{i}")
            items += [(mm_tn(dp, h, name=f"a_dw_in_{i}"), Z_AIN, j),
                      (mm_tn(y, dx1b, name=f"a_dw_out_{i}"), Z_AOUT, j)]
        elif kind == 1:
            dp, d_cw = b_mid_bwd(p, dy, cw_full, tm=tm, S=S, name=f"b_bwd_mid_{i}")
            items += [(mm_tn(dp, h, name=f"b_dw_in_{i}"), Z_BIN, 0),
                      (mm_tn(y, dx1b, name=f"b_dw_out_{i}"), Z_BOUT, 0)]
        else:
            dp, d_cs, d_wgrp = c_mid_bwd(p, dy, w[1], cs_full, tm=tm, S=S, name=f"c_bwd_mid_{i}")
            gw_cgrp = jnp.transpose(d_wgrp.reshape(C_GROUPS, NDEV, C_GDIM // NDEV, C_GDIM), (1, 0, 2, 3)).astype(bf16)
            items += [(mm_tn(h, dp, name=f"c_dw_in_{i}"), Z_CIN, 0),
                      (gw_cgrp.reshape(NDEV, GBLK, C_GDIM), Z_CGRP, 0),
                      (mm_tn(y, dx1b, name=f"c_dw_out_{i}"), Z_COUT, 0)]
        tok = exchange(items, tok, f"exchange_start_{i}")
        dx, d_mix[i] = mm_normbwd(dp, w_in, xin, norm_mix_g[i][None], dx1, tok, tm=tms, transposed=(kind != 2), name=f"mix_bwd_in_{i}")
    grad_x = dx.reshape(B, S, D)

    mask = _chunk_causal_mask()
    small_parts = [
        jnp.concatenate(d_mix, axis=0).reshape(-1, GBLK),
        jnp.concatenate(d_ffn, axis=0).reshape(-1, GBLK),
        d_final.reshape(-1, GBLK),
        jnp.concatenate(d_gv, axis=0).reshape(-1, GBLK),
        jnp.where(mask[None, None], jnp.stack(d_wm), 0.0).reshape(-1, GBLK),
        jnp.stack(d_bs).reshape(-1, GBLK),
        d_cw.reshape(-1, GBLK),
        d_cs.reshape(-1, GBLK),
    ]
    small_rows = [q.shape[0] for q in small_parts]
    sg_send, sg_recv, sg_shards, sg_lands, _ = gather_start(
        [jnp.concatenate(small_parts, axis=0)], me, tok, name="gather_small_start")

    def plain(w, m, v, g, name):
        shp = w.shape
        w2, m2, v2, g2 = (t.reshape(-1, shp[-1]) for t in (w, m, v, g))
        return tuple(t.reshape(shp) for t in adamw_plain(w2, m2, v2, g2, name=name))

    def sliced(w, m, v, recv, name, **kw):
        shp = w.shape
        w3, m3, v3 = (t.reshape((shp[0], -1, shp[-1])) for t in (w, m, v))
        return tuple(t.reshape(shp) for t in adamw_slices(w3, m3, v3, recv, name=name, **kw))

    def sliced_t(w, m, v, recv, name, **kw):
        wt, mt, vt = (jnp.transpose(t, (0, 2, 1)) for t in (w, m, v))
        return adamw_slices(wt, mt, vt, recv, name=name, **kw)

    def untransposed(outs):
        return tuple(jnp.transpose(t, (0, 2, 1)) for t in outs)

    lands = list(exchange_wait(parts[:3], lands, dx, name="exchange_wait"))
    _, _, r_bin, r_bout, r_cin, r_cgrp, r_cout, r_fgu, r_fd = lands
    res = {}
    res["b_w_in"] = adamw_slices_t(b_w_in, m_b_w_in, v_b_w_in, r_bin, name="adam_b_w_in")
    res["b_w_out"] = sliced(b_w_out, m_b_w_out, v_b_w_out, r_bout, "adam_b_w_out")
    res["c_w_in"] = sliced(c_w_in, m_c_w_in, v_c_w_in, r_cin, "adam_c_w_in")
    res["c_w_grp"] = sliced(c_w_grp, m_c_w_grp, v_c_w_grp, r_cgrp, "adam_c_w_grp")
    res["c_w_out"] = sliced(c_w_out, m_c_w_out, v_c_w_out, r_cout, "adam_c_w_out")
    fg = sliced_t(f_w_gate, m_f_w_gate, v_f_w_gate, r_fgu, "adam_f_gate_123", row0=0, rrows=F_PAD, tc=256, l0=1)
    fu = sliced_t(f_w_up, m_f_w_up, v_f_w_up, r_fgu, "adam_f_up_123", row0=F_PAD, rrows=F_PAD, tc=256, l0=1)
    fd = adamw_slices(f_w_down, m_f_w_down, v_f_w_down, r_fd, tc=256, l0=1, name="adam_f_down_123")
    big = [res[n][1] for n in ("b_w_in", "b_w_out", "c_w_in", "c_w_grp", "c_w_out")] + [fg[1], fu[1], fd[1]]

    ffn0_items, ffn0_send, ffn0_recv, ffn0_gs = parts[3]
    ffn0_zones = [Z_FGU, Z_FD]
    ffn0_part = ([(None, ffn0_zones.index(zi), l) for _, zi, l in ffn0_items], ffn0_send, ffn0_recv, ffn0_gs)
    r_fgu, r_fd = exchange_wait([ffn0_part], [lands[zi] for zi in ffn0_zones], big, name="exchange_wait_ffn0")
    fg = sliced_t(f_w_gate, m_f_w_gate, v_f_w_gate, r_fgu, "adam_f_gate_0", row0=0, rrows=F_PAD, tc=256, l0=0, nl=1, prev=fg)
    fu = sliced_t(f_w_up, m_f_w_up, v_f_w_up, r_fgu, "adam_f_up_0", row0=F_PAD, rrows=F_PAD, tc=256, l0=0, nl=1, prev=fu)
    fd = adamw_slices(f_w_down, m_f_w_down, v_f_w_down, r_fd, tc=256, l0=0, nl=1, prev=fd, name="adam_f_down_0")
    res["f_w_gate"], res["f_w_up"], res["f_w_down"] = untransposed(fg), untransposed(fu), tuple(fd)

    last_items, last_send, last_recv, last_gs = parts[4]
    last_zones = [Z_AIN, Z_AOUT]
    last_part = ([(None, last_zones.index(zi), l) for _, zi, l in last_items], last_send, last_recv, last_gs)
    r_ain, r_aout = exchange_wait([last_part], [lands[zi] for zi in last_zones], [fg[1], fu[1], fd[1]], name="exchange_wait_last")
    res["a_w_in"] = adamw_slices_t(a_w_in, m_a_w_in, v_a_w_in, r_ain, name="adam_a_w_in")
    res["a_w_out"] = sliced(a_w_out, m_a_w_out, v_a_w_out, r_aout, "adam_a_w_out")

    _, sg_lands = gather_wait(sg_send, sg_recv, sg_shards, sg_lands, [res["a_w_in"][1], res["a_w_out"][1]], name="gather_small_wait")
    (gs_all,) = gather_finish(sg_lands, name="gather_small_finish")
    gs = sum8(gs_all, name="sum_small_grads")
    offs = [0]
    for r in small_rows:
        offs.append(offs[-1] + r)
    sp = [gs[offs[q]:offs[q + 1]] for q in range(len(small_parts))]
    grad_norm_mix_g = sp[0].reshape(DEPTH, D)
    grad_norm_ffn_g = sp[1].reshape(DEPTH, D)
    grad_final_norm_g = sp[2].reshape(D)
    grad_a_w_s = sp[4].reshape(a_w_s.shape)
    grad_a_b_s = sp[5].reshape(a_b_s.shape)

    def my_cols(full):
        return lax.dynamic_slice_in_dim(full, me * GBLK, GBLK, axis=1)

    grad_a_v_norm_g = my_cols(sp[3].reshape(2, D))
    grad_b_conv_w = my_cols(sp[6].reshape(3, D))[None]
    grad_c_scale = my_cols(sp[7].reshape(1, D))
    res["norm_mix_g"] = (grad_norm_mix_g,) + plain(norm_mix_g, m_norm_mix_g, v_norm_mix_g, grad_norm_mix_g, "adam_norm_mix")
    res["norm_ffn_g"] = (grad_norm_ffn_g,) + plain(norm_ffn_g, m_norm_ffn_g, v_norm_ffn_g, grad_norm_ffn_g, "adam_norm_ffn")
    res["final_norm_g"] = (grad_final_norm_g,) + tuple(
        t.reshape(D) for t in plain(final_norm_g[None], m_final_norm_g[None], v_final_norm_g[None], grad_final_norm_g[None], "adam_final"))
    res["a_v_norm_g"] = (grad_a_v_norm_g,) + plain(a_v_norm_g, m_a_v_norm_g, v_a_v_norm_g, grad_a_v_norm_g, "adam_a_v_norm")
    res["a_w_s"] = (grad_a_w_s,) + plain(a_w_s, m_a_w_s, v_a_w_s, grad_a_w_s, "adam_a_w_s")
    res["a_b_s"] = (grad_a_b_s,) + plain(a_b_s, m_a_b_s, v_a_b_s, grad_a_b_s, "adam_a_b_s")
    res["b_conv_w"] = (grad_b_conv_w,) + plain(b_conv_w, m_b_conv_w, v_b_conv_w, grad_b_conv_w, "adam_b_conv")
    res["c_scale"] = (grad_c_scale,) + plain(c_scale, m_c_scale, v_c_scale, grad_c_scale, "adam_c_scale")

    order = ["norm_mix_g", "norm_ffn_g", "final_norm_g", "a_w_in", "a_v_norm_g", "a_w_s", "a_b_s", "a_w_out", "b_w_in",
             "b_conv_w", "b_w_out", "c_w_in", "c_w_grp", "c_scale", "c_w_out", "f_w_gate", "f_w_up", "f_w_down"]
    return (loss, grad_x, *[res[n][0] for n in order], *[res[n][1] for n in order],
            *[res[n][2] for n in order], *[res[n][3] for n in order])
```

```python
import functools

import jax
import jax.numpy as jnp
from jax import lax
from jax.experimental import pallas as pl
from jax.experimental.pallas import tpu as pltpu

f32 = jnp.float32
bf16 = jnp.bfloat16

NDEV = 8
D = 1024
EPS = 1e-6
CHUNK = 64
GBLK = 128
A_GROUPS = 8
C_GROUPS = 4
C_GDIM = 256
POOL_WINDOWS = (2, 4, 8, 16)
HALO = 16
COLS = 256
F_LOC = 352
MXU_COLS = 256
DEPTH = 4
AXES = ("x", "y", "c")
MESH = pl.DeviceIdType.MESH

ADAM_LR = 0.001
ADAM_B1 = 0.9
ADAM_B2 = 0.999
ADAM_EPS = 1e-08
ADAM_WD = 0.01
ADAM_STEP = 10

VMEM_LIMIT = 56 * 1024 * 1024


def _cparams(sem):
    return pltpu.CompilerParams(dimension_semantics=sem, vmem_limit_bytes=VMEM_LIMIT)


def _gelu(z):
    return 0.5 * z * (1.0 + lax.erf(z * 0.7071067811865476))


def _gelu_cdf(z):
    return 0.5 * (1.0 + lax.erf(z * 0.7071067811865476))


def _gelu_grad(z, cdf):
    return cdf + z * jnp.exp(-0.5 * z * z) * 0.3989422804014327


def _dot(a, b):
    return jnp.dot(a, b, preferred_element_type=f32)


def _dot_nt(a, b):
    return lax.dot_general(a, b, (((1,), (1,)), ((), ())), preferred_element_type=f32)


def _dot_tn(a, b):
    return lax.dot_general(a, b, (((0,), (0,)), ((), ())), preferred_element_type=f32)


def _col_chunks(n, width=1024):
    return [(c, min(c + width, n)) for c in range(0, n, width)]


def norm_mm(x, g, w, *, tm, transposed, name):
    T = x.shape[0]
    n = w.shape[0] if transposed else w.shape[1]

    def body(x_ref, g_ref, w_ref, p_ref, h_ref):
        xv = x_ref[...]
        r = lax.rsqrt(jnp.mean(xv * xv, axis=-1, keepdims=True) + EPS)
        h = (xv * r * g_ref[...]).astype(bf16)
        h_ref[...] = h
        p = _dot_nt(h, w_ref[...]) if transposed else _dot(h, w_ref[...])
        p_ref[...] = p.astype(bf16)

    return pl.pallas_call(
        body, grid=(T // tm,),
        in_specs=[pl.BlockSpec((tm, D), lambda i: (i, 0)), pl.BlockSpec((1, D), lambda i: (0, 0)),
                  pl.BlockSpec(w.shape, lambda i: (0, 0), pipeline_mode=pl.Buffered(1))],
        out_specs=[pl.BlockSpec((tm, n), lambda i: (i, 0)), pl.BlockSpec((tm, D), lambda i: (i, 0))],
        out_shape=[jax.ShapeDtypeStruct((T, n), bf16), jax.ShapeDtypeStruct((T, D), bf16)],
        compiler_params=_cparams(("parallel",)), name=name,
    )(x, g, w)


def ffn_in(x, g, wtg, wtu, *, tm, name):
    T = x.shape[0]
    H = wtg.shape[0]

    def body(x_ref, g_ref, wg_ref, wu_ref, pg_ref, pu_ref, a_ref, h_ref):
        xv = x_ref[...]
        r = lax.rsqrt(jnp.mean(xv * xv, axis=-1, keepdims=True) + EPS)
        h = (xv * r * g_ref[...]).astype(bf16)
        h_ref[...] = h
        for c0, c1 in _col_chunks(H):
            gate = _dot_nt(h, wg_ref[c0:c1, :])
            up = _dot_nt(h, wu_ref[c0:c1, :])
            pg_ref[:, c0:c1] = gate.astype(bf16)
            pu_ref[:, c0:c1] = up.astype(bf16)
            a_ref[:, c0:c1] = (gate * jax.nn.sigmoid(gate) * up).astype(bf16)

    wspec = pl.BlockSpec((H, D), lambda i: (0, 0), pipeline_mode=pl.Buffered(1))
    hspec = pl.BlockSpec((tm, H), lambda i: (i, 0))
    return pl.pallas_call(
        body, grid=(T // tm,),
        in_specs=[pl.BlockSpec((tm, D), lambda i: (i, 0)), pl.BlockSpec((1, D), lambda i: (0, 0)), wspec, wspec],
        out_specs=[hspec, hspec, hspec, pl.BlockSpec((tm, D), lambda i: (i, 0))],
        out_shape=[jax.ShapeDtypeStruct((T, H), bf16)] * 3 + [jax.ShapeDtypeStruct((T, D), bf16)],
        compiler_params=_cparams(("parallel",)), name=name,
    )(x, g, wtg, wtu)


def mm_res(y, w, x, *, tm, name):
    T, K = y.shape

    def body(y_ref, w_ref, x_ref, o_ref):
        o_ref[...] = x_ref[...] + _dot(y_ref[...], w_ref[...])

    return pl.pallas_call(
        body, grid=(T // tm,),
        in_specs=[pl.BlockSpec((tm, K), lambda i: (i, 0)),
                  pl.BlockSpec((K, D), lambda i: (0, 0), pipeline_mode=pl.Buffered(1)),
                  pl.BlockSpec((tm, D), lambda i: (i, 0))],
        out_specs=pl.BlockSpec((tm, D), lambda i: (i, 0)),
        out_shape=jax.ShapeDtypeStruct((T, D), f32),
        compiler_params=_cparams(("parallel",)), name=name,
    )(y, w, x)


def mm_nt(dy, w, *, tm, name):
    T = dy.shape[0]
    K = w.shape[0]

    def body(dy_ref, w_ref, o_ref, dyb_ref):
        dyb = dy_ref[...].astype(bf16)
        dyb_ref[...] = dyb
        o_ref[...] = _dot_nt(dyb, w_ref[...]).astype(bf16)

    return pl.pallas_call(
        body, grid=(T // tm,),
        in_specs=[pl.BlockSpec((tm, D), lambda i: (i, 0)),
                  pl.BlockSpec((K, D), lambda i: (0, 0), pipeline_mode=pl.Buffered(1))],
        out_specs=[pl.BlockSpec((tm, K), lambda i: (i, 0)), pl.BlockSpec((tm, D), lambda i: (i, 0))],
        out_shape=[jax.ShapeDtypeStruct((T, K), bf16), jax.ShapeDtypeStruct((T, D), bf16)],
        compiler_params=_cparams(("parallel",)), name=name,
    )(dy, w)


def ffn_bwd_mid(dx, wd, pg, pu, *, tm, name):
    T = dx.shape[0]
    H = wd.shape[0]

    def body(dx_ref, w_ref, pg_ref, pu_ref, dg_ref, du_ref, dxb_ref):
        dxb = dx_ref[...].astype(bf16)
        dxb_ref[...] = dxb
        for c0, c1 in _col_chunks(H):
            da = _dot_nt(dxb, w_ref[c0:c1, :])
            gate = pg_ref[:, c0:c1].astype(f32)
            up = pu_ref[:, c0:c1].astype(f32)
            sg = jax.nn.sigmoid(gate)
            dg_ref[:, c0:c1] = (da * up * (sg * (1.0 + gate * (1.0 - sg)))).astype(bf16)
            du_ref[:, c0:c1] = (da * gate * sg).astype(bf16)

    hspec = pl.BlockSpec((tm, H), lambda i: (i, 0))
    return pl.pallas_call(
        body, grid=(T // tm,),
        in_specs=[pl.BlockSpec((tm, D), lambda i: (i, 0)),
                  pl.BlockSpec((H, D), lambda i: (0, 0), pipeline_mode=pl.Buffered(1)), hspec, hspec],
        out_specs=[hspec, hspec, pl.BlockSpec((tm, D), lambda i: (i, 0))],
        out_shape=[jax.ShapeDtypeStruct((T, H), bf16)] * 2 + [jax.ShapeDtypeStruct((T, D), bf16)],
        compiler_params=_cparams(("parallel",)), name=name,
    )(dx, wd, pg, pu)


def mm_normbwd(dps, ws, x, g, dres, tok, *, tm, transposed, name):
    T = x.shape[0]
    nq = len(dps)
    toks = [] if tok is None else [tok]

    def body(*refs):
        dp_refs, w_refs = refs[:nq], refs[nq:2 * nq]
        x_ref, g_ref, dres_ref = refs[2 * nq:2 * nq + 3]
        dx_ref, dg_ref = refs[2 * nq + 3 + len(toks):]

        @pl.when(pl.program_id(0) == 0)
        def _():
            dg_ref[...] = jnp.zeros(dg_ref.shape, f32)

        dh = None
        for q in range(nq):
            if transposed:
                part = _dot(dp_refs[q][...], w_refs[q][...])
            else:
                part = _dot_nt(dp_refs[q][...], w_refs[q][...])
            dh = part if dh is None else dh + part
        gv = g_ref[...] + refs[2 * nq + 3][0:1, 0:1] if toks else g_ref[...]
        xv = x_ref[...]
        r = lax.rsqrt(jnp.mean(xv * xv, axis=-1, keepdims=True) + EPS)
        xhat = xv * r
        dg_ref[...] += jnp.sum(dh * xhat, axis=0, keepdims=True)
        dxhat = dh * gv
        dx_ref[...] = dres_ref[...] + r * (dxhat - xhat * jnp.mean(dxhat * xhat, axis=-1, keepdims=True))

    return pl.pallas_call(
        body, grid=(T // tm,),
        in_specs=[pl.BlockSpec((tm, dp.shape[1]), lambda i: (i, 0)) for dp in dps]
        + [pl.BlockSpec(w.shape, lambda i: (0, 0), pipeline_mode=pl.Buffered(1)) for w in ws]
        + [pl.BlockSpec((tm, D), lambda i: (i, 0)), pl.BlockSpec((1, D), lambda i: (0, 0)),
           pl.BlockSpec((tm, D), lambda i: (i, 0))] + [pl.BlockSpec((8, 128), lambda i: (0, 0))] * len(toks),
        out_specs=[pl.BlockSpec((tm, D), lambda i: (i, 0)), pl.BlockSpec((1, D), lambda i: (0, 0))],
        out_shape=[jax.ShapeDtypeStruct((T, D), f32), jax.ShapeDtypeStruct((1, D), f32)],
        compiler_params=_cparams(("arbitrary",)), name=name,
    )(*dps, *ws, x, g, dres, *toks)


def mm_tn(a, b, *, name):
    T, K = a.shape
    n = b.shape[1]

    def body(a_ref, b_ref, o_ref):
        o_ref[...] = _dot_tn(a_ref[...], b_ref[...]).astype(bf16)

    out = pl.pallas_call(
        body, grid=(K // MXU_COLS,),
        in_specs=[pl.BlockSpec((T, MXU_COLS), lambda d: (0, d)),
                  pl.BlockSpec((T, n), lambda d: (0, 0), pipeline_mode=pl.Buffered(1))],
        out_specs=pl.BlockSpec((MXU_COLS, n), lambda d: (d, 0)),
        out_shape=jax.ShapeDtypeStruct((K, n), bf16),
        compiler_params=_cparams(("parallel",)), name=name,
    )(a, b)
    return out.reshape(NDEV, K // NDEV, n)


def _prev_halo_spec(tm, ncol):
    return pl.BlockSpec((HALO, ncol), lambda i: (jnp.maximum(i * (tm // HALO) - 1, 0), 0))


def _next_halo_spec(tm, ncol, T):
    return pl.BlockSpec((HALO, ncol), lambda i: (jnp.minimum((i + 1) * (tm // HALO), T // HALO - 1), 0))


def a_mid_fwd(p, gv, wm, bsb, *, tm, name):
    T = p.shape[0]

    def body(p_ref, gv_ref, wm_ref, bs_ref, y_ref, vn_sc):
        v = _gelu(p_ref[:, D:].astype(f32))
        vc = v - jnp.mean(v, axis=-1, keepdims=True)
        var = jnp.mean(vc * vc, axis=-1, keepdims=True)
        vn_sc[...] = (vc * lax.rsqrt(var + EPS) * gv_ref[...]).astype(bf16)
        for g in range(A_GROUPS):
            cs = slice(g * GBLK, (g + 1) * GBLK)
            for r in range(tm // GBLK):
                rs = slice(r * GBLK, (r + 1) * GBLK)
                sv = _dot(wm_ref[g], vn_sc[rs, cs]) + bs_ref[g]
                y_ref[rs, cs] = (_gelu(p_ref[rs, cs].astype(f32)) * sv).astype(bf16)

    return pl.pallas_call(
        body, grid=(T // tm,),
        in_specs=[pl.BlockSpec((tm, 2 * D), lambda i: (i, 0)), pl.BlockSpec((1, D), lambda i: (0, 0)),
                  pl.BlockSpec((A_GROUPS, GBLK, GBLK), lambda i: (0, 0, 0)),
                  pl.BlockSpec((A_GROUPS, GBLK, GBLK), lambda i: (0, 0, 0))],
        out_specs=pl.BlockSpec((tm, D), lambda i: (i, 0)),
        out_shape=jax.ShapeDtypeStruct((T, D), bf16),
        scratch_shapes=[pltpu.VMEM((tm, D), bf16)],
        compiler_params=_cparams(("parallel",)), name=name,
    )(p, gv, wm, bsb)


def a_mid_bwd(p, dy, gv, wm, bsb, *, tm, name):
    T = p.shape[0]

    def body(p_ref, dy_ref, gv_ref, wm_ref, bs_ref, dp_ref, dgv_ref, dwm_ref, dbs_ref, vn_sc, dvn_sc):
        @pl.when(pl.program_id(0) == 0)
        def _():
            dgv_ref[...] = jnp.zeros(dgv_ref.shape, f32)
            dwm_ref[...] = jnp.zeros(dwm_ref.shape, f32)
            dbs_ref[...] = jnp.zeros(dbs_ref.shape, f32)

        zv = p_ref[:, D:].astype(f32)
        cdf_v = _gelu_cdf(zv)
        v = zv * cdf_v
        vc = v - jnp.mean(v, axis=-1, keepdims=True)
        rstd = lax.rsqrt(jnp.mean(vc * vc, axis=-1, keepdims=True) + EPS)
        vhat = vc * rstd
        vn_sc[...] = (vhat * gv_ref[...]).astype(bf16)
        for g in range(A_GROUPS):
            cs = slice(g * GBLK, (g + 1) * GBLK)
            dwm = jnp.zeros((GBLK, GBLK), f32)
            dbs = jnp.zeros((GBLK, 1), f32)
            for r in range(tm // GBLK):
                rs = slice(r * GBLK, (r + 1) * GBLK)
                zu = p_ref[rs, cs].astype(f32)
                cdf_u = _gelu_cdf(zu)
                vn = vn_sc[rs, cs]
                sv = _dot(wm_ref[g], vn) + bs_ref[g]
                dyb = dy_ref[rs, cs].astype(f32)
                dsv = dyb * (zu * cdf_u)
                dsvb = dsv.astype(bf16)
                dp_ref[rs, cs] = (dyb * sv * _gelu_grad(zu, cdf_u)).astype(bf16)
                dwm += _dot_nt(dsvb, vn)
                dbs += jnp.sum(dsv, axis=1, keepdims=True)
                dvn_sc[rs, cs] = _dot_tn(wm_ref[g], dsvb)
            dwm_ref[g] += dwm
            dbs_ref[g] += dbs
        dvn = dvn_sc[...]
        dgv_ref[...] += jnp.sum(dvn * vhat, axis=0, keepdims=True)
        dvhat = dvn * gv_ref[...]
        dv = rstd * (dvhat - jnp.mean(dvhat, axis=-1, keepdims=True)
                     - vhat * jnp.mean(dvhat * vhat, axis=-1, keepdims=True))
        dp_ref[:, D:] = (dv * _gelu_grad(zv, cdf_v)).astype(bf16)

    return pl.pallas_call(
        body, grid=(T // tm,),
        in_specs=[pl.BlockSpec((tm, 2 * D), lambda i: (i, 0)), pl.BlockSpec((tm, D), lambda i: (i, 0)),
                  pl.BlockSpec((1, D), lambda i: (0, 0)),
                  pl.BlockSpec((A_GROUPS, GBLK, GBLK), lambda i: (0, 0, 0)),
                  pl.BlockSpec((A_GROUPS, GBLK, GBLK), lambda i: (0, 0, 0))],
        out_specs=[pl.BlockSpec((tm, 2 * D), lambda i: (i, 0)), pl.BlockSpec((1, D), lambda i: (0, 0)),
                   pl.BlockSpec((A_GROUPS, GBLK, GBLK), lambda i: (0, 0, 0)),
                   pl.BlockSpec((A_GROUPS, GBLK, 1), lambda i: (0, 0, 0))],
        out_shape=[jax.ShapeDtypeStruct((T, 2 * D), bf16), jax.ShapeDtypeStruct((1, D), f32),
                   jax.ShapeDtypeStruct((A_GROUPS, GBLK, GBLK), f32), jax.ShapeDtypeStruct((A_GROUPS, GBLK, 1), f32)],
        scratch_shapes=[pltpu.VMEM((tm, D), bf16), pltpu.VMEM((tm, D), f32)],
        compiler_params=_cparams(("arbitrary",)), name=name,
    )(p, dy, gv, wm, bsb)


def b_mid_fwd(p, cw, *, tm, S, name):
    T = p.shape[0]
    nst = S // tm

    def body(p_ref, ph_ref, cw_ref, y_ref, ext):
        first = (pl.program_id(0) % nst) == 0
        for c0 in range(0, D, COLS):
            cs, cc, cx = slice(c0, c0 + COLS), slice(D + c0, D + c0 + COLS), slice(2 * D + c0, 2 * D + c0 + COLS)
            q = p_ref[:, cc].astype(f32) * p_ref[:, cx].astype(f32)
            qh = ph_ref[:, cc].astype(f32) * ph_ref[:, cx].astype(f32)
            ext[0:HALO, cs] = jnp.where(first, 0.0, qh)
            ext[HALO:, cs] = q
            y = (cw_ref[0:1, cs] * ext[pl.ds(HALO - 2, tm), cs] + cw_ref[1:2, cs] * ext[pl.ds(HALO - 1, tm), cs]
                 + cw_ref[2:3, cs] * q)
            y_ref[:, cs] = (p_ref[:, cs].astype(f32) * y).astype(bf16)

    return pl.pallas_call(
        body, grid=(T // tm,),
        in_specs=[pl.BlockSpec((tm, 3 * D), lambda i: (i, 0)), _prev_halo_spec(tm, 3 * D),
                  pl.BlockSpec((3, D), lambda i: (0, 0))],
        out_specs=pl.BlockSpec((tm, D), lambda i: (i, 0)),
        out_shape=jax.ShapeDtypeStruct((T, D), bf16),
        scratch_shapes=[pltpu.VMEM((tm + HALO, D), f32)],
        compiler_params=_cparams(("parallel",)), name=name,
    )(p, p, cw)


def b_mid_bwd(p, dy, cw, *, tm, S, name):
    T = p.shape[0]
    nst = S // tm

    def body(p_ref, ph_ref, pn_ref, dy_ref, dyn_ref, cw_ref, dp_ref, dcw_ref, ext, ext2):
        i = pl.program_id(0)
        first = (i % nst) == 0
        last = (i % nst) == nst - 1

        @pl.when(i == 0)
        def _():
            dcw_ref[...] = jnp.zeros(dcw_ref.shape, f32)

        for c0 in range(0, D, COLS):
            cs, cc, cx = slice(c0, c0 + COLS), slice(D + c0, D + c0 + COLS), slice(2 * D + c0, 2 * D + c0 + COLS)
            gb, gc, xt = p_ref[:, cs].astype(f32), p_ref[:, cc].astype(f32), p_ref[:, cx].astype(f32)
            q = gc * xt
            ext[0:HALO, cs] = jnp.where(first, 0.0, ph_ref[:, cc].astype(f32) * ph_ref[:, cx].astype(f32))
            ext[HALO:, cs] = q
            q2 = ext[pl.ds(HALO - 2, tm), cs]
            q1 = ext[pl.ds(HALO - 1, tm), cs]
            y = cw_ref[0:1, cs] * q2 + cw_ref[1:2, cs] * q1 + cw_ref[2:3, cs] * q
            dyo = dy_ref[:, cs].astype(f32)
            dp_ref[:, cs] = (dyo * y).astype(bf16)
            dyc = dyo * gb
            ext2[0:tm, cs] = dyc
            ext2[tm:, cs] = jnp.where(last, 0.0, dyn_ref[:, cs].astype(f32) * pn_ref[:, cs].astype(f32))
            dq = cw_ref[2:3, cs] * dyc + cw_ref[1:2, cs] * ext2[pl.ds(1, tm), cs] + cw_ref[0:1, cs] * ext2[pl.ds(2, tm), cs]
            dp_ref[:, cc] = (dq * xt).astype(bf16)
            dp_ref[:, cx] = (dq * gc).astype(bf16)
            dcw_ref[0:1, cs] += jnp.sum(dyc * q2, axis=0, keepdims=True)
            dcw_ref[1:2, cs] += jnp.sum(dyc * q1, axis=0, keepdims=True)
            dcw_ref[2:3, cs] += jnp.sum(dyc * q, axis=0, keepdims=True)

    return pl.pallas_call(
        body, grid=(T // tm,),
        in_specs=[pl.BlockSpec((tm, 3 * D), lambda i: (i, 0)), _prev_halo_spec(tm, 3 * D), _next_halo_spec(tm, 3 * D, T),
                  pl.BlockSpec((tm, D), lambda i: (i, 0)), _next_halo_spec(tm, D, T),
                  pl.BlockSpec((3, D), lambda i: (0, 0))],
        out_specs=[pl.BlockSpec((tm, 3 * D), lambda i: (i, 0)), pl.BlockSpec((3, D), lambda i: (0, 0))],
        out_shape=[jax.ShapeDtypeStruct((T, 3 * D), bf16), jax.ShapeDtypeStruct((3, D), f32)],
        scratch_shapes=[pltpu.VMEM((tm + HALO, D), f32), pltpu.VMEM((tm + HALO, D), f32)],
        compiler_params=_cparams(("arbitrary",)), name=name,
    )(p, p, p, dy, dy, cw)


def _pool_counts(i, nst, tm, rows, row0, w):
    t = (i % nst) * tm + row0 + lax.broadcasted_iota(jnp.int32, (rows, 1), 0)
    return jnp.minimum(t + 1, w).astype(f32)


def _pool_diff(p_ref, ext, g, i, nst, tm):
    w = POOL_WINDOWS[g]
    cs = slice(g * C_GDIM, (g + 1) * C_GDIM)
    pg = p_ref[:, cs].astype(f32)
    s = pg
    for k in range(1, w):
        s = s + ext[pl.ds(HALO - k, tm), cs]
    return s / _pool_counts(i, nst, tm, tm, 0, w) - pg


def c_mid_fwd(p, wg, scale, *, tm, S, name):
    T = p.shape[0]
    nst = S // tm

    def body(p_ref, ph_ref, wg_ref, sc_ref, y_ref, ext):
        i = pl.program_id(0)
        first = (i % nst) == 0
        ext[0:HALO, :] = jnp.where(first, 0.0, ph_ref[...].astype(f32))
        ext[HALO:, :] = p_ref[...].astype(f32)
        for g in range(C_GROUPS):
            cs = slice(g * C_GDIM, (g + 1) * C_GDIM)
            dg = _pool_diff(p_ref, ext, g, i, nst, tm).astype(bf16)
            wv = wg_ref[:, g].reshape(C_GDIM, C_GDIM)
            y_ref[:, cs] = (_dot(dg, wv) * sc_ref[:, cs]).astype(bf16)

    return pl.pallas_call(
        body, grid=(T // tm,),
        in_specs=[pl.BlockSpec((tm, D), lambda i: (i, 0)), _prev_halo_spec(tm, D),
                  pl.BlockSpec((NDEV, C_GROUPS, C_GDIM // NDEV, C_GDIM), lambda i: (0, 0, 0, 0)),
                  pl.BlockSpec((1, D), lambda i: (0, 0))],
        out_specs=pl.BlockSpec((tm, D), lambda i: (i, 0)),
        out_shape=jax.ShapeDtypeStruct((T, D), bf16),
        scratch_shapes=[pltpu.VMEM((tm + HALO, D), f32)],
        compiler_params=_cparams(("parallel",)), name=name,
    )(p, p, wg, scale)


def c_mid_bwd(p, dy, wg, scale, *, tm, S, name):
    T = p.shape[0]
    nst = S // tm

    def body(p_ref, ph_ref, dy_ref, dyn_ref, wg_ref, sc_ref, dp_ref, dsc_ref, dwg_ref, ext, ext2):
        i = pl.program_id(0)
        first = (i % nst) == 0
        last = (i % nst) == nst - 1

        @pl.when(i == 0)
        def _():
            dsc_ref[...] = jnp.zeros(dsc_ref.shape, f32)
            dwg_ref[...] = jnp.zeros(dwg_ref.shape, f32)

        ext[0:HALO, :] = jnp.where(first, 0.0, ph_ref[...].astype(f32))
        ext[HALO:, :] = p_ref[...].astype(f32)
        for g in range(C_GROUPS):
            w = POOL_WINDOWS[g]
            cs = slice(g * C_GDIM, (g + 1) * C_GDIM)
            dg = _pool_diff(p_ref, ext, g, i, nst, tm).astype(bf16)
            wv = wg_ref[:, g].reshape(C_GDIM, C_GDIM)
            dyo = dy_ref[:, cs].astype(f32)
            dsc_ref[:, cs] += jnp.sum(dyo * _dot(dg, wv), axis=0, keepdims=True)
            dyp = (dyo * sc_ref[:, cs]).astype(bf16)
            dypn = (dyn_ref[:, cs].astype(f32) * sc_ref[:, cs]).astype(bf16)
            dwg_ref[g] += _dot_tn(dg, dyp)
            dd = _dot_nt(dyp, wv)
            ddn = _dot_nt(dypn, wv)
            ext2[0:tm, cs] = dd / _pool_counts(i, nst, tm, tm, 0, w)
            ext2[tm:, cs] = jnp.where(last, 0.0, ddn / _pool_counts(i, nst, tm, HALO, tm, w))
            s = -dd
            for k in range(w):
                s = s + ext2[pl.ds(k, tm), cs]
            dp_ref[:, cs] = s.astype(bf16)

    return pl.pallas_call(
        body, grid=(T // tm,),
        in_specs=[pl.BlockSpec((tm, D), lambda i: (i, 0)), _prev_halo_spec(tm, D),
                  pl.BlockSpec((tm, D), lambda i: (i, 0)), _next_halo_spec(tm, D, T),
                  pl.BlockSpec((NDEV, C_GROUPS, C_GDIM // NDEV, C_GDIM), lambda i: (0, 0, 0, 0)),
                  pl.BlockSpec((1, D), lambda i: (0, 0))],
        out_specs=[pl.BlockSpec((tm, D), lambda i: (i, 0)), pl.BlockSpec((1, D), lambda i: (0, 0)),
                   pl.BlockSpec((C_GROUPS, C_GDIM, C_GDIM), lambda i: (0, 0, 0))],
        out_shape=[jax.ShapeDtypeStruct((T, D), bf16), jax.ShapeDtypeStruct((1, D), f32),
                   jax.ShapeDtypeStruct((C_GROUPS, C_GDIM, C_GDIM), f32)],
        scratch_shapes=[pltpu.VMEM((tm + HALO, D), f32), pltpu.VMEM((tm + HALO, D), f32)],
        compiler_params=_cparams(("arbitrary",)), name=name,
    )(p, p, dy, dy, wg, scale)


def final_loss(x, g, tgt, *, tm, name):
    T = x.shape[0]

    def body(x_ref, g_ref, t_ref, loss_ref, dx_ref, dg_ref):
        @pl.when(pl.program_id(0) == 0)
        def _():
            loss_ref[...] = jnp.zeros(loss_ref.shape, f32)
            dg_ref[...] = jnp.zeros(dg_ref.shape, f32)

        xv = x_ref[...]
        r = lax.rsqrt(jnp.mean(xv * xv, axis=-1, keepdims=True) + EPS)
        xhat = xv * r
        err = xhat * g_ref[...] - t_ref[...]
        loss_ref[...] += 0.5 * jnp.sum(jnp.mean(err * err, axis=-1, keepdims=True))
        dy = err * (1.0 / D)
        dg_ref[...] += jnp.sum(dy * xhat, axis=0, keepdims=True)
        dxhat = dy * g_ref[...]
        dx_ref[...] = r * (dxhat - xhat * jnp.mean(dxhat * xhat, axis=-1, keepdims=True))

    return pl.pallas_call(
        body, grid=(T // tm,),
        in_specs=[pl.BlockSpec((tm, D), lambda i: (i, 0)), pl.BlockSpec((1, D), lambda i: (0, 0)),
                  pl.BlockSpec((tm, D), lambda i: (i, 0))],
        out_specs=[pl.BlockSpec((8, 128), lambda i: (0, 0)), pl.BlockSpec((tm, D), lambda i: (i, 0)),
                   pl.BlockSpec((1, D), lambda i: (0, 0))],
        out_shape=[jax.ShapeDtypeStruct((8, 128), f32), jax.ShapeDtypeStruct((T, D), f32),
                   jax.ShapeDtypeStruct((1, D), f32)],
        compiler_params=_cparams(("arbitrary",)), name=name,
    )(x, g, tgt)


def _slot(px, py, pc):
    return 4 * px + 2 * py + pc


def _with_own_block(s, me):
    zone = lax.empty((NDEV,) + s.shape, s.dtype)
    return lax.dynamic_update_slice(zone, s[None], (me,) + (0,) * s.ndim)


def all_gather(arrs, me, *, name):
    n = len(arrs)

    def body(*refs):
        ins, outs = refs[:n], refs[2 * n:3 * n]
        send_sems, recv_sems = refs[3 * n:]
        x, y, c = lax.axis_index("x"), lax.axis_index("y"), lax.axis_index("c")
        me, sibling = (x, y, c), (x, y, 1 - c)
        chips = [(1 - x, y), (x, 1 - y), (1 - x, 1 - y)]

        def copy(a, k, block, to, src=None):
            dst = outs[a].at[_slot(*block)]
            return pltpu.make_async_remote_copy(
                src_ref=dst if src is None else src, dst_ref=dst,
                send_sem=send_sems.at[a, k], recv_sem=recv_sems.at[a, k], device_id=to, device_id_type=MESH)

        first = []
        for a in range(n):
            first.append(copy(a, 0, me, sibling, src=ins[a]))
            first += [copy(a, 1 + j, me, (*chip, c), src=ins[a]) for j, chip in enumerate(chips)]
        for cp in first:
            cp.start()
        passed = []
        for j, chip in enumerate(chips):
            for a in range(n):
                copy(a, 1 + j, (*chip, c), me).wait_recv()
                fwd = copy(a, 4 + j, (*chip, c), sibling)
                fwd.start()
                passed.append(fwd)
        for a in range(n):
            copy(a, 0, sibling, me).wait_recv()
        for j, chip in enumerate(chips):
            for a in range(n):
                copy(a, 4 + j, (*chip, 1 - c), me).wait_recv()
        for cp in first + passed:
            cp.wait_send()

    any_spec = pl.BlockSpec(memory_space=pl.ANY)
    return pl.pallas_call(
        body,
        in_specs=[any_spec] * (2 * n), out_specs=[any_spec] * n,
        out_shape=[jax.ShapeDtypeStruct((NDEV,) + a.shape, a.dtype) for a in arrs],
        input_output_aliases={n + i: i for i in range(n)},
        scratch_shapes=[pltpu.SemaphoreType.DMA((n, 7)), pltpu.SemaphoreType.DMA((n, 7))],
        compiler_params=pltpu.CompilerParams(has_side_effects=True), name=name,
    )(*arrs, *[_with_own_block(a, me) for a in arrs])


HBM_SPEC = pl.BlockSpec(memory_space=pltpu.HBM)
SEM_SPEC = pl.BlockSpec(memory_space=pltpu.SEMAPHORE)
ANY_SPEC = pl.BlockSpec(memory_space=pl.ANY)
TOKEN_SHAPE = jax.ShapeDtypeStruct((8, 128), f32)
DATAFLOW_EFFECT = pltpu.SideEffectType.DATAFLOW_SIDE_EFFECTING


def _in_hbm(a):
    return pltpu.with_memory_space_constraint(a, pltpu.HBM)


def _hbm_like(a):
    return pltpu.HBM(a.shape, a.dtype)


def _mesh_pos():
    return lax.axis_index("x"), lax.axis_index("y"), lax.axis_index("c")


def _gather_targets(x, y, c):
    return [(x, y, 1 - c), (1 - x, y, c), (x, 1 - y, c), (1 - x, 1 - y, c)]


def gather_start(shards, me, after, *, name):
    n = len(shards)
    extra = [] if after is None else [after]

    def body(*refs):
        srcs, lands = refs[:n], refs[n:2 * n]
        send_sems, recv_sems = refs[2 * n + len(extra)], refs[2 * n + len(extra) + 1]
        token = refs[-1]
        x, y, c = _mesh_pos()
        me = _slot(x, y, c)
        for a in range(n):
            for k, to in enumerate(_gather_targets(x, y, c)):
                pltpu.make_async_remote_copy(
                    src_ref=srcs[a], dst_ref=lands[a].at[me], send_sem=send_sems.at[4 * a + k], recv_sem=recv_sems.at[4 * a + k],
                    device_id=to, device_id_type=MESH).start()
        token[...] = jnp.zeros(token.shape, f32)

    lands = [_with_own_block(s, me) for s in shards]
    sems = pltpu.SemaphoreType.DMA((4 * n,))
    out = pl.pallas_call(
        body, name=name,
        in_specs=[HBM_SPEC] * (2 * n) + [ANY_SPEC] * len(extra),
        out_specs=[SEM_SPEC, SEM_SPEC] + [HBM_SPEC] * (2 * n) + [pl.BlockSpec(memory_space=pltpu.VMEM)],
        out_shape=[sems, sems] + [_hbm_like(s) for s in shards] + [_hbm_like(l) for l in lands] + [TOKEN_SHAPE],
        input_output_aliases={i: 2 + i for i in range(2 * n)},
        compiler_params=pltpu.CompilerParams(has_side_effects=DATAFLOW_EFFECT),
    )(*[_in_hbm(s) for s in shards], *[_in_hbm(l) for l in lands], *extra)
    return out[0], out[1], out[2:2 + n], out[2 + n:2 + 2 * n], out[-1]


def gather_wait(send_sems, recv_sems, shards, lands, after, *, name):
    n = len(shards)
    after = list(after) if isinstance(after, (list, tuple)) else [after]

    def body(*refs):
        srcs, lands_in = refs[:n], refs[n:2 * n]
        send_sems, recv_sems = refs[2 * n], refs[2 * n + 1]
        x, y, c = _mesh_pos()
        for a in range(n):
            for k, frm in enumerate(_gather_targets(x, y, c)):
                cp = pltpu.make_async_remote_copy(
                    src_ref=srcs[a], dst_ref=lands_in[a].at[_slot(*frm)], send_sem=send_sems.at[4 * a + k],
                    recv_sem=recv_sems.at[4 * a + k], device_id=frm, device_id_type=MESH)
                cp.wait_send()
                cp.wait_recv()

    out = pl.pallas_call(
        body, name=name,
        in_specs=[HBM_SPEC] * (2 * n) + [SEM_SPEC, SEM_SPEC] + [ANY_SPEC] * len(after),
        out_specs=[HBM_SPEC] * (2 * n),
        out_shape=[_hbm_like(s) for s in shards] + [_hbm_like(l) for l in lands],
        input_output_aliases={i: i for i in range(2 * n)},
        compiler_params=pltpu.CompilerParams(has_side_effects=DATAFLOW_EFFECT),
    )(*shards, *lands, send_sems, recv_sems, *after)
    return out[:n], out[n:]


def gather_finish(lands, *, name):
    n = len(lands)

    def body(*refs):
        lands_in, lands_out = refs[:n], refs[n:2 * n]
        send_sems, recv_sems = refs[2 * n:]
        x, y, c = _mesh_pos()
        sibling = (x, y, 1 - c)
        chips = [(1 - x, y), (x, 1 - y), (1 - x, 1 - y)]
        sent = []
        for a in range(n):
            for j, chip in enumerate(chips):
                s = _slot(*chip, c)
                cp = pltpu.make_async_remote_copy(
                    src_ref=lands_in[a].at[s], dst_ref=lands_out[a].at[s], send_sem=send_sems.at[a, j],
                    recv_sem=recv_sems.at[a, j], device_id=sibling, device_id_type=MESH)
                cp.start()
                sent.append(cp)
        for a in range(n):
            for j, chip in enumerate(chips):
                s = _slot(*chip, 1 - c)
                pltpu.make_async_remote_copy(
                    src_ref=lands_in[a].at[s], dst_ref=lands_out[a].at[s], send_sem=send_sems.at[a, j],
                    recv_sem=recv_sems.at[a, j], device_id=sibling, device_id_type=MESH).wait_recv()
        for cp in sent:
            cp.wait_send()

    return pl.pallas_call(
        body, name=name,
        in_specs=[ANY_SPEC] * n, out_specs=[ANY_SPEC] * n,
        out_shape=[jax.ShapeDtypeStruct(l.shape, l.dtype) for l in lands],
        input_output_aliases={i: i for i in range(n)},
        scratch_shapes=[pltpu.SemaphoreType.DMA((n, 3)), pltpu.SemaphoreType.DMA((n, 3))],
        compiler_params=pltpu.CompilerParams(has_side_effects=True),
    )(*lands)


def _peer(x, y, c, r):
    return (x ^ ((r >> 2) & 1), y ^ ((r >> 1) & 1), c ^ (r & 1))


def exchange_start(items, lands, after, *, name):
    n, m = len(items), len(lands)
    extra = [] if after is None else [after]

    def body(*refs):
        gs, zones = refs[:n], refs[n:n + m]
        send_sems, recv_sems = refs[n + m + len(extra)], refs[n + m + len(extra) + 1]
        token = refs[-1]
        x, y, c = _mesh_pos()
        me = _slot(x, y, c)
        for r in (1, 2, 4, 3, 5, 6, 7):
            to = _peer(x, y, c, r)
            for a, (_, zi, l) in enumerate(items):
                pltpu.make_async_remote_copy(
                    src_ref=gs[a].at[_slot(*to)], dst_ref=zones[zi].at[me, l], send_sem=send_sems.at[7 * a + r - 1],
                    recv_sem=recv_sems.at[7 * a + r - 1], device_id=to, device_id_type=MESH).start()
        token[...] = jnp.zeros(token.shape, f32)

    gs = [g for g, _, _ in items]
    sems = pltpu.SemaphoreType.DMA((7 * n,))
    out = pl.pallas_call(
        body, name=name,
        in_specs=[HBM_SPEC] * (n + m) + [ANY_SPEC] * len(extra),
        out_specs=[SEM_SPEC, SEM_SPEC] + [HBM_SPEC] * (n + m) + [pl.BlockSpec(memory_space=pltpu.VMEM)],
        out_shape=[sems, sems] + [_hbm_like(g) for g in gs] + [_hbm_like(z) for z in lands] + [TOKEN_SHAPE],
        input_output_aliases={i: 2 + i for i in range(n + m)},
        compiler_params=pltpu.CompilerParams(has_side_effects=DATAFLOW_EFFECT),
    )(*[_in_hbm(g) for g in gs], *[_in_hbm(z) for z in lands], *extra)
    return out[0], out[1], out[2:2 + n], out[2 + n:2 + n + m], out[-1]


def exchange_wait(parts, lands, after, *, name):
    m = len(lands)
    flat_gs = [g for _, _, _, gs in parts for g in gs]
    ng = len(flat_gs)
    after = list(after) if isinstance(after, (list, tuple)) else [after]

    def body(*refs):
        gs, zones = refs[:ng], refs[ng:ng + m]
        sem_refs = refs[ng + m:ng + m + 2 * len(parts)]
        x, y, c = _mesh_pos()
        base = 0
        for pi, (items, _, _, _) in enumerate(parts):
            send_sems, recv_sems = sem_refs[2 * pi], sem_refs[2 * pi + 1]
            for r in range(1, NDEV):
                frm = _peer(x, y, c, r)
                for a, (_, zi, l) in enumerate(items):
                    cp = pltpu.make_async_remote_copy(
                        src_ref=gs[base + a].at[_slot(*frm)], dst_ref=zones[zi].at[_slot(*frm), l],
                        send_sem=send_sems.at[7 * a + r - 1], recv_sem=recv_sems.at[7 * a + r - 1],
                        device_id=frm, device_id_type=MESH)
                    cp.wait_send()
                    cp.wait_recv()
            base += len(items)

    sem_args = [s for _, ss, rs, _ in parts for s in (ss, rs)]
    out = pl.pallas_call(
        body, name=name,
        in_specs=[HBM_SPEC] * (ng + m) + [SEM_SPEC] * len(sem_args) + [ANY_SPEC] * len(after),
        out_specs=[HBM_SPEC] * (ng + m),
        out_shape=[_hbm_like(g) for g in flat_gs] + [_hbm_like(z) for z in lands],
        input_output_aliases={i: i for i in range(ng + m)},
        compiler_params=pltpu.CompilerParams(has_side_effects=DATAFLOW_EFFECT),
    )(*flat_gs, *lands, *sem_args, *after)
    return out[ng:]


def sum8(g, *, name):
    _, R, C = g.shape
    tr = R

    def body(g_ref, o_ref):
        s = g_ref[0]
        for k in range(1, NDEV):
            s = s + g_ref[k]
        o_ref[...] = s

    return pl.pallas_call(
        body, grid=(R // tr,),
        in_specs=[pl.BlockSpec((NDEV, tr, C), lambda i: (0, i, 0))],
        out_specs=pl.BlockSpec((tr, C), lambda i: (i, 0)),
        out_shape=jax.ShapeDtypeStruct((R, C), f32),
        compiler_params=_cparams(("parallel",)), name=name,
    )(g)


def _adam_math(w, g, m, v):
    m = ADAM_B1 * m + (1.0 - ADAM_B1) * g
    v = ADAM_B2 * v + (1.0 - ADAM_B2) * (g * g)
    m_hat = m / (1.0 - ADAM_B1 ** ADAM_STEP)
    v_hat = v / (1.0 - ADAM_B2 ** ADAM_STEP)
    delta = -ADAM_LR * (m_hat / (jnp.sqrt(v_hat) + ADAM_EPS) + ADAM_WD * w)
    return delta, m, v


def adamw_slices(w, m, v, recv, *, row0=0, col0=0, rrows=None, tr=None, tc=None, l0=0, nl=None, prev=None, name):
    L, R, C = w.shape
    Rp, Cp = recv.shape[2], recv.shape[3]
    nl = L - l0 if nl is None else nl
    tr = R if tr is None else tr
    tc = C if tc is None else tc
    rrows = Rp if rrows is None else rrows
    assert R % tr == 0 and C % tc == 0 and row0 % rrows == 0 and R <= rrows
    assert (tr == R or (Rp == R and row0 == 0)) and (tc == C or (Cp == C and col0 == 0))
    rr = rrows if tr == R else tr
    rc = Cp if tc == C else tc
    rb = row0 // rrows if tr == R else 0
    prevs = [] if prev is None else list(prev)

    def body(w_ref, m_ref, v_ref, r_ref, *rest):
        g_ref, d_ref, nm_ref, nv_ref = rest[len(prevs):]
        c0 = col0 if tc == C else 0
        g = r_ref[0, 0:tr, c0:c0 + tc].astype(f32)
        for k in range(1, NDEV):
            g = g + r_ref[k, 0:tr, c0:c0 + tc].astype(f32)
        delta, nm, nv = _adam_math(w_ref[...], g, m_ref[...], v_ref[...])
        g_ref[...] = g
        d_ref[...] = delta
        nm_ref[...] = nm
        nv_ref[...] = nv

    wspec = pl.BlockSpec((None, tr, tc), lambda l, i, j: (l0 + l, i, j))
    return pl.pallas_call(
        body, grid=(nl, R // tr, C // tc),
        in_specs=[wspec, wspec, wspec, pl.BlockSpec((NDEV, None, rr, rc), lambda l, i, j: (0, l0 + l, rb + i, j))]
        + [pl.BlockSpec(memory_space=pl.ANY)] * len(prevs),
        out_specs=[wspec] * 4, out_shape=[jax.ShapeDtypeStruct((L, R, C), f32)] * 4,
        input_output_aliases={4 + q: q for q in range(len(prevs))},
        compiler_params=_cparams(("parallel", "parallel", "parallel")), name=name,
    )(w, m, v, recv, *prevs)


def adamw_slices_t(w, m, v, recv, *, name):
    L, R, C = w.shape

    def body(w_ref, m_ref, v_ref, r_ref, g_ref, d_ref, nm_ref, nv_ref):
        gt = r_ref[0].astype(f32)
        for k in range(1, NDEV):
            gt = gt + r_ref[k].astype(f32)
        g = gt.T
        delta, nm, nv = _adam_math(w_ref[...], g, m_ref[...], v_ref[...])
        g_ref[...] = g
        d_ref[...] = delta
        nm_ref[...] = nm
        nv_ref[...] = nv

    wspec = pl.BlockSpec((None, R, C), lambda l: (l, 0, 0))
    return pl.pallas_call(
        body, grid=(L,),
        in_specs=[wspec, wspec, wspec, pl.BlockSpec((NDEV, None, C, R), lambda l: (0, l, 0, 0))],
        out_specs=[wspec] * 4, out_shape=[jax.ShapeDtypeStruct((L, R, C), f32)] * 4,
        compiler_params=_cparams(("parallel",)), name=name,
    )(w, m, v, recv)


def adamw_plain(w, m, v, g, *, name):
    R, C = w.shape

    def body(w_ref, m_ref, v_ref, g_ref, d_ref, nm_ref, nv_ref):
        delta, nm, nv = _adam_math(w_ref[...], g_ref[...], m_ref[...], v_ref[...])
        d_ref[...] = delta
        nm_ref[...] = nm
        nv_ref[...] = nv

    return pl.pallas_call(
        body, out_shape=[jax.ShapeDtypeStruct((R, C), f32)] * 3, name=name,
    )(w, m, v, g)


def _chunk_causal_mask():
    pos = jnp.arange(GBLK)
    return (pos[None, :] // CHUNK) <= (pos[:, None] // CHUNK)


def kernel(x, norm_mix_g, norm_ffn_g, final_norm_g, a_w_in, a_v_norm_g, a_w_s, a_b_s, a_w_out, b_w_in, b_conv_w, b_w_out, c_w_in, c_w_grp, c_scale, c_w_out, f_w_gate, f_w_up, f_w_down, loss_target, m_norm_mix_g, m_norm_ffn_g, m_final_norm_g, m_a_w_in, m_a_v_norm_g, m_a_w_s, m_a_b_s, m_a_w_out, m_b_w_in, m_b_conv_w, m_b_w_out, m_c_w_in, m_c_w_grp, m_c_scale, m_c_w_out, m_f_w_gate, m_f_w_up, m_f_w_down, v_norm_mix_g, v_norm_ffn_g, v_final_norm_g, v_a_w_in, v_a_v_norm_g, v_a_w_s, v_a_b_s, v_a_w_out, v_b_w_in, v_b_conv_w, v_b_w_out, v_c_w_in, v_c_w_grp, v_c_scale, v_c_w_out, v_f_w_gate, v_f_w_up, v_f_w_down):
    B, S, _ = x.shape
    T = B * S
    tm = min(512, S)
    tmm = min(1024, T)
    tms = min(512, T)
    tma = min(256, S)
    me = _slot(lax.axis_index("x"), lax.axis_index("y"), lax.axis_index("c"))

    def tb(w):
        return jnp.transpose(w, (0, 2, 1)).astype(bf16)

    f_g, f_u, f_d = tb(f_w_gate), tb(f_w_up), f_w_down.astype(bf16)
    small = jnp.concatenate([a_v_norm_g, b_conv_w[0], c_scale, jnp.zeros((2, GBLK), f32)], axis=0)
    a_in_b, a_out_b = tb(a_w_in), a_w_out.astype(bf16)
    stage_shards = [
        [a_in_b[0], a_out_b[0], small],
        [f_g[0], f_u[0], f_d[0]],
        [tb(b_w_in)[0], b_w_out[0].astype(bf16), f_g[1], f_u[1], f_d[1]],
        [c_w_in[0].astype(bf16), c_w_grp[0].astype(bf16), c_w_out[0].astype(bf16), f_g[2], f_u[2], f_d[2]],
        [a_in_b[1], a_out_b[1], f_g[3], f_u[3], f_d[3]],
    ]
    started, tok = [], None
    for s in range(len(stage_shards)):
        started.append(gather_start(stage_shards[s], me, tok, name=f"gather_start_{s}"))
        tok = started[-1][4]

    def stage_weights(s, after):
        send_sems, recv_sems, shards, lands, _ = started[s]
        _, lands = gather_wait(send_sems, recv_sems, shards, lands, after, name=f"gather_wait_{s}")
        lands = gather_finish(lands, name=f"gather_finish_{s}")
        return [t.reshape(-1, t.shape[-1]) if t.ndim == 3 else t for t in lands]

    wm_masked = jnp.where(_chunk_causal_mask()[None, None], a_w_s, 0.0).astype(bf16)
    bsb = jnp.broadcast_to(a_b_s[:, :, :, None], a_b_s.shape + (GBLK,))

    xc = x.reshape(T, D)
    saved, weights = [], []
    for i in range(DEPTH):
        kind, j = i % 3, i // 3
        gmix = norm_mix_g[i][None]
        if i == 0:
            w_in, w_out, small_g = stage_weights(0, tok)
            small_full = jnp.transpose(small_g.reshape(NDEV, 8, GBLK), (1, 0, 2)).reshape(8, D)
            gv_full, cw_full, cs_full = small_full[0:2], small_full[2:5], small_full[5:6]
            w = [w_in, w_out]
        else:
            w = stage_weights(i + 1, xc)
        if kind == 0:
            w_in, w_out = w[:2]
            p, h = norm_mm(xc, gmix, w_in, tm=tms, transposed=True, name=f"a_in_{i}")
            y = a_mid_fwd(p, gv_full[j][None], wm_masked[j], bsb[j], tm=tma, name=f"a_mid_{i}")
        elif kind == 1:
            w_in, w_out = w[:2]
            p, h = norm_mm(xc, gmix, w_in, tm=tms, transposed=True, name=f"b_in_{i}")
            y = b_mid_fwd(p, cw_full, tm=tm, S=S, name=f"b_mid_{i}")
        else:
            w_in, w_grp, w_out = w[:3]
            p, h = norm_mm(xc, gmix, w_in, tm=tms, transposed=False, name=f"c_in_{i}")
            y = c_mid_fwd(p, w_grp, cs_full, tm=tm, S=S, name=f"c_mid_{i}")
        x1 = mm_res(y, w_out, xc, tm=tms, name=f"mix_out_{i}")
        if i == 0:
            w = w + stage_weights(1, x1)
        weights.append(w)
        w_fg, w_fu, w_fd = w[-3:]
        pg, pu, a, h2 = ffn_in(x1, norm_ffn_g[i][None], w_fg, w_fu, tm=tms, name=f"f_in_{i}")
        x2 = mm_res(a, w_fd, x1, tm=tms, name=f"f_out_{i}")
        saved.append((xc, p, h, y, x1, pg, pu, h2, a))
        xc = x2

    loss_blk, dx, d_final = final_loss(xc, final_norm_g[None], loss_target.reshape(T, D), tm=tms, name="final_loss")
    loss = lax.psum(loss_blk[0, 0], AXES)

    zone_shapes = [(2, 256, D), (2, GBLK, D), (1, 384, D), (1, GBLK, D), (1, GBLK, D), (1, GBLK, C_GDIM), (1, GBLK, D),
                   (DEPTH, F_LOC, D), (DEPTH, F_LOC, D), (DEPTH, F_LOC, D)]
    Z_AIN, Z_AOUT, Z_BIN, Z_BOUT, Z_CIN, Z_CGRP, Z_COUT, Z_FG, Z_FU, Z_FD = range(10)
    lands = [lax.empty((NDEV,) + s, bf16) for s in zone_shapes]
    parts = []

    def exchange(items, after, name):
        nonlocal lands
        for g, zi, l in items:
            own = lax.dynamic_index_in_dim(g, me, axis=0, keepdims=True)[None]
            lands[zi] = lax.dynamic_update_slice(lands[zi], own, (me, l) + (0,) * (g.ndim - 1))
        idx = sorted({zi for _, zi, _ in items})
        local = [(g, idx.index(zi), l) for g, zi, l in items]
        send_sems, recv_sems, gs, zs, token = exchange_start(local, [lands[zi] for zi in idx], after, name=name)
        for q, zi in enumerate(idx):
            lands[zi] = zs[q]
        parts.append(([(None, zi, l) for _, zi, l in items], send_sems, recv_sems, gs))
        return token

    d_mix, d_ffn = [None] * DEPTH, [None] * DEPTH
    d_gv, d_wm, d_bs = [None] * 2, [None] * 2, [None] * 2
    tok = None
    for i in reversed(range(DEPTH)):
        kind, j = i % 3, i // 3
        xin, p, h, y, x1, pg, pu, h2, a = saved[i]
        w = weights[i]
        w_in, w_out = w[0], w[-4]
        w_fg, w_fu, w_fd = w[-3:]
        dpg, dpu, dxb = ffn_bwd_mid(dx, w_fd, pg, pu, tm=tms, name=f"f_bwd_mid_{i}")
        items = [(mm_tn(dpg, h2, name=f"f_dw_gate_{i}"), Z_FG, i),
                 (mm_tn(dpu, h2, name=f"f_dw_up_{i}"), Z_FU, i),
                 (mm_tn(a, dxb, name=f"f_dw_out_{i}"), Z_FD, i)]
        if i == 0:
            tok = exchange(items, tok, "exchange_start_ffn0")
            items = []
        dx1, d_ffn[i] = mm_normbwd([dpg, dpu], [w_fg, w_fu], x1, norm_ffn_g[i][None], dx, tok, tm=tms, transposed=True,
                                   name=f"f_bwd_in_{i}")
        dy, dx1b = mm_nt(dx1, w_out, tm=tmm, name=f"mix_bwd_out_{i}")
        if kind == 0:
            dp, d_gv[j], d_wm[j], d_bs[j] = a_mid_bwd(p, dy, gv_full[j][None], wm_masked[j], bsb[j], tm=tma, name=f"a_bwd_mid_{i}")
            items += [(mm_tn(dp, h, name=f"a_dw_in_{i}"), Z_AIN, j),
                      (mm_tn(y, dx1b, name=f"a_dw_out_{i}"), Z_AOUT, j)]
        elif kind == 1:
            dp, d_cw = b_mid_bwd(p, dy, cw_full, tm=tm, S=S, name=f"b_bwd_mid_{i}")
            items += [(mm_tn(dp, h, name=f"b_dw_in_{i}"), Z_BIN, 0),
                      (mm_tn(y, dx1b, name=f"b_dw_out_{i}"), Z_BOUT, 0)]
        else:
            dp, d_cs, d_wgrp = c_mid_bwd(p, dy, w[1], cs_full, tm=tm, S=S, name=f"c_bwd_mid_{i}")
            gw_cgrp = jnp.transpose(d_wgrp.reshape(C_GROUPS, NDEV, C_GDIM // NDEV, C_GDIM), (1, 0, 2, 3)).astype(bf16)
            items += [(mm_tn(h, dp, name=f"c_dw_in_{i}"), Z_CIN, 0),
                      (gw_cgrp.reshape(NDEV, GBLK, C_GDIM), Z_CGRP, 0),
                      (mm_tn(y, dx1b, name=f"c_dw_out_{i}"), Z_COUT, 0)]
        tok = exchange(items, tok, f"exchange_start_{i}")
        dx, d_mix[i] = mm_normbwd([dp], [w_in], xin, norm_mix_g[i][None], dx1, tok, tm=tms, transposed=(kind != 2),
                                  name=f"mix_bwd_in_{i}")
    grad_x = dx.reshape(B, S, D)

    mask = _chunk_causal_mask()
    small_parts = [
        jnp.concatenate(d_mix, axis=0).reshape(-1, GBLK),
        jnp.concatenate(d_ffn, axis=0).reshape(-1, GBLK),
        d_final.reshape(-1, GBLK),
        jnp.concatenate(d_gv, axis=0).reshape(-1, GBLK),
        jnp.where(mask[None, None], jnp.stack(d_wm), 0.0).reshape(-1, GBLK),
        jnp.stack(d_bs).reshape(-1, GBLK),
        d_cw.reshape(-1, GBLK),
        d_cs.reshape(-1, GBLK),
    ]
    small_rows = [q.shape[0] for q in small_parts]
    sg_send, sg_recv, sg_shards, sg_lands, _ = gather_start(
        [jnp.concatenate(small_parts, axis=0)], me, tok, name="gather_small_start")

    def plain(w, m, v, g, name):
        shp = w.shape
        w2, m2, v2, g2 = (t.reshape(-1, shp[-1]) for t in (w, m, v, g))
        return tuple(t.reshape(shp) for t in adamw_plain(w2, m2, v2, g2, name=name))

    def sliced(w, m, v, recv, name, **kw):
        shp = w.shape
        w3, m3, v3 = (t.reshape((shp[0], -1, shp[-1])) for t in (w, m, v))
        return tuple(t.reshape(shp) for t in adamw_slices(w3, m3, v3, recv, name=name, **kw))

    def sliced_t(w, m, v, recv, name, **kw):
        wt, mt, vt = (jnp.transpose(t, (0, 2, 1)) for t in (w, m, v))
        return adamw_slices(wt, mt, vt, recv, name=name, **kw)

    def untransposed(outs):
        return tuple(jnp.transpose(t, (0, 2, 1)) for t in outs)

    lands = list(exchange_wait(parts[:3], lands, dx, name="exchange_wait"))
    _, _, r_bin, r_bout, r_cin, r_cgrp, r_cout, r_fg, r_fu, r_fd = lands
    res = {}
    res["b_w_in"] = adamw_slices_t(b_w_in, m_b_w_in, v_b_w_in, r_bin, name="adam_b_w_in")
    res["b_w_out"] = sliced(b_w_out, m_b_w_out, v_b_w_out, r_bout, "adam_b_w_out")
    res["c_w_in"] = sliced(c_w_in, m_c_w_in, v_c_w_in, r_cin, "adam_c_w_in")
    res["c_w_grp"] = sliced(c_w_grp, m_c_w_grp, v_c_w_grp, r_cgrp, "adam_c_w_grp")
    res["c_w_out"] = sliced(c_w_out, m_c_w_out, v_c_w_out, r_cout, "adam_c_w_out")
    fg = sliced_t(f_w_gate, m_f_w_gate, v_f_w_gate, r_fg, "adam_f_gate_123", tc=256, l0=1)
    fu = sliced_t(f_w_up, m_f_w_up, v_f_w_up, r_fu, "adam_f_up_123", tc=256, l0=1)
    fd = adamw_slices(f_w_down, m_f_w_down, v_f_w_down, r_fd, tc=256, l0=1, name="adam_f_down_123")
    big = [res[n][1] for n in ("b_w_in", "b_w_out", "c_w_in", "c_w_grp", "c_w_out")] + [fg[1], fu[1], fd[1]]

    ffn0_items, ffn0_send, ffn0_recv, ffn0_gs = parts[3]
    ffn0_zones = [Z_FG, Z_FU, Z_FD]
    ffn0_part = ([(None, ffn0_zones.index(zi), l) for _, zi, l in ffn0_items], ffn0_send, ffn0_recv, ffn0_gs)
    r_fg, r_fu, r_fd = exchange_wait([ffn0_part], [lands[zi] for zi in ffn0_zones], big, name="exchange_wait_ffn0")
    fg = sliced_t(f_w_gate, m_f_w_gate, v_f_w_gate, r_fg, "adam_f_gate_0", tc=256, l0=0, nl=1, prev=fg)
    fu = sliced_t(f_w_up, m_f_w_up, v_f_w_up, r_fu, "adam_f_up_0", tc=256, l0=0, nl=1, prev=fu)
    fd = adamw_slices(f_w_down, m_f_w_down, v_f_w_down, r_fd, tc=256, l0=0, nl=1, prev=fd, name="adam_f_down_0")
    res["f_w_gate"], res["f_w_up"], res["f_w_down"] = untransposed(fg), untransposed(fu), tuple(fd)

    last_items, last_send, last_recv, last_gs = parts[4]
    last_zones = [Z_AIN, Z_AOUT]
    last_part = ([(None, last_zones.index(zi), l) for _, zi, l in last_items], last_send, last_recv, last_gs)
    r_ain, r_aout = exchange_wait([last_part], [lands[zi] for zi in last_zones], [fg[1], fu[1], fd[1]], name="exchange_wait_last")
    res["a_w_in"] = adamw_slices_t(a_w_in, m_a_w_in, v_a_w_in, r_ain, name="adam_a_w_in")
    res["a_w_out"] = sliced(a_w_out, m_a_w_out, v_a_w_out, r_aout, "adam_a_w_out")

    _, sg_lands = gather_wait(sg_send, sg_recv, sg_shards, sg_lands, [res["a_w_in"][1], res["a_w_out"][1]], name="gather_small_wait")
    (gs_all,) = gather_finish(sg_lands, name="gather_small_finish")
    gs = sum8(gs_all, name="sum_small_grads")
    offs = [0]
    for r in small_rows:
        offs.append(offs[-1] + r)
    sp = [gs[offs[q]:offs[q + 1]] for q in range(len(small_parts))]
    grad_norm_mix_g = sp[0].reshape(DEPTH, D)
    grad_norm_ffn_g = sp[1].reshape(DEPTH, D)
    grad_final_norm_g = sp[2].reshape(D)
    grad_a_w_s = sp[4].reshape(a_w_s.shape)
    grad_a_b_s = sp[5].reshape(a_b_s.shape)

    def my_cols(full):
        return lax.dynamic_slice_in_dim(full, me * GBLK, GBLK, axis=1)

    grad_a_v_norm_g = my_cols(sp[3].reshape(2, D))
    grad_b_conv_w = my_cols(sp[6].reshape(3, D))[None]
    grad_c_scale = my_cols(sp[7].reshape(1, D))
    res["norm_mix_g"] = (grad_norm_mix_g,) + plain(norm_mix_g, m_norm_mix_g, v_norm_mix_g, grad_norm_mix_g, "adam_norm_mix")
    res["norm_ffn_g"] = (grad_norm_ffn_g,) + plain(norm_ffn_g, m_norm_ffn_g, v_norm_ffn_g, grad_norm_ffn_g, "adam_norm_ffn")
    res["final_norm_g"] = (grad_final_norm_g,) + tuple(
        t.reshape(D) for t in plain(final_norm_g[None], m_final_norm_g[None], v_final_norm_g[None], grad_final_norm_g[None], "adam_final"))
    res["a_v_norm_g"] = (grad_a_v_norm_g,) + plain(a_v_norm_g, m_a_v_norm_g, v_a_v_norm_g, grad_a_v_norm_g, "adam_a_v_norm")
    res["a_w_s"] = (grad_a_w_s,) + plain(a_w_s, m_a_w_s, v_a_w_s, grad_a_w_s, "adam_a_w_s")
    res["a_b_s"] = (grad_a_b_s,) + plain(a_b_s, m_a_b_s, v_a_b_s, grad_a_b_s, "adam_a_b_s")
    res["b_conv_w"] = (grad_b_conv_w,) + plain(b_conv_w, m_b_conv_w, v_b_conv_w, grad_b_conv_w, "adam_b_conv")
    res["c_scale"] = (grad_c_scale,) + plain(c_scale, m_c_scale, v_c_scale, grad_c_scale, "adam_c_scale")

    order = ["norm_mix_g", "norm_ffn_g", "final_norm_g", "a_w_in", "a_v_norm_g", "a_w_s", "a_b_s", "a_w_out", "b_w_in",
             "b_conv_w", "b_w_out", "c_w_in", "c_w_grp", "c_scale", "c_w_out", "f_w_gate", "f_w_up", "f_w_down"]
    return (loss, grad_x, *[res[n][0] for n in order], *[res[n][1] for n in order],
            *[res[n][2] for n in order], *[res[n][3] for n in order])
```

```python
import functools

import jax
import jax.numpy as jnp
from jax import lax
from jax.experimental import pallas as pl
from jax.experimental.pallas import tpu as pltpu

f32 = jnp.float32
bf16 = jnp.bfloat16

NDEV = 8
D = 1024
EPS = 1e-6
CHUNK = 64
GBLK = 128
A_GROUPS = 8
C_GROUPS = 4
C_GDIM = 256
POOL_WINDOWS = (2, 4, 8, 16)
HALO = 16
COLS = 256
F_LOC = 352
MXU_COLS = 256
DEPTH = 4
AXES = ("x", "y", "c")
MESH = pl.DeviceIdType.MESH

ADAM_LR = 0.001
ADAM_B1 = 0.9
ADAM_B2 = 0.999
ADAM_EPS = 1e-08
ADAM_WD = 0.01
ADAM_STEP = 10

VMEM_LIMIT = 56 * 1024 * 1024


def _cparams(sem):
    return pltpu.CompilerParams(dimension_semantics=sem, vmem_limit_bytes=VMEM_LIMIT)


def _gelu(z):
    return 0.5 * z * (1.0 + lax.erf(z * 0.7071067811865476))


def _gelu_cdf(z):
    return 0.5 * (1.0 + lax.erf(z * 0.7071067811865476))


def _gelu_grad(z, cdf):
    return cdf + z * jnp.exp(-0.5 * z * z) * 0.3989422804014327


def _dot(a, b):
    return jnp.dot(a, b, preferred_element_type=f32)


def _dot_nt(a, b):
    return lax.dot_general(a, b, (((1,), (1,)), ((), ())), preferred_element_type=f32)


def _dot_tn(a, b):
    return lax.dot_general(a, b, (((0,), (0,)), ((), ())), preferred_element_type=f32)


def _col_chunks(n, width=1024):
    return [(c, min(c + width, n)) for c in range(0, n, width)]


def norm_mm(x, g, w, *, tm, transposed, name):
    T = x.shape[0]
    n = w.shape[0] if transposed else w.shape[1]

    def body(x_ref, g_ref, w_ref, p_ref, h_ref):
        xv = x_ref[...]
        r = lax.rsqrt(jnp.mean(xv * xv, axis=-1, keepdims=True) + EPS)
        h = (xv * r * g_ref[...]).astype(bf16)
        h_ref[...] = h
        p = _dot_nt(h, w_ref[...]) if transposed else _dot(h, w_ref[...])
        p_ref[...] = p.astype(bf16)

    return pl.pallas_call(
        body, grid=(T // tm,),
        in_specs=[pl.BlockSpec((tm, D), lambda i: (i, 0)), pl.BlockSpec((1, D), lambda i: (0, 0)),
                  pl.BlockSpec(w.shape, lambda i: (0, 0), pipeline_mode=pl.Buffered(1))],
        out_specs=[pl.BlockSpec((tm, n), lambda i: (i, 0)), pl.BlockSpec((tm, D), lambda i: (i, 0))],
        out_shape=[jax.ShapeDtypeStruct((T, n), bf16), jax.ShapeDtypeStruct((T, D), bf16)],
        compiler_params=_cparams(("parallel",)), name=name,
    )(x, g, w)


def ffn_in(y, w_out, xin, g, wtg, wtu, *, tm, name):
    T, K = y.shape
    H = wtg.shape[0]

    def body(y_ref, wo_ref, xin_ref, g_ref, wg_ref, wu_ref, x_ref, pg_ref, pu_ref, a_ref, h_ref):
        xv = xin_ref[...] + _dot(y_ref[...], wo_ref[...])
        x_ref[...] = xv
        r = lax.rsqrt(jnp.mean(xv * xv, axis=-1, keepdims=True) + EPS)
        h = (xv * r * g_ref[...]).astype(bf16)
        h_ref[...] = h
        for c0, c1 in _col_chunks(H):
            gate = _dot_nt(h, wg_ref[c0:c1, :])
            up = _dot_nt(h, wu_ref[c0:c1, :])
            pg_ref[:, c0:c1] = gate.astype(bf16)
            pu_ref[:, c0:c1] = up.astype(bf16)
            a_ref[:, c0:c1] = (gate * jax.nn.sigmoid(gate) * up).astype(bf16)

    wspec = pl.BlockSpec((H, D), lambda i: (0, 0), pipeline_mode=pl.Buffered(1))
    hspec = pl.BlockSpec((tm, H), lambda i: (i, 0))
    dspec = pl.BlockSpec((tm, D), lambda i: (i, 0))
    return pl.pallas_call(
        body, grid=(T // tm,),
        in_specs=[pl.BlockSpec((tm, K), lambda i: (i, 0)),
                  pl.BlockSpec((K, D), lambda i: (0, 0), pipeline_mode=pl.Buffered(1)),
                  dspec, pl.BlockSpec((1, D), lambda i: (0, 0)), wspec, wspec],
        out_specs=[dspec, hspec, hspec, hspec, dspec],
        out_shape=[jax.ShapeDtypeStruct((T, D), f32)] + [jax.ShapeDtypeStruct((T, H), bf16)] * 3
        + [jax.ShapeDtypeStruct((T, D), bf16)],
        compiler_params=_cparams(("parallel",)), name=name,
    )(y, w_out, xin, g, wtg, wtu)


def mm_res(y, w, x, *, tm, name):
    T, K = y.shape

    def body(y_ref, w_ref, x_ref, o_ref):
        o_ref[...] = x_ref[...] + _dot(y_ref[...], w_ref[...])

    return pl.pallas_call(
        body, grid=(T // tm,),
        in_specs=[pl.BlockSpec((tm, K), lambda i: (i, 0)),
                  pl.BlockSpec((K, D), lambda i: (0, 0), pipeline_mode=pl.Buffered(1)),
                  pl.BlockSpec((tm, D), lambda i: (i, 0))],
        out_specs=pl.BlockSpec((tm, D), lambda i: (i, 0)),
        out_shape=jax.ShapeDtypeStruct((T, D), f32),
        compiler_params=_cparams(("parallel",)), name=name,
    )(y, w, x)


def ffn_bwd_mid(dx, wd, pg, pu, *, tm, name):
    T = dx.shape[0]
    H = wd.shape[0]

    def body(dx_ref, w_ref, pg_ref, pu_ref, dg_ref, du_ref, dxb_ref):
        dxb = dx_ref[...].astype(bf16)
        dxb_ref[...] = dxb
        for c0, c1 in _col_chunks(H):
            da = _dot_nt(dxb, w_ref[c0:c1, :])
            gate = pg_ref[:, c0:c1].astype(f32)
            up = pu_ref[:, c0:c1].astype(f32)
            sg = jax.nn.sigmoid(gate)
            dg_ref[:, c0:c1] = (da * up * (sg * (1.0 + gate * (1.0 - sg)))).astype(bf16)
            du_ref[:, c0:c1] = (da * gate * sg).astype(bf16)

    hspec = pl.BlockSpec((tm, H), lambda i: (i, 0))
    return pl.pallas_call(
        body, grid=(T // tm,),
        in_specs=[pl.BlockSpec((tm, D), lambda i: (i, 0)),
                  pl.BlockSpec((H, D), lambda i: (0, 0), pipeline_mode=pl.Buffered(1)), hspec, hspec],
        out_specs=[hspec, hspec, pl.BlockSpec((tm, D), lambda i: (i, 0))],
        out_shape=[jax.ShapeDtypeStruct((T, H), bf16)] * 2 + [jax.ShapeDtypeStruct((T, D), bf16)],
        compiler_params=_cparams(("parallel",)), name=name,
    )(dx, wd, pg, pu)


def mm_normbwd(dps, ws, x, g, dres, tok, *, tm, transposed, w_next=None, name):
    T = x.shape[0]
    nq = len(dps)
    toks = [] if tok is None else [tok]
    nexts = [] if w_next is None else [w_next]
    n_in = 2 * nq + 3 + len(toks) + len(nexts)

    def body(*refs):
        dp_refs, w_refs = refs[:nq], refs[nq:2 * nq]
        x_ref, g_ref, dres_ref = refs[2 * nq:2 * nq + 3]
        dx_ref, dg_ref = refs[n_in:n_in + 2]

        @pl.when(pl.program_id(0) == 0)
        def _():
            dg_ref[...] = jnp.zeros(dg_ref.shape, f32)

        dh = None
        for q in range(nq):
            if transposed:
                part = _dot(dp_refs[q][...], w_refs[q][...])
            else:
                part = _dot_nt(dp_refs[q][...], w_refs[q][...])
            dh = part if dh is None else dh + part
        gv = g_ref[...] + refs[2 * nq + 3][0:1, 0:1] if toks else g_ref[...]
        xv = x_ref[...]
        r = lax.rsqrt(jnp.mean(xv * xv, axis=-1, keepdims=True) + EPS)
        xhat = xv * r
        dg_ref[...] += jnp.sum(dh * xhat, axis=0, keepdims=True)
        dxhat = dh * gv
        dxv = dres_ref[...] + r * (dxhat - xhat * jnp.mean(dxhat * xhat, axis=-1, keepdims=True))
        dx_ref[...] = dxv
        if nexts:
            dy_ref, dxb_ref = refs[n_in + 2:]
            dxb = dxv.astype(bf16)
            dxb_ref[...] = dxb
            dy_ref[...] = _dot_nt(dxb, refs[n_in - 1][...]).astype(bf16)

    dspec = pl.BlockSpec((tm, D), lambda i: (i, 0))
    out_specs = [dspec, pl.BlockSpec((1, D), lambda i: (0, 0))]
    out_shape = [jax.ShapeDtypeStruct((T, D), f32), jax.ShapeDtypeStruct((1, D), f32)]
    if nexts:
        kn = w_next.shape[0]
        out_specs += [pl.BlockSpec((tm, kn), lambda i: (i, 0)), dspec]
        out_shape += [jax.ShapeDtypeStruct((T, kn), bf16), jax.ShapeDtypeStruct((T, D), bf16)]
    return pl.pallas_call(
        body, grid=(T // tm,),
        in_specs=[pl.BlockSpec((tm, dp.shape[1]), lambda i: (i, 0)) for dp in dps]
        + [pl.BlockSpec(w.shape, lambda i: (0, 0), pipeline_mode=pl.Buffered(1)) for w in ws]
        + [dspec, pl.BlockSpec((1, D), lambda i: (0, 0)), dspec] + [pl.BlockSpec((8, 128), lambda i: (0, 0))] * len(toks)
        + [pl.BlockSpec(w.shape, lambda i: (0, 0), pipeline_mode=pl.Buffered(1)) for w in nexts],
        out_specs=out_specs, out_shape=out_shape,
        compiler_params=_cparams(("arbitrary",)), name=name,
    )(*dps, *ws, x, g, dres, *toks, *nexts)


def mm_tn(a, b, *, name):
    T, K = a.shape
    n = b.shape[1]

    def body(a_ref, b_ref, o_ref):
        o_ref[...] = _dot_tn(a_ref[...], b_ref[...]).astype(bf16)

    out = pl.pallas_call(
        body, grid=(K // MXU_COLS,),
        in_specs=[pl.BlockSpec((T, MXU_COLS), lambda d: (0, d)),
                  pl.BlockSpec((T, n), lambda d: (0, 0), pipeline_mode=pl.Buffered(1))],
        out_specs=pl.BlockSpec((MXU_COLS, n), lambda d: (d, 0)),
        out_shape=jax.ShapeDtypeStruct((K, n), bf16),
        compiler_params=_cparams(("parallel",)), name=name,
    )(a, b)
    return out.reshape(NDEV, K // NDEV, n)


def ffn_dw(dpg, dpu, a, h2, dxb, *, name):
    T, H = dpg.shape

    def body(g_ref, u_ref, a_ref, h_ref, x_ref, og_ref, ou_ref, od_ref):
        h = h_ref[...]
        og_ref[...] = _dot_tn(g_ref[...], h).astype(bf16)
        ou_ref[...] = _dot_tn(u_ref[...], h).astype(bf16)
        od_ref[...] = _dot_tn(a_ref[...], x_ref[...]).astype(bf16)

    blk = pl.BlockSpec((T, MXU_COLS), lambda d: (0, d))
    full = pl.BlockSpec((T, D), lambda d: (0, 0), pipeline_mode=pl.Buffered(1))
    out = pl.BlockSpec((MXU_COLS, D), lambda d: (d, 0))
    outs = pl.pallas_call(
        body, grid=(H // MXU_COLS,),
        in_specs=[blk, blk, blk, full, full], out_specs=[out] * 3,
        out_shape=[jax.ShapeDtypeStruct((H, D), bf16)] * 3,
        compiler_params=_cparams(("parallel",)), name=name,
    )(dpg, dpu, a, h2, dxb)
    return [o.reshape(NDEV, H // NDEV, D) for o in outs]


def _prev_halo_spec(tm, ncol):
    return pl.BlockSpec((HALO, ncol), lambda i: (jnp.maximum(i * (tm // HALO) - 1, 0), 0))


def _next_halo_spec(tm, ncol, T):
    return pl.BlockSpec((HALO, ncol), lambda i: (jnp.minimum((i + 1) * (tm // HALO), T // HALO - 1), 0))


def a_mid_fwd(p, gv, wm, bsb, *, tm, name):
    T = p.shape[0]

    def body(p_ref, gv_ref, wm_ref, bs_ref, y_ref, vn_sc):
        v = _gelu(p_ref[:, D:].astype(f32))
        vc = v - jnp.mean(v, axis=-1, keepdims=True)
        var = jnp.mean(vc * vc, axis=-1, keepdims=True)
        vn_sc[...] = (vc * lax.rsqrt(var + EPS) * gv_ref[...]).astype(bf16)
        for g in range(A_GROUPS):
            cs = slice(g * GBLK, (g + 1) * GBLK)
            for r in range(tm // GBLK):
                rs = slice(r * GBLK, (r + 1) * GBLK)
                sv = _dot(wm_ref[g], vn_sc[rs, cs]) + bs_ref[g]
                y_ref[rs, cs] = (_gelu(p_ref[rs, cs].astype(f32)) * sv).astype(bf16)

    return pl.pallas_call(
        body, grid=(T // tm,),
        in_specs=[pl.BlockSpec((tm, 2 * D), lambda i: (i, 0)), pl.BlockSpec((1, D), lambda i: (0, 0)),
                  pl.BlockSpec((A_GROUPS, GBLK, GBLK), lambda i: (0, 0, 0)),
                  pl.BlockSpec((A_GROUPS, GBLK, GBLK), lambda i: (0, 0, 0))],
        out_specs=pl.BlockSpec((tm, D), lambda i: (i, 0)),
        out_shape=jax.ShapeDtypeStruct((T, D), bf16),
        scratch_shapes=[pltpu.VMEM((tm, D), bf16)],
        compiler_params=_cparams(("parallel",)), name=name,
    )(p, gv, wm, bsb)


def a_mid_bwd(p, dy, gv, wm, bsb, *, tm, name):
    T = p.shape[0]

    def body(p_ref, dy_ref, gv_ref, wm_ref, bs_ref, dp_ref, dgv_ref, dwm_ref, dbs_ref, vn_sc, dvn_sc):
        @pl.when(pl.program_id(0) == 0)
        def _():
            dgv_ref[...] = jnp.zeros(dgv_ref.shape, f32)
            dwm_ref[...] = jnp.zeros(dwm_ref.shape, f32)
            dbs_ref[...] = jnp.zeros(dbs_ref.shape, f32)

        zv = p_ref[:, D:].astype(f32)
        cdf_v = _gelu_cdf(zv)
        v = zv * cdf_v
        vc = v - jnp.mean(v, axis=-1, keepdims=True)
        rstd = lax.rsqrt(jnp.mean(vc * vc, axis=-1, keepdims=True) + EPS)
        vhat = vc * rstd
        vn_sc[...] = (vhat * gv_ref[...]).astype(bf16)
        for g in range(A_GROUPS):
            cs = slice(g * GBLK, (g + 1) * GBLK)
            dwm = jnp.zeros((GBLK, GBLK), f32)
            dbs = jnp.zeros((GBLK, 1), f32)
            for r in range(tm // GBLK):
                rs = slice(r * GBLK, (r + 1) * GBLK)
                zu = p_ref[rs, cs].astype(f32)
                cdf_u = _gelu_cdf(zu)
                vn = vn_sc[rs, cs]
                sv = _dot(wm_ref[g], vn) + bs_ref[g]
                dyb = dy_ref[rs, cs].astype(f32)
                dsv = dyb * (zu * cdf_u)
                dsvb = dsv.astype(bf16)
                dp_ref[rs, cs] = (dyb * sv * _gelu_grad(zu, cdf_u)).astype(bf16)
                dwm += _dot_nt(dsvb, vn)
                dbs += jnp.sum(dsv, axis=1, keepdims=True)
                dvn_sc[rs, cs] = _dot_tn(wm_ref[g], dsvb)
            dwm_ref[g] += dwm
            dbs_ref[g] += dbs
        dvn = dvn_sc[...]
        dgv_ref[...] += jnp.sum(dvn * vhat, axis=0, keepdims=True)
        dvhat = dvn * gv_ref[...]
        dv = rstd * (dvhat - jnp.mean(dvhat, axis=-1, keepdims=True)
                     - vhat * jnp.mean(dvhat * vhat, axis=-1, keepdims=True))
        dp_ref[:, D:] = (dv * _gelu_grad(zv, cdf_v)).astype(bf16)

    return pl.pallas_call(
        body, grid=(T // tm,),
        in_specs=[pl.BlockSpec((tm, 2 * D), lambda i: (i, 0)), pl.BlockSpec((tm, D), lambda i: (i, 0)),
                  pl.BlockSpec((1, D), lambda i: (0, 0)),
                  pl.BlockSpec((A_GROUPS, GBLK, GBLK), lambda i: (0, 0, 0)),
                  pl.BlockSpec((A_GROUPS, GBLK, GBLK), lambda i: (0, 0, 0))],
        out_specs=[pl.BlockSpec((tm, 2 * D), lambda i: (i, 0)), pl.BlockSpec((1, D), lambda i: (0, 0)),
                   pl.BlockSpec((A_GROUPS, GBLK, GBLK), lambda i: (0, 0, 0)),
                   pl.BlockSpec((A_GROUPS, GBLK, 1), lambda i: (0, 0, 0))],
        out_shape=[jax.ShapeDtypeStruct((T, 2 * D), bf16), jax.ShapeDtypeStruct((1, D), f32),
                   jax.ShapeDtypeStruct((A_GROUPS, GBLK, GBLK), f32), jax.ShapeDtypeStruct((A_GROUPS, GBLK, 1), f32)],
        scratch_shapes=[pltpu.VMEM((tm, D), bf16), pltpu.VMEM((tm, D), f32)],
        compiler_params=_cparams(("arbitrary",)), name=name,
    )(p, dy, gv, wm, bsb)


def b_mid_fwd(p, cw, *, tm, S, name):
    T = p.shape[0]
    nst = S // tm

    def body(p_ref, ph_ref, cw_ref, y_ref, ext):
        first = (pl.program_id(0) % nst) == 0
        for c0 in range(0, D, COLS):
            cs, cc, cx = slice(c0, c0 + COLS), slice(D + c0, D + c0 + COLS), slice(2 * D + c0, 2 * D + c0 + COLS)
            q = p_ref[:, cc].astype(f32) * p_ref[:, cx].astype(f32)
            qh = ph_ref[:, cc].astype(f32) * ph_ref[:, cx].astype(f32)
            ext[0:HALO, cs] = jnp.where(first, 0.0, qh)
            ext[HALO:, cs] = q
            y = (cw_ref[0:1, cs] * ext[pl.ds(HALO - 2, tm), cs] + cw_ref[1:2, cs] * ext[pl.ds(HALO - 1, tm), cs]
                 + cw_ref[2:3, cs] * q)
            y_ref[:, cs] = (p_ref[:, cs].astype(f32) * y).astype(bf16)

    return pl.pallas_call(
        body, grid=(T // tm,),
        in_specs=[pl.BlockSpec((tm, 3 * D), lambda i: (i, 0)), _prev_halo_spec(tm, 3 * D),
                  pl.BlockSpec((3, D), lambda i: (0, 0))],
        out_specs=pl.BlockSpec((tm, D), lambda i: (i, 0)),
        out_shape=jax.ShapeDtypeStruct((T, D), bf16),
        scratch_shapes=[pltpu.VMEM((tm + HALO, D), f32)],
        compiler_params=_cparams(("parallel",)), name=name,
    )(p, p, cw)


def b_mid_bwd(p, dy, cw, *, tm, S, name):
    T = p.shape[0]
    nst = S // tm

    def body(p_ref, ph_ref, pn_ref, dy_ref, dyn_ref, cw_ref, dp_ref, dcw_ref, ext, ext2):
        i = pl.program_id(0)
        first = (i % nst) == 0
        last = (i % nst) == nst - 1

        @pl.when(i == 0)
        def _():
            dcw_ref[...] = jnp.zeros(dcw_ref.shape, f32)

        for c0 in range(0, D, COLS):
            cs, cc, cx = slice(c0, c0 + COLS), slice(D + c0, D + c0 + COLS), slice(2 * D + c0, 2 * D + c0 + COLS)
            gb, gc, xt = p_ref[:, cs].astype(f32), p_ref[:, cc].astype(f32), p_ref[:, cx].astype(f32)
            q = gc * xt
            ext[0:HALO, cs] = jnp.where(first, 0.0, ph_ref[:, cc].astype(f32) * ph_ref[:, cx].astype(f32))
            ext[HALO:, cs] = q
            q2 = ext[pl.ds(HALO - 2, tm), cs]
            q1 = ext[pl.ds(HALO - 1, tm), cs]
            y = cw_ref[0:1, cs] * q2 + cw_ref[1:2, cs] * q1 + cw_ref[2:3, cs] * q
            dyo = dy_ref[:, cs].astype(f32)
            dp_ref[:, cs] = (dyo * y).astype(bf16)
            dyc = dyo * gb
            ext2[0:tm, cs] = dyc
            ext2[tm:, cs] = jnp.where(last, 0.0, dyn_ref[:, cs].astype(f32) * pn_ref[:, cs].astype(f32))
            dq = cw_ref[2:3, cs] * dyc + cw_ref[1:2, cs] * ext2[pl.ds(1, tm), cs] + cw_ref[0:1, cs] * ext2[pl.ds(2, tm), cs]
            dp_ref[:, cc] = (dq * xt).astype(bf16)
            dp_ref[:, cx] = (dq * gc).astype(bf16)
            dcw_ref[0:1, cs] += jnp.sum(dyc * q2, axis=0, keepdims=True)
            dcw_ref[1:2, cs] += jnp.sum(dyc * q1, axis=0, keepdims=True)
            dcw_ref[2:3, cs] += jnp.sum(dyc * q, axis=0, keepdims=True)

    return pl.pallas_call(
        body, grid=(T // tm,),
        in_specs=[pl.BlockSpec((tm, 3 * D), lambda i: (i, 0)), _prev_halo_spec(tm, 3 * D), _next_halo_spec(tm, 3 * D, T),
                  pl.BlockSpec((tm, D), lambda i: (i, 0)), _next_halo_spec(tm, D, T),
                  pl.BlockSpec((3, D), lambda i: (0, 0))],
        out_specs=[pl.BlockSpec((tm, 3 * D), lambda i: (i, 0)), pl.BlockSpec((3, D), lambda i: (0, 0))],
        out_shape=[jax.ShapeDtypeStruct((T, 3 * D), bf16), jax.ShapeDtypeStruct((3, D), f32)],
        scratch_shapes=[pltpu.VMEM((tm + HALO, D), f32), pltpu.VMEM((tm + HALO, D), f32)],
        compiler_params=_cparams(("arbitrary",)), name=name,
    )(p, p, p, dy, dy, cw)


def _pool_counts(i, nst, tm, rows, row0, w):
    t = (i % nst) * tm + row0 + lax.broadcasted_iota(jnp.int32, (rows, 1), 0)
    return jnp.minimum(t + 1, w).astype(f32)


def _pool_diff(p_ref, ext, g, i, nst, tm):
    w = POOL_WINDOWS[g]
    cs = slice(g * C_GDIM, (g + 1) * C_GDIM)
    pg = p_ref[:, cs].astype(f32)
    s = pg
    for k in range(1, w):
        s = s + ext[pl.ds(HALO - k, tm), cs]
    return s / _pool_counts(i, nst, tm, tm, 0, w) - pg


def c_mid_fwd(p, wg, scale, *, tm, S, name):
    T = p.shape[0]
    nst = S // tm

    def body(p_ref, ph_ref, wg_ref, sc_ref, y_ref, ext):
        i = pl.program_id(0)
        first = (i % nst) == 0
        ext[0:HALO, :] = jnp.where(first, 0.0, ph_ref[...].astype(f32))
        ext[HALO:, :] = p_ref[...].astype(f32)
        for g in range(C_GROUPS):
            cs = slice(g * C_GDIM, (g + 1) * C_GDIM)
            dg = _pool_diff(p_ref, ext, g, i, nst, tm).astype(bf16)
            wv = wg_ref[:, g].reshape(C_GDIM, C_GDIM)
            y_ref[:, cs] = (_dot(dg, wv) * sc_ref[:, cs]).astype(bf16)

    return pl.pallas_call(
        body, grid=(T // tm,),
        in_specs=[pl.BlockSpec((tm, D), lambda i: (i, 0)), _prev_halo_spec(tm, D),
                  pl.BlockSpec((NDEV, C_GROUPS, C_GDIM // NDEV, C_GDIM), lambda i: (0, 0, 0, 0)),
                  pl.BlockSpec((1, D), lambda i: (0, 0))],
        out_specs=pl.BlockSpec((tm, D), lambda i: (i, 0)),
        out_shape=jax.ShapeDtypeStruct((T, D), bf16),
        scratch_shapes=[pltpu.VMEM((tm + HALO, D), f32)],
        compiler_params=_cparams(("parallel",)), name=name,
    )(p, p, wg, scale)


def c_mid_bwd(p, dy, wg, scale, *, tm, S, name):
    T = p.shape[0]
    nst = S // tm

    def body(p_ref, ph_ref, dy_ref, dyn_ref, wg_ref, sc_ref, dp_ref, dsc_ref, dwg_ref, ext, ext2):
        i = pl.program_id(0)
        first = (i % nst) == 0
        last = (i % nst) == nst - 1

        @pl.when(i == 0)
        def _():
            dsc_ref[...] = jnp.zeros(dsc_ref.shape, f32)
            dwg_ref[...] = jnp.zeros(dwg_ref.shape, f32)

        ext[0:HALO, :] = jnp.where(first, 0.0, ph_ref[...].astype(f32))
        ext[HALO:, :] = p_ref[...].astype(f32)
        for g in range(C_GROUPS):
            w = POOL_WINDOWS[g]
            cs = slice(g * C_GDIM, (g + 1) * C_GDIM)
            dg = _pool_diff(p_ref, ext, g, i, nst, tm).astype(bf16)
            wv = wg_ref[:, g].reshape(C_GDIM, C_GDIM)
            dyo = dy_ref[:, cs].astype(f32)
            dsc_ref[:, cs] += jnp.sum(dyo * _dot(dg, wv), axis=0, keepdims=True)
            dyp = (dyo * sc_ref[:, cs]).astype(bf16)
            dypn = (dyn_ref[:, cs].astype(f32) * sc_ref[:, cs]).astype(bf16)
            dwg_ref[g] += _dot_tn(dg, dyp)
            dd = _dot_nt(dyp, wv)
            ddn = _dot_nt(dypn, wv)
            ext2[0:tm, cs] = dd / _pool_counts(i, nst, tm, tm, 0, w)
            ext2[tm:, cs] = jnp.where(last, 0.0, ddn / _pool_counts(i, nst, tm, HALO, tm, w))
            s = -dd
            for k in range(w):
                s = s + ext2[pl.ds(k, tm), cs]
            dp_ref[:, cs] = s.astype(bf16)

    return pl.pallas_call(
        body, grid=(T // tm,),
        in_specs=[pl.BlockSpec((tm, D), lambda i: (i, 0)), _prev_halo_spec(tm, D),
                  pl.BlockSpec((tm, D), lambda i: (i, 0)), _next_halo_spec(tm, D, T),
                  pl.BlockSpec((NDEV, C_GROUPS, C_GDIM // NDEV, C_GDIM), lambda i: (0, 0, 0, 0)),
                  pl.BlockSpec((1, D), lambda i: (0, 0))],
        out_specs=[pl.BlockSpec((tm, D), lambda i: (i, 0)), pl.BlockSpec((1, D), lambda i: (0, 0)),
                   pl.BlockSpec((C_GROUPS, C_GDIM, C_GDIM), lambda i: (0, 0, 0))],
        out_shape=[jax.ShapeDtypeStruct((T, D), bf16), jax.ShapeDtypeStruct((1, D), f32),
                   jax.ShapeDtypeStruct((C_GROUPS, C_GDIM, C_GDIM), f32)],
        scratch_shapes=[pltpu.VMEM((tm + HALO, D), f32), pltpu.VMEM((tm + HALO, D), f32)],
        compiler_params=_cparams(("arbitrary",)), name=name,
    )(p, p, dy, dy, wg, scale)


def final_loss(x, g, tgt, *, tm, name):
    T = x.shape[0]

    def body(x_ref, g_ref, t_ref, loss_ref, dx_ref, dg_ref):
        @pl.when(pl.program_id(0) == 0)
        def _():
            loss_ref[...] = jnp.zeros(loss_ref.shape, f32)
            dg_ref[...] = jnp.zeros(dg_ref.shape, f32)

        xv = x_ref[...]
        r = lax.rsqrt(jnp.mean(xv * xv, axis=-1, keepdims=True) + EPS)
        xhat = xv * r
        err = xhat * g_ref[...] - t_ref[...]
        loss_ref[...] += 0.5 * jnp.sum(jnp.mean(err * err, axis=-1, keepdims=True))
        dy = err * (1.0 / D)
        dg_ref[...] += jnp.sum(dy * xhat, axis=0, keepdims=True)
        dxhat = dy * g_ref[...]
        dx_ref[...] = r * (dxhat - xhat * jnp.mean(dxhat * xhat, axis=-1, keepdims=True))

    return pl.pallas_call(
        body, grid=(T // tm,),
        in_specs=[pl.BlockSpec((tm, D), lambda i: (i, 0)), pl.BlockSpec((1, D), lambda i: (0, 0)),
                  pl.BlockSpec((tm, D), lambda i: (i, 0))],
        out_specs=[pl.BlockSpec((8, 128), lambda i: (0, 0)), pl.BlockSpec((tm, D), lambda i: (i, 0)),
                   pl.BlockSpec((1, D), lambda i: (0, 0))],
        out_shape=[jax.ShapeDtypeStruct((8, 128), f32), jax.ShapeDtypeStruct((T, D), f32),
                   jax.ShapeDtypeStruct((1, D), f32)],
        compiler_params=_cparams(("arbitrary",)), name=name,
    )(x, g, tgt)


def _slot(px, py, pc):
    return 4 * px + 2 * py + pc


def _with_own_block(s, me):
    zone = lax.empty((NDEV,) + s.shape, s.dtype)
    return lax.dynamic_update_slice(zone, s[None], (me,) + (0,) * s.ndim)


def all_gather(arrs, me, *, name):
    n = len(arrs)

    def body(*refs):
        ins, outs = refs[:n], refs[2 * n:3 * n]
        send_sems, recv_sems = refs[3 * n:]
        x, y, c = lax.axis_index("x"), lax.axis_index("y"), lax.axis_index("c")
        me, sibling = (x, y, c), (x, y, 1 - c)
        chips = [(1 - x, y), (x, 1 - y), (1 - x, 1 - y)]

        def copy(a, k, block, to, src=None):
            dst = outs[a].at[_slot(*block)]
            return pltpu.make_async_remote_copy(
                src_ref=dst if src is None else src, dst_ref=dst,
                send_sem=send_sems.at[a, k], recv_sem=recv_sems.at[a, k], device_id=to, device_id_type=MESH)

        first = []
        for a in range(n):
            first.append(copy(a, 0, me, sibling, src=ins[a]))
            first += [copy(a, 1 + j, me, (*chip, c), src=ins[a]) for j, chip in enumerate(chips)]
        for cp in first:
            cp.start()
        passed = []
        for j, chip in enumerate(chips):
            for a in range(n):
                copy(a, 1 + j, (*chip, c), me).wait_recv()
                fwd = copy(a, 4 + j, (*chip, c), sibling)
                fwd.start()
                passed.append(fwd)
        for a in range(n):
            copy(a, 0, sibling, me).wait_recv()
        for j, chip in enumerate(chips):
            for a in range(n):
                copy(a, 4 + j, (*chip, 1 - c), me).wait_recv()
        for cp in first + passed:
            cp.wait_send()

    any_spec = pl.BlockSpec(memory_space=pl.ANY)
    return pl.pallas_call(
        body,
        in_specs=[any_spec] * (2 * n), out_specs=[any_spec] * n,
        out_shape=[jax.ShapeDtypeStruct((NDEV,) + a.shape, a.dtype) for a in arrs],
        input_output_aliases={n + i: i for i in range(n)},
        scratch_shapes=[pltpu.SemaphoreType.DMA((n, 7)), pltpu.SemaphoreType.DMA((n, 7))],
        compiler_params=pltpu.CompilerParams(has_side_effects=True), name=name,
    )(*arrs, *[_with_own_block(a, me) for a in arrs])


HBM_SPEC = pl.BlockSpec(memory_space=pltpu.HBM)
SEM_SPEC = pl.BlockSpec(memory_space=pltpu.SEMAPHORE)
ANY_SPEC = pl.BlockSpec(memory_space=pl.ANY)
TOKEN_SHAPE = jax.ShapeDtypeStruct((8, 128), f32)
DATAFLOW_EFFECT = pltpu.SideEffectType.DATAFLOW_SIDE_EFFECTING


def _in_hbm(a):
    return pltpu.with_memory_space_constraint(a, pltpu.HBM)


def _hbm_like(a):
    return pltpu.HBM(a.shape, a.dtype)


def _mesh_pos():
    return lax.axis_index("x"), lax.axis_index("y"), lax.axis_index("c")


def _gather_targets(x, y, c):
    return [(x, y, 1 - c), (1 - x, y, c), (x, 1 - y, c), (1 - x, 1 - y, c)]


def gather_start(shards, me, after, *, name):
    n = len(shards)
    extra = [] if after is None else [after]

    def body(*refs):
        srcs, lands = refs[:n], refs[n:2 * n]
        send_sems, recv_sems = refs[2 * n + len(extra)], refs[2 * n + len(extra) + 1]
        token = refs[-1]
        x, y, c = _mesh_pos()
        me = _slot(x, y, c)
        for a in range(n):
            for k, to in enumerate(_gather_targets(x, y, c)):
                pltpu.make_async_remote_copy(
                    src_ref=srcs[a], dst_ref=lands[a].at[me], send_sem=send_sems.at[4 * a + k], recv_sem=recv_sems.at[4 * a + k],
                    device_id=to, device_id_type=MESH).start()
        token[...] = jnp.zeros(token.shape, f32)

    lands = [_with_own_block(s, me) for s in shards]
    sems = pltpu.SemaphoreType.DMA((4 * n,))
    out = pl.pallas_call(
        body, name=name,
        in_specs=[HBM_SPEC] * (2 * n) + [ANY_SPEC] * len(extra),
        out_specs=[SEM_SPEC, SEM_SPEC] + [HBM_SPEC] * (2 * n) + [pl.BlockSpec(memory_space=pltpu.VMEM)],
        out_shape=[sems, sems] + [_hbm_like(s) for s in shards] + [_hbm_like(l) for l in lands] + [TOKEN_SHAPE],
        input_output_aliases={i: 2 + i for i in range(2 * n)},
        compiler_params=pltpu.CompilerParams(has_side_effects=DATAFLOW_EFFECT),
    )(*[_in_hbm(s) for s in shards], *[_in_hbm(l) for l in lands], *extra)
    return out[0], out[1], out[2:2 + n], out[2 + n:2 + 2 * n], out[-1]


def gather_wait(send_sems, recv_sems, shards, lands, after, *, name):
    n = len(shards)
    after = list(after) if isinstance(after, (list, tuple)) else [after]

    def body(*refs):
        srcs, lands_in = refs[:n], refs[n:2 * n]
        send_sems, recv_sems = refs[2 * n], refs[2 * n + 1]
        x, y, c = _mesh_pos()
        for a in range(n):
            for k, frm in enumerate(_gather_targets(x, y, c)):
                cp = pltpu.make_async_remote_copy(
                    src_ref=srcs[a], dst_ref=lands_in[a].at[_slot(*frm)], send_sem=send_sems.at[4 * a + k],
                    recv_sem=recv_sems.at[4 * a + k], device_id=frm, device_id_type=MESH)
                cp.wait_send()
                cp.wait_recv()

    out = pl.pallas_call(
        body, name=name,
        in_specs=[HBM_SPEC] * (2 * n) + [SEM_SPEC, SEM_SPEC] + [ANY_SPEC] * len(after),
        out_specs=[HBM_SPEC] * (2 * n),
        out_shape=[_hbm_like(s) for s in shards] + [_hbm_like(l) for l in lands],
        input_output_aliases={i: i for i in range(2 * n)},
        compiler_params=pltpu.CompilerParams(has_side_effects=DATAFLOW_EFFECT),
    )(*shards, *lands, send_sems, recv_sems, *after)
    return out[:n], out[n:]


def gather_finish(lands, *, name):
    n = len(lands)

    def body(*refs):
        lands_in, lands_out = refs[:n], refs[n:2 * n]
        send_sems, recv_sems = refs[2 * n:]
        x, y, c = _mesh_pos()
        sibling = (x, y, 1 - c)
        chips = [(1 - x, y), (x, 1 - y), (1 - x, 1 - y)]
        sent = []
        for a in range(n):
            for j, chip in enumerate(chips):
                s = _slot(*chip, c)
                cp = pltpu.make_async_remote_copy(
                    src_ref=lands_in[a].at[s], dst_ref=lands_out[a].at[s], send_sem=send_sems.at[a, j],
                    recv_sem=recv_sems.at[a, j], device_id=sibling, device_id_type=MESH)
                cp.start()
                sent.append(cp)
        for a in range(n):
            for j, chip in enumerate(chips):
                s = _slot(*chip, 1 - c)
                pltpu.make_async_remote_copy(
                    src_ref=lands_in[a].at[s], dst_ref=lands_out[a].at[s], send_sem=send_sems.at[a, j],
                    recv_sem=recv_sems.at[a, j], device_id=sibling, device_id_type=MESH).wait_recv()
        for cp in sent:
            cp.wait_send()

    return pl.pallas_call(
        body, name=name,
        in_specs=[ANY_SPEC] * n, out_specs=[ANY_SPEC] * n,
        out_shape=[jax.ShapeDtypeStruct(l.shape, l.dtype) for l in lands],
        input_output_aliases={i: i for i in range(n)},
        scratch_shapes=[pltpu.SemaphoreType.DMA((n, 3)), pltpu.SemaphoreType.DMA((n, 3))],
        compiler_params=pltpu.CompilerParams(has_side_effects=True),
    )(*lands)


def _peer(x, y, c, r):
    return (x ^ ((r >> 2) & 1), y ^ ((r >> 1) & 1), c ^ (r & 1))


def exchange_start(items, lands, after, *, name):
    n, m = len(items), len(lands)
    extra = [] if after is None else [after]

    def body(*refs):
        gs, zones = refs[:n], refs[n:n + m]
        send_sems, recv_sems = refs[n + m + len(extra)], refs[n + m + len(extra) + 1]
        token = refs[-1]
        x, y, c = _mesh_pos()
        me = _slot(x, y, c)
        for r in (1, 2, 4, 3, 5, 6, 7):
            to = _peer(x, y, c, r)
            for a, (_, zi, l) in enumerate(items):
                pltpu.make_async_remote_copy(
                    src_ref=gs[a].at[_slot(*to)], dst_ref=zones[zi].at[me, l], send_sem=send_sems.at[7 * a + r - 1],
                    recv_sem=recv_sems.at[7 * a + r - 1], device_id=to, device_id_type=MESH).start()
        token[...] = jnp.zeros(token.shape, f32)

    gs = [g for g, _, _ in items]
    sems = pltpu.SemaphoreType.DMA((7 * n,))
    out = pl.pallas_call(
        body, name=name,
        in_specs=[HBM_SPEC] * (n + m) + [ANY_SPEC] * len(extra),
        out_specs=[SEM_SPEC, SEM_SPEC] + [HBM_SPEC] * (n + m) + [pl.BlockSpec(memory_space=pltpu.VMEM)],
        out_shape=[sems, sems] + [_hbm_like(g) for g in gs] + [_hbm_like(z) for z in lands] + [TOKEN_SHAPE],
        input_output_aliases={i: 2 + i for i in range(n + m)},
        compiler_params=pltpu.CompilerParams(has_side_effects=DATAFLOW_EFFECT),
    )(*[_in_hbm(g) for g in gs], *[_in_hbm(z) for z in lands], *extra)
    return out[0], out[1], out[2:2 + n], out[2 + n:2 + n + m], out[-1]


def exchange_wait(parts, lands, after, *, name):
    m = len(lands)
    flat_gs = [g for _, _, _, gs in parts for g in gs]
    ng = len(flat_gs)
    after = list(after) if isinstance(after, (list, tuple)) else [after]

    def body(*refs):
        gs, zones = refs[:ng], refs[ng:ng + m]
        sem_refs = refs[ng + m:ng + m + 2 * len(parts)]
        x, y, c = _mesh_pos()
        base = 0
        for pi, (items, _, _, _) in enumerate(parts):
            send_sems, recv_sems = sem_refs[2 * pi], sem_refs[2 * pi + 1]
            for r in range(1, NDEV):
                frm = _peer(x, y, c, r)
                for a, (_, zi, l) in enumerate(items):
                    cp = pltpu.make_async_remote_copy(
                        src_ref=gs[base + a].at[_slot(*frm)], dst_ref=zones[zi].at[_slot(*frm), l],
                        send_sem=send_sems.at[7 * a + r - 1], recv_sem=recv_sems.at[7 * a + r - 1],
                        device_id=frm, device_id_type=MESH)
                    cp.wait_send()
                    cp.wait_recv()
            base += len(items)

    sem_args = [s for _, ss, rs, _ in parts for s in (ss, rs)]
    out = pl.pallas_call(
        body, name=name,
        in_specs=[HBM_SPEC] * (ng + m) + [SEM_SPEC] * len(sem_args) + [ANY_SPEC] * len(after),
        out_specs=[HBM_SPEC] * (ng + m),
        out_shape=[_hbm_like(g) for g in flat_gs] + [_hbm_like(z) for z in lands],
        input_output_aliases={i: i for i in range(ng + m)},
        compiler_params=pltpu.CompilerParams(has_side_effects=DATAFLOW_EFFECT),
    )(*flat_gs, *lands, *sem_args, *after)
    return out[ng:]


def sum8(g, *, name):
    _, R, C = g.shape
    tr = R

    def body(g_ref, o_ref):
        s = g_ref[0]
        for k in range(1, NDEV):
            s = s + g_ref[k]
        o_ref[...] = s

    return pl.pallas_call(
        body, grid=(R // tr,),
        in_specs=[pl.BlockSpec((NDEV, tr, C), lambda i: (0, i, 0))],
        out_specs=pl.BlockSpec((tr, C), lambda i: (i, 0)),
        out_shape=jax.ShapeDtypeStruct((R, C), f32),
        compiler_params=_cparams(("parallel",)), name=name,
    )(g)


def _adam_math(w, g, m, v):
    m = ADAM_B1 * m + (1.0 - ADAM_B1) * g
    v = ADAM_B2 * v + (1.0 - ADAM_B2) * (g * g)
    m_hat = m / (1.0 - ADAM_B1 ** ADAM_STEP)
    v_hat = v / (1.0 - ADAM_B2 ** ADAM_STEP)
    delta = -ADAM_LR * (m_hat / (jnp.sqrt(v_hat) + ADAM_EPS) + ADAM_WD * w)
    return delta, m, v


def adamw_slices(w, m, v, recv, *, row0=0, col0=0, rrows=None, tr=None, tc=None, l0=0, nl=None, prev=None, name):
    L, R, C = w.shape
    Rp, Cp = recv.shape[2], recv.shape[3]
    nl = L - l0 if nl is None else nl
    tr = R if tr is None else tr
    tc = C if tc is None else tc
    rrows = Rp if rrows is None else rrows
    assert R % tr == 0 and C % tc == 0 and row0 % rrows == 0 and R <= rrows
    assert (tr == R or (Rp == R and row0 == 0)) and (tc == C or (Cp == C and col0 == 0))
    rr = rrows if tr == R else tr
    rc = Cp if tc == C else tc
    rb = row0 // rrows if tr == R else 0
    prevs = [] if prev is None else list(prev)

    def body(w_ref, m_ref, v_ref, r_ref, *rest):
        g_ref, d_ref, nm_ref, nv_ref = rest[len(prevs):]
        c0 = col0 if tc == C else 0
        g = r_ref[0, 0:tr, c0:c0 + tc].astype(f32)
        for k in range(1, NDEV):
            g = g + r_ref[k, 0:tr, c0:c0 + tc].astype(f32)
        delta, nm, nv = _adam_math(w_ref[...], g, m_ref[...], v_ref[...])
        g_ref[...] = g
        d_ref[...] = delta
        nm_ref[...] = nm
        nv_ref[...] = nv

    wspec = pl.BlockSpec((None, tr, tc), lambda l, i, j: (l0 + l, i, j))
    return pl.pallas_call(
        body, grid=(nl, R // tr, C // tc),
        in_specs=[wspec, wspec, wspec, pl.BlockSpec((NDEV, None, rr, rc), lambda l, i, j: (0, l0 + l, rb + i, j))]
        + [pl.BlockSpec(memory_space=pl.ANY)] * len(prevs),
        out_specs=[wspec] * 4, out_shape=[jax.ShapeDtypeStruct((L, R, C), f32)] * 4,
        input_output_aliases={4 + q: q for q in range(len(prevs))},
        compiler_params=_cparams(("parallel", "parallel", "parallel")), name=name,
    )(w, m, v, recv, *prevs)


def adamw_slices_t(w, m, v, recv, *, name):
    L, R, C = w.shape

    def body(w_ref, m_ref, v_ref, r_ref, g_ref, d_ref, nm_ref, nv_ref):
        gt = r_ref[0].astype(f32)
        for k in range(1, NDEV):
            gt = gt + r_ref[k].astype(f32)
        g = gt.T
        delta, nm, nv = _adam_math(w_ref[...], g, m_ref[...], v_ref[...])
        g_ref[...] = g
        d_ref[...] = delta
        nm_ref[...] = nm
        nv_ref[...] = nv

    wspec = pl.BlockSpec((None, R, C), lambda l: (l, 0, 0))
    return pl.pallas_call(
        body, grid=(L,),
        in_specs=[wspec, wspec, wspec, pl.BlockSpec((NDEV, None, C, R), lambda l: (0, l, 0, 0))],
        out_specs=[wspec] * 4, out_shape=[jax.ShapeDtypeStruct((L, R, C), f32)] * 4,
        compiler_params=_cparams(("parallel",)), name=name,
    )(w, m, v, recv)


def adamw_plain(w, m, v, g, *, name):
    R, C = w.shape

    def body(w_ref, m_ref, v_ref, g_ref, d_ref, nm_ref, nv_ref):
        delta, nm, nv = _adam_math(w_ref[...], g_ref[...], m_ref[...], v_ref[...])
        d_ref[...] = delta
        nm_ref[...] = nm
        nv_ref[...] = nv

    return pl.pallas_call(
        body, out_shape=[jax.ShapeDtypeStruct((R, C), f32)] * 3, name=name,
    )(w, m, v, g)


def _chunk_causal_mask():
    pos = jnp.arange(GBLK)
    return (pos[None, :] // CHUNK) <= (pos[:, None] // CHUNK)


def kernel(x, norm_mix_g, norm_ffn_g, final_norm_g, a_w_in, a_v_norm_g, a_w_s, a_b_s, a_w_out, b_w_in, b_conv_w, b_w_out, c_w_in, c_w_grp, c_scale, c_w_out, f_w_gate, f_w_up, f_w_down, loss_target, m_norm_mix_g, m_norm_ffn_g, m_final_norm_g, m_a_w_in, m_a_v_norm_g, m_a_w_s, m_a_b_s, m_a_w_out, m_b_w_in, m_b_conv_w, m_b_w_out, m_c_w_in, m_c_w_grp, m_c_scale, m_c_w_out, m_f_w_gate, m_f_w_up, m_f_w_down, v_norm_mix_g, v_norm_ffn_g, v_final_norm_g, v_a_w_in, v_a_v_norm_g, v_a_w_s, v_a_b_s, v_a_w_out, v_b_w_in, v_b_conv_w, v_b_w_out, v_c_w_in, v_c_w_grp, v_c_scale, v_c_w_out, v_f_w_gate, v_f_w_up, v_f_w_down):
    B, S, _ = x.shape
    T = B * S
    tm = min(512, S)
    tmm = min(1024, T)
    tms = min(512, T)
    tma = min(256, S)
    me = _slot(lax.axis_index("x"), lax.axis_index("y"), lax.axis_index("c"))

    def tb(w):
        return jnp.transpose(w, (0, 2, 1)).astype(bf16)

    small = jnp.concatenate([a_v_norm_g, b_conv_w[0], c_scale, jnp.zeros((2, GBLK), f32)], axis=0)
    started = [gather_start([tb(a_w_in[0:1])[0], a_w_out[0].astype(bf16), small], me, None, name="gather_start_0")]
    tok = started[0][4]
    z = tok[0, 0]
    f_g, f_u, f_d = tb(f_w_gate + z), tb(f_w_up + z), (f_w_down + z).astype(bf16)
    stage_shards = [
        None,
        [f_g[0], f_u[0], f_d[0]],
        [tb(b_w_in + z)[0], (b_w_out[0] + z).astype(bf16), f_g[1], f_u[1], f_d[1]],
        [(c_w_in[0] + z).astype(bf16), (c_w_grp[0] + z).astype(bf16), (c_w_out[0] + z).astype(bf16), f_g[2], f_u[2], f_d[2]],
        [tb(a_w_in[1:2] + z)[0], (a_w_out[1] + z).astype(bf16), f_g[3], f_u[3], f_d[3]],
    ]
    for s in range(1, len(stage_shards)):
        started.append(gather_start(stage_shards[s], me, tok, name=f"gather_start_{s}"))
        tok = started[-1][4]

    def stage_weights(s, after):
        send_sems, recv_sems, shards, lands, _ = started[s]
        _, lands = gather_wait(send_sems, recv_sems, shards, lands, after, name=f"gather_wait_{s}")
        lands = gather_finish(lands, name=f"gather_finish_{s}")
        return [t.reshape(-1, t.shape[-1]) if t.ndim == 3 else t for t in lands]

    wm_masked = jnp.where(_chunk_causal_mask()[None, None], a_w_s, 0.0).astype(bf16)
    bsb = jnp.broadcast_to(a_b_s[:, :, :, None], a_b_s.shape + (GBLK,))

    xc = x.reshape(T, D)
    saved, weights = [], []
    for i in range(DEPTH):
        kind, j = i % 3, i // 3
        gmix = norm_mix_g[i][None]
        if i == 0:
            w_in, w_out, small_g = stage_weights(0, tok)
            small_full = jnp.transpose(small_g.reshape(NDEV, 8, GBLK), (1, 0, 2)).reshape(8, D)
            gv_full, cw_full, cs_full = small_full[0:2], small_full[2:5], small_full[5:6]
            w = [w_in, w_out]
        else:
            w = stage_weights(i + 1, xc)
        if kind == 0:
            w_in, w_out = w[:2]
            p, h = norm_mm(xc, gmix, w_in, tm=tms, transposed=True, name=f"a_in_{i}")
            y = a_mid_fwd(p, gv_full[j][None], wm_masked[j], bsb[j], tm=tma, name=f"a_mid_{i}")
        elif kind == 1:
            w_in, w_out = w[:2]
            p, h = norm_mm(xc, gmix, w_in, tm=tms, transposed=True, name=f"b_in_{i}")
            y = b_mid_fwd(p, cw_full, tm=tm, S=S, name=f"b_mid_{i}")
        else:
            w_in, w_grp, w_out = w[:3]
            p, h = norm_mm(xc, gmix, w_in, tm=tms, transposed=False, name=f"c_in_{i}")
            y = c_mid_fwd(p, w_grp, cs_full, tm=tm, S=S, name=f"c_mid_{i}")
        if i == 0:
            w = w + stage_weights(1, y)
        weights.append(w)
        w_fg, w_fu, w_fd = w[-3:]
        x1, pg, pu, a, h2 = ffn_in(y, w_out, xc, norm_ffn_g[i][None], w_fg, w_fu, tm=tms, name=f"f_in_{i}")
        x2 = mm_res(a, w_fd, x1, tm=tms, name=f"f_out_{i}")
        saved.append((xc, p, h, y, x1, pg, pu, h2, a))
        xc = x2

    loss_blk, dx, d_final = final_loss(xc, final_norm_g[None], loss_target.reshape(T, D), tm=tms, name="final_loss")
    loss = lax.psum(loss_blk[0, 0], AXES)

    zone_shapes = [(2, 256, D), (2, GBLK, D), (1, 384, D), (1, GBLK, D), (1, GBLK, D), (1, GBLK, C_GDIM), (1, GBLK, D),
                   (DEPTH, F_LOC, D), (DEPTH, F_LOC, D), (DEPTH, F_LOC, D)]
    Z_AIN, Z_AOUT, Z_BIN, Z_BOUT, Z_CIN, Z_CGRP, Z_COUT, Z_FG, Z_FU, Z_FD = range(10)
    lands = [lax.empty((NDEV,) + s, bf16) for s in zone_shapes]
    parts = []

    def exchange(items, after, name):
        nonlocal lands
        for g, zi, l in items:
            own = lax.dynamic_index_in_dim(g, me, axis=0, keepdims=True)[None]
            lands[zi] = lax.dynamic_update_slice(lands[zi], own, (me, l) + (0,) * (g.ndim - 1))
        idx = sorted({zi for _, zi, _ in items})
        local = [(g, idx.index(zi), l) for g, zi, l in items]
        send_sems, recv_sems, gs, zs, token = exchange_start(local, [lands[zi] for zi in idx], after, name=name)
        for q, zi in enumerate(idx):
            lands[zi] = zs[q]
        parts.append(([(None, zi, l) for _, zi, l in items], send_sems, recv_sems, gs))
        return token

    d_mix, d_ffn = [None] * DEPTH, [None] * DEPTH
    d_gv, d_wm, d_bs = [None] * 2, [None] * 2, [None] * 2
    tok = None
    for i in reversed(range(DEPTH)):
        kind, j = i % 3, i // 3
        xin, p, h, y, x1, pg, pu, h2, a = saved[i]
        w = weights[i]
        w_in, w_out = w[0], w[-4]
        w_fg, w_fu, w_fd = w[-3:]
        dpg, dpu, dxb = ffn_bwd_mid(dx, w_fd, pg, pu, tm=tms, name=f"f_bwd_mid_{i}")
        gw_g, gw_u, gw_d = ffn_dw(dpg, dpu, a, h2, dxb, name=f"f_dw_{i}")
        items = [(gw_g, Z_FG, i), (gw_u, Z_FU, i), (gw_d, Z_FD, i)]
        if i == 0:
            tok = exchange(items, tok, "exchange_start_ffn0")
            items = []
        dx1, d_ffn[i], dy, dx1b = mm_normbwd([dpg, dpu], [w_fg, w_fu], x1, norm_ffn_g[i][None], dx, tok, tm=tms,
                                             transposed=True, w_next=w_out, name=f"f_bwd_in_{i}")
        if kind == 0:
            dp, d_gv[j], d_wm[j], d_bs[j] = a_mid_bwd(p, dy, gv_full[j][None], wm_masked[j], bsb[j], tm=tma, name=f"a_bwd_mid_{i}")
            items += [(mm_tn(dp, h, name=f"a_dw_in_{i}"), Z_AIN, j),
                      (mm_tn(y, dx1b, name=f"a_dw_out_{i}"), Z_AOUT, j)]
        elif kind == 1:
            dp, d_cw = b_mid_bwd(p, dy, cw_full, tm=tm, S=S, name=f"b_bwd_mid_{i}")
            items += [(mm_tn(dp, h, name=f"b_dw_in_{i}"), Z_BIN, 0),
                      (mm_tn(y, dx1b, name=f"b_dw_out_{i}"), Z_BOUT, 0)]
        else:
            dp, d_cs, d_wgrp = c_mid_bwd(p, dy, w[1], cs_full, tm=tm, S=S, name=f"c_bwd_mid_{i}")
            gw_cgrp = jnp.transpose(d_wgrp.reshape(C_GROUPS, NDEV, C_GDIM // NDEV, C_GDIM), (1, 0, 2, 3)).astype(bf16)
            items += [(mm_tn(h, dp, name=f"c_dw_in_{i}"), Z_CIN, 0),
                      (gw_cgrp.reshape(NDEV, GBLK, C_GDIM), Z_CGRP, 0),
                      (mm_tn(y, dx1b, name=f"c_dw_out_{i}"), Z_COUT, 0)]
        tok = exchange(items, tok, f"exchange_start_{i}")
        dx, d_mix[i] = mm_normbwd([dp], [w_in], xin, norm_mix_g[i][None], dx1, tok, tm=tms, transposed=(kind != 2),
                                  name=f"mix_bwd_in_{i}")
    grad_x = dx.reshape(B, S, D)

    mask = _chunk_causal_mask()
    small_parts = [
        jnp.concatenate(d_mix, axis=0).reshape(-1, GBLK),
        jnp.concatenate(d_ffn, axis=0).reshape(-1, GBLK),
        d_final.reshape(-1, GBLK),
        jnp.concatenate(d_gv, axis=0).reshape(-1, GBLK),
        jnp.where(mask[None, None], jnp.stack(d_wm), 0.0).reshape(-1, GBLK),
        jnp.stack(d_bs).reshape(-1, GBLK),
        d_cw.reshape(-1, GBLK),
        d_cs.reshape(-1, GBLK),
    ]
    small_rows = [q.shape[0] for q in small_parts]
    sg_send, sg_recv, sg_shards, sg_lands, _ = gather_start(
        [jnp.concatenate(small_parts, axis=0)], me, tok, name="gather_small_start")

    def plain(w, m, v, g, name):
        shp = w.shape
        w2, m2, v2, g2 = (t.reshape(-1, shp[-1]) for t in (w, m, v, g))
        return tuple(t.reshape(shp) for t in adamw_plain(w2, m2, v2, g2, name=name))

    def sliced(w, m, v, recv, name, **kw):
        shp = w.shape
        w3, m3, v3 = (t.reshape((shp[0], -1, shp[-1])) for t in (w, m, v))
        return tuple(t.reshape(shp) for t in adamw_slices(w3, m3, v3, recv, name=name, **kw))

    def sliced_t(w, m, v, recv, name, **kw):
        wt, mt, vt = (jnp.transpose(t, (0, 2, 1)) for t in (w, m, v))
        return adamw_slices(wt, mt, vt, recv, name=name, **kw)

    def untransposed(outs):
        return tuple(jnp.transpose(t, (0, 2, 1)) for t in outs)

    lands = list(exchange_wait(parts[:3], lands, dx, name="exchange_wait"))
    _, _, r_bin, r_bout, r_cin, r_cgrp, r_cout, r_fg, r_fu, r_fd = lands
    res = {}
    res["b_w_in"] = adamw_slices_t(b_w_in, m_b_w_in, v_b_w_in, r_bin, name="adam_b_w_in")
    res["b_w_out"] = sliced(b_w_out, m_b_w_out, v_b_w_out, r_bout, "adam_b_w_out")
    res["c_w_in"] = sliced(c_w_in, m_c_w_in, v_c_w_in, r_cin, "adam_c_w_in")
    res["c_w_grp"] = sliced(c_w_grp, m_c_w_grp, v_c_w_grp, r_cgrp, "adam_c_w_grp")
    res["c_w_out"] = sliced(c_w_out, m_c_w_out, v_c_w_out, r_cout, "adam_c_w_out")
    fg = sliced_t(f_w_gate, m_f_w_gate, v_f_w_gate, r_fg, "adam_f_gate_123", tc=256, l0=1)
    fu = sliced_t(f_w_up, m_f_w_up, v_f_w_up, r_fu, "adam_f_up_123", tc=256, l0=1)
    fd = adamw_slices(f_w_down, m_f_w_down, v_f_w_down, r_fd, tc=256, l0=1, name="adam_f_down_123")
    big = [res[n][1] for n in ("b_w_in", "b_w_out", "c_w_in", "c_w_grp", "c_w_out")] + [fg[1], fu[1], fd[1]]

    ffn0_items, ffn0_send, ffn0_recv, ffn0_gs = parts[3]
    ffn0_zones = [Z_FG, Z_FU, Z_FD]
    ffn0_part = ([(None, ffn0_zones.index(zi), l) for _, zi, l in ffn0_items], ffn0_send, ffn0_recv, ffn0_gs)
    r_fg, r_fu, r_fd = exchange_wait([ffn0_part], [lands[zi] for zi in ffn0_zones], big, name="exchange_wait_ffn0")
    fg = sliced_t(f_w_gate, m_f_w_gate, v_f_w_gate, r_fg, "adam_f_gate_0", tc=256, l0=0, nl=1, prev=fg)
    fu = sliced_t(f_w_up, m_f_w_up, v_f_w_up, r_fu, "adam_f_up_0", tc=256, l0=0, nl=1, prev=fu)
    fd = adamw_slices(f_w_down, m_f_w_down, v_f_w_down, r_fd, tc=256, l0=0, nl=1, prev=fd, name="adam_f_down_0")
    res["f_w_gate"], res["f_w_up"], res["f_w_down"] = untransposed(fg), untransposed(fu), tuple(fd)

    last_items, last_send, last_recv, last_gs = parts[4]
    last_zones = [Z_AIN, Z_AOUT]
    last_part = ([(None, last_zones.index(zi), l) for _, zi, l in last_items], last_send, last_recv, last_gs)
    r_ain, r_aout = exchange_wait([last_part], [lands[zi] for zi in last_zones], [fg[1], fu[1], fd[1]], name="exchange_wait_last")
    res["a_w_in"] = adamw_slices_t(a_w_in, m_a_w_in, v_a_w_in, r_ain, name="adam_a_w_in")
    res["a_w_out"] = sliced(a_w_out, m_a_w_out, v_a_w_out, r_aout, "adam_a_w_out")

    _, sg_lands = gather_wait(sg_send, sg_recv, sg_shards, sg_lands, [res["a_w_in"][1], res["a_w_out"][1]], name="gather_small_wait")
    (gs_all,) = gather_finish(sg_lands, name="gather_small_finish")
    gs = sum8(gs_all, name="sum_small_grads")
    offs = [0]
    for r in small_rows:
        offs.append(offs[-1] + r)
    sp = [gs[offs[q]:offs[q + 1]] for q in range(len(small_parts))]
    grad_norm_mix_g = sp[0].reshape(DEPTH, D)
    grad_norm_ffn_g = sp[1].reshape(DEPTH, D)
    grad_final_norm_g = sp[2].reshape(D)
    grad_a_w_s = sp[4].reshape(a_w_s.shape)
    grad_a_b_s = sp[5].reshape(a_b_s.shape)

    def my_cols(full):
        return lax.dynamic_slice_in_dim(full, me * GBLK, GBLK, axis=1)

    grad_a_v_norm_g = my_cols(sp[3].reshape(2, D))
    grad_b_conv_w = my_cols(sp[6].reshape(3, D))[None]
    grad_c_scale = my_cols(sp[7].reshape(1, D))
    res["norm_mix_g"] = (grad_norm_mix_g,) + plain(norm_mix_g, m_norm_mix_g, v_norm_mix_g, grad_norm_mix_g, "adam_norm_mix")
    res["norm_ffn_g"] = (grad_norm_ffn_g,) + plain(norm_ffn_g, m_norm_ffn_g, v_norm_ffn_g, grad_norm_ffn_g, "adam_norm_ffn")
    res["final_norm_g"] = (grad_final_norm_g,) + tuple(
        t.reshape(D) for t in plain(final_norm_g[None], m_final_norm_g[None], v_final_norm_g[None], grad_final_norm_g[None], "adam_final"))
    res["a_v_norm_g"] = (grad_a_v_norm_g,) + plain(a_v_norm_g, m_a_v_norm_g, v_a_v_norm_g, grad_a_v_norm_g, "adam_a_v_norm")
    res["a_w_s"] = (grad_a_w_s,) + plain(a_w_s, m_a_w_s, v_a_w_s, grad_a_w_s, "adam_a_w_s")
    res["a_b_s"] = (grad_a_b_s,) + plain(a_b_s, m_a_b_s, v_a_b_s, grad_a_b_s, "adam_a_b_s")
    res["b_conv_w"] = (grad_b_conv_w,) + plain(b_conv_w, m_b_conv_w, v_b_conv_w, grad_b_conv_w, "adam_b_conv")
    res["c_scale"] = (grad_c_scale,) + plain(c_scale, m_c_scale, v_c_scale, grad_c_scale, "adam_c_scale")

    order = ["norm_mix_g", "norm_ffn_g", "final_norm_g", "a_w_in", "a_v_norm_g", "a_w_s", "a_b_s", "a_w_out", "b_w_in",
             "b_conv_w", "b_w_out", "c_w_in", "c_w_grp", "c_scale", "c_w_out", "f_w_gate", "f_w_up", "f_w_down"]
    return (loss, grad_x, *[res[n][0] for n in order], *[res[n][1] for n in order],
            *[res[n][2] for n in order], *[res[n][3] for n in order])
```

```python
import functools

import jax
import jax.numpy as jnp
from jax import lax
from jax.experimental import pallas as pl
from jax.experimental.pallas import tpu as pltpu

f32 = jnp.float32
bf16 = jnp.bfloat16

NDEV = 8
D = 1024
EPS = 1e-6
CHUNK = 64
GBLK = 128
A_GROUPS = 8
C_GROUPS = 4
C_GDIM = 256
POOL_WINDOWS = (2, 4, 8, 16)
HALO = 16
COLS = 256
F_LOC = 352
MXU_COLS = 256
DEPTH = 4
AXES = ("x", "y", "c")
MESH = pl.DeviceIdType.MESH

ADAM_LR = 0.001
ADAM_B1 = 0.9
ADAM_B2 = 0.999
ADAM_EPS = 1e-08
ADAM_WD = 0.01
ADAM_STEP = 10

VMEM_LIMIT = 56 * 1024 * 1024


def _cparams(sem):
    return pltpu.CompilerParams(dimension_semantics=sem, vmem_limit_bytes=VMEM_LIMIT)


def _gelu(z):
    return 0.5 * z * (1.0 + lax.erf(z * 0.7071067811865476))


def _gelu_cdf(z):
    return 0.5 * (1.0 + lax.erf(z * 0.7071067811865476))


def _gelu_grad(z, cdf):
    return cdf + z * jnp.exp(-0.5 * z * z) * 0.3989422804014327


def _dot(a, b):
    return jnp.dot(a, b, preferred_element_type=f32)


def _dot_nt(a, b):
    return lax.dot_general(a, b, (((1,), (1,)), ((), ())), preferred_element_type=f32)


def _dot_tn(a, b):
    return lax.dot_general(a, b, (((0,), (0,)), ((), ())), preferred_element_type=f32)


def _col_chunks(n, width=1024):
    return [(c, min(c + width, n)) for c in range(0, n, width)]


def norm_mm(x, g, w, *, tm, transposed, name):
    T = x.shape[0]
    n = w.shape[0] if transposed else w.shape[1]

    def body(x_ref, g_ref, w_ref, p_ref, h_ref):
        xv = x_ref[...]
        r = lax.rsqrt(jnp.mean(xv * xv, axis=-1, keepdims=True) + EPS)
        h = (xv * r * g_ref[...]).astype(bf16)
        h_ref[...] = h
        p = _dot_nt(h, w_ref[...]) if transposed else _dot(h, w_ref[...])
        p_ref[...] = p.astype(bf16)

    return pl.pallas_call(
        body, grid=(T // tm,),
        in_specs=[pl.BlockSpec((tm, D), lambda i: (i, 0)), pl.BlockSpec((1, D), lambda i: (0, 0)),
                  pl.BlockSpec(w.shape, lambda i: (0, 0), pipeline_mode=pl.Buffered(1))],
        out_specs=[pl.BlockSpec((tm, n), lambda i: (i, 0)), pl.BlockSpec((tm, D), lambda i: (i, 0))],
        out_shape=[jax.ShapeDtypeStruct((T, n), bf16), jax.ShapeDtypeStruct((T, D), bf16)],
        compiler_params=_cparams(("parallel",)), name=name,
    )(x, g, w)


def ffn_in(y, w_out, xin, g, wtg, wtu, *, tm, name):
    T, K = y.shape
    H = wtg.shape[0]

    def body(y_ref, wo_ref, xin_ref, g_ref, wg_ref, wu_ref, x_ref, pg_ref, pu_ref, a_ref, h_ref):
        xv = xin_ref[...] + _dot(y_ref[...], wo_ref[...])
        x_ref[...] = xv
        r = lax.rsqrt(jnp.mean(xv * xv, axis=-1, keepdims=True) + EPS)
        h = (xv * r * g_ref[...]).astype(bf16)
        h_ref[...] = h
        for c0, c1 in _col_chunks(H):
            gate = _dot_nt(h, wg_ref[c0:c1, :])
            up = _dot_nt(h, wu_ref[c0:c1, :])
            pg_ref[:, c0:c1] = gate.astype(bf16)
            pu_ref[:, c0:c1] = up.astype(bf16)
            a_ref[:, c0:c1] = (gate * jax.nn.sigmoid(gate) * up).astype(bf16)

    wspec = pl.BlockSpec((H, D), lambda i: (0, 0), pipeline_mode=pl.Buffered(1))
    hspec = pl.BlockSpec((tm, H), lambda i: (i, 0))
    dspec = pl.BlockSpec((tm, D), lambda i: (i, 0))
    return pl.pallas_call(
        body, grid=(T // tm,),
        in_specs=[pl.BlockSpec((tm, K), lambda i: (i, 0)),
                  pl.BlockSpec((K, D), lambda i: (0, 0), pipeline_mode=pl.Buffered(1)),
                  dspec, pl.BlockSpec((1, D), lambda i: (0, 0)), wspec, wspec],
        out_specs=[dspec, hspec, hspec, hspec, dspec],
        out_shape=[jax.ShapeDtypeStruct((T, D), f32)] + [jax.ShapeDtypeStruct((T, H), bf16)] * 3
        + [jax.ShapeDtypeStruct((T, D), bf16)],
        compiler_params=_cparams(("parallel",)), name=name,
    )(y, w_out, xin, g, wtg, wtu)


def mm_res(y, w, x, *, tm, name):
    T, K = y.shape

    def body(y_ref, w_ref, x_ref, o_ref):
        o_ref[...] = x_ref[...] + _dot(y_ref[...], w_ref[...])

    return pl.pallas_call(
        body, grid=(T // tm,),
        in_specs=[pl.BlockSpec((tm, K), lambda i: (i, 0)),
                  pl.BlockSpec((K, D), lambda i: (0, 0), pipeline_mode=pl.Buffered(1)),
                  pl.BlockSpec((tm, D), lambda i: (i, 0))],
        out_specs=pl.BlockSpec((tm, D), lambda i: (i, 0)),
        out_shape=jax.ShapeDtypeStruct((T, D), f32),
        compiler_params=_cparams(("parallel",)), name=name,
    )(y, w, x)


def ffn_bwd(dx, wd, pg, pu, wtg, wtu, x, g, tok, w_next, *, tm, name):
    T = dx.shape[0]
    H = wd.shape[0]
    kn = w_next.shape[0]
    toks = [] if tok is None else [tok]

    def body(dx_ref, wd_ref, pg_ref, pu_ref, wg_ref, wu_ref, x_ref, g_ref, *rest):
        wn_ref = rest[len(toks)]
        dg_ref, du_ref, dxb_ref, dx1_ref, dgn_ref, dy_ref, dx1b_ref = rest[len(toks) + 1:]

        @pl.when(pl.program_id(0) == 0)
        def _():
            dgn_ref[...] = jnp.zeros(dgn_ref.shape, f32)

        dxv = dx_ref[...]
        dxb = dxv.astype(bf16)
        dxb_ref[...] = dxb
        dh = None
        for c0, c1 in _col_chunks(H):
            da = _dot_nt(dxb, wd_ref[c0:c1, :])
            gate = pg_ref[:, c0:c1].astype(f32)
            up = pu_ref[:, c0:c1].astype(f32)
            sg = jax.nn.sigmoid(gate)
            dgc = (da * up * (sg * (1.0 + gate * (1.0 - sg)))).astype(bf16)
            duc = (da * gate * sg).astype(bf16)
            dg_ref[:, c0:c1] = dgc
            du_ref[:, c0:c1] = duc
            part = _dot(dgc, wg_ref[c0:c1, :]) + _dot(duc, wu_ref[c0:c1, :])
            dh = part if dh is None else dh + part
        gv = g_ref[...] + rest[0][0:1, 0:1] if toks else g_ref[...]
        xv = x_ref[...]
        r = lax.rsqrt(jnp.mean(xv * xv, axis=-1, keepdims=True) + EPS)
        xhat = xv * r
        dgn_ref[...] += jnp.sum(dh * xhat, axis=0, keepdims=True)
        dxhat = dh * gv
        dx1 = dxv + r * (dxhat - xhat * jnp.mean(dxhat * xhat, axis=-1, keepdims=True))
        dx1_ref[...] = dx1
        dx1b = dx1.astype(bf16)
        dx1b_ref[...] = dx1b
        dy_ref[...] = _dot_nt(dx1b, wn_ref[...]).astype(bf16)

    dspec = pl.BlockSpec((tm, D), lambda i: (i, 0))
    hspec = pl.BlockSpec((tm, H), lambda i: (i, 0))
    wspec = pl.BlockSpec((H, D), lambda i: (0, 0), pipeline_mode=pl.Buffered(1))
    gspec = pl.BlockSpec((1, D), lambda i: (0, 0))
    return pl.pallas_call(
        body, grid=(T // tm,),
        in_specs=[dspec, wspec, hspec, hspec, wspec, wspec, dspec, gspec]
        + [pl.BlockSpec((8, 128), lambda i: (0, 0))] * len(toks)
        + [pl.BlockSpec((kn, D), lambda i: (0, 0), pipeline_mode=pl.Buffered(1))],
        out_specs=[hspec, hspec, dspec, dspec, gspec, pl.BlockSpec((tm, kn), lambda i: (i, 0)), dspec],
        out_shape=[jax.ShapeDtypeStruct((T, H), bf16)] * 2 + [jax.ShapeDtypeStruct((T, D), bf16),
                   jax.ShapeDtypeStruct((T, D), f32), jax.ShapeDtypeStruct((1, D), f32),
                   jax.ShapeDtypeStruct((T, kn), bf16), jax.ShapeDtypeStruct((T, D), bf16)],
        compiler_params=_cparams(("arbitrary",)), name=name,
    )(dx, wd, pg, pu, wtg, wtu, x, g, *toks, w_next)


def mm_normbwd(dps, ws, x, g, dres, tok, *, tm, transposed, name):
    T = x.shape[0]
    nq = len(dps)
    toks = [] if tok is None else [tok]
    n_in = 2 * nq + 3 + len(toks)

    def body(*refs):
        dp_refs, w_refs = refs[:nq], refs[nq:2 * nq]
        x_ref, g_ref, dres_ref = refs[2 * nq:2 * nq + 3]
        dx_ref, dg_ref = refs[n_in:n_in + 2]

        @pl.when(pl.program_id(0) == 0)
        def _():
            dg_ref[...] = jnp.zeros(dg_ref.shape, f32)

        dh = None
        for q in range(nq):
            if transposed:
                part = _dot(dp_refs[q][...], w_refs[q][...])
            else:
                part = _dot_nt(dp_refs[q][...], w_refs[q][...])
            dh = part if dh is None else dh + part
        gv = g_ref[...] + refs[2 * nq + 3][0:1, 0:1] if toks else g_ref[...]
        xv = x_ref[...]
        r = lax.rsqrt(jnp.mean(xv * xv, axis=-1, keepdims=True) + EPS)
        xhat = xv * r
        dg_ref[...] += jnp.sum(dh * xhat, axis=0, keepdims=True)
        dxhat = dh * gv
        dx_ref[...] = dres_ref[...] + r * (dxhat - xhat * jnp.mean(dxhat * xhat, axis=-1, keepdims=True))

    dspec = pl.BlockSpec((tm, D), lambda i: (i, 0))
    return pl.pallas_call(
        body, grid=(T // tm,),
        in_specs=[pl.BlockSpec((tm, dp.shape[1]), lambda i: (i, 0)) for dp in dps]
        + [pl.BlockSpec(w.shape, lambda i: (0, 0), pipeline_mode=pl.Buffered(1)) for w in ws]
        + [dspec, pl.BlockSpec((1, D), lambda i: (0, 0)), dspec] + [pl.BlockSpec((8, 128), lambda i: (0, 0))] * len(toks),
        out_specs=[dspec, pl.BlockSpec((1, D), lambda i: (0, 0))],
        out_shape=[jax.ShapeDtypeStruct((T, D), f32), jax.ShapeDtypeStruct((1, D), f32)],
        compiler_params=_cparams(("arbitrary",)), name=name,
    )(*dps, *ws, x, g, dres, *toks)


def mm_tn(a, b, *, name):
    T, K = a.shape
    n = b.shape[1]

    def body(a_ref, b_ref, o_ref):
        o_ref[...] = _dot_tn(a_ref[...], b_ref[...]).astype(bf16)

    out = pl.pallas_call(
        body, grid=(K // MXU_COLS,),
        in_specs=[pl.BlockSpec((T, MXU_COLS), lambda d: (0, d)),
                  pl.BlockSpec((T, n), lambda d: (0, 0), pipeline_mode=pl.Buffered(1))],
        out_specs=pl.BlockSpec((MXU_COLS, n), lambda d: (d, 0)),
        out_shape=jax.ShapeDtypeStruct((K, n), bf16),
        compiler_params=_cparams(("parallel",)), name=name,
    )(a, b)
    return out.reshape(NDEV, K // NDEV, n)


def ffn_dw(dpg, dpu, a, h2, dxb, *, name):
    T, H = dpg.shape

    def body(g_ref, u_ref, a_ref, h_ref, x_ref, og_ref, ou_ref, od_ref):
        h = h_ref[...]
        og_ref[...] = _dot_tn(g_ref[...], h).astype(bf16)
        ou_ref[...] = _dot_tn(u_ref[...], h).astype(bf16)
        od_ref[...] = _dot_tn(a_ref[...], x_ref[...]).astype(bf16)

    blk = pl.BlockSpec((T, MXU_COLS), lambda d: (0, d))
    full = pl.BlockSpec((T, D), lambda d: (0, 0), pipeline_mode=pl.Buffered(1))
    out = pl.BlockSpec((MXU_COLS, D), lambda d: (d, 0))
    outs = pl.pallas_call(
        body, grid=(H // MXU_COLS,),
        in_specs=[blk, blk, blk, full, full], out_specs=[out] * 3,
        out_shape=[jax.ShapeDtypeStruct((H, D), bf16)] * 3,
        compiler_params=_cparams(("parallel",)), name=name,
    )(dpg, dpu, a, h2, dxb)
    return [o.reshape(NDEV, H // NDEV, D) for o in outs]


def _prev_halo_spec(tm, ncol):
    return pl.BlockSpec((HALO, ncol), lambda i: (jnp.maximum(i * (tm // HALO) - 1, 0), 0))


def _next_halo_spec(tm, ncol, T):
    return pl.BlockSpec((HALO, ncol), lambda i: (jnp.minimum((i + 1) * (tm // HALO), T // HALO - 1), 0))


def a_mid_fwd(p, gv, wm, bsb, *, tm, name):
    T = p.shape[0]

    def body(p_ref, gv_ref, wm_ref, bs_ref, y_ref, vn_sc):
        v = _gelu(p_ref[:, D:].astype(f32))
        vc = v - jnp.mean(v, axis=-1, keepdims=True)
        var = jnp.mean(vc * vc, axis=-1, keepdims=True)
        vn_sc[...] = (vc * lax.rsqrt(var + EPS) * gv_ref[...]).astype(bf16)
        for g in range(A_GROUPS):
            cs = slice(g * GBLK, (g + 1) * GBLK)
            for r in range(tm // GBLK):
                rs = slice(r * GBLK, (r + 1) * GBLK)
                sv = _dot(wm_ref[g], vn_sc[rs, cs]) + bs_ref[g]
                y_ref[rs, cs] = (_gelu(p_ref[rs, cs].astype(f32)) * sv).astype(bf16)

    return pl.pallas_call(
        body, grid=(T // tm,),
        in_specs=[pl.BlockSpec((tm, 2 * D), lambda i: (i, 0)), pl.BlockSpec((1, D), lambda i: (0, 0)),
                  pl.BlockSpec((A_GROUPS, GBLK, GBLK), lambda i: (0, 0, 0)),
                  pl.BlockSpec((A_GROUPS, GBLK, GBLK), lambda i: (0, 0, 0))],
        out_specs=pl.BlockSpec((tm, D), lambda i: (i, 0)),
        out_shape=jax.ShapeDtypeStruct((T, D), bf16),
        scratch_shapes=[pltpu.VMEM((tm, D), bf16)],
        compiler_params=_cparams(("parallel",)), name=name,
    )(p, gv, wm, bsb)


def a_mid_bwd(p, dy, gv, wm, bsb, *, tm, name):
    T = p.shape[0]

    def body(p_ref, dy_ref, gv_ref, wm_ref, bs_ref, dp_ref, dgv_ref, dwm_ref, dbs_ref, vn_sc, dvn_sc):
        @pl.when(pl.program_id(0) == 0)
        def _():
            dgv_ref[...] = jnp.zeros(dgv_ref.shape, f32)
            dwm_ref[...] = jnp.zeros(dwm_ref.shape, f32)
            dbs_ref[...] = jnp.zeros(dbs_ref.shape, f32)

        zv = p_ref[:, D:].astype(f32)
        cdf_v = _gelu_cdf(zv)
        v = zv * cdf_v
        vc = v - jnp.mean(v, axis=-1, keepdims=True)
        rstd = lax.rsqrt(jnp.mean(vc * vc, axis=-1, keepdims=True) + EPS)
        vhat = vc * rstd
        vn_sc[...] = (vhat * gv_ref[...]).astype(bf16)
        for g in range(A_GROUPS):
            cs = slice(g * GBLK, (g + 1) * GBLK)
            dwm = jnp.zeros((GBLK, GBLK), f32)
            dbs = jnp.zeros((GBLK, 1), f32)
            for r in range(tm // GBLK):
                rs = slice(r * GBLK, (r + 1) * GBLK)
                zu = p_ref[rs, cs].astype(f32)
                cdf_u = _gelu_cdf(zu)
                vn = vn_sc[rs, cs]
                sv = _dot(wm_ref[g], vn) + bs_ref[g]
                dyb = dy_ref[rs, cs].astype(f32)
                dsv = dyb * (zu * cdf_u)
                dsvb = dsv.astype(bf16)
                dp_ref[rs, cs] = (dyb * sv * _gelu_grad(zu, cdf_u)).astype(bf16)
                dwm += _dot_nt(dsvb, vn)
                dbs += jnp.sum(dsv, axis=1, keepdims=True)
                dvn_sc[rs, cs] = _dot_tn(wm_ref[g], dsvb)
            dwm_ref[g] += dwm
            dbs_ref[g] += dbs
        dvn = dvn_sc[...]
        dgv_ref[...] += jnp.sum(dvn * vhat, axis=0, keepdims=True)
        dvhat = dvn * gv_ref[...]
        dv = rstd * (dvhat - jnp.mean(dvhat, axis=-1, keepdims=True)
                     - vhat * jnp.mean(dvhat * vhat, axis=-1, keepdims=True))
        dp_ref[:, D:] = (dv * _gelu_grad(zv, cdf_v)).astype(bf16)

    return pl.pallas_call(
        body, grid=(T // tm,),
        in_specs=[pl.BlockSpec((tm, 2 * D), lambda i: (i, 0)), pl.BlockSpec((tm, D), lambda i: (i, 0)),
                  pl.BlockSpec((1, D), lambda i: (0, 0)),
                  pl.BlockSpec((A_GROUPS, GBLK, GBLK), lambda i: (0, 0, 0)),
                  pl.BlockSpec((A_GROUPS, GBLK, GBLK), lambda i: (0, 0, 0))],
        out_specs=[pl.BlockSpec((tm, 2 * D), lambda i: (i, 0)), pl.BlockSpec((1, D), lambda i: (0, 0)),
                   pl.BlockSpec((A_GROUPS, GBLK, GBLK), lambda i: (0, 0, 0)),
                   pl.BlockSpec((A_GROUPS, GBLK, 1), lambda i: (0, 0, 0))],
        out_shape=[jax.ShapeDtypeStruct((T, 2 * D), bf16), jax.ShapeDtypeStruct((1, D), f32),
                   jax.ShapeDtypeStruct((A_GROUPS, GBLK, GBLK), f32), jax.ShapeDtypeStruct((A_GROUPS, GBLK, 1), f32)],
        scratch_shapes=[pltpu.VMEM((tm, D), bf16), pltpu.VMEM((tm, D), f32)],
        compiler_params=_cparams(("arbitrary",)), name=name,
    )(p, dy, gv, wm, bsb)


def b_mid_fwd(p, cw, *, tm, S, name):
    T = p.shape[0]
    nst = S // tm

    def body(p_ref, ph_ref, cw_ref, y_ref, ext):
        first = (pl.program_id(0) % nst) == 0
        for c0 in range(0, D, COLS):
            cs, cc, cx = slice(c0, c0 + COLS), slice(D + c0, D + c0 + COLS), slice(2 * D + c0, 2 * D + c0 + COLS)
            q = p_ref[:, cc].astype(f32) * p_ref[:, cx].astype(f32)
            qh = ph_ref[:, cc].astype(f32) * ph_ref[:, cx].astype(f32)
            ext[0:HALO, cs] = jnp.where(first, 0.0, qh)
            ext[HALO:, cs] = q
            y = (cw_ref[0:1, cs] * ext[pl.ds(HALO - 2, tm), cs] + cw_ref[1:2, cs] * ext[pl.ds(HALO - 1, tm), cs]
                 + cw_ref[2:3, cs] * q)
            y_ref[:, cs] = (p_ref[:, cs].astype(f32) * y).astype(bf16)

    return pl.pallas_call(
        body, grid=(T // tm,),
        in_specs=[pl.BlockSpec((tm, 3 * D), lambda i: (i, 0)), _prev_halo_spec(tm, 3 * D),
                  pl.BlockSpec((3, D), lambda i: (0, 0))],
        out_specs=pl.BlockSpec((tm, D), lambda i: (i, 0)),
        out_shape=jax.ShapeDtypeStruct((T, D), bf16),
        scratch_shapes=[pltpu.VMEM((tm + HALO, D), f32)],
        compiler_params=_cparams(("parallel",)), name=name,
    )(p, p, cw)


def b_mid_bwd(p, dy, cw, *, tm, S, name):
    T = p.shape[0]
    nst = S // tm

    def body(p_ref, ph_ref, pn_ref, dy_ref, dyn_ref, cw_ref, dp_ref, dcw_ref, ext, ext2):
        i = pl.program_id(0)
        first = (i % nst) == 0
        last = (i % nst) == nst - 1

        @pl.when(i == 0)
        def _():
            dcw_ref[...] = jnp.zeros(dcw_ref.shape, f32)

        for c0 in range(0, D, COLS):
            cs, cc, cx = slice(c0, c0 + COLS), slice(D + c0, D + c0 + COLS), slice(2 * D + c0, 2 * D + c0 + COLS)
            gb, gc, xt = p_ref[:, cs].astype(f32), p_ref[:, cc].astype(f32), p_ref[:, cx].astype(f32)
            q = gc * xt
            ext[0:HALO, cs] = jnp.where(first, 0.0, ph_ref[:, cc].astype(f32) * ph_ref[:, cx].astype(f32))
            ext[HALO:, cs] = q
            q2 = ext[pl.ds(HALO - 2, tm), cs]
            q1 = ext[pl.ds(HALO - 1, tm), cs]
            y = cw_ref[0:1, cs] * q2 + cw_ref[1:2, cs] * q1 + cw_ref[2:3, cs] * q
            dyo = dy_ref[:, cs].astype(f32)
            dp_ref[:, cs] = (dyo * y).astype(bf16)
            dyc = dyo * gb
            ext2[0:tm, cs] = dyc
            ext2[tm:, cs] = jnp.where(last, 0.0, dyn_ref[:, cs].astype(f32) * pn_ref[:, cs].astype(f32))
            dq = cw_ref[2:3, cs] * dyc + cw_ref[1:2, cs] * ext2[pl.ds(1, tm), cs] + cw_ref[0:1, cs] * ext2[pl.ds(2, tm), cs]
            dp_ref[:, cc] = (dq * xt).astype(bf16)
            dp_ref[:, cx] = (dq * gc).astype(bf16)
            dcw_ref[0:1, cs] += jnp.sum(dyc * q2, axis=0, keepdims=True)
            dcw_ref[1:2, cs] += jnp.sum(dyc * q1, axis=0, keepdims=True)
            dcw_ref[2:3, cs] += jnp.sum(dyc * q, axis=0, keepdims=True)

    return pl.pallas_call(
        body, grid=(T // tm,),
        in_specs=[pl.BlockSpec((tm, 3 * D), lambda i: (i, 0)), _prev_halo_spec(tm, 3 * D), _next_halo_spec(tm, 3 * D, T),
                  pl.BlockSpec((tm, D), lambda i: (i, 0)), _next_halo_spec(tm, D, T),
                  pl.BlockSpec((3, D), lambda i: (0, 0))],
        out_specs=[pl.BlockSpec((tm, 3 * D), lambda i: (i, 0)), pl.BlockSpec((3, D), lambda i: (0, 0))],
        out_shape=[jax.ShapeDtypeStruct((T, 3 * D), bf16), jax.ShapeDtypeStruct((3, D), f32)],
        scratch_shapes=[pltpu.VMEM((tm + HALO, D), f32), pltpu.VMEM((tm + HALO, D), f32)],
        compiler_params=_cparams(("arbitrary",)), name=name,
    )(p, p, p, dy, dy, cw)


def _pool_counts(i, nst, tm, rows, row0, w):
    t = (i % nst) * tm + row0 + lax.broadcasted_iota(jnp.int32, (rows, 1), 0)
    return jnp.minimum(t + 1, w).astype(f32)


def _pool_diff(p_ref, ext, g, i, nst, tm):
    w = POOL_WINDOWS[g]
    cs = slice(g * C_GDIM, (g + 1) * C_GDIM)
    pg = p_ref[:, cs].astype(f32)
    s = pg
    for k in range(1, w):
        s = s + ext[pl.ds(HALO - k, tm), cs]
    return s / _pool_counts(i, nst, tm, tm, 0, w) - pg


def c_mid_fwd(p, wg, scale, *, tm, S, name):
    T = p.shape[0]
    nst = S // tm

    def body(p_ref, ph_ref, wg_ref, sc_ref, y_ref, ext):
        i = pl.program_id(0)
        first = (i % nst) == 0
        ext[0:HALO, :] = jnp.where(first, 0.0, ph_ref[...].astype(f32))
        ext[HALO:, :] = p_ref[...].astype(f32)
        for g in range(C_GROUPS):
            cs = slice(g * C_GDIM, (g + 1) * C_GDIM)
            dg = _pool_diff(p_ref, ext, g, i, nst, tm).astype(bf16)
            wv = wg_ref[:, g].reshape(C_GDIM, C_GDIM)
            y_ref[:, cs] = (_dot(dg, wv) * sc_ref[:, cs]).astype(bf16)

    return pl.pallas_call(
        body, grid=(T // tm,),
        in_specs=[pl.BlockSpec((tm, D), lambda i: (i, 0)), _prev_halo_spec(tm, D),
                  pl.BlockSpec((NDEV, C_GROUPS, C_GDIM // NDEV, C_GDIM), lambda i: (0, 0, 0, 0)),
                  pl.BlockSpec((1, D), lambda i: (0, 0))],
        out_specs=pl.BlockSpec((tm, D), lambda i: (i, 0)),
        out_shape=jax.ShapeDtypeStruct((T, D), bf16),
        scratch_shapes=[pltpu.VMEM((tm + HALO, D), f32)],
        compiler_params=_cparams(("parallel",)), name=name,
    )(p, p, wg, scale)


def c_mid_bwd(p, dy, wg, scale, *, tm, S, name):
    T = p.shape[0]
    nst = S // tm

    def body(p_ref, ph_ref, dy_ref, dyn_ref, wg_ref, sc_ref, dp_ref, dsc_ref, dwg_ref, ext, ext2):
        i = pl.program_id(0)
        first = (i % nst) == 0
        last = (i % nst) == nst - 1

        @pl.when(i == 0)
        def _():
            dsc_ref[...] = jnp.zeros(dsc_ref.shape, f32)
            dwg_ref[...] = jnp.zeros(dwg_ref.shape, f32)

        ext[0:HALO, :] = jnp.where(first, 0.0, ph_ref[...].astype(f32))
        ext[HALO:, :] = p_ref[...].astype(f32)
        for g in range(C_GROUPS):
            w = POOL_WINDOWS[g]
            cs = slice(g * C_GDIM, (g + 1) * C_GDIM)
            dg = _pool_diff(p_ref, ext, g, i, nst, tm).astype(bf16)
            wv = wg_ref[:, g].reshape(C_GDIM, C_GDIM)
            dyo = dy_ref[:, cs].astype(f32)
            dsc_ref[:, cs] += jnp.sum(dyo * _dot(dg, wv), axis=0, keepdims=True)
            dyp = (dyo * sc_ref[:, cs]).astype(bf16)
            dypn = (dyn_ref[:, cs].astype(f32) * sc_ref[:, cs]).astype(bf16)
            dwg_ref[g] += _dot_tn(dg, dyp)
            dd = _dot_nt(dyp, wv)
            ddn = _dot_nt(dypn, wv)
            ext2[0:tm, cs] = dd / _pool_counts(i, nst, tm, tm, 0, w)
            ext2[tm:, cs] = jnp.where(last, 0.0, ddn / _pool_counts(i, nst, tm, HALO, tm, w))
            s = -dd
            for k in range(w):
                s = s + ext2[pl.ds(k, tm), cs]
            dp_ref[:, cs] = s.astype(bf16)

    return pl.pallas_call(
        body, grid=(T // tm,),
        in_specs=[pl.BlockSpec((tm, D), lambda i: (i, 0)), _prev_halo_spec(tm, D),
                  pl.BlockSpec((tm, D), lambda i: (i, 0)), _next_halo_spec(tm, D, T),
                  pl.BlockSpec((NDEV, C_GROUPS, C_GDIM // NDEV, C_GDIM), lambda i: (0, 0, 0, 0)),
                  pl.BlockSpec((1, D), lambda i: (0, 0))],
        out_specs=[pl.BlockSpec((tm, D), lambda i: (i, 0)), pl.BlockSpec((1, D), lambda i: (0, 0)),
                   pl.BlockSpec((C_GROUPS, C_GDIM, C_GDIM), lambda i: (0, 0, 0))],
        out_shape=[jax.ShapeDtypeStruct((T, D), bf16), jax.ShapeDtypeStruct((1, D), f32),
                   jax.ShapeDtypeStruct((C_GROUPS, C_GDIM, C_GDIM), f32)],
        scratch_shapes=[pltpu.VMEM((tm + HALO, D), f32), pltpu.VMEM((tm + HALO, D), f32)],
        compiler_params=_cparams(("arbitrary",)), name=name,
    )(p, p, dy, dy, wg, scale)


def final_loss(x, g, tgt, *, tm, name):
    T = x.shape[0]

    def body(x_ref, g_ref, t_ref, loss_ref, dx_ref, dg_ref):
        @pl.when(pl.program_id(0) == 0)
        def _():
            loss_ref[...] = jnp.zeros(loss_ref.shape, f32)
            dg_ref[...] = jnp.zeros(dg_ref.shape, f32)

        xv = x_ref[...]
        r = lax.rsqrt(jnp.mean(xv * xv, axis=-1, keepdims=True) + EPS)
        xhat = xv * r
        err = xhat * g_ref[...] - t_ref[...]
        loss_ref[...] += 0.5 * jnp.sum(jnp.mean(err * err, axis=-1, keepdims=True))
        dy = err * (1.0 / D)
        dg_ref[...] += jnp.sum(dy * xhat, axis=0, keepdims=True)
        dxhat = dy * g_ref[...]
        dx_ref[...] = r * (dxhat - xhat * jnp.mean(dxhat * xhat, axis=-1, keepdims=True))

    return pl.pallas_call(
        body, grid=(T // tm,),
        in_specs=[pl.BlockSpec((tm, D), lambda i: (i, 0)), pl.BlockSpec((1, D), lambda i: (0, 0)),
                  pl.BlockSpec((tm, D), lambda i: (i, 0))],
        out_specs=[pl.BlockSpec((8, 128), lambda i: (0, 0)), pl.BlockSpec((tm, D), lambda i: (i, 0)),
                   pl.BlockSpec((1, D), lambda i: (0, 0))],
        out_shape=[jax.ShapeDtypeStruct((8, 128), f32), jax.ShapeDtypeStruct((T, D), f32),
                   jax.ShapeDtypeStruct((1, D), f32)],
        compiler_params=_cparams(("arbitrary",)), name=name,
    )(x, g, tgt)


def _slot(px, py, pc):
    return 4 * px + 2 * py + pc


def _with_own_block(s, me):
    zone = lax.empty((NDEV,) + s.shape, s.dtype)
    return lax.dynamic_update_slice(zone, s[None], (me,) + (0,) * s.ndim)


def all_gather(arrs, me, *, name):
    n = len(arrs)

    def body(*refs):
        ins, outs = refs[:n], refs[2 * n:3 * n]
        send_sems, recv_sems = refs[3 * n:]
        x, y, c = lax.axis_index("x"), lax.axis_index("y"), lax.axis_index("c")
        me, sibling = (x, y, c), (x, y, 1 - c)
        chips = [(1 - x, y), (x, 1 - y), (1 - x, 1 - y)]

        def copy(a, k, block, to, src=None):
            dst = outs[a].at[_slot(*block)]
            return pltpu.make_async_remote_copy(
                src_ref=dst if src is None else src, dst_ref=dst,
                send_sem=send_sems.at[a, k], recv_sem=recv_sems.at[a, k], device_id=to, device_id_type=MESH)

        first = []
        for a in range(n):
            first.append(copy(a, 0, me, sibling, src=ins[a]))
            first += [copy(a, 1 + j, me, (*chip, c), src=ins[a]) for j, chip in enumerate(chips)]
        for cp in first:
            cp.start()
        passed = []
        for j, chip in enumerate(chips):
            for a in range(n):
                copy(a, 1 + j, (*chip, c), me).wait_recv()
                fwd = copy(a, 4 + j, (*chip, c), sibling)
                fwd.start()
                passed.append(fwd)
        for a in range(n):
            copy(a, 0, sibling, me).wait_recv()
        for j, chip in enumerate(chips):
            for a in range(n):
                copy(a, 4 + j, (*chip, 1 - c), me).wait_recv()
        for cp in first + passed:
            cp.wait_send()

    any_spec = pl.BlockSpec(memory_space=pl.ANY)
    return pl.pallas_call(
        body,
        in_specs=[any_spec] * (2 * n), out_specs=[any_spec] * n,
        out_shape=[jax.ShapeDtypeStruct((NDEV,) + a.shape, a.dtype) for a in arrs],
        input_output_aliases={n + i: i for i in range(n)},
        scratch_shapes=[pltpu.SemaphoreType.DMA((n, 7)), pltpu.SemaphoreType.DMA((n, 7))],
        compiler_params=pltpu.CompilerParams(has_side_effects=True), name=name,
    )(*arrs, *[_with_own_block(a, me) for a in arrs])


HBM_SPEC = pl.BlockSpec(memory_space=pltpu.HBM)
SEM_SPEC = pl.BlockSpec(memory_space=pltpu.SEMAPHORE)
ANY_SPEC = pl.BlockSpec(memory_space=pl.ANY)
TOKEN_SHAPE = jax.ShapeDtypeStruct((8, 128), f32)
DATAFLOW_EFFECT = pltpu.SideEffectType.DATAFLOW_SIDE_EFFECTING


def _in_hbm(a):
    return pltpu.with_memory_space_constraint(a, pltpu.HBM)


def _hbm_like(a):
    return pltpu.HBM(a.shape, a.dtype)


def _mesh_pos():
    return lax.axis_index("x"), lax.axis_index("y"), lax.axis_index("c")


def _gather_targets(x, y, c):
    return [(x, y, 1 - c), (1 - x, y, c), (x, 1 - y, c), (1 - x, 1 - y, c)]


def gather_start(shards, me, after, *, name):
    n = len(shards)
    extra = [] if after is None else [after]

    def body(*refs):
        srcs, lands = refs[:n], refs[n:2 * n]
        send_sems, recv_sems = refs[2 * n + len(extra)], refs[2 * n + len(extra) + 1]
        token = refs[-1]
        x, y, c = _mesh_pos()
        me = _slot(x, y, c)
        for a in range(n):
            for k, to in enumerate(_gather_targets(x, y, c)):
                pltpu.make_async_remote_copy(
                    src_ref=srcs[a], dst_ref=lands[a].at[me], send_sem=send_sems.at[4 * a + k], recv_sem=recv_sems.at[4 * a + k],
                    device_id=to, device_id_type=MESH).start()
        token[...] = jnp.zeros(token.shape, f32)

    lands = [_with_own_block(s, me) for s in shards]
    sems = pltpu.SemaphoreType.DMA((4 * n,))
    out = pl.pallas_call(
        body, name=name,
        in_specs=[HBM_SPEC] * (2 * n) + [ANY_SPEC] * len(extra),
        out_specs=[SEM_SPEC, SEM_SPEC] + [HBM_SPEC] * (2 * n) + [pl.BlockSpec(memory_space=pltpu.VMEM)],
        out_shape=[sems, sems] + [_hbm_like(s) for s in shards] + [_hbm_like(l) for l in lands] + [TOKEN_SHAPE],
        input_output_aliases={i: 2 + i for i in range(2 * n)},
        compiler_params=pltpu.CompilerParams(has_side_effects=DATAFLOW_EFFECT),
    )(*[_in_hbm(s) for s in shards], *[_in_hbm(l) for l in lands], *extra)
    return out[0], out[1], out[2:2 + n], out[2 + n:2 + 2 * n], out[-1]


def gather_wait(send_sems, recv_sems, shards, lands, after, *, name):
    n = len(shards)
    after = list(after) if isinstance(after, (list, tuple)) else [after]

    def body(*refs):
        srcs, lands_in = refs[:n], refs[n:2 * n]
        send_sems, recv_sems = refs[2 * n], refs[2 * n + 1]
        x, y, c = _mesh_pos()
        for a in range(n):
            for k, frm in enumerate(_gather_targets(x, y, c)):
                cp = pltpu.make_async_remote_copy(
                    src_ref=srcs[a], dst_ref=lands_in[a].at[_slot(*frm)], send_sem=send_sems.at[4 * a + k],
                    recv_sem=recv_sems.at[4 * a + k], device_id=frm, device_id_type=MESH)
                cp.wait_send()
                cp.wait_recv()

    out = pl.pallas_call(
        body, name=name,
        in_specs=[HBM_SPEC] * (2 * n) + [SEM_SPEC, SEM_SPEC] + [ANY_SPEC] * len(after),
        out_specs=[HBM_SPEC] * (2 * n),
        out_shape=[_hbm_like(s) for s in shards] + [_hbm_like(l) for l in lands],
        input_output_aliases={i: i for i in range(2 * n)},
        compiler_params=pltpu.CompilerParams(has_side_effects=DATAFLOW_EFFECT),
    )(*shards, *lands, send_sems, recv_sems, *after)
    return out[:n], out[n:]


def gather_finish(lands, *, name):
    n = len(lands)

    def body(*refs):
        lands_in, lands_out = refs[:n], refs[n:2 * n]
        send_sems, recv_sems = refs[2 * n:]
        x, y, c = _mesh_pos()
        sibling = (x, y, 1 - c)
        chips = [(1 - x, y), (x, 1 - y), (1 - x, 1 - y)]
        sent = []
        for a in range(n):
            for j, chip in enumerate(chips):
                s = _slot(*chip, c)
                cp = pltpu.make_async_remote_copy(
                    src_ref=lands_in[a].at[s], dst_ref=lands_out[a].at[s], send_sem=send_sems.at[a, j],
                    recv_sem=recv_sems.at[a, j], device_id=sibling, device_id_type=MESH)
                cp.start()
                sent.append(cp)
        for a in range(n):
            for j, chip in enumerate(chips):
                s = _slot(*chip, 1 - c)
                pltpu.make_async_remote_copy(
                    src_ref=lands_in[a].at[s], dst_ref=lands_out[a].at[s], send_sem=send_sems.at[a, j],
                    recv_sem=recv_sems.at[a, j], device_id=sibling, device_id_type=MESH).wait_recv()
        for cp in sent:
            cp.wait_send()

    return pl.pallas_call(
        body, name=name,
        in_specs=[ANY_SPEC] * n, out_specs=[ANY_SPEC] * n,
        out_shape=[jax.ShapeDtypeStruct(l.shape, l.dtype) for l in lands],
        input_output_aliases={i: i for i in range(n)},
        scratch_shapes=[pltpu.SemaphoreType.DMA((n, 3)), pltpu.SemaphoreType.DMA((n, 3))],
        compiler_params=pltpu.CompilerParams(has_side_effects=True),
    )(*lands)


def _peer(x, y, c, r):
    return (x ^ ((r >> 2) & 1), y ^ ((r >> 1) & 1), c ^ (r & 1))


def exchange_start(items, lands, after, *, name):
    n, m = len(items), len(lands)
    extra = [] if after is None else [after]

    def body(*refs):
        gs, zones = refs[:n], refs[n:n + m]
        send_sems, recv_sems = refs[n + m + len(extra)], refs[n + m + len(extra) + 1]
        token = refs[-1]
        x, y, c = _mesh_pos()
        me = _slot(x, y, c)
        for r in (1, 2, 4, 3, 5, 6, 7):
            to = _peer(x, y, c, r)
            for a, (_, zi, l) in enumerate(items):
                pltpu.make_async_remote_copy(
                    src_ref=gs[a].at[_slot(*to)], dst_ref=zones[zi].at[me, l], send_sem=send_sems.at[7 * a + r - 1],
                    recv_sem=recv_sems.at[7 * a + r - 1], device_id=to, device_id_type=MESH).start()
        token[...] = jnp.zeros(token.shape, f32)

    gs = [g for g, _, _ in items]
    sems = pltpu.SemaphoreType.DMA((7 * n,))
    out = pl.pallas_call(
        body, name=name,
        in_specs=[HBM_SPEC] * (n + m) + [ANY_SPEC] * len(extra),
        out_specs=[SEM_SPEC, SEM_SPEC] + [HBM_SPEC] * (n + m) + [pl.BlockSpec(memory_space=pltpu.VMEM)],
        out_shape=[sems, sems] + [_hbm_like(g) for g in gs] + [_hbm_like(z) for z in lands] + [TOKEN_SHAPE],
        input_output_aliases={i: 2 + i for i in range(n + m)},
        compiler_params=pltpu.CompilerParams(has_side_effects=DATAFLOW_EFFECT),
    )(*[_in_hbm(g) for g in gs], *[_in_hbm(z) for z in lands], *extra)
    return out[0], out[1], out[2:2 + n], out[2 + n:2 + n + m], out[-1]


def exchange_wait(parts, lands, after, *, name):
    m = len(lands)
    flat_gs = [g for _, _, _, gs in parts for g in gs]
    ng = len(flat_gs)
    after = list(after) if isinstance(after, (list, tuple)) else [after]

    def body(*refs):
        gs, zones = refs[:ng], refs[ng:ng + m]
        sem_refs = refs[ng + m:ng + m + 2 * len(parts)]
        x, y, c = _mesh_pos()
        base = 0
        for pi, (items, _, _, _) in enumerate(parts):
            send_sems, recv_sems = sem_refs[2 * pi], sem_refs[2 * pi + 1]
            for r in range(1, NDEV):
                frm = _peer(x, y, c, r)
                for a, (_, zi, l) in enumerate(items):
                    cp = pltpu.make_async_remote_copy(
                        src_ref=gs[base + a].at[_slot(*frm)], dst_ref=zones[zi].at[_slot(*frm), l],
                        send_sem=send_sems.at[7 * a + r - 1], recv_sem=recv_sems.at[7 * a + r - 1],
                        device_id=frm, device_id_type=MESH)
                    cp.wait_send()
                    cp.wait_recv()
            base += len(items)

    sem_args = [s for _, ss, rs, _ in parts for s in (ss, rs)]
    out = pl.pallas_call(
        body, name=name,
        in_specs=[HBM_SPEC] * (ng + m) + [SEM_SPEC] * len(sem_args) + [ANY_SPEC] * len(after),
        out_specs=[HBM_SPEC] * (ng + m),
        out_shape=[_hbm_like(g) for g in flat_gs] + [_hbm_like(z) for z in lands],
        input_output_aliases={i: i for i in range(ng + m)},
        compiler_params=pltpu.CompilerParams(has_side_effects=DATAFLOW_EFFECT),
    )(*flat_gs, *lands, *sem_args, *after)
    return out[ng:]


def sum8(g, *, name):
    _, R, C = g.shape
    tr = R

    def body(g_ref, o_ref):
        s = g_ref[0]
        for k in range(1, NDEV):
            s = s + g_ref[k]
        o_ref[...] = s

    return pl.pallas_call(
        body, grid=(R // tr,),
        in_specs=[pl.BlockSpec((NDEV, tr, C), lambda i: (0, i, 0))],
        out_specs=pl.BlockSpec((tr, C), lambda i: (i, 0)),
        out_shape=jax.ShapeDtypeStruct((R, C), f32),
        compiler_params=_cparams(("parallel",)), name=name,
    )(g)


def _adam_math(w, g, m, v):
    m = ADAM_B1 * m + (1.0 - ADAM_B1) * g
    v = ADAM_B2 * v + (1.0 - ADAM_B2) * (g * g)
    m_hat = m / (1.0 - ADAM_B1 ** ADAM_STEP)
    v_hat = v / (1.0 - ADAM_B2 ** ADAM_STEP)
    delta = -ADAM_LR * (m_hat / (jnp.sqrt(v_hat) + ADAM_EPS) + ADAM_WD * w)
    return delta, m, v


def adamw_slices(w, m, v, recv, *, row0=0, col0=0, rrows=None, tr=None, tc=None, l0=0, nl=None, prev=None, name):
    L, R, C = w.shape
    Rp, Cp = recv.shape[2], recv.shape[3]
    nl = L - l0 if nl is None else nl
    tr = R if tr is None else tr
    tc = C if tc is None else tc
    rrows = Rp if rrows is None else rrows
    assert R % tr == 0 and C % tc == 0 and row0 % rrows == 0 and R <= rrows
    assert (tr == R or (Rp == R and row0 == 0)) and (tc == C or (Cp == C and col0 == 0))
    rr = rrows if tr == R else tr
    rc = Cp if tc == C else tc
    rb = row0 // rrows if tr == R else 0
    prevs = [] if prev is None else list(prev)

    def body(w_ref, m_ref, v_ref, r_ref, *rest):
        g_ref, d_ref, nm_ref, nv_ref = rest[len(prevs):]
        c0 = col0 if tc == C else 0
        g = r_ref[0, 0:tr, c0:c0 + tc].astype(f32)
        for k in range(1, NDEV):
            g = g + r_ref[k, 0:tr, c0:c0 + tc].astype(f32)
        delta, nm, nv = _adam_math(w_ref[...], g, m_ref[...], v_ref[...])
        g_ref[...] = g
        d_ref[...] = delta
        nm_ref[...] = nm
        nv_ref[...] = nv

    wspec = pl.BlockSpec((None, tr, tc), lambda l, i, j: (l0 + l, i, j))
    return pl.pallas_call(
        body, grid=(nl, R // tr, C // tc),
        in_specs=[wspec, wspec, wspec, pl.BlockSpec((NDEV, None, rr, rc), lambda l, i, j: (0, l0 + l, rb + i, j))]
        + [pl.BlockSpec(memory_space=pl.ANY)] * len(prevs),
        out_specs=[wspec] * 4, out_shape=[jax.ShapeDtypeStruct((L, R, C), f32)] * 4,
        input_output_aliases={4 + q: q for q in range(len(prevs))},
        compiler_params=_cparams(("parallel", "parallel", "parallel")), name=name,
    )(w, m, v, recv, *prevs)


def adamw_slices_t(w, m, v, recv, *, name):
    L, R, C = w.shape

    def body(w_ref, m_ref, v_ref, r_ref, g_ref, d_ref, nm_ref, nv_ref):
        gt = r_ref[0].astype(f32)
        for k in range(1, NDEV):
            gt = gt + r_ref[k].astype(f32)
        g = gt.T
        delta, nm, nv = _adam_math(w_ref[...], g, m_ref[...], v_ref[...])
        g_ref[...] = g
        d_ref[...] = delta
        nm_ref[...] = nm
        nv_ref[...] = nv

    wspec = pl.BlockSpec((None, R, C), lambda l: (l, 0, 0))
    return pl.pallas_call(
        body, grid=(L,),
        in_specs=[wspec, wspec, wspec, pl.BlockSpec((NDEV, None, C, R), lambda l: (0, l, 0, 0))],
        out_specs=[wspec] * 4, out_shape=[jax.ShapeDtypeStruct((L, R, C), f32)] * 4,
        compiler_params=_cparams(("parallel",)), name=name,
    )(w, m, v, recv)


def adamw_plain(w, m, v, g, *, name):
    R, C = w.shape

    def body(w_ref, m_ref, v_ref, g_ref, d_ref, nm_ref, nv_ref):
        delta, nm, nv = _adam_math(w_ref[...], g_ref[...], m_ref[...], v_ref[...])
        d_ref[...] = delta
        nm_ref[...] = nm
        nv_ref[...] = nv

    return pl.pallas_call(
        body, out_shape=[jax.ShapeDtypeStruct((R, C), f32)] * 3, name=name,
    )(w, m, v, g)


def _chunk_causal_mask():
    pos = jnp.arange(GBLK)
    return (pos[None, :] // CHUNK) <= (pos[:, None] // CHUNK)


def kernel(x, norm_mix_g, norm_ffn_g, final_norm_g, a_w_in, a_v_norm_g, a_w_s, a_b_s, a_w_out, b_w_in, b_conv_w, b_w_out, c_w_in, c_w_grp, c_scale, c_w_out, f_w_gate, f_w_up, f_w_down, loss_target, m_norm_mix_g, m_norm_ffn_g, m_final_norm_g, m_a_w_in, m_a_v_norm_g, m_a_w_s, m_a_b_s, m_a_w_out, m_b_w_in, m_b_conv_w, m_b_w_out, m_c_w_in, m_c_w_grp, m_c_scale, m_c_w_out, m_f_w_gate, m_f_w_up, m_f_w_down, v_norm_mix_g, v_norm_ffn_g, v_final_norm_g, v_a_w_in, v_a_v_norm_g, v_a_w_s, v_a_b_s, v_a_w_out, v_b_w_in, v_b_conv_w, v_b_w_out, v_c_w_in, v_c_w_grp, v_c_scale, v_c_w_out, v_f_w_gate, v_f_w_up, v_f_w_down):
    B, S, _ = x.shape
    T = B * S
    tm = min(512, S)
    tmm = min(1024, T)
    tms = min(512, T)
    tma = min(256, S)
    tmf = min(256, T)
    me = _slot(lax.axis_index("x"), lax.axis_index("y"), lax.axis_index("c"))

    def tb(w):
        return jnp.transpose(w, (0, 2, 1)).astype(bf16)

    small = jnp.concatenate([a_v_norm_g, b_conv_w[0], c_scale, jnp.zeros((2, GBLK), f32)], axis=0)
    started = [gather_start([tb(a_w_in[0:1])[0], a_w_out[0].astype(bf16), small], me, None, name="gather_start_0")]
    tok = started[0][4]
    z = tok[0, 0]
    f_g, f_u, f_d = tb(f_w_gate + z), tb(f_w_up + z), (f_w_down + z).astype(bf16)
    stage_shards = [
        None,
        [f_g[0], f_u[0], f_d[0]],
        [tb(b_w_in + z)[0], (b_w_out[0] + z).astype(bf16), f_g[1], f_u[1], f_d[1]],
        [(c_w_in[0] + z).astype(bf16), (c_w_grp[0] + z).astype(bf16), (c_w_out[0] + z).astype(bf16), f_g[2], f_u[2], f_d[2]],
        [tb(a_w_in[1:2] + z)[0], (a_w_out[1] + z).astype(bf16), f_g[3], f_u[3], f_d[3]],
    ]
    for s in range(1, len(stage_shards)):
        started.append(gather_start(stage_shards[s], me, tok, name=f"gather_start_{s}"))
        tok = started[-1][4]

    def stage_weights(s, after):
        send_sems, recv_sems, shards, lands, _ = started[s]
        _, lands = gather_wait(send_sems, recv_sems, shards, lands, after, name=f"gather_wait_{s}")
        lands = gather_finish(lands, name=f"gather_finish_{s}")
        return [t.reshape(-1, t.shape[-1]) if t.ndim == 3 else t for t in lands]

    wm_masked = jnp.where(_chunk_causal_mask()[None, None], a_w_s, 0.0).astype(bf16)
    bsb = jnp.broadcast_to(a_b_s[:, :, :, None], a_b_s.shape + (GBLK,))

    xc = x.reshape(T, D)
    saved, weights = [], []
    for i in range(DEPTH):
        kind, j = i % 3, i // 3
        gmix = norm_mix_g[i][None]
        if i == 0:
            w_in, w_out, small_g = stage_weights(0, tok)
            small_full = jnp.transpose(small_g.reshape(NDEV, 8, GBLK), (1, 0, 2)).reshape(8, D)
            gv_full, cw_full, cs_full = small_full[0:2], small_full[2:5], small_full[5:6]
            w = [w_in, w_out]
        else:
            w = stage_weights(i + 1, xc)
        if kind == 0:
            w_in, w_out = w[:2]
            p, h = norm_mm(xc, gmix, w_in, tm=tms, transposed=True, name=f"a_in_{i}")
            y = a_mid_fwd(p, gv_full[j][None], wm_masked[j], bsb[j], tm=tma, name=f"a_mid_{i}")
        elif kind == 1:
            w_in, w_out = w[:2]
            p, h = norm_mm(xc, gmix, w_in, tm=tms, transposed=True, name=f"b_in_{i}")
            y = b_mid_fwd(p, cw_full, tm=tm, S=S, name=f"b_mid_{i}")
        else:
            w_in, w_grp, w_out = w[:3]
            p, h = norm_mm(xc, gmix, w_in, tm=tms, transposed=False, name=f"c_in_{i}")
            y = c_mid_fwd(p, w_grp, cs_full, tm=tm, S=S, name=f"c_mid_{i}")
        if i == 0:
            w = w + stage_weights(1, y)
        weights.append(w)
        w_fg, w_fu, w_fd = w[-3:]
        x1, pg, pu, a, h2 = ffn_in(y, w_out, xc, norm_ffn_g[i][None], w_fg, w_fu, tm=tms, name=f"f_in_{i}")
        x2 = mm_res(a, w_fd, x1, tm=tms, name=f"f_out_{i}")
        saved.append((xc, p, h, y, x1, pg, pu, h2, a))
        xc = x2

    loss_blk, dx, d_final = final_loss(xc, final_norm_g[None], loss_target.reshape(T, D), tm=tms, name="final_loss")
    loss = lax.psum(loss_blk[0, 0], AXES)

    zone_shapes = [(2, 256, D), (2, GBLK, D), (1, 384, D), (1, GBLK, D), (1, GBLK, D), (1, GBLK, C_GDIM), (1, GBLK, D),
                   (DEPTH, F_LOC, D), (DEPTH, F_LOC, D), (DEPTH, F_LOC, D)]
    Z_AIN, Z_AOUT, Z_BIN, Z_BOUT, Z_CIN, Z_CGRP, Z_COUT, Z_FG, Z_FU, Z_FD = range(10)
    lands = [lax.empty((NDEV,) + s, bf16) for s in zone_shapes]
    parts = []

    def exchange(items, after, name):
        nonlocal lands
        for g, zi, l in items:
            own = lax.dynamic_index_in_dim(g, me, axis=0, keepdims=True)[None]
            lands[zi] = lax.dynamic_update_slice(lands[zi], own, (me, l) + (0,) * (g.ndim - 1))
        idx = sorted({zi for _, zi, _ in items})
        local = [(g, idx.index(zi), l) for g, zi, l in items]
        send_sems, recv_sems, gs, zs, token = exchange_start(local, [lands[zi] for zi in idx], after, name=name)
        for q, zi in enumerate(idx):
            lands[zi] = zs[q]
        parts.append(([(None, zi, l) for _, zi, l in items], send_sems, recv_sems, gs))
        return token

    d_mix, d_ffn = [None] * DEPTH, [None] * DEPTH
    d_gv, d_wm, d_bs = [None] * 2, [None] * 2, [None] * 2
    tok = None
    for i in reversed(range(DEPTH)):
        kind, j = i % 3, i // 3
        xin, p, h, y, x1, pg, pu, h2, a = saved[i]
        w = weights[i]
        w_in, w_out = w[0], w[-4]
        w_fg, w_fu, w_fd = w[-3:]
        dpg, dpu, dxb, dx1, d_ffn[i], dy, dx1b = ffn_bwd(dx, w_fd, pg, pu, w_fg, w_fu, x1, norm_ffn_g[i][None], tok, w_out,
                                                         tm=tmf, name=f"f_bwd_{i}")
        gw_g, gw_u, gw_d = ffn_dw(dpg, dpu, a, h2, dxb, name=f"f_dw_{i}")
        items = [(gw_g, Z_FG, i), (gw_u, Z_FU, i), (gw_d, Z_FD, i)]
        if i == 0:
            tok = exchange(items, tok, "exchange_start_ffn0")
            items = []
        if kind == 0:
            dp, d_gv[j], d_wm[j], d_bs[j] = a_mid_bwd(p, dy, gv_full[j][None], wm_masked[j], bsb[j], tm=tma, name=f"a_bwd_mid_{i}")
            items += [(mm_tn(dp, h, name=f"a_dw_in_{i}"), Z_AIN, j),
                      (mm_tn(y, dx1b, name=f"a_dw_out_{i}"), Z_AOUT, j)]
        elif kind == 1:
            dp, d_cw = b_mid_bwd(p, dy, cw_full, tm=tm, S=S, name=f"b_bwd_mid_{i}")
            items += [(mm_tn(dp, h, name=f"b_dw_in_{i}"), Z_BIN, 0),
                      (mm_tn(y, dx1b, name=f"b_dw_out_{i}"), Z_BOUT, 0)]
        else:
            dp, d_cs, d_wgrp = c_mid_bwd(p, dy, w[1], cs_full, tm=tm, S=S, name=f"c_bwd_mid_{i}")
            gw_cgrp = jnp.transpose(d_wgrp.reshape(C_GROUPS, NDEV, C_GDIM // NDEV, C_GDIM), (1, 0, 2, 3)).astype(bf16)
            items += [(mm_tn(h, dp, name=f"c_dw_in_{i}"), Z_CIN, 0),
                      (gw_cgrp.reshape(NDEV, GBLK, C_GDIM), Z_CGRP, 0),
                      (mm_tn(y, dx1b, name=f"c_dw_out_{i}"), Z_COUT, 0)]
        tok = exchange(items, tok, f"exchange_start_{i}")
        dx, d_mix[i] = mm_normbwd([dp], [w_in], xin, norm_mix_g[i][None], dx1, tok, tm=tms, transposed=(kind != 2),
                                  name=f"mix_bwd_in_{i}")
    grad_x = dx.reshape(B, S, D)

    mask = _chunk_causal_mask()
    small_parts = [
        jnp.concatenate(d_mix, axis=0).reshape(-1, GBLK),
        jnp.concatenate(d_ffn, axis=0).reshape(-1, GBLK),
        d_final.reshape(-1, GBLK),
        jnp.concatenate(d_gv, axis=0).reshape(-1, GBLK),
        jnp.where(mask[None, None], jnp.stack(d_wm), 0.0).reshape(-1, GBLK),
        jnp.stack(d_bs).reshape(-1, GBLK),
        d_cw.reshape(-1, GBLK),
        d_cs.reshape(-1, GBLK),
    ]
    small_rows = [q.shape[0] for q in small_parts]
    sg_send, sg_recv, sg_shards, sg_lands, _ = gather_start(
        [jnp.concatenate(small_parts, axis=0)], me, tok, name="gather_small_start")

    def plain(w, m, v, g, name):
        shp = w.shape
        w2, m2, v2, g2 = (t.reshape(-1, shp[-1]) for t in (w, m, v, g))
        return tuple(t.reshape(shp) for t in adamw_plain(w2, m2, v2, g2, name=name))

    def sliced(w, m, v, recv, name, **kw):
        shp = w.shape
        w3, m3, v3 = (t.reshape((shp[0], -1, shp[-1])) for t in (w, m, v))
        return tuple(t.reshape(shp) for t in adamw_slices(w3, m3, v3, recv, name=name, **kw))

    def sliced_t(w, m, v, recv, name, **kw):
        wt, mt, vt = (jnp.transpose(t, (0, 2, 1)) for t in (w, m, v))
        return adamw_slices(wt, mt, vt, recv, name=name, **kw)

    def untransposed(outs):
        return tuple(jnp.transpose(t, (0, 2, 1)) for t in outs)

    lands = list(exchange_wait(parts[:3], lands, dx, name="exchange_wait"))
    _, _, r_bin, r_bout, r_cin, r_cgrp, r_cout, r_fg, r_fu, r_fd = lands
    res = {}
    res["b_w_in"] = adamw_slices_t(b_w_in, m_b_w_in, v_b_w_in, r_bin, name="adam_b_w_in")
    res["b_w_out"] = sliced(b_w_out, m_b_w_out, v_b_w_out, r_bout, "adam_b_w_out")
    res["c_w_in"] = sliced(c_w_in, m_c_w_in, v_c_w_in, r_cin, "adam_c_w_in")
    res["c_w_grp"] = sliced(c_w_grp, m_c_w_grp, v_c_w_grp, r_cgrp, "adam_c_w_grp")
    res["c_w_out"] = sliced(c_w_out, m_c_w_out, v_c_w_out, r_cout, "adam_c_w_out")
    fg = sliced_t(f_w_gate, m_f_w_gate, v_f_w_gate, r_fg, "adam_f_gate_123", tc=256, l0=1)
    fu = sliced_t(f_w_up, m_f_w_up, v_f_w_up, r_fu, "adam_f_up_123", tc=256, l0=1)
    fd = adamw_slices(f_w_down, m_f_w_down, v_f_w_down, r_fd, tc=256, l0=1, name="adam_f_down_123")
    big = [res[n][1] for n in ("b_w_in", "b_w_out", "c_w_in", "c_w_grp", "c_w_out")] + [fg[1], fu[1], fd[1]]

    ffn0_items, ffn0_send, ffn0_recv, ffn0_gs = parts[3]
    ffn0_zones = [Z_FG, Z_FU, Z_FD]
    ffn0_part = ([(None, ffn0_zones.index(zi), l) for _, zi, l in ffn0_items], ffn0_send, ffn0_recv, ffn0_gs)
    r_fg, r_fu, r_fd = exchange_wait([ffn0_part], [lands[zi] for zi in ffn0_zones], big, name="exchange_wait_ffn0")
    fg = sliced_t(f_w_gate, m_f_w_gate, v_f_w_gate, r_fg, "adam_f_gate_0", tc=256, l0=0, nl=1, prev=fg)
    fu = sliced_t(f_w_up, m_f_w_up, v_f_w_up, r_fu, "adam_f_up_0", tc=256, l0=0, nl=1, prev=fu)
    fd = adamw_slices(f_w_down, m_f_w_down, v_f_w_down, r_fd, tc=256, l0=0, nl=1, prev=fd, name="adam_f_down_0")
    res["f_w_gate"], res["f_w_up"], res["f_w_down"] = untransposed(fg), untransposed(fu), tuple(fd)

    last_items, last_send, last_recv, last_gs = parts[4]
    last_zones = [Z_AIN, Z_AOUT]
    last_part = ([(None, last_zones.index(zi), l) for _, zi, l in last_items], last_send, last_recv, last_gs)
    r_ain, r_aout = exchange_wait([last_part], [lands[zi] for zi in last_zones], [fg[1], fu[1], fd[1]], name="exchange_wait_last")
    res["a_w_in"] = adamw_slices_t(a_w_in, m_a_w_in, v_a_w_in, r_ain, name="adam_a_w_in")
    res["a_w_out"] = sliced(a_w_out, m_a_w_out, v_a_w_out, r_aout, "adam_a_w_out")

    _, sg_lands = gather_wait(sg_send, sg_recv, sg_shards, sg_lands, [res["a_w_in"][1], res["a_w_out"][1]], name="gather_small_wait")
    (gs_all,) = gather_finish(sg_lands, name="gather_small_finish")
    gs = sum8(gs_all, name="sum_small_grads")
    offs = [0]
    for r in small_rows:
        offs.append(offs[-1] + r)
    sp = [gs[offs[q]:offs[q + 1]] for q in range(len(small_parts))]
    grad_norm_mix_g = sp[0].reshape(DEPTH, D)
    grad_norm_ffn_g = sp[1].reshape(DEPTH, D)
    grad_final_norm_g = sp[2].reshape(D)
    grad_a_w_s = sp[4].reshape(a_w_s.shape)
    grad_a_b_s = sp[5].reshape(a_b_s.shape)

    def my_cols(full):
        return lax.dynamic_slice_in_dim(full, me * GBLK, GBLK, axis=1)

    grad_a_v_norm_g = my_cols(sp[3].reshape(2, D))
    grad_b_conv_w = my_cols(sp[6].reshape(3, D))[None]
    grad_c_scale = my_cols(sp[7].reshape(1, D))
    res["norm_mix_g"] = (grad_norm_mix_g,) + plain(norm_mix_g, m_norm_mix_g, v_norm_mix_g, grad_norm_mix_g, "adam_norm_mix")
    res["norm_ffn_g"] = (grad_norm_ffn_g,) + plain(norm_ffn_g, m_norm_ffn_g, v_norm_ffn_g, grad_norm_ffn_g, "adam_norm_ffn")
    res["final_norm_g"] = (grad_final_norm_g,) + tuple(
        t.reshape(D) for t in plain(final_norm_g[None], m_final_norm_g[None], v_final_norm_g[None], grad_final_norm_g[None], "adam_final"))
    res["a_v_norm_g"] = (grad_a_v_norm_g,) + plain(a_v_norm_g, m_a_v_norm_g, v_a_v_norm_g, grad_a_v_norm_g, "adam_a_v_norm")
    res["a_w_s"] = (grad_a_w_s,) + plain(a_w_s, m_a_w_s, v_a_w_s, grad_a_w_s, "adam_a_w_s")
    res["a_b_s"] = (grad_a_b_s,) + plain(a_b_s, m_a_b_s, v_a_b_s, grad_a_b_s, "adam_a_b_s")
    res["b_conv_w"] = (grad_b_conv_w,) + plain(b_conv_w, m_b_conv_w, v_b_conv_w, grad_b_conv_w, "adam_b_conv")
    res["c_scale"] = (grad_c_scale,) + plain(c_scale, m_c_scale, v_c_scale, grad_c_scale, "adam_c_scale")

    order = ["norm_mix_g", "norm_ffn_g", "final_norm_g", "a_w_in", "a_v_norm_g", "a_w_s", "a_b_s", "a_w_out", "b_w_in",
             "b_conv_w", "b_w_out", "c_w_in", "c_w_grp", "c_scale", "c_w_out", "f_w_gate", "f_w_up", "f_w_down"]
    return (loss, grad_x, *[res[n][0] for n in order], *[res[n][1] for n in order],
            *[res[n][2] for n in order], *[res[n][3] for n in order])
```

```python
import functools

import jax
import jax.numpy as jnp
from jax import lax
from jax.experimental import pallas as pl
from jax.experimental.pallas import tpu as pltpu

f32 = jnp.float32
bf16 = jnp.bfloat16

NDEV = 8
D = 1024
EPS = 1e-6
CHUNK = 64
GBLK = 128
A_GROUPS = 8
C_GROUPS = 4
C_GDIM = 256
POOL_WINDOWS = (2, 4, 8, 16)
HALO = 16
COLS = 256
F_LOC = 352
MXU_COLS = 256
DEPTH = 4
AXES = ("x", "y", "c")
MESH = pl.DeviceIdType.MESH

ADAM_LR = 0.001
ADAM_B1 = 0.9
ADAM_B2 = 0.999
ADAM_EPS = 1e-08
ADAM_WD = 0.01
ADAM_STEP = 10

VMEM_LIMIT = 56 * 1024 * 1024


def _cparams(sem):
    return pltpu.CompilerParams(dimension_semantics=sem, vmem_limit_bytes=VMEM_LIMIT)


def _gelu(z):
    return 0.5 * z * (1.0 + lax.erf(z * 0.7071067811865476))


def _gelu_cdf(z):
    return 0.5 * (1.0 + lax.erf(z * 0.7071067811865476))


def _gelu_grad(z, cdf):
    return cdf + z * jnp.exp(-0.5 * z * z) * 0.3989422804014327


def _dot(a, b):
    return jnp.dot(a, b, preferred_element_type=f32)


def _dot_nt(a, b):
    return lax.dot_general(a, b, (((1,), (1,)), ((), ())), preferred_element_type=f32)


def _dot_tn(a, b):
    return lax.dot_general(a, b, (((0,), (0,)), ((), ())), preferred_element_type=f32)


def _col_chunks(n, width=1024):
    return [(c, min(c + width, n)) for c in range(0, n, width)]


def norm_mm(x, g, w, *, tm, transposed, name):
    T = x.shape[0]
    n = w.shape[0] if transposed else w.shape[1]

    def body(x_ref, g_ref, w_ref, p_ref, h_ref):
        xv = x_ref[...]
        r = lax.rsqrt(jnp.mean(xv * xv, axis=-1, keepdims=True) + EPS)
        h = (xv * r * g_ref[...]).astype(bf16)
        h_ref[...] = h
        p = _dot_nt(h, w_ref[...]) if transposed else _dot(h, w_ref[...])
        p_ref[...] = p.astype(bf16)

    return pl.pallas_call(
        body, grid=(T // tm,),
        in_specs=[pl.BlockSpec((tm, D), lambda i: (i, 0)), pl.BlockSpec((1, D), lambda i: (0, 0)),
                  pl.BlockSpec(w.shape, lambda i: (0, 0), pipeline_mode=pl.Buffered(1))],
        out_specs=[pl.BlockSpec((tm, n), lambda i: (i, 0)), pl.BlockSpec((tm, D), lambda i: (i, 0))],
        out_shape=[jax.ShapeDtypeStruct((T, n), bf16), jax.ShapeDtypeStruct((T, D), bf16)],
        compiler_params=_cparams(("parallel",)), name=name,
    )(x, g, w)


def ffn_fwd(y, w_out, xin, g, wtg, wtu, wd, *, tm, name):
    T, K = y.shape
    H = wtg.shape[0]

    def body(y_ref, wo_ref, xin_ref, g_ref, wg_ref, wu_ref, wd_ref, x_ref, x2_ref, pg_ref, pu_ref, a_ref, h_ref):
        xv = xin_ref[...] + _dot(y_ref[...], wo_ref[...])
        x_ref[...] = xv
        r = lax.rsqrt(jnp.mean(xv * xv, axis=-1, keepdims=True) + EPS)
        h = (xv * r * g_ref[...]).astype(bf16)
        h_ref[...] = h
        out = None
        for c0, c1 in _col_chunks(H):
            gate = _dot_nt(h, wg_ref[c0:c1, :])
            up = _dot_nt(h, wu_ref[c0:c1, :])
            pg_ref[:, c0:c1] = gate.astype(bf16)
            pu_ref[:, c0:c1] = up.astype(bf16)
            ac = (gate * jax.nn.sigmoid(gate) * up).astype(bf16)
            a_ref[:, c0:c1] = ac
            part = _dot(ac, wd_ref[c0:c1, :])
            out = part if out is None else out + part
        x2_ref[...] = xv + out

    wspec = pl.BlockSpec((H, D), lambda i: (0, 0), pipeline_mode=pl.Buffered(1))
    hspec = pl.BlockSpec((tm, H), lambda i: (i, 0))
    dspec = pl.BlockSpec((tm, D), lambda i: (i, 0))
    return pl.pallas_call(
        body, grid=(T // tm,),
        in_specs=[pl.BlockSpec((tm, K), lambda i: (i, 0)),
                  pl.BlockSpec((K, D), lambda i: (0, 0), pipeline_mode=pl.Buffered(1)),
                  dspec, pl.BlockSpec((1, D), lambda i: (0, 0)), wspec, wspec, wspec],
        out_specs=[dspec, dspec, hspec, hspec, hspec, dspec],
        out_shape=[jax.ShapeDtypeStruct((T, D), f32)] * 2 + [jax.ShapeDtypeStruct((T, H), bf16)] * 3
        + [jax.ShapeDtypeStruct((T, D), bf16)],
        compiler_params=_cparams(("parallel",)), name=name,
    )(y, w_out, xin, g, wtg, wtu, wd)


def ffn_bwd(dx, wd, pg, pu, wtg, wtu, x, g, tok, w_next, *, tm, name):
    T = dx.shape[0]
    H = wd.shape[0]
    kn = w_next.shape[0]
    toks = [] if tok is None else [tok]

    def body(dx_ref, wd_ref, pg_ref, pu_ref, wg_ref, wu_ref, x_ref, g_ref, *rest):
        wn_ref = rest[len(toks)]
        dg_ref, du_ref, dxb_ref, dx1_ref, dgn_ref, dy_ref, dx1b_ref = rest[len(toks) + 1:]

        @pl.when(pl.program_id(0) == 0)
        def _():
            dgn_ref[...] = jnp.zeros(dgn_ref.shape, f32)

        dxv = dx_ref[...]
        dxb = dxv.astype(bf16)
        dxb_ref[...] = dxb
        dh = None
        for c0, c1 in _col_chunks(H):
            da = _dot_nt(dxb, wd_ref[c0:c1, :])
            gate = pg_ref[:, c0:c1].astype(f32)
            up = pu_ref[:, c0:c1].astype(f32)
            sg = jax.nn.sigmoid(gate)
            dgc = (da * up * (sg * (1.0 + gate * (1.0 - sg)))).astype(bf16)
            duc = (da * gate * sg).astype(bf16)
            dg_ref[:, c0:c1] = dgc
            du_ref[:, c0:c1] = duc
            part = _dot(dgc, wg_ref[c0:c1, :]) + _dot(duc, wu_ref[c0:c1, :])
            dh = part if dh is None else dh + part
        gv = g_ref[...] + rest[0][0:1, 0:1] if toks else g_ref[...]
        xv = x_ref[...]
        r = lax.rsqrt(jnp.mean(xv * xv, axis=-1, keepdims=True) + EPS)
        xhat = xv * r
        dgn_ref[...] += jnp.sum(dh * xhat, axis=0, keepdims=True)
        dxhat = dh * gv
        dx1 = dxv + r * (dxhat - xhat * jnp.mean(dxhat * xhat, axis=-1, keepdims=True))
        dx1_ref[...] = dx1
        dx1b = dx1.astype(bf16)
        dx1b_ref[...] = dx1b
        dy_ref[...] = _dot_nt(dx1b, wn_ref[...]).astype(bf16)

    dspec = pl.BlockSpec((tm, D), lambda i: (i, 0))
    hspec = pl.BlockSpec((tm, H), lambda i: (i, 0))
    wspec = pl.BlockSpec((H, D), lambda i: (0, 0), pipeline_mode=pl.Buffered(1))
    gspec = pl.BlockSpec((1, D), lambda i: (0, 0))
    return pl.pallas_call(
        body, grid=(T // tm,),
        in_specs=[dspec, wspec, hspec, hspec, wspec, wspec, dspec, gspec]
        + [pl.BlockSpec((8, 128), lambda i: (0, 0))] * len(toks)
        + [pl.BlockSpec((kn, D), lambda i: (0, 0), pipeline_mode=pl.Buffered(1))],
        out_specs=[hspec, hspec, dspec, dspec, gspec, pl.BlockSpec((tm, kn), lambda i: (i, 0)), dspec],
        out_shape=[jax.ShapeDtypeStruct((T, H), bf16)] * 2 + [jax.ShapeDtypeStruct((T, D), bf16),
                   jax.ShapeDtypeStruct((T, D), f32), jax.ShapeDtypeStruct((1, D), f32),
                   jax.ShapeDtypeStruct((T, kn), bf16), jax.ShapeDtypeStruct((T, D), bf16)],
        compiler_params=_cparams(("arbitrary",)), name=name,
    )(dx, wd, pg, pu, wtg, wtu, x, g, *toks, w_next)


def mm_normbwd(dps, ws, x, g, dres, tok, *, tm, transposed, name):
    T = x.shape[0]
    nq = len(dps)
    toks = [] if tok is None else [tok]
    n_in = 2 * nq + 3 + len(toks)

    def body(*refs):
        dp_refs, w_refs = refs[:nq], refs[nq:2 * nq]
        x_ref, g_ref, dres_ref = refs[2 * nq:2 * nq + 3]
        dx_ref, dg_ref = refs[n_in:n_in + 2]

        @pl.when(pl.program_id(0) == 0)
        def _():
            dg_ref[...] = jnp.zeros(dg_ref.shape, f32)

        dh = None
        for q in range(nq):
            if transposed:
                part = _dot(dp_refs[q][...], w_refs[q][...])
            else:
                part = _dot_nt(dp_refs[q][...], w_refs[q][...])
            dh = part if dh is None else dh + part
        gv = g_ref[...] + refs[2 * nq + 3][0:1, 0:1] if toks else g_ref[...]
        xv = x_ref[...]
        r = lax.rsqrt(jnp.mean(xv * xv, axis=-1, keepdims=True) + EPS)
        xhat = xv * r
        dg_ref[...] += jnp.sum(dh * xhat, axis=0, keepdims=True)
        dxhat = dh * gv
        dx_ref[...] = dres_ref[...] + r * (dxhat - xhat * jnp.mean(dxhat * xhat, axis=-1, keepdims=True))

    dspec = pl.BlockSpec((tm, D), lambda i: (i, 0))
    return pl.pallas_call(
        body, grid=(T // tm,),
        in_specs=[pl.BlockSpec((tm, dp.shape[1]), lambda i: (i, 0)) for dp in dps]
        + [pl.BlockSpec(w.shape, lambda i: (0, 0), pipeline_mode=pl.Buffered(1)) for w in ws]
        + [dspec, pl.BlockSpec((1, D), lambda i: (0, 0)), dspec] + [pl.BlockSpec((8, 128), lambda i: (0, 0))] * len(toks),
        out_specs=[dspec, pl.BlockSpec((1, D), lambda i: (0, 0))],
        out_shape=[jax.ShapeDtypeStruct((T, D), f32), jax.ShapeDtypeStruct((1, D), f32)],
        compiler_params=_cparams(("arbitrary",)), name=name,
    )(*dps, *ws, x, g, dres, *toks)


def mm_tn(a, b, *, name):
    T, K = a.shape
    n = b.shape[1]

    def body(a_ref, b_ref, o_ref):
        o_ref[...] = _dot_tn(a_ref[...], b_ref[...]).astype(bf16)

    out = pl.pallas_call(
        body, grid=(K // MXU_COLS,),
        in_specs=[pl.BlockSpec((T, MXU_COLS), lambda d: (0, d)),
                  pl.BlockSpec((T, n), lambda d: (0, 0), pipeline_mode=pl.Buffered(1))],
        out_specs=pl.BlockSpec((MXU_COLS, n), lambda d: (d, 0)),
        out_shape=jax.ShapeDtypeStruct((K, n), bf16),
        compiler_params=_cparams(("parallel",)), name=name,
    )(a, b)
    return out.reshape(NDEV, K // NDEV, n)


def ffn_dw(dpg, dpu, a, h2, dxb, *, name):
    T, H = dpg.shape

    def body(g_ref, u_ref, a_ref, h_ref, x_ref, og_ref, ou_ref, od_ref):
        h = h_ref[...]
        og_ref[...] = _dot_tn(g_ref[...], h).astype(bf16)
        ou_ref[...] = _dot_tn(u_ref[...], h).astype(bf16)
        od_ref[...] = _dot_tn(a_ref[...], x_ref[...]).astype(bf16)

    blk = pl.BlockSpec((T, MXU_COLS), lambda d: (0, d))
    full = pl.BlockSpec((T, D), lambda d: (0, 0), pipeline_mode=pl.Buffered(1))
    out = pl.BlockSpec((MXU_COLS, D), lambda d: (d, 0))
    outs = pl.pallas_call(
        body, grid=(H // MXU_COLS,),
        in_specs=[blk, blk, blk, full, full], out_specs=[out] * 3,
        out_shape=[jax.ShapeDtypeStruct((H, D), bf16)] * 3,
        compiler_params=_cparams(("parallel",)), name=name,
    )(dpg, dpu, a, h2, dxb)
    return [o.reshape(NDEV, H // NDEV, D) for o in outs]


def _prev_halo_spec(tm, ncol):
    return pl.BlockSpec((HALO, ncol), lambda i: (jnp.maximum(i * (tm // HALO) - 1, 0), 0))


def _next_halo_spec(tm, ncol, T):
    return pl.BlockSpec((HALO, ncol), lambda i: (jnp.minimum((i + 1) * (tm // HALO), T // HALO - 1), 0))


def a_mid_fwd(p, gv, wm, bsb, *, tm, name):
    T = p.shape[0]

    def body(p_ref, gv_ref, wm_ref, bs_ref, y_ref, vn_sc):
        v = _gelu(p_ref[:, D:].astype(f32))
        vc = v - jnp.mean(v, axis=-1, keepdims=True)
        var = jnp.mean(vc * vc, axis=-1, keepdims=True)
        vn_sc[...] = (vc * lax.rsqrt(var + EPS) * gv_ref[...]).astype(bf16)
        for g in range(A_GROUPS):
            cs = slice(g * GBLK, (g + 1) * GBLK)
            for r in range(tm // GBLK):
                rs = slice(r * GBLK, (r + 1) * GBLK)
                sv = _dot(wm_ref[g], vn_sc[rs, cs]) + bs_ref[g]
                y_ref[rs, cs] = (_gelu(p_ref[rs, cs].astype(f32)) * sv).astype(bf16)

    return pl.pallas_call(
        body, grid=(T // tm,),
        in_specs=[pl.BlockSpec((tm, 2 * D), lambda i: (i, 0)), pl.BlockSpec((1, D), lambda i: (0, 0)),
                  pl.BlockSpec((A_GROUPS, GBLK, GBLK), lambda i: (0, 0, 0)),
                  pl.BlockSpec((A_GROUPS, GBLK, GBLK), lambda i: (0, 0, 0))],
        out_specs=pl.BlockSpec((tm, D), lambda i: (i, 0)),
        out_shape=jax.ShapeDtypeStruct((T, D), bf16),
        scratch_shapes=[pltpu.VMEM((tm, D), bf16)],
        compiler_params=_cparams(("parallel",)), name=name,
    )(p, gv, wm, bsb)


def a_mid_bwd(p, dy, gv, wm, bsb, *, tm, name):
    T = p.shape[0]

    def body(p_ref, dy_ref, gv_ref, wm_ref, bs_ref, dp_ref, dgv_ref, dwm_ref, dbs_ref, vn_sc, dvn_sc):
        @pl.when(pl.program_id(0) == 0)
        def _():
            dgv_ref[...] = jnp.zeros(dgv_ref.shape, f32)
            dwm_ref[...] = jnp.zeros(dwm_ref.shape, f32)
            dbs_ref[...] = jnp.zeros(dbs_ref.shape, f32)

        zv = p_ref[:, D:].astype(f32)
        cdf_v = _gelu_cdf(zv)
        v = zv * cdf_v
        vc = v - jnp.mean(v, axis=-1, keepdims=True)
        rstd = lax.rsqrt(jnp.mean(vc * vc, axis=-1, keepdims=True) + EPS)
        vhat = vc * rstd
        vn_sc[...] = (vhat * gv_ref[...]).astype(bf16)
        for g in range(A_GROUPS):
            cs = slice(g * GBLK, (g + 1) * GBLK)
            dwm = jnp.zeros((GBLK, GBLK), f32)
            dbs = jnp.zeros((GBLK, 1), f32)
            for r in range(tm // GBLK):
                rs = slice(r * GBLK, (r + 1) * GBLK)
                zu = p_ref[rs, cs].astype(f32)
                cdf_u = _gelu_cdf(zu)
                vn = vn_sc[rs, cs]
                sv = _dot(wm_ref[g], vn) + bs_ref[g]
                dyb = dy_ref[rs, cs].astype(f32)
                dsv = dyb * (zu * cdf_u)
                dsvb = dsv.astype(bf16)
                dp_ref[rs, cs] = (dyb * sv * _gelu_grad(zu, cdf_u)).astype(bf16)
                dwm += _dot_nt(dsvb, vn)
                dbs += jnp.sum(dsv, axis=1, keepdims=True)
                dvn_sc[rs, cs] = _dot_tn(wm_ref[g], dsvb)
            dwm_ref[g] += dwm
            dbs_ref[g] += dbs
        dvn = dvn_sc[...]
        dgv_ref[...] += jnp.sum(dvn * vhat, axis=0, keepdims=True)
        dvhat = dvn * gv_ref[...]
        dv = rstd * (dvhat - jnp.mean(dvhat, axis=-1, keepdims=True)
                     - vhat * jnp.mean(dvhat * vhat, axis=-1, keepdims=True))
        dp_ref[:, D:] = (dv * _gelu_grad(zv, cdf_v)).astype(bf16)

    return pl.pallas_call(
        body, grid=(T // tm,),
        in_specs=[pl.BlockSpec((tm, 2 * D), lambda i: (i, 0)), pl.BlockSpec((tm, D), lambda i: (i, 0)),
                  pl.BlockSpec((1, D), lambda i: (0, 0)),
                  pl.BlockSpec((A_GROUPS, GBLK, GBLK), lambda i: (0, 0, 0)),
                  pl.BlockSpec((A_GROUPS, GBLK, GBLK), lambda i: (0, 0, 0))],
        out_specs=[pl.BlockSpec((tm, 2 * D), lambda i: (i, 0)), pl.BlockSpec((1, D), lambda i: (0, 0)),
                   pl.BlockSpec((A_GROUPS, GBLK, GBLK), lambda i: (0, 0, 0)),
                   pl.BlockSpec((A_GROUPS, GBLK, 1), lambda i: (0, 0, 0))],
        out_shape=[jax.ShapeDtypeStruct((T, 2 * D), bf16), jax.ShapeDtypeStruct((1, D), f32),
                   jax.ShapeDtypeStruct((A_GROUPS, GBLK, GBLK), f32), jax.ShapeDtypeStruct((A_GROUPS, GBLK, 1), f32)],
        scratch_shapes=[pltpu.VMEM((tm, D), bf16), pltpu.VMEM((tm, D), f32)],
        compiler_params=_cparams(("arbitrary",)), name=name,
    )(p, dy, gv, wm, bsb)


def b_mid_fwd(p, cw, *, tm, S, name):
    T = p.shape[0]
    nst = S // tm

    def body(p_ref, ph_ref, cw_ref, y_ref, ext):
        first = (pl.program_id(0) % nst) == 0
        for c0 in range(0, D, COLS):
            cs, cc, cx = slice(c0, c0 + COLS), slice(D + c0, D + c0 + COLS), slice(2 * D + c0, 2 * D + c0 + COLS)
            q = p_ref[:, cc].astype(f32) * p_ref[:, cx].astype(f32)
            qh = ph_ref[:, cc].astype(f32) * ph_ref[:, cx].astype(f32)
            ext[0:HALO, cs] = jnp.where(first, 0.0, qh)
            ext[HALO:, cs] = q
            y = (cw_ref[0:1, cs] * ext[pl.ds(HALO - 2, tm), cs] + cw_ref[1:2, cs] * ext[pl.ds(HALO - 1, tm), cs]
                 + cw_ref[2:3, cs] * q)
            y_ref[:, cs] = (p_ref[:, cs].astype(f32) * y).astype(bf16)

    return pl.pallas_call(
        body, grid=(T // tm,),
        in_specs=[pl.BlockSpec((tm, 3 * D), lambda i: (i, 0)), _prev_halo_spec(tm, 3 * D),
                  pl.BlockSpec((3, D), lambda i: (0, 0))],
        out_specs=pl.BlockSpec((tm, D), lambda i: (i, 0)),
        out_shape=jax.ShapeDtypeStruct((T, D), bf16),
        scratch_shapes=[pltpu.VMEM((tm + HALO, D), f32)],
        compiler_params=_cparams(("parallel",)), name=name,
    )(p, p, cw)


def b_mid_bwd(p, dy, cw, *, tm, S, name):
    T = p.shape[0]
    nst = S // tm

    def body(p_ref, ph_ref, pn_ref, dy_ref, dyn_ref, cw_ref, dp_ref, dcw_ref, ext, ext2):
        i = pl.program_id(0)
        first = (i % nst) == 0
        last = (i % nst) == nst - 1

        @pl.when(i == 0)
        def _():
            dcw_ref[...] = jnp.zeros(dcw_ref.shape, f32)

        for c0 in range(0, D, COLS):
            cs, cc, cx = slice(c0, c0 + COLS), slice(D + c0, D + c0 + COLS), slice(2 * D + c0, 2 * D + c0 + COLS)
            gb, gc, xt = p_ref[:, cs].astype(f32), p_ref[:, cc].astype(f32), p_ref[:, cx].astype(f32)
            q = gc * xt
            ext[0:HALO, cs] = jnp.where(first, 0.0, ph_ref[:, cc].astype(f32) * ph_ref[:, cx].astype(f32))
            ext[HALO:, cs] = q
            q2 = ext[pl.ds(HALO - 2, tm), cs]
            q1 = ext[pl.ds(HALO - 1, tm), cs]
            y = cw_ref[0:1, cs] * q2 + cw_ref[1:2, cs] * q1 + cw_ref[2:3, cs] * q
            dyo = dy_ref[:, cs].astype(f32)
            dp_ref[:, cs] = (dyo * y).astype(bf16)
            dyc = dyo * gb
            ext2[0:tm, cs] = dyc
            ext2[tm:, cs] = jnp.where(last, 0.0, dyn_ref[:, cs].astype(f32) * pn_ref[:, cs].astype(f32))
            dq = cw_ref[2:3, cs] * dyc + cw_ref[1:2, cs] * ext2[pl.ds(1, tm), cs] + cw_ref[0:1, cs] * ext2[pl.ds(2, tm), cs]
            dp_ref[:, cc] = (dq * xt).astype(bf16)
            dp_ref[:, cx] = (dq * gc).astype(bf16)
            dcw_ref[0:1, cs] += jnp.sum(dyc * q2, axis=0, keepdims=True)
            dcw_ref[1:2, cs] += jnp.sum(dyc * q1, axis=0, keepdims=True)
            dcw_ref[2:3, cs] += jnp.sum(dyc * q, axis=0, keepdims=True)

    return pl.pallas_call(
        body, grid=(T // tm,),
        in_specs=[pl.BlockSpec((tm, 3 * D), lambda i: (i, 0)), _prev_halo_spec(tm, 3 * D), _next_halo_spec(tm, 3 * D, T),
                  pl.BlockSpec((tm, D), lambda i: (i, 0)), _next_halo_spec(tm, D, T),
                  pl.BlockSpec((3, D), lambda i: (0, 0))],
        out_specs=[pl.BlockSpec((tm, 3 * D), lambda i: (i, 0)), pl.BlockSpec((3, D), lambda i: (0, 0))],
        out_shape=[jax.ShapeDtypeStruct((T, 3 * D), bf16), jax.ShapeDtypeStruct((3, D), f32)],
        scratch_shapes=[pltpu.VMEM((tm + HALO, D), f32), pltpu.VMEM((tm + HALO, D), f32)],
        compiler_params=_cparams(("arbitrary",)), name=name,
    )(p, p, p, dy, dy, cw)


def _pool_counts(i, nst, tm, rows, row0, w):
    t = (i % nst) * tm + row0 + lax.broadcasted_iota(jnp.int32, (rows, 1), 0)
    return jnp.minimum(t + 1, w).astype(f32)


def _pool_diff(p_ref, ext, g, i, nst, tm):
    w = POOL_WINDOWS[g]
    cs = slice(g * C_GDIM, (g + 1) * C_GDIM)
    pg = p_ref[:, cs].astype(f32)
    s = pg
    for k in range(1, w):
        s = s + ext[pl.ds(HALO - k, tm), cs]
    return s / _pool_counts(i, nst, tm, tm, 0, w) - pg


def c_mid_fwd(p, wg, scale, *, tm, S, name):
    T = p.shape[0]
    nst = S // tm

    def body(p_ref, ph_ref, wg_ref, sc_ref, y_ref, ext):
        i = pl.program_id(0)
        first = (i % nst) == 0
        ext[0:HALO, :] = jnp.where(first, 0.0, ph_ref[...].astype(f32))
        ext[HALO:, :] = p_ref[...].astype(f32)
        for g in range(C_GROUPS):
            cs = slice(g * C_GDIM, (g + 1) * C_GDIM)
            dg = _pool_diff(p_ref, ext, g, i, nst, tm).astype(bf16)
            wv = wg_ref[:, g].reshape(C_GDIM, C_GDIM)
            y_ref[:, cs] = (_dot(dg, wv) * sc_ref[:, cs]).astype(bf16)

    return pl.pallas_call(
        body, grid=(T // tm,),
        in_specs=[pl.BlockSpec((tm, D), lambda i: (i, 0)), _prev_halo_spec(tm, D),
                  pl.BlockSpec((NDEV, C_GROUPS, C_GDIM // NDEV, C_GDIM), lambda i: (0, 0, 0, 0)),
                  pl.BlockSpec((1, D), lambda i: (0, 0))],
        out_specs=pl.BlockSpec((tm, D), lambda i: (i, 0)),
        out_shape=jax.ShapeDtypeStruct((T, D), bf16),
        scratch_shapes=[pltpu.VMEM((tm + HALO, D), f32)],
        compiler_params=_cparams(("parallel",)), name=name,
    )(p, p, wg, scale)


def c_mid_bwd(p, dy, wg, scale, *, tm, S, name):
    T = p.shape[0]
    nst = S // tm

    def body(p_ref, ph_ref, dy_ref, dyn_ref, wg_ref, sc_ref, dp_ref, dsc_ref, dwg_ref, ext, ext2):
        i = pl.program_id(0)
        first = (i % nst) == 0
        last = (i % nst) == nst - 1

        @pl.when(i == 0)
        def _():
            dsc_ref[...] = jnp.zeros(dsc_ref.shape, f32)
            dwg_ref[...] = jnp.zeros(dwg_ref.shape, f32)

        ext[0:HALO, :] = jnp.where(first, 0.0, ph_ref[...].astype(f32))
        ext[HALO:, :] = p_ref[...].astype(f32)
        for g in range(C_GROUPS):
            w = POOL_WINDOWS[g]
            cs = slice(g * C_GDIM, (g + 1) * C_GDIM)
            dg = _pool_diff(p_ref, ext, g, i, nst, tm).astype(bf16)
            wv = wg_ref[:, g].reshape(C_GDIM, C_GDIM)
            dyo = dy_ref[:, cs].astype(f32)
            dsc_ref[:, cs] += jnp.sum(dyo * _dot(dg, wv), axis=0, keepdims=True)
            dyp = (dyo * sc_ref[:, cs]).astype(bf16)
            dypn = (dyn_ref[:, cs].astype(f32) * sc_ref[:, cs]).astype(bf16)
            dwg_ref[g] += _dot_tn(dg, dyp)
            dd = _dot_nt(dyp, wv)
            ddn = _dot_nt(dypn, wv)
            ext2[0:tm, cs] = dd / _pool_counts(i, nst, tm, tm, 0, w)
            ext2[tm:, cs] = jnp.where(last, 0.0, ddn / _pool_counts(i, nst, tm, HALO, tm, w))
            s = -dd
            for k in range(w):
                s = s + ext2[pl.ds(k, tm), cs]
            dp_ref[:, cs] = s.astype(bf16)

    return pl.pallas_call(
        body, grid=(T // tm,),
        in_specs=[pl.BlockSpec((tm, D), lambda i: (i, 0)), _prev_halo_spec(tm, D),
                  pl.BlockSpec((tm, D), lambda i: (i, 0)), _next_halo_spec(tm, D, T),
                  pl.BlockSpec((NDEV, C_GROUPS, C_GDIM // NDEV, C_GDIM), lambda i: (0, 0, 0, 0)),
                  pl.BlockSpec((1, D), lambda i: (0, 0))],
        out_specs=[pl.BlockSpec((tm, D), lambda i: (i, 0)), pl.BlockSpec((1, D), lambda i: (0, 0)),
                   pl.BlockSpec((C_GROUPS, C_GDIM, C_GDIM), lambda i: (0, 0, 0))],
        out_shape=[jax.ShapeDtypeStruct((T, D), bf16), jax.ShapeDtypeStruct((1, D), f32),
                   jax.ShapeDtypeStruct((C_GROUPS, C_GDIM, C_GDIM), f32)],
        scratch_shapes=[pltpu.VMEM((tm + HALO, D), f32), pltpu.VMEM((tm + HALO, D), f32)],
        compiler_params=_cparams(("arbitrary",)), name=name,
    )(p, p, dy, dy, wg, scale)


def final_loss(x, g, tgt, *, tm, name):
    T = x.shape[0]

    def body(x_ref, g_ref, t_ref, loss_ref, dx_ref, dg_ref):
        @pl.when(pl.program_id(0) == 0)
        def _():
            loss_ref[...] = jnp.zeros(loss_ref.shape, f32)
            dg_ref[...] = jnp.zeros(dg_ref.shape, f32)

        xv = x_ref[...]
        r = lax.rsqrt(jnp.mean(xv * xv, axis=-1, keepdims=True) + EPS)
        xhat = xv * r
        err = xhat * g_ref[...] - t_ref[...]
        loss_ref[...] += 0.5 * jnp.sum(jnp.mean(err * err, axis=-1, keepdims=True))
        dy = err * (1.0 / D)
        dg_ref[...] += jnp.sum(dy * xhat, axis=0, keepdims=True)
        dxhat = dy * g_ref[...]
        dx_ref[...] = r * (dxhat - xhat * jnp.mean(dxhat * xhat, axis=-1, keepdims=True))

    return pl.pallas_call(
        body, grid=(T // tm,),
        in_specs=[pl.BlockSpec((tm, D), lambda i: (i, 0)), pl.BlockSpec((1, D), lambda i: (0, 0)),
                  pl.BlockSpec((tm, D), lambda i: (i, 0))],
        out_specs=[pl.BlockSpec((8, 128), lambda i: (0, 0)), pl.BlockSpec((tm, D), lambda i: (i, 0)),
                   pl.BlockSpec((1, D), lambda i: (0, 0))],
        out_shape=[jax.ShapeDtypeStruct((8, 128), f32), jax.ShapeDtypeStruct((T, D), f32),
                   jax.ShapeDtypeStruct((1, D), f32)],
        compiler_params=_cparams(("arbitrary",)), name=name,
    )(x, g, tgt)


def _slot(px, py, pc):
    return 4 * px + 2 * py + pc


def _with_own_block(s, me):
    zone = lax.empty((NDEV,) + s.shape, s.dtype)
    return lax.dynamic_update_slice(zone, s[None], (me,) + (0,) * s.ndim)


def all_gather(arrs, me, *, name):
    n = len(arrs)

    def body(*refs):
        ins, outs = refs[:n], refs[2 * n:3 * n]
        send_sems, recv_sems = refs[3 * n:]
        x, y, c = lax.axis_index("x"), lax.axis_index("y"), lax.axis_index("c")
        me, sibling = (x, y, c), (x, y, 1 - c)
        chips = [(1 - x, y), (x, 1 - y), (1 - x, 1 - y)]

        def copy(a, k, block, to, src=None):
            dst = outs[a].at[_slot(*block)]
            return pltpu.make_async_remote_copy(
                src_ref=dst if src is None else src, dst_ref=dst,
                send_sem=send_sems.at[a, k], recv_sem=recv_sems.at[a, k], device_id=to, device_id_type=MESH)

        first = []
        for a in range(n):
            first.append(copy(a, 0, me, sibling, src=ins[a]))
            first += [copy(a, 1 + j, me, (*chip, c), src=ins[a]) for j, chip in enumerate(chips)]
        for cp in first:
            cp.start()
        passed = []
        for j, chip in enumerate(chips):
            for a in range(n):
                copy(a, 1 + j, (*chip, c), me).wait_recv()
                fwd = copy(a, 4 + j, (*chip, c), sibling)
                fwd.start()
                passed.append(fwd)
        for a in range(n):
            copy(a, 0, sibling, me).wait_recv()
        for j, chip in enumerate(chips):
            for a in range(n):
                copy(a, 4 + j, (*chip, 1 - c), me).wait_recv()
        for cp in first + passed:
            cp.wait_send()

    any_spec = pl.BlockSpec(memory_space=pl.ANY)
    return pl.pallas_call(
        body,
        in_specs=[any_spec] * (2 * n), out_specs=[any_spec] * n,
        out_shape=[jax.ShapeDtypeStruct((NDEV,) + a.shape, a.dtype) for a in arrs],
        input_output_aliases={n + i: i for i in range(n)},
        scratch_shapes=[pltpu.SemaphoreType.DMA((n, 7)), pltpu.SemaphoreType.DMA((n, 7))],
        compiler_params=pltpu.CompilerParams(has_side_effects=True), name=name,
    )(*arrs, *[_with_own_block(a, me) for a in arrs])


HBM_SPEC = pl.BlockSpec(memory_space=pltpu.HBM)
SEM_SPEC = pl.BlockSpec(memory_space=pltpu.SEMAPHORE)
ANY_SPEC = pl.BlockSpec(memory_space=pl.ANY)
TOKEN_SHAPE = jax.ShapeDtypeStruct((8, 128), f32)
DATAFLOW_EFFECT = pltpu.SideEffectType.DATAFLOW_SIDE_EFFECTING


def _in_hbm(a):
    return pltpu.with_memory_space_constraint(a, pltpu.HBM)


def _hbm_like(a):
    return pltpu.HBM(a.shape, a.dtype)


def _mesh_pos():
    return lax.axis_index("x"), lax.axis_index("y"), lax.axis_index("c")


def _gather_targets(x, y, c):
    return [(x, y, 1 - c), (1 - x, y, c), (x, 1 - y, c), (1 - x, 1 - y, c)]


def gather_start(shards, me, after, *, name):
    n = len(shards)
    extra = [] if after is None else [after]

    def body(*refs):
        srcs, lands = refs[:n], refs[n:2 * n]
        send_sems, recv_sems = refs[2 * n + len(extra)], refs[2 * n + len(extra) + 1]
        token = refs[-1]
        x, y, c = _mesh_pos()
        me = _slot(x, y, c)
        for a in range(n):
            for k, to in enumerate(_gather_targets(x, y, c)):
                pltpu.make_async_remote_copy(
                    src_ref=srcs[a], dst_ref=lands[a].at[me], send_sem=send_sems.at[4 * a + k], recv_sem=recv_sems.at[4 * a + k],
                    device_id=to, device_id_type=MESH).start()
        token[...] = jnp.zeros(token.shape, f32)

    lands = [_with_own_block(s, me) for s in shards]
    sems = pltpu.SemaphoreType.DMA((4 * n,))
    out = pl.pallas_call(
        body, name=name,
        in_specs=[HBM_SPEC] * (2 * n) + [ANY_SPEC] * len(extra),
        out_specs=[SEM_SPEC, SEM_SPEC] + [HBM_SPEC] * (2 * n) + [pl.BlockSpec(memory_space=pltpu.VMEM)],
        out_shape=[sems, sems] + [_hbm_like(s) for s in shards] + [_hbm_like(l) for l in lands] + [TOKEN_SHAPE],
        input_output_aliases={i: 2 + i for i in range(2 * n)},
        compiler_params=pltpu.CompilerParams(has_side_effects=DATAFLOW_EFFECT),
    )(*[_in_hbm(s) for s in shards], *[_in_hbm(l) for l in lands], *extra)
    return out[0], out[1], out[2:2 + n], out[2 + n:2 + 2 * n], out[-1]


def gather_wait(send_sems, recv_sems, shards, lands, after, *, name):
    n = len(shards)
    after = list(after) if isinstance(after, (list, tuple)) else [after]

    def body(*refs):
        srcs, lands_in = refs[:n], refs[n:2 * n]
        send_sems, recv_sems = refs[2 * n], refs[2 * n + 1]
        x, y, c = _mesh_pos()
        for a in range(n):
            for k, frm in enumerate(_gather_targets(x, y, c)):
                cp = pltpu.make_async_remote_copy(
                    src_ref=srcs[a], dst_ref=lands_in[a].at[_slot(*frm)], send_sem=send_sems.at[4 * a + k],
                    recv_sem=recv_sems.at[4 * a + k], device_id=frm, device_id_type=MESH)
                cp.wait_send()
                cp.wait_recv()

    out = pl.pallas_call(
        body, name=name,
        in_specs=[HBM_SPEC] * (2 * n) + [SEM_SPEC, SEM_SPEC] + [ANY_SPEC] * len(after),
        out_specs=[HBM_SPEC] * (2 * n),
        out_shape=[_hbm_like(s) for s in shards] + [_hbm_like(l) for l in lands],
        input_output_aliases={i: i for i in range(2 * n)},
        compiler_params=pltpu.CompilerParams(has_side_effects=DATAFLOW_EFFECT),
    )(*shards, *lands, send_sems, recv_sems, *after)
    return out[:n], out[n:]


def gather_finish(lands, *, name):
    n = len(lands)

    def body(*refs):
        lands_in, lands_out = refs[:n], refs[n:2 * n]
        send_sems, recv_sems = refs[2 * n:]
        x, y, c = _mesh_pos()
        sibling = (x, y, 1 - c)
        chips = [(1 - x, y), (x, 1 - y), (1 - x, 1 - y)]
        sent = []
        for a in range(n):
            for j, chip in enumerate(chips):
                s = _slot(*chip, c)
                cp = pltpu.make_async_remote_copy(
                    src_ref=lands_in[a].at[s], dst_ref=lands_out[a].at[s], send_sem=send_sems.at[a, j],
                    recv_sem=recv_sems.at[a, j], device_id=sibling, device_id_type=MESH)
                cp.start()
                sent.append(cp)
        for a in range(n):
            for j, chip in enumerate(chips):
                s = _slot(*chip, 1 - c)
                pltpu.make_async_remote_copy(
                    src_ref=lands_in[a].at[s], dst_ref=lands_out[a].at[s], send_sem=send_sems.at[a, j],
                    recv_sem=recv_sems.at[a, j], device_id=sibling, device_id_type=MESH).wait_recv()
        for cp in sent:
            cp.wait_send()

    return pl.pallas_call(
        body, name=name,
        in_specs=[ANY_SPEC] * n, out_specs=[ANY_SPEC] * n,
        out_shape=[jax.ShapeDtypeStruct(l.shape, l.dtype) for l in lands],
        input_output_aliases={i: i for i in range(n)},
        scratch_shapes=[pltpu.SemaphoreType.DMA((n, 3)), pltpu.SemaphoreType.DMA((n, 3))],
        compiler_params=pltpu.CompilerParams(has_side_effects=True),
    )(*lands)


def _peer(x, y, c, r):
    return (x ^ ((r >> 2) & 1), y ^ ((r >> 1) & 1), c ^ (r & 1))


def exchange_start(items, lands, after, *, name):
    n, m = len(items), len(lands)
    extra = [] if after is None else [after]

    def body(*refs):
        gs, zones = refs[:n], refs[n:n + m]
        send_sems, recv_sems = refs[n + m + len(extra)], refs[n + m + len(extra) + 1]
        token = refs[-1]
        x, y, c = _mesh_pos()
        me = _slot(x, y, c)
        for r in (1, 2, 4, 3, 5, 6, 7):
            to = _peer(x, y, c, r)
            for a, (_, zi, l) in enumerate(items):
                pltpu.make_async_remote_copy(
                    src_ref=gs[a].at[_slot(*to)], dst_ref=zones[zi].at[me, l], send_sem=send_sems.at[7 * a + r - 1],
                    recv_sem=recv_sems.at[7 * a + r - 1], device_id=to, device_id_type=MESH).start()
        token[...] = jnp.zeros(token.shape, f32)

    gs = [g for g, _, _ in items]
    sems = pltpu.SemaphoreType.DMA((7 * n,))
    out = pl.pallas_call(
        body, name=name,
        in_specs=[HBM_SPEC] * (n + m) + [ANY_SPEC] * len(extra),
        out_specs=[SEM_SPEC, SEM_SPEC] + [HBM_SPEC] * (n + m) + [pl.BlockSpec(memory_space=pltpu.VMEM)],
        out_shape=[sems, sems] + [_hbm_like(g) for g in gs] + [_hbm_like(z) for z in lands] + [TOKEN_SHAPE],
        input_output_aliases={i: 2 + i for i in range(n + m)},
        compiler_params=pltpu.CompilerParams(has_side_effects=DATAFLOW_EFFECT),
    )(*[_in_hbm(g) for g in gs], *[_in_hbm(z) for z in lands], *extra)
    return out[0], out[1], out[2:2 + n], out[2 + n:2 + n + m], out[-1]


def exchange_wait(parts, lands, after, *, name):
    m = len(lands)
    flat_gs = [g for _, _, _, gs in parts for g in gs]
    ng = len(flat_gs)
    after = list(after) if isinstance(after, (list, tuple)) else [after]

    def body(*refs):
        gs, zones = refs[:ng], refs[ng:ng + m]
        sem_refs = refs[ng + m:ng + m + 2 * len(parts)]
        x, y, c = _mesh_pos()
        base = 0
        for pi, (items, _, _, _) in enumerate(parts):
            send_sems, recv_sems = sem_refs[2 * pi], sem_refs[2 * pi + 1]
            for r in range(1, NDEV):
                frm = _peer(x, y, c, r)
                for a, (_, zi, l) in enumerate(items):
                    cp = pltpu.make_async_remote_copy(
                        src_ref=gs[base + a].at[_slot(*frm)], dst_ref=zones[zi].at[_slot(*frm), l],
                        send_sem=send_sems.at[7 * a + r - 1], recv_sem=recv_sems.at[7 * a + r - 1],
                        device_id=frm, device_id_type=MESH)
                    cp.wait_send()
                    cp.wait_recv()
            base += len(items)

    sem_args = [s for _, ss, rs, _ in parts for s in (ss, rs)]
    out = pl.pallas_call(
        body, name=name,
        in_specs=[HBM_SPEC] * (ng + m) + [SEM_SPEC] * len(sem_args) + [ANY_SPEC] * len(after),
        out_specs=[HBM_SPEC] * (ng + m),
        out_shape=[_hbm_like(g) for g in flat_gs] + [_hbm_like(z) for z in lands],
        input_output_aliases={i: i for i in range(ng + m)},
        compiler_params=pltpu.CompilerParams(has_side_effects=DATAFLOW_EFFECT),
    )(*flat_gs, *lands, *sem_args, *after)
    return out[ng:]


def sum8(g, *, name):
    _, R, C = g.shape
    tr = R

    def body(g_ref, o_ref):
        s = g_ref[0]
        for k in range(1, NDEV):
            s = s + g_ref[k]
        o_ref[...] = s

    return pl.pallas_call(
        body, grid=(R // tr,),
        in_specs=[pl.BlockSpec((NDEV, tr, C), lambda i: (0, i, 0))],
        out_specs=pl.BlockSpec((tr, C), lambda i: (i, 0)),
        out_shape=jax.ShapeDtypeStruct((R, C), f32),
        compiler_params=_cparams(("parallel",)), name=name,
    )(g)


def _adam_math(w, g, m, v):
    m = ADAM_B1 * m + (1.0 - ADAM_B1) * g
    v = ADAM_B2 * v + (1.0 - ADAM_B2) * (g * g)
    m_hat = m / (1.0 - ADAM_B1 ** ADAM_STEP)
    v_hat = v / (1.0 - ADAM_B2 ** ADAM_STEP)
    delta = -ADAM_LR * (m_hat / (jnp.sqrt(v_hat) + ADAM_EPS) + ADAM_WD * w)
    return delta, m, v


def adamw_slices(w, m, v, recv, *, row0=0, col0=0, rrows=None, tr=None, tc=None, l0=0, nl=None, prev=None, name):
    L, R, C = w.shape
    Rp, Cp = recv.shape[2], recv.shape[3]
    nl = L - l0 if nl is None else nl
    tr = R if tr is None else tr
    tc = C if tc is None else tc
    rrows = Rp if rrows is None else rrows
    assert R % tr == 0 and C % tc == 0 and row0 % rrows == 0 and R <= rrows
    assert (tr == R or (Rp == R and row0 == 0)) and (tc == C or (Cp == C and col0 == 0))
    rr = rrows if tr == R else tr
    rc = Cp if tc == C else tc
    rb = row0 // rrows if tr == R else 0
    prevs = [] if prev is None else list(prev)

    def body(w_ref, m_ref, v_ref, r_ref, *rest):
        g_ref, d_ref, nm_ref, nv_ref = rest[len(prevs):]
        c0 = col0 if tc == C else 0
        g = r_ref[0, 0:tr, c0:c0 + tc].astype(f32)
        for k in range(1, NDEV):
            g = g + r_ref[k, 0:tr, c0:c0 + tc].astype(f32)
        delta, nm, nv = _adam_math(w_ref[...], g, m_ref[...], v_ref[...])
        g_ref[...] = g
        d_ref[...] = delta
        nm_ref[...] = nm
        nv_ref[...] = nv

    wspec = pl.BlockSpec((None, tr, tc), lambda l, i, j: (l0 + l, i, j))
    return pl.pallas_call(
        body, grid=(nl, R // tr, C // tc),
        in_specs=[wspec, wspec, wspec, pl.BlockSpec((NDEV, None, rr, rc), lambda l, i, j: (0, l0 + l, rb + i, j))]
        + [pl.BlockSpec(memory_space=pl.ANY)] * len(prevs),
        out_specs=[wspec] * 4, out_shape=[jax.ShapeDtypeStruct((L, R, C), f32)] * 4,
        input_output_aliases={4 + q: q for q in range(len(prevs))},
        compiler_params=_cparams(("parallel", "parallel", "parallel")), name=name,
    )(w, m, v, recv, *prevs)


def adamw_slices_t(w, m, v, recv, *, name):
    L, R, C = w.shape

    def body(w_ref, m_ref, v_ref, r_ref, g_ref, d_ref, nm_ref, nv_ref):
        gt = r_ref[0].astype(f32)
        for k in range(1, NDEV):
            gt = gt + r_ref[k].astype(f32)
        g = gt.T
        delta, nm, nv = _adam_math(w_ref[...], g, m_ref[...], v_ref[...])
        g_ref[...] = g
        d_ref[...] = delta
        nm_ref[...] = nm
        nv_ref[...] = nv

    wspec = pl.BlockSpec((None, R, C), lambda l: (l, 0, 0))
    return pl.pallas_call(
        body, grid=(L,),
        in_specs=[wspec, wspec, wspec, pl.BlockSpec((NDEV, None, C, R), lambda l: (0, l, 0, 0))],
        out_specs=[wspec] * 4, out_shape=[jax.ShapeDtypeStruct((L, R, C), f32)] * 4,
        compiler_params=_cparams(("parallel",)), name=name,
    )(w, m, v, recv)


def adamw_plain(w, m, v, g, *, name):
    R, C = w.shape

    def body(w_ref, m_ref, v_ref, g_ref, d_ref, nm_ref, nv_ref):
        delta, nm, nv = _adam_math(w_ref[...], g_ref[...], m_ref[...], v_ref[...])
        d_ref[...] = delta
        nm_ref[...] = nm
        nv_ref[...] = nv

    return pl.pallas_call(
        body, out_shape=[jax.ShapeDtypeStruct((R, C), f32)] * 3, name=name,
    )(w, m, v, g)


def _chunk_causal_mask():
    pos = jnp.arange(GBLK)
    return (pos[None, :] // CHUNK) <= (pos[:, None] // CHUNK)


def kernel(x, norm_mix_g, norm_ffn_g, final_norm_g, a_w_in, a_v_norm_g, a_w_s, a_b_s, a_w_out, b_w_in, b_conv_w, b_w_out, c_w_in, c_w_grp, c_scale, c_w_out, f_w_gate, f_w_up, f_w_down, loss_target, m_norm_mix_g, m_norm_ffn_g, m_final_norm_g, m_a_w_in, m_a_v_norm_g, m_a_w_s, m_a_b_s, m_a_w_out, m_b_w_in, m_b_conv_w, m_b_w_out, m_c_w_in, m_c_w_grp, m_c_scale, m_c_w_out, m_f_w_gate, m_f_w_up, m_f_w_down, v_norm_mix_g, v_norm_ffn_g, v_final_norm_g, v_a_w_in, v_a_v_norm_g, v_a_w_s, v_a_b_s, v_a_w_out, v_b_w_in, v_b_conv_w, v_b_w_out, v_c_w_in, v_c_w_grp, v_c_scale, v_c_w_out, v_f_w_gate, v_f_w_up, v_f_w_down):
    B, S, _ = x.shape
    T = B * S
    tm = min(512, S)
    tmm = min(1024, T)
    tms = min(512, T)
    tma = min(256, S)
    tmf = min(256, T)
    me = _slot(lax.axis_index("x"), lax.axis_index("y"), lax.axis_index("c"))

    def tb(w):
        return jnp.transpose(w, (0, 2, 1)).astype(bf16)

    small = jnp.concatenate([a_v_norm_g, b_conv_w[0], c_scale, jnp.zeros((2, GBLK), f32)], axis=0)
    started = [gather_start([tb(a_w_in[0:1])[0], a_w_out[0].astype(bf16), small], me, None, name="gather_start_0")]
    tok = started[0][4]
    z = tok[0, 0]
    f_g, f_u, f_d = tb(f_w_gate + z), tb(f_w_up + z), (f_w_down + z).astype(bf16)
    stage_shards = [
        None,
        [f_g[0], f_u[0], f_d[0]],
        [tb(b_w_in + z)[0], (b_w_out[0] + z).astype(bf16), f_g[1], f_u[1], f_d[1]],
        [(c_w_in[0] + z).astype(bf16), (c_w_grp[0] + z).astype(bf16), (c_w_out[0] + z).astype(bf16), f_g[2], f_u[2], f_d[2]],
        [tb(a_w_in[1:2] + z)[0], (a_w_out[1] + z).astype(bf16), f_g[3], f_u[3], f_d[3]],
    ]
    for s in range(1, len(stage_shards)):
        started.append(gather_start(stage_shards[s], me, tok, name=f"gather_start_{s}"))
        tok = started[-1][4]

    def stage_weights(s, after):
        send_sems, recv_sems, shards, lands, _ = started[s]
        _, lands = gather_wait(send_sems, recv_sems, shards, lands, after, name=f"gather_wait_{s}")
        lands = gather_finish(lands, name=f"gather_finish_{s}")
        return [t.reshape(-1, t.shape[-1]) if t.ndim == 3 else t for t in lands]

    wm_masked = jnp.where(_chunk_causal_mask()[None, None], a_w_s, 0.0).astype(bf16)
    bsb = jnp.broadcast_to(a_b_s[:, :, :, None], a_b_s.shape + (GBLK,))

    xc = x.reshape(T, D)
    saved, weights = [], []
    for i in range(DEPTH):
        kind, j = i % 3, i // 3
        gmix = norm_mix_g[i][None]
        if i == 0:
            w_in, w_out, small_g = stage_weights(0, tok)
            small_full = jnp.transpose(small_g.reshape(NDEV, 8, GBLK), (1, 0, 2)).reshape(8, D)
            gv_full, cw_full, cs_full = small_full[0:2], small_full[2:5], small_full[5:6]
            w = [w_in, w_out]
        else:
            w = stage_weights(i + 1, xc)
        if kind == 0:
            w_in, w_out = w[:2]
            p, h = norm_mm(xc, gmix, w_in, tm=tms, transposed=True, name=f"a_in_{i}")
            y = a_mid_fwd(p, gv_full[j][None], wm_masked[j], bsb[j], tm=tma, name=f"a_mid_{i}")
        elif kind == 1:
            w_in, w_out = w[:2]
            p, h = norm_mm(xc, gmix, w_in, tm=tms, transposed=True, name=f"b_in_{i}")
            y = b_mid_fwd(p, cw_full, tm=tm, S=S, name=f"b_mid_{i}")
        else:
            w_in, w_grp, w_out = w[:3]
            p, h = norm_mm(xc, gmix, w_in, tm=tms, transposed=False, name=f"c_in_{i}")
            y = c_mid_fwd(p, w_grp, cs_full, tm=tm, S=S, name=f"c_mid_{i}")
        if i == 0:
            w = w + stage_weights(1, y)
        weights.append(w)
        w_fg, w_fu, w_fd = w[-3:]
        x1, x2, pg, pu, a, h2 = ffn_fwd(y, w_out, xc, norm_ffn_g[i][None], w_fg, w_fu, w_fd, tm=tmf, name=f"f_fwd_{i}")
        saved.append((xc, p, h, y, x1, pg, pu, h2, a))
        xc = x2

    loss_blk, dx, d_final = final_loss(xc, final_norm_g[None], loss_target.reshape(T, D), tm=tms, name="final_loss")
    loss = lax.psum(loss_blk[0, 0], AXES)

    zone_shapes = [(2, 256, D), (2, GBLK, D), (1, 384, D), (1, GBLK, D), (1, GBLK, D), (1, GBLK, C_GDIM), (1, GBLK, D),
                   (DEPTH, F_LOC, D), (DEPTH, F_LOC, D), (DEPTH, F_LOC, D)]
    Z_AIN, Z_AOUT, Z_BIN, Z_BOUT, Z_CIN, Z_CGRP, Z_COUT, Z_FG, Z_FU, Z_FD = range(10)
    lands = [lax.empty((NDEV,) + s, bf16) for s in zone_shapes]
    parts = []

    def exchange(items, after, name):
        nonlocal lands
        for g, zi, l in items:
            own = lax.dynamic_index_in_dim(g, me, axis=0, keepdims=True)[None]
            lands[zi] = lax.dynamic_update_slice(lands[zi], own, (me, l) + (0,) * (g.ndim - 1))
        idx = sorted({zi for _, zi, _ in items})
        local = [(g, idx.index(zi), l) for g, zi, l in items]
        send_sems, recv_sems, gs, zs, token = exchange_start(local, [lands[zi] for zi in idx], after, name=name)
        for q, zi in enumerate(idx):
            lands[zi] = zs[q]
        parts.append(([(None, zi, l) for _, zi, l in items], send_sems, recv_sems, gs))
        return token

    d_mix, d_ffn = [None] * DEPTH, [None] * DEPTH
    d_gv, d_wm, d_bs = [None] * 2, [None] * 2, [None] * 2
    tok = None
    for i in reversed(range(DEPTH)):
        kind, j = i % 3, i // 3
        xin, p, h, y, x1, pg, pu, h2, a = saved[i]
        w = weights[i]
        w_in, w_out = w[0], w[-4]
        w_fg, w_fu, w_fd = w[-3:]
        dpg, dpu, dxb, dx1, d_ffn[i], dy, dx1b = ffn_bwd(dx, w_fd, pg, pu, w_fg, w_fu, x1, norm_ffn_g[i][None], tok, w_out,
                                                         tm=tmf, name=f"f_bwd_{i}")
        gw_g, gw_u, gw_d = ffn_dw(dpg, dpu, a, h2, dxb, name=f"f_dw_{i}")
        items = [(gw_g, Z_FG, i), (gw_u, Z_FU, i), (gw_d, Z_FD, i)]
        gv_i = gv_full[j][None] if kind == 0 else None
        if i == 0:
            tok = exchange(items, tok, "exchange_start_ffn0")
            gv_i = gv_i + tok[0:1, 0:1]
            items = []
        if kind == 0:
            dp, d_gv[j], d_wm[j], d_bs[j] = a_mid_bwd(p, dy, gv_i, wm_masked[j], bsb[j], tm=tma, name=f"a_bwd_mid_{i}")
            items += [(mm_tn(dp, h, name=f"a_dw_in_{i}"), Z_AIN, j),
                      (mm_tn(y, dx1b, name=f"a_dw_out_{i}"), Z_AOUT, j)]
        elif kind == 1:
            dp, d_cw = b_mid_bwd(p, dy, cw_full, tm=tm, S=S, name=f"b_bwd_mid_{i}")
            items += [(mm_tn(dp, h, name=f"b_dw_in_{i}"), Z_BIN, 0),
                      (mm_tn(y, dx1b, name=f"b_dw_out_{i}"), Z_BOUT, 0)]
        else:
            dp, d_cs, d_wgrp = c_mid_bwd(p, dy, w[1], cs_full, tm=tm, S=S, name=f"c_bwd_mid_{i}")
            gw_cgrp = jnp.transpose(d_wgrp.reshape(C_GROUPS, NDEV, C_GDIM // NDEV, C_GDIM), (1, 0, 2, 3)).astype(bf16)
            items += [(mm_tn(h, dp, name=f"c_dw_in_{i}"), Z_CIN, 0),
                      (gw_cgrp.reshape(NDEV, GBLK, C_GDIM), Z_CGRP, 0),
                      (mm_tn(y, dx1b, name=f"c_dw_out_{i}"), Z_COUT, 0)]
        tok = exchange(items, tok, f"exchange_start_{i}")
        dx, d_mix[i] = mm_normbwd([dp], [w_in], xin, norm_mix_g[i][None], dx1, tok, tm=tms, transposed=(kind != 2),
                                  name=f"mix_bwd_in_{i}")
    grad_x = dx.reshape(B, S, D)

    mask = _chunk_causal_mask()
    small_parts = [
        jnp.concatenate(d_mix, axis=0).reshape(-1, GBLK),
        jnp.concatenate(d_ffn, axis=0).reshape(-1, GBLK),
        d_final.reshape(-1, GBLK),
        jnp.concatenate(d_gv, axis=0).reshape(-1, GBLK),
        jnp.where(mask[None, None], jnp.stack(d_wm), 0.0).reshape(-1, GBLK),
        jnp.stack(d_bs).reshape(-1, GBLK),
        d_cw.reshape(-1, GBLK),
        d_cs.reshape(-1, GBLK),
    ]
    small_rows = [q.shape[0] for q in small_parts]
    sg_send, sg_recv, sg_shards, sg_lands, _ = gather_start(
        [jnp.concatenate(small_parts, axis=0)], me, tok, name="gather_small_start")

    def plain(w, m, v, g, name):
        shp = w.shape
        w2, m2, v2, g2 = (t.reshape(-1, shp[-1]) for t in (w, m, v, g))
        return tuple(t.reshape(shp) for t in adamw_plain(w2, m2, v2, g2, name=name))

    def sliced(w, m, v, recv, name, **kw):
        shp = w.shape
        w3, m3, v3 = (t.reshape((shp[0], -1, shp[-1])) for t in (w, m, v))
        return tuple(t.reshape(shp) for t in adamw_slices(w3, m3, v3, recv, name=name, **kw))

    def sliced_t(w, m, v, recv, name, **kw):
        wt, mt, vt = (jnp.transpose(t, (0, 2, 1)) for t in (w, m, v))
        return adamw_slices(wt, mt, vt, recv, name=name, **kw)

    def untransposed(outs):
        return tuple(jnp.transpose(t, (0, 2, 1)) for t in outs)

    lands = list(exchange_wait(parts[:3], lands, dx, name="exchange_wait"))
    _, _, r_bin, r_bout, r_cin, r_cgrp, r_cout, r_fg, r_fu, r_fd = lands
    res = {}
    res["b_w_in"] = adamw_slices_t(b_w_in, m_b_w_in, v_b_w_in, r_bin, name="adam_b_w_in")
    res["b_w_out"] = sliced(b_w_out, m_b_w_out, v_b_w_out, r_bout, "adam_b_w_out")
    res["c_w_in"] = sliced(c_w_in, m_c_w_in, v_c_w_in, r_cin, "adam_c_w_in")
    res["c_w_grp"] = sliced(c_w_grp, m_c_w_grp, v_c_w_grp, r_cgrp, "adam_c_w_grp")
    res["c_w_out"] = sliced(c_w_out, m_c_w_out, v_c_w_out, r_cout, "adam_c_w_out")
    fg = sliced_t(f_w_gate, m_f_w_gate, v_f_w_gate, r_fg, "adam_f_gate_123", tc=256, l0=1)
    fu = sliced_t(f_w_up, m_f_w_up, v_f_w_up, r_fu, "adam_f_up_123", tc=256, l0=1)
    fd = adamw_slices(f_w_down, m_f_w_down, v_f_w_down, r_fd, tc=256, l0=1, name="adam_f_down_123")
    big = [res[n][1] for n in ("b_w_in", "b_w_out", "c_w_in", "c_w_grp", "c_w_out")] + [fg[1], fu[1], fd[1]]

    ffn0_items, ffn0_send, ffn0_recv, ffn0_gs = parts[3]
    ffn0_zones = [Z_FG, Z_FU, Z_FD]
    ffn0_part = ([(None, ffn0_zones.index(zi), l) for _, zi, l in ffn0_items], ffn0_send, ffn0_recv, ffn0_gs)
    r_fg, r_fu, r_fd = exchange_wait([ffn0_part], [lands[zi] for zi in ffn0_zones], big, name="exchange_wait_ffn0")
    fg = sliced_t(f_w_gate, m_f_w_gate, v_f_w_gate, r_fg, "adam_f_gate_0", tc=256, l0=0, nl=1, prev=fg)
    fu = sliced_t(f_w_up, m_f_w_up, v_f_w_up, r_fu, "adam_f_up_0", tc=256, l0=0, nl=1, prev=fu)
    fd = adamw_slices(f_w_down, m_f_w_down, v_f_w_down, r_fd, tc=256, l0=0, nl=1, prev=fd, name="adam_f_down_0")
    res["f_w_gate"], res["f_w_up"], res["f_w_down"] = untransposed(fg), untransposed(fu), tuple(fd)

    last_items, last_send, last_recv, last_gs = parts[4]
    last_zones = [Z_AIN, Z_AOUT]
    last_part = ([(None, last_zones.index(zi), l) for _, zi, l in last_items], last_send, last_recv, last_gs)
    r_ain, r_aout = exchange_wait([last_part], [lands[zi] for zi in last_zones], [fg[1], fu[1], fd[1]], name="exchange_wait_last")
    res["a_w_in"] = adamw_slices_t(a_w_in, m_a_w_in, v_a_w_in, r_ain, name="adam_a_w_in")
    res["a_w_out"] = sliced(a_w_out, m_a_w_out, v_a_w_out, r_aout, "adam_a_w_out")

    _, sg_lands = gather_wait(sg_send, sg_recv, sg_shards, sg_lands, [res["a_w_in"][1], res["a_w_out"][1]], name="gather_small_wait")
    (gs_all,) = gather_finish(sg_lands, name="gather_small_finish")
    gs = sum8(gs_all, name="sum_small_grads")
    offs = [0]
    for r in small_rows:
        offs.append(offs[-1] + r)
    sp = [gs[offs[q]:offs[q + 1]] for q in range(len(small_parts))]
    grad_norm_mix_g = sp[0].reshape(DEPTH, D)
    grad_norm_ffn_g = sp[1].reshape(DEPTH, D)
    grad_final_norm_g = sp[2].reshape(D)
    grad_a_w_s = sp[4].reshape(a_w_s.shape)
    grad_a_b_s = sp[5].reshape(a_b_s.shape)

    def my_cols(full):
        return lax.dynamic_slice_in_dim(full, me * GBLK, GBLK, axis=1)

    grad_a_v_norm_g = my_cols(sp[3].reshape(2, D))
    grad_b_conv_w = my_cols(sp[6].reshape(3, D))[None]
    grad_c_scale = my_cols(sp[7].reshape(1, D))
    res["norm_mix_g"] = (grad_norm_mix_g,) + plain(norm_mix_g, m_norm_mix_g, v_norm_mix_g, grad_norm_mix_g, "adam_norm_mix")
    res["norm_ffn_g"] = (grad_norm_ffn_g,) + plain(norm_ffn_g, m_norm_ffn_g, v_norm_ffn_g, grad_norm_ffn_g, "adam_norm_ffn")
    res["final_norm_g"] = (grad_final_norm_g,) + tuple(
        t.reshape(D) for t in plain(final_norm_g[None], m_final_norm_g[None], v_final_norm_g[None], grad_final_norm_g[None], "adam_final"))
    res["a_v_norm_g"] = (grad_a_v_norm_g,) + plain(a_v_norm_g, m_a_v_norm_g, v_a_v_norm_g, grad_a_v_norm_g, "adam_a_v_norm")
    res["a_w_s"] = (grad_a_w_s,) + plain(a_w_s, m_a_w_s, v_a_w_s, grad_a_w_s, "adam_a_w_s")
    res["a_b_s"] = (grad_a_b_s,) + plain(a_b_s, m_a_b_s, v_a_b_s, grad_a_b_s, "adam_a_b_s")
    res["b_conv_w"] = (grad_b_conv_w,) + plain(b_conv_w, m_b_conv_w, v_b_conv_w, grad_b_conv_w, "adam_b_conv")
    res["c_scale"] = (grad_c_scale,) + plain(c_scale, m_c_scale, v_c_scale, grad_c_scale, "adam_c_scale")

    order = ["norm_mix_g", "norm_ffn_g", "final_norm_g", "a_w_in", "a_v_norm_g", "a_w_s", "a_b_s", "a_w_out", "b_w_in",
             "b_conv_w", "b_w_out", "c_w_in", "c_w_grp", "c_scale", "c_w_out", "f_w_gate", "f_w_up", "f_w_down"]
    return (loss, grad_x, *[res[n][0] for n in order], *[res[n][1] for n in order],
            *[res[n][2] for n in order], *[res[n][3] for n in order])
```

```python
import functools

import jax
import jax.numpy as jnp
from jax import lax
from jax.experimental import pallas as pl
from jax.experimental.pallas import tpu as pltpu

f32 = jnp.float32
bf16 = jnp.bfloat16

NDEV = 8
D = 1024
EPS = 1e-6
CHUNK = 64
GBLK = 128
A_GROUPS = 8
C_GROUPS = 4
C_GDIM = 256
POOL_WINDOWS = (2, 4, 8, 16)
HALO = 16
COLS = 256
F_LOC = 352
MXU_COLS = 256
DEPTH = 4
AXES = ("x", "y", "c")
MESH = pl.DeviceIdType.MESH

ADAM_LR = 0.001
ADAM_B1 = 0.9
ADAM_B2 = 0.999
ADAM_EPS = 1e-08
ADAM_WD = 0.01
ADAM_STEP = 10

VMEM_LIMIT = 56 * 1024 * 1024


def _cparams(sem):
    return pltpu.CompilerParams(dimension_semantics=sem, vmem_limit_bytes=VMEM_LIMIT)


def _gelu(z):
    return 0.5 * z * (1.0 + lax.erf(z * 0.7071067811865476))


def _gelu_cdf(z):
    return 0.5 * (1.0 + lax.erf(z * 0.7071067811865476))


def _gelu_grad(z, cdf):
    return cdf + z * jnp.exp(-0.5 * z * z) * 0.3989422804014327


def _dot(a, b):
    return jnp.dot(a, b, preferred_element_type=f32)


def _dot_nt(a, b):
    return lax.dot_general(a, b, (((1,), (1,)), ((), ())), preferred_element_type=f32)


def _dot_tn(a, b):
    return lax.dot_general(a, b, (((0,), (0,)), ((), ())), preferred_element_type=f32)


def _col_chunks(n, width=1024):
    return [(c, min(c + width, n)) for c in range(0, n, width)]


def norm_mm(x, g, w, *, tm, transposed, name):
    T = x.shape[0]
    n = w.shape[0] if transposed else w.shape[1]

    def body(x_ref, g_ref, w_ref, p_ref, h_ref):
        xv = x_ref[...]
        r = lax.rsqrt(jnp.mean(xv * xv, axis=-1, keepdims=True) + EPS)
        h = (xv * r * g_ref[...]).astype(bf16)
        h_ref[...] = h
        p = _dot_nt(h, w_ref[...]) if transposed else _dot(h, w_ref[...])
        p_ref[...] = p.astype(bf16)

    return pl.pallas_call(
        body, grid=(T // tm,),
        in_specs=[pl.BlockSpec((tm, D), lambda i: (i, 0)), pl.BlockSpec((1, D), lambda i: (0, 0)),
                  pl.BlockSpec(w.shape, lambda i: (0, 0), pipeline_mode=pl.Buffered(1))],
        out_specs=[pl.BlockSpec((tm, n), lambda i: (i, 0)), pl.BlockSpec((tm, D), lambda i: (i, 0))],
        out_shape=[jax.ShapeDtypeStruct((T, n), bf16), jax.ShapeDtypeStruct((T, D), bf16)],
        compiler_params=_cparams(("parallel",)), name=name,
    )(x, g, w)


def ffn_fwd(y, w_out, xin, g, wtg, wtu, wd, *, tm, name):
    T, K = y.shape
    H = wtg.shape[0]

    def body(y_ref, wo_ref, xin_ref, g_ref, wg_ref, wu_ref, wd_ref, x_ref, x2_ref, pg_ref, pu_ref, a_ref, h_ref):
        xv = xin_ref[...] + _dot(y_ref[...], wo_ref[...])
        x_ref[...] = xv
        r = lax.rsqrt(jnp.mean(xv * xv, axis=-1, keepdims=True) + EPS)
        h = (xv * r * g_ref[...]).astype(bf16)
        h_ref[...] = h
        out = None
        for c0, c1 in _col_chunks(H):
            gate = _dot_nt(h, wg_ref[c0:c1, :])
            up = _dot_nt(h, wu_ref[c0:c1, :])
            pg_ref[:, c0:c1] = gate.astype(bf16)
            pu_ref[:, c0:c1] = up.astype(bf16)
            ac = (gate * jax.nn.sigmoid(gate) * up).astype(bf16)
            a_ref[:, c0:c1] = ac
            part = _dot(ac, wd_ref[c0:c1, :])
            out = part if out is None else out + part
        x2_ref[...] = xv + out

    wspec = pl.BlockSpec((H, D), lambda i: (0, 0), pipeline_mode=pl.Buffered(1))
    hspec = pl.BlockSpec((tm, H), lambda i: (i, 0))
    dspec = pl.BlockSpec((tm, D), lambda i: (i, 0))
    return pl.pallas_call(
        body, grid=(T // tm,),
        in_specs=[pl.BlockSpec((tm, K), lambda i: (i, 0)),
                  pl.BlockSpec((K, D), lambda i: (0, 0), pipeline_mode=pl.Buffered(1)),
                  dspec, pl.BlockSpec((1, D), lambda i: (0, 0)), wspec, wspec, wspec],
        out_specs=[dspec, dspec, hspec, hspec, hspec, dspec],
        out_shape=[jax.ShapeDtypeStruct((T, D), f32)] * 2 + [jax.ShapeDtypeStruct((T, H), bf16)] * 3
        + [jax.ShapeDtypeStruct((T, D), bf16)],
        compiler_params=_cparams(("parallel",)), name=name,
    )(y, w_out, xin, g, wtg, wtu, wd)


def ffn_bwd(dx, wd, pg, pu, wtg, wtu, x, g, tok, w_next, *, tm, name):
    T = dx.shape[0]
    H = wd.shape[0]
    kn = w_next.shape[0]
    toks = [] if tok is None else [tok]

    def body(dx_ref, wd_ref, pg_ref, pu_ref, wg_ref, wu_ref, x_ref, g_ref, *rest):
        wn_ref = rest[len(toks)]
        dg_ref, du_ref, dxb_ref, dx1_ref, dgn_ref, dy_ref, dx1b_ref = rest[len(toks) + 1:]

        @pl.when(pl.program_id(0) == 0)
        def _():
            dgn_ref[...] = jnp.zeros(dgn_ref.shape, f32)

        dxv = dx_ref[...]
        dxb = dxv.astype(bf16)
        dxb_ref[...] = dxb
        dh = None
        for c0, c1 in _col_chunks(H):
            da = _dot_nt(dxb, wd_ref[c0:c1, :])
            gate = pg_ref[:, c0:c1].astype(f32)
            up = pu_ref[:, c0:c1].astype(f32)
            sg = jax.nn.sigmoid(gate)
            dgc = (da * up * (sg * (1.0 + gate * (1.0 - sg)))).astype(bf16)
            duc = (da * gate * sg).astype(bf16)
            dg_ref[:, c0:c1] = dgc
            du_ref[:, c0:c1] = duc
            part = _dot(dgc, wg_ref[c0:c1, :]) + _dot(duc, wu_ref[c0:c1, :])
            dh = part if dh is None else dh + part
        gv = g_ref[...] + rest[0][0:1, 0:1] if toks else g_ref[...]
        xv = x_ref[...]
        r = lax.rsqrt(jnp.mean(xv * xv, axis=-1, keepdims=True) + EPS)
        xhat = xv * r
        dgn_ref[...] += jnp.sum(dh * xhat, axis=0, keepdims=True)
        dxhat = dh * gv
        dx1 = dxv + r * (dxhat - xhat * jnp.mean(dxhat * xhat, axis=-1, keepdims=True))
        dx1_ref[...] = dx1
        dx1b = dx1.astype(bf16)
        dx1b_ref[...] = dx1b
        dy_ref[...] = _dot_nt(dx1b, wn_ref[...]).astype(bf16)

    dspec = pl.BlockSpec((tm, D), lambda i: (i, 0))
    hspec = pl.BlockSpec((tm, H), lambda i: (i, 0))
    wspec = pl.BlockSpec((H, D), lambda i: (0, 0), pipeline_mode=pl.Buffered(1))
    gspec = pl.BlockSpec((1, D), lambda i: (0, 0))
    return pl.pallas_call(
        body, grid=(T // tm,),
        in_specs=[dspec, wspec, hspec, hspec, wspec, wspec, dspec, gspec]
        + [pl.BlockSpec((8, 128), lambda i: (0, 0))] * len(toks)
        + [pl.BlockSpec((kn, D), lambda i: (0, 0), pipeline_mode=pl.Buffered(1))],
        out_specs=[hspec, hspec, dspec, dspec, gspec, pl.BlockSpec((tm, kn), lambda i: (i, 0)), dspec],
        out_shape=[jax.ShapeDtypeStruct((T, H), bf16)] * 2 + [jax.ShapeDtypeStruct((T, D), bf16),
                   jax.ShapeDtypeStruct((T, D), f32), jax.ShapeDtypeStruct((1, D), f32),
                   jax.ShapeDtypeStruct((T, kn), bf16), jax.ShapeDtypeStruct((T, D), bf16)],
        compiler_params=_cparams(("arbitrary",)), name=name,
    )(dx, wd, pg, pu, wtg, wtu, x, g, *toks, w_next)


def mm_normbwd(dps, ws, x, g, dres, tok, *, tm, transposed, name):
    T = x.shape[0]
    nq = len(dps)
    toks = [] if tok is None else [tok]
    n_in = 2 * nq + 3 + len(toks)

    def body(*refs):
        dp_refs, w_refs = refs[:nq], refs[nq:2 * nq]
        x_ref, g_ref, dres_ref = refs[2 * nq:2 * nq + 3]
        dx_ref, dg_ref = refs[n_in:n_in + 2]

        @pl.when(pl.program_id(0) == 0)
        def _():
            dg_ref[...] = jnp.zeros(dg_ref.shape, f32)

        dh = None
        for q in range(nq):
            if transposed:
                part = _dot(dp_refs[q][...], w_refs[q][...])
            else:
                part = _dot_nt(dp_refs[q][...], w_refs[q][...])
            dh = part if dh is None else dh + part
        gv = g_ref[...] + refs[2 * nq + 3][0:1, 0:1] if toks else g_ref[...]
        xv = x_ref[...]
        r = lax.rsqrt(jnp.mean(xv * xv, axis=-1, keepdims=True) + EPS)
        xhat = xv * r
        dg_ref[...] += jnp.sum(dh * xhat, axis=0, keepdims=True)
        dxhat = dh * gv
        dx_ref[...] = dres_ref[...] + r * (dxhat - xhat * jnp.mean(dxhat * xhat, axis=-1, keepdims=True))

    dspec = pl.BlockSpec((tm, D), lambda i: (i, 0))
    return pl.pallas_call(
        body, grid=(T // tm,),
        in_specs=[pl.BlockSpec((tm, dp.shape[1]), lambda i: (i, 0)) for dp in dps]
        + [pl.BlockSpec(w.shape, lambda i: (0, 0), pipeline_mode=pl.Buffered(1)) for w in ws]
        + [dspec, pl.BlockSpec((1, D), lambda i: (0, 0)), dspec] + [pl.BlockSpec((8, 128), lambda i: (0, 0))] * len(toks),
        out_specs=[dspec, pl.BlockSpec((1, D), lambda i: (0, 0))],
        out_shape=[jax.ShapeDtypeStruct((T, D), f32), jax.ShapeDtypeStruct((1, D), f32)],
        compiler_params=_cparams(("arbitrary",)), name=name,
    )(*dps, *ws, x, g, dres, *toks)


def mm_tn(a, b, *, name):
    T, K = a.shape
    n = b.shape[1]

    def body(a_ref, b_ref, o_ref):
        o_ref[...] = _dot_tn(a_ref[...], b_ref[...]).astype(bf16)

    out = pl.pallas_call(
        body, grid=(K // MXU_COLS,),
        in_specs=[pl.BlockSpec((T, MXU_COLS), lambda d: (0, d)),
                  pl.BlockSpec((T, n), lambda d: (0, 0), pipeline_mode=pl.Buffered(1))],
        out_specs=pl.BlockSpec((MXU_COLS, n), lambda d: (d, 0)),
        out_shape=jax.ShapeDtypeStruct((K, n), bf16),
        compiler_params=_cparams(("parallel",)), name=name,
    )(a, b)
    return out.reshape(NDEV, K // NDEV, n)


def ffn_dw(dpg, dpu, a, h2, dxb, *, name):
    T, H = dpg.shape

    def body(g_ref, u_ref, a_ref, h_ref, x_ref, og_ref, ou_ref, od_ref):
        h = h_ref[...]
        og_ref[...] = _dot_tn(g_ref[...], h).astype(bf16)
        ou_ref[...] = _dot_tn(u_ref[...], h).astype(bf16)
        od_ref[...] = _dot_tn(a_ref[...], x_ref[...]).astype(bf16)

    blk = pl.BlockSpec((T, MXU_COLS), lambda d: (0, d))
    full = pl.BlockSpec((T, D), lambda d: (0, 0), pipeline_mode=pl.Buffered(1))
    out = pl.BlockSpec((MXU_COLS, D), lambda d: (d, 0))
    outs = pl.pallas_call(
        body, grid=(H // MXU_COLS,),
        in_specs=[blk, blk, blk, full, full], out_specs=[out] * 3,
        out_shape=[jax.ShapeDtypeStruct((H, D), bf16)] * 3,
        compiler_params=_cparams(("parallel",)), name=name,
    )(dpg, dpu, a, h2, dxb)
    return [o.reshape(NDEV, H // NDEV, D) for o in outs]


def _prev_halo_spec(tm, ncol):
    return pl.BlockSpec((HALO, ncol), lambda i: (jnp.maximum(i * (tm // HALO) - 1, 0), 0))


def _next_halo_spec(tm, ncol, T):
    return pl.BlockSpec((HALO, ncol), lambda i: (jnp.minimum((i + 1) * (tm // HALO), T // HALO - 1), 0))


def a_mid_fwd(p, gv, wm, bsb, *, tm, name):
    T = p.shape[0]

    def body(p_ref, gv_ref, wm_ref, bs_ref, y_ref, vn_sc):
        v = _gelu(p_ref[:, D:].astype(f32))
        vc = v - jnp.mean(v, axis=-1, keepdims=True)
        var = jnp.mean(vc * vc, axis=-1, keepdims=True)
        vn_sc[...] = (vc * lax.rsqrt(var + EPS) * gv_ref[...]).astype(bf16)
        for g in range(A_GROUPS):
            cs = slice(g * GBLK, (g + 1) * GBLK)
            for r in range(tm // GBLK):
                rs = slice(r * GBLK, (r + 1) * GBLK)
                sv = _dot(wm_ref[g], vn_sc[rs, cs]) + bs_ref[g]
                y_ref[rs, cs] = (_gelu(p_ref[rs, cs].astype(f32)) * sv).astype(bf16)

    return pl.pallas_call(
        body, grid=(T // tm,),
        in_specs=[pl.BlockSpec((tm, 2 * D), lambda i: (i, 0)), pl.BlockSpec((1, D), lambda i: (0, 0)),
                  pl.BlockSpec((A_GROUPS, GBLK, GBLK), lambda i: (0, 0, 0)),
                  pl.BlockSpec((A_GROUPS, GBLK, GBLK), lambda i: (0, 0, 0))],
        out_specs=pl.BlockSpec((tm, D), lambda i: (i, 0)),
        out_shape=jax.ShapeDtypeStruct((T, D), bf16),
        scratch_shapes=[pltpu.VMEM((tm, D), bf16)],
        compiler_params=_cparams(("parallel",)), name=name,
    )(p, gv, wm, bsb)


def a_mid_bwd(p, dy, gv, wm, bsb, *, tm, name):
    T = p.shape[0]

    def body(p_ref, dy_ref, gv_ref, wm_ref, bs_ref, dp_ref, dgv_ref, dwm_ref, dbs_ref, vn_sc, dvn_sc):
        @pl.when(pl.program_id(0) == 0)
        def _():
            dgv_ref[...] = jnp.zeros(dgv_ref.shape, f32)
            dwm_ref[...] = jnp.zeros(dwm_ref.shape, f32)
            dbs_ref[...] = jnp.zeros(dbs_ref.shape, f32)

        zv = p_ref[:, D:].astype(f32)
        cdf_v = _gelu_cdf(zv)
        v = zv * cdf_v
        vc = v - jnp.mean(v, axis=-1, keepdims=True)
        rstd = lax.rsqrt(jnp.mean(vc * vc, axis=-1, keepdims=True) + EPS)
        vhat = vc * rstd
        vn_sc[...] = (vhat * gv_ref[...]).astype(bf16)
        for g in range(A_GROUPS):
            cs = slice(g * GBLK, (g + 1) * GBLK)
            dwm = jnp.zeros((GBLK, GBLK), f32)
            dbs = jnp.zeros((GBLK, 1), f32)
            for r in range(tm // GBLK):
                rs = slice(r * GBLK, (r + 1) * GBLK)
                zu = p_ref[rs, cs].astype(f32)
                cdf_u = _gelu_cdf(zu)
                vn = vn_sc[rs, cs]
                sv = _dot(wm_ref[g], vn) + bs_ref[g]
                dyb = dy_ref[rs, cs].astype(f32)
                dsv = dyb * (zu * cdf_u)
                dsvb = dsv.astype(bf16)
                dp_ref[rs, cs] = (dyb * sv * _gelu_grad(zu, cdf_u)).astype(bf16)
                dwm += _dot_nt(dsvb, vn)
                dbs += jnp.sum(dsv, axis=1, keepdims=True)
                dvn_sc[rs, cs] = _dot_tn(wm_ref[g], dsvb)
            dwm_ref[g] += dwm
            dbs_ref[g] += dbs
        dvn = dvn_sc[...]
        dgv_ref[...] += jnp.sum(dvn * vhat, axis=0, keepdims=True)
        dvhat = dvn * gv_ref[...]
        dv = rstd * (dvhat - jnp.mean(dvhat, axis=-1, keepdims=True)
                     - vhat * jnp.mean(dvhat * vhat, axis=-1, keepdims=True))
        dp_ref[:, D:] = (dv * _gelu_grad(zv, cdf_v)).astype(bf16)

    return pl.pallas_call(
        body, grid=(T // tm,),
        in_specs=[pl.BlockSpec((tm, 2 * D), lambda i: (i, 0)), pl.BlockSpec((tm, D), lambda i: (i, 0)),
                  pl.BlockSpec((1, D), lambda i: (0, 0)),
                  pl.BlockSpec((A_GROUPS, GBLK, GBLK), lambda i: (0, 0, 0)),
                  pl.BlockSpec((A_GROUPS, GBLK, GBLK), lambda i: (0, 0, 0))],
        out_specs=[pl.BlockSpec((tm, 2 * D), lambda i: (i, 0)), pl.BlockSpec((1, D), lambda i: (0, 0)),
                   pl.BlockSpec((A_GROUPS, GBLK, GBLK), lambda i: (0, 0, 0)),
                   pl.BlockSpec((A_GROUPS, GBLK, 1), lambda i: (0, 0, 0))],
        out_shape=[jax.ShapeDtypeStruct((T, 2 * D), bf16), jax.ShapeDtypeStruct((1, D), f32),
                   jax.ShapeDtypeStruct((A_GROUPS, GBLK, GBLK), f32), jax.ShapeDtypeStruct((A_GROUPS, GBLK, 1), f32)],
        scratch_shapes=[pltpu.VMEM((tm, D), bf16), pltpu.VMEM((tm, D), f32)],
        compiler_params=_cparams(("arbitrary",)), name=name,
    )(p, dy, gv, wm, bsb)


def b_mid_fwd(p, cw, *, tm, S, name):
    T = p.shape[0]
    nst = S // tm

    def body(p_ref, ph_ref, cw_ref, y_ref, ext):
        first = (pl.program_id(0) % nst) == 0
        for c0 in range(0, D, COLS):
            cs, cc, cx = slice(c0, c0 + COLS), slice(D + c0, D + c0 + COLS), slice(2 * D + c0, 2 * D + c0 + COLS)
            q = p_ref[:, cc].astype(f32) * p_ref[:, cx].astype(f32)
            qh = ph_ref[:, cc].astype(f32) * ph_ref[:, cx].astype(f32)
            ext[0:HALO, cs] = jnp.where(first, 0.0, qh)
            ext[HALO:, cs] = q
            y = (cw_ref[0:1, cs] * ext[pl.ds(HALO - 2, tm), cs] + cw_ref[1:2, cs] * ext[pl.ds(HALO - 1, tm), cs]
                 + cw_ref[2:3, cs] * q)
            y_ref[:, cs] = (p_ref[:, cs].astype(f32) * y).astype(bf16)

    return pl.pallas_call(
        body, grid=(T // tm,),
        in_specs=[pl.BlockSpec((tm, 3 * D), lambda i: (i, 0)), _prev_halo_spec(tm, 3 * D),
                  pl.BlockSpec((3, D), lambda i: (0, 0))],
        out_specs=pl.BlockSpec((tm, D), lambda i: (i, 0)),
        out_shape=jax.ShapeDtypeStruct((T, D), bf16),
        scratch_shapes=[pltpu.VMEM((tm + HALO, D), f32)],
        compiler_params=_cparams(("parallel",)), name=name,
    )(p, p, cw)


def b_mid_bwd(p, dy, cw, *, tm, S, name):
    T = p.shape[0]
    nst = S // tm

    def body(p_ref, ph_ref, pn_ref, dy_ref, dyn_ref, cw_ref, dp_ref, dcw_ref, ext, ext2):
        i = pl.program_id(0)
        first = (i % nst) == 0
        last = (i % nst) == nst - 1

        @pl.when(i == 0)
        def _():
            dcw_ref[...] = jnp.zeros(dcw_ref.shape, f32)

        for c0 in range(0, D, COLS):
            cs, cc, cx = slice(c0, c0 + COLS), slice(D + c0, D + c0 + COLS), slice(2 * D + c0, 2 * D + c0 + COLS)
            gb, gc, xt = p_ref[:, cs].astype(f32), p_ref[:, cc].astype(f32), p_ref[:, cx].astype(f32)
            q = gc * xt
            ext[0:HALO, cs] = jnp.where(first, 0.0, ph_ref[:, cc].astype(f32) * ph_ref[:, cx].astype(f32))
            ext[HALO:, cs] = q
            q2 = ext[pl.ds(HALO - 2, tm), cs]
            q1 = ext[pl.ds(HALO - 1, tm), cs]
            y = cw_ref[0:1, cs] * q2 + cw_ref[1:2, cs] * q1 + cw_ref[2:3, cs] * q
            dyo = dy_ref[:, cs].astype(f32)
            dp_ref[:, cs] = (dyo * y).astype(bf16)
            dyc = dyo * gb
            ext2[0:tm, cs] = dyc
            ext2[tm:, cs] = jnp.where(last, 0.0, dyn_ref[:, cs].astype(f32) * pn_ref[:, cs].astype(f32))
            dq = cw_ref[2:3, cs] * dyc + cw_ref[1:2, cs] * ext2[pl.ds(1, tm), cs] + cw_ref[0:1, cs] * ext2[pl.ds(2, tm), cs]
            dp_ref[:, cc] = (dq * xt).astype(bf16)
            dp_ref[:, cx] = (dq * gc).astype(bf16)
            dcw_ref[0:1, cs] += jnp.sum(dyc * q2, axis=0, keepdims=True)
            dcw_ref[1:2, cs] += jnp.sum(dyc * q1, axis=0, keepdims=True)
            dcw_ref[2:3, cs] += jnp.sum(dyc * q, axis=0, keepdims=True)

    return pl.pallas_call(
        body, grid=(T // tm,),
        in_specs=[pl.BlockSpec((tm, 3 * D), lambda i: (i, 0)), _prev_halo_spec(tm, 3 * D), _next_halo_spec(tm, 3 * D, T),
                  pl.BlockSpec((tm, D), lambda i: (i, 0)), _next_halo_spec(tm, D, T),
                  pl.BlockSpec((3, D), lambda i: (0, 0))],
        out_specs=[pl.BlockSpec((tm, 3 * D), lambda i: (i, 0)), pl.BlockSpec((3, D), lambda i: (0, 0))],
        out_shape=[jax.ShapeDtypeStruct((T, 3 * D), bf16), jax.ShapeDtypeStruct((3, D), f32)],
        scratch_shapes=[pltpu.VMEM((tm + HALO, D), f32), pltpu.VMEM((tm + HALO, D), f32)],
        compiler_params=_cparams(("arbitrary",)), name=name,
    )(p, p, p, dy, dy, cw)


def _pool_counts(i, nst, tm, rows, row0, w):
    t = (i % nst) * tm + row0 + lax.broadcasted_iota(jnp.int32, (rows, 1), 0)
    return jnp.minimum(t + 1, w).astype(f32)


def _pool_diff(p_ref, ext, g, i, nst, tm):
    w = POOL_WINDOWS[g]
    cs = slice(g * C_GDIM, (g + 1) * C_GDIM)
    pg = p_ref[:, cs].astype(f32)
    s = pg
    for k in range(1, w):
        s = s + ext[pl.ds(HALO - k, tm), cs]
    return s / _pool_counts(i, nst, tm, tm, 0, w) - pg


def c_mid_fwd(p, wg, scale, *, tm, S, name):
    T = p.shape[0]
    nst = S // tm

    def body(p_ref, ph_ref, wg_ref, sc_ref, y_ref, ext):
        i = pl.program_id(0)
        first = (i % nst) == 0
        ext[0:HALO, :] = jnp.where(first, 0.0, ph_ref[...].astype(f32))
        ext[HALO:, :] = p_ref[...].astype(f32)
        for g in range(C_GROUPS):
            cs = slice(g * C_GDIM, (g + 1) * C_GDIM)
            dg = _pool_diff(p_ref, ext, g, i, nst, tm).astype(bf16)
            wv = wg_ref[:, g].reshape(C_GDIM, C_GDIM)
            y_ref[:, cs] = (_dot(dg, wv) * sc_ref[:, cs]).astype(bf16)

    return pl.pallas_call(
        body, grid=(T // tm,),
        in_specs=[pl.BlockSpec((tm, D), lambda i: (i, 0)), _prev_halo_spec(tm, D),
                  pl.BlockSpec((NDEV, C_GROUPS, C_GDIM // NDEV, C_GDIM), lambda i: (0, 0, 0, 0)),
                  pl.BlockSpec((1, D), lambda i: (0, 0))],
        out_specs=pl.BlockSpec((tm, D), lambda i: (i, 0)),
        out_shape=jax.ShapeDtypeStruct((T, D), bf16),
        scratch_shapes=[pltpu.VMEM((tm + HALO, D), f32)],
        compiler_params=_cparams(("parallel",)), name=name,
    )(p, p, wg, scale)


def c_mid_bwd(p, dy, wg, scale, *, tm, S, name):
    T = p.shape[0]
    nst = S // tm

    def body(p_ref, ph_ref, dy_ref, dyn_ref, wg_ref, sc_ref, dp_ref, dsc_ref, dwg_ref, ext, ext2):
        i = pl.program_id(0)
        first = (i % nst) == 0
        last = (i % nst) == nst - 1

        @pl.when(i == 0)
        def _():
            dsc_ref[...] = jnp.zeros(dsc_ref.shape, f32)
            dwg_ref[...] = jnp.zeros(dwg_ref.shape, f32)

        ext[0:HALO, :] = jnp.where(first, 0.0, ph_ref[...].astype(f32))
        ext[HALO:, :] = p_ref[...].astype(f32)
        for g in range(C_GROUPS):
            w = POOL_WINDOWS[g]
            cs = slice(g * C_GDIM, (g + 1) * C_GDIM)
            dg = _pool_diff(p_ref, ext, g, i, nst, tm).astype(bf16)
            wv = wg_ref[:, g].reshape(C_GDIM, C_GDIM)
            dyo = dy_ref[:, cs].astype(f32)
            dsc_ref[:, cs] += jnp.sum(dyo * _dot(dg, wv), axis=0, keepdims=True)
            dyp = (dyo * sc_ref[:, cs]).astype(bf16)
            dypn = (dyn_ref[:, cs].astype(f32) * sc_ref[:, cs]).astype(bf16)
            dwg_ref[g] += _dot_tn(dg, dyp)
            dd = _dot_nt(dyp, wv)
            ddn = _dot_nt(dypn, wv)
            ext2[0:tm, cs] = dd / _pool_counts(i, nst, tm, tm, 0, w)
            ext2[tm:, cs] = jnp.where(last, 0.0, ddn / _pool_counts(i, nst, tm, HALO, tm, w))
            s = -dd
            for k in range(w):
                s = s + ext2[pl.ds(k, tm), cs]
            dp_ref[:, cs] = s.astype(bf16)

    return pl.pallas_call(
        body, grid=(T // tm,),
        in_specs=[pl.BlockSpec((tm, D), lambda i: (i, 0)), _prev_halo_spec(tm, D),
                  pl.BlockSpec((tm, D), lambda i: (i, 0)), _next_halo_spec(tm, D, T),
                  pl.BlockSpec((NDEV, C_GROUPS, C_GDIM // NDEV, C_GDIM), lambda i: (0, 0, 0, 0)),
                  pl.BlockSpec((1, D), lambda i: (0, 0))],
        out_specs=[pl.BlockSpec((tm, D), lambda i: (i, 0)), pl.BlockSpec((1, D), lambda i: (0, 0)),
                   pl.BlockSpec((C_GROUPS, C_GDIM, C_GDIM), lambda i: (0, 0, 0))],
        out_shape=[jax.ShapeDtypeStruct((T, D), bf16), jax.ShapeDtypeStruct((1, D), f32),
                   jax.ShapeDtypeStruct((C_GROUPS, C_GDIM, C_GDIM), f32)],
        scratch_shapes=[pltpu.VMEM((tm + HALO, D), f32), pltpu.VMEM((tm + HALO, D), f32)],
        compiler_params=_cparams(("arbitrary",)), name=name,
    )(p, p, dy, dy, wg, scale)


def final_loss(x, g, tgt, *, tm, name):
    T = x.shape[0]

    def body(x_ref, g_ref, t_ref, loss_ref, dx_ref, dg_ref):
        @pl.when(pl.program_id(0) == 0)
        def _():
            loss_ref[...] = jnp.zeros(loss_ref.shape, f32)
            dg_ref[...] = jnp.zeros(dg_ref.shape, f32)

        xv = x_ref[...]
        r = lax.rsqrt(jnp.mean(xv * xv, axis=-1, keepdims=True) + EPS)
        xhat = xv * r
        err = xhat * g_ref[...] - t_ref[...]
        loss_ref[...] += 0.5 * jnp.sum(jnp.mean(err * err, axis=-1, keepdims=True))
        dy = err * (1.0 / D)
        dg_ref[...] += jnp.sum(dy * xhat, axis=0, keepdims=True)
        dxhat = dy * g_ref[...]
        dx_ref[...] = r * (dxhat - xhat * jnp.mean(dxhat * xhat, axis=-1, keepdims=True))

    return pl.pallas_call(
        body, grid=(T // tm,),
        in_specs=[pl.BlockSpec((tm, D), lambda i: (i, 0)), pl.BlockSpec((1, D), lambda i: (0, 0)),
                  pl.BlockSpec((tm, D), lambda i: (i, 0))],
        out_specs=[pl.BlockSpec((8, 128), lambda i: (0, 0)), pl.BlockSpec((tm, D), lambda i: (i, 0)),
                   pl.BlockSpec((1, D), lambda i: (0, 0))],
        out_shape=[jax.ShapeDtypeStruct((8, 128), f32), jax.ShapeDtypeStruct((T, D), f32),
                   jax.ShapeDtypeStruct((1, D), f32)],
        compiler_params=_cparams(("arbitrary",)), name=name,
    )(x, g, tgt)


def _slot(px, py, pc):
    return 4 * px + 2 * py + pc


def _with_own_block(s, me):
    zone = lax.empty((NDEV,) + s.shape, s.dtype)
    return lax.dynamic_update_slice(zone, s[None], (me,) + (0,) * s.ndim)


def all_gather(arrs, me, *, name):
    n = len(arrs)

    def body(*refs):
        ins, outs = refs[:n], refs[2 * n:3 * n]
        send_sems, recv_sems = refs[3 * n:]
        x, y, c = lax.axis_index("x"), lax.axis_index("y"), lax.axis_index("c")
        me, sibling = (x, y, c), (x, y, 1 - c)
        chips = [(1 - x, y), (x, 1 - y), (1 - x, 1 - y)]

        def copy(a, k, block, to, src=None):
            dst = outs[a].at[_slot(*block)]
            return pltpu.make_async_remote_copy(
                src_ref=dst if src is None else src, dst_ref=dst,
                send_sem=send_sems.at[a, k], recv_sem=recv_sems.at[a, k], device_id=to, device_id_type=MESH)

        first = []
        for a in range(n):
            first.append(copy(a, 0, me, sibling, src=ins[a]))
            first += [copy(a, 1 + j, me, (*chip, c), src=ins[a]) for j, chip in enumerate(chips)]
        for cp in first:
            cp.start()
        passed = []
        for j, chip in enumerate(chips):
            for a in range(n):
                copy(a, 1 + j, (*chip, c), me).wait_recv()
                fwd = copy(a, 4 + j, (*chip, c), sibling)
                fwd.start()
                passed.append(fwd)
        for a in range(n):
            copy(a, 0, sibling, me).wait_recv()
        for j, chip in enumerate(chips):
            for a in range(n):
                copy(a, 4 + j, (*chip, 1 - c), me).wait_recv()
        for cp in first + passed:
            cp.wait_send()

    any_spec = pl.BlockSpec(memory_space=pl.ANY)
    return pl.pallas_call(
        body,
        in_specs=[any_spec] * (2 * n), out_specs=[any_spec] * n,
        out_shape=[jax.ShapeDtypeStruct((NDEV,) + a.shape, a.dtype) for a in arrs],
        input_output_aliases={n + i: i for i in range(n)},
        scratch_shapes=[pltpu.SemaphoreType.DMA((n, 7)), pltpu.SemaphoreType.DMA((n, 7))],
        compiler_params=pltpu.CompilerParams(has_side_effects=True), name=name,
    )(*arrs, *[_with_own_block(a, me) for a in arrs])


HBM_SPEC = pl.BlockSpec(memory_space=pltpu.HBM)
SEM_SPEC = pl.BlockSpec(memory_space=pltpu.SEMAPHORE)
ANY_SPEC = pl.BlockSpec(memory_space=pl.ANY)
TOKEN_SHAPE = jax.ShapeDtypeStruct((8, 128), f32)
DATAFLOW_EFFECT = pltpu.SideEffectType.DATAFLOW_SIDE_EFFECTING


def _in_hbm(a):
    return pltpu.with_memory_space_constraint(a, pltpu.HBM)


def _hbm_like(a):
    return pltpu.HBM(a.shape, a.dtype)


def _mesh_pos():
    return lax.axis_index("x"), lax.axis_index("y"), lax.axis_index("c")


def _gather_targets(x, y, c):
    return [(x, y, 1 - c), (1 - x, y, c), (x, 1 - y, c), (1 - x, 1 - y, c)]


def gather_start(shards, me, after, *, name):
    n = len(shards)
    extra = [] if after is None else [after]

    def body(*refs):
        srcs, lands = refs[:n], refs[n:2 * n]
        send_sems, recv_sems = refs[2 * n + len(extra)], refs[2 * n + len(extra) + 1]
        token = refs[-1]
        x, y, c = _mesh_pos()
        me = _slot(x, y, c)
        for a in range(n):
            for k, to in enumerate(_gather_targets(x, y, c)):
                pltpu.make_async_remote_copy(
                    src_ref=srcs[a], dst_ref=lands[a].at[me], send_sem=send_sems.at[4 * a + k], recv_sem=recv_sems.at[4 * a + k],
                    device_id=to, device_id_type=MESH).start()
        token[...] = jnp.zeros(token.shape, f32)

    lands = [_with_own_block(s, me) for s in shards]
    sems = pltpu.SemaphoreType.DMA((4 * n,))
    out = pl.pallas_call(
        body, name=name,
        in_specs=[HBM_SPEC] * (2 * n) + [ANY_SPEC] * len(extra),
        out_specs=[SEM_SPEC, SEM_SPEC] + [HBM_SPEC] * (2 * n) + [pl.BlockSpec(memory_space=pltpu.VMEM)],
        out_shape=[sems, sems] + [_hbm_like(s) for s in shards] + [_hbm_like(l) for l in lands] + [TOKEN_SHAPE],
        input_output_aliases={i: 2 + i for i in range(2 * n)},
        compiler_params=pltpu.CompilerParams(has_side_effects=DATAFLOW_EFFECT),
    )(*[_in_hbm(s) for s in shards], *[_in_hbm(l) for l in lands], *extra)
    return out[0], out[1], out[2:2 + n], out[2 + n:2 + 2 * n], out[-1]


def gather_wait(send_sems, recv_sems, shards, lands, after, *, name):
    n = len(shards)
    after = list(after) if isinstance(after, (list, tuple)) else [after]

    def body(*refs):
        srcs, lands_in = refs[:n], refs[n:2 * n]
        send_sems, recv_sems = refs[2 * n], refs[2 * n + 1]
        x, y, c = _mesh_pos()
        for a in range(n):
            for k, frm in enumerate(_gather_targets(x, y, c)):
                cp = pltpu.make_async_remote_copy(
                    src_ref=srcs[a], dst_ref=lands_in[a].at[_slot(*frm)], send_sem=send_sems.at[4 * a + k],
                    recv_sem=recv_sems.at[4 * a + k], device_id=frm, device_id_type=MESH)
                cp.wait_send()
                cp.wait_recv()

    out = pl.pallas_call(
        body, name=name,
        in_specs=[HBM_SPEC] * (2 * n) + [SEM_SPEC, SEM_SPEC] + [ANY_SPEC] * len(after),
        out_specs=[HBM_SPEC] * (2 * n),
        out_shape=[_hbm_like(s) for s in shards] + [_hbm_like(l) for l in lands],
        input_output_aliases={i: i for i in range(2 * n)},
        compiler_params=pltpu.CompilerParams(has_side_effects=DATAFLOW_EFFECT),
    )(*shards, *lands, send_sems, recv_sems, *after)
    return out[:n], out[n:]


def gather_finish(lands, *, name):
    n = len(lands)

    def body(*refs):
        lands_in, lands_out = refs[:n], refs[n:2 * n]
        send_sems, recv_sems = refs[2 * n:]
        x, y, c = _mesh_pos()
        sibling = (x, y, 1 - c)
        chips = [(1 - x, y), (x, 1 - y), (1 - x, 1 - y)]
        sent = []
        for a in range(n):
            for j, chip in enumerate(chips):
                s = _slot(*chip, c)
                cp = pltpu.make_async_remote_copy(
                    src_ref=lands_in[a].at[s], dst_ref=lands_out[a].at[s], send_sem=send_sems.at[a, j],
                    recv_sem=recv_sems.at[a, j], device_id=sibling, device_id_type=MESH)
                cp.start()
                sent.append(cp)
        for a in range(n):
            for j, chip in enumerate(chips):
                s = _slot(*chip, 1 - c)
                pltpu.make_async_remote_copy(
                    src_ref=lands_in[a].at[s], dst_ref=lands_out[a].at[s], send_sem=send_sems.at[a, j],
                    recv_sem=recv_sems.at[a, j], device_id=sibling, device_id_type=MESH).wait_recv()
        for cp in sent:
            cp.wait_send()

    return pl.pallas_call(
        body, name=name,
        in_specs=[ANY_SPEC] * n, out_specs=[ANY_SPEC] * n,
        out_shape=[jax.ShapeDtypeStruct(l.shape, l.dtype) for l in lands],
        input_output_aliases={i: i for i in range(n)},
        scratch_shapes=[pltpu.SemaphoreType.DMA((n, 3)), pltpu.SemaphoreType.DMA((n, 3))],
        compiler_params=pltpu.CompilerParams(has_side_effects=True),
    )(*lands)


def _peer(x, y, c, r):
    return (x ^ ((r >> 2) & 1), y ^ ((r >> 1) & 1), c ^ (r & 1))


def exchange_start(items, lands, after, *, name):
    n, m = len(items), len(lands)
    extra = [] if after is None else [after]

    def body(*refs):
        gs, zones = refs[:n], refs[n:n + m]
        send_sems, recv_sems = refs[n + m + len(extra)], refs[n + m + len(extra) + 1]
        token = refs[-1]
        x, y, c = _mesh_pos()
        me = _slot(x, y, c)
        for r in (1, 2, 4, 3, 5, 6, 7):
            to = _peer(x, y, c, r)
            for a, (_, zi, l) in enumerate(items):
                pltpu.make_async_remote_copy(
                    src_ref=gs[a].at[_slot(*to)], dst_ref=zones[zi].at[me, l], send_sem=send_sems.at[7 * a + r - 1],
                    recv_sem=recv_sems.at[7 * a + r - 1], device_id=to, device_id_type=MESH).start()
        token[...] = jnp.zeros(token.shape, f32)

    gs = [g for g, _, _ in items]
    sems = pltpu.SemaphoreType.DMA((7 * n,))
    out = pl.pallas_call(
        body, name=name,
        in_specs=[HBM_SPEC] * (n + m) + [ANY_SPEC] * len(extra),
        out_specs=[SEM_SPEC, SEM_SPEC] + [HBM_SPEC] * (n + m) + [pl.BlockSpec(memory_space=pltpu.VMEM)],
        out_shape=[sems, sems] + [_hbm_like(g) for g in gs] + [_hbm_like(z) for z in lands] + [TOKEN_SHAPE],
        input_output_aliases={i: 2 + i for i in range(n + m)},
        compiler_params=pltpu.CompilerParams(has_side_effects=DATAFLOW_EFFECT),
    )(*[_in_hbm(g) for g in gs], *[_in_hbm(z) for z in lands], *extra)
    return out[0], out[1], out[2:2 + n], out[2 + n:2 + n + m], out[-1]


def exchange_wait(parts, lands, after, *, name):
    m = len(lands)
    flat_gs = [g for _, _, _, gs in parts for g in gs]
    ng = len(flat_gs)
    after = list(after) if isinstance(after, (list, tuple)) else [after]

    def body(*refs):
        gs, zones = refs[:ng], refs[ng:ng + m]
        sem_refs = refs[ng + m:ng + m + 2 * len(parts)]
        x, y, c = _mesh_pos()
        base = 0
        for pi, (items, _, _, _) in enumerate(parts):
            send_sems, recv_sems = sem_refs[2 * pi], sem_refs[2 * pi + 1]
            for r in range(1, NDEV):
                frm = _peer(x, y, c, r)
                for a, (_, zi, l) in enumerate(items):
                    cp = pltpu.make_async_remote_copy(
                        src_ref=gs[base + a].at[_slot(*frm)], dst_ref=zones[zi].at[_slot(*frm), l],
                        send_sem=send_sems.at[7 * a + r - 1], recv_sem=recv_sems.at[7 * a + r - 1],
                        device_id=frm, device_id_type=MESH)
                    cp.wait_send()
                    cp.wait_recv()
            base += len(items)

    sem_args = [s for _, ss, rs, _ in parts for s in (ss, rs)]
    out = pl.pallas_call(
        body, name=name,
        in_specs=[HBM_SPEC] * (ng + m) + [SEM_SPEC] * len(sem_args) + [ANY_SPEC] * len(after),
        out_specs=[HBM_SPEC] * (ng + m),
        out_shape=[_hbm_like(g) for g in flat_gs] + [_hbm_like(z) for z in lands],
        input_output_aliases={i: i for i in range(ng + m)},
        compiler_params=pltpu.CompilerParams(has_side_effects=DATAFLOW_EFFECT),
    )(*flat_gs, *lands, *sem_args, *after)
    return out[ng:]


def sum8(g, *, name):
    _, R, C = g.shape
    tr = R

    def body(g_ref, o_ref):
        s = g_ref[0]
        for k in range(1, NDEV):
            s = s + g_ref[k]
        o_ref[...] = s

    return pl.pallas_call(
        body, grid=(R // tr,),
        in_specs=[pl.BlockSpec((NDEV, tr, C), lambda i: (0, i, 0))],
        out_specs=pl.BlockSpec((tr, C), lambda i: (i, 0)),
        out_shape=jax.ShapeDtypeStruct((R, C), f32),
        compiler_params=_cparams(("parallel",)), name=name,
    )(g)


def _adam_math(w, g, m, v):
    m = ADAM_B1 * m + (1.0 - ADAM_B1) * g
    v = ADAM_B2 * v + (1.0 - ADAM_B2) * (g * g)
    m_hat = m / (1.0 - ADAM_B1 ** ADAM_STEP)
    v_hat = v / (1.0 - ADAM_B2 ** ADAM_STEP)
    delta = -ADAM_LR * (m_hat / (jnp.sqrt(v_hat) + ADAM_EPS) + ADAM_WD * w)
    return delta, m, v


def adamw_slices(w, m, v, recv, *, row0=0, col0=0, rrows=None, tr=None, tc=None, l0=0, nl=None, prev=None, name):
    L, R, C = w.shape
    Rp, Cp = recv.shape[2], recv.shape[3]
    nl = L - l0 if nl is None else nl
    tr = R if tr is None else tr
    tc = C if tc is None else tc
    rrows = Rp if rrows is None else rrows
    assert R % tr == 0 and C % tc == 0 and row0 % rrows == 0 and R <= rrows
    assert (tr == R or (Rp == R and row0 == 0)) and (tc == C or (Cp == C and col0 == 0))
    rr = rrows if tr == R else tr
    rc = Cp if tc == C else tc
    rb = row0 // rrows if tr == R else 0
    prevs = [] if prev is None else list(prev)

    def body(w_ref, m_ref, v_ref, r_ref, *rest):
        g_ref, d_ref, nm_ref, nv_ref = rest[len(prevs):]
        c0 = col0 if tc == C else 0
        g = r_ref[0, 0:tr, c0:c0 + tc].astype(f32)
        for k in range(1, NDEV):
            g = g + r_ref[k, 0:tr, c0:c0 + tc].astype(f32)
        delta, nm, nv = _adam_math(w_ref[...], g, m_ref[...], v_ref[...])
        g_ref[...] = g
        d_ref[...] = delta
        nm_ref[...] = nm
        nv_ref[...] = nv

    wspec = pl.BlockSpec((None, tr, tc), lambda l, i, j: (l0 + l, i, j))
    return pl.pallas_call(
        body, grid=(nl, R // tr, C // tc),
        in_specs=[wspec, wspec, wspec, pl.BlockSpec((NDEV, None, rr, rc), lambda l, i, j: (0, l0 + l, rb + i, j))]
        + [pl.BlockSpec(memory_space=pl.ANY)] * len(prevs),
        out_specs=[wspec] * 4, out_shape=[jax.ShapeDtypeStruct((L, R, C), f32)] * 4,
        input_output_aliases={4 + q: q for q in range(len(prevs))},
        compiler_params=_cparams(("parallel", "parallel", "parallel")), name=name,
    )(w, m, v, recv, *prevs)


def adamw_slices_t(w, m, v, recv, *, name):
    L, R, C = w.shape

    def body(w_ref, m_ref, v_ref, r_ref, g_ref, d_ref, nm_ref, nv_ref):
        gt = r_ref[0].astype(f32)
        for k in range(1, NDEV):
            gt = gt + r_ref[k].astype(f32)
        g = gt.T
        delta, nm, nv = _adam_math(w_ref[...], g, m_ref[...], v_ref[...])
        g_ref[...] = g
        d_ref[...] = delta
        nm_ref[...] = nm
        nv_ref[...] = nv

    wspec = pl.BlockSpec((None, R, C), lambda l: (l, 0, 0))
    return pl.pallas_call(
        body, grid=(L,),
        in_specs=[wspec, wspec, wspec, pl.BlockSpec((NDEV, None, C, R), lambda l: (0, l, 0, 0))],
        out_specs=[wspec] * 4, out_shape=[jax.ShapeDtypeStruct((L, R, C), f32)] * 4,
        compiler_params=_cparams(("parallel",)), name=name,
    )(w, m, v, recv)


def adamw_plain(w, m, v, g, *, name):
    R, C = w.shape

    def body(w_ref, m_ref, v_ref, g_ref, d_ref, nm_ref, nv_ref):
        delta, nm, nv = _adam_math(w_ref[...], g_ref[...], m_ref[...], v_ref[...])
        d_ref[...] = delta
        nm_ref[...] = nm
        nv_ref[...] = nv

    return pl.pallas_call(
        body, out_shape=[jax.ShapeDtypeStruct((R, C), f32)] * 3, name=name,
    )(w, m, v, g)


def _chunk_causal_mask():
    pos = jnp.arange(GBLK)
    return (pos[None, :] // CHUNK) <= (pos[:, None] // CHUNK)


def kernel(x, norm_mix_g, norm_ffn_g, final_norm_g, a_w_in, a_v_norm_g, a_w_s, a_b_s, a_w_out, b_w_in, b_conv_w, b_w_out, c_w_in, c_w_grp, c_scale, c_w_out, f_w_gate, f_w_up, f_w_down, loss_target, m_norm_mix_g, m_norm_ffn_g, m_final_norm_g, m_a_w_in, m_a_v_norm_g, m_a_w_s, m_a_b_s, m_a_w_out, m_b_w_in, m_b_conv_w, m_b_w_out, m_c_w_in, m_c_w_grp, m_c_scale, m_c_w_out, m_f_w_gate, m_f_w_up, m_f_w_down, v_norm_mix_g, v_norm_ffn_g, v_final_norm_g, v_a_w_in, v_a_v_norm_g, v_a_w_s, v_a_b_s, v_a_w_out, v_b_w_in, v_b_conv_w, v_b_w_out, v_c_w_in, v_c_w_grp, v_c_scale, v_c_w_out, v_f_w_gate, v_f_w_up, v_f_w_down):
    B, S, _ = x.shape
    T = B * S
    tm = min(512, S)
    tmm = min(1024, T)
    tms = min(512, T)
    tma = min(256, S)
    tmf = min(256, T)
    me = _slot(lax.axis_index("x"), lax.axis_index("y"), lax.axis_index("c"))

    def tb(w):
        return jnp.transpose(w, (0, 2, 1)).astype(bf16)

    small = jnp.concatenate([a_v_norm_g, b_conv_w[0], c_scale, jnp.zeros((2, GBLK), f32)], axis=0)
    started = [gather_start([tb(a_w_in[0:1])[0], a_w_out[0].astype(bf16), small], me, None, name="gather_start_0")]
    tok = started[0][4]
    z = tok[0, 0]
    f_g, f_u, f_d = tb(f_w_gate + z), tb(f_w_up + z), (f_w_down + z).astype(bf16)
    stage_shards = [
        None,
        [f_g[0], f_u[0], f_d[0]],
        [tb(b_w_in + z)[0], (b_w_out[0] + z).astype(bf16), f_g[1], f_u[1], f_d[1]],
        [(c_w_in[0] + z).astype(bf16), (c_w_grp[0] + z).astype(bf16), (c_w_out[0] + z).astype(bf16), f_g[2], f_u[2], f_d[2]],
        [tb(a_w_in[1:2] + z)[0], (a_w_out[1] + z).astype(bf16), f_g[3], f_u[3], f_d[3]],
    ]
    for s in range(1, len(stage_shards)):
        started.append(gather_start(stage_shards[s], me, tok, name=f"gather_start_{s}"))
        tok = started[-1][4]

    def stage_weights(s, after):
        send_sems, recv_sems, shards, lands, _ = started[s]
        _, lands = gather_wait(send_sems, recv_sems, shards, lands, after, name=f"gather_wait_{s}")
        lands = gather_finish(lands, name=f"gather_finish_{s}")
        return [t.reshape(-1, t.shape[-1]) if t.ndim == 3 else t for t in lands]

    wm_masked = jnp.where(_chunk_causal_mask()[None, None], a_w_s, 0.0).astype(bf16)
    bsb = jnp.broadcast_to(a_b_s[:, :, :, None], a_b_s.shape + (GBLK,))

    xc = x.reshape(T, D)
    saved, weights = [], []
    for i in range(DEPTH):
        kind, j = i % 3, i // 3
        gmix = norm_mix_g[i][None]
        if i == 0:
            w_in, w_out, small_g = stage_weights(0, tok)
            small_full = jnp.transpose(small_g.reshape(NDEV, 8, GBLK), (1, 0, 2)).reshape(8, D)
            gv_full, cw_full, cs_full = small_full[0:2], small_full[2:5], small_full[5:6]
            w = [w_in, w_out]
        else:
            w = stage_weights(i + 1, xc)
        if kind == 0:
            w_in, w_out = w[:2]
            p, h = norm_mm(xc, gmix, w_in, tm=tms, transposed=True, name=f"a_in_{i}")
            y = a_mid_fwd(p, gv_full[j][None], wm_masked[j], bsb[j], tm=tma, name=f"a_mid_{i}")
        elif kind == 1:
            w_in, w_out = w[:2]
            p, h = norm_mm(xc, gmix, w_in, tm=tms, transposed=True, name=f"b_in_{i}")
            y = b_mid_fwd(p, cw_full, tm=tm, S=S, name=f"b_mid_{i}")
        else:
            w_in, w_grp, w_out = w[:3]
            p, h = norm_mm(xc, gmix, w_in, tm=tms, transposed=False, name=f"c_in_{i}")
            y = c_mid_fwd(p, w_grp, cs_full, tm=tm, S=S, name=f"c_mid_{i}")
        if i == 0:
            w = w + stage_weights(1, y)
        weights.append(w)
        w_fg, w_fu, w_fd = w[-3:]
        x1, x2, pg, pu, a, h2 = ffn_fwd(y, w_out, xc, norm_ffn_g[i][None], w_fg, w_fu, w_fd, tm=tmf, name=f"f_fwd_{i}")
        saved.append((xc, p, h, y, x1, pg, pu, h2, a))
        xc = x2

    loss_blk, dx, d_final = final_loss(xc, final_norm_g[None], loss_target.reshape(T, D), tm=tms, name="final_loss")
    loss = lax.psum(loss_blk[0, 0], AXES)

    zone_shapes = [(2, 256, D), (2, GBLK, D), (1, 384, D), (1, GBLK, D), (1, GBLK, D), (1, GBLK, C_GDIM), (1, GBLK, D),
                   (DEPTH, F_LOC, D), (DEPTH, F_LOC, D), (DEPTH, F_LOC, D)]
    Z_AIN, Z_AOUT, Z_BIN, Z_BOUT, Z_CIN, Z_CGRP, Z_COUT, Z_FG, Z_FU, Z_FD = range(10)
    lands = [lax.empty((NDEV,) + s, bf16) for s in zone_shapes]
    parts = []

    def exchange(items, after, name):
        nonlocal lands
        for g, zi, l in items:
            own = lax.dynamic_index_in_dim(g, me, axis=0, keepdims=True)[None]
            lands[zi] = lax.dynamic_update_slice(lands[zi], own, (me, l) + (0,) * (g.ndim - 1))
        idx = sorted({zi for _, zi, _ in items})
        local = [(g, idx.index(zi), l) for g, zi, l in items]
        send_sems, recv_sems, gs, zs, token = exchange_start(local, [lands[zi] for zi in idx], after, name=name)
        for q, zi in enumerate(idx):
            lands[zi] = zs[q]
        parts.append(([(None, zi, l) for _, zi, l in items], send_sems, recv_sems, gs))
        return token

    d_mix, d_ffn = [None] * DEPTH, [None] * DEPTH
    d_gv, d_wm, d_bs = [None] * 2, [None] * 2, [None] * 2
    tok = None
    for i in reversed(range(DEPTH)):
        kind, j = i % 3, i // 3
        xin, p, h, y, x1, pg, pu, h2, a = saved[i]
        w = weights[i]
        w_in, w_out = w[0], w[-4]
        w_fg, w_fu, w_fd = w[-3:]
        dpg, dpu, dxb, dx1, d_ffn[i], dy, dx1b = ffn_bwd(dx, w_fd, pg, pu, w_fg, w_fu, x1, norm_ffn_g[i][None], tok, w_out,
                                                         tm=tmf, name=f"f_bwd_{i}")
        gw_g, gw_u, gw_d = ffn_dw(dpg, dpu, a, h2, dxb, name=f"f_dw_{i}")
        tok = exchange([(gw_g, Z_FG, i), (gw_u, Z_FU, i), (gw_d, Z_FD, i)], tok, f"exchange_start_ffn_{i}")
        z = tok[0:1, 0:1]
        if kind == 0:
            dp, d_gv[j], d_wm[j], d_bs[j] = a_mid_bwd(p, dy, gv_full[j][None] + z, wm_masked[j], bsb[j], tm=tma,
                                                      name=f"a_bwd_mid_{i}")
            items = [(mm_tn(dp, h, name=f"a_dw_in_{i}"), Z_AIN, j),
                     (mm_tn(y, dx1b, name=f"a_dw_out_{i}"), Z_AOUT, j)]
        elif kind == 1:
            dp, d_cw = b_mid_bwd(p, dy, cw_full + z, tm=tm, S=S, name=f"b_bwd_mid_{i}")
            items = [(mm_tn(dp, h, name=f"b_dw_in_{i}"), Z_BIN, 0),
                     (mm_tn(y, dx1b, name=f"b_dw_out_{i}"), Z_BOUT, 0)]
        else:
            dp, d_cs, d_wgrp = c_mid_bwd(p, dy, w[1], cs_full + z, tm=tm, S=S, name=f"c_bwd_mid_{i}")
            gw_cgrp = jnp.transpose(d_wgrp.reshape(C_GROUPS, NDEV, C_GDIM // NDEV, C_GDIM), (1, 0, 2, 3)).astype(bf16)
            items = [(mm_tn(h, dp, name=f"c_dw_in_{i}"), Z_CIN, 0),
                     (gw_cgrp.reshape(NDEV, GBLK, C_GDIM), Z_CGRP, 0),
                     (mm_tn(y, dx1b, name=f"c_dw_out_{i}"), Z_COUT, 0)]
        tok = exchange(items, tok, f"exchange_start_mix_{i}")
        dx, d_mix[i] = mm_normbwd([dp], [w_in], xin, norm_mix_g[i][None], dx1, tok, tm=tms, transposed=(kind != 2),
                                  name=f"mix_bwd_in_{i}")
    grad_x = dx.reshape(B, S, D)

    mask = _chunk_causal_mask()
    small_parts = [
        jnp.concatenate(d_mix, axis=0).reshape(-1, GBLK),
        jnp.concatenate(d_ffn, axis=0).reshape(-1, GBLK),
        d_final.reshape(-1, GBLK),
        jnp.concatenate(d_gv, axis=0).reshape(-1, GBLK),
        jnp.where(mask[None, None], jnp.stack(d_wm), 0.0).reshape(-1, GBLK),
        jnp.stack(d_bs).reshape(-1, GBLK),
        d_cw.reshape(-1, GBLK),
        d_cs.reshape(-1, GBLK),
    ]
    small_rows = [q.shape[0] for q in small_parts]
    sg_send, sg_recv, sg_shards, sg_lands, _ = gather_start(
        [jnp.concatenate(small_parts, axis=0)], me, tok, name="gather_small_start")

    def plain(w, m, v, g, name):
        shp = w.shape
        w2, m2, v2, g2 = (t.reshape(-1, shp[-1]) for t in (w, m, v, g))
        return tuple(t.reshape(shp) for t in adamw_plain(w2, m2, v2, g2, name=name))

    def sliced(w, m, v, recv, name, **kw):
        shp = w.shape
        w3, m3, v3 = (t.reshape((shp[0], -1, shp[-1])) for t in (w, m, v))
        return tuple(t.reshape(shp) for t in adamw_slices(w3, m3, v3, recv, name=name, **kw))

    def sliced_t(w, m, v, recv, name, **kw):
        wt, mt, vt = (jnp.transpose(t, (0, 2, 1)) for t in (w, m, v))
        return adamw_slices(wt, mt, vt, recv, name=name, **kw)

    def untransposed(outs):
        return tuple(jnp.transpose(t, (0, 2, 1)) for t in outs)

    lands = list(exchange_wait(parts[:6], lands, dx, name="exchange_wait"))
    _, _, r_bin, r_bout, r_cin, r_cgrp, r_cout, r_fg, r_fu, r_fd = lands
    res = {}
    res["b_w_in"] = adamw_slices_t(b_w_in, m_b_w_in, v_b_w_in, r_bin, name="adam_b_w_in")
    res["b_w_out"] = sliced(b_w_out, m_b_w_out, v_b_w_out, r_bout, "adam_b_w_out")
    res["c_w_in"] = sliced(c_w_in, m_c_w_in, v_c_w_in, r_cin, "adam_c_w_in")
    res["c_w_grp"] = sliced(c_w_grp, m_c_w_grp, v_c_w_grp, r_cgrp, "adam_c_w_grp")
    res["c_w_out"] = sliced(c_w_out, m_c_w_out, v_c_w_out, r_cout, "adam_c_w_out")
    fg = sliced_t(f_w_gate, m_f_w_gate, v_f_w_gate, r_fg, "adam_f_gate_123", tc=256, l0=1)
    fu = sliced_t(f_w_up, m_f_w_up, v_f_w_up, r_fu, "adam_f_up_123", tc=256, l0=1)
    fd = adamw_slices(f_w_down, m_f_w_down, v_f_w_down, r_fd, tc=256, l0=1, name="adam_f_down_123")
    big = [res[n][1] for n in ("b_w_in", "b_w_out", "c_w_in", "c_w_grp", "c_w_out")] + [fg[1], fu[1], fd[1]]

    ffn0_items, ffn0_send, ffn0_recv, ffn0_gs = parts[6]
    ffn0_zones = [Z_FG, Z_FU, Z_FD]
    ffn0_part = ([(None, ffn0_zones.index(zi), l) for _, zi, l in ffn0_items], ffn0_send, ffn0_recv, ffn0_gs)
    r_fg, r_fu, r_fd = exchange_wait([ffn0_part], [lands[zi] for zi in ffn0_zones], big, name="exchange_wait_ffn0")
    fg = sliced_t(f_w_gate, m_f_w_gate, v_f_w_gate, r_fg, "adam_f_gate_0", tc=256, l0=0, nl=1, prev=fg)
    fu = sliced_t(f_w_up, m_f_w_up, v_f_w_up, r_fu, "adam_f_up_0", tc=256, l0=0, nl=1, prev=fu)
    fd = adamw_slices(f_w_down, m_f_w_down, v_f_w_down, r_fd, tc=256, l0=0, nl=1, prev=fd, name="adam_f_down_0")
    res["f_w_gate"], res["f_w_up"], res["f_w_down"] = untransposed(fg), untransposed(fu), tuple(fd)

    last_items, last_send, last_recv, last_gs = parts[7]
    last_zones = [Z_AIN, Z_AOUT]
    last_part = ([(None, last_zones.index(zi), l) for _, zi, l in last_items], last_send, last_recv, last_gs)
    r_ain, r_aout = exchange_wait([last_part], [lands[zi] for zi in last_zones], [fg[1], fu[1], fd[1]], name="exchange_wait_last")
    res["a_w_in"] = adamw_slices_t(a_w_in, m_a_w_in, v_a_w_in, r_ain, name="adam_a_w_in")
    res["a_w_out"] = sliced(a_w_out, m_a_w_out, v_a_w_out, r_aout, "adam_a_w_out")

    _, sg_lands = gather_wait(sg_send, sg_recv, sg_shards, sg_lands, [res["a_w_in"][1], res["a_w_out"][1]], name="gather_small_wait")
    (gs_all,) = gather_finish(sg_lands, name="gather_small_finish")
    gs = sum8(gs_all, name="sum_small_grads")
    offs = [0]
    for r in small_rows:
        offs.append(offs[-1] + r)
    sp = [gs[offs[q]:offs[q + 1]] for q in range(len(small_parts))]
    grad_norm_mix_g = sp[0].reshape(DEPTH, D)
    grad_norm_ffn_g = sp[1].reshape(DEPTH, D)
    grad_final_norm_g = sp[2].reshape(D)
    grad_a_w_s = sp[4].reshape(a_w_s.shape)
    grad_a_b_s = sp[5].reshape(a_b_s.shape)

    def my_cols(full):
        return lax.dynamic_slice_in_dim(full, me * GBLK, GBLK, axis=1)

    grad_a_v_norm_g = my_cols(sp[3].reshape(2, D))
    grad_b_conv_w = my_cols(sp[6].reshape(3, D))[None]
    grad_c_scale = my_cols(sp[7].reshape(1, D))
    res["norm_mix_g"] = (grad_norm_mix_g,) + plain(norm_mix_g, m_norm_mix_g, v_norm_mix_g, grad_norm_mix_g, "adam_norm_mix")
    res["norm_ffn_g"] = (grad_norm_ffn_g,) + plain(norm_ffn_g, m_norm_ffn_g, v_norm_ffn_g, grad_norm_ffn_g, "adam_norm_ffn")
    res["final_norm_g"] = (grad_final_norm_g,) + tuple(
        t.reshape(D) for t in plain(final_norm_g[None], m_final_norm_g[None], v_final_norm_g[None], grad_final_norm_g[None], "adam_final"))
    res["a_v_norm_g"] = (grad_a_v_norm_g,) + plain(a_v_norm_g, m_a_v_norm_g, v_a_v_norm_g, grad_a_v_norm_g, "adam_a_v_norm")
    res["a_w_s"] = (grad_a_w_s,) + plain(a_w_s, m_a_w_s, v_a_w_s, grad_a_w_s, "adam_a_w_s")
    res["a_b_s"] = (grad_a_b_s,) + plain(a_b_s, m_a_b_s, v_a_b_s, grad_a_b_s, "adam_a_b_s")
    res["b_conv_w"] = (grad_b_conv_w,) + plain(b_conv_w, m_b_conv_w, v_b_conv_w, grad_b_conv_w, "adam_b_conv")
    res["c_scale"] = (grad_c_scale,) + plain(c_scale, m_c_scale, v_c_scale, grad_c_scale, "adam_c_scale")

    order = ["norm_mix_g", "norm_ffn_g", "final_norm_g", "a_w_in", "a_v_norm_g", "a_w_s", "a_b_s", "a_w_out", "b_w_in",
             "b_conv_w", "b_w_out", "c_w_in", "c_w_grp", "c_scale", "c_w_out", "f_w_gate", "f_w_up", "f_w_down"]
    return (loss, grad_x, *[res[n][0] for n in order], *[res[n][1] for n in order],
            *[res[n][2] for n in order], *[res[n][3] for n in order])
```

```python
import functools

import jax
import jax.numpy as jnp
from jax import lax
from jax.experimental import pallas as pl
from jax.experimental.pallas import tpu as pltpu

f32 = jnp.float32
bf16 = jnp.bfloat16

NDEV = 8
D = 1024
EPS = 1e-6
CHUNK = 64
GBLK = 128
A_GROUPS = 8
C_GROUPS = 4
C_GDIM = 256
POOL_WINDOWS = (2, 4, 8, 16)
HALO = 16
COLS = 256
F_LOC = 352
MXU_COLS = 256
DEPTH = 4
AXES = ("x", "y", "c")
MESH = pl.DeviceIdType.MESH

ADAM_LR = 0.001
ADAM_B1 = 0.9
ADAM_B2 = 0.999
ADAM_EPS = 1e-08
ADAM_WD = 0.01
ADAM_STEP = 10

VMEM_LIMIT = 56 * 1024 * 1024


def _cparams(sem):
    return pltpu.CompilerParams(dimension_semantics=sem, vmem_limit_bytes=VMEM_LIMIT)


def _gelu(z):
    return 0.5 * z * (1.0 + lax.erf(z * 0.7071067811865476))


def _gelu_cdf(z):
    return 0.5 * (1.0 + lax.erf(z * 0.7071067811865476))


def _gelu_grad(z, cdf):
    return cdf + z * jnp.exp(-0.5 * z * z) * 0.3989422804014327


def _dot(a, b):
    return jnp.dot(a, b, preferred_element_type=f32)


def _dot_nt(a, b):
    return lax.dot_general(a, b, (((1,), (1,)), ((), ())), preferred_element_type=f32)


def _dot_tn(a, b):
    return lax.dot_general(a, b, (((0,), (0,)), ((), ())), preferred_element_type=f32)


def _col_chunks(n, width=1024):
    return [(c, min(c + width, n)) for c in range(0, n, width)]


def norm_mm(x, g, w, *, tm, transposed, name):
    T = x.shape[0]
    n = w.shape[0] if transposed else w.shape[1]

    def body(x_ref, g_ref, w_ref, p_ref, h_ref):
        xv = x_ref[...]
        r = lax.rsqrt(jnp.mean(xv * xv, axis=-1, keepdims=True) + EPS)
        h = (xv * r * g_ref[...]).astype(bf16)
        h_ref[...] = h
        p = _dot_nt(h, w_ref[...]) if transposed else _dot(h, w_ref[...])
        p_ref[...] = p.astype(bf16)

    return pl.pallas_call(
        body, grid=(T // tm,),
        in_specs=[pl.BlockSpec((tm, D), lambda i: (i, 0)), pl.BlockSpec((1, D), lambda i: (0, 0)),
                  pl.BlockSpec(w.shape, lambda i: (0, 0), pipeline_mode=pl.Buffered(1))],
        out_specs=[pl.BlockSpec((tm, n), lambda i: (i, 0)), pl.BlockSpec((tm, D), lambda i: (i, 0))],
        out_shape=[jax.ShapeDtypeStruct((T, n), bf16), jax.ShapeDtypeStruct((T, D), bf16)],
        compiler_params=_cparams(("parallel",)), name=name,
    )(x, g, w)


def ffn_fwd(y, w_out, xin, g, wtg, wtu, wd, *, tm, name):
    T, K = y.shape
    H = wtg.shape[0]

    def body(y_ref, wo_ref, xin_ref, g_ref, wg_ref, wu_ref, wd_ref, x_ref, x2_ref, pg_ref, pu_ref, a_ref, h_ref):
        xv = xin_ref[...] + _dot(y_ref[...], wo_ref[...])
        x_ref[...] = xv
        r = lax.rsqrt(jnp.mean(xv * xv, axis=-1, keepdims=True) + EPS)
        h = (xv * r * g_ref[...]).astype(bf16)
        h_ref[...] = h
        out = None
        for c0, c1 in _col_chunks(H):
            gate = _dot_nt(h, wg_ref[c0:c1, :])
            up = _dot_nt(h, wu_ref[c0:c1, :])
            pg_ref[:, c0:c1] = gate.astype(bf16)
            pu_ref[:, c0:c1] = up.astype(bf16)
            ac = (gate * jax.nn.sigmoid(gate) * up).astype(bf16)
            a_ref[:, c0:c1] = ac
            part = _dot(ac, wd_ref[c0:c1, :])
            out = part if out is None else out + part
        x2_ref[...] = xv + out

    wspec = pl.BlockSpec((H, D), lambda i: (0, 0), pipeline_mode=pl.Buffered(1))
    hspec = pl.BlockSpec((tm, H), lambda i: (i, 0))
    dspec = pl.BlockSpec((tm, D), lambda i: (i, 0))
    return pl.pallas_call(
        body, grid=(T // tm,),
        in_specs=[pl.BlockSpec((tm, K), lambda i: (i, 0)),
                  pl.BlockSpec((K, D), lambda i: (0, 0), pipeline_mode=pl.Buffered(1)),
                  dspec, pl.BlockSpec((1, D), lambda i: (0, 0)), wspec, wspec, wspec],
        out_specs=[dspec, dspec, hspec, hspec, hspec, dspec],
        out_shape=[jax.ShapeDtypeStruct((T, D), f32)] * 2 + [jax.ShapeDtypeStruct((T, H), bf16)] * 3
        + [jax.ShapeDtypeStruct((T, D), bf16)],
        compiler_params=_cparams(("parallel",)), name=name,
    )(y, w_out, xin, g, wtg, wtu, wd)


def ffn_bwd(dx, wd, pg, pu, wtg, wtu, x, g, tok, w_next, *, tm, name):
    T = dx.shape[0]
    H = wd.shape[0]
    kn = w_next.shape[0]
    toks = [] if tok is None else [tok]

    def body(dx_ref, wd_ref, pg_ref, pu_ref, wg_ref, wu_ref, x_ref, g_ref, *rest):
        wn_ref = rest[len(toks)]
        dg_ref, du_ref, dxb_ref, dx1_ref, dgn_ref, dy_ref, dx1b_ref = rest[len(toks) + 1:]

        @pl.when(pl.program_id(0) == 0)
        def _():
            dgn_ref[...] = jnp.zeros(dgn_ref.shape, f32)

        dxv = dx_ref[...]
        dxb = dxv.astype(bf16)
        dxb_ref[...] = dxb
        dh = None
        for c0, c1 in _col_chunks(H):
            da = _dot_nt(dxb, wd_ref[c0:c1, :])
            gate = pg_ref[:, c0:c1].astype(f32)
            up = pu_ref[:, c0:c1].astype(f32)
            sg = jax.nn.sigmoid(gate)
            dgc = (da * up * (sg * (1.0 + gate * (1.0 - sg)))).astype(bf16)
            duc = (da * gate * sg).astype(bf16)
            dg_ref[:, c0:c1] = dgc
            du_ref[:, c0:c1] = duc
            part = _dot(dgc, wg_ref[c0:c1, :]) + _dot(duc, wu_ref[c0:c1, :])
            dh = part if dh is None else dh + part
        gv = g_ref[...] + rest[0][0:1, 0:1] if toks else g_ref[...]
        xv = x_ref[...]
        r = lax.rsqrt(jnp.mean(xv * xv, axis=-1, keepdims=True) + EPS)
        xhat = xv * r
        dgn_ref[...] += jnp.sum(dh * xhat, axis=0, keepdims=True)
        dxhat = dh * gv
        dx1 = dxv + r * (dxhat - xhat * jnp.mean(dxhat * xhat, axis=-1, keepdims=True))
        dx1_ref[...] = dx1
        dx1b = dx1.astype(bf16)
        dx1b_ref[...] = dx1b
        dy_ref[...] = _dot_nt(dx1b, wn_ref[...]).astype(bf16)

    dspec = pl.BlockSpec((tm, D), lambda i: (i, 0))
    hspec = pl.BlockSpec((tm, H), lambda i: (i, 0))
    wspec = pl.BlockSpec((H, D), lambda i: (0, 0), pipeline_mode=pl.Buffered(1))
    gspec = pl.BlockSpec((1, D), lambda i: (0, 0))
    return pl.pallas_call(
        body, grid=(T // tm,),
        in_specs=[dspec, wspec, hspec, hspec, wspec, wspec, dspec, gspec]
        + [pl.BlockSpec((8, 128), lambda i: (0, 0))] * len(toks)
        + [pl.BlockSpec((kn, D), lambda i: (0, 0), pipeline_mode=pl.Buffered(1))],
        out_specs=[hspec, hspec, dspec, dspec, gspec, pl.BlockSpec((tm, kn), lambda i: (i, 0)), dspec],
        out_shape=[jax.ShapeDtypeStruct((T, H), bf16)] * 2 + [jax.ShapeDtypeStruct((T, D), bf16),
                   jax.ShapeDtypeStruct((T, D), f32), jax.ShapeDtypeStruct((1, D), f32),
                   jax.ShapeDtypeStruct((T, kn), bf16), jax.ShapeDtypeStruct((T, D), bf16)],
        compiler_params=_cparams(("arbitrary",)), name=name,
    )(dx, wd, pg, pu, wtg, wtu, x, g, *toks, w_next)


def mm_tn(a, b, *, name):
    T, K = a.shape
    n = b.shape[1]

    def body(a_ref, b_ref, o_ref):
        o_ref[...] = _dot_tn(a_ref[...], b_ref[...]).astype(bf16)

    out = pl.pallas_call(
        body, grid=(K // MXU_COLS,),
        in_specs=[pl.BlockSpec((T, MXU_COLS), lambda d: (0, d)),
                  pl.BlockSpec((T, n), lambda d: (0, 0), pipeline_mode=pl.Buffered(1))],
        out_specs=pl.BlockSpec((MXU_COLS, n), lambda d: (d, 0)),
        out_shape=jax.ShapeDtypeStruct((K, n), bf16),
        compiler_params=_cparams(("parallel",)), name=name,
    )(a, b)
    return out.reshape(NDEV, K // NDEV, n)


def ffn_dw(dpg, dpu, a, h2, dxb, *, name):
    T, H = dpg.shape

    def body(g_ref, u_ref, a_ref, h_ref, x_ref, og_ref, ou_ref, od_ref):
        h = h_ref[...]
        og_ref[...] = _dot_tn(g_ref[...], h).astype(bf16)
        ou_ref[...] = _dot_tn(u_ref[...], h).astype(bf16)
        od_ref[...] = _dot_tn(a_ref[...], x_ref[...]).astype(bf16)

    blk = pl.BlockSpec((T, MXU_COLS), lambda d: (0, d))
    full = pl.BlockSpec((T, D), lambda d: (0, 0), pipeline_mode=pl.Buffered(1))
    out = pl.BlockSpec((MXU_COLS, D), lambda d: (d, 0))
    outs = pl.pallas_call(
        body, grid=(H // MXU_COLS,),
        in_specs=[blk, blk, blk, full, full], out_specs=[out] * 3,
        out_shape=[jax.ShapeDtypeStruct((H, D), bf16)] * 3,
        compiler_params=_cparams(("parallel",)), name=name,
    )(dpg, dpu, a, h2, dxb)
    return [o.reshape(NDEV, H // NDEV, D) for o in outs]


def _normbwd_tail(dh, x_ref, g_ref, tok_ref, dres_ref, dx_ref, dgn_ref):
    gv = g_ref[...] + tok_ref[0:1, 0:1]
    xv = x_ref[...]
    r = lax.rsqrt(jnp.mean(xv * xv, axis=-1, keepdims=True) + EPS)
    xhat = xv * r
    dgn_ref[...] += jnp.sum(dh * xhat, axis=0, keepdims=True)
    dxhat = dh * gv
    dx_ref[...] = dres_ref[...] + r * (dxhat - xhat * jnp.mean(dxhat * xhat, axis=-1, keepdims=True))


def _normbwd_specs(tm, w):
    dspec = pl.BlockSpec((tm, D), lambda i: (i, 0))
    gspec = pl.BlockSpec((1, D), lambda i: (0, 0))
    ins = [pl.BlockSpec(w.shape, lambda i: (0, 0), pipeline_mode=pl.Buffered(1)), dspec, gspec,
           pl.BlockSpec((8, 128), lambda i: (0, 0)), dspec]
    return ins, [dspec, gspec]


def _prev_halo_spec(tm, ncol):
    return pl.BlockSpec((HALO, ncol), lambda i: (jnp.maximum(i * (tm // HALO) - 1, 0), 0))


def _next_halo_spec(tm, ncol, T):
    return pl.BlockSpec((HALO, ncol), lambda i: (jnp.minimum((i + 1) * (tm // HALO), T // HALO - 1), 0))


def a_mid_fwd(p, gv, wm, bsb, *, tm, name):
    T = p.shape[0]

    def body(p_ref, gv_ref, wm_ref, bs_ref, y_ref, vn_sc):
        v = _gelu(p_ref[:, D:].astype(f32))
        vc = v - jnp.mean(v, axis=-1, keepdims=True)
        var = jnp.mean(vc * vc, axis=-1, keepdims=True)
        vn_sc[...] = (vc * lax.rsqrt(var + EPS) * gv_ref[...]).astype(bf16)
        for g in range(A_GROUPS):
            cs = slice(g * GBLK, (g + 1) * GBLK)
            for r in range(tm // GBLK):
                rs = slice(r * GBLK, (r + 1) * GBLK)
                sv = _dot(wm_ref[g], vn_sc[rs, cs]) + bs_ref[g]
                y_ref[rs, cs] = (_gelu(p_ref[rs, cs].astype(f32)) * sv).astype(bf16)

    return pl.pallas_call(
        body, grid=(T // tm,),
        in_specs=[pl.BlockSpec((tm, 2 * D), lambda i: (i, 0)), pl.BlockSpec((1, D), lambda i: (0, 0)),
                  pl.BlockSpec((A_GROUPS, GBLK, GBLK), lambda i: (0, 0, 0)),
                  pl.BlockSpec((A_GROUPS, GBLK, GBLK), lambda i: (0, 0, 0))],
        out_specs=pl.BlockSpec((tm, D), lambda i: (i, 0)),
        out_shape=jax.ShapeDtypeStruct((T, D), bf16),
        scratch_shapes=[pltpu.VMEM((tm, D), bf16)],
        compiler_params=_cparams(("parallel",)), name=name,
    )(p, gv, wm, bsb)


def a_mid_bwd(p, dy, gv, wm, bsb, wt, x, g, tok, dres, *, tm, name):
    T = p.shape[0]

    def body(p_ref, dy_ref, gv_ref, wm_ref, bs_ref, wt_ref, x_ref, g_ref, tok_ref, dres_ref,
             dp_ref, dgv_ref, dwm_ref, dbs_ref, dx_ref, dgn_ref, vn_sc, dvn_sc):
        @pl.when(pl.program_id(0) == 0)
        def _():
            dgn_ref[...] = jnp.zeros(dgn_ref.shape, f32)
            dgv_ref[...] = jnp.zeros(dgv_ref.shape, f32)
            dwm_ref[...] = jnp.zeros(dwm_ref.shape, f32)
            dbs_ref[...] = jnp.zeros(dbs_ref.shape, f32)

        zv = p_ref[:, D:].astype(f32)
        cdf_v = _gelu_cdf(zv)
        v = zv * cdf_v
        vc = v - jnp.mean(v, axis=-1, keepdims=True)
        rstd = lax.rsqrt(jnp.mean(vc * vc, axis=-1, keepdims=True) + EPS)
        vhat = vc * rstd
        vn_sc[...] = (vhat * gv_ref[...]).astype(bf16)
        for g in range(A_GROUPS):
            cs = slice(g * GBLK, (g + 1) * GBLK)
            dwm = jnp.zeros((GBLK, GBLK), f32)
            dbs = jnp.zeros((GBLK, 1), f32)
            for r in range(tm // GBLK):
                rs = slice(r * GBLK, (r + 1) * GBLK)
                zu = p_ref[rs, cs].astype(f32)
                cdf_u = _gelu_cdf(zu)
                vn = vn_sc[rs, cs]
                sv = _dot(wm_ref[g], vn) + bs_ref[g]
                dyb = dy_ref[rs, cs].astype(f32)
                dsv = dyb * (zu * cdf_u)
                dsvb = dsv.astype(bf16)
                dp_ref[rs, cs] = (dyb * sv * _gelu_grad(zu, cdf_u)).astype(bf16)
                dwm += _dot_nt(dsvb, vn)
                dbs += jnp.sum(dsv, axis=1, keepdims=True)
                dvn_sc[rs, cs] = _dot_tn(wm_ref[g], dsvb)
            dwm_ref[g] += dwm
            dbs_ref[g] += dbs
        dvn = dvn_sc[...]
        dgv_ref[...] += jnp.sum(dvn * vhat, axis=0, keepdims=True)
        dvhat = dvn * gv_ref[...]
        dv = rstd * (dvhat - jnp.mean(dvhat, axis=-1, keepdims=True)
                     - vhat * jnp.mean(dvhat * vhat, axis=-1, keepdims=True))
        dp_ref[:, D:] = (dv * _gelu_grad(zv, cdf_v)).astype(bf16)
        _normbwd_tail(_dot(dp_ref[...], wt_ref[...]), x_ref, g_ref, tok_ref, dres_ref, dx_ref, dgn_ref)

    tail_in, tail_out = _normbwd_specs(tm, wt)
    return pl.pallas_call(
        body, grid=(T // tm,),
        in_specs=[pl.BlockSpec((tm, 2 * D), lambda i: (i, 0)), pl.BlockSpec((tm, D), lambda i: (i, 0)),
                  pl.BlockSpec((1, D), lambda i: (0, 0)),
                  pl.BlockSpec((A_GROUPS, GBLK, GBLK), lambda i: (0, 0, 0)),
                  pl.BlockSpec((A_GROUPS, GBLK, GBLK), lambda i: (0, 0, 0))] + tail_in,
        out_specs=[pl.BlockSpec((tm, 2 * D), lambda i: (i, 0)), pl.BlockSpec((1, D), lambda i: (0, 0)),
                   pl.BlockSpec((A_GROUPS, GBLK, GBLK), lambda i: (0, 0, 0)),
                   pl.BlockSpec((A_GROUPS, GBLK, 1), lambda i: (0, 0, 0))] + tail_out,
        out_shape=[jax.ShapeDtypeStruct((T, 2 * D), bf16), jax.ShapeDtypeStruct((1, D), f32),
                   jax.ShapeDtypeStruct((A_GROUPS, GBLK, GBLK), f32), jax.ShapeDtypeStruct((A_GROUPS, GBLK, 1), f32),
                   jax.ShapeDtypeStruct((T, D), f32), jax.ShapeDtypeStruct((1, D), f32)],
        scratch_shapes=[pltpu.VMEM((tm, D), bf16), pltpu.VMEM((tm, D), f32)],
        compiler_params=_cparams(("arbitrary",)), name=name,
    )(p, dy, gv, wm, bsb, wt, x, g, tok, dres)


def b_mid_fwd(p, cw, *, tm, S, name):
    T = p.shape[0]
    nst = S // tm

    def body(p_ref, ph_ref, cw_ref, y_ref, ext):
        first = (pl.program_id(0) % nst) == 0
        for c0 in range(0, D, COLS):
            cs, cc, cx = slice(c0, c0 + COLS), slice(D + c0, D + c0 + COLS), slice(2 * D + c0, 2 * D + c0 + COLS)
            q = p_ref[:, cc].astype(f32) * p_ref[:, cx].astype(f32)
            qh = ph_ref[:, cc].astype(f32) * ph_ref[:, cx].astype(f32)
            ext[0:HALO, cs] = jnp.where(first, 0.0, qh)
            ext[HALO:, cs] = q
            y = (cw_ref[0:1, cs] * ext[pl.ds(HALO - 2, tm), cs] + cw_ref[1:2, cs] * ext[pl.ds(HALO - 1, tm), cs]
                 + cw_ref[2:3, cs] * q)
            y_ref[:, cs] = (p_ref[:, cs].astype(f32) * y).astype(bf16)

    return pl.pallas_call(
        body, grid=(T // tm,),
        in_specs=[pl.BlockSpec((tm, 3 * D), lambda i: (i, 0)), _prev_halo_spec(tm, 3 * D),
                  pl.BlockSpec((3, D), lambda i: (0, 0))],
        out_specs=pl.BlockSpec((tm, D), lambda i: (i, 0)),
        out_shape=jax.ShapeDtypeStruct((T, D), bf16),
        scratch_shapes=[pltpu.VMEM((tm + HALO, D), f32)],
        compiler_params=_cparams(("parallel",)), name=name,
    )(p, p, cw)


def b_mid_bwd(p, dy, cw, wt, x, g, tok, dres, *, tm, S, name):
    T = p.shape[0]
    nst = S // tm

    def body(p_ref, ph_ref, pn_ref, dy_ref, dyn_ref, cw_ref, wt_ref, x_ref, g_ref, tok_ref, dres_ref,
             dp_ref, dcw_ref, dx_ref, dgn_ref, ext, ext2):
        i = pl.program_id(0)
        first = (i % nst) == 0
        last = (i % nst) == nst - 1

        @pl.when(i == 0)
        def _():
            dgn_ref[...] = jnp.zeros(dgn_ref.shape, f32)
            dcw_ref[...] = jnp.zeros(dcw_ref.shape, f32)

        for c0 in range(0, D, COLS):
            cs, cc, cx = slice(c0, c0 + COLS), slice(D + c0, D + c0 + COLS), slice(2 * D + c0, 2 * D + c0 + COLS)
            gb, gc, xt = p_ref[:, cs].astype(f32), p_ref[:, cc].astype(f32), p_ref[:, cx].astype(f32)
            q = gc * xt
            ext[0:HALO, cs] = jnp.where(first, 0.0, ph_ref[:, cc].astype(f32) * ph_ref[:, cx].astype(f32))
            ext[HALO:, cs] = q
            q2 = ext[pl.ds(HALO - 2, tm), cs]
            q1 = ext[pl.ds(HALO - 1, tm), cs]
            y = cw_ref[0:1, cs] * q2 + cw_ref[1:2, cs] * q1 + cw_ref[2:3, cs] * q
            dyo = dy_ref[:, cs].astype(f32)
            dp_ref[:, cs] = (dyo * y).astype(bf16)
            dyc = dyo * gb
            ext2[0:tm, cs] = dyc
            ext2[tm:, cs] = jnp.where(last, 0.0, dyn_ref[:, cs].astype(f32) * pn_ref[:, cs].astype(f32))
            dq = cw_ref[2:3, cs] * dyc + cw_ref[1:2, cs] * ext2[pl.ds(1, tm), cs] + cw_ref[0:1, cs] * ext2[pl.ds(2, tm), cs]
            dp_ref[:, cc] = (dq * xt).astype(bf16)
            dp_ref[:, cx] = (dq * gc).astype(bf16)
            dcw_ref[0:1, cs] += jnp.sum(dyc * q2, axis=0, keepdims=True)
            dcw_ref[1:2, cs] += jnp.sum(dyc * q1, axis=0, keepdims=True)
            dcw_ref[2:3, cs] += jnp.sum(dyc * q, axis=0, keepdims=True)
        _normbwd_tail(_dot(dp_ref[...], wt_ref[...]), x_ref, g_ref, tok_ref, dres_ref, dx_ref, dgn_ref)

    tail_in, tail_out = _normbwd_specs(tm, wt)
    return pl.pallas_call(
        body, grid=(T // tm,),
        in_specs=[pl.BlockSpec((tm, 3 * D), lambda i: (i, 0)), _prev_halo_spec(tm, 3 * D), _next_halo_spec(tm, 3 * D, T),
                  pl.BlockSpec((tm, D), lambda i: (i, 0)), _next_halo_spec(tm, D, T),
                  pl.BlockSpec((3, D), lambda i: (0, 0))] + tail_in,
        out_specs=[pl.BlockSpec((tm, 3 * D), lambda i: (i, 0)), pl.BlockSpec((3, D), lambda i: (0, 0))] + tail_out,
        out_shape=[jax.ShapeDtypeStruct((T, 3 * D), bf16), jax.ShapeDtypeStruct((3, D), f32),
                   jax.ShapeDtypeStruct((T, D), f32), jax.ShapeDtypeStruct((1, D), f32)],
        scratch_shapes=[pltpu.VMEM((tm + HALO, D), f32), pltpu.VMEM((tm + HALO, D), f32)],
        compiler_params=_cparams(("arbitrary",)), name=name,
    )(p, p, p, dy, dy, cw, wt, x, g, tok, dres)


def _pool_counts(i, nst, tm, rows, row0, w):
    t = (i % nst) * tm + row0 + lax.broadcasted_iota(jnp.int32, (rows, 1), 0)
    return jnp.minimum(t + 1, w).astype(f32)


def _pool_diff(p_ref, ext, g, i, nst, tm):
    w = POOL_WINDOWS[g]
    cs = slice(g * C_GDIM, (g + 1) * C_GDIM)
    pg = p_ref[:, cs].astype(f32)
    s = pg
    for k in range(1, w):
        s = s + ext[pl.ds(HALO - k, tm), cs]
    return s / _pool_counts(i, nst, tm, tm, 0, w) - pg


def c_mid_fwd(p, wg, scale, *, tm, S, name):
    T = p.shape[0]
    nst = S // tm

    def body(p_ref, ph_ref, wg_ref, sc_ref, y_ref, ext):
        i = pl.program_id(0)
        first = (i % nst) == 0
        ext[0:HALO, :] = jnp.where(first, 0.0, ph_ref[...].astype(f32))
        ext[HALO:, :] = p_ref[...].astype(f32)
        for g in range(C_GROUPS):
            cs = slice(g * C_GDIM, (g + 1) * C_GDIM)
            dg = _pool_diff(p_ref, ext, g, i, nst, tm).astype(bf16)
            wv = wg_ref[:, g].reshape(C_GDIM, C_GDIM)
            y_ref[:, cs] = (_dot(dg, wv) * sc_ref[:, cs]).astype(bf16)

    return pl.pallas_call(
        body, grid=(T // tm,),
        in_specs=[pl.BlockSpec((tm, D), lambda i: (i, 0)), _prev_halo_spec(tm, D),
                  pl.BlockSpec((NDEV, C_GROUPS, C_GDIM // NDEV, C_GDIM), lambda i: (0, 0, 0, 0)),
                  pl.BlockSpec((1, D), lambda i: (0, 0))],
        out_specs=pl.BlockSpec((tm, D), lambda i: (i, 0)),
        out_shape=jax.ShapeDtypeStruct((T, D), bf16),
        scratch_shapes=[pltpu.VMEM((tm + HALO, D), f32)],
        compiler_params=_cparams(("parallel",)), name=name,
    )(p, p, wg, scale)


def c_mid_bwd(p, dy, wg, scale, w, x, g, tok, dres, *, tm, S, name):
    T = p.shape[0]
    nst = S // tm

    def body(p_ref, ph_ref, dy_ref, dyn_ref, wg_ref, sc_ref, w_ref, x_ref, g_ref, tok_ref, dres_ref,
             dp_ref, dsc_ref, dwg_ref, dx_ref, dgn_ref, ext, ext2):
        i = pl.program_id(0)
        first = (i % nst) == 0
        last = (i % nst) == nst - 1

        @pl.when(i == 0)
        def _():
            dgn_ref[...] = jnp.zeros(dgn_ref.shape, f32)
            dsc_ref[...] = jnp.zeros(dsc_ref.shape, f32)
            dwg_ref[...] = jnp.zeros(dwg_ref.shape, f32)

        ext[0:HALO, :] = jnp.where(first, 0.0, ph_ref[...].astype(f32))
        ext[HALO:, :] = p_ref[...].astype(f32)
        for g in range(C_GROUPS):
            w = POOL_WINDOWS[g]
            cs = slice(g * C_GDIM, (g + 1) * C_GDIM)
            dg = _pool_diff(p_ref, ext, g, i, nst, tm).astype(bf16)
            wv = wg_ref[:, g].reshape(C_GDIM, C_GDIM)
            dyo = dy_ref[:, cs].astype(f32)
            dsc_ref[:, cs] += jnp.sum(dyo * _dot(dg, wv), axis=0, keepdims=True)
            dyp = (dyo * sc_ref[:, cs]).astype(bf16)
            dypn = (dyn_ref[:, cs].astype(f32) * sc_ref[:, cs]).astype(bf16)
            dwg_ref[g] += _dot_tn(dg, dyp)
            dd = _dot_nt(dyp, wv)
            ddn = _dot_nt(dypn, wv)
            ext2[0:tm, cs] = dd / _pool_counts(i, nst, tm, tm, 0, w)
            ext2[tm:, cs] = jnp.where(last, 0.0, ddn / _pool_counts(i, nst, tm, HALO, tm, w))
            s = -dd
            for k in range(w):
                s = s + ext2[pl.ds(k, tm), cs]
            dp_ref[:, cs] = s.astype(bf16)
        _normbwd_tail(_dot_nt(dp_ref[...], w_ref[...]), x_ref, g_ref, tok_ref, dres_ref, dx_ref, dgn_ref)

    tail_in, tail_out = _normbwd_specs(tm, w)
    return pl.pallas_call(
        body, grid=(T // tm,),
        in_specs=[pl.BlockSpec((tm, D), lambda i: (i, 0)), _prev_halo_spec(tm, D),
                  pl.BlockSpec((tm, D), lambda i: (i, 0)), _next_halo_spec(tm, D, T),
                  pl.BlockSpec((NDEV, C_GROUPS, C_GDIM // NDEV, C_GDIM), lambda i: (0, 0, 0, 0)),
                  pl.BlockSpec((1, D), lambda i: (0, 0))] + tail_in,
        out_specs=[pl.BlockSpec((tm, D), lambda i: (i, 0)), pl.BlockSpec((1, D), lambda i: (0, 0)),
                   pl.BlockSpec((C_GROUPS, C_GDIM, C_GDIM), lambda i: (0, 0, 0))] + tail_out,
        out_shape=[jax.ShapeDtypeStruct((T, D), bf16), jax.ShapeDtypeStruct((1, D), f32),
                   jax.ShapeDtypeStruct((C_GROUPS, C_GDIM, C_GDIM), f32),
                   jax.ShapeDtypeStruct((T, D), f32), jax.ShapeDtypeStruct((1, D), f32)],
        scratch_shapes=[pltpu.VMEM((tm + HALO, D), f32), pltpu.VMEM((tm + HALO, D), f32)],
        compiler_params=_cparams(("arbitrary",)), name=name,
    )(p, p, dy, dy, wg, scale, w, x, g, tok, dres)


def final_loss(x, g, tgt, *, tm, name):
    T = x.shape[0]

    def body(x_ref, g_ref, t_ref, loss_ref, dx_ref, dg_ref):
        @pl.when(pl.program_id(0) == 0)
        def _():
            loss_ref[...] = jnp.zeros(loss_ref.shape, f32)
            dg_ref[...] = jnp.zeros(dg_ref.shape, f32)

        xv = x_ref[...]
        r = lax.rsqrt(jnp.mean(xv * xv, axis=-1, keepdims=True) + EPS)
        xhat = xv * r
        err = xhat * g_ref[...] - t_ref[...]
        loss_ref[...] += 0.5 * jnp.sum(jnp.mean(err * err, axis=-1, keepdims=True))
        dy = err * (1.0 / D)
        dg_ref[...] += jnp.sum(dy * xhat, axis=0, keepdims=True)
        dxhat = dy * g_ref[...]
        dx_ref[...] = r * (dxhat - xhat * jnp.mean(dxhat * xhat, axis=-1, keepdims=True))

    return pl.pallas_call(
        body, grid=(T // tm,),
        in_specs=[pl.BlockSpec((tm, D), lambda i: (i, 0)), pl.BlockSpec((1, D), lambda i: (0, 0)),
                  pl.BlockSpec((tm, D), lambda i: (i, 0))],
        out_specs=[pl.BlockSpec((8, 128), lambda i: (0, 0)), pl.BlockSpec((tm, D), lambda i: (i, 0)),
                   pl.BlockSpec((1, D), lambda i: (0, 0))],
        out_shape=[jax.ShapeDtypeStruct((8, 128), f32), jax.ShapeDtypeStruct((T, D), f32),
                   jax.ShapeDtypeStruct((1, D), f32)],
        compiler_params=_cparams(("arbitrary",)), name=name,
    )(x, g, tgt)


def _slot(px, py, pc):
    return 4 * px + 2 * py + pc


def _with_own_block(s, me):
    zone = lax.empty((NDEV,) + s.shape, s.dtype)
    return lax.dynamic_update_slice(zone, s[None], (me,) + (0,) * s.ndim)


def all_gather(arrs, me, *, name):
    n = len(arrs)

    def body(*refs):
        ins, outs = refs[:n], refs[2 * n:3 * n]
        send_sems, recv_sems = refs[3 * n:]
        x, y, c = lax.axis_index("x"), lax.axis_index("y"), lax.axis_index("c")
        me, sibling = (x, y, c), (x, y, 1 - c)
        chips = [(1 - x, y), (x, 1 - y), (1 - x, 1 - y)]

        def copy(a, k, block, to, src=None):
            dst = outs[a].at[_slot(*block)]
            return pltpu.make_async_remote_copy(
                src_ref=dst if src is None else src, dst_ref=dst,
                send_sem=send_sems.at[a, k], recv_sem=recv_sems.at[a, k], device_id=to, device_id_type=MESH)

        first = []
        for a in range(n):
            first.append(copy(a, 0, me, sibling, src=ins[a]))
            first += [copy(a, 1 + j, me, (*chip, c), src=ins[a]) for j, chip in enumerate(chips)]
        for cp in first:
            cp.start()
        passed = []
        for j, chip in enumerate(chips):
            for a in range(n):
                copy(a, 1 + j, (*chip, c), me).wait_recv()
                fwd = copy(a, 4 + j, (*chip, c), sibling)
                fwd.start()
                passed.append(fwd)
        for a in range(n):
            copy(a, 0, sibling, me).wait_recv()
        for j, chip in enumerate(chips):
            for a in range(n):
                copy(a, 4 + j, (*chip, 1 - c), me).wait_recv()
        for cp in first + passed:
            cp.wait_send()

    any_spec = pl.BlockSpec(memory_space=pl.ANY)
    return pl.pallas_call(
        body,
        in_specs=[any_spec] * (2 * n), out_specs=[any_spec] * n,
        out_shape=[jax.ShapeDtypeStruct((NDEV,) + a.shape, a.dtype) for a in arrs],
        input_output_aliases={n + i: i for i in range(n)},
        scratch_shapes=[pltpu.SemaphoreType.DMA((n, 7)), pltpu.SemaphoreType.DMA((n, 7))],
        compiler_params=pltpu.CompilerParams(has_side_effects=True), name=name,
    )(*arrs, *[_with_own_block(a, me) for a in arrs])


HBM_SPEC = pl.BlockSpec(memory_space=pltpu.HBM)
SEM_SPEC = pl.BlockSpec(memory_space=pltpu.SEMAPHORE)
ANY_SPEC = pl.BlockSpec(memory_space=pl.ANY)
TOKEN_SHAPE = jax.ShapeDtypeStruct((8, 128), f32)
DATAFLOW_EFFECT = pltpu.SideEffectType.DATAFLOW_SIDE_EFFECTING


def _in_hbm(a):
    return pltpu.with_memory_space_constraint(a, pltpu.HBM)


def _hbm_like(a):
    return pltpu.HBM(a.shape, a.dtype)


def _mesh_pos():
    return lax.axis_index("x"), lax.axis_index("y"), lax.axis_index("c")


def _gather_targets(x, y, c):
    return [(x, y, 1 - c), (1 - x, y, c), (x, 1 - y, c), (1 - x, 1 - y, c)]


def gather_start(shards, me, after, *, name):
    n = len(shards)
    extra = [] if after is None else [after]

    def body(*refs):
        srcs, lands = refs[:n], refs[n:2 * n]
        send_sems, recv_sems = refs[2 * n + len(extra)], refs[2 * n + len(extra) + 1]
        token = refs[-1]
        x, y, c = _mesh_pos()
        me = _slot(x, y, c)
        for a in range(n):
            for k, to in enumerate(_gather_targets(x, y, c)):
                pltpu.make_async_remote_copy(
                    src_ref=srcs[a], dst_ref=lands[a].at[me], send_sem=send_sems.at[4 * a + k], recv_sem=recv_sems.at[4 * a + k],
                    device_id=to, device_id_type=MESH).start()
        token[...] = jnp.zeros(token.shape, f32)

    lands = [_with_own_block(s, me) for s in shards]
    sems = pltpu.SemaphoreType.DMA((4 * n,))
    out = pl.pallas_call(
        body, name=name,
        in_specs=[HBM_SPEC] * (2 * n) + [ANY_SPEC] * len(extra),
        out_specs=[SEM_SPEC, SEM_SPEC] + [HBM_SPEC] * (2 * n) + [pl.BlockSpec(memory_space=pltpu.VMEM)],
        out_shape=[sems, sems] + [_hbm_like(s) for s in shards] + [_hbm_like(l) for l in lands] + [TOKEN_SHAPE],
        input_output_aliases={i: 2 + i for i in range(2 * n)},
        compiler_params=pltpu.CompilerParams(has_side_effects=DATAFLOW_EFFECT),
    )(*[_in_hbm(s) for s in shards], *[_in_hbm(l) for l in lands], *extra)
    return out[0], out[1], out[2:2 + n], out[2 + n:2 + 2 * n], out[-1]


def gather_wait(send_sems, recv_sems, shards, lands, after, *, name):
    n = len(shards)
    after = list(after) if isinstance(after, (list, tuple)) else [after]

    def body(*refs):
        srcs, lands_in = refs[:n], refs[n:2 * n]
        send_sems, recv_sems = refs[2 * n], refs[2 * n + 1]
        x, y, c = _mesh_pos()
        for a in range(n):
            for k, frm in enumerate(_gather_targets(x, y, c)):
                cp = pltpu.make_async_remote_copy(
                    src_ref=srcs[a], dst_ref=lands_in[a].at[_slot(*frm)], send_sem=send_sems.at[4 * a + k],
                    recv_sem=recv_sems.at[4 * a + k], device_id=frm, device_id_type=MESH)
                cp.wait_send()
                cp.wait_recv()

    out = pl.pallas_call(
        body, name=name,
        in_specs=[HBM_SPEC] * (2 * n) + [SEM_SPEC, SEM_SPEC] + [ANY_SPEC] * len(after),
        out_specs=[HBM_SPEC] * (2 * n),
        out_shape=[_hbm_like(s) for s in shards] + [_hbm_like(l) for l in lands],
        input_output_aliases={i: i for i in range(2 * n)},
        compiler_params=pltpu.CompilerParams(has_side_effects=DATAFLOW_EFFECT),
    )(*shards, *lands, send_sems, recv_sems, *after)
    return out[:n], out[n:]


def gather_finish(lands, *, name):
    n = len(lands)

    def body(*refs):
        lands_in, lands_out = refs[:n], refs[n:2 * n]
        send_sems, recv_sems = refs[2 * n:]
        x, y, c = _mesh_pos()
        sibling = (x, y, 1 - c)
        chips = [(1 - x, y), (x, 1 - y), (1 - x, 1 - y)]
        sent = []
        for a in range(n):
            for j, chip in enumerate(chips):
                s = _slot(*chip, c)
                cp = pltpu.make_async_remote_copy(
                    src_ref=lands_in[a].at[s], dst_ref=lands_out[a].at[s], send_sem=send_sems.at[a, j],
                    recv_sem=recv_sems.at[a, j], device_id=sibling, device_id_type=MESH)
                cp.start()
                sent.append(cp)
        for a in range(n):
            for j, chip in enumerate(chips):
                s = _slot(*chip, 1 - c)
                pltpu.make_async_remote_copy(
                    src_ref=lands_in[a].at[s], dst_ref=lands_out[a].at[s], send_sem=send_sems.at[a, j],
                    recv_sem=recv_sems.at[a, j], device_id=sibling, device_id_type=MESH).wait_recv()
        for cp in sent:
            cp.wait_send()

    return pl.pallas_call(
        body, name=name,
        in_specs=[ANY_SPEC] * n, out_specs=[ANY_SPEC] * n,
        out_shape=[jax.ShapeDtypeStruct(l.shape, l.dtype) for l in lands],
        input_output_aliases={i: i for i in range(n)},
        scratch_shapes=[pltpu.SemaphoreType.DMA((n, 3)), pltpu.SemaphoreType.DMA((n, 3))],
        compiler_params=pltpu.CompilerParams(has_side_effects=True),
    )(*lands)


def _peer(x, y, c, r):
    return (x ^ ((r >> 2) & 1), y ^ ((r >> 1) & 1), c ^ (r & 1))


def exchange_start(items, lands, after, *, name):
    n, m = len(items), len(lands)
    extra = [] if after is None else [after]

    def body(*refs):
        gs, zones = refs[:n], refs[n:n + m]
        send_sems, recv_sems = refs[n + m + len(extra)], refs[n + m + len(extra) + 1]
        token = refs[-1]
        x, y, c = _mesh_pos()
        me = _slot(x, y, c)
        for r in (1, 2, 4, 3, 5, 6, 7):
            to = _peer(x, y, c, r)
            for a, (_, zi, l) in enumerate(items):
                pltpu.make_async_remote_copy(
                    src_ref=gs[a].at[_slot(*to)], dst_ref=zones[zi].at[me, l], send_sem=send_sems.at[7 * a + r - 1],
                    recv_sem=recv_sems.at[7 * a + r - 1], device_id=to, device_id_type=MESH).start()
        token[...] = jnp.zeros(token.shape, f32)

    gs = [g for g, _, _ in items]
    sems = pltpu.SemaphoreType.DMA((7 * n,))
    out = pl.pallas_call(
        body, name=name,
        in_specs=[HBM_SPEC] * (n + m) + [ANY_SPEC] * len(extra),
        out_specs=[SEM_SPEC, SEM_SPEC] + [HBM_SPEC] * (n + m) + [pl.BlockSpec(memory_space=pltpu.VMEM)],
        out_shape=[sems, sems] + [_hbm_like(g) for g in gs] + [_hbm_like(z) for z in lands] + [TOKEN_SHAPE],
        input_output_aliases={i: 2 + i for i in range(n + m)},
        compiler_params=pltpu.CompilerParams(has_side_effects=DATAFLOW_EFFECT),
    )(*[_in_hbm(g) for g in gs], *[_in_hbm(z) for z in lands], *extra)
    return out[0], out[1], out[2:2 + n], out[2 + n:2 + n + m], out[-1]


def exchange_wait(parts, lands, after, *, name):
    m = len(lands)
    flat_gs = [g for _, _, _, gs in parts for g in gs]
    ng = len(flat_gs)
    after = list(after) if isinstance(after, (list, tuple)) else [after]

    def body(*refs):
        gs, zones = refs[:ng], refs[ng:ng + m]
        sem_refs = refs[ng + m:ng + m + 2 * len(parts)]
        x, y, c = _mesh_pos()
        base = 0
        for pi, (items, _, _, _) in enumerate(parts):
            send_sems, recv_sems = sem_refs[2 * pi], sem_refs[2 * pi + 1]
            for r in range(1, NDEV):
                frm = _peer(x, y, c, r)
                for a, (_, zi, l) in enumerate(items):
                    cp = pltpu.make_async_remote_copy(
                        src_ref=gs[base + a].at[_slot(*frm)], dst_ref=zones[zi].at[_slot(*frm), l],
                        send_sem=send_sems.at[7 * a + r - 1], recv_sem=recv_sems.at[7 * a + r - 1],
                        device_id=frm, device_id_type=MESH)
                    cp.wait_send()
                    cp.wait_recv()
            base += len(items)

    sem_args = [s for _, ss, rs, _ in parts for s in (ss, rs)]
    out = pl.pallas_call(
        body, name=name,
        in_specs=[HBM_SPEC] * (ng + m) + [SEM_SPEC] * len(sem_args) + [ANY_SPEC] * len(after),
        out_specs=[HBM_SPEC] * (ng + m),
        out_shape=[_hbm_like(g) for g in flat_gs] + [_hbm_like(z) for z in lands],
        input_output_aliases={i: i for i in range(ng + m)},
        compiler_params=pltpu.CompilerParams(has_side_effects=DATAFLOW_EFFECT),
    )(*flat_gs, *lands, *sem_args, *after)
    return out[ng:]


def sum8(g, *, name):
    _, R, C = g.shape
    tr = R

    def body(g_ref, o_ref):
        s = g_ref[0]
        for k in range(1, NDEV):
            s = s + g_ref[k]
        o_ref[...] = s

    return pl.pallas_call(
        body, grid=(R // tr,),
        in_specs=[pl.BlockSpec((NDEV, tr, C), lambda i: (0, i, 0))],
        out_specs=pl.BlockSpec((tr, C), lambda i: (i, 0)),
        out_shape=jax.ShapeDtypeStruct((R, C), f32),
        compiler_params=_cparams(("parallel",)), name=name,
    )(g)


def _adam_math(w, g, m, v):
    m = ADAM_B1 * m + (1.0 - ADAM_B1) * g
    v = ADAM_B2 * v + (1.0 - ADAM_B2) * (g * g)
    m_hat = m / (1.0 - ADAM_B1 ** ADAM_STEP)
    v_hat = v / (1.0 - ADAM_B2 ** ADAM_STEP)
    delta = -ADAM_LR * (m_hat / (jnp.sqrt(v_hat) + ADAM_EPS) + ADAM_WD * w)
    return delta, m, v


def adamw_slices(w, m, v, recv, *, row0=0, col0=0, rrows=None, tr=None, tc=None, l0=0, nl=None, prev=None, name):
    L, R, C = w.shape
    Rp, Cp = recv.shape[2], recv.shape[3]
    nl = L - l0 if nl is None else nl
    tr = R if tr is None else tr
    tc = C if tc is None else tc
    rrows = Rp if rrows is None else rrows
    assert R % tr == 0 and C % tc == 0 and row0 % rrows == 0 and R <= rrows
    assert (tr == R or (Rp == R and row0 == 0)) and (tc == C or (Cp == C and col0 == 0))
    rr = rrows if tr == R else tr
    rc = Cp if tc == C else tc
    rb = row0 // rrows if tr == R else 0
    prevs = [] if prev is None else list(prev)

    def body(w_ref, m_ref, v_ref, r_ref, *rest):
        g_ref, d_ref, nm_ref, nv_ref = rest[len(prevs):]
        c0 = col0 if tc == C else 0
        g = r_ref[0, 0:tr, c0:c0 + tc].astype(f32)
        for k in range(1, NDEV):
            g = g + r_ref[k, 0:tr, c0:c0 + tc].astype(f32)
        delta, nm, nv = _adam_math(w_ref[...], g, m_ref[...], v_ref[...])
        g_ref[...] = g
        d_ref[...] = delta
        nm_ref[...] = nm
        nv_ref[...] = nv

    wspec = pl.BlockSpec((None, tr, tc), lambda l, i, j: (l0 + l, i, j))
    return pl.pallas_call(
        body, grid=(nl, R // tr, C // tc),
        in_specs=[wspec, wspec, wspec, pl.BlockSpec((NDEV, None, rr, rc), lambda l, i, j: (0, l0 + l, rb + i, j))]
        + [pl.BlockSpec(memory_space=pl.ANY)] * len(prevs),
        out_specs=[wspec] * 4, out_shape=[jax.ShapeDtypeStruct((L, R, C), f32)] * 4,
        input_output_aliases={4 + q: q for q in range(len(prevs))},
        compiler_params=_cparams(("parallel", "parallel", "parallel")), name=name,
    )(w, m, v, recv, *prevs)


def adamw_slices_t(w, m, v, recv, *, name):
    L, R, C = w.shape

    def body(w_ref, m_ref, v_ref, r_ref, g_ref, d_ref, nm_ref, nv_ref):
        gt = r_ref[0].astype(f32)
        for k in range(1, NDEV):
            gt = gt + r_ref[k].astype(f32)
        g = gt.T
        delta, nm, nv = _adam_math(w_ref[...], g, m_ref[...], v_ref[...])
        g_ref[...] = g
        d_ref[...] = delta
        nm_ref[...] = nm
        nv_ref[...] = nv

    wspec = pl.BlockSpec((None, R, C), lambda l: (l, 0, 0))
    return pl.pallas_call(
        body, grid=(L,),
        in_specs=[wspec, wspec, wspec, pl.BlockSpec((NDEV, None, C, R), lambda l: (0, l, 0, 0))],
        out_specs=[wspec] * 4, out_shape=[jax.ShapeDtypeStruct((L, R, C), f32)] * 4,
        compiler_params=_cparams(("parallel",)), name=name,
    )(w, m, v, recv)


def adamw_plain(w, m, v, g, *, name):
    R, C = w.shape

    def body(w_ref, m_ref, v_ref, g_ref, d_ref, nm_ref, nv_ref):
        delta, nm, nv = _adam_math(w_ref[...], g_ref[...], m_ref[...], v_ref[...])
        d_ref[...] = delta
        nm_ref[...] = nm
        nv_ref[...] = nv

    return pl.pallas_call(
        body, out_shape=[jax.ShapeDtypeStruct((R, C), f32)] * 3, name=name,
    )(w, m, v, g)


def _chunk_causal_mask():
    pos = jnp.arange(GBLK)
    return (pos[None, :] // CHUNK) <= (pos[:, None] // CHUNK)


def kernel(x, norm_mix_g, norm_ffn_g, final_norm_g, a_w_in, a_v_norm_g, a_w_s, a_b_s, a_w_out, b_w_in, b_conv_w, b_w_out, c_w_in, c_w_grp, c_scale, c_w_out, f_w_gate, f_w_up, f_w_down, loss_target, m_norm_mix_g, m_norm_ffn_g, m_final_norm_g, m_a_w_in, m_a_v_norm_g, m_a_w_s, m_a_b_s, m_a_w_out, m_b_w_in, m_b_conv_w, m_b_w_out, m_c_w_in, m_c_w_grp, m_c_scale, m_c_w_out, m_f_w_gate, m_f_w_up, m_f_w_down, v_norm_mix_g, v_norm_ffn_g, v_final_norm_g, v_a_w_in, v_a_v_norm_g, v_a_w_s, v_a_b_s, v_a_w_out, v_b_w_in, v_b_conv_w, v_b_w_out, v_c_w_in, v_c_w_grp, v_c_scale, v_c_w_out, v_f_w_gate, v_f_w_up, v_f_w_down):
    B, S, _ = x.shape
    T = B * S
    tm = min(512, S)
    tmm = min(1024, T)
    tms = min(512, T)
    tma = min(256, S)
    tmf = min(256, T)
    me = _slot(lax.axis_index("x"), lax.axis_index("y"), lax.axis_index("c"))

    def tb(w):
        return jnp.transpose(w, (0, 2, 1)).astype(bf16)

    small = jnp.concatenate([a_v_norm_g, b_conv_w[0], c_scale, jnp.zeros((2, GBLK), f32)], axis=0)
    started = [gather_start([tb(a_w_in[0:1])[0], a_w_out[0].astype(bf16), small], me, None, name="gather_start_0")]
    tok = started[0][4]
    z = tok[0, 0]
    f_g, f_u, f_d = tb(f_w_gate + z), tb(f_w_up + z), (f_w_down + z).astype(bf16)
    stage_shards = [
        None,
        [f_g[0], f_u[0], f_d[0]],
        [tb(b_w_in + z)[0], (b_w_out[0] + z).astype(bf16), f_g[1], f_u[1], f_d[1]],
        [(c_w_in[0] + z).astype(bf16), (c_w_grp[0] + z).astype(bf16), (c_w_out[0] + z).astype(bf16), f_g[2], f_u[2], f_d[2]],
        [tb(a_w_in[1:2] + z)[0], (a_w_out[1] + z).astype(bf16), f_g[3], f_u[3], f_d[3]],
    ]
    for s in range(1, len(stage_shards)):
        started.append(gather_start(stage_shards[s], me, tok, name=f"gather_start_{s}"))
        tok = started[-1][4]

    def stage_weights(s, after):
        send_sems, recv_sems, shards, lands, _ = started[s]
        _, lands = gather_wait(send_sems, recv_sems, shards, lands, after, name=f"gather_wait_{s}")
        lands = gather_finish(lands, name=f"gather_finish_{s}")
        return [t.reshape(-1, t.shape[-1]) if t.ndim == 3 else t for t in lands]

    wm_masked = jnp.where(_chunk_causal_mask()[None, None], a_w_s, 0.0).astype(bf16)
    bsb = jnp.broadcast_to(a_b_s[:, :, :, None], a_b_s.shape + (GBLK,))

    xc = x.reshape(T, D)
    saved, weights = [], []
    for i in range(DEPTH):
        kind, j = i % 3, i // 3
        gmix = norm_mix_g[i][None]
        if i == 0:
            w_in, w_out, small_g = stage_weights(0, tok)
            small_full = jnp.transpose(small_g.reshape(NDEV, 8, GBLK), (1, 0, 2)).reshape(8, D)
            gv_full, cw_full, cs_full = small_full[0:2], small_full[2:5], small_full[5:6]
            w = [w_in, w_out]
        else:
            w = stage_weights(i + 1, xc)
        if kind == 0:
            w_in, w_out = w[:2]
            p, h = norm_mm(xc, gmix, w_in, tm=tms, transposed=True, name=f"a_in_{i}")
            y = a_mid_fwd(p, gv_full[j][None], wm_masked[j], bsb[j], tm=tma, name=f"a_mid_{i}")
        elif kind == 1:
            w_in, w_out = w[:2]
            p, h = norm_mm(xc, gmix, w_in, tm=tms, transposed=True, name=f"b_in_{i}")
            y = b_mid_fwd(p, cw_full, tm=tm, S=S, name=f"b_mid_{i}")
        else:
            w_in, w_grp, w_out = w[:3]
            p, h = norm_mm(xc, gmix, w_in, tm=tms, transposed=False, name=f"c_in_{i}")
            y = c_mid_fwd(p, w_grp, cs_full, tm=tm, S=S, name=f"c_mid_{i}")
        if i == 0:
            w = w + stage_weights(1, y)
        weights.append(w)
        w_fg, w_fu, w_fd = w[-3:]
        x1, x2, pg, pu, a, h2 = ffn_fwd(y, w_out, xc, norm_ffn_g[i][None], w_fg, w_fu, w_fd, tm=tmf, name=f"f_fwd_{i}")
        saved.append((xc, p, h, y, x1, pg, pu, h2, a))
        xc = x2

    loss_blk, dx, d_final = final_loss(xc, final_norm_g[None], loss_target.reshape(T, D), tm=tms, name="final_loss")
    loss = lax.psum(loss_blk[0, 0], AXES)

    zone_shapes = [(2, 256, D), (2, GBLK, D), (1, 384, D), (1, GBLK, D), (1, GBLK, D), (1, GBLK, C_GDIM), (1, GBLK, D),
                   (DEPTH, F_LOC, D), (DEPTH, F_LOC, D), (DEPTH, F_LOC, D)]
    Z_AIN, Z_AOUT, Z_BIN, Z_BOUT, Z_CIN, Z_CGRP, Z_COUT, Z_FG, Z_FU, Z_FD = range(10)
    lands = [lax.empty((NDEV,) + s, bf16) for s in zone_shapes]
    parts = []

    def exchange(items, after, name):
        nonlocal lands
        for g, zi, l in items:
            own = lax.dynamic_index_in_dim(g, me, axis=0, keepdims=True)[None]
            lands[zi] = lax.dynamic_update_slice(lands[zi], own, (me, l) + (0,) * (g.ndim - 1))
        idx = sorted({zi for _, zi, _ in items})
        local = [(g, idx.index(zi), l) for g, zi, l in items]
        send_sems, recv_sems, gs, zs, token = exchange_start(local, [lands[zi] for zi in idx], after, name=name)
        for q, zi in enumerate(idx):
            lands[zi] = zs[q]
        parts.append(([(None, zi, l) for _, zi, l in items], send_sems, recv_sems, gs))
        return token

    d_mix, d_ffn = [None] * DEPTH, [None] * DEPTH
    d_gv, d_wm, d_bs = [None] * 2, [None] * 2, [None] * 2
    tok = None
    for i in reversed(range(DEPTH)):
        kind, j = i % 3, i // 3
        xin, p, h, y, x1, pg, pu, h2, a = saved[i]
        w = weights[i]
        w_in, w_out = w[0], w[-4]
        w_fg, w_fu, w_fd = w[-3:]
        dpg, dpu, dxb, dx1, d_ffn[i], dy, dx1b = ffn_bwd(dx, w_fd, pg, pu, w_fg, w_fu, x1, norm_ffn_g[i][None], tok, w_out,
                                                         tm=tmf, name=f"f_bwd_{i}")
        gw_g, gw_u, gw_d = ffn_dw(dpg, dpu, a, h2, dxb, name=f"f_dw_{i}")
        tok = exchange([(gw_g, Z_FG, i), (gw_u, Z_FU, i), (gw_d, Z_FD, i)], tok, f"exchange_start_ffn_{i}")
        gmix = norm_mix_g[i][None]
        if kind == 0:
            dp, d_gv[j], d_wm[j], d_bs[j], dx, d_mix[i] = a_mid_bwd(
                p, dy, gv_full[j][None], wm_masked[j], bsb[j], w_in, xin, gmix, tok, dx1, tm=tma, name=f"a_bwd_{i}")
            items = [(mm_tn(dp, h, name=f"a_dw_in_{i}"), Z_AIN, j),
                     (mm_tn(y, dx1b, name=f"a_dw_out_{i}"), Z_AOUT, j)]
        elif kind == 1:
            dp, d_cw, dx, d_mix[i] = b_mid_bwd(p, dy, cw_full, w_in, xin, gmix, tok, dx1, tm=tm, S=S, name=f"b_bwd_{i}")
            items = [(mm_tn(dp, h, name=f"b_dw_in_{i}"), Z_BIN, 0),
                     (mm_tn(y, dx1b, name=f"b_dw_out_{i}"), Z_BOUT, 0)]
        else:
            dp, d_cs, d_wgrp, dx, d_mix[i] = c_mid_bwd(p, dy, w[1], cs_full, w_in, xin, gmix, tok, dx1, tm=tm, S=S,
                                                       name=f"c_bwd_{i}")
            gw_cgrp = jnp.transpose(d_wgrp.reshape(C_GROUPS, NDEV, C_GDIM // NDEV, C_GDIM), (1, 0, 2, 3)).astype(bf16)
            items = [(mm_tn(h, dp, name=f"c_dw_in_{i}"), Z_CIN, 0),
                     (gw_cgrp.reshape(NDEV, GBLK, C_GDIM), Z_CGRP, 0),
                     (mm_tn(y, dx1b, name=f"c_dw_out_{i}"), Z_COUT, 0)]
        tok = exchange(items, tok, f"exchange_start_mix_{i}")
    grad_x = dx.reshape(B, S, D)

    mask = _chunk_causal_mask()
    small_parts = [
        jnp.concatenate(d_mix, axis=0).reshape(-1, GBLK),
        jnp.concatenate(d_ffn, axis=0).reshape(-1, GBLK),
        d_final.reshape(-1, GBLK),
        jnp.concatenate(d_gv, axis=0).reshape(-1, GBLK),
        jnp.where(mask[None, None], jnp.stack(d_wm), 0.0).reshape(-1, GBLK),
        jnp.stack(d_bs).reshape(-1, GBLK),
        d_cw.reshape(-1, GBLK),
        d_cs.reshape(-1, GBLK),
    ]
    small_rows = [q.shape[0] for q in small_parts]
    sg_send, sg_recv, sg_shards, sg_lands, _ = gather_start(
        [jnp.concatenate(small_parts, axis=0)], me, tok, name="gather_small_start")

    def plain(w, m, v, g, name):
        shp = w.shape
        w2, m2, v2, g2 = (t.reshape(-1, shp[-1]) for t in (w, m, v, g))
        return tuple(t.reshape(shp) for t in adamw_plain(w2, m2, v2, g2, name=name))

    def sliced(w, m, v, recv, name, **kw):
        shp = w.shape
        w3, m3, v3 = (t.reshape((shp[0], -1, shp[-1])) for t in (w, m, v))
        return tuple(t.reshape(shp) for t in adamw_slices(w3, m3, v3, recv, name=name, **kw))

    def sliced_t(w, m, v, recv, name, **kw):
        wt, mt, vt = (jnp.transpose(t, (0, 2, 1)) for t in (w, m, v))
        return adamw_slices(wt, mt, vt, recv, name=name, **kw)

    def untransposed(outs):
        return tuple(jnp.transpose(t, (0, 2, 1)) for t in outs)

    lands = list(exchange_wait(parts[:6], lands, dx, name="exchange_wait"))
    _, _, r_bin, r_bout, r_cin, r_cgrp, r_cout, r_fg, r_fu, r_fd = lands
    res = {}
    res["b_w_in"] = adamw_slices_t(b_w_in, m_b_w_in, v_b_w_in, r_bin, name="adam_b_w_in")
    res["b_w_out"] = sliced(b_w_out, m_b_w_out, v_b_w_out, r_bout, "adam_b_w_out")
    res["c_w_in"] = sliced(c_w_in, m_c_w_in, v_c_w_in, r_cin, "adam_c_w_in")
    res["c_w_grp"] = sliced(c_w_grp, m_c_w_grp, v_c_w_grp, r_cgrp, "adam_c_w_grp")
    res["c_w_out"] = sliced(c_w_out, m_c_w_out, v_c_w_out, r_cout, "adam_c_w_out")
    fg = sliced_t(f_w_gate, m_f_w_gate, v_f_w_gate, r_fg, "adam_f_gate_123", tc=256, l0=1)
    fu = sliced_t(f_w_up, m_f_w_up, v_f_w_up, r_fu, "adam_f_up_123", tc=256, l0=1)
    fd = adamw_slices(f_w_down, m_f_w_down, v_f_w_down, r_fd, tc=256, l0=1, name="adam_f_down_123")
    big = [res[n][1] for n in ("b_w_in", "b_w_out", "c_w_in", "c_w_grp", "c_w_out")] + [fg[1], fu[1], fd[1]]

    ffn0_items, ffn0_send, ffn0_recv, ffn0_gs = parts[6]
    ffn0_zones = [Z_FG, Z_FU, Z_FD]
    ffn0_part = ([(None, ffn0_zones.index(zi), l) for _, zi, l in ffn0_items], ffn0_send, ffn0_recv, ffn0_gs)
    r_fg, r_fu, r_fd = exchange_wait([ffn0_part], [lands[zi] for zi in ffn0_zones], big, name="exchange_wait_ffn0")
    fg = sliced_t(f_w_gate, m_f_w_gate, v_f_w_gate, r_fg, "adam_f_gate_0", tc=256, l0=0, nl=1, prev=fg)
    fu = sliced_t(f_w_up, m_f_w_up, v_f_w_up, r_fu, "adam_f_up_0", tc=256, l0=0, nl=1, prev=fu)
    fd = adamw_slices(f_w_down, m_f_w_down, v_f_w_down, r_fd, tc=256, l0=0, nl=1, prev=fd, name="adam_f_down_0")
    res["f_w_gate"], res["f_w_up"], res["f_w_down"] = untransposed(fg), untransposed(fu), tuple(fd)

    last_items, last_send, last_recv, last_gs = parts[7]
    last_zones = [Z_AIN, Z_AOUT]
    last_part = ([(None, last_zones.index(zi), l) for _, zi, l in last_items], last_send, last_recv, last_gs)
    r_ain, r_aout = exchange_wait([last_part], [lands[zi] for zi in last_zones], [fg[1], fu[1], fd[1]], name="exchange_wait_last")
    res["a_w_in"] = adamw_slices_t(a_w_in, m_a_w_in, v_a_w_in, r_ain, name="adam_a_w_in")
    res["a_w_out"] = sliced(a_w_out, m_a_w_out, v_a_w_out, r_aout, "adam_a_w_out")

    _, sg_lands = gather_wait(sg_send, sg_recv, sg_shards, sg_lands, [res["a_w_in"][1], res["a_w_out"][1]], name="gather_small_wait")
    (gs_all,) = gather_finish(sg_lands, name="gather_small_finish")
    gs = sum8(gs_all, name="sum_small_grads")
    offs = [0]
    for r in small_rows:
        offs.append(offs[-1] + r)
    sp = [gs[offs[q]:offs[q + 1]] for q in range(len(small_parts))]
    grad_norm_mix_g = sp[0].reshape(DEPTH, D)
    grad_norm_ffn_g = sp[1].reshape(DEPTH, D)
    grad_final_norm_g = sp[2].reshape(D)
    grad_a_w_s = sp[4].reshape(a_w_s.shape)
    grad_a_b_s = sp[5].reshape(a_b_s.shape)

    def my_cols(full):
        return lax.dynamic_slice_in_dim(full, me * GBLK, GBLK, axis=1)

    grad_a_v_norm_g = my_cols(sp[3].reshape(2, D))
    grad_b_conv_w = my_cols(sp[6].reshape(3, D))[None]
    grad_c_scale = my_cols(sp[7].reshape(1, D))
    res["norm_mix_g"] = (grad_norm_mix_g,) + plain(norm_mix_g, m_norm_mix_g, v_norm_mix_g, grad_norm_mix_g, "adam_norm_mix")
    res["norm_ffn_g"] = (grad_norm_ffn_g,) + plain(norm_ffn_g, m_norm_ffn_g, v_norm_ffn_g, grad_norm_ffn_g, "adam_norm_ffn")
    res["final_norm_g"] = (grad_final_norm_g,) + tuple(
        t.reshape(D) for t in plain(final_norm_g[None], m_final_norm_g[None], v_final_norm_g[None], grad_final_norm_g[None], "adam_final"))
    res["a_v_norm_g"] = (grad_a_v_norm_g,) + plain(a_v_norm_g, m_a_v_norm_g, v_a_v_norm_g, grad_a_v_norm_g, "adam_a_v_norm")
    res["a_w_s"] = (grad_a_w_s,) + plain(a_w_s, m_a_w_s, v_a_w_s, grad_a_w_s, "adam_a_w_s")
    res["a_b_s"] = (grad_a_b_s,) + plain(a_b_s, m_a_b_s, v_a_b_s, grad_a_b_s, "adam_a_b_s")
    res["b_conv_w"] = (grad_b_conv_w,) + plain(b_conv_w, m_b_conv_w, v_b_conv_w, grad_b_conv_w, "adam_b_conv")
    res["c_scale"] = (grad_c_scale,) + plain(c_scale, m_c_scale, v_c_scale, grad_c_scale, "adam_c_scale")

    order = ["norm_mix_g", "norm_ffn_g", "final_norm_g", "a_w_in", "a_v_norm_g", "a_w_s", "a_b_s", "a_w_out", "b_w_in",
             "b_conv_w", "b_w_out", "c_w_in", "c_w_grp", "c_scale", "c_w_out", "f_w_gate", "f_w_up", "f_w_down"]
    return (loss, grad_x, *[res[n][0] for n in order], *[res[n][1] for n in order],
            *[res[n][2] for n in order], *[res[n][3] for n in order])
```

```python
import functools

import jax
import jax.numpy as jnp
from jax import lax
from jax.experimental import pallas as pl
from jax.experimental.pallas import tpu as pltpu

f32 = jnp.float32
bf16 = jnp.bfloat16

NDEV = 8
D = 1024
EPS = 1e-6
CHUNK = 64
GBLK = 128
A_GROUPS = 8
C_GROUPS = 4
C_GDIM = 256
POOL_WINDOWS = (2, 4, 8, 16)
HALO = 16
COLS = 256
F_LOC = 352
MXU_COLS = 256
DEPTH = 4
AXES = ("x", "y", "c")
MESH = pl.DeviceIdType.MESH

ADAM_LR = 0.001
ADAM_B1 = 0.9
ADAM_B2 = 0.999
ADAM_EPS = 1e-08
ADAM_WD = 0.01
ADAM_STEP = 10

VMEM_LIMIT = 56 * 1024 * 1024


def _cparams(sem):
    return pltpu.CompilerParams(dimension_semantics=sem, vmem_limit_bytes=VMEM_LIMIT)


def _gelu(z):
    return 0.5 * z * (1.0 + lax.erf(z * 0.7071067811865476))


def _gelu_cdf(z):
    return 0.5 * (1.0 + lax.erf(z * 0.7071067811865476))


def _gelu_grad(z, cdf):
    return cdf + z * jnp.exp(-0.5 * z * z) * 0.3989422804014327


def _dot(a, b):
    return jnp.dot(a, b, preferred_element_type=f32)


def _dot_nt(a, b):
    return lax.dot_general(a, b, (((1,), (1,)), ((), ())), preferred_element_type=f32)


def _dot_tn(a, b):
    return lax.dot_general(a, b, (((0,), (0,)), ((), ())), preferred_element_type=f32)


def _col_chunks(n, width=1024):
    return [(c, min(c + width, n)) for c in range(0, n, width)]


def norm_mm(x, g, w, *, tm, transposed, name):
    T = x.shape[0]
    n = w.shape[0] if transposed else w.shape[1]

    def body(x_ref, g_ref, w_ref, p_ref, h_ref):
        xv = x_ref[...]
        r = lax.rsqrt(jnp.mean(xv * xv, axis=-1, keepdims=True) + EPS)
        h = (xv * r * g_ref[...]).astype(bf16)
        h_ref[...] = h
        p = _dot_nt(h, w_ref[...]) if transposed else _dot(h, w_ref[...])
        p_ref[...] = p.astype(bf16)

    return pl.pallas_call(
        body, grid=(T // tm,),
        in_specs=[pl.BlockSpec((tm, D), lambda i: (i, 0)), pl.BlockSpec((1, D), lambda i: (0, 0)),
                  pl.BlockSpec(w.shape, lambda i: (0, 0), pipeline_mode=pl.Buffered(1))],
        out_specs=[pl.BlockSpec((tm, n), lambda i: (i, 0)), pl.BlockSpec((tm, D), lambda i: (i, 0))],
        out_shape=[jax.ShapeDtypeStruct((T, n), bf16), jax.ShapeDtypeStruct((T, D), bf16)],
        compiler_params=_cparams(("parallel",)), name=name,
    )(x, g, w)


def ffn_fwd(y, w_out, xin, g, wtg, wtu, wd, *, tm, name):
    T, K = y.shape
    H = wtg.shape[0]

    def body(y_ref, wo_ref, xin_ref, g_ref, wg_ref, wu_ref, wd_ref, x_ref, x2_ref, pg_ref, pu_ref, a_ref, h_ref):
        xv = xin_ref[...] + _dot(y_ref[...], wo_ref[...])
        x_ref[...] = xv
        r = lax.rsqrt(jnp.mean(xv * xv, axis=-1, keepdims=True) + EPS)
        h = (xv * r * g_ref[...]).astype(bf16)
        h_ref[...] = h
        out = None
        for c0, c1 in _col_chunks(H):
            gate = _dot_nt(h, wg_ref[c0:c1, :])
            up = _dot_nt(h, wu_ref[c0:c1, :])
            pg_ref[:, c0:c1] = gate.astype(bf16)
            pu_ref[:, c0:c1] = up.astype(bf16)
            ac = (gate * jax.nn.sigmoid(gate) * up).astype(bf16)
            a_ref[:, c0:c1] = ac
            part = _dot(ac, wd_ref[c0:c1, :])
            out = part if out is None else out + part
        x2_ref[...] = xv + out

    wspec = pl.BlockSpec((H, D), lambda i: (0, 0), pipeline_mode=pl.Buffered(1))
    hspec = pl.BlockSpec((tm, H), lambda i: (i, 0))
    dspec = pl.BlockSpec((tm, D), lambda i: (i, 0))
    return pl.pallas_call(
        body, grid=(T // tm,),
        in_specs=[pl.BlockSpec((tm, K), lambda i: (i, 0)),
                  pl.BlockSpec((K, D), lambda i: (0, 0), pipeline_mode=pl.Buffered(1)),
                  dspec, pl.BlockSpec((1, D), lambda i: (0, 0)), wspec, wspec, wspec],
        out_specs=[dspec, dspec, hspec, hspec, hspec, dspec],
        out_shape=[jax.ShapeDtypeStruct((T, D), f32)] * 2 + [jax.ShapeDtypeStruct((T, H), bf16)] * 3
        + [jax.ShapeDtypeStruct((T, D), bf16)],
        compiler_params=_cparams(("parallel",)), name=name,
    )(y, w_out, xin, g, wtg, wtu, wd)


def ffn_bwd(dx, wd, pg, pu, wtg, wtu, x, g, tok, w_next, *, tm, name):
    T = dx.shape[0]
    H = wd.shape[0]
    kn = w_next.shape[0]
    toks = [] if tok is None else [tok]

    def body(dx_ref, wd_ref, pg_ref, pu_ref, wg_ref, wu_ref, x_ref, g_ref, *rest):
        wn_ref = rest[len(toks)]
        dg_ref, du_ref, dxb_ref, dx1_ref, dgn_ref, dy_ref, dx1b_ref = rest[len(toks) + 1:]

        @pl.when(pl.program_id(0) == 0)
        def _():
            dgn_ref[...] = jnp.zeros(dgn_ref.shape, f32)

        dxv = dx_ref[...]
        dxb = dxv.astype(bf16)
        dxb_ref[...] = dxb
        dh = None
        for c0, c1 in _col_chunks(H):
            da = _dot_nt(dxb, wd_ref[c0:c1, :])
            gate = pg_ref[:, c0:c1].astype(f32)
            up = pu_ref[:, c0:c1].astype(f32)
            sg = jax.nn.sigmoid(gate)
            dgc = (da * up * (sg * (1.0 + gate * (1.0 - sg)))).astype(bf16)
            duc = (da * gate * sg).astype(bf16)
            dg_ref[:, c0:c1] = dgc
            du_ref[:, c0:c1] = duc
            part = _dot(dgc, wg_ref[c0:c1, :]) + _dot(duc, wu_ref[c0:c1, :])
            dh = part if dh is None else dh + part
        gv = g_ref[...] + rest[0][0:1, 0:1] if toks else g_ref[...]
        xv = x_ref[...]
        r = lax.rsqrt(jnp.mean(xv * xv, axis=-1, keepdims=True) + EPS)
        xhat = xv * r
        dgn_ref[...] += jnp.sum(dh * xhat, axis=0, keepdims=True)
        dxhat = dh * gv
        dx1 = dxv + r * (dxhat - xhat * jnp.mean(dxhat * xhat, axis=-1, keepdims=True))
        dx1_ref[...] = dx1
        dx1b = dx1.astype(bf16)
        dx1b_ref[...] = dx1b
        dy_ref[...] = _dot_nt(dx1b, wn_ref[...]).astype(bf16)

    dspec = pl.BlockSpec((tm, D), lambda i: (i, 0))
    hspec = pl.BlockSpec((tm, H), lambda i: (i, 0))
    wspec = pl.BlockSpec((H, D), lambda i: (0, 0), pipeline_mode=pl.Buffered(1))
    gspec = pl.BlockSpec((1, D), lambda i: (0, 0))
    return pl.pallas_call(
        body, grid=(T // tm,),
        in_specs=[dspec, wspec, hspec, hspec, wspec, wspec, dspec, gspec]
        + [pl.BlockSpec((8, 128), lambda i: (0, 0))] * len(toks)
        + [pl.BlockSpec((kn, D), lambda i: (0, 0), pipeline_mode=pl.Buffered(1))],
        out_specs=[hspec, hspec, dspec, dspec, gspec, pl.BlockSpec((tm, kn), lambda i: (i, 0)), dspec],
        out_shape=[jax.ShapeDtypeStruct((T, H), bf16)] * 2 + [jax.ShapeDtypeStruct((T, D), bf16),
                   jax.ShapeDtypeStruct((T, D), f32), jax.ShapeDtypeStruct((1, D), f32),
                   jax.ShapeDtypeStruct((T, kn), bf16), jax.ShapeDtypeStruct((T, D), bf16)],
        compiler_params=_cparams(("arbitrary",)), name=name,
    )(dx, wd, pg, pu, wtg, wtu, x, g, *toks, w_next)


def mm_tn(a, b, *, name):
    T, K = a.shape
    n = b.shape[1]

    def body(a_ref, b_ref, o_ref):
        o_ref[...] = _dot_tn(a_ref[...], b_ref[...]).astype(bf16)

    out = pl.pallas_call(
        body, grid=(K // MXU_COLS,),
        in_specs=[pl.BlockSpec((T, MXU_COLS), lambda d: (0, d)),
                  pl.BlockSpec((T, n), lambda d: (0, 0), pipeline_mode=pl.Buffered(1))],
        out_specs=pl.BlockSpec((MXU_COLS, n), lambda d: (d, 0)),
        out_shape=jax.ShapeDtypeStruct((K, n), bf16),
        compiler_params=_cparams(("parallel",)), name=name,
    )(a, b)
    return out.reshape(NDEV, K // NDEV, n)


def ffn_dw(dpg, dpu, a, h2, dxb, *, name):
    T, H = dpg.shape

    def body(g_ref, u_ref, a_ref, h_ref, x_ref, og_ref, ou_ref, od_ref):
        h = h_ref[...]
        og_ref[...] = _dot_tn(g_ref[...], h).astype(bf16)
        ou_ref[...] = _dot_tn(u_ref[...], h).astype(bf16)
        od_ref[...] = _dot_tn(a_ref[...], x_ref[...]).astype(bf16)

    blk = pl.BlockSpec((T, MXU_COLS), lambda d: (0, d))
    full = pl.BlockSpec((T, D), lambda d: (0, 0), pipeline_mode=pl.Buffered(1))
    out = pl.BlockSpec((MXU_COLS, D), lambda d: (d, 0))
    outs = pl.pallas_call(
        body, grid=(H // MXU_COLS,),
        in_specs=[blk, blk, blk, full, full], out_specs=[out] * 3,
        out_shape=[jax.ShapeDtypeStruct((H, D), bf16)] * 3,
        compiler_params=_cparams(("parallel",)), name=name,
    )(dpg, dpu, a, h2, dxb)
    return [o.reshape(NDEV, H // NDEV, D) for o in outs]


def _normbwd_tail(dh, x_ref, g_ref, tok_ref, dres_ref, dx_ref, dgn_ref):
    gv = g_ref[...] + tok_ref[0:1, 0:1]
    xv = x_ref[...]
    r = lax.rsqrt(jnp.mean(xv * xv, axis=-1, keepdims=True) + EPS)
    xhat = xv * r
    dgn_ref[...] += jnp.sum(dh * xhat, axis=0, keepdims=True)
    dxhat = dh * gv
    dx_ref[...] = dres_ref[...] + r * (dxhat - xhat * jnp.mean(dxhat * xhat, axis=-1, keepdims=True))


def _normbwd_specs(tm, w):
    dspec = pl.BlockSpec((tm, D), lambda i: (i, 0))
    gspec = pl.BlockSpec((1, D), lambda i: (0, 0))
    ins = [pl.BlockSpec(w.shape, lambda i: (0, 0), pipeline_mode=pl.Buffered(1)), dspec, gspec,
           pl.BlockSpec((8, 128), lambda i: (0, 0)), dspec]
    return ins, [dspec, gspec]


def _prev_halo_spec(tm, ncol):
    return pl.BlockSpec((HALO, ncol), lambda i: (jnp.maximum(i * (tm // HALO) - 1, 0), 0))


def _next_halo_spec(tm, ncol, T):
    return pl.BlockSpec((HALO, ncol), lambda i: (jnp.minimum((i + 1) * (tm // HALO), T // HALO - 1), 0))


def a_fwd(x, g, wt, gv, wm, bsb, *, tm, name):
    T = x.shape[0]

    def body(x_ref, g_ref, wt_ref, gv_ref, wm_ref, bs_ref, p_ref, h_ref, y_ref, vn_sc):
        xv = x_ref[...]
        r = lax.rsqrt(jnp.mean(xv * xv, axis=-1, keepdims=True) + EPS)
        h = (xv * r * g_ref[...]).astype(bf16)
        h_ref[...] = h
        p_ref[...] = _dot_nt(h, wt_ref[...]).astype(bf16)
        v = _gelu(p_ref[:, D:].astype(f32))
        vc = v - jnp.mean(v, axis=-1, keepdims=True)
        var = jnp.mean(vc * vc, axis=-1, keepdims=True)
        vn_sc[...] = (vc * lax.rsqrt(var + EPS) * gv_ref[...]).astype(bf16)
        for g in range(A_GROUPS):
            cs = slice(g * GBLK, (g + 1) * GBLK)
            for r in range(tm // GBLK):
                rs = slice(r * GBLK, (r + 1) * GBLK)
                sv = _dot(wm_ref[g], vn_sc[rs, cs]) + bs_ref[g]
                y_ref[rs, cs] = (_gelu(p_ref[rs, cs].astype(f32)) * sv).astype(bf16)

    dspec = pl.BlockSpec((tm, D), lambda i: (i, 0))
    gspec = pl.BlockSpec((1, D), lambda i: (0, 0))
    return pl.pallas_call(
        body, grid=(T // tm,),
        in_specs=[dspec, gspec, pl.BlockSpec(wt.shape, lambda i: (0, 0), pipeline_mode=pl.Buffered(1)), gspec,
                  pl.BlockSpec((A_GROUPS, GBLK, GBLK), lambda i: (0, 0, 0)),
                  pl.BlockSpec((A_GROUPS, GBLK, GBLK), lambda i: (0, 0, 0))],
        out_specs=[pl.BlockSpec((tm, 2 * D), lambda i: (i, 0)), dspec, dspec],
        out_shape=[jax.ShapeDtypeStruct((T, 2 * D), bf16), jax.ShapeDtypeStruct((T, D), bf16),
                   jax.ShapeDtypeStruct((T, D), bf16)],
        scratch_shapes=[pltpu.VMEM((tm, D), bf16)],
        compiler_params=_cparams(("parallel",)), name=name,
    )(x, g, wt, gv, wm, bsb)


def a_mid_bwd(p, dy, gv, wm, bsb, wt, x, g, tok, dres, *, tm, name):
    T = p.shape[0]

    def body(p_ref, dy_ref, gv_ref, wm_ref, bs_ref, wt_ref, x_ref, g_ref, tok_ref, dres_ref,
             dp_ref, dgv_ref, dwm_ref, dbs_ref, dx_ref, dgn_ref, vn_sc, dvn_sc):
        @pl.when(pl.program_id(0) == 0)
        def _():
            dgn_ref[...] = jnp.zeros(dgn_ref.shape, f32)
            dgv_ref[...] = jnp.zeros(dgv_ref.shape, f32)
            dwm_ref[...] = jnp.zeros(dwm_ref.shape, f32)
            dbs_ref[...] = jnp.zeros(dbs_ref.shape, f32)

        zv = p_ref[:, D:].astype(f32)
        cdf_v = _gelu_cdf(zv)
        v = zv * cdf_v
        vc = v - jnp.mean(v, axis=-1, keepdims=True)
        rstd = lax.rsqrt(jnp.mean(vc * vc, axis=-1, keepdims=True) + EPS)
        vhat = vc * rstd
        vn_sc[...] = (vhat * gv_ref[...]).astype(bf16)
        for g in range(A_GROUPS):
            cs = slice(g * GBLK, (g + 1) * GBLK)
            dwm = jnp.zeros((GBLK, GBLK), f32)
            dbs = jnp.zeros((GBLK, 1), f32)
            for r in range(tm // GBLK):
                rs = slice(r * GBLK, (r + 1) * GBLK)
                zu = p_ref[rs, cs].astype(f32)
                cdf_u = _gelu_cdf(zu)
                vn = vn_sc[rs, cs]
                sv = _dot(wm_ref[g], vn) + bs_ref[g]
                dyb = dy_ref[rs, cs].astype(f32)
                dsv = dyb * (zu * cdf_u)
                dsvb = dsv.astype(bf16)
                dp_ref[rs, cs] = (dyb * sv * _gelu_grad(zu, cdf_u)).astype(bf16)
                dwm += _dot_nt(dsvb, vn)
                dbs += jnp.sum(dsv, axis=1, keepdims=True)
                dvn_sc[rs, cs] = _dot_tn(wm_ref[g], dsvb)
            dwm_ref[g] += dwm
            dbs_ref[g] += dbs
        dvn = dvn_sc[...]
        dgv_ref[...] += jnp.sum(dvn * vhat, axis=0, keepdims=True)
        dvhat = dvn * gv_ref[...]
        dv = rstd * (dvhat - jnp.mean(dvhat, axis=-1, keepdims=True)
                     - vhat * jnp.mean(dvhat * vhat, axis=-1, keepdims=True))
        dp_ref[:, D:] = (dv * _gelu_grad(zv, cdf_v)).astype(bf16)
        _normbwd_tail(_dot(dp_ref[...], wt_ref[...]), x_ref, g_ref, tok_ref, dres_ref, dx_ref, dgn_ref)

    tail_in, tail_out = _normbwd_specs(tm, wt)
    return pl.pallas_call(
        body, grid=(T // tm,),
        in_specs=[pl.BlockSpec((tm, 2 * D), lambda i: (i, 0)), pl.BlockSpec((tm, D), lambda i: (i, 0)),
                  pl.BlockSpec((1, D), lambda i: (0, 0)),
                  pl.BlockSpec((A_GROUPS, GBLK, GBLK), lambda i: (0, 0, 0)),
                  pl.BlockSpec((A_GROUPS, GBLK, GBLK), lambda i: (0, 0, 0))] + tail_in,
        out_specs=[pl.BlockSpec((tm, 2 * D), lambda i: (i, 0)), pl.BlockSpec((1, D), lambda i: (0, 0)),
                   pl.BlockSpec((A_GROUPS, GBLK, GBLK), lambda i: (0, 0, 0)),
                   pl.BlockSpec((A_GROUPS, GBLK, 1), lambda i: (0, 0, 0))] + tail_out,
        out_shape=[jax.ShapeDtypeStruct((T, 2 * D), bf16), jax.ShapeDtypeStruct((1, D), f32),
                   jax.ShapeDtypeStruct((A_GROUPS, GBLK, GBLK), f32), jax.ShapeDtypeStruct((A_GROUPS, GBLK, 1), f32),
                   jax.ShapeDtypeStruct((T, D), f32), jax.ShapeDtypeStruct((1, D), f32)],
        scratch_shapes=[pltpu.VMEM((tm, D), bf16), pltpu.VMEM((tm, D), f32)],
        compiler_params=_cparams(("arbitrary",)), name=name,
    )(p, dy, gv, wm, bsb, wt, x, g, tok, dres)


def b_mid_fwd(p, cw, *, tm, S, name):
    T = p.shape[0]
    nst = S // tm

    def body(p_ref, ph_ref, cw_ref, y_ref, ext):
        first = (pl.program_id(0) % nst) == 0
        for c0 in range(0, D, COLS):
            cs, cc, cx = slice(c0, c0 + COLS), slice(D + c0, D + c0 + COLS), slice(2 * D + c0, 2 * D + c0 + COLS)
            q = p_ref[:, cc].astype(f32) * p_ref[:, cx].astype(f32)
            qh = ph_ref[:, cc].astype(f32) * ph_ref[:, cx].astype(f32)
            ext[0:HALO, cs] = jnp.where(first, 0.0, qh)
            ext[HALO:, cs] = q
            y = (cw_ref[0:1, cs] * ext[pl.ds(HALO - 2, tm), cs] + cw_ref[1:2, cs] * ext[pl.ds(HALO - 1, tm), cs]
                 + cw_ref[2:3, cs] * q)
            y_ref[:, cs] = (p_ref[:, cs].astype(f32) * y).astype(bf16)

    return pl.pallas_call(
        body, grid=(T // tm,),
        in_specs=[pl.BlockSpec((tm, 3 * D), lambda i: (i, 0)), _prev_halo_spec(tm, 3 * D),
                  pl.BlockSpec((3, D), lambda i: (0, 0))],
        out_specs=pl.BlockSpec((tm, D), lambda i: (i, 0)),
        out_shape=jax.ShapeDtypeStruct((T, D), bf16),
        scratch_shapes=[pltpu.VMEM((tm + HALO, D), f32)],
        compiler_params=_cparams(("parallel",)), name=name,
    )(p, p, cw)


def b_mid_bwd(p, dy, cw, wt, x, g, tok, dres, *, tm, S, name):
    T = p.shape[0]
    nst = S // tm

    def body(p_ref, ph_ref, pn_ref, dy_ref, dyn_ref, cw_ref, wt_ref, x_ref, g_ref, tok_ref, dres_ref,
             dp_ref, dcw_ref, dx_ref, dgn_ref, ext, ext2):
        i = pl.program_id(0)
        first = (i % nst) == 0
        last = (i % nst) == nst - 1

        @pl.when(i == 0)
        def _():
            dgn_ref[...] = jnp.zeros(dgn_ref.shape, f32)
            dcw_ref[...] = jnp.zeros(dcw_ref.shape, f32)

        for c0 in range(0, D, COLS):
            cs, cc, cx = slice(c0, c0 + COLS), slice(D + c0, D + c0 + COLS), slice(2 * D + c0, 2 * D + c0 + COLS)
            gb, gc, xt = p_ref[:, cs].astype(f32), p_ref[:, cc].astype(f32), p_ref[:, cx].astype(f32)
            q = gc * xt
            ext[0:HALO, cs] = jnp.where(first, 0.0, ph_ref[:, cc].astype(f32) * ph_ref[:, cx].astype(f32))
            ext[HALO:, cs] = q
            q2 = ext[pl.ds(HALO - 2, tm), cs]
            q1 = ext[pl.ds(HALO - 1, tm), cs]
            y = cw_ref[0:1, cs] * q2 + cw_ref[1:2, cs] * q1 + cw_ref[2:3, cs] * q
            dyo = dy_ref[:, cs].astype(f32)
            dp_ref[:, cs] = (dyo * y).astype(bf16)
            dyc = dyo * gb
            ext2[0:tm, cs] = dyc
            ext2[tm:, cs] = jnp.where(last, 0.0, dyn_ref[:, cs].astype(f32) * pn_ref[:, cs].astype(f32))
            dq = cw_ref[2:3, cs] * dyc + cw_ref[1:2, cs] * ext2[pl.ds(1, tm), cs] + cw_ref[0:1, cs] * ext2[pl.ds(2, tm), cs]
            dp_ref[:, cc] = (dq * xt).astype(bf16)
            dp_ref[:, cx] = (dq * gc).astype(bf16)
            dcw_ref[0:1, cs] += jnp.sum(dyc * q2, axis=0, keepdims=True)
            dcw_ref[1:2, cs] += jnp.sum(dyc * q1, axis=0, keepdims=True)
            dcw_ref[2:3, cs] += jnp.sum(dyc * q, axis=0, keepdims=True)
        _normbwd_tail(_dot(dp_ref[...], wt_ref[...]), x_ref, g_ref, tok_ref, dres_ref, dx_ref, dgn_ref)

    tail_in, tail_out = _normbwd_specs(tm, wt)
    return pl.pallas_call(
        body, grid=(T // tm,),
        in_specs=[pl.BlockSpec((tm, 3 * D), lambda i: (i, 0)), _prev_halo_spec(tm, 3 * D), _next_halo_spec(tm, 3 * D, T),
                  pl.BlockSpec((tm, D), lambda i: (i, 0)), _next_halo_spec(tm, D, T),
                  pl.BlockSpec((3, D), lambda i: (0, 0))] + tail_in,
        out_specs=[pl.BlockSpec((tm, 3 * D), lambda i: (i, 0)), pl.BlockSpec((3, D), lambda i: (0, 0))] + tail_out,
        out_shape=[jax.ShapeDtypeStruct((T, 3 * D), bf16), jax.ShapeDtypeStruct((3, D), f32),
                   jax.ShapeDtypeStruct((T, D), f32), jax.ShapeDtypeStruct((1, D), f32)],
        scratch_shapes=[pltpu.VMEM((tm + HALO, D), f32), pltpu.VMEM((tm + HALO, D), f32)],
        compiler_params=_cparams(("arbitrary",)), name=name,
    )(p, p, p, dy, dy, cw, wt, x, g, tok, dres)


def _pool_counts(i, nst, tm, rows, row0, w):
    t = (i % nst) * tm + row0 + lax.broadcasted_iota(jnp.int32, (rows, 1), 0)
    return jnp.minimum(t + 1, w).astype(f32)


def _pool_diff(p_ref, ext, g, i, nst, tm):
    w = POOL_WINDOWS[g]
    cs = slice(g * C_GDIM, (g + 1) * C_GDIM)
    pg = p_ref[:, cs].astype(f32)
    s = pg
    for k in range(1, w):
        s = s + ext[pl.ds(HALO - k, tm), cs]
    return s / _pool_counts(i, nst, tm, tm, 0, w) - pg


def c_mid_fwd(p, wg, scale, *, tm, S, name):
    T = p.shape[0]
    nst = S // tm

    def body(p_ref, ph_ref, wg_ref, sc_ref, y_ref, ext):
        i = pl.program_id(0)
        first = (i % nst) == 0
        ext[0:HALO, :] = jnp.where(first, 0.0, ph_ref[...].astype(f32))
        ext[HALO:, :] = p_ref[...].astype(f32)
        for g in range(C_GROUPS):
            cs = slice(g * C_GDIM, (g + 1) * C_GDIM)
            dg = _pool_diff(p_ref, ext, g, i, nst, tm).astype(bf16)
            wv = wg_ref[:, g].reshape(C_GDIM, C_GDIM)
            y_ref[:, cs] = (_dot(dg, wv) * sc_ref[:, cs]).astype(bf16)

    return pl.pallas_call(
        body, grid=(T // tm,),
        in_specs=[pl.BlockSpec((tm, D), lambda i: (i, 0)), _prev_halo_spec(tm, D),
                  pl.BlockSpec((NDEV, C_GROUPS, C_GDIM // NDEV, C_GDIM), lambda i: (0, 0, 0, 0)),
                  pl.BlockSpec((1, D), lambda i: (0, 0))],
        out_specs=pl.BlockSpec((tm, D), lambda i: (i, 0)),
        out_shape=jax.ShapeDtypeStruct((T, D), bf16),
        scratch_shapes=[pltpu.VMEM((tm + HALO, D), f32)],
        compiler_params=_cparams(("parallel",)), name=name,
    )(p, p, wg, scale)


def c_mid_bwd(p, dy, wg, scale, w, x, g, tok, dres, *, tm, S, name):
    T = p.shape[0]
    nst = S // tm

    def body(p_ref, ph_ref, dy_ref, dyn_ref, wg_ref, sc_ref, w_ref, x_ref, g_ref, tok_ref, dres_ref,
             dp_ref, dsc_ref, dwg_ref, dx_ref, dgn_ref, ext, ext2):
        i = pl.program_id(0)
        first = (i % nst) == 0
        last = (i % nst) == nst - 1

        @pl.when(i == 0)
        def _():
            dgn_ref[...] = jnp.zeros(dgn_ref.shape, f32)
            dsc_ref[...] = jnp.zeros(dsc_ref.shape, f32)
            dwg_ref[...] = jnp.zeros(dwg_ref.shape, f32)

        ext[0:HALO, :] = jnp.where(first, 0.0, ph_ref[...].astype(f32))
        ext[HALO:, :] = p_ref[...].astype(f32)
        for g in range(C_GROUPS):
            w = POOL_WINDOWS[g]
            cs = slice(g * C_GDIM, (g + 1) * C_GDIM)
            dg = _pool_diff(p_ref, ext, g, i, nst, tm).astype(bf16)
            wv = wg_ref[:, g].reshape(C_GDIM, C_GDIM)
            dyo = dy_ref[:, cs].astype(f32)
            dsc_ref[:, cs] += jnp.sum(dyo * _dot(dg, wv), axis=0, keepdims=True)
            dyp = (dyo * sc_ref[:, cs]).astype(bf16)
            dypn = (dyn_ref[:, cs].astype(f32) * sc_ref[:, cs]).astype(bf16)
            dwg_ref[g] += _dot_tn(dg, dyp)
            dd = _dot_nt(dyp, wv)
            ddn = _dot_nt(dypn, wv)
            ext2[0:tm, cs] = dd / _pool_counts(i, nst, tm, tm, 0, w)
            ext2[tm:, cs] = jnp.where(last, 0.0, ddn / _pool_counts(i, nst, tm, HALO, tm, w))
            s = -dd
            for k in range(w):
                s = s + ext2[pl.ds(k, tm), cs]
            dp_ref[:, cs] = s.astype(bf16)
        _normbwd_tail(_dot_nt(dp_ref[...], w_ref[...]), x_ref, g_ref, tok_ref, dres_ref, dx_ref, dgn_ref)

    tail_in, tail_out = _normbwd_specs(tm, w)
    return pl.pallas_call(
        body, grid=(T // tm,),
        in_specs=[pl.BlockSpec((tm, D), lambda i: (i, 0)), _prev_halo_spec(tm, D),
                  pl.BlockSpec((tm, D), lambda i: (i, 0)), _next_halo_spec(tm, D, T),
                  pl.BlockSpec((NDEV, C_GROUPS, C_GDIM // NDEV, C_GDIM), lambda i: (0, 0, 0, 0)),
                  pl.BlockSpec((1, D), lambda i: (0, 0))] + tail_in,
        out_specs=[pl.BlockSpec((tm, D), lambda i: (i, 0)), pl.BlockSpec((1, D), lambda i: (0, 0)),
                   pl.BlockSpec((C_GROUPS, C_GDIM, C_GDIM), lambda i: (0, 0, 0))] + tail_out,
        out_shape=[jax.ShapeDtypeStruct((T, D), bf16), jax.ShapeDtypeStruct((1, D), f32),
                   jax.ShapeDtypeStruct((C_GROUPS, C_GDIM, C_GDIM), f32),
                   jax.ShapeDtypeStruct((T, D), f32), jax.ShapeDtypeStruct((1, D), f32)],
        scratch_shapes=[pltpu.VMEM((tm + HALO, D), f32), pltpu.VMEM((tm + HALO, D), f32)],
        compiler_params=_cparams(("arbitrary",)), name=name,
    )(p, p, dy, dy, wg, scale, w, x, g, tok, dres)


def final_loss(x, g, tgt, *, tm, name):
    T = x.shape[0]

    def body(x_ref, g_ref, t_ref, loss_ref, dx_ref, dg_ref):
        @pl.when(pl.program_id(0) == 0)
        def _():
            loss_ref[...] = jnp.zeros(loss_ref.shape, f32)
            dg_ref[...] = jnp.zeros(dg_ref.shape, f32)

        xv = x_ref[...]
        r = lax.rsqrt(jnp.mean(xv * xv, axis=-1, keepdims=True) + EPS)
        xhat = xv * r
        err = xhat * g_ref[...] - t_ref[...]
        loss_ref[...] += 0.5 * jnp.sum(jnp.mean(err * err, axis=-1, keepdims=True))
        dy = err * (1.0 / D)
        dg_ref[...] += jnp.sum(dy * xhat, axis=0, keepdims=True)
        dxhat = dy * g_ref[...]
        dx_ref[...] = r * (dxhat - xhat * jnp.mean(dxhat * xhat, axis=-1, keepdims=True))

    return pl.pallas_call(
        body, grid=(T // tm,),
        in_specs=[pl.BlockSpec((tm, D), lambda i: (i, 0)), pl.BlockSpec((1, D), lambda i: (0, 0)),
                  pl.BlockSpec((tm, D), lambda i: (i, 0))],
        out_specs=[pl.BlockSpec((8, 128), lambda i: (0, 0)), pl.BlockSpec((tm, D), lambda i: (i, 0)),
                   pl.BlockSpec((1, D), lambda i: (0, 0))],
        out_shape=[jax.ShapeDtypeStruct((8, 128), f32), jax.ShapeDtypeStruct((T, D), f32),
                   jax.ShapeDtypeStruct((1, D), f32)],
        compiler_params=_cparams(("arbitrary",)), name=name,
    )(x, g, tgt)


def _slot(px, py, pc):
    return 4 * px + 2 * py + pc


def _with_own_block(s, me):
    zone = lax.empty((NDEV,) + s.shape, s.dtype)
    return lax.dynamic_update_slice(zone, s[None], (me,) + (0,) * s.ndim)


def all_gather(arrs, me, *, name):
    n = len(arrs)

    def body(*refs):
        ins, outs = refs[:n], refs[2 * n:3 * n]
        send_sems, recv_sems = refs[3 * n:]
        x, y, c = lax.axis_index("x"), lax.axis_index("y"), lax.axis_index("c")
        me, sibling = (x, y, c), (x, y, 1 - c)
        chips = [(1 - x, y), (x, 1 - y), (1 - x, 1 - y)]

        def copy(a, k, block, to, src=None):
            dst = outs[a].at[_slot(*block)]
            return pltpu.make_async_remote_copy(
                src_ref=dst if src is None else src, dst_ref=dst,
                send_sem=send_sems.at[a, k], recv_sem=recv_sems.at[a, k], device_id=to, device_id_type=MESH)

        first = []
        for a in range(n):
            first.append(copy(a, 0, me, sibling, src=ins[a]))
            first += [copy(a, 1 + j, me, (*chip, c), src=ins[a]) for j, chip in enumerate(chips)]
        for cp in first:
            cp.start()
        passed = []
        for j, chip in enumerate(chips):
            for a in range(n):
                copy(a, 1 + j, (*chip, c), me).wait_recv()
                fwd = copy(a, 4 + j, (*chip, c), sibling)
                fwd.start()
                passed.append(fwd)
        for a in range(n):
            copy(a, 0, sibling, me).wait_recv()
        for j, chip in enumerate(chips):
            for a in range(n):
                copy(a, 4 + j, (*chip, 1 - c), me).wait_recv()
        for cp in first + passed:
            cp.wait_send()

    any_spec = pl.BlockSpec(memory_space=pl.ANY)
    return pl.pallas_call(
        body,
        in_specs=[any_spec] * (2 * n), out_specs=[any_spec] * n,
        out_shape=[jax.ShapeDtypeStruct((NDEV,) + a.shape, a.dtype) for a in arrs],
        input_output_aliases={n + i: i for i in range(n)},
        scratch_shapes=[pltpu.SemaphoreType.DMA((n, 7)), pltpu.SemaphoreType.DMA((n, 7))],
        compiler_params=pltpu.CompilerParams(has_side_effects=True), name=name,
    )(*arrs, *[_with_own_block(a, me) for a in arrs])


HBM_SPEC = pl.BlockSpec(memory_space=pltpu.HBM)
SEM_SPEC = pl.BlockSpec(memory_space=pltpu.SEMAPHORE)
ANY_SPEC = pl.BlockSpec(memory_space=pl.ANY)
TOKEN_SHAPE = jax.ShapeDtypeStruct((8, 128), f32)
DATAFLOW_EFFECT = pltpu.SideEffectType.DATAFLOW_SIDE_EFFECTING


def _in_hbm(a):
    return pltpu.with_memory_space_constraint(a, pltpu.HBM)


def _hbm_like(a):
    return pltpu.HBM(a.shape, a.dtype)


def _mesh_pos():
    return lax.axis_index("x"), lax.axis_index("y"), lax.axis_index("c")


def _gather_targets(x, y, c):
    return [(x, y, 1 - c), (1 - x, y, c), (x, 1 - y, c), (1 - x, 1 - y, c)]


def gather_start(shards, me, after, *, name):
    n = len(shards)
    extra = [] if after is None else [after]

    def body(*refs):
        srcs, lands = refs[:n], refs[n:2 * n]
        send_sems, recv_sems = refs[2 * n + len(extra)], refs[2 * n + len(extra) + 1]
        token = refs[-1]
        x, y, c = _mesh_pos()
        me = _slot(x, y, c)
        for a in range(n):
            for k, to in enumerate(_gather_targets(x, y, c)):
                pltpu.make_async_remote_copy(
                    src_ref=srcs[a], dst_ref=lands[a].at[me], send_sem=send_sems.at[4 * a + k], recv_sem=recv_sems.at[4 * a + k],
                    device_id=to, device_id_type=MESH).start()
        token[...] = jnp.zeros(token.shape, f32)

    lands = [_with_own_block(s, me) for s in shards]
    sems = pltpu.SemaphoreType.DMA((4 * n,))
    out = pl.pallas_call(
        body, name=name,
        in_specs=[HBM_SPEC] * (2 * n) + [ANY_SPEC] * len(extra),
        out_specs=[SEM_SPEC, SEM_SPEC] + [HBM_SPEC] * (2 * n) + [pl.BlockSpec(memory_space=pltpu.VMEM)],
        out_shape=[sems, sems] + [_hbm_like(s) for s in shards] + [_hbm_like(l) for l in lands] + [TOKEN_SHAPE],
        input_output_aliases={i: 2 + i for i in range(2 * n)},
        compiler_params=pltpu.CompilerParams(has_side_effects=DATAFLOW_EFFECT),
    )(*[_in_hbm(s) for s in shards], *[_in_hbm(l) for l in lands], *extra)
    return out[0], out[1], out[2:2 + n], out[2 + n:2 + 2 * n], out[-1]


def gather_wait(send_sems, recv_sems, shards, lands, after, *, name):
    n = len(shards)
    after = list(after) if isinstance(after, (list, tuple)) else [after]

    def body(*refs):
        srcs, lands_in = refs[:n], refs[n:2 * n]
        send_sems, recv_sems = refs[2 * n], refs[2 * n + 1]
        x, y, c = _mesh_pos()
        for a in range(n):
            for k, frm in enumerate(_gather_targets(x, y, c)):
                cp = pltpu.make_async_remote_copy(
                    src_ref=srcs[a], dst_ref=lands_in[a].at[_slot(*frm)], send_sem=send_sems.at[4 * a + k],
                    recv_sem=recv_sems.at[4 * a + k], device_id=frm, device_id_type=MESH)
                cp.wait_send()
                cp.wait_recv()

    out = pl.pallas_call(
        body, name=name,
        in_specs=[HBM_SPEC] * (2 * n) + [SEM_SPEC, SEM_SPEC] + [ANY_SPEC] * len(after),
        out_specs=[HBM_SPEC] * (2 * n),
        out_shape=[_hbm_like(s) for s in shards] + [_hbm_like(l) for l in lands],
        input_output_aliases={i: i for i in range(2 * n)},
        compiler_params=pltpu.CompilerParams(has_side_effects=DATAFLOW_EFFECT),
    )(*shards, *lands, send_sems, recv_sems, *after)
    return out[:n], out[n:]


def gather_finish(lands, *, name):
    n = len(lands)

    def body(*refs):
        lands_in, lands_out = refs[:n], refs[n:2 * n]
        send_sems, recv_sems = refs[2 * n:]
        x, y, c = _mesh_pos()
        sibling = (x, y, 1 - c)
        chips = [(1 - x, y), (x, 1 - y), (1 - x, 1 - y)]
        sent = []
        for a in range(n):
            for j, chip in enumerate(chips):
                s = _slot(*chip, c)
                cp = pltpu.make_async_remote_copy(
                    src_ref=lands_in[a].at[s], dst_ref=lands_out[a].at[s], send_sem=send_sems.at[a, j],
                    recv_sem=recv_sems.at[a, j], device_id=sibling, device_id_type=MESH)
                cp.start()
                sent.append(cp)
        for a in range(n):
            for j, chip in enumerate(chips):
                s = _slot(*chip, 1 - c)
                pltpu.make_async_remote_copy(
                    src_ref=lands_in[a].at[s], dst_ref=lands_out[a].at[s], send_sem=send_sems.at[a, j],
                    recv_sem=recv_sems.at[a, j], device_id=sibling, device_id_type=MESH).wait_recv()
        for cp in sent:
            cp.wait_send()

    return pl.pallas_call(
        body, name=name,
        in_specs=[ANY_SPEC] * n, out_specs=[ANY_SPEC] * n,
        out_shape=[jax.ShapeDtypeStruct(l.shape, l.dtype) for l in lands],
        input_output_aliases={i: i for i in range(n)},
        scratch_shapes=[pltpu.SemaphoreType.DMA((n, 3)), pltpu.SemaphoreType.DMA((n, 3))],
        compiler_params=pltpu.CompilerParams(has_side_effects=True),
    )(*lands)


def _peer(x, y, c, r):
    return (x ^ ((r >> 2) & 1), y ^ ((r >> 1) & 1), c ^ (r & 1))


def exchange_start(items, lands, after, *, name):
    n, m = len(items), len(lands)
    extra = [] if after is None else [after]

    def body(*refs):
        gs, zones = refs[:n], refs[n:n + m]
        send_sems, recv_sems = refs[n + m + len(extra)], refs[n + m + len(extra) + 1]
        token = refs[-1]
        x, y, c = _mesh_pos()
        me = _slot(x, y, c)
        for r in (1, 2, 4, 3, 5, 6, 7):
            to = _peer(x, y, c, r)
            for a, (_, zi, l) in enumerate(items):
                pltpu.make_async_remote_copy(
                    src_ref=gs[a].at[_slot(*to)], dst_ref=zones[zi].at[me, l], send_sem=send_sems.at[7 * a + r - 1],
                    recv_sem=recv_sems.at[7 * a + r - 1], device_id=to, device_id_type=MESH).start()
        token[...] = jnp.zeros(token.shape, f32)

    gs = [g for g, _, _ in items]
    sems = pltpu.SemaphoreType.DMA((7 * n,))
    out = pl.pallas_call(
        body, name=name,
        in_specs=[HBM_SPEC] * (n + m) + [ANY_SPEC] * len(extra),
        out_specs=[SEM_SPEC, SEM_SPEC] + [HBM_SPEC] * (n + m) + [pl.BlockSpec(memory_space=pltpu.VMEM)],
        out_shape=[sems, sems] + [_hbm_like(g) for g in gs] + [_hbm_like(z) for z in lands] + [TOKEN_SHAPE],
        input_output_aliases={i: 2 + i for i in range(n + m)},
        compiler_params=pltpu.CompilerParams(has_side_effects=DATAFLOW_EFFECT),
    )(*[_in_hbm(g) for g in gs], *[_in_hbm(z) for z in lands], *extra)
    return out[0], out[1], out[2:2 + n], out[2 + n:2 + n + m], out[-1]


def exchange_wait(parts, lands, after, *, name):
    m = len(lands)
    flat_gs = [g for _, _, _, gs in parts for g in gs]
    ng = len(flat_gs)
    after = list(after) if isinstance(after, (list, tuple)) else [after]

    def body(*refs):
        gs, zones = refs[:ng], refs[ng:ng + m]
        sem_refs = refs[ng + m:ng + m + 2 * len(parts)]
        x, y, c = _mesh_pos()
        base = 0
        for pi, (items, _, _, _) in enumerate(parts):
            send_sems, recv_sems = sem_refs[2 * pi], sem_refs[2 * pi + 1]
            for r in range(1, NDEV):
                frm = _peer(x, y, c, r)
                for a, (_, zi, l) in enumerate(items):
                    cp = pltpu.make_async_remote_copy(
                        src_ref=gs[base + a].at[_slot(*frm)], dst_ref=zones[zi].at[_slot(*frm), l],
                        send_sem=send_sems.at[7 * a + r - 1], recv_sem=recv_sems.at[7 * a + r - 1],
                        device_id=frm, device_id_type=MESH)
                    cp.wait_send()
                    cp.wait_recv()
            base += len(items)

    sem_args = [s for _, ss, rs, _ in parts for s in (ss, rs)]
    out = pl.pallas_call(
        body, name=name,
        in_specs=[HBM_SPEC] * (ng + m) + [SEM_SPEC] * len(sem_args) + [ANY_SPEC] * len(after),
        out_specs=[HBM_SPEC] * (ng + m),
        out_shape=[_hbm_like(g) for g in flat_gs] + [_hbm_like(z) for z in lands],
        input_output_aliases={i: i for i in range(ng + m)},
        compiler_params=pltpu.CompilerParams(has_side_effects=DATAFLOW_EFFECT),
    )(*flat_gs, *lands, *sem_args, *after)
    return out[ng:]


def sum8(g, *, name):
    _, R, C = g.shape
    tr = R

    def body(g_ref, o_ref):
        s = g_ref[0]
        for k in range(1, NDEV):
            s = s + g_ref[k]
        o_ref[...] = s

    return pl.pallas_call(
        body, grid=(R // tr,),
        in_specs=[pl.BlockSpec((NDEV, tr, C), lambda i: (0, i, 0))],
        out_specs=pl.BlockSpec((tr, C), lambda i: (i, 0)),
        out_shape=jax.ShapeDtypeStruct((R, C), f32),
        compiler_params=_cparams(("parallel",)), name=name,
    )(g)


def _adam_math(w, g, m, v):
    m = ADAM_B1 * m + (1.0 - ADAM_B1) * g
    v = ADAM_B2 * v + (1.0 - ADAM_B2) * (g * g)
    m_hat = m / (1.0 - ADAM_B1 ** ADAM_STEP)
    v_hat = v / (1.0 - ADAM_B2 ** ADAM_STEP)
    delta = -ADAM_LR * (m_hat / (jnp.sqrt(v_hat) + ADAM_EPS) + ADAM_WD * w)
    return delta, m, v


def adamw_slices(w, m, v, recv, *, row0=0, col0=0, rrows=None, tr=None, tc=None, l0=0, nl=None, prev=None, name):
    L, R, C = w.shape
    Rp, Cp = recv.shape[2], recv.shape[3]
    nl = L - l0 if nl is None else nl
    tr = R if tr is None else tr
    tc = C if tc is None else tc
    rrows = Rp if rrows is None else rrows
    assert R % tr == 0 and C % tc == 0 and row0 % rrows == 0 and R <= rrows
    assert (tr == R or (Rp == R and row0 == 0)) and (tc == C or (Cp == C and col0 == 0))
    rr = rrows if tr == R else tr
    rc = Cp if tc == C else tc
    rb = row0 // rrows if tr == R else 0
    prevs = [] if prev is None else list(prev)

    def body(w_ref, m_ref, v_ref, r_ref, *rest):
        g_ref, d_ref, nm_ref, nv_ref = rest[len(prevs):]
        c0 = col0 if tc == C else 0
        g = r_ref[0, 0:tr, c0:c0 + tc].astype(f32)
        for k in range(1, NDEV):
            g = g + r_ref[k, 0:tr, c0:c0 + tc].astype(f32)
        delta, nm, nv = _adam_math(w_ref[...], g, m_ref[...], v_ref[...])
        g_ref[...] = g
        d_ref[...] = delta
        nm_ref[...] = nm
        nv_ref[...] = nv

    wspec = pl.BlockSpec((None, tr, tc), lambda l, i, j: (l0 + l, i, j))
    return pl.pallas_call(
        body, grid=(nl, R // tr, C // tc),
        in_specs=[wspec, wspec, wspec, pl.BlockSpec((NDEV, None, rr, rc), lambda l, i, j: (0, l0 + l, rb + i, j))]
        + [pl.BlockSpec(memory_space=pl.ANY)] * len(prevs),
        out_specs=[wspec] * 4, out_shape=[jax.ShapeDtypeStruct((L, R, C), f32)] * 4,
        input_output_aliases={4 + q: q for q in range(len(prevs))},
        compiler_params=_cparams(("parallel", "parallel", "parallel")), name=name,
    )(w, m, v, recv, *prevs)


def adamw_slices_t(w, m, v, recv, *, name):
    L, R, C = w.shape

    def body(w_ref, m_ref, v_ref, r_ref, g_ref, d_ref, nm_ref, nv_ref):
        gt = r_ref[0].astype(f32)
        for k in range(1, NDEV):
            gt = gt + r_ref[k].astype(f32)
        g = gt.T
        delta, nm, nv = _adam_math(w_ref[...], g, m_ref[...], v_ref[...])
        g_ref[...] = g
        d_ref[...] = delta
        nm_ref[...] = nm
        nv_ref[...] = nv

    wspec = pl.BlockSpec((None, R, C), lambda l: (l, 0, 0))
    return pl.pallas_call(
        body, grid=(L,),
        in_specs=[wspec, wspec, wspec, pl.BlockSpec((NDEV, None, C, R), lambda l: (0, l, 0, 0))],
        out_specs=[wspec] * 4, out_shape=[jax.ShapeDtypeStruct((L, R, C), f32)] * 4,
        compiler_params=_cparams(("parallel",)), name=name,
    )(w, m, v, recv)


def adamw_plain(w, m, v, g, *, name):
    R, C = w.shape

    def body(w_ref, m_ref, v_ref, g_ref, d_ref, nm_ref, nv_ref):
        delta, nm, nv = _adam_math(w_ref[...], g_ref[...], m_ref[...], v_ref[...])
        d_ref[...] = delta
        nm_ref[...] = nm
        nv_ref[...] = nv

    return pl.pallas_call(
        body, out_shape=[jax.ShapeDtypeStruct((R, C), f32)] * 3, name=name,
    )(w, m, v, g)


def _chunk_causal_mask():
    pos = jnp.arange(GBLK)
    return (pos[None, :] // CHUNK) <= (pos[:, None] // CHUNK)


def kernel(x, norm_mix_g, norm_ffn_g, final_norm_g, a_w_in, a_v_norm_g, a_w_s, a_b_s, a_w_out, b_w_in, b_conv_w, b_w_out, c_w_in, c_w_grp, c_scale, c_w_out, f_w_gate, f_w_up, f_w_down, loss_target, m_norm_mix_g, m_norm_ffn_g, m_final_norm_g, m_a_w_in, m_a_v_norm_g, m_a_w_s, m_a_b_s, m_a_w_out, m_b_w_in, m_b_conv_w, m_b_w_out, m_c_w_in, m_c_w_grp, m_c_scale, m_c_w_out, m_f_w_gate, m_f_w_up, m_f_w_down, v_norm_mix_g, v_norm_ffn_g, v_final_norm_g, v_a_w_in, v_a_v_norm_g, v_a_w_s, v_a_b_s, v_a_w_out, v_b_w_in, v_b_conv_w, v_b_w_out, v_c_w_in, v_c_w_grp, v_c_scale, v_c_w_out, v_f_w_gate, v_f_w_up, v_f_w_down):
    B, S, _ = x.shape
    T = B * S
    tm = min(512, S)
    tmm = min(1024, T)
    tms = min(512, T)
    tma = min(256, S)
    tmf = min(256, T)
    me = _slot(lax.axis_index("x"), lax.axis_index("y"), lax.axis_index("c"))

    def tb(w):
        return jnp.transpose(w, (0, 2, 1)).astype(bf16)

    small = jnp.concatenate([a_v_norm_g, b_conv_w[0], c_scale, jnp.zeros((2, GBLK), f32)], axis=0)
    started = [gather_start([tb(a_w_in[0:1])[0], small], me, None, name="gather_start_0")]
    tok = started[0][4]
    z = tok[0, 0]
    f_g, f_u, f_d = tb(f_w_gate + z), tb(f_w_up + z), (f_w_down + z).astype(bf16)
    stage_shards = [
        None,
        [(a_w_out[0] + z).astype(bf16), f_g[0], f_u[0], f_d[0]],
        [tb(b_w_in + z)[0], (b_w_out[0] + z).astype(bf16), f_g[1], f_u[1], f_d[1]],
        [(c_w_in[0] + z).astype(bf16), (c_w_grp[0] + z).astype(bf16), (c_w_out[0] + z).astype(bf16), f_g[2], f_u[2], f_d[2]],
        [tb(a_w_in[1:2] + z)[0], (a_w_out[1] + z).astype(bf16), f_g[3], f_u[3], f_d[3]],
    ]
    for s in range(1, len(stage_shards)):
        started.append(gather_start(stage_shards[s], me, tok, name=f"gather_start_{s}"))
        tok = started[-1][4]

    def stage_weights(s, after):
        send_sems, recv_sems, shards, lands, _ = started[s]
        _, lands = gather_wait(send_sems, recv_sems, shards, lands, after, name=f"gather_wait_{s}")
        lands = gather_finish(lands, name=f"gather_finish_{s}")
        return [t.reshape(-1, t.shape[-1]) if t.ndim == 3 else t for t in lands]

    wm_masked = jnp.where(_chunk_causal_mask()[None, None], a_w_s, 0.0).astype(bf16)
    bsb = jnp.broadcast_to(a_b_s[:, :, :, None], a_b_s.shape + (GBLK,))

    xc = x.reshape(T, D)
    saved, weights = [], []
    for i in range(DEPTH):
        kind, j = i % 3, i // 3
        gmix = norm_mix_g[i][None]
        if i == 0:
            w_in, small_g = stage_weights(0, tok)
            small_full = jnp.transpose(small_g.reshape(NDEV, 8, GBLK), (1, 0, 2)).reshape(8, D)
            gv_full, cw_full, cs_full = small_full[0:2], small_full[2:5], small_full[5:6]
            w = [w_in]
        else:
            w = stage_weights(i + 1, xc)
        if kind == 0:
            w_in = w[0]
            p, h, y = a_fwd(xc, gmix, w_in, gv_full[j][None], wm_masked[j], bsb[j], tm=tma, name=f"a_fwd_{i}")
        elif kind == 1:
            w_in, w_out = w[:2]
            p, h = norm_mm(xc, gmix, w_in, tm=tms, transposed=True, name=f"b_in_{i}")
            y = b_mid_fwd(p, cw_full, tm=tm, S=S, name=f"b_mid_{i}")
        else:
            w_in, w_grp, w_out = w[:3]
            p, h = norm_mm(xc, gmix, w_in, tm=tms, transposed=False, name=f"c_in_{i}")
            y = c_mid_fwd(p, w_grp, cs_full, tm=tm, S=S, name=f"c_mid_{i}")
        if i == 0:
            w = w + stage_weights(1, y)
        weights.append(w)
        w_out, w_fg, w_fu, w_fd = w[-4:]
        x1, x2, pg, pu, a, h2 = ffn_fwd(y, w_out, xc, norm_ffn_g[i][None], w_fg, w_fu, w_fd, tm=tmf, name=f"f_fwd_{i}")
        saved.append((xc, p, h, y, x1, pg, pu, h2, a))
        xc = x2

    loss_blk, dx, d_final = final_loss(xc, final_norm_g[None], loss_target.reshape(T, D), tm=tms, name="final_loss")
    loss = lax.psum(loss_blk[0, 0], AXES)

    zone_shapes = [(2, 256, D), (2, GBLK, D), (1, 384, D), (1, GBLK, D), (1, GBLK, D), (1, GBLK, C_GDIM), (1, GBLK, D),
                   (DEPTH, F_LOC, D), (DEPTH, F_LOC, D), (DEPTH, F_LOC, D)]
    Z_AIN, Z_AOUT, Z_BIN, Z_BOUT, Z_CIN, Z_CGRP, Z_COUT, Z_FG, Z_FU, Z_FD = range(10)
    lands = [lax.empty((NDEV,) + s, bf16) for s in zone_shapes]
    parts = []

    def exchange(items, after, name):
        nonlocal lands
        for g, zi, l in items:
            own = lax.dynamic_index_in_dim(g, me, axis=0, keepdims=True)[None]
            lands[zi] = lax.dynamic_update_slice(lands[zi], own, (me, l) + (0,) * (g.ndim - 1))
        idx = sorted({zi for _, zi, _ in items})
        local = [(g, idx.index(zi), l) for g, zi, l in items]
        send_sems, recv_sems, gs, zs, token = exchange_start(local, [lands[zi] for zi in idx], after, name=name)
        for q, zi in enumerate(idx):
            lands[zi] = zs[q]
        parts.append(([(None, zi, l) for _, zi, l in items], send_sems, recv_sems, gs))
        return token

    d_mix, d_ffn = [None] * DEPTH, [None] * DEPTH
    d_gv, d_wm, d_bs = [None] * 2, [None] * 2, [None] * 2
    tok = None
    for i in reversed(range(DEPTH)):
        kind, j = i % 3, i // 3
        xin, p, h, y, x1, pg, pu, h2, a = saved[i]
        w = weights[i]
        w_in, w_out = w[0], w[-4]
        w_fg, w_fu, w_fd = w[-3:]
        dpg, dpu, dxb, dx1, d_ffn[i], dy, dx1b = ffn_bwd(dx, w_fd, pg, pu, w_fg, w_fu, x1, norm_ffn_g[i][None], tok, w_out,
                                                         tm=tmf, name=f"f_bwd_{i}")
        gw_g, gw_u, gw_d = ffn_dw(dpg, dpu, a, h2, dxb, name=f"f_dw_{i}")
        tok = exchange([(gw_g, Z_FG, i), (gw_u, Z_FU, i), (gw_d, Z_FD, i)], tok, f"exchange_start_ffn_{i}")
        gmix = norm_mix_g[i][None]
        if kind == 0:
            dp, d_gv[j], d_wm[j], d_bs[j], dx, d_mix[i] = a_mid_bwd(
                p, dy, gv_full[j][None], wm_masked[j], bsb[j], w_in, xin, gmix, tok, dx1, tm=tma, name=f"a_bwd_{i}")
            items = [(mm_tn(dp, h, name=f"a_dw_in_{i}"), Z_AIN, j),
                     (mm_tn(y, dx1b, name=f"a_dw_out_{i}"), Z_AOUT, j)]
        elif kind == 1:
            dp, d_cw, dx, d_mix[i] = b_mid_bwd(p, dy, cw_full, w_in, xin, gmix, tok, dx1, tm=tm, S=S, name=f"b_bwd_{i}")
            items = [(mm_tn(dp, h, name=f"b_dw_in_{i}"), Z_BIN, 0),
                     (mm_tn(y, dx1b, name=f"b_dw_out_{i}"), Z_BOUT, 0)]
        else:
            dp, d_cs, d_wgrp, dx, d_mix[i] = c_mid_bwd(p, dy, w[1], cs_full, w_in, xin, gmix, tok, dx1, tm=tm, S=S,
                                                       name=f"c_bwd_{i}")
            gw_cgrp = jnp.transpose(d_wgrp.reshape(C_GROUPS, NDEV, C_GDIM // NDEV, C_GDIM), (1, 0, 2, 3)).astype(bf16)
            items = [(mm_tn(h, dp, name=f"c_dw_in_{i}"), Z_CIN, 0),
                     (gw_cgrp.reshape(NDEV, GBLK, C_GDIM), Z_CGRP, 0),
                     (mm_tn(y, dx1b, name=f"c_dw_out_{i}"), Z_COUT, 0)]
        if i == 0:
            small_parts = [
                jnp.concatenate(d_mix, axis=0).reshape(-1, GBLK),
                jnp.concatenate(d_ffn, axis=0).reshape(-1, GBLK),
                d_final.reshape(-1, GBLK),
                jnp.concatenate(d_gv, axis=0).reshape(-1, GBLK),
                jnp.where(_chunk_causal_mask()[None, None], jnp.stack(d_wm), 0.0).reshape(-1, GBLK),
                jnp.stack(d_bs).reshape(-1, GBLK),
                d_cw.reshape(-1, GBLK),
                d_cs.reshape(-1, GBLK),
            ]
            small_rows = [q.shape[0] for q in small_parts]
            sg_send, sg_recv, sg_shards, sg_lands, tok = gather_start(
                [jnp.concatenate(small_parts, axis=0)], me, tok, name="gather_small_start")
        tok = exchange(items, tok, f"exchange_start_mix_{i}")
    grad_x = dx.reshape(B, S, D)

    def plain(w, m, v, g, name):
        shp = w.shape
        w2, m2, v2, g2 = (t.reshape(-1, shp[-1]) for t in (w, m, v, g))
        return tuple(t.reshape(shp) for t in adamw_plain(w2, m2, v2, g2, name=name))

    def sliced(w, m, v, recv, name, **kw):
        shp = w.shape
        w3, m3, v3 = (t.reshape((shp[0], -1, shp[-1])) for t in (w, m, v))
        return tuple(t.reshape(shp) for t in adamw_slices(w3, m3, v3, recv, name=name, **kw))

    def sliced_t(w, m, v, recv, name, **kw):
        wt, mt, vt = (jnp.transpose(t, (0, 2, 1)) for t in (w, m, v))
        return adamw_slices(wt, mt, vt, recv, name=name, **kw)

    def untransposed(outs):
        return tuple(jnp.transpose(t, (0, 2, 1)) for t in outs)

    lands = list(exchange_wait(parts[:6], lands, dx, name="exchange_wait"))
    _, _, r_bin, r_bout, r_cin, r_cgrp, r_cout, r_fg, r_fu, r_fd = lands
    res = {}
    res["b_w_in"] = adamw_slices_t(b_w_in, m_b_w_in, v_b_w_in, r_bin, name="adam_b_w_in")
    res["b_w_out"] = sliced(b_w_out, m_b_w_out, v_b_w_out, r_bout, "adam_b_w_out")
    res["c_w_in"] = sliced(c_w_in, m_c_w_in, v_c_w_in, r_cin, "adam_c_w_in")
    res["c_w_grp"] = sliced(c_w_grp, m_c_w_grp, v_c_w_grp, r_cgrp, "adam_c_w_grp")
    res["c_w_out"] = sliced(c_w_out, m_c_w_out, v_c_w_out, r_cout, "adam_c_w_out")
    fg = sliced_t(f_w_gate, m_f_w_gate, v_f_w_gate, r_fg, "adam_f_gate_123", tc=256, l0=1)
    fu = sliced_t(f_w_up, m_f_w_up, v_f_w_up, r_fu, "adam_f_up_123", tc=256, l0=1)
    fd = adamw_slices(f_w_down, m_f_w_down, v_f_w_down, r_fd, tc=256, l0=1, name="adam_f_down_123")
    big = [res[n][1] for n in ("b_w_in", "b_w_out", "c_w_in", "c_w_grp", "c_w_out")] + [fg[1], fu[1], fd[1]]

    ffn0_items, ffn0_send, ffn0_recv, ffn0_gs = parts[6]
    ffn0_zones = [Z_FG, Z_FU, Z_FD]
    ffn0_part = ([(None, ffn0_zones.index(zi), l) for _, zi, l in ffn0_items], ffn0_send, ffn0_recv, ffn0_gs)
    r_fg, r_fu, r_fd = exchange_wait([ffn0_part], [lands[zi] for zi in ffn0_zones], big, name="exchange_wait_ffn0")
    fg = sliced_t(f_w_gate, m_f_w_gate, v_f_w_gate, r_fg, "adam_f_gate_0", tc=256, l0=0, nl=1, prev=fg)
    fu = sliced_t(f_w_up, m_f_w_up, v_f_w_up, r_fu, "adam_f_up_0", tc=256, l0=0, nl=1, prev=fu)
    fd = adamw_slices(f_w_down, m_f_w_down, v_f_w_down, r_fd, tc=256, l0=0, nl=1, prev=fd, name="adam_f_down_0")
    res["f_w_gate"], res["f_w_up"], res["f_w_down"] = untransposed(fg), untransposed(fu), tuple(fd)

    last_items, last_send, last_recv, last_gs = parts[7]
    last_zones = [Z_AIN, Z_AOUT]
    last_part = ([(None, last_zones.index(zi), l) for _, zi, l in last_items], last_send, last_recv, last_gs)
    r_ain, r_aout = exchange_wait([last_part], [lands[zi] for zi in last_zones], [fg[1], fu[1], fd[1]], name="exchange_wait_last")
    res["a_w_in"] = adamw_slices_t(a_w_in, m_a_w_in, v_a_w_in, r_ain, name="adam_a_w_in")
    res["a_w_out"] = sliced(a_w_out, m_a_w_out, v_a_w_out, r_aout, "adam_a_w_out")

    _, sg_lands = gather_wait(sg_send, sg_recv, sg_shards, sg_lands, [res["a_w_in"][1], res["a_w_out"][1]], name="gather_small_wait")
    (gs_all,) = gather_finish(sg_lands, name="gather_small_finish")
    gs = sum8(gs_all, name="sum_small_grads")
    offs = [0]
    for r in small_rows:
        offs.append(offs[-1] + r)
    sp = [gs[offs[q]:offs[q + 1]] for q in range(len(small_parts))]
    grad_norm_mix_g = sp[0].reshape(DEPTH, D)
    grad_norm_ffn_g = sp[1].reshape(DEPTH, D)
    grad_final_norm_g = sp[2].reshape(D)
    grad_a_w_s = sp[4].reshape(a_w_s.shape)
    grad_a_b_s = sp[5].reshape(a_b_s.shape)

    def my_cols(full):
        return lax.dynamic_slice_in_dim(full, me * GBLK, GBLK, axis=1)

    grad_a_v_norm_g = my_cols(sp[3].reshape(2, D))
    grad_b_conv_w = my_cols(sp[6].reshape(3, D))[None]
    grad_c_scale = my_cols(sp[7].reshape(1, D))
    res["norm_mix_g"] = (grad_norm_mix_g,) + plain(norm_mix_g, m_norm_mix_g, v_norm_mix_g, grad_norm_mix_g, "adam_norm_mix")
    res["norm_ffn_g"] = (grad_norm_ffn_g,) + plain(norm_ffn_g, m_norm_ffn_g, v_norm_ffn_g, grad_norm_ffn_g, "adam_norm_ffn")
    res["final_norm_g"] = (grad_final_norm_g,) + tuple(
        t.reshape(D) for t in plain(final_norm_g[None], m_final_norm_g[None], v_final_norm_g[None], grad_final_norm_g[None], "adam_final"))
    res["a_v_norm_g"] = (grad_a_v_norm_g,) + plain(a_v_norm_g, m_a_v_norm_g, v_a_v_norm_g, grad_a_v_norm_g, "adam_a_v_norm")
    res["a_w_s"] = (grad_a_w_s,) + plain(a_w_s, m_a_w_s, v_a_w_s, grad_a_w_s, "adam_a_w_s")
    res["a_b_s"] = (grad_a_b_s,) + plain(a_b_s, m_a_b_s, v_a_b_s, grad_a_b_s, "adam_a_b_s")
    res["b_conv_w"] = (grad_b_conv_w,) + plain(b_conv_w, m_b_conv_w, v_b_conv_w, grad_b_conv_w, "adam_b_conv")
    res["c_scale"] = (grad_c_scale,) + plain(c_scale, m_c_scale, v_c_scale, grad_c_scale, "adam_c_scale")

    order = ["norm_mix_g", "norm_ffn_g", "final_norm_g", "a_w_in", "a_v_norm_g", "a_w_s", "a_b_s", "a_w_out", "b_w_in",
             "b_conv_w", "b_w_out", "c_w_in", "c_w_grp", "c_scale", "c_w_out", "f_w_gate", "f_w_up", "f_w_down"]
    return (loss, grad_x, *[res[n][0] for n in order], *[res[n][1] for n in order],
            *[res[n][2] for n in order], *[res[n][3] for n in order])
```

```python
import jax
import jax.numpy as jnp
from jax import lax
from jax.experimental import pallas as pl
from jax.experimental.pallas import tpu as pltpu

f32 = jnp.float32
bf16 = jnp.bfloat16

NDEV = 8
D = 1024
EPS = 1e-6
CHUNK = 64
GBLK = 128
A_GROUPS = 8
C_GROUPS = 4
C_GDIM = 256
POOL_WINDOWS = (2, 4, 8, 16)
HALO = 16
COLS = 256
F_LOC = 352
MXU_COLS = 256
DEPTH = 4
AXES = ("x", "y", "c")
MESH = pl.DeviceIdType.MESH

ADAM_LR = 0.001
ADAM_B1 = 0.9
ADAM_B2 = 0.999
ADAM_EPS = 1e-08
ADAM_WD = 0.01
ADAM_STEP = 10

VMEM_LIMIT = 56 * 1024 * 1024


def _cparams(sem):
    return pltpu.CompilerParams(dimension_semantics=sem, vmem_limit_bytes=VMEM_LIMIT)


def _gelu(z):
    return 0.5 * z * (1.0 + lax.erf(z * 0.7071067811865476))


def _gelu_cdf(z):
    return 0.5 * (1.0 + lax.erf(z * 0.7071067811865476))


def _gelu_grad(z, cdf):
    return cdf + z * jnp.exp(-0.5 * z * z) * 0.3989422804014327


def _dot(a, b):
    return jnp.dot(a, b, preferred_element_type=f32)


def _dot_nt(a, b):
    return lax.dot_general(a, b, (((1,), (1,)), ((), ())), preferred_element_type=f32)


def _dot_tn(a, b):
    return lax.dot_general(a, b, (((0,), (0,)), ((), ())), preferred_element_type=f32)


def _col_chunks(n, width=1024):
    return [(c, min(c + width, n)) for c in range(0, n, width)]


def norm_mm(x, g, w, *, tm, transposed, name):
    T = x.shape[0]
    n = w.shape[0] if transposed else w.shape[1]

    def body(x_ref, g_ref, w_ref, p_ref, h_ref):
        xv = x_ref[...]
        r = lax.rsqrt(jnp.mean(xv * xv, axis=-1, keepdims=True) + EPS)
        h = (xv * r * g_ref[...]).astype(bf16)
        h_ref[...] = h
        p = _dot_nt(h, w_ref[...]) if transposed else _dot(h, w_ref[...])
        p_ref[...] = p.astype(bf16)

    return pl.pallas_call(
        body, grid=(T // tm,),
        in_specs=[pl.BlockSpec((tm, D), lambda i: (i, 0)), pl.BlockSpec((1, D), lambda i: (0, 0)),
                  pl.BlockSpec(w.shape, lambda i: (0, 0), pipeline_mode=pl.Buffered(1))],
        out_specs=[pl.BlockSpec((tm, n), lambda i: (i, 0)), pl.BlockSpec((tm, D), lambda i: (i, 0))],
        out_shape=[jax.ShapeDtypeStruct((T, n), bf16), jax.ShapeDtypeStruct((T, D), bf16)],
        compiler_params=_cparams(("parallel",)), name=name,
    )(x, g, w)


def ffn_fwd(y, w_out, xin, g, wtg, wtu, wd, *, tm, name):
    T, K = y.shape
    H = wtg.shape[0]

    def body(y_ref, wo_ref, xin_ref, g_ref, wg_ref, wu_ref, wd_ref, x_ref, x2_ref, pg_ref, pu_ref, a_ref, h_ref):
        xv = xin_ref[...] + _dot(y_ref[...], wo_ref[...])
        x_ref[...] = xv
        r = lax.rsqrt(jnp.mean(xv * xv, axis=-1, keepdims=True) + EPS)
        h = (xv * r * g_ref[...]).astype(bf16)
        h_ref[...] = h
        out = None
        for c0, c1 in _col_chunks(H):
            gate = _dot_nt(h, wg_ref[c0:c1, :])
            up = _dot_nt(h, wu_ref[c0:c1, :])
            pg_ref[:, c0:c1] = gate.astype(bf16)
            pu_ref[:, c0:c1] = up.astype(bf16)
            ac = (gate * jax.nn.sigmoid(gate) * up).astype(bf16)
            a_ref[:, c0:c1] = ac
            part = _dot(ac, wd_ref[c0:c1, :])
            out = part if out is None else out + part
        x2_ref[...] = xv + out

    wspec = pl.BlockSpec((H, D), lambda i: (0, 0), pipeline_mode=pl.Buffered(1))
    hspec = pl.BlockSpec((tm, H), lambda i: (i, 0))
    dspec = pl.BlockSpec((tm, D), lambda i: (i, 0))
    return pl.pallas_call(
        body, grid=(T // tm,),
        in_specs=[pl.BlockSpec((tm, K), lambda i: (i, 0)),
                  pl.BlockSpec((K, D), lambda i: (0, 0), pipeline_mode=pl.Buffered(1)),
                  dspec, pl.BlockSpec((1, D), lambda i: (0, 0)), wspec, wspec, wspec],
        out_specs=[dspec, dspec, hspec, hspec, hspec, dspec],
        out_shape=[jax.ShapeDtypeStruct((T, D), f32)] * 2 + [jax.ShapeDtypeStruct((T, H), bf16)] * 3
        + [jax.ShapeDtypeStruct((T, D), bf16)],
        compiler_params=_cparams(("parallel",)), name=name,
    )(y, w_out, xin, g, wtg, wtu, wd)


def ffn_bwd(dx, wd, pg, pu, wtg, wtu, x, g, tok, w_next, *, tm, name):
    T = dx.shape[0]
    H = wd.shape[0]
    kn = w_next.shape[0]
    toks = [] if tok is None else [tok]

    def body(dx_ref, wd_ref, pg_ref, pu_ref, wg_ref, wu_ref, x_ref, g_ref, *rest):
        wn_ref = rest[len(toks)]
        dg_ref, du_ref, dxb_ref, dx1_ref, dgn_ref, dy_ref, dx1b_ref = rest[len(toks) + 1:]

        @pl.when(pl.program_id(0) == 0)
        def _():
            dgn_ref[...] = jnp.zeros(dgn_ref.shape, f32)

        dxv = dx_ref[...]
        dxb = dxv.astype(bf16)
        dxb_ref[...] = dxb
        dh = None
        for c0, c1 in _col_chunks(H):
            da = _dot_nt(dxb, wd_ref[c0:c1, :])
            gate = pg_ref[:, c0:c1].astype(f32)
            up = pu_ref[:, c0:c1].astype(f32)
            sg = jax.nn.sigmoid(gate)
            dgc = (da * up * (sg * (1.0 + gate * (1.0 - sg)))).astype(bf16)
            duc = (da * gate * sg).astype(bf16)
            dg_ref[:, c0:c1] = dgc
            du_ref[:, c0:c1] = duc
            part = _dot(dgc, wg_ref[c0:c1, :]) + _dot(duc, wu_ref[c0:c1, :])
            dh = part if dh is None else dh + part
        gv = g_ref[...] + rest[0][0:1, 0:1] if toks else g_ref[...]
        xv = x_ref[...]
        r = lax.rsqrt(jnp.mean(xv * xv, axis=-1, keepdims=True) + EPS)
        xhat = xv * r
        dgn_ref[...] += jnp.sum(dh * xhat, axis=0, keepdims=True)
        dxhat = dh * gv
        dx1 = dxv + r * (dxhat - xhat * jnp.mean(dxhat * xhat, axis=-1, keepdims=True))
        dx1_ref[...] = dx1
        dx1b = dx1.astype(bf16)
        dx1b_ref[...] = dx1b
        dy_ref[...] = _dot_nt(dx1b, wn_ref[...]).astype(bf16)

    dspec = pl.BlockSpec((tm, D), lambda i: (i, 0))
    hspec = pl.BlockSpec((tm, H), lambda i: (i, 0))
    wspec = pl.BlockSpec((H, D), lambda i: (0, 0), pipeline_mode=pl.Buffered(1))
    gspec = pl.BlockSpec((1, D), lambda i: (0, 0))
    return pl.pallas_call(
        body, grid=(T // tm,),
        in_specs=[dspec, wspec, hspec, hspec, wspec, wspec, dspec, gspec]
        + [pl.BlockSpec((8, 128), lambda i: (0, 0))] * len(toks)
        + [pl.BlockSpec((kn, D), lambda i: (0, 0), pipeline_mode=pl.Buffered(1))],
        out_specs=[hspec, hspec, dspec, dspec, gspec, pl.BlockSpec((tm, kn), lambda i: (i, 0)), dspec],
        out_shape=[jax.ShapeDtypeStruct((T, H), bf16)] * 2 + [jax.ShapeDtypeStruct((T, D), bf16),
                   jax.ShapeDtypeStruct((T, D), f32), jax.ShapeDtypeStruct((1, D), f32),
                   jax.ShapeDtypeStruct((T, kn), bf16), jax.ShapeDtypeStruct((T, D), bf16)],
        compiler_params=_cparams(("arbitrary",)), name=name,
    )(dx, wd, pg, pu, wtg, wtu, x, g, *toks, w_next)


def mm_tn(a, b, *, name):
    T, K = a.shape
    n = b.shape[1]

    def body(a_ref, b_ref, o_ref):
        o_ref[...] = _dot_tn(a_ref[...], b_ref[...]).astype(bf16)

    out = pl.pallas_call(
        body, grid=(K // MXU_COLS,),
        in_specs=[pl.BlockSpec((T, MXU_COLS), lambda d: (0, d)),
                  pl.BlockSpec((T, n), lambda d: (0, 0), pipeline_mode=pl.Buffered(1))],
        out_specs=pl.BlockSpec((MXU_COLS, n), lambda d: (d, 0)),
        out_shape=jax.ShapeDtypeStruct((K, n), bf16),
        compiler_params=_cparams(("parallel",)), name=name,
    )(a, b)
    return out.reshape(NDEV, K // NDEV, n)


def ffn_dw(dpg, dpu, a, h2, dxb, *, name):
    T, H = dpg.shape

    def body(g_ref, u_ref, a_ref, h_ref, x_ref, og_ref, ou_ref, od_ref):
        h = h_ref[...]
        og_ref[...] = _dot_tn(g_ref[...], h).astype(bf16)
        ou_ref[...] = _dot_tn(u_ref[...], h).astype(bf16)
        od_ref[...] = _dot_tn(a_ref[...], x_ref[...]).astype(bf16)

    blk = pl.BlockSpec((T, MXU_COLS), lambda d: (0, d))
    full = pl.BlockSpec((T, D), lambda d: (0, 0), pipeline_mode=pl.Buffered(1))
    out = pl.BlockSpec((MXU_COLS, D), lambda d: (d, 0))
    outs = pl.pallas_call(
        body, grid=(H // MXU_COLS,),
        in_specs=[blk, blk, blk, full, full], out_specs=[out] * 3,
        out_shape=[jax.ShapeDtypeStruct((H, D), bf16)] * 3,
        compiler_params=_cparams(("parallel",)), name=name,
    )(dpg, dpu, a, h2, dxb)
    return [o.reshape(NDEV, H // NDEV, D) for o in outs]


def _normbwd_tail(dh, x_ref, g_ref, tok_ref, dres_ref, dx_ref, dgn_ref):
    gv = g_ref[...] + tok_ref[0:1, 0:1]
    xv = x_ref[...]
    r = lax.rsqrt(jnp.mean(xv * xv, axis=-1, keepdims=True) + EPS)
    xhat = xv * r
    dgn_ref[...] += jnp.sum(dh * xhat, axis=0, keepdims=True)
    dxhat = dh * gv
    dx_ref[...] = dres_ref[...] + r * (dxhat - xhat * jnp.mean(dxhat * xhat, axis=-1, keepdims=True))


def _normbwd_specs(tm, w):
    dspec = pl.BlockSpec((tm, D), lambda i: (i, 0))
    gspec = pl.BlockSpec((1, D), lambda i: (0, 0))
    ins = [pl.BlockSpec(w.shape, lambda i: (0, 0), pipeline_mode=pl.Buffered(1)), dspec, gspec,
           pl.BlockSpec((8, 128), lambda i: (0, 0)), dspec]
    return ins, [dspec, gspec]


def _prev_halo_spec(tm, ncol):
    return pl.BlockSpec((HALO, ncol), lambda i: (jnp.maximum(i * (tm // HALO) - 1, 0), 0))


def _next_halo_spec(tm, ncol, T):
    return pl.BlockSpec((HALO, ncol), lambda i: (jnp.minimum((i + 1) * (tm // HALO), T // HALO - 1), 0))


def a_fwd(x, g, wt, gv, wm, bsb, *, tm, name):
    T = x.shape[0]

    def body(x_ref, g_ref, wt_ref, gv_ref, wm_ref, bs_ref, p_ref, h_ref, y_ref, vn_sc):
        xv = x_ref[...]
        r = lax.rsqrt(jnp.mean(xv * xv, axis=-1, keepdims=True) + EPS)
        h = (xv * r * g_ref[...]).astype(bf16)
        h_ref[...] = h
        p_ref[...] = _dot_nt(h, wt_ref[...]).astype(bf16)
        v = _gelu(p_ref[:, D:].astype(f32))
        vc = v - jnp.mean(v, axis=-1, keepdims=True)
        var = jnp.mean(vc * vc, axis=-1, keepdims=True)
        vn_sc[...] = (vc * lax.rsqrt(var + EPS) * gv_ref[...]).astype(bf16)
        for g in range(A_GROUPS):
            cs = slice(g * GBLK, (g + 1) * GBLK)
            for r in range(tm // GBLK):
                rs = slice(r * GBLK, (r + 1) * GBLK)
                sv = _dot(wm_ref[g], vn_sc[rs, cs]) + bs_ref[g]
                y_ref[rs, cs] = (_gelu(p_ref[rs, cs].astype(f32)) * sv).astype(bf16)

    dspec = pl.BlockSpec((tm, D), lambda i: (i, 0))
    gspec = pl.BlockSpec((1, D), lambda i: (0, 0))
    return pl.pallas_call(
        body, grid=(T // tm,),
        in_specs=[dspec, gspec, pl.BlockSpec(wt.shape, lambda i: (0, 0), pipeline_mode=pl.Buffered(1)), gspec,
                  pl.BlockSpec((A_GROUPS, GBLK, GBLK), lambda i: (0, 0, 0)),
                  pl.BlockSpec((A_GROUPS, GBLK, GBLK), lambda i: (0, 0, 0))],
        out_specs=[pl.BlockSpec((tm, 2 * D), lambda i: (i, 0)), dspec, dspec],
        out_shape=[jax.ShapeDtypeStruct((T, 2 * D), bf16), jax.ShapeDtypeStruct((T, D), bf16),
                   jax.ShapeDtypeStruct((T, D), bf16)],
        scratch_shapes=[pltpu.VMEM((tm, D), bf16)],
        compiler_params=_cparams(("parallel",)), name=name,
    )(x, g, wt, gv, wm, bsb)


def a_mid_bwd(p, dy, gv, wm, bsb, wt, x, g, tok, dres, *, tm, name):
    T = p.shape[0]

    def body(p_ref, dy_ref, gv_ref, wm_ref, bs_ref, wt_ref, x_ref, g_ref, tok_ref, dres_ref,
             dp_ref, dgv_ref, dwm_ref, dbs_ref, dx_ref, dgn_ref, vn_sc, dvn_sc):
        @pl.when(pl.program_id(0) == 0)
        def _():
            dgn_ref[...] = jnp.zeros(dgn_ref.shape, f32)
            dgv_ref[...] = jnp.zeros(dgv_ref.shape, f32)
            dwm_ref[...] = jnp.zeros(dwm_ref.shape, f32)
            dbs_ref[...] = jnp.zeros(dbs_ref.shape, f32)

        zv = p_ref[:, D:].astype(f32)
        cdf_v = _gelu_cdf(zv)
        v = zv * cdf_v
        vc = v - jnp.mean(v, axis=-1, keepdims=True)
        rstd = lax.rsqrt(jnp.mean(vc * vc, axis=-1, keepdims=True) + EPS)
        vhat = vc * rstd
        vn_sc[...] = (vhat * gv_ref[...]).astype(bf16)
        for g in range(A_GROUPS):
            cs = slice(g * GBLK, (g + 1) * GBLK)
            dwm = jnp.zeros((GBLK, GBLK), f32)
            dbs = jnp.zeros((GBLK, 1), f32)
            for r in range(tm // GBLK):
                rs = slice(r * GBLK, (r + 1) * GBLK)
                zu = p_ref[rs, cs].astype(f32)
                cdf_u = _gelu_cdf(zu)
                vn = vn_sc[rs, cs]
                sv = _dot(wm_ref[g], vn) + bs_ref[g]
                dyb = dy_ref[rs, cs].astype(f32)
                dsv = dyb * (zu * cdf_u)
                dsvb = dsv.astype(bf16)
                dp_ref[rs, cs] = (dyb * sv * _gelu_grad(zu, cdf_u)).astype(bf16)
                dwm += _dot_nt(dsvb, vn)
                dbs += jnp.sum(dsv, axis=1, keepdims=True)
                dvn_sc[rs, cs] = _dot_tn(wm_ref[g], dsvb)
            dwm_ref[g] += dwm
            dbs_ref[g] += dbs
        dvn = dvn_sc[...]
        dgv_ref[...] += jnp.sum(dvn * vhat, axis=0, keepdims=True)
        dvhat = dvn * gv_ref[...]
        dv = rstd * (dvhat - jnp.mean(dvhat, axis=-1, keepdims=True)
                     - vhat * jnp.mean(dvhat * vhat, axis=-1, keepdims=True))
        dp_ref[:, D:] = (dv * _gelu_grad(zv, cdf_v)).astype(bf16)
        _normbwd_tail(_dot(dp_ref[...], wt_ref[...]), x_ref, g_ref, tok_ref, dres_ref, dx_ref, dgn_ref)

    tail_in, tail_out = _normbwd_specs(tm, wt)
    return pl.pallas_call(
        body, grid=(T // tm,),
        in_specs=[pl.BlockSpec((tm, 2 * D), lambda i: (i, 0)), pl.BlockSpec((tm, D), lambda i: (i, 0)),
                  pl.BlockSpec((1, D), lambda i: (0, 0)),
                  pl.BlockSpec((A_GROUPS, GBLK, GBLK), lambda i: (0, 0, 0)),
                  pl.BlockSpec((A_GROUPS, GBLK, GBLK), lambda i: (0, 0, 0))] + tail_in,
        out_specs=[pl.BlockSpec((tm, 2 * D), lambda i: (i, 0)), pl.BlockSpec((1, D), lambda i: (0, 0)),
                   pl.BlockSpec((A_GROUPS, GBLK, GBLK), lambda i: (0, 0, 0)),
                   pl.BlockSpec((A_GROUPS, GBLK, 1), lambda i: (0, 0, 0))] + tail_out,
        out_shape=[jax.ShapeDtypeStruct((T, 2 * D), bf16), jax.ShapeDtypeStruct((1, D), f32),
                   jax.ShapeDtypeStruct((A_GROUPS, GBLK, GBLK), f32), jax.ShapeDtypeStruct((A_GROUPS, GBLK, 1), f32),
                   jax.ShapeDtypeStruct((T, D), f32), jax.ShapeDtypeStruct((1, D), f32)],
        scratch_shapes=[pltpu.VMEM((tm, D), bf16), pltpu.VMEM((tm, D), f32)],
        compiler_params=_cparams(("arbitrary",)), name=name,
    )(p, dy, gv, wm, bsb, wt, x, g, tok, dres)


def b_mid_fwd(p, cw, *, tm, S, name):
    T = p.shape[0]
    nst = S // tm

    def body(p_ref, ph_ref, cw_ref, y_ref, ext):
        first = (pl.program_id(0) % nst) == 0
        for c0 in range(0, D, COLS):
            cs, cc, cx = slice(c0, c0 + COLS), slice(D + c0, D + c0 + COLS), slice(2 * D + c0, 2 * D + c0 + COLS)
            q = p_ref[:, cc].astype(f32) * p_ref[:, cx].astype(f32)
            qh = ph_ref[:, cc].astype(f32) * ph_ref[:, cx].astype(f32)
            ext[0:HALO, cs] = jnp.where(first, 0.0, qh)
            ext[HALO:, cs] = q
            y = (cw_ref[0:1, cs] * ext[pl.ds(HALO - 2, tm), cs] + cw_ref[1:2, cs] * ext[pl.ds(HALO - 1, tm), cs]
                 + cw_ref[2:3, cs] * q)
            y_ref[:, cs] = (p_ref[:, cs].astype(f32) * y).astype(bf16)

    return pl.pallas_call(
        body, grid=(T // tm,),
        in_specs=[pl.BlockSpec((tm, 3 * D), lambda i: (i, 0)), _prev_halo_spec(tm, 3 * D),
                  pl.BlockSpec((3, D), lambda i: (0, 0))],
        out_specs=pl.BlockSpec((tm, D), lambda i: (i, 0)),
        out_shape=jax.ShapeDtypeStruct((T, D), bf16),
        scratch_shapes=[pltpu.VMEM((tm + HALO, D), f32)],
        compiler_params=_cparams(("parallel",)), name=name,
    )(p, p, cw)


def b_mid_bwd(p, dy, cw, wt, x, g, tok, dres, *, tm, S, name):
    T = p.shape[0]
    nst = S // tm

    def body(p_ref, ph_ref, pn_ref, dy_ref, dyn_ref, cw_ref, wt_ref, x_ref, g_ref, tok_ref, dres_ref,
             dp_ref, dcw_ref, dx_ref, dgn_ref, ext, ext2):
        i = pl.program_id(0)
        first = (i % nst) == 0
        last = (i % nst) == nst - 1

        @pl.when(i == 0)
        def _():
            dgn_ref[...] = jnp.zeros(dgn_ref.shape, f32)
            dcw_ref[...] = jnp.zeros(dcw_ref.shape, f32)

        for c0 in range(0, D, COLS):
            cs, cc, cx = slice(c0, c0 + COLS), slice(D + c0, D + c0 + COLS), slice(2 * D + c0, 2 * D + c0 + COLS)
            gb, gc, xt = p_ref[:, cs].astype(f32), p_ref[:, cc].astype(f32), p_ref[:, cx].astype(f32)
            q = gc * xt
            ext[0:HALO, cs] = jnp.where(first, 0.0, ph_ref[:, cc].astype(f32) * ph_ref[:, cx].astype(f32))
            ext[HALO:, cs] = q
            q2 = ext[pl.ds(HALO - 2, tm), cs]
            q1 = ext[pl.ds(HALO - 1, tm), cs]
            y = cw_ref[0:1, cs] * q2 + cw_ref[1:2, cs] * q1 + cw_ref[2:3, cs] * q
            dyo = dy_ref[:, cs].astype(f32)
            dp_ref[:, cs] = (dyo * y).astype(bf16)
            dyc = dyo * gb
            ext2[0:tm, cs] = dyc
            ext2[tm:, cs] = jnp.where(last, 0.0, dyn_ref[:, cs].astype(f32) * pn_ref[:, cs].astype(f32))
            dq = cw_ref[2:3, cs] * dyc + cw_ref[1:2, cs] * ext2[pl.ds(1, tm), cs] + cw_ref[0:1, cs] * ext2[pl.ds(2, tm), cs]
            dp_ref[:, cc] = (dq * xt).astype(bf16)
            dp_ref[:, cx] = (dq * gc).astype(bf16)
            dcw_ref[0:1, cs] += jnp.sum(dyc * q2, axis=0, keepdims=True)
            dcw_ref[1:2, cs] += jnp.sum(dyc * q1, axis=0, keepdims=True)
            dcw_ref[2:3, cs] += jnp.sum(dyc * q, axis=0, keepdims=True)
        _normbwd_tail(_dot(dp_ref[...], wt_ref[...]), x_ref, g_ref, tok_ref, dres_ref, dx_ref, dgn_ref)

    tail_in, tail_out = _normbwd_specs(tm, wt)
    return pl.pallas_call(
        body, grid=(T // tm,),
        in_specs=[pl.BlockSpec((tm, 3 * D), lambda i: (i, 0)), _prev_halo_spec(tm, 3 * D), _next_halo_spec(tm, 3 * D, T),
                  pl.BlockSpec((tm, D), lambda i: (i, 0)), _next_halo_spec(tm, D, T),
                  pl.BlockSpec((3, D), lambda i: (0, 0))] + tail_in,
        out_specs=[pl.BlockSpec((tm, 3 * D), lambda i: (i, 0)), pl.BlockSpec((3, D), lambda i: (0, 0))] + tail_out,
        out_shape=[jax.ShapeDtypeStruct((T, 3 * D), bf16), jax.ShapeDtypeStruct((3, D), f32),
                   jax.ShapeDtypeStruct((T, D), f32), jax.ShapeDtypeStruct((1, D), f32)],
        scratch_shapes=[pltpu.VMEM((tm + HALO, D), f32), pltpu.VMEM((tm + HALO, D), f32)],
        compiler_params=_cparams(("arbitrary",)), name=name,
    )(p, p, p, dy, dy, cw, wt, x, g, tok, dres)


def _pool_counts(i, nst, tm, rows, row0, w):
    t = (i % nst) * tm + row0 + lax.broadcasted_iota(jnp.int32, (rows, 1), 0)
    return jnp.minimum(t + 1, w).astype(f32)


def _pool_diff(p_ref, ext, g, i, nst, tm):
    w = POOL_WINDOWS[g]
    cs = slice(g * C_GDIM, (g + 1) * C_GDIM)
    pg = p_ref[:, cs].astype(f32)
    s = pg
    for k in range(1, w):
        s = s + ext[pl.ds(HALO - k, tm), cs]
    return s / _pool_counts(i, nst, tm, tm, 0, w) - pg


def c_mid_fwd(p, wg, scale, *, tm, S, name):
    T = p.shape[0]
    nst = S // tm

    def body(p_ref, ph_ref, wg_ref, sc_ref, y_ref, ext):
        i = pl.program_id(0)
        first = (i % nst) == 0
        ext[0:HALO, :] = jnp.where(first, 0.0, ph_ref[...].astype(f32))
        ext[HALO:, :] = p_ref[...].astype(f32)
        for g in range(C_GROUPS):
            cs = slice(g * C_GDIM, (g + 1) * C_GDIM)
            dg = _pool_diff(p_ref, ext, g, i, nst, tm).astype(bf16)
            wv = wg_ref[:, g].reshape(C_GDIM, C_GDIM)
            y_ref[:, cs] = (_dot(dg, wv) * sc_ref[:, cs]).astype(bf16)

    return pl.pallas_call(
        body, grid=(T // tm,),
        in_specs=[pl.BlockSpec((tm, D), lambda i: (i, 0)), _prev_halo_spec(tm, D),
                  pl.BlockSpec((NDEV, C_GROUPS, C_GDIM // NDEV, C_GDIM), lambda i: (0, 0, 0, 0)),
                  pl.BlockSpec((1, D), lambda i: (0, 0))],
        out_specs=pl.BlockSpec((tm, D), lambda i: (i, 0)),
        out_shape=jax.ShapeDtypeStruct((T, D), bf16),
        scratch_shapes=[pltpu.VMEM((tm + HALO, D), f32)],
        compiler_params=_cparams(("parallel",)), name=name,
    )(p, p, wg, scale)


def c_mid_bwd(p, dy, wg, scale, w, x, g, tok, dres, *, tm, S, name):
    T = p.shape[0]
    nst = S // tm

    def body(p_ref, ph_ref, dy_ref, dyn_ref, wg_ref, sc_ref, w_ref, x_ref, g_ref, tok_ref, dres_ref,
             dp_ref, dsc_ref, dwg_ref, dx_ref, dgn_ref, ext, ext2):
        i = pl.program_id(0)
        first = (i % nst) == 0
        last = (i % nst) == nst - 1

        @pl.when(i == 0)
        def _():
            dgn_ref[...] = jnp.zeros(dgn_ref.shape, f32)
            dsc_ref[...] = jnp.zeros(dsc_ref.shape, f32)
            dwg_ref[...] = jnp.zeros(dwg_ref.shape, f32)

        ext[0:HALO, :] = jnp.where(first, 0.0, ph_ref[...].astype(f32))
        ext[HALO:, :] = p_ref[...].astype(f32)
        for g in range(C_GROUPS):
            w = POOL_WINDOWS[g]
            cs = slice(g * C_GDIM, (g + 1) * C_GDIM)
            dg = _pool_diff(p_ref, ext, g, i, nst, tm).astype(bf16)
            wv = wg_ref[:, g].reshape(C_GDIM, C_GDIM)
            dyo = dy_ref[:, cs].astype(f32)
            dsc_ref[:, cs] += jnp.sum(dyo * _dot(dg, wv), axis=0, keepdims=True)
            dyp = (dyo * sc_ref[:, cs]).astype(bf16)
            dypn = (dyn_ref[:, cs].astype(f32) * sc_ref[:, cs]).astype(bf16)
            dwg_ref[g] += _dot_tn(dg, dyp)
            dd = _dot_nt(dyp, wv)
            ddn = _dot_nt(dypn, wv)
            ext2[0:tm, cs] = dd / _pool_counts(i, nst, tm, tm, 0, w)
            ext2[tm:, cs] = jnp.where(last, 0.0, ddn / _pool_counts(i, nst, tm, HALO, tm, w))
            s = -dd
            for k in range(w):
                s = s + ext2[pl.ds(k, tm), cs]
            dp_ref[:, cs] = s.astype(bf16)
        _normbwd_tail(_dot_nt(dp_ref[...], w_ref[...]), x_ref, g_ref, tok_ref, dres_ref, dx_ref, dgn_ref)

    tail_in, tail_out = _normbwd_specs(tm, w)
    return pl.pallas_call(
        body, grid=(T // tm,),
        in_specs=[pl.BlockSpec((tm, D), lambda i: (i, 0)), _prev_halo_spec(tm, D),
                  pl.BlockSpec((tm, D), lambda i: (i, 0)), _next_halo_spec(tm, D, T),
                  pl.BlockSpec((NDEV, C_GROUPS, C_GDIM // NDEV, C_GDIM), lambda i: (0, 0, 0, 0)),
                  pl.BlockSpec((1, D), lambda i: (0, 0))] + tail_in,
        out_specs=[pl.BlockSpec((tm, D), lambda i: (i, 0)), pl.BlockSpec((1, D), lambda i: (0, 0)),
                   pl.BlockSpec((C_GROUPS, C_GDIM, C_GDIM), lambda i: (0, 0, 0))] + tail_out,
        out_shape=[jax.ShapeDtypeStruct((T, D), bf16), jax.ShapeDtypeStruct((1, D), f32),
                   jax.ShapeDtypeStruct((C_GROUPS, C_GDIM, C_GDIM), f32),
                   jax.ShapeDtypeStruct((T, D), f32), jax.ShapeDtypeStruct((1, D), f32)],
        scratch_shapes=[pltpu.VMEM((tm + HALO, D), f32), pltpu.VMEM((tm + HALO, D), f32)],
        compiler_params=_cparams(("arbitrary",)), name=name,
    )(p, p, dy, dy, wg, scale, w, x, g, tok, dres)


def final_loss(x, g, tgt, *, tm, name):
    T = x.shape[0]

    def body(x_ref, g_ref, t_ref, loss_ref, dx_ref, dg_ref):
        @pl.when(pl.program_id(0) == 0)
        def _():
            loss_ref[...] = jnp.zeros(loss_ref.shape, f32)
            dg_ref[...] = jnp.zeros(dg_ref.shape, f32)

        xv = x_ref[...]
        r = lax.rsqrt(jnp.mean(xv * xv, axis=-1, keepdims=True) + EPS)
        xhat = xv * r
        err = xhat * g_ref[...] - t_ref[...]
        loss_ref[...] += 0.5 * jnp.sum(jnp.mean(err * err, axis=-1, keepdims=True))
        dy = err * (1.0 / D)
        dg_ref[...] += jnp.sum(dy * xhat, axis=0, keepdims=True)
        dxhat = dy * g_ref[...]
        dx_ref[...] = r * (dxhat - xhat * jnp.mean(dxhat * xhat, axis=-1, keepdims=True))

    return pl.pallas_call(
        body, grid=(T // tm,),
        in_specs=[pl.BlockSpec((tm, D), lambda i: (i, 0)), pl.BlockSpec((1, D), lambda i: (0, 0)),
                  pl.BlockSpec((tm, D), lambda i: (i, 0))],
        out_specs=[pl.BlockSpec((8, 128), lambda i: (0, 0)), pl.BlockSpec((tm, D), lambda i: (i, 0)),
                   pl.BlockSpec((1, D), lambda i: (0, 0))],
        out_shape=[jax.ShapeDtypeStruct((8, 128), f32), jax.ShapeDtypeStruct((T, D), f32),
                   jax.ShapeDtypeStruct((1, D), f32)],
        compiler_params=_cparams(("arbitrary",)), name=name,
    )(x, g, tgt)


def _slot(px, py, pc):
    return 4 * px + 2 * py + pc


def _with_own_block(s, me):
    zone = lax.empty((NDEV,) + s.shape, s.dtype)
    return lax.dynamic_update_slice(zone, s[None], (me,) + (0,) * s.ndim)


def all_gather(arrs, me, *, name):
    n = len(arrs)

    def body(*refs):
        ins, outs = refs[:n], refs[2 * n:3 * n]
        send_sems, recv_sems = refs[3 * n:]
        x, y, c = lax.axis_index("x"), lax.axis_index("y"), lax.axis_index("c")
        me, sibling = (x, y, c), (x, y, 1 - c)
        chips = [(1 - x, y), (x, 1 - y), (1 - x, 1 - y)]

        def copy(a, k, block, to, src=None):
            dst = outs[a].at[_slot(*block)]
            return pltpu.make_async_remote_copy(
                src_ref=dst if src is None else src, dst_ref=dst,
                send_sem=send_sems.at[a, k], recv_sem=recv_sems.at[a, k], device_id=to, device_id_type=MESH)

        first = []
        for a in range(n):
            first.append(copy(a, 0, me, sibling, src=ins[a]))
            first += [copy(a, 1 + j, me, (*chip, c), src=ins[a]) for j, chip in enumerate(chips)]
        for cp in first:
            cp.start()
        passed = []
        for j, chip in enumerate(chips):
            for a in range(n):
                copy(a, 1 + j, (*chip, c), me).wait_recv()
                fwd = copy(a, 4 + j, (*chip, c), sibling)
                fwd.start()
                passed.append(fwd)
        for a in range(n):
            copy(a, 0, sibling, me).wait_recv()
        for j, chip in enumerate(chips):
            for a in range(n):
                copy(a, 4 + j, (*chip, 1 - c), me).wait_recv()
        for cp in first + passed:
            cp.wait_send()

    any_spec = pl.BlockSpec(memory_space=pl.ANY)
    return pl.pallas_call(
        body,
        in_specs=[any_spec] * (2 * n), out_specs=[any_spec] * n,
        out_shape=[jax.ShapeDtypeStruct((NDEV,) + a.shape, a.dtype) for a in arrs],
        input_output_aliases={n + i: i for i in range(n)},
        scratch_shapes=[pltpu.SemaphoreType.DMA((n, 7)), pltpu.SemaphoreType.DMA((n, 7))],
        compiler_params=pltpu.CompilerParams(has_side_effects=True), name=name,
    )(*arrs, *[_with_own_block(a, me) for a in arrs])


HBM_SPEC = pl.BlockSpec(memory_space=pltpu.HBM)
SEM_SPEC = pl.BlockSpec(memory_space=pltpu.SEMAPHORE)
ANY_SPEC = pl.BlockSpec(memory_space=pl.ANY)
TOKEN_SHAPE = jax.ShapeDtypeStruct((8, 128), f32)
DATAFLOW_EFFECT = pltpu.SideEffectType.DATAFLOW_SIDE_EFFECTING


def _in_hbm(a):
    return pltpu.with_memory_space_constraint(a, pltpu.HBM)


def _hbm_like(a):
    return pltpu.HBM(a.shape, a.dtype)


def _mesh_pos():
    return lax.axis_index("x"), lax.axis_index("y"), lax.axis_index("c")


def _gather_targets(x, y, c):
    return [(x, y, 1 - c), (1 - x, y, c), (x, 1 - y, c), (1 - x, 1 - y, c)]


def gather_start(shards, me, after, *, name):
    n = len(shards)
    extra = [] if after is None else [after]

    def body(*refs):
        srcs, lands = refs[:n], refs[n:2 * n]
        send_sems, recv_sems = refs[2 * n + len(extra)], refs[2 * n + len(extra) + 1]
        token = refs[-1]
        x, y, c = _mesh_pos()
        me = _slot(x, y, c)
        for a in range(n):
            for k, to in enumerate(_gather_targets(x, y, c)):
                pltpu.make_async_remote_copy(
                    src_ref=srcs[a], dst_ref=lands[a].at[me], send_sem=send_sems.at[4 * a + k], recv_sem=recv_sems.at[4 * a + k],
                    device_id=to, device_id_type=MESH).start()
        token[...] = jnp.zeros(token.shape, f32)

    lands = [_with_own_block(s, me) for s in shards]
    sems = pltpu.SemaphoreType.DMA((4 * n,))
    out = pl.pallas_call(
        body, name=name,
        in_specs=[HBM_SPEC] * (2 * n) + [ANY_SPEC] * len(extra),
        out_specs=[SEM_SPEC, SEM_SPEC] + [HBM_SPEC] * (2 * n) + [pl.BlockSpec(memory_space=pltpu.VMEM)],
        out_shape=[sems, sems] + [_hbm_like(s) for s in shards] + [_hbm_like(l) for l in lands] + [TOKEN_SHAPE],
        input_output_aliases={i: 2 + i for i in range(2 * n)},
        compiler_params=pltpu.CompilerParams(has_side_effects=DATAFLOW_EFFECT),
    )(*[_in_hbm(s) for s in shards], *[_in_hbm(l) for l in lands], *extra)
    return out[0], out[1], out[2:2 + n], out[2 + n:2 + 2 * n], out[-1]


def gather_wait(send_sems, recv_sems, shards, lands, after, *, name):
    n = len(shards)
    after = list(after) if isinstance(after, (list, tuple)) else [after]

    def body(*refs):
        srcs, lands_in = refs[:n], refs[n:2 * n]
        send_sems, recv_sems = refs[2 * n], refs[2 * n + 1]
        x, y, c = _mesh_pos()
        for a in range(n):
            for k, frm in enumerate(_gather_targets(x, y, c)):
                cp = pltpu.make_async_remote_copy(
                    src_ref=srcs[a], dst_ref=lands_in[a].at[_slot(*frm)], send_sem=send_sems.at[4 * a + k],
                    recv_sem=recv_sems.at[4 * a + k], device_id=frm, device_id_type=MESH)
                cp.wait_send()
                cp.wait_recv()

    out = pl.pallas_call(
        body, name=name,
        in_specs=[HBM_SPEC] * (2 * n) + [SEM_SPEC, SEM_SPEC] + [ANY_SPEC] * len(after),
        out_specs=[HBM_SPEC] * (2 * n),
        out_shape=[_hbm_like(s) for s in shards] + [_hbm_like(l) for l in lands],
        input_output_aliases={i: i for i in range(2 * n)},
        compiler_params=pltpu.CompilerParams(has_side_effects=DATAFLOW_EFFECT),
    )(*shards, *lands, send_sems, recv_sems, *after)
    return out[:n], out[n:]


def gather_finish(lands, *, name):
    n = len(lands)

    def body(*refs):
        lands_in, lands_out = refs[:n], refs[n:2 * n]
        send_sems, recv_sems = refs[2 * n:]
        x, y, c = _mesh_pos()
        sibling = (x, y, 1 - c)
        chips = [(1 - x, y), (x, 1 - y), (1 - x, 1 - y)]
        sent = []
        for a in range(n):
            for j, chip in enumerate(chips):
                s = _slot(*chip, c)
                cp = pltpu.make_async_remote_copy(
                    src_ref=lands_in[a].at[s], dst_ref=lands_out[a].at[s], send_sem=send_sems.at[a, j],
                    recv_sem=recv_sems.at[a, j], device_id=sibling, device_id_type=MESH)
                cp.start()
                sent.append(cp)
        for a in range(n):
            for j, chip in enumerate(chips):
                s = _slot(*chip, 1 - c)
                pltpu.make_async_remote_copy(
                    src_ref=lands_in[a].at[s], dst_ref=lands_out[a].at[s], send_sem=send_sems.at[a, j],
                    recv_sem=recv_sems.at[a, j], device_id=sibling, device_id_type=MESH).wait_recv()
        for cp in sent:
            cp.wait_send()

    return pl.pallas_call(
        body, name=name,
        in_specs=[ANY_SPEC] * n, out_specs=[ANY_SPEC] * n,
        out_shape=[jax.ShapeDtypeStruct(l.shape, l.dtype) for l in lands],
        input_output_aliases={i: i for i in range(n)},
        scratch_shapes=[pltpu.SemaphoreType.DMA((n, 3)), pltpu.SemaphoreType.DMA((n, 3))],
        compiler_params=pltpu.CompilerParams(has_side_effects=True),
    )(*lands)


def _peer(x, y, c, r):
    return (x ^ ((r >> 2) & 1), y ^ ((r >> 1) & 1), c ^ (r & 1))


def exchange_start(items, lands, after, *, name):
    n, m = len(items), len(lands)
    extra = [] if after is None else [after]

    def body(*refs):
        gs, zones = refs[:n], refs[n:n + m]
        send_sems, recv_sems = refs[n + m + len(extra)], refs[n + m + len(extra) + 1]
        token = refs[-1]
        x, y, c = _mesh_pos()
        me = _slot(x, y, c)
        for r in (1, 2, 4, 3, 5, 6, 7):
            to = _peer(x, y, c, r)
            for a, (_, zi, l) in enumerate(items):
                pltpu.make_async_remote_copy(
                    src_ref=gs[a].at[_slot(*to)], dst_ref=zones[zi].at[me, l], send_sem=send_sems.at[7 * a + r - 1],
                    recv_sem=recv_sems.at[7 * a + r - 1], device_id=to, device_id_type=MESH).start()
        token[...] = jnp.zeros(token.shape, f32)

    gs = [g for g, _, _ in items]
    sems = pltpu.SemaphoreType.DMA((7 * n,))
    out = pl.pallas_call(
        body, name=name,
        in_specs=[HBM_SPEC] * (n + m) + [ANY_SPEC] * len(extra),
        out_specs=[SEM_SPEC, SEM_SPEC] + [HBM_SPEC] * (n + m) + [pl.BlockSpec(memory_space=pltpu.VMEM)],
        out_shape=[sems, sems] + [_hbm_like(g) for g in gs] + [_hbm_like(z) for z in lands] + [TOKEN_SHAPE],
        input_output_aliases={i: 2 + i for i in range(n + m)},
        compiler_params=pltpu.CompilerParams(has_side_effects=DATAFLOW_EFFECT),
    )(*[_in_hbm(g) for g in gs], *[_in_hbm(z) for z in lands], *extra)
    return out[0], out[1], out[2:2 + n], out[2 + n:2 + n + m], out[-1]


def exchange_wait(parts, lands, after, *, name):
    m = len(lands)
    flat_gs = [g for _, _, _, gs in parts for g in gs]
    ng = len(flat_gs)
    after = list(after) if isinstance(after, (list, tuple)) else [after]

    def body(*refs):
        gs, zones = refs[:ng], refs[ng:ng + m]
        sem_refs = refs[ng + m:ng + m + 2 * len(parts)]
        x, y, c = _mesh_pos()
        base = 0
        for pi, (items, _, _, _) in enumerate(parts):
            send_sems, recv_sems = sem_refs[2 * pi], sem_refs[2 * pi + 1]
            for r in range(1, NDEV):
                frm = _peer(x, y, c, r)
                for a, (_, zi, l) in enumerate(items):
                    cp = pltpu.make_async_remote_copy(
                        src_ref=gs[base + a].at[_slot(*frm)], dst_ref=zones[zi].at[_slot(*frm), l],
                        send_sem=send_sems.at[7 * a + r - 1], recv_sem=recv_sems.at[7 * a + r - 1],
                        device_id=frm, device_id_type=MESH)
                    cp.wait_send()
                    cp.wait_recv()
            base += len(items)

    sem_args = [s for _, ss, rs, _ in parts for s in (ss, rs)]
    out = pl.pallas_call(
        body, name=name,
        in_specs=[HBM_SPEC] * (ng + m) + [SEM_SPEC] * len(sem_args) + [ANY_SPEC] * len(after),
        out_specs=[HBM_SPEC] * (ng + m),
        out_shape=[_hbm_like(g) for g in flat_gs] + [_hbm_like(z) for z in lands],
        input_output_aliases={i: i for i in range(ng + m)},
        compiler_params=pltpu.CompilerParams(has_side_effects=DATAFLOW_EFFECT),
    )(*flat_gs, *lands, *sem_args, *after)
    return out[ng:]


def sum8(g, *, name):
    _, R, C = g.shape
    tr = R

    def body(g_ref, o_ref):
        s = g_ref[0]
        for k in range(1, NDEV):
            s = s + g_ref[k]
        o_ref[...] = s

    return pl.pallas_call(
        body, grid=(R // tr,),
        in_specs=[pl.BlockSpec((NDEV, tr, C), lambda i: (0, i, 0))],
        out_specs=pl.BlockSpec((tr, C), lambda i: (i, 0)),
        out_shape=jax.ShapeDtypeStruct((R, C), f32),
        compiler_params=_cparams(("parallel",)), name=name,
    )(g)


def _adam_math(w, g, m, v):
    m = ADAM_B1 * m + (1.0 - ADAM_B1) * g
    v = ADAM_B2 * v + (1.0 - ADAM_B2) * (g * g)
    m_hat = m / (1.0 - ADAM_B1 ** ADAM_STEP)
    v_hat = v / (1.0 - ADAM_B2 ** ADAM_STEP)
    delta = -ADAM_LR * (m_hat / (jnp.sqrt(v_hat) + ADAM_EPS) + ADAM_WD * w)
    return delta, m, v


def adamw_slices(w, m, v, recv, *, row0=0, col0=0, rrows=None, tr=None, tc=None, l0=0, nl=None, prev=None, name):
    L, R, C = w.shape
    Rp, Cp = recv.shape[2], recv.shape[3]
    nl = L - l0 if nl is None else nl
    tr = R if tr is None else tr
    tc = C if tc is None else tc
    rrows = Rp if rrows is None else rrows
    assert R % tr == 0 and C % tc == 0 and row0 % rrows == 0 and R <= rrows
    assert (tr == R or (Rp == R and row0 == 0)) and (tc == C or (Cp == C and col0 == 0))
    rr = rrows if tr == R else tr
    rc = Cp if tc == C else tc
    rb = row0 // rrows if tr == R else 0
    prevs = [] if prev is None else list(prev)

    def body(w_ref, m_ref, v_ref, r_ref, *rest):
        g_ref, d_ref, nm_ref, nv_ref = rest[len(prevs):]
        c0 = col0 if tc == C else 0
        g = r_ref[0, 0:tr, c0:c0 + tc].astype(f32)
        for k in range(1, NDEV):
            g = g + r_ref[k, 0:tr, c0:c0 + tc].astype(f32)
        delta, nm, nv = _adam_math(w_ref[...], g, m_ref[...], v_ref[...])
        g_ref[...] = g
        d_ref[...] = delta
        nm_ref[...] = nm
        nv_ref[...] = nv

    wspec = pl.BlockSpec((None, tr, tc), lambda l, i, j: (l0 + l, i, j))
    return pl.pallas_call(
        body, grid=(nl, R // tr, C // tc),
        in_specs=[wspec, wspec, wspec, pl.BlockSpec((NDEV, None, rr, rc), lambda l, i, j: (0, l0 + l, rb + i, j))]
        + [pl.BlockSpec(memory_space=pl.ANY)] * len(prevs),
        out_specs=[wspec] * 4, out_shape=[jax.ShapeDtypeStruct((L, R, C), f32)] * 4,
        input_output_aliases={4 + q: q for q in range(len(prevs))},
        compiler_params=_cparams(("parallel", "parallel", "parallel")), name=name,
    )(w, m, v, recv, *prevs)


def adamw_slices_t(w, m, v, recv, *, name):
    L, R, C = w.shape

    def body(w_ref, m_ref, v_ref, r_ref, g_ref, d_ref, nm_ref, nv_ref):
        gt = r_ref[0].astype(f32)
        for k in range(1, NDEV):
            gt = gt + r_ref[k].astype(f32)
        g = gt.T
        delta, nm, nv = _adam_math(w_ref[...], g, m_ref[...], v_ref[...])
        g_ref[...] = g
        d_ref[...] = delta
        nm_ref[...] = nm
        nv_ref[...] = nv

    wspec = pl.BlockSpec((None, R, C), lambda l: (l, 0, 0))
    return pl.pallas_call(
        body, grid=(L,),
        in_specs=[wspec, wspec, wspec, pl.BlockSpec((NDEV, None, C, R), lambda l: (0, l, 0, 0))],
        out_specs=[wspec] * 4, out_shape=[jax.ShapeDtypeStruct((L, R, C), f32)] * 4,
        compiler_params=_cparams(("parallel",)), name=name,
    )(w, m, v, recv)


def adamw_plain(w, m, v, g, *, name):
    R, C = w.shape

    def body(w_ref, m_ref, v_ref, g_ref, d_ref, nm_ref, nv_ref):
        delta, nm, nv = _adam_math(w_ref[...], g_ref[...], m_ref[...], v_ref[...])
        d_ref[...] = delta
        nm_ref[...] = nm
        nv_ref[...] = nv

    return pl.pallas_call(
        body, out_shape=[jax.ShapeDtypeStruct((R, C), f32)] * 3, name=name,
    )(w, m, v, g)


def _chunk_causal_mask():
    pos = jnp.arange(GBLK)
    return (pos[None, :] // CHUNK) <= (pos[:, None] // CHUNK)


def kernel(x, norm_mix_g, norm_ffn_g, final_norm_g, a_w_in, a_v_norm_g, a_w_s, a_b_s, a_w_out, b_w_in, b_conv_w, b_w_out, c_w_in, c_w_grp, c_scale, c_w_out, f_w_gate, f_w_up, f_w_down, loss_target, m_norm_mix_g, m_norm_ffn_g, m_final_norm_g, m_a_w_in, m_a_v_norm_g, m_a_w_s, m_a_b_s, m_a_w_out, m_b_w_in, m_b_conv_w, m_b_w_out, m_c_w_in, m_c_w_grp, m_c_scale, m_c_w_out, m_f_w_gate, m_f_w_up, m_f_w_down, v_norm_mix_g, v_norm_ffn_g, v_final_norm_g, v_a_w_in, v_a_v_norm_g, v_a_w_s, v_a_b_s, v_a_w_out, v_b_w_in, v_b_conv_w, v_b_w_out, v_c_w_in, v_c_w_grp, v_c_scale, v_c_w_out, v_f_w_gate, v_f_w_up, v_f_w_down):
    B, S, _ = x.shape
    T = B * S
    tm = min(512, S)
    tms = min(512, T)
    tma = min(256, S)
    tmf = min(256, T)
    me = _slot(lax.axis_index("x"), lax.axis_index("y"), lax.axis_index("c"))

    def tb(w):
        return jnp.transpose(w, (0, 2, 1)).astype(bf16)

    small = jnp.concatenate([a_v_norm_g, b_conv_w[0], c_scale, jnp.zeros((2, GBLK), f32)], axis=0)
    started = [gather_start([tb(a_w_in[0:1])[0], a_w_out[0].astype(bf16), small], me, None, name="gather_start_0")]
    tok = started[0][4]
    z = tok[0, 0]
    f_g, f_u, f_d = tb(f_w_gate + z), tb(f_w_up + z), (f_w_down + z).astype(bf16)
    stage_shards = [
        None,
        [f_g[0], f_u[0], f_d[0]],
        [tb(b_w_in + z)[0], (b_w_out[0] + z).astype(bf16), f_g[1], f_u[1], f_d[1]],
        [(c_w_in[0] + z).astype(bf16), (c_w_grp[0] + z).astype(bf16), (c_w_out[0] + z).astype(bf16), f_g[2], f_u[2], f_d[2]],
        [tb(a_w_in[1:2] + z)[0], (a_w_out[1] + z).astype(bf16), f_g[3], f_u[3], f_d[3]],
    ]
    for s in range(1, len(stage_shards)):
        started.append(gather_start(stage_shards[s], me, tok, name=f"gather_start_{s}"))
        tok = started[-1][4]

    def stage_weights(s, after):
        send_sems, recv_sems, shards, lands, _ = started[s]
        _, lands = gather_wait(send_sems, recv_sems, shards, lands, after, name=f"gather_wait_{s}")
        lands = gather_finish(lands, name=f"gather_finish_{s}")
        return [t.reshape(-1, t.shape[-1]) if t.ndim == 3 else t for t in lands]

    wm_masked = jnp.where(_chunk_causal_mask()[None, None], a_w_s, 0.0).astype(bf16)
    bsb = jnp.broadcast_to(a_b_s[:, :, :, None], a_b_s.shape + (GBLK,))

    xc = x.reshape(T, D)
    saved, weights = [], []
    for i in range(DEPTH):
        kind, j = i % 3, i // 3
        gmix = norm_mix_g[i][None]
        if i == 0:
            w_in, w_out, small_g = stage_weights(0, tok)
            small_full = jnp.transpose(small_g.reshape(NDEV, 8, GBLK), (1, 0, 2)).reshape(8, D)
            gv_full, cw_full, cs_full = small_full[0:2], small_full[2:5], small_full[5:6]
            w = [w_in, w_out]
        else:
            w = stage_weights(i + 1, xc)
        if kind == 0:
            w_in = w[0]
            p, h, y = a_fwd(xc, gmix, w_in, gv_full[j][None], wm_masked[j], bsb[j], tm=tma, name=f"a_fwd_{i}")
        elif kind == 1:
            w_in, w_out = w[:2]
            p, h = norm_mm(xc, gmix, w_in, tm=tms, transposed=True, name=f"b_in_{i}")
            y = b_mid_fwd(p, cw_full, tm=tm, S=S, name=f"b_mid_{i}")
        else:
            w_in, w_grp, w_out = w[:3]
            p, h = norm_mm(xc, gmix, w_in, tm=tms, transposed=False, name=f"c_in_{i}")
            y = c_mid_fwd(p, w_grp, cs_full, tm=tm, S=S, name=f"c_mid_{i}")
        if i == 0:
            w = w + stage_weights(1, y)
        weights.append(w)
        w_out, w_fg, w_fu, w_fd = w[-4:]
        x1, x2, pg, pu, a, h2 = ffn_fwd(y, w_out, xc, norm_ffn_g[i][None], w_fg, w_fu, w_fd, tm=tmf, name=f"f_fwd_{i}")
        saved.append((xc, p, h, y, x1, pg, pu, h2, a))
        xc = x2

    loss_blk, dx, d_final = final_loss(xc, final_norm_g[None], loss_target.reshape(T, D), tm=tms, name="final_loss")
    loss = lax.psum(loss_blk[0, 0], AXES)

    zone_shapes = [(2, 256, D), (2, GBLK, D), (1, 384, D), (1, GBLK, D), (1, GBLK, D), (1, GBLK, C_GDIM), (1, GBLK, D),
                   (DEPTH, F_LOC, D), (DEPTH, F_LOC, D), (DEPTH, F_LOC, D)]
    Z_AIN, Z_AOUT, Z_BIN, Z_BOUT, Z_CIN, Z_CGRP, Z_COUT, Z_FG, Z_FU, Z_FD = range(10)
    lands = [lax.empty((NDEV,) + s, bf16) for s in zone_shapes]
    parts = []

    def exchange(items, after, name):
        nonlocal lands
        for g, zi, l in items:
            own = lax.dynamic_index_in_dim(g, me, axis=0, keepdims=True)[None]
            lands[zi] = lax.dynamic_update_slice(lands[zi], own, (me, l) + (0,) * (g.ndim - 1))
        idx = sorted({zi for _, zi, _ in items})
        local = [(g, idx.index(zi), l) for g, zi, l in items]
        send_sems, recv_sems, gs, zs, token = exchange_start(local, [lands[zi] for zi in idx], after, name=name)
        for q, zi in enumerate(idx):
            lands[zi] = zs[q]
        parts.append(([(None, zi, l) for _, zi, l in items], send_sems, recv_sems, gs))
        return token

    d_mix, d_ffn = [None] * DEPTH, [None] * DEPTH
    d_gv, d_wm, d_bs = [None] * 2, [None] * 2, [None] * 2
    tok = None
    for i in reversed(range(DEPTH)):
        kind, j = i % 3, i // 3
        xin, p, h, y, x1, pg, pu, h2, a = saved[i]
        w = weights[i]
        w_in, w_out = w[0], w[-4]
        w_fg, w_fu, w_fd = w[-3:]
        dpg, dpu, dxb, dx1, d_ffn[i], dy, dx1b = ffn_bwd(dx, w_fd, pg, pu, w_fg, w_fu, x1, norm_ffn_g[i][None], tok, w_out,
                                                         tm=tmf, name=f"f_bwd_{i}")
        gw_g, gw_u, gw_d = ffn_dw(dpg, dpu, a, h2, dxb, name=f"f_dw_{i}")
        tok = exchange([(gw_g, Z_FG, i), (gw_u, Z_FU, i), (gw_d, Z_FD, i)], tok, f"exchange_start_ffn_{i}")
        gmix = norm_mix_g[i][None]
        if kind == 0:
            dp, d_gv[j], d_wm[j], d_bs[j], dx, d_mix[i] = a_mid_bwd(
                p, dy, gv_full[j][None], wm_masked[j], bsb[j], w_in, xin, gmix, tok, dx1, tm=tma, name=f"a_bwd_{i}")
            items = [(mm_tn(dp, h, name=f"a_dw_in_{i}"), Z_AIN, j),
                     (mm_tn(y, dx1b, name=f"a_dw_out_{i}"), Z_AOUT, j)]
        elif kind == 1:
            dp, d_cw, dx, d_mix[i] = b_mid_bwd(p, dy, cw_full, w_in, xin, gmix, tok, dx1, tm=tm, S=S, name=f"b_bwd_{i}")
            items = [(mm_tn(dp, h, name=f"b_dw_in_{i}"), Z_BIN, 0),
                     (mm_tn(y, dx1b, name=f"b_dw_out_{i}"), Z_BOUT, 0)]
        else:
            dp, d_cs, d_wgrp, dx, d_mix[i] = c_mid_bwd(p, dy, w[1], cs_full, w_in, xin, gmix, tok, dx1, tm=tm, S=S,
                                                       name=f"c_bwd_{i}")
            gw_cgrp = jnp.transpose(d_wgrp.reshape(C_GROUPS, NDEV, C_GDIM // NDEV, C_GDIM), (1, 0, 2, 3)).astype(bf16)
            items = [(mm_tn(h, dp, name=f"c_dw_in_{i}"), Z_CIN, 0),
                     (gw_cgrp.reshape(NDEV, GBLK, C_GDIM), Z_CGRP, 0),
                     (mm_tn(y, dx1b, name=f"c_dw_out_{i}"), Z_COUT, 0)]
        tok = exchange(items, tok, f"exchange_start_mix_{i}")
    grad_x = dx.reshape(B, S, D)

    small_parts = [
        jnp.concatenate(d_mix, axis=0).reshape(-1, GBLK),
        jnp.concatenate(d_ffn, axis=0).reshape(-1, GBLK),
        d_final.reshape(-1, GBLK),
        jnp.concatenate(d_gv, axis=0).reshape(-1, GBLK),
        jnp.where(_chunk_causal_mask()[None, None], jnp.stack(d_wm), 0.0).reshape(-1, GBLK),
        jnp.stack(d_bs).reshape(-1, GBLK),
        d_cw.reshape(-1, GBLK),
        d_cs.reshape(-1, GBLK),
    ]
    small_rows = [q.shape[0] for q in small_parts]
    sg_send, sg_recv, sg_shards, sg_lands, _ = gather_start(
        [jnp.concatenate(small_parts, axis=0)], me, tok, name="gather_small_start")

    def plain(w, m, v, g, name):
        shp = w.shape
        w2, m2, v2, g2 = (t.reshape(-1, shp[-1]) for t in (w, m, v, g))
        return tuple(t.reshape(shp) for t in adamw_plain(w2, m2, v2, g2, name=name))

    def sliced(w, m, v, recv, name, **kw):
        shp = w.shape
        w3, m3, v3 = (t.reshape((shp[0], -1, shp[-1])) for t in (w, m, v))
        return tuple(t.reshape(shp) for t in adamw_slices(w3, m3, v3, recv, name=name, **kw))

    def sliced_t(w, m, v, recv, name, **kw):
        wt, mt, vt = (jnp.transpose(t, (0, 2, 1)) for t in (w, m, v))
        return adamw_slices(wt, mt, vt, recv, name=name, **kw)

    def untransposed(outs):
        return tuple(jnp.transpose(t, (0, 2, 1)) for t in outs)

    lands = list(exchange_wait(parts[:6], lands, dx, name="exchange_wait"))
    _, _, r_bin, r_bout, r_cin, r_cgrp, r_cout, r_fg, r_fu, r_fd = lands
    res = {}
    res["b_w_in"] = adamw_slices_t(b_w_in, m_b_w_in, v_b_w_in, r_bin, name="adam_b_w_in")
    res["b_w_out"] = sliced(b_w_out, m_b_w_out, v_b_w_out, r_bout, "adam_b_w_out")
    res["c_w_in"] = sliced(c_w_in, m_c_w_in, v_c_w_in, r_cin, "adam_c_w_in")
    res["c_w_grp"] = sliced(c_w_grp, m_c_w_grp, v_c_w_grp, r_cgrp, "adam_c_w_grp")
    res["c_w_out"] = sliced(c_w_out, m_c_w_out, v_c_w_out, r_cout, "adam_c_w_out")
    fg = sliced_t(f_w_gate, m_f_w_gate, v_f_w_gate, r_fg, "adam_f_gate_123", tc=256, l0=1)
    fu = sliced_t(f_w_up, m_f_w_up, v_f_w_up, r_fu, "adam_f_up_123", tc=256, l0=1)
    fd = adamw_slices(f_w_down, m_f_w_down, v_f_w_down, r_fd, tc=256, l0=1, name="adam_f_down_123")
    big = [res[n][1] for n in ("b_w_in", "b_w_out", "c_w_in", "c_w_grp", "c_w_out")] + [fg[1], fu[1], fd[1]]

    ffn0_items, ffn0_send, ffn0_recv, ffn0_gs = parts[6]
    ffn0_zones = [Z_FG, Z_FU, Z_FD]
    ffn0_part = ([(None, ffn0_zones.index(zi), l) for _, zi, l in ffn0_items], ffn0_send, ffn0_recv, ffn0_gs)
    r_fg, r_fu, r_fd = exchange_wait([ffn0_part], [lands[zi] for zi in ffn0_zones], big, name="exchange_wait_ffn0")
    fg = sliced_t(f_w_gate, m_f_w_gate, v_f_w_gate, r_fg, "adam_f_gate_0", tc=256, l0=0, nl=1, prev=fg)
    fu = sliced_t(f_w_up, m_f_w_up, v_f_w_up, r_fu, "adam_f_up_0", tc=256, l0=0, nl=1, prev=fu)
    fd = adamw_slices(f_w_down, m_f_w_down, v_f_w_down, r_fd, tc=256, l0=0, nl=1, prev=fd, name="adam_f_down_0")
    res["f_w_gate"], res["f_w_up"], res["f_w_down"] = untransposed(fg), untransposed(fu), tuple(fd)

    last_items, last_send, last_recv, last_gs = parts[7]
    last_zones = [Z_AIN, Z_AOUT]
    last_part = ([(None, last_zones.index(zi), l) for _, zi, l in last_items], last_send, last_recv, last_gs)
    r_ain, r_aout = exchange_wait([last_part], [lands[zi] for zi in last_zones], [fg[1], fu[1], fd[1]], name="exchange_wait_last")
    res["a_w_in"] = adamw_slices_t(a_w_in, m_a_w_in, v_a_w_in, r_ain, name="adam_a_w_in")
    res["a_w_out"] = sliced(a_w_out, m_a_w_out, v_a_w_out, r_aout, "adam_a_w_out")

    _, sg_lands = gather_wait(sg_send, sg_recv, sg_shards, sg_lands, [res["a_w_in"][1], res["a_w_out"][1]], name="gather_small_wait")
    (gs_all,) = gather_finish(sg_lands, name="gather_small_finish")
    gs = sum8(gs_all, name="sum_small_grads")
    offs = [0]
    for r in small_rows:
        offs.append(offs[-1] + r)
    sp = [gs[offs[q]:offs[q + 1]] for q in range(len(small_parts))]
    grad_norm_mix_g = sp[0].reshape(DEPTH, D)
    grad_norm_ffn_g = sp[1].reshape(DEPTH, D)
    grad_final_norm_g = sp[2].reshape(D)
    grad_a_w_s = sp[4].reshape(a_w_s.shape)
    grad_a_b_s = sp[5].reshape(a_b_s.shape)

    def my_cols(full):
        return lax.dynamic_slice_in_dim(full, me * GBLK, GBLK, axis=1)

    grad_a_v_norm_g = my_cols(sp[3].reshape(2, D))
    grad_b_conv_w = my_cols(sp[6].reshape(3, D))[None]
    grad_c_scale = my_cols(sp[7].reshape(1, D))
    res["norm_mix_g"] = (grad_norm_mix_g,) + plain(norm_mix_g, m_norm_mix_g, v_norm_mix_g, grad_norm_mix_g, "adam_norm_mix")
    res["norm_ffn_g"] = (grad_norm_ffn_g,) + plain(norm_ffn_g, m_norm_ffn_g, v_norm_ffn_g, grad_norm_ffn_g, "adam_norm_ffn")
    res["final_norm_g"] = (grad_final_norm_g,) + tuple(
        t.reshape(D) for t in plain(final_norm_g[None], m_final_norm_g[None], v_final_norm_g[None], grad_final_norm_g[None], "adam_final"))
    res["a_v_norm_g"] = (grad_a_v_norm_g,) + plain(a_v_norm_g, m_a_v_norm_g, v_a_v_norm_g, grad_a_v_norm_g, "adam_a_v_norm")
    res["a_w_s"] = (grad_a_w_s,) + plain(a_w_s, m_a_w_s, v_a_w_s, grad_a_w_s, "adam_a_w_s")
    res["a_b_s"] = (grad_a_b_s,) + plain(a_b_s, m_a_b_s, v_a_b_s, grad_a_b_s, "adam_a_b_s")
    res["b_conv_w"] = (grad_b_conv_w,) + plain(b_conv_w, m_b_conv_w, v_b_conv_w, grad_b_conv_w, "adam_b_conv")
    res["c_scale"] = (grad_c_scale,) + plain(c_scale, m_c_scale, v_c_scale, grad_c_scale, "adam_c_scale")

    order = ["norm_mix_g", "norm_ffn_g", "final_norm_g", "a_w_in", "a_v_norm_g", "a_w_s", "a_b_s", "a_w_out", "b_w_in",
             "b_conv_w", "b_w_out", "c_w_in", "c_w_grp", "c_scale", "c_w_out", "f_w_gate", "f_w_up", "f_w_down"]
    return (loss, grad_x, *[res[n][0] for n in order], *[res[n][1] for n in order],
            *[res[n][2] for n in order], *[res[n][3] for n in order])
```

```python
import jax
import jax.numpy as jnp
from jax import lax
from jax.experimental import pallas as pl
from jax.experimental.pallas import tpu as pltpu

f32 = jnp.float32
bf16 = jnp.bfloat16

NDEV = 8
D = 1024
EPS = 1e-6
CHUNK = 64
GBLK = 128
A_GROUPS = 8
C_GROUPS = 4
C_GDIM = 256
POOL_WINDOWS = (2, 4, 8, 16)
HALO = 16
COLS = 256
F_LOC = 352
MXU_COLS = 256
DEPTH = 4
AXES = ("x", "y", "c")
MESH = pl.DeviceIdType.MESH

ADAM_LR = 0.001
ADAM_B1 = 0.9
ADAM_B2 = 0.999
ADAM_EPS = 1e-08
ADAM_WD = 0.01
ADAM_STEP = 10

VMEM_LIMIT = 56 * 1024 * 1024


def _cparams(sem):
    return pltpu.CompilerParams(dimension_semantics=sem, vmem_limit_bytes=VMEM_LIMIT)


def _gelu(z):
    return 0.5 * z * (1.0 + lax.erf(z * 0.7071067811865476))


def _gelu_cdf(z):
    return 0.5 * (1.0 + lax.erf(z * 0.7071067811865476))


def _gelu_grad(z, cdf):
    return cdf + z * jnp.exp(-0.5 * z * z) * 0.3989422804014327


def _dot(a, b):
    return jnp.dot(a, b, preferred_element_type=f32)


def _dot_nt(a, b):
    return lax.dot_general(a, b, (((1,), (1,)), ((), ())), preferred_element_type=f32)


def _dot_tn(a, b):
    return lax.dot_general(a, b, (((0,), (0,)), ((), ())), preferred_element_type=f32)


def _col_chunks(n, width=1024):
    return [(c, min(c + width, n)) for c in range(0, n, width)]


def norm_mm(x, g, w, *, tm, transposed, name):
    T = x.shape[0]
    n = w.shape[0] if transposed else w.shape[1]

    def body(x_ref, g_ref, w_ref, p_ref, h_ref):
        xv = x_ref[...]
        r = lax.rsqrt(jnp.mean(xv * xv, axis=-1, keepdims=True) + EPS)
        h = (xv * r * g_ref[...]).astype(bf16)
        h_ref[...] = h
        p = _dot_nt(h, w_ref[...]) if transposed else _dot(h, w_ref[...])
        p_ref[...] = p.astype(bf16)

    return pl.pallas_call(
        body, grid=(T // tm,),
        in_specs=[pl.BlockSpec((tm, D), lambda i: (i, 0)), pl.BlockSpec((1, D), lambda i: (0, 0)),
                  pl.BlockSpec(w.shape, lambda i: (0, 0), pipeline_mode=pl.Buffered(1))],
        out_specs=[pl.BlockSpec((tm, n), lambda i: (i, 0)), pl.BlockSpec((tm, D), lambda i: (i, 0))],
        out_shape=[jax.ShapeDtypeStruct((T, n), bf16), jax.ShapeDtypeStruct((T, D), bf16)],
        compiler_params=_cparams(("parallel",)), name=name,
    )(x, g, w)


def _stream_weights(first, hbm_refs, vmem_refs, sems, plan):
    copies = [pltpu.make_async_copy(hbm_refs[w].at[pl.ds(r0, r1 - r0)], vmem_refs[w].at[pl.ds(r0, r1 - r0)], sems.at[n])
              for n, (w, r0, r1) in enumerate(plan)]

    @pl.when(first)
    def _():
        for cp in copies:
            cp.start()

    def ready(n):
        @pl.when(first)
        def _():
            copies[n].wait()

    return ready


def ffn_fwd(y, w_out, xin, g, wtg, wtu, wd, *, tm, name):
    T, K = y.shape
    H = wtg.shape[0]
    chunks = _col_chunks(H)
    plan = [(0, 0, K)] + [(w, c0, c1) for c0, c1 in chunks for w in (1, 2, 3)]

    def body(y_ref, xin_ref, g_ref, wo_hbm, wg_hbm, wu_hbm, wd_hbm, x_ref, x2_ref, pg_ref, pu_ref, a_ref, h_ref,
             wo_ref, wg_ref, wu_ref, wd_ref, sems):
        ready = _stream_weights(pl.program_id(0) == 0, [wo_hbm, wg_hbm, wu_hbm, wd_hbm],
                                [wo_ref, wg_ref, wu_ref, wd_ref], sems, plan)
        ready(0)
        xv = xin_ref[...] + _dot(y_ref[...], wo_ref[...])
        x_ref[...] = xv
        r = lax.rsqrt(jnp.mean(xv * xv, axis=-1, keepdims=True) + EPS)
        h = (xv * r * g_ref[...]).astype(bf16)
        h_ref[...] = h
        out = None
        for q, (c0, c1) in enumerate(chunks):
            ready(1 + 3 * q)
            gate = _dot_nt(h, wg_ref[c0:c1, :])
            ready(2 + 3 * q)
            up = _dot_nt(h, wu_ref[c0:c1, :])
            pg_ref[:, c0:c1] = gate.astype(bf16)
            pu_ref[:, c0:c1] = up.astype(bf16)
            ac = (gate * jax.nn.sigmoid(gate) * up).astype(bf16)
            a_ref[:, c0:c1] = ac
            ready(3 + 3 * q)
            part = _dot(ac, wd_ref[c0:c1, :])
            out = part if out is None else out + part
        x2_ref[...] = xv + out

    hspec = pl.BlockSpec((tm, H), lambda i: (i, 0))
    dspec = pl.BlockSpec((tm, D), lambda i: (i, 0))
    return pl.pallas_call(
        body, grid=(T // tm,),
        in_specs=[pl.BlockSpec((tm, K), lambda i: (i, 0)), dspec, pl.BlockSpec((1, D), lambda i: (0, 0))]
        + [pl.BlockSpec(memory_space=pl.ANY)] * 4,
        out_specs=[dspec, dspec, hspec, hspec, hspec, dspec],
        out_shape=[jax.ShapeDtypeStruct((T, D), f32)] * 2 + [jax.ShapeDtypeStruct((T, H), bf16)] * 3
        + [jax.ShapeDtypeStruct((T, D), bf16)],
        scratch_shapes=[pltpu.VMEM((K, D), w_out.dtype)] + [pltpu.VMEM((H, D), wd.dtype)] * 3
        + [pltpu.SemaphoreType.DMA((len(plan),))],
        compiler_params=_cparams(("arbitrary",)), name=name,
    )(y, xin, g, w_out, wtg, wtu, wd)


def ffn_bwd(dx, wd, pg, pu, wtg, wtu, x, g, tok, w_next, *, tm, name):
    T = dx.shape[0]
    H = wd.shape[0]
    kn = w_next.shape[0]
    toks = [] if tok is None else [tok]
    chunks = _col_chunks(H)
    plan = [(w, c0, c1) for c0, c1 in chunks for w in (0, 1, 2)] + [(3, 0, kn)]

    def body(dx_ref, pg_ref, pu_ref, x_ref, g_ref, *rest):
        wd_hbm, wg_hbm, wu_hbm, wn_hbm = rest[len(toks):len(toks) + 4]
        dg_ref, du_ref, dxb_ref, dx1_ref, dgn_ref, dy_ref, dx1b_ref = rest[len(toks) + 4:len(toks) + 11]
        wd_ref, wg_ref, wu_ref, wn_ref, sems = rest[len(toks) + 11:]
        ready = _stream_weights(pl.program_id(0) == 0, [wd_hbm, wg_hbm, wu_hbm, wn_hbm],
                                [wd_ref, wg_ref, wu_ref, wn_ref], sems, plan)

        @pl.when(pl.program_id(0) == 0)
        def _():
            dgn_ref[...] = jnp.zeros(dgn_ref.shape, f32)

        dxv = dx_ref[...]
        dxb = dxv.astype(bf16)
        dxb_ref[...] = dxb
        dh = None
        for q, (c0, c1) in enumerate(chunks):
            ready(3 * q)
            da = _dot_nt(dxb, wd_ref[c0:c1, :])
            gate = pg_ref[:, c0:c1].astype(f32)
            up = pu_ref[:, c0:c1].astype(f32)
            sg = jax.nn.sigmoid(gate)
            dgc = (da * up * (sg * (1.0 + gate * (1.0 - sg)))).astype(bf16)
            duc = (da * gate * sg).astype(bf16)
            dg_ref[:, c0:c1] = dgc
            du_ref[:, c0:c1] = duc
            ready(3 * q + 1)
            ready(3 * q + 2)
            part = _dot(dgc, wg_ref[c0:c1, :]) + _dot(duc, wu_ref[c0:c1, :])
            dh = part if dh is None else dh + part
        gv = g_ref[...] + rest[0][0:1, 0:1] if toks else g_ref[...]
        xv = x_ref[...]
        r = lax.rsqrt(jnp.mean(xv * xv, axis=-1, keepdims=True) + EPS)
        xhat = xv * r
        dgn_ref[...] += jnp.sum(dh * xhat, axis=0, keepdims=True)
        dxhat = dh * gv
        dx1 = dxv + r * (dxhat - xhat * jnp.mean(dxhat * xhat, axis=-1, keepdims=True))
        dx1_ref[...] = dx1
        dx1b = dx1.astype(bf16)
        dx1b_ref[...] = dx1b
        ready(len(plan) - 1)
        dy_ref[...] = _dot_nt(dx1b, wn_ref[...]).astype(bf16)

    dspec = pl.BlockSpec((tm, D), lambda i: (i, 0))
    hspec = pl.BlockSpec((tm, H), lambda i: (i, 0))
    gspec = pl.BlockSpec((1, D), lambda i: (0, 0))
    return pl.pallas_call(
        body, grid=(T // tm,),
        in_specs=[dspec, hspec, hspec, dspec, gspec] + [pl.BlockSpec((8, 128), lambda i: (0, 0))] * len(toks)
        + [pl.BlockSpec(memory_space=pl.ANY)] * 4,
        out_specs=[hspec, hspec, dspec, dspec, gspec, pl.BlockSpec((tm, kn), lambda i: (i, 0)), dspec],
        out_shape=[jax.ShapeDtypeStruct((T, H), bf16)] * 2 + [jax.ShapeDtypeStruct((T, D), bf16),
                   jax.ShapeDtypeStruct((T, D), f32), jax.ShapeDtypeStruct((1, D), f32),
                   jax.ShapeDtypeStruct((T, kn), bf16), jax.ShapeDtypeStruct((T, D), bf16)],
        scratch_shapes=[pltpu.VMEM((H, D), wd.dtype)] * 3 + [pltpu.VMEM((kn, D), w_next.dtype),
                        pltpu.SemaphoreType.DMA((len(plan),))],
        compiler_params=_cparams(("arbitrary",)), name=name,
    )(dx, pg, pu, x, g, *toks, wd, wtg, wtu, w_next)


def mm_tn(a, b, *, name):
    T, K = a.shape
    n = b.shape[1]

    def body(a_ref, b_ref, o_ref):
        o_ref[...] = _dot_tn(a_ref[...], b_ref[...]).astype(bf16)

    out = pl.pallas_call(
        body, grid=(K // MXU_COLS,),
        in_specs=[pl.BlockSpec((T, MXU_COLS), lambda d: (0, d)),
                  pl.BlockSpec((T, n), lambda d: (0, 0), pipeline_mode=pl.Buffered(1))],
        out_specs=pl.BlockSpec((MXU_COLS, n), lambda d: (d, 0)),
        out_shape=jax.ShapeDtypeStruct((K, n), bf16),
        compiler_params=_cparams(("parallel",)), name=name,
    )(a, b)
    return out.reshape(NDEV, K // NDEV, n)


def ffn_dw(dpg, dpu, a, h2, dxb, *, name):
    T, H = dpg.shape

    def body(g_ref, u_ref, a_ref, h_ref, x_ref, og_ref, ou_ref, od_ref):
        h = h_ref[...]
        og_ref[...] = _dot_tn(g_ref[...], h).astype(bf16)
        ou_ref[...] = _dot_tn(u_ref[...], h).astype(bf16)
        od_ref[...] = _dot_tn(a_ref[...], x_ref[...]).astype(bf16)

    blk = pl.BlockSpec((T, MXU_COLS), lambda d: (0, d))
    full = pl.BlockSpec((T, D), lambda d: (0, 0), pipeline_mode=pl.Buffered(1))
    out = pl.BlockSpec((MXU_COLS, D), lambda d: (d, 0))
    outs = pl.pallas_call(
        body, grid=(H // MXU_COLS,),
        in_specs=[blk, blk, blk, full, full], out_specs=[out] * 3,
        out_shape=[jax.ShapeDtypeStruct((H, D), bf16)] * 3,
        compiler_params=_cparams(("parallel",)), name=name,
    )(dpg, dpu, a, h2, dxb)
    return [o.reshape(NDEV, H // NDEV, D) for o in outs]


def _normbwd_tail(dh, x_ref, g_ref, tok_ref, dres_ref, dx_ref, dgn_ref):
    gv = g_ref[...] + tok_ref[0:1, 0:1]
    xv = x_ref[...]
    r = lax.rsqrt(jnp.mean(xv * xv, axis=-1, keepdims=True) + EPS)
    xhat = xv * r
    dgn_ref[...] += jnp.sum(dh * xhat, axis=0, keepdims=True)
    dxhat = dh * gv
    dx_ref[...] = dres_ref[...] + r * (dxhat - xhat * jnp.mean(dxhat * xhat, axis=-1, keepdims=True))


def _normbwd_specs(tm, w):
    dspec = pl.BlockSpec((tm, D), lambda i: (i, 0))
    gspec = pl.BlockSpec((1, D), lambda i: (0, 0))
    ins = [pl.BlockSpec(w.shape, lambda i: (0, 0), pipeline_mode=pl.Buffered(1)), dspec, gspec,
           pl.BlockSpec((8, 128), lambda i: (0, 0)), dspec]
    return ins, [dspec, gspec]


def _prev_halo_spec(tm, ncol):
    return pl.BlockSpec((HALO, ncol), lambda i: (jnp.maximum(i * (tm // HALO) - 1, 0), 0))


def _next_halo_spec(tm, ncol, T):
    return pl.BlockSpec((HALO, ncol), lambda i: (jnp.minimum((i + 1) * (tm // HALO), T // HALO - 1), 0))


def a_fwd(x, g, wt, gv, wm, bsb, *, tm, name):
    T = x.shape[0]

    def body(x_ref, g_ref, wt_ref, gv_ref, wm_ref, bs_ref, p_ref, h_ref, y_ref, vn_sc):
        xv = x_ref[...]
        r = lax.rsqrt(jnp.mean(xv * xv, axis=-1, keepdims=True) + EPS)
        h = (xv * r * g_ref[...]).astype(bf16)
        h_ref[...] = h
        p_ref[...] = _dot_nt(h, wt_ref[...]).astype(bf16)
        v = _gelu(p_ref[:, D:].astype(f32))
        vc = v - jnp.mean(v, axis=-1, keepdims=True)
        var = jnp.mean(vc * vc, axis=-1, keepdims=True)
        vn_sc[...] = (vc * lax.rsqrt(var + EPS) * gv_ref[...]).astype(bf16)
        for g in range(A_GROUPS):
            cs = slice(g * GBLK, (g + 1) * GBLK)
            for r in range(tm // GBLK):
                rs = slice(r * GBLK, (r + 1) * GBLK)
                sv = _dot(wm_ref[g], vn_sc[rs, cs]) + bs_ref[g]
                y_ref[rs, cs] = (_gelu(p_ref[rs, cs].astype(f32)) * sv).astype(bf16)

    dspec = pl.BlockSpec((tm, D), lambda i: (i, 0))
    gspec = pl.BlockSpec((1, D), lambda i: (0, 0))
    return pl.pallas_call(
        body, grid=(T // tm,),
        in_specs=[dspec, gspec, pl.BlockSpec(wt.shape, lambda i: (0, 0), pipeline_mode=pl.Buffered(1)), gspec,
                  pl.BlockSpec((A_GROUPS, GBLK, GBLK), lambda i: (0, 0, 0)),
                  pl.BlockSpec((A_GROUPS, GBLK, GBLK), lambda i: (0, 0, 0))],
        out_specs=[pl.BlockSpec((tm, 2 * D), lambda i: (i, 0)), dspec, dspec],
        out_shape=[jax.ShapeDtypeStruct((T, 2 * D), bf16), jax.ShapeDtypeStruct((T, D), bf16),
                   jax.ShapeDtypeStruct((T, D), bf16)],
        scratch_shapes=[pltpu.VMEM((tm, D), bf16)],
        compiler_params=_cparams(("parallel",)), name=name,
    )(x, g, wt, gv, wm, bsb)


def a_mid_bwd(p, dy, gv, wm, bsb, wt, x, g, tok, dres, *, tm, name):
    T = p.shape[0]

    def body(p_ref, dy_ref, gv_ref, wm_ref, bs_ref, wt_ref, x_ref, g_ref, tok_ref, dres_ref,
             dp_ref, dgv_ref, dwm_ref, dbs_ref, dx_ref, dgn_ref, vn_sc, dvn_sc):
        @pl.when(pl.program_id(0) == 0)
        def _():
            dgn_ref[...] = jnp.zeros(dgn_ref.shape, f32)
            dgv_ref[...] = jnp.zeros(dgv_ref.shape, f32)
            dwm_ref[...] = jnp.zeros(dwm_ref.shape, f32)
            dbs_ref[...] = jnp.zeros(dbs_ref.shape, f32)

        zv = p_ref[:, D:].astype(f32)
        cdf_v = _gelu_cdf(zv)
        v = zv * cdf_v
        vc = v - jnp.mean(v, axis=-1, keepdims=True)
        rstd = lax.rsqrt(jnp.mean(vc * vc, axis=-1, keepdims=True) + EPS)
        vhat = vc * rstd
        vn_sc[...] = (vhat * gv_ref[...]).astype(bf16)
        for g in range(A_GROUPS):
            cs = slice(g * GBLK, (g + 1) * GBLK)
            dwm = jnp.zeros((GBLK, GBLK), f32)
            dbs = jnp.zeros((GBLK, 1), f32)
            for r in range(tm // GBLK):
                rs = slice(r * GBLK, (r + 1) * GBLK)
                zu = p_ref[rs, cs].astype(f32)
                cdf_u = _gelu_cdf(zu)
                vn = vn_sc[rs, cs]
                sv = _dot(wm_ref[g], vn) + bs_ref[g]
                dyb = dy_ref[rs, cs].astype(f32)
                dsv = dyb * (zu * cdf_u)
                dsvb = dsv.astype(bf16)
                dp_ref[rs, cs] = (dyb * sv * _gelu_grad(zu, cdf_u)).astype(bf16)
                dwm += _dot_nt(dsvb, vn)
                dbs += jnp.sum(dsv, axis=1, keepdims=True)
                dvn_sc[rs, cs] = _dot_tn(wm_ref[g], dsvb)
            dwm_ref[g] += dwm
            dbs_ref[g] += dbs
        dvn = dvn_sc[...]
        dgv_ref[...] += jnp.sum(dvn * vhat, axis=0, keepdims=True)
        dvhat = dvn * gv_ref[...]
        dv = rstd * (dvhat - jnp.mean(dvhat, axis=-1, keepdims=True)
                     - vhat * jnp.mean(dvhat * vhat, axis=-1, keepdims=True))
        dp_ref[:, D:] = (dv * _gelu_grad(zv, cdf_v)).astype(bf16)
        _normbwd_tail(_dot(dp_ref[...], wt_ref[...]), x_ref, g_ref, tok_ref, dres_ref, dx_ref, dgn_ref)

    tail_in, tail_out = _normbwd_specs(tm, wt)
    return pl.pallas_call(
        body, grid=(T // tm,),
        in_specs=[pl.BlockSpec((tm, 2 * D), lambda i: (i, 0)), pl.BlockSpec((tm, D), lambda i: (i, 0)),
                  pl.BlockSpec((1, D), lambda i: (0, 0)),
                  pl.BlockSpec((A_GROUPS, GBLK, GBLK), lambda i: (0, 0, 0)),
                  pl.BlockSpec((A_GROUPS, GBLK, GBLK), lambda i: (0, 0, 0))] + tail_in,
        out_specs=[pl.BlockSpec((tm, 2 * D), lambda i: (i, 0)), pl.BlockSpec((1, D), lambda i: (0, 0)),
                   pl.BlockSpec((A_GROUPS, GBLK, GBLK), lambda i: (0, 0, 0)),
                   pl.BlockSpec((A_GROUPS, GBLK, 1), lambda i: (0, 0, 0))] + tail_out,
        out_shape=[jax.ShapeDtypeStruct((T, 2 * D), bf16), jax.ShapeDtypeStruct((1, D), f32),
                   jax.ShapeDtypeStruct((A_GROUPS, GBLK, GBLK), f32), jax.ShapeDtypeStruct((A_GROUPS, GBLK, 1), f32),
                   jax.ShapeDtypeStruct((T, D), f32), jax.ShapeDtypeStruct((1, D), f32)],
        scratch_shapes=[pltpu.VMEM((tm, D), bf16), pltpu.VMEM((tm, D), f32)],
        compiler_params=_cparams(("arbitrary",)), name=name,
    )(p, dy, gv, wm, bsb, wt, x, g, tok, dres)


def b_mid_fwd(p, cw, *, tm, S, name):
    T = p.shape[0]
    nst = S // tm

    def body(p_ref, ph_ref, cw_ref, y_ref, ext):
        first = (pl.program_id(0) % nst) == 0
        for c0 in range(0, D, COLS):
            cs, cc, cx = slice(c0, c0 + COLS), slice(D + c0, D + c0 + COLS), slice(2 * D + c0, 2 * D + c0 + COLS)
            q = p_ref[:, cc].astype(f32) * p_ref[:, cx].astype(f32)
            qh = ph_ref[:, cc].astype(f32) * ph_ref[:, cx].astype(f32)
            ext[0:HALO, cs] = jnp.where(first, 0.0, qh)
            ext[HALO:, cs] = q
            y = (cw_ref[0:1, cs] * ext[pl.ds(HALO - 2, tm), cs] + cw_ref[1:2, cs] * ext[pl.ds(HALO - 1, tm), cs]
                 + cw_ref[2:3, cs] * q)
            y_ref[:, cs] = (p_ref[:, cs].astype(f32) * y).astype(bf16)

    return pl.pallas_call(
        body, grid=(T // tm,),
        in_specs=[pl.BlockSpec((tm, 3 * D), lambda i: (i, 0)), _prev_halo_spec(tm, 3 * D),
                  pl.BlockSpec((3, D), lambda i: (0, 0))],
        out_specs=pl.BlockSpec((tm, D), lambda i: (i, 0)),
        out_shape=jax.ShapeDtypeStruct((T, D), bf16),
        scratch_shapes=[pltpu.VMEM((tm + HALO, D), f32)],
        compiler_params=_cparams(("parallel",)), name=name,
    )(p, p, cw)


def b_mid_bwd(p, dy, cw, wt, x, g, tok, dres, *, tm, S, name):
    T = p.shape[0]
    nst = S // tm

    def body(p_ref, ph_ref, pn_ref, dy_ref, dyn_ref, cw_ref, wt_ref, x_ref, g_ref, tok_ref, dres_ref,
             dp_ref, dcw_ref, dx_ref, dgn_ref, ext, ext2):
        i = pl.program_id(0)
        first = (i % nst) == 0
        last = (i % nst) == nst - 1

        @pl.when(i == 0)
        def _():
            dgn_ref[...] = jnp.zeros(dgn_ref.shape, f32)
            dcw_ref[...] = jnp.zeros(dcw_ref.shape, f32)

        for c0 in range(0, D, COLS):
            cs, cc, cx = slice(c0, c0 + COLS), slice(D + c0, D + c0 + COLS), slice(2 * D + c0, 2 * D + c0 + COLS)
            gb, gc, xt = p_ref[:, cs].astype(f32), p_ref[:, cc].astype(f32), p_ref[:, cx].astype(f32)
            q = gc * xt
            ext[0:HALO, cs] = jnp.where(first, 0.0, ph_ref[:, cc].astype(f32) * ph_ref[:, cx].astype(f32))
            ext[HALO:, cs] = q
            q2 = ext[pl.ds(HALO - 2, tm), cs]
            q1 = ext[pl.ds(HALO - 1, tm), cs]
            y = cw_ref[0:1, cs] * q2 + cw_ref[1:2, cs] * q1 + cw_ref[2:3, cs] * q
            dyo = dy_ref[:, cs].astype(f32)
            dp_ref[:, cs] = (dyo * y).astype(bf16)
            dyc = dyo * gb
            ext2[0:tm, cs] = dyc
            ext2[tm:, cs] = jnp.where(last, 0.0, dyn_ref[:, cs].astype(f32) * pn_ref[:, cs].astype(f32))
            dq = cw_ref[2:3, cs] * dyc + cw_ref[1:2, cs] * ext2[pl.ds(1, tm), cs] + cw_ref[0:1, cs] * ext2[pl.ds(2, tm), cs]
            dp_ref[:, cc] = (dq * xt).astype(bf16)
            dp_ref[:, cx] = (dq * gc).astype(bf16)
            dcw_ref[0:1, cs] += jnp.sum(dyc * q2, axis=0, keepdims=True)
            dcw_ref[1:2, cs] += jnp.sum(dyc * q1, axis=0, keepdims=True)
            dcw_ref[2:3, cs] += jnp.sum(dyc * q, axis=0, keepdims=True)
        _normbwd_tail(_dot(dp_ref[...], wt_ref[...]), x_ref, g_ref, tok_ref, dres_ref, dx_ref, dgn_ref)

    tail_in, tail_out = _normbwd_specs(tm, wt)
    return pl.pallas_call(
        body, grid=(T // tm,),
        in_specs=[pl.BlockSpec((tm, 3 * D), lambda i: (i, 0)), _prev_halo_spec(tm, 3 * D), _next_halo_spec(tm, 3 * D, T),
                  pl.BlockSpec((tm, D), lambda i: (i, 0)), _next_halo_spec(tm, D, T),
                  pl.BlockSpec((3, D), lambda i: (0, 0))] + tail_in,
        out_specs=[pl.BlockSpec((tm, 3 * D), lambda i: (i, 0)), pl.BlockSpec((3, D), lambda i: (0, 0))] + tail_out,
        out_shape=[jax.ShapeDtypeStruct((T, 3 * D), bf16), jax.ShapeDtypeStruct((3, D), f32),
                   jax.ShapeDtypeStruct((T, D), f32), jax.ShapeDtypeStruct((1, D), f32)],
        scratch_shapes=[pltpu.VMEM((tm + HALO, D), f32), pltpu.VMEM((tm + HALO, D), f32)],
        compiler_params=_cparams(("arbitrary",)), name=name,
    )(p, p, p, dy, dy, cw, wt, x, g, tok, dres)


def _pool_counts(i, nst, tm, rows, row0, w):
    t = (i % nst) * tm + row0 + lax.broadcasted_iota(jnp.int32, (rows, 1), 0)
    return jnp.minimum(t + 1, w).astype(f32)


def _pool_diff(p_ref, ext, g, i, nst, tm):
    w = POOL_WINDOWS[g]
    cs = slice(g * C_GDIM, (g + 1) * C_GDIM)
    pg = p_ref[:, cs].astype(f32)
    s = pg
    for k in range(1, w):
        s = s + ext[pl.ds(HALO - k, tm), cs]
    return s / _pool_counts(i, nst, tm, tm, 0, w) - pg


def c_mid_fwd(p, wg, scale, *, tm, S, name):
    T = p.shape[0]
    nst = S // tm

    def body(p_ref, ph_ref, wg_ref, sc_ref, y_ref, ext):
        i = pl.program_id(0)
        first = (i % nst) == 0
        ext[0:HALO, :] = jnp.where(first, 0.0, ph_ref[...].astype(f32))
        ext[HALO:, :] = p_ref[...].astype(f32)
        for g in range(C_GROUPS):
            cs = slice(g * C_GDIM, (g + 1) * C_GDIM)
            dg = _pool_diff(p_ref, ext, g, i, nst, tm).astype(bf16)
            wv = wg_ref[:, g].reshape(C_GDIM, C_GDIM)
            y_ref[:, cs] = (_dot(dg, wv) * sc_ref[:, cs]).astype(bf16)

    return pl.pallas_call(
        body, grid=(T // tm,),
        in_specs=[pl.BlockSpec((tm, D), lambda i: (i, 0)), _prev_halo_spec(tm, D),
                  pl.BlockSpec((NDEV, C_GROUPS, C_GDIM // NDEV, C_GDIM), lambda i: (0, 0, 0, 0)),
                  pl.BlockSpec((1, D), lambda i: (0, 0))],
        out_specs=pl.BlockSpec((tm, D), lambda i: (i, 0)),
        out_shape=jax.ShapeDtypeStruct((T, D), bf16),
        scratch_shapes=[pltpu.VMEM((tm + HALO, D), f32)],
        compiler_params=_cparams(("parallel",)), name=name,
    )(p, p, wg, scale)


def c_mid_bwd(p, dy, wg, scale, w, x, g, tok, dres, *, tm, S, name):
    T = p.shape[0]
    nst = S // tm

    def body(p_ref, ph_ref, dy_ref, dyn_ref, wg_ref, sc_ref, w_ref, x_ref, g_ref, tok_ref, dres_ref,
             dp_ref, dsc_ref, dwg_ref, dx_ref, dgn_ref, ext, ext2):
        i = pl.program_id(0)
        first = (i % nst) == 0
        last = (i % nst) == nst - 1

        @pl.when(i == 0)
        def _():
            dgn_ref[...] = jnp.zeros(dgn_ref.shape, f32)
            dsc_ref[...] = jnp.zeros(dsc_ref.shape, f32)
            dwg_ref[...] = jnp.zeros(dwg_ref.shape, f32)

        ext[0:HALO, :] = jnp.where(first, 0.0, ph_ref[...].astype(f32))
        ext[HALO:, :] = p_ref[...].astype(f32)
        for g in range(C_GROUPS):
            w = POOL_WINDOWS[g]
            cs = slice(g * C_GDIM, (g + 1) * C_GDIM)
            dg = _pool_diff(p_ref, ext, g, i, nst, tm).astype(bf16)
            wv = wg_ref[:, g].reshape(C_GDIM, C_GDIM)
            dyo = dy_ref[:, cs].astype(f32)
            dsc_ref[:, cs] += jnp.sum(dyo * _dot(dg, wv), axis=0, keepdims=True)
            dyp = (dyo * sc_ref[:, cs]).astype(bf16)
            dypn = (dyn_ref[:, cs].astype(f32) * sc_ref[:, cs]).astype(bf16)
            dwg_ref[g] += _dot_tn(dg, dyp)
            dd = _dot_nt(dyp, wv)
            ddn = _dot_nt(dypn, wv)
            ext2[0:tm, cs] = dd / _pool_counts(i, nst, tm, tm, 0, w)
            ext2[tm:, cs] = jnp.where(last, 0.0, ddn / _pool_counts(i, nst, tm, HALO, tm, w))
            s = -dd
            for k in range(w):
                s = s + ext2[pl.ds(k, tm), cs]
            dp_ref[:, cs] = s.astype(bf16)
        _normbwd_tail(_dot_nt(dp_ref[...], w_ref[...]), x_ref, g_ref, tok_ref, dres_ref, dx_ref, dgn_ref)

    tail_in, tail_out = _normbwd_specs(tm, w)
    return pl.pallas_call(
        body, grid=(T // tm,),
        in_specs=[pl.BlockSpec((tm, D), lambda i: (i, 0)), _prev_halo_spec(tm, D),
                  pl.BlockSpec((tm, D), lambda i: (i, 0)), _next_halo_spec(tm, D, T),
                  pl.BlockSpec((NDEV, C_GROUPS, C_GDIM // NDEV, C_GDIM), lambda i: (0, 0, 0, 0)),
                  pl.BlockSpec((1, D), lambda i: (0, 0))] + tail_in,
        out_specs=[pl.BlockSpec((tm, D), lambda i: (i, 0)), pl.BlockSpec((1, D), lambda i: (0, 0)),
                   pl.BlockSpec((C_GROUPS, C_GDIM, C_GDIM), lambda i: (0, 0, 0))] + tail_out,
        out_shape=[jax.ShapeDtypeStruct((T, D), bf16), jax.ShapeDtypeStruct((1, D), f32),
                   jax.ShapeDtypeStruct((C_GROUPS, C_GDIM, C_GDIM), f32),
                   jax.ShapeDtypeStruct((T, D), f32), jax.ShapeDtypeStruct((1, D), f32)],
        scratch_shapes=[pltpu.VMEM((tm + HALO, D), f32), pltpu.VMEM((tm + HALO, D), f32)],
        compiler_params=_cparams(("arbitrary",)), name=name,
    )(p, p, dy, dy, wg, scale, w, x, g, tok, dres)


def final_loss(x, g, tgt, *, tm, name):
    T = x.shape[0]

    def body(x_ref, g_ref, t_ref, loss_ref, dx_ref, dg_ref):
        @pl.when(pl.program_id(0) == 0)
        def _():
            loss_ref[...] = jnp.zeros(loss_ref.shape, f32)
            dg_ref[...] = jnp.zeros(dg_ref.shape, f32)

        xv = x_ref[...]
        r = lax.rsqrt(jnp.mean(xv * xv, axis=-1, keepdims=True) + EPS)
        xhat = xv * r
        err = xhat * g_ref[...] - t_ref[...]
        loss_ref[...] += 0.5 * jnp.sum(jnp.mean(err * err, axis=-1, keepdims=True))
        dy = err * (1.0 / D)
        dg_ref[...] += jnp.sum(dy * xhat, axis=0, keepdims=True)
        dxhat = dy * g_ref[...]
        dx_ref[...] = r * (dxhat - xhat * jnp.mean(dxhat * xhat, axis=-1, keepdims=True))

    return pl.pallas_call(
        body, grid=(T // tm,),
        in_specs=[pl.BlockSpec((tm, D), lambda i: (i, 0)), pl.BlockSpec((1, D), lambda i: (0, 0)),
                  pl.BlockSpec((tm, D), lambda i: (i, 0))],
        out_specs=[pl.BlockSpec((8, 128), lambda i: (0, 0)), pl.BlockSpec((tm, D), lambda i: (i, 0)),
                   pl.BlockSpec((1, D), lambda i: (0, 0))],
        out_shape=[jax.ShapeDtypeStruct((8, 128), f32), jax.ShapeDtypeStruct((T, D), f32),
                   jax.ShapeDtypeStruct((1, D), f32)],
        compiler_params=_cparams(("arbitrary",)), name=name,
    )(x, g, tgt)


def _slot(px, py, pc):
    return 4 * px + 2 * py + pc


def _with_own_block(s, me):
    zone = lax.empty((NDEV,) + s.shape, s.dtype)
    return lax.dynamic_update_slice(zone, s[None], (me,) + (0,) * s.ndim)


def all_gather(arrs, me, *, name):
    n = len(arrs)

    def body(*refs):
        ins, outs = refs[:n], refs[2 * n:3 * n]
        send_sems, recv_sems = refs[3 * n:]
        x, y, c = lax.axis_index("x"), lax.axis_index("y"), lax.axis_index("c")
        me, sibling = (x, y, c), (x, y, 1 - c)
        chips = [(1 - x, y), (x, 1 - y), (1 - x, 1 - y)]

        def copy(a, k, block, to, src=None):
            dst = outs[a].at[_slot(*block)]
            return pltpu.make_async_remote_copy(
                src_ref=dst if src is None else src, dst_ref=dst,
                send_sem=send_sems.at[a, k], recv_sem=recv_sems.at[a, k], device_id=to, device_id_type=MESH)

        first = []
        for a in range(n):
            first.append(copy(a, 0, me, sibling, src=ins[a]))
            first += [copy(a, 1 + j, me, (*chip, c), src=ins[a]) for j, chip in enumerate(chips)]
        for cp in first:
            cp.start()
        passed = []
        for j, chip in enumerate(chips):
            for a in range(n):
                copy(a, 1 + j, (*chip, c), me).wait_recv()
                fwd = copy(a, 4 + j, (*chip, c), sibling)
                fwd.start()
                passed.append(fwd)
        for a in range(n):
            copy(a, 0, sibling, me).wait_recv()
        for j, chip in enumerate(chips):
            for a in range(n):
                copy(a, 4 + j, (*chip, 1 - c), me).wait_recv()
        for cp in first + passed:
            cp.wait_send()

    any_spec = pl.BlockSpec(memory_space=pl.ANY)
    return pl.pallas_call(
        body,
        in_specs=[any_spec] * (2 * n), out_specs=[any_spec] * n,
        out_shape=[jax.ShapeDtypeStruct((NDEV,) + a.shape, a.dtype) for a in arrs],
        input_output_aliases={n + i: i for i in range(n)},
        scratch_shapes=[pltpu.SemaphoreType.DMA((n, 7)), pltpu.SemaphoreType.DMA((n, 7))],
        compiler_params=pltpu.CompilerParams(has_side_effects=True), name=name,
    )(*arrs, *[_with_own_block(a, me) for a in arrs])


HBM_SPEC = pl.BlockSpec(memory_space=pltpu.HBM)
SEM_SPEC = pl.BlockSpec(memory_space=pltpu.SEMAPHORE)
ANY_SPEC = pl.BlockSpec(memory_space=pl.ANY)
TOKEN_SHAPE = jax.ShapeDtypeStruct((8, 128), f32)
DATAFLOW_EFFECT = pltpu.SideEffectType.DATAFLOW_SIDE_EFFECTING


def _in_hbm(a):
    return pltpu.with_memory_space_constraint(a, pltpu.HBM)


def _hbm_like(a):
    return pltpu.HBM(a.shape, a.dtype)


def _mesh_pos():
    return lax.axis_index("x"), lax.axis_index("y"), lax.axis_index("c")


def _gather_targets(x, y, c):
    return [(x, y, 1 - c), (1 - x, y, c), (x, 1 - y, c), (1 - x, 1 - y, c)]


def gather_start(shards, me, after, *, name):
    n = len(shards)
    extra = [] if after is None else [after]

    def body(*refs):
        srcs, lands = refs[:n], refs[n:2 * n]
        send_sems, recv_sems = refs[2 * n + len(extra)], refs[2 * n + len(extra) + 1]
        token = refs[-1]
        x, y, c = _mesh_pos()
        me = _slot(x, y, c)
        for a in range(n):
            for k, to in enumerate(_gather_targets(x, y, c)):
                pltpu.make_async_remote_copy(
                    src_ref=srcs[a], dst_ref=lands[a].at[me], send_sem=send_sems.at[4 * a + k], recv_sem=recv_sems.at[4 * a + k],
                    device_id=to, device_id_type=MESH).start()
        token[...] = jnp.zeros(token.shape, f32)

    lands = [_with_own_block(s, me) for s in shards]
    sems = pltpu.SemaphoreType.DMA((4 * n,))
    out = pl.pallas_call(
        body, name=name,
        in_specs=[HBM_SPEC] * (2 * n) + [ANY_SPEC] * len(extra),
        out_specs=[SEM_SPEC, SEM_SPEC] + [HBM_SPEC] * (2 * n) + [pl.BlockSpec(memory_space=pltpu.VMEM)],
        out_shape=[sems, sems] + [_hbm_like(s) for s in shards] + [_hbm_like(l) for l in lands] + [TOKEN_SHAPE],
        input_output_aliases={i: 2 + i for i in range(2 * n)},
        compiler_params=pltpu.CompilerParams(has_side_effects=DATAFLOW_EFFECT),
    )(*[_in_hbm(s) for s in shards], *[_in_hbm(l) for l in lands], *extra)
    return out[0], out[1], out[2:2 + n], out[2 + n:2 + 2 * n], out[-1]


def gather_wait(send_sems, recv_sems, shards, lands, after, *, name):
    n = len(shards)
    after = list(after) if isinstance(after, (list, tuple)) else [after]

    def body(*refs):
        srcs, lands_in = refs[:n], refs[n:2 * n]
        send_sems, recv_sems = refs[2 * n], refs[2 * n + 1]
        x, y, c = _mesh_pos()
        for a in range(n):
            for k, frm in enumerate(_gather_targets(x, y, c)):
                cp = pltpu.make_async_remote_copy(
                    src_ref=srcs[a], dst_ref=lands_in[a].at[_slot(*frm)], send_sem=send_sems.at[4 * a + k],
                    recv_sem=recv_sems.at[4 * a + k], device_id=frm, device_id_type=MESH)
                cp.wait_send()
                cp.wait_recv()

    out = pl.pallas_call(
        body, name=name,
        in_specs=[HBM_SPEC] * (2 * n) + [SEM_SPEC, SEM_SPEC] + [ANY_SPEC] * len(after),
        out_specs=[HBM_SPEC] * (2 * n),
        out_shape=[_hbm_like(s) for s in shards] + [_hbm_like(l) for l in lands],
        input_output_aliases={i: i for i in range(2 * n)},
        compiler_params=pltpu.CompilerParams(has_side_effects=DATAFLOW_EFFECT),
    )(*shards, *lands, send_sems, recv_sems, *after)
    return out[:n], out[n:]


def gather_finish(lands, *, name):
    n = len(lands)

    def body(*refs):
        lands_in, lands_out = refs[:n], refs[n:2 * n]
        send_sems, recv_sems = refs[2 * n:]
        x, y, c = _mesh_pos()
        sibling = (x, y, 1 - c)
        chips = [(1 - x, y), (x, 1 - y), (1 - x, 1 - y)]
        sent = []
        for a in range(n):
            for j, chip in enumerate(chips):
                s = _slot(*chip, c)
                cp = pltpu.make_async_remote_copy(
                    src_ref=lands_in[a].at[s], dst_ref=lands_out[a].at[s], send_sem=send_sems.at[a, j],
                    recv_sem=recv_sems.at[a, j], device_id=sibling, device_id_type=MESH)
                cp.start()
                sent.append(cp)
        for a in range(n):
            for j, chip in enumerate(chips):
                s = _slot(*chip, 1 - c)
                pltpu.make_async_remote_copy(
                    src_ref=lands_in[a].at[s], dst_ref=lands_out[a].at[s], send_sem=send_sems.at[a, j],
                    recv_sem=recv_sems.at[a, j], device_id=sibling, device_id_type=MESH).wait_recv()
        for cp in sent:
            cp.wait_send()

    return pl.pallas_call(
        body, name=name,
        in_specs=[ANY_SPEC] * n, out_specs=[ANY_SPEC] * n,
        out_shape=[jax.ShapeDtypeStruct(l.shape, l.dtype) for l in lands],
        input_output_aliases={i: i for i in range(n)},
        scratch_shapes=[pltpu.SemaphoreType.DMA((n, 3)), pltpu.SemaphoreType.DMA((n, 3))],
        compiler_params=pltpu.CompilerParams(has_side_effects=True),
    )(*lands)


def _peer(x, y, c, r):
    return (x ^ ((r >> 2) & 1), y ^ ((r >> 1) & 1), c ^ (r & 1))


def exchange_start(items, lands, after, *, name):
    n, m = len(items), len(lands)
    extra = [] if after is None else [after]

    def body(*refs):
        gs, zones = refs[:n], refs[n:n + m]
        send_sems, recv_sems = refs[n + m + len(extra)], refs[n + m + len(extra) + 1]
        token = refs[-1]
        x, y, c = _mesh_pos()
        me = _slot(x, y, c)
        for r in (1, 2, 4, 3, 5, 6, 7):
            to = _peer(x, y, c, r)
            for a, (_, zi, l) in enumerate(items):
                pltpu.make_async_remote_copy(
                    src_ref=gs[a].at[_slot(*to)], dst_ref=zones[zi].at[me, l], send_sem=send_sems.at[7 * a + r - 1],
                    recv_sem=recv_sems.at[7 * a + r - 1], device_id=to, device_id_type=MESH).start()
        token[...] = jnp.zeros(token.shape, f32)

    gs = [g for g, _, _ in items]
    sems = pltpu.SemaphoreType.DMA((7 * n,))
    out = pl.pallas_call(
        body, name=name,
        in_specs=[HBM_SPEC] * (n + m) + [ANY_SPEC] * len(extra),
        out_specs=[SEM_SPEC, SEM_SPEC] + [HBM_SPEC] * (n + m) + [pl.BlockSpec(memory_space=pltpu.VMEM)],
        out_shape=[sems, sems] + [_hbm_like(g) for g in gs] + [_hbm_like(z) for z in lands] + [TOKEN_SHAPE],
        input_output_aliases={i: 2 + i for i in range(n + m)},
        compiler_params=pltpu.CompilerParams(has_side_effects=DATAFLOW_EFFECT),
    )(*[_in_hbm(g) for g in gs], *[_in_hbm(z) for z in lands], *extra)
    return out[0], out[1], out[2:2 + n], out[2 + n:2 + n + m], out[-1]


def exchange_wait(parts, lands, after, *, name):
    m = len(lands)
    flat_gs = [g for _, _, _, gs in parts for g in gs]
    ng = len(flat_gs)
    after = list(after) if isinstance(after, (list, tuple)) else [after]

    def body(*refs):
        gs, zones = refs[:ng], refs[ng:ng + m]
        sem_refs = refs[ng + m:ng + m + 2 * len(parts)]
        x, y, c = _mesh_pos()
        base = 0
        for pi, (items, _, _, _) in enumerate(parts):
            send_sems, recv_sems = sem_refs[2 * pi], sem_refs[2 * pi + 1]
            for r in range(1, NDEV):
                frm = _peer(x, y, c, r)
                for a, (_, zi, l) in enumerate(items):
                    cp = pltpu.make_async_remote_copy(
                        src_ref=gs[base + a].at[_slot(*frm)], dst_ref=zones[zi].at[_slot(*frm), l],
                        send_sem=send_sems.at[7 * a + r - 1], recv_sem=recv_sems.at[7 * a + r - 1],
                        device_id=frm, device_id_type=MESH)
                    cp.wait_send()
                    cp.wait_recv()
            base += len(items)

    sem_args = [s for _, ss, rs, _ in parts for s in (ss, rs)]
    out = pl.pallas_call(
        body, name=name,
        in_specs=[HBM_SPEC] * (ng + m) + [SEM_SPEC] * len(sem_args) + [ANY_SPEC] * len(after),
        out_specs=[HBM_SPEC] * (ng + m),
        out_shape=[_hbm_like(g) for g in flat_gs] + [_hbm_like(z) for z in lands],
        input_output_aliases={i: i for i in range(ng + m)},
        compiler_params=pltpu.CompilerParams(has_side_effects=DATAFLOW_EFFECT),
    )(*flat_gs, *lands, *sem_args, *after)
    return out[ng:]


def sum8(g, *, name):
    _, R, C = g.shape
    tr = R

    def body(g_ref, o_ref):
        s = g_ref[0]
        for k in range(1, NDEV):
            s = s + g_ref[k]
        o_ref[...] = s

    return pl.pallas_call(
        body, grid=(R // tr,),
        in_specs=[pl.BlockSpec((NDEV, tr, C), lambda i: (0, i, 0))],
        out_specs=pl.BlockSpec((tr, C), lambda i: (i, 0)),
        out_shape=jax.ShapeDtypeStruct((R, C), f32),
        compiler_params=_cparams(("parallel",)), name=name,
    )(g)


def _adam_math(w, g, m, v):
    m = ADAM_B1 * m + (1.0 - ADAM_B1) * g
    v = ADAM_B2 * v + (1.0 - ADAM_B2) * (g * g)
    m_hat = m / (1.0 - ADAM_B1 ** ADAM_STEP)
    v_hat = v / (1.0 - ADAM_B2 ** ADAM_STEP)
    delta = -ADAM_LR * (m_hat / (jnp.sqrt(v_hat) + ADAM_EPS) + ADAM_WD * w)
    return delta, m, v


def adamw_slices(w, m, v, recv, *, row0=0, col0=0, rrows=None, tr=None, tc=None, l0=0, nl=None, prev=None, name):
    L, R, C = w.shape
    Rp, Cp = recv.shape[2], recv.shape[3]
    nl = L - l0 if nl is None else nl
    tr = R if tr is None else tr
    tc = C if tc is None else tc
    rrows = Rp if rrows is None else rrows
    assert R % tr == 0 and C % tc == 0 and row0 % rrows == 0 and R <= rrows
    assert (tr == R or (Rp == R and row0 == 0)) and (tc == C or (Cp == C and col0 == 0))
    rr = rrows if tr == R else tr
    rc = Cp if tc == C else tc
    rb = row0 // rrows if tr == R else 0
    prevs = [] if prev is None else list(prev)

    def body(w_ref, m_ref, v_ref, r_ref, *rest):
        g_ref, d_ref, nm_ref, nv_ref = rest[len(prevs):]
        c0 = col0 if tc == C else 0
        g = r_ref[0, 0:tr, c0:c0 + tc].astype(f32)
        for k in range(1, NDEV):
            g = g + r_ref[k, 0:tr, c0:c0 + tc].astype(f32)
        delta, nm, nv = _adam_math(w_ref[...], g, m_ref[...], v_ref[...])
        g_ref[...] = g
        d_ref[...] = delta
        nm_ref[...] = nm
        nv_ref[...] = nv

    wspec = pl.BlockSpec((None, tr, tc), lambda l, i, j: (l0 + l, i, j))
    return pl.pallas_call(
        body, grid=(nl, R // tr, C // tc),
        in_specs=[wspec, wspec, wspec, pl.BlockSpec((NDEV, None, rr, rc), lambda l, i, j: (0, l0 + l, rb + i, j))]
        + [pl.BlockSpec(memory_space=pl.ANY)] * len(prevs),
        out_specs=[wspec] * 4, out_shape=[jax.ShapeDtypeStruct((L, R, C), f32)] * 4,
        input_output_aliases={4 + q: q for q in range(len(prevs))},
        compiler_params=_cparams(("parallel", "parallel", "parallel")), name=name,
    )(w, m, v, recv, *prevs)


def adamw_slices_t(w, m, v, recv, *, name):
    L, R, C = w.shape

    def body(w_ref, m_ref, v_ref, r_ref, g_ref, d_ref, nm_ref, nv_ref):
        gt = r_ref[0].astype(f32)
        for k in range(1, NDEV):
            gt = gt + r_ref[k].astype(f32)
        g = gt.T
        delta, nm, nv = _adam_math(w_ref[...], g, m_ref[...], v_ref[...])
        g_ref[...] = g
        d_ref[...] = delta
        nm_ref[...] = nm
        nv_ref[...] = nv

    wspec = pl.BlockSpec((None, R, C), lambda l: (l, 0, 0))
    return pl.pallas_call(
        body, grid=(L,),
        in_specs=[wspec, wspec, wspec, pl.BlockSpec((NDEV, None, C, R), lambda l: (0, l, 0, 0))],
        out_specs=[wspec] * 4, out_shape=[jax.ShapeDtypeStruct((L, R, C), f32)] * 4,
        compiler_params=_cparams(("parallel",)), name=name,
    )(w, m, v, recv)


def adamw_plain(w, m, v, g, *, name):
    R, C = w.shape

    def body(w_ref, m_ref, v_ref, g_ref, d_ref, nm_ref, nv_ref):
        delta, nm, nv = _adam_math(w_ref[...], g_ref[...], m_ref[...], v_ref[...])
        d_ref[...] = delta
        nm_ref[...] = nm
        nv_ref[...] = nv

    return pl.pallas_call(
        body, out_shape=[jax.ShapeDtypeStruct((R, C), f32)] * 3, name=name,
    )(w, m, v, g)


def _chunk_causal_mask():
    pos = jnp.arange(GBLK)
    return (pos[None, :] // CHUNK) <= (pos[:, None] // CHUNK)


def kernel(x, norm_mix_g, norm_ffn_g, final_norm_g, a_w_in, a_v_norm_g, a_w_s, a_b_s, a_w_out, b_w_in, b_conv_w, b_w_out, c_w_in, c_w_grp, c_scale, c_w_out, f_w_gate, f_w_up, f_w_down, loss_target, m_norm_mix_g, m_norm_ffn_g, m_final_norm_g, m_a_w_in, m_a_v_norm_g, m_a_w_s, m_a_b_s, m_a_w_out, m_b_w_in, m_b_conv_w, m_b_w_out, m_c_w_in, m_c_w_grp, m_c_scale, m_c_w_out, m_f_w_gate, m_f_w_up, m_f_w_down, v_norm_mix_g, v_norm_ffn_g, v_final_norm_g, v_a_w_in, v_a_v_norm_g, v_a_w_s, v_a_b_s, v_a_w_out, v_b_w_in, v_b_conv_w, v_b_w_out, v_c_w_in, v_c_w_grp, v_c_scale, v_c_w_out, v_f_w_gate, v_f_w_up, v_f_w_down):
    B, S, _ = x.shape
    T = B * S
    tm = min(512, S)
    tms = min(512, T)
    tma = min(256, S)
    tmf = min(256, T)
    me = _slot(lax.axis_index("x"), lax.axis_index("y"), lax.axis_index("c"))

    def tb(w):
        return jnp.transpose(w, (0, 2, 1)).astype(bf16)

    small = jnp.concatenate([a_v_norm_g, b_conv_w[0], c_scale, jnp.zeros((2, GBLK), f32)], axis=0)
    started = [gather_start([tb(a_w_in[0:1])[0], a_w_out[0].astype(bf16), small], me, None, name="gather_start_0")]
    tok = started[0][4]
    z = tok[0, 0]
    f_g, f_u, f_d = tb(f_w_gate + z), tb(f_w_up + z), (f_w_down + z).astype(bf16)
    stage_shards = [
        None,
        [f_g[0], f_u[0], f_d[0]],
        [tb(b_w_in + z)[0], (b_w_out[0] + z).astype(bf16), f_g[1], f_u[1], f_d[1]],
        [(c_w_in[0] + z).astype(bf16), (c_w_grp[0] + z).astype(bf16), (c_w_out[0] + z).astype(bf16), f_g[2], f_u[2], f_d[2]],
        [tb(a_w_in[1:2] + z)[0], (a_w_out[1] + z).astype(bf16), f_g[3], f_u[3], f_d[3]],
    ]
    for s in range(1, len(stage_shards)):
        started.append(gather_start(stage_shards[s], me, tok, name=f"gather_start_{s}"))
        tok = started[-1][4]

    def stage_weights(s, after):
        send_sems, recv_sems, shards, lands, _ = started[s]
        _, lands = gather_wait(send_sems, recv_sems, shards, lands, after, name=f"gather_wait_{s}")
        lands = gather_finish(lands, name=f"gather_finish_{s}")
        return [t.reshape(-1, t.shape[-1]) if t.ndim == 3 else t for t in lands]

    wm_masked = jnp.where(_chunk_causal_mask()[None, None], a_w_s, 0.0).astype(bf16)
    bsb = jnp.broadcast_to(a_b_s[:, :, :, None], a_b_s.shape + (GBLK,))

    xc = x.reshape(T, D)
    saved, weights = [], []
    for i in range(DEPTH):
        kind, j = i % 3, i // 3
        gmix = norm_mix_g[i][None]
        if i == 0:
            w_in, w_out, small_g = stage_weights(0, tok)
            small_full = jnp.transpose(small_g.reshape(NDEV, 8, GBLK), (1, 0, 2)).reshape(8, D)
            gv_full, cw_full, cs_full = small_full[0:2], small_full[2:5], small_full[5:6]
            w = [w_in, w_out]
        else:
            w = stage_weights(i + 1, xc)
        if kind == 0:
            w_in = w[0]
            p, h, y = a_fwd(xc, gmix, w_in, gv_full[j][None], wm_masked[j], bsb[j], tm=tma, name=f"a_fwd_{i}")
        elif kind == 1:
            w_in, w_out = w[:2]
            p, h = norm_mm(xc, gmix, w_in, tm=tms, transposed=True, name=f"b_in_{i}")
            y = b_mid_fwd(p, cw_full, tm=tm, S=S, name=f"b_mid_{i}")
        else:
            w_in, w_grp, w_out = w[:3]
            p, h = norm_mm(xc, gmix, w_in, tm=tms, transposed=False, name=f"c_in_{i}")
            y = c_mid_fwd(p, w_grp, cs_full, tm=tm, S=S, name=f"c_mid_{i}")
        if i == 0:
            w = w + stage_weights(1, y)
        weights.append(w)
        w_out, w_fg, w_fu, w_fd = w[-4:]
        x1, x2, pg, pu, a, h2 = ffn_fwd(y, w_out, xc, norm_ffn_g[i][None], w_fg, w_fu, w_fd, tm=tmf, name=f"f_fwd_{i}")
        saved.append((xc, p, h, y, x1, pg, pu, h2, a))
        xc = x2

    loss_blk, dx, d_final = final_loss(xc, final_norm_g[None], loss_target.reshape(T, D), tm=tms, name="final_loss")
    loss = lax.psum(loss_blk[0, 0], AXES)

    zone_shapes = [(2, 256, D), (2, GBLK, D), (1, 384, D), (1, GBLK, D), (1, GBLK, D), (1, GBLK, C_GDIM), (1, GBLK, D),
                   (DEPTH, F_LOC, D), (DEPTH, F_LOC, D), (DEPTH, F_LOC, D)]
    Z_AIN, Z_AOUT, Z_BIN, Z_BOUT, Z_CIN, Z_CGRP, Z_COUT, Z_FG, Z_FU, Z_FD = range(10)
    lands = [lax.empty((NDEV,) + s, bf16) for s in zone_shapes]
    parts = []

    def exchange(items, after, name):
        nonlocal lands
        for g, zi, l in items:
            own = lax.dynamic_index_in_dim(g, me, axis=0, keepdims=True)[None]
            lands[zi] = lax.dynamic_update_slice(lands[zi], own, (me, l) + (0,) * (g.ndim - 1))
        idx = sorted({zi for _, zi, _ in items})
        local = [(g, idx.index(zi), l) for g, zi, l in items]
        send_sems, recv_sems, gs, zs, token = exchange_start(local, [lands[zi] for zi in idx], after, name=name)
        for q, zi in enumerate(idx):
            lands[zi] = zs[q]
        parts.append(([(None, zi, l) for _, zi, l in items], send_sems, recv_sems, gs))
        return token

    d_mix, d_ffn = [None] * DEPTH, [None] * DEPTH
    d_gv, d_wm, d_bs = [None] * 2, [None] * 2, [None] * 2
    tok = None
    for i in reversed(range(DEPTH)):
        kind, j = i % 3, i // 3
        xin, p, h, y, x1, pg, pu, h2, a = saved[i]
        w = weights[i]
        w_in, w_out = w[0], w[-4]
        w_fg, w_fu, w_fd = w[-3:]
        dpg, dpu, dxb, dx1, d_ffn[i], dy, dx1b = ffn_bwd(dx, w_fd, pg, pu, w_fg, w_fu, x1, norm_ffn_g[i][None], tok, w_out,
                                                         tm=tmf, name=f"f_bwd_{i}")
        gw_g, gw_u, gw_d = ffn_dw(dpg, dpu, a, h2, dxb, name=f"f_dw_{i}")
        tok = exchange([(gw_g, Z_FG, i), (gw_u, Z_FU, i), (gw_d, Z_FD, i)], tok, f"exchange_start_ffn_{i}")
        gmix = norm_mix_g[i][None]
        if kind == 0:
            dp, d_gv[j], d_wm[j], d_bs[j], dx, d_mix[i] = a_mid_bwd(
                p, dy, gv_full[j][None], wm_masked[j], bsb[j], w_in, xin, gmix, tok, dx1, tm=tma, name=f"a_bwd_{i}")
            items = [(mm_tn(dp, h, name=f"a_dw_in_{i}"), Z_AIN, j),
                     (mm_tn(y, dx1b, name=f"a_dw_out_{i}"), Z_AOUT, j)]
        elif kind == 1:
            dp, d_cw, dx, d_mix[i] = b_mid_bwd(p, dy, cw_full, w_in, xin, gmix, tok, dx1, tm=tm, S=S, name=f"b_bwd_{i}")
            items = [(mm_tn(dp, h, name=f"b_dw_in_{i}"), Z_BIN, 0),
                     (mm_tn(y, dx1b, name=f"b_dw_out_{i}"), Z_BOUT, 0)]
        else:
            dp, d_cs, d_wgrp, dx, d_mix[i] = c_mid_bwd(p, dy, w[1], cs_full, w_in, xin, gmix, tok, dx1, tm=tm, S=S,
                                                       name=f"c_bwd_{i}")
            gw_cgrp = jnp.transpose(d_wgrp.reshape(C_GROUPS, NDEV, C_GDIM // NDEV, C_GDIM), (1, 0, 2, 3)).astype(bf16)
            items = [(mm_tn(h, dp, name=f"c_dw_in_{i}"), Z_CIN, 0),
                     (gw_cgrp.reshape(NDEV, GBLK, C_GDIM), Z_CGRP, 0),
                     (mm_tn(y, dx1b, name=f"c_dw_out_{i}"), Z_COUT, 0)]
        tok = exchange(items, tok, f"exchange_start_mix_{i}")
    grad_x = dx.reshape(B, S, D)

    small_parts = [
        jnp.concatenate(d_mix, axis=0).reshape(-1, GBLK),
        jnp.concatenate(d_ffn, axis=0).reshape(-1, GBLK),
        d_final.reshape(-1, GBLK),
        jnp.concatenate(d_gv, axis=0).reshape(-1, GBLK),
        jnp.where(_chunk_causal_mask()[None, None], jnp.stack(d_wm), 0.0).reshape(-1, GBLK),
        jnp.stack(d_bs).reshape(-1, GBLK),
        d_cw.reshape(-1, GBLK),
        d_cs.reshape(-1, GBLK),
    ]
    small_rows = [q.shape[0] for q in small_parts]
    sg_send, sg_recv, sg_shards, sg_lands, _ = gather_start(
        [jnp.concatenate(small_parts, axis=0)], me, tok, name="gather_small_start")

    def plain(w, m, v, g, name):
        shp = w.shape
        w2, m2, v2, g2 = (t.reshape(-1, shp[-1]) for t in (w, m, v, g))
        return tuple(t.reshape(shp) for t in adamw_plain(w2, m2, v2, g2, name=name))

    def sliced(w, m, v, recv, name, **kw):
        shp = w.shape
        w3, m3, v3 = (t.reshape((shp[0], -1, shp[-1])) for t in (w, m, v))
        return tuple(t.reshape(shp) for t in adamw_slices(w3, m3, v3, recv, name=name, **kw))

    def sliced_t(w, m, v, recv, name, **kw):
        wt, mt, vt = (jnp.transpose(t, (0, 2, 1)) for t in (w, m, v))
        return adamw_slices(wt, mt, vt, recv, name=name, **kw)

    def untransposed(outs):
        return tuple(jnp.transpose(t, (0, 2, 1)) for t in outs)

    lands = list(exchange_wait(parts[:6], lands, dx, name="exchange_wait"))
    _, _, r_bin, r_bout, r_cin, r_cgrp, r_cout, r_fg, r_fu, r_fd = lands
    res = {}
    res["b_w_in"] = adamw_slices_t(b_w_in, m_b_w_in, v_b_w_in, r_bin, name="adam_b_w_in")
    res["b_w_out"] = sliced(b_w_out, m_b_w_out, v_b_w_out, r_bout, "adam_b_w_out")
    res["c_w_in"] = sliced(c_w_in, m_c_w_in, v_c_w_in, r_cin, "adam_c_w_in")
    res["c_w_grp"] = sliced(c_w_grp, m_c_w_grp, v_c_w_grp, r_cgrp, "adam_c_w_grp")
    res["c_w_out"] = sliced(c_w_out, m_c_w_out, v_c_w_out, r_cout, "adam_c_w_out")
    fg = sliced_t(f_w_gate, m_f_w_gate, v_f_w_gate, r_fg, "adam_f_gate_123", tc=256, l0=1)
    fu = sliced_t(f_w_up, m_f_w_up, v_f_w_up, r_fu, "adam_f_up_123", tc=256, l0=1)
    fd = adamw_slices(f_w_down, m_f_w_down, v_f_w_down, r_fd, tc=256, l0=1, name="adam_f_down_123")
    big = [res[n][1] for n in ("b_w_in", "b_w_out", "c_w_in", "c_w_grp", "c_w_out")] + [fg[1], fu[1], fd[1]]

    ffn0_items, ffn0_send, ffn0_recv, ffn0_gs = parts[6]
    ffn0_zones = [Z_FG, Z_FU, Z_FD]
    ffn0_part = ([(None, ffn0_zones.index(zi), l) for _, zi, l in ffn0_items], ffn0_send, ffn0_recv, ffn0_gs)
    r_fg, r_fu, r_fd = exchange_wait([ffn0_part], [lands[zi] for zi in ffn0_zones], big, name="exchange_wait_ffn0")
    fg = sliced_t(f_w_gate, m_f_w_gate, v_f_w_gate, r_fg, "adam_f_gate_0", tc=256, l0=0, nl=1, prev=fg)
    fu = sliced_t(f_w_up, m_f_w_up, v_f_w_up, r_fu, "adam_f_up_0", tc=256, l0=0, nl=1, prev=fu)
    fd = adamw_slices(f_w_down, m_f_w_down, v_f_w_down, r_fd, tc=256, l0=0, nl=1, prev=fd, name="adam_f_down_0")
    res["f_w_gate"], res["f_w_up"], res["f_w_down"] = untransposed(fg), untransposed(fu), tuple(fd)

    last_items, last_send, last_recv, last_gs = parts[7]
    last_zones = [Z_AIN, Z_AOUT]
    last_part = ([(None, last_zones.index(zi), l) for _, zi, l in last_items], last_send, last_recv, last_gs)
    r_ain, r_aout = exchange_wait([last_part], [lands[zi] for zi in last_zones], [fg[1], fu[1], fd[1]], name="exchange_wait_last")
    res["a_w_in"] = adamw_slices_t(a_w_in, m_a_w_in, v_a_w_in, r_ain, name="adam_a_w_in")
    res["a_w_out"] = sliced(a_w_out, m_a_w_out, v_a_w_out, r_aout, "adam_a_w_out")

    _, sg_lands = gather_wait(sg_send, sg_recv, sg_shards, sg_lands, [res["a_w_in"][1], res["a_w_out"][1]], name="gather_small_wait")
    (gs_all,) = gather_finish(sg_lands, name="gather_small_finish")
    gs = sum8(gs_all, name="sum_small_grads")
    offs = [0]
    for r in small_rows:
        offs.append(offs[-1] + r)
    sp = [gs[offs[q]:offs[q + 1]] for q in range(len(small_parts))]
    grad_norm_mix_g = sp[0].reshape(DEPTH, D)
    grad_norm_ffn_g = sp[1].reshape(DEPTH, D)
    grad_final_norm_g = sp[2].reshape(D)
    grad_a_w_s = sp[4].reshape(a_w_s.shape)
    grad_a_b_s = sp[5].reshape(a_b_s.shape)

    def my_cols(full):
        return lax.dynamic_slice_in_dim(full, me * GBLK, GBLK, axis=1)

    grad_a_v_norm_g = my_cols(sp[3].reshape(2, D))
    grad_b_conv_w = my_cols(sp[6].reshape(3, D))[None]
    grad_c_scale = my_cols(sp[7].reshape(1, D))
    res["norm_mix_g"] = (grad_norm_mix_g,) + plain(norm_mix_g, m_norm_mix_g, v_norm_mix_g, grad_norm_mix_g, "adam_norm_mix")
    res["norm_ffn_g"] = (grad_norm_ffn_g,) + plain(norm_ffn_g, m_norm_ffn_g, v_norm_ffn_g, grad_norm_ffn_g, "adam_norm_ffn")
    res["final_norm_g"] = (grad_final_norm_g,) + tuple(
        t.reshape(D) for t in plain(final_norm_g[None], m_final_norm_g[None], v_final_norm_g[None], grad_final_norm_g[None], "adam_final"))
    res["a_v_norm_g"] = (grad_a_v_norm_g,) + plain(a_v_norm_g, m_a_v_norm_g, v_a_v_norm_g, grad_a_v_norm_g, "adam_a_v_norm")
    res["a_w_s"] = (grad_a_w_s,) + plain(a_w_s, m_a_w_s, v_a_w_s, grad_a_w_s, "adam_a_w_s")
    res["a_b_s"] = (grad_a_b_s,) + plain(a_b_s, m_a_b_s, v_a_b_s, grad_a_b_s, "adam_a_b_s")
    res["b_conv_w"] = (grad_b_conv_w,) + plain(b_conv_w, m_b_conv_w, v_b_conv_w, grad_b_conv_w, "adam_b_conv")
    res["c_scale"] = (grad_c_scale,) + plain(c_scale, m_c_scale, v_c_scale, grad_c_scale, "adam_c_scale")

    order = ["norm_mix_g", "norm_ffn_g", "final_norm_g", "a_w_in", "a_v_norm_g", "a_w_s", "a_b_s", "a_w_out", "b_w_in",
             "b_conv_w", "b_w_out", "c_w_in", "c_w_grp", "c_scale", "c_w_out", "f_w_gate", "f_w_up", "f_w_down"]
    return (loss, grad_x, *[res[n][0] for n in order], *[res[n][1] for n in order],
            *[res[n][2] for n in order], *[res[n][3] for n in order])
```

```python
import jax
import jax.numpy as jnp
from jax import lax
from jax.experimental import pallas as pl
from jax.experimental.pallas import tpu as pltpu

f32 = jnp.float32
bf16 = jnp.bfloat16

NDEV = 8
D = 1024
EPS = 1e-6
CHUNK = 64
GBLK = 128
A_GROUPS = 8
C_GROUPS = 4
C_GDIM = 256
POOL_WINDOWS = (2, 4, 8, 16)
HALO = 16
COLS = 256
F_LOC = 352
MXU_COLS = 256
DEPTH = 4
AXES = ("x", "y", "c")
MESH = pl.DeviceIdType.MESH

ADAM_LR = 0.001
ADAM_B1 = 0.9
ADAM_B2 = 0.999
ADAM_EPS = 1e-08
ADAM_WD = 0.01
ADAM_STEP = 10

VMEM_LIMIT = 56 * 1024 * 1024


def _cparams(sem):
    return pltpu.CompilerParams(dimension_semantics=sem, vmem_limit_bytes=VMEM_LIMIT)


def _gelu(z):
    return 0.5 * z * (1.0 + lax.erf(z * 0.7071067811865476))


def _gelu_cdf(z):
    return 0.5 * (1.0 + lax.erf(z * 0.7071067811865476))


def _gelu_grad(z, cdf):
    return cdf + z * jnp.exp(-0.5 * z * z) * 0.3989422804014327


def _dot(a, b):
    return jnp.dot(a, b, preferred_element_type=f32)


def _dot_nt(a, b):
    return lax.dot_general(a, b, (((1,), (1,)), ((), ())), preferred_element_type=f32)


def _dot_tn(a, b):
    return lax.dot_general(a, b, (((0,), (0,)), ((), ())), preferred_element_type=f32)


def _col_chunks(n, width=1024):
    return [(c, min(c + width, n)) for c in range(0, n, width)]


def norm_mm(x, g, w, *, tm, transposed, name):
    T = x.shape[0]
    n = w.shape[0] if transposed else w.shape[1]

    def body(x_ref, g_ref, w_ref, p_ref, h_ref):
        xv = x_ref[...]
        r = lax.rsqrt(jnp.mean(xv * xv, axis=-1, keepdims=True) + EPS)
        h = (xv * r * g_ref[...]).astype(bf16)
        h_ref[...] = h
        p = _dot_nt(h, w_ref[...]) if transposed else _dot(h, w_ref[...])
        p_ref[...] = p.astype(bf16)

    return pl.pallas_call(
        body, grid=(T // tm,),
        in_specs=[pl.BlockSpec((tm, D), lambda i: (i, 0)), pl.BlockSpec((1, D), lambda i: (0, 0)),
                  pl.BlockSpec(w.shape, lambda i: (0, 0), pipeline_mode=pl.Buffered(1))],
        out_specs=[pl.BlockSpec((tm, n), lambda i: (i, 0)), pl.BlockSpec((tm, D), lambda i: (i, 0))],
        out_shape=[jax.ShapeDtypeStruct((T, n), bf16), jax.ShapeDtypeStruct((T, D), bf16)],
        compiler_params=_cparams(("parallel",)), name=name,
    )(x, g, w)


def ffn_fwd(y, w_out, xin, g, wtg, wtu, wd, *, tm, name):
    T, K = y.shape
    H = wtg.shape[0]

    def body(y_ref, wo_ref, xin_ref, g_ref, wg_ref, wu_ref, wd_ref, x_ref, x2_ref, pg_ref, pu_ref, a_ref, h_ref):
        xv = xin_ref[...] + _dot(y_ref[...], wo_ref[...])
        x_ref[...] = xv
        r = lax.rsqrt(jnp.mean(xv * xv, axis=-1, keepdims=True) + EPS)
        h = (xv * r * g_ref[...]).astype(bf16)
        h_ref[...] = h
        out = None
        for c0, c1 in _col_chunks(H):
            gate = _dot_nt(h, wg_ref[c0:c1, :])
            up = _dot_nt(h, wu_ref[c0:c1, :])
            pg_ref[:, c0:c1] = gate.astype(bf16)
            pu_ref[:, c0:c1] = up.astype(bf16)
            ac = (gate * jax.nn.sigmoid(gate) * up).astype(bf16)
            a_ref[:, c0:c1] = ac
            part = _dot(ac, wd_ref[c0:c1, :])
            out = part if out is None else out + part
        x2_ref[...] = xv + out

    wspec = pl.BlockSpec((H, D), lambda i: (0, 0), pipeline_mode=pl.Buffered(1))
    hspec = pl.BlockSpec((tm, H), lambda i: (i, 0))
    dspec = pl.BlockSpec((tm, D), lambda i: (i, 0))
    return pl.pallas_call(
        body, grid=(T // tm,),
        in_specs=[pl.BlockSpec((tm, K), lambda i: (i, 0)),
                  pl.BlockSpec((K, D), lambda i: (0, 0), pipeline_mode=pl.Buffered(1)),
                  dspec, pl.BlockSpec((1, D), lambda i: (0, 0)), wspec, wspec, wspec],
        out_specs=[dspec, dspec, hspec, hspec, hspec, dspec],
        out_shape=[jax.ShapeDtypeStruct((T, D), f32)] * 2 + [jax.ShapeDtypeStruct((T, H), bf16)] * 3
        + [jax.ShapeDtypeStruct((T, D), bf16)],
        compiler_params=_cparams(("parallel",)), name=name,
    )(y, w_out, xin, g, wtg, wtu, wd)


def ffn_bwd(dx, wd, pg, pu, wtg, wtu, x, g, tok, w_next, *, tm, name):
    T = dx.shape[0]
    H = wd.shape[0]
    kn = w_next.shape[0]
    toks = [] if tok is None else [tok]

    def body(dx_ref, wd_ref, pg_ref, pu_ref, wg_ref, wu_ref, x_ref, g_ref, *rest):
        wn_ref = rest[len(toks)]
        dg_ref, du_ref, dxb_ref, dx1_ref, dgn_ref, dy_ref, dx1b_ref = rest[len(toks) + 1:]

        @pl.when(pl.program_id(0) == 0)
        def _():
            dgn_ref[...] = jnp.zeros(dgn_ref.shape, f32)

        dxv = dx_ref[...]
        dxb = dxv.astype(bf16)
        dxb_ref[...] = dxb
        dh = None
        for c0, c1 in _col_chunks(H):
            da = _dot_nt(dxb, wd_ref[c0:c1, :])
            gate = pg_ref[:, c0:c1].astype(f32)
            up = pu_ref[:, c0:c1].astype(f32)
            sg = jax.nn.sigmoid(gate)
            dgc = (da * up * (sg * (1.0 + gate * (1.0 - sg)))).astype(bf16)
            duc = (da * gate * sg).astype(bf16)
            dg_ref[:, c0:c1] = dgc
            du_ref[:, c0:c1] = duc
            part = _dot(dgc, wg_ref[c0:c1, :]) + _dot(duc, wu_ref[c0:c1, :])
            dh = part if dh is None else dh + part
        gv = g_ref[...] + rest[0][0:1, 0:1] if toks else g_ref[...]
        xv = x_ref[...]
        r = lax.rsqrt(jnp.mean(xv * xv, axis=-1, keepdims=True) + EPS)
        xhat = xv * r
        dgn_ref[...] += jnp.sum(dh * xhat, axis=0, keepdims=True)
        dxhat = dh * gv
        dx1 = dxv + r * (dxhat - xhat * jnp.mean(dxhat * xhat, axis=-1, keepdims=True))
        dx1_ref[...] = dx1
        dx1b = dx1.astype(bf16)
        dx1b_ref[...] = dx1b
        dy_ref[...] = _dot_nt(dx1b, wn_ref[...]).astype(bf16)

    dspec = pl.BlockSpec((tm, D), lambda i: (i, 0))
    hspec = pl.BlockSpec((tm, H), lambda i: (i, 0))
    wspec = pl.BlockSpec((H, D), lambda i: (0, 0), pipeline_mode=pl.Buffered(1))
    gspec = pl.BlockSpec((1, D), lambda i: (0, 0))
    return pl.pallas_call(
        body, grid=(T // tm,),
        in_specs=[dspec, wspec, hspec, hspec, wspec, wspec, dspec, gspec]
        + [pl.BlockSpec((8, 128), lambda i: (0, 0))] * len(toks)
        + [pl.BlockSpec((kn, D), lambda i: (0, 0), pipeline_mode=pl.Buffered(1))],
        out_specs=[hspec, hspec, dspec, dspec, gspec, pl.BlockSpec((tm, kn), lambda i: (i, 0)), dspec],
        out_shape=[jax.ShapeDtypeStruct((T, H), bf16)] * 2 + [jax.ShapeDtypeStruct((T, D), bf16),
                   jax.ShapeDtypeStruct((T, D), f32), jax.ShapeDtypeStruct((1, D), f32),
                   jax.ShapeDtypeStruct((T, kn), bf16), jax.ShapeDtypeStruct((T, D), bf16)],
        compiler_params=_cparams(("arbitrary",)), name=name,
    )(dx, wd, pg, pu, wtg, wtu, x, g, *toks, w_next)


def mm_tn2(a1, b1, a2, b2, *, name):
    T, K1 = a1.shape
    K2 = a2.shape[1]
    n1, n2 = b1.shape[1], b2.shape[1]
    last2 = K2 // MXU_COLS - 1

    def body(a1_ref, b1_ref, a2_ref, b2_ref, o1_ref, o2_ref):
        o1_ref[...] = _dot_tn(a1_ref[...], b1_ref[...]).astype(bf16)

        @pl.when(pl.program_id(0) <= last2)
        def _():
            o2_ref[...] = _dot_tn(a2_ref[...], b2_ref[...]).astype(bf16)

    o1, o2 = pl.pallas_call(
        body, grid=(K1 // MXU_COLS,),
        in_specs=[pl.BlockSpec((T, MXU_COLS), lambda d: (0, d)),
                  pl.BlockSpec((T, n1), lambda d: (0, 0), pipeline_mode=pl.Buffered(1)),
                  pl.BlockSpec((T, MXU_COLS), lambda d: (0, jnp.minimum(d, last2))),
                  pl.BlockSpec((T, n2), lambda d: (0, 0), pipeline_mode=pl.Buffered(1))],
        out_specs=[pl.BlockSpec((MXU_COLS, n1), lambda d: (d, 0)),
                   pl.BlockSpec((MXU_COLS, n2), lambda d: (jnp.minimum(d, last2), 0))],
        out_shape=[jax.ShapeDtypeStruct((K1, n1), bf16), jax.ShapeDtypeStruct((K2, n2), bf16)],
        compiler_params=_cparams(("arbitrary",)), name=name,
    )(a1, b1, a2, b2)
    return o1.reshape(NDEV, K1 // NDEV, n1), o2.reshape(NDEV, K2 // NDEV, n2)


def ffn_dw(dpg, dpu, a, h2, dxb, *, name):
    T, H = dpg.shape

    def body(g_ref, u_ref, a_ref, h_ref, x_ref, og_ref, ou_ref, od_ref):
        h = h_ref[...]
        og_ref[...] = _dot_tn(g_ref[...], h).astype(bf16)
        ou_ref[...] = _dot_tn(u_ref[...], h).astype(bf16)
        od_ref[...] = _dot_tn(a_ref[...], x_ref[...]).astype(bf16)

    blk = pl.BlockSpec((T, MXU_COLS), lambda d: (0, d))
    full = pl.BlockSpec((T, D), lambda d: (0, 0), pipeline_mode=pl.Buffered(1))
    out = pl.BlockSpec((MXU_COLS, D), lambda d: (d, 0))
    outs = pl.pallas_call(
        body, grid=(H // MXU_COLS,),
        in_specs=[blk, blk, blk, full, full], out_specs=[out] * 3,
        out_shape=[jax.ShapeDtypeStruct((H, D), bf16)] * 3,
        compiler_params=_cparams(("parallel",)), name=name,
    )(dpg, dpu, a, h2, dxb)
    return [o.reshape(NDEV, H // NDEV, D) for o in outs]


def _normbwd_tail(dh, x_ref, g_ref, tok_ref, dres_ref, dx_ref, dgn_ref):
    gv = g_ref[...] + tok_ref[0:1, 0:1]
    xv = x_ref[...]
    r = lax.rsqrt(jnp.mean(xv * xv, axis=-1, keepdims=True) + EPS)
    xhat = xv * r
    dgn_ref[...] += jnp.sum(dh * xhat, axis=0, keepdims=True)
    dxhat = dh * gv
    dx_ref[...] = dres_ref[...] + r * (dxhat - xhat * jnp.mean(dxhat * xhat, axis=-1, keepdims=True))


def _normbwd_specs(tm, w):
    dspec = pl.BlockSpec((tm, D), lambda i: (i, 0))
    gspec = pl.BlockSpec((1, D), lambda i: (0, 0))
    ins = [pl.BlockSpec(w.shape, lambda i: (0, 0), pipeline_mode=pl.Buffered(1)), dspec, gspec,
           pl.BlockSpec((8, 128), lambda i: (0, 0)), dspec]
    return ins, [dspec, gspec]


def _prev_halo_spec(tm, ncol):
    return pl.BlockSpec((HALO, ncol), lambda i: (jnp.maximum(i * (tm // HALO) - 1, 0), 0))


def _next_halo_spec(tm, ncol, T):
    return pl.BlockSpec((HALO, ncol), lambda i: (jnp.minimum((i + 1) * (tm // HALO), T // HALO - 1), 0))


def a_fwd(x, g, wt, gv, wm, bsb, *, tm, name):
    T = x.shape[0]

    def body(x_ref, g_ref, wt_ref, gv_ref, wm_ref, bs_ref, p_ref, h_ref, y_ref, vn_sc):
        xv = x_ref[...]
        r = lax.rsqrt(jnp.mean(xv * xv, axis=-1, keepdims=True) + EPS)
        h = (xv * r * g_ref[...]).astype(bf16)
        h_ref[...] = h
        p_ref[...] = _dot_nt(h, wt_ref[...]).astype(bf16)
        v = _gelu(p_ref[:, D:].astype(f32))
        vc = v - jnp.mean(v, axis=-1, keepdims=True)
        var = jnp.mean(vc * vc, axis=-1, keepdims=True)
        vn_sc[...] = (vc * lax.rsqrt(var + EPS) * gv_ref[...]).astype(bf16)
        for g in range(A_GROUPS):
            cs = slice(g * GBLK, (g + 1) * GBLK)
            for r in range(tm // GBLK):
                rs = slice(r * GBLK, (r + 1) * GBLK)
                sv = _dot(wm_ref[g], vn_sc[rs, cs]) + bs_ref[g]
                y_ref[rs, cs] = (_gelu(p_ref[rs, cs].astype(f32)) * sv).astype(bf16)

    dspec = pl.BlockSpec((tm, D), lambda i: (i, 0))
    gspec = pl.BlockSpec((1, D), lambda i: (0, 0))
    return pl.pallas_call(
        body, grid=(T // tm,),
        in_specs=[dspec, gspec, pl.BlockSpec(wt.shape, lambda i: (0, 0), pipeline_mode=pl.Buffered(1)), gspec,
                  pl.BlockSpec((A_GROUPS, GBLK, GBLK), lambda i: (0, 0, 0)),
                  pl.BlockSpec((A_GROUPS, GBLK, GBLK), lambda i: (0, 0, 0))],
        out_specs=[pl.BlockSpec((tm, 2 * D), lambda i: (i, 0)), dspec, dspec],
        out_shape=[jax.ShapeDtypeStruct((T, 2 * D), bf16), jax.ShapeDtypeStruct((T, D), bf16),
                   jax.ShapeDtypeStruct((T, D), bf16)],
        scratch_shapes=[pltpu.VMEM((tm, D), bf16)],
        compiler_params=_cparams(("parallel",)), name=name,
    )(x, g, wt, gv, wm, bsb)


def a_mid_bwd(p, dy, gv, wm, bsb, wt, x, g, tok, dres, *, tm, name):
    T = p.shape[0]

    def body(p_ref, dy_ref, gv_ref, wm_ref, bs_ref, wt_ref, x_ref, g_ref, tok_ref, dres_ref,
             dp_ref, dgv_ref, dwm_ref, dbs_ref, dx_ref, dgn_ref, vn_sc, dvn_sc):
        @pl.when(pl.program_id(0) == 0)
        def _():
            dgn_ref[...] = jnp.zeros(dgn_ref.shape, f32)
            dgv_ref[...] = jnp.zeros(dgv_ref.shape, f32)
            dwm_ref[...] = jnp.zeros(dwm_ref.shape, f32)
            dbs_ref[...] = jnp.zeros(dbs_ref.shape, f32)

        zv = p_ref[:, D:].astype(f32)
        cdf_v = _gelu_cdf(zv)
        v = zv * cdf_v
        vc = v - jnp.mean(v, axis=-1, keepdims=True)
        rstd = lax.rsqrt(jnp.mean(vc * vc, axis=-1, keepdims=True) + EPS)
        vhat = vc * rstd
        vn_sc[...] = (vhat * gv_ref[...]).astype(bf16)
        for g in range(A_GROUPS):
            cs = slice(g * GBLK, (g + 1) * GBLK)
            dwm = jnp.zeros((GBLK, GBLK), f32)
            dbs = jnp.zeros((GBLK, 1), f32)
            for r in range(tm // GBLK):
                rs = slice(r * GBLK, (r + 1) * GBLK)
                zu = p_ref[rs, cs].astype(f32)
                cdf_u = _gelu_cdf(zu)
                vn = vn_sc[rs, cs]
                sv = _dot(wm_ref[g], vn) + bs_ref[g]
                dyb = dy_ref[rs, cs].astype(f32)
                dsv = dyb * (zu * cdf_u)
                dsvb = dsv.astype(bf16)
                dp_ref[rs, cs] = (dyb * sv * _gelu_grad(zu, cdf_u)).astype(bf16)
                dwm += _dot_nt(dsvb, vn)
                dbs += jnp.sum(dsv, axis=1, keepdims=True)
                dvn_sc[rs, cs] = _dot_tn(wm_ref[g], dsvb)
            dwm_ref[g] += dwm
            dbs_ref[g] += dbs
        dvn = dvn_sc[...]
        dgv_ref[...] += jnp.sum(dvn * vhat, axis=0, keepdims=True)
        dvhat = dvn * gv_ref[...]
        dv = rstd * (dvhat - jnp.mean(dvhat, axis=-1, keepdims=True)
                     - vhat * jnp.mean(dvhat * vhat, axis=-1, keepdims=True))
        dp_ref[:, D:] = (dv * _gelu_grad(zv, cdf_v)).astype(bf16)
        _normbwd_tail(_dot(dp_ref[...], wt_ref[...]), x_ref, g_ref, tok_ref, dres_ref, dx_ref, dgn_ref)

    tail_in, tail_out = _normbwd_specs(tm, wt)
    return pl.pallas_call(
        body, grid=(T // tm,),
        in_specs=[pl.BlockSpec((tm, 2 * D), lambda i: (i, 0)), pl.BlockSpec((tm, D), lambda i: (i, 0)),
                  pl.BlockSpec((1, D), lambda i: (0, 0)),
                  pl.BlockSpec((A_GROUPS, GBLK, GBLK), lambda i: (0, 0, 0)),
                  pl.BlockSpec((A_GROUPS, GBLK, GBLK), lambda i: (0, 0, 0))] + tail_in,
        out_specs=[pl.BlockSpec((tm, 2 * D), lambda i: (i, 0)), pl.BlockSpec((1, D), lambda i: (0, 0)),
                   pl.BlockSpec((A_GROUPS, GBLK, GBLK), lambda i: (0, 0, 0)),
                   pl.BlockSpec((A_GROUPS, GBLK, 1), lambda i: (0, 0, 0))] + tail_out,
        out_shape=[jax.ShapeDtypeStruct((T, 2 * D), bf16), jax.ShapeDtypeStruct((1, D), f32),
                   jax.ShapeDtypeStruct((A_GROUPS, GBLK, GBLK), f32), jax.ShapeDtypeStruct((A_GROUPS, GBLK, 1), f32),
                   jax.ShapeDtypeStruct((T, D), f32), jax.ShapeDtypeStruct((1, D), f32)],
        scratch_shapes=[pltpu.VMEM((tm, D), bf16), pltpu.VMEM((tm, D), f32)],
        compiler_params=_cparams(("arbitrary",)), name=name,
    )(p, dy, gv, wm, bsb, wt, x, g, tok, dres)


def b_mid_fwd(p, cw, *, tm, S, name):
    T = p.shape[0]
    nst = S // tm

    def body(p_ref, ph_ref, cw_ref, y_ref, ext):
        first = (pl.program_id(0) % nst) == 0
        for c0 in range(0, D, COLS):
            cs, cc, cx = slice(c0, c0 + COLS), slice(D + c0, D + c0 + COLS), slice(2 * D + c0, 2 * D + c0 + COLS)
            q = p_ref[:, cc].astype(f32) * p_ref[:, cx].astype(f32)
            qh = ph_ref[:, cc].astype(f32) * ph_ref[:, cx].astype(f32)
            ext[0:HALO, cs] = jnp.where(first, 0.0, qh)
            ext[HALO:, cs] = q
            y = (cw_ref[0:1, cs] * ext[pl.ds(HALO - 2, tm), cs] + cw_ref[1:2, cs] * ext[pl.ds(HALO - 1, tm), cs]
                 + cw_ref[2:3, cs] * q)
            y_ref[:, cs] = (p_ref[:, cs].astype(f32) * y).astype(bf16)

    return pl.pallas_call(
        body, grid=(T // tm,),
        in_specs=[pl.BlockSpec((tm, 3 * D), lambda i: (i, 0)), _prev_halo_spec(tm, 3 * D),
                  pl.BlockSpec((3, D), lambda i: (0, 0))],
        out_specs=pl.BlockSpec((tm, D), lambda i: (i, 0)),
        out_shape=jax.ShapeDtypeStruct((T, D), bf16),
        scratch_shapes=[pltpu.VMEM((tm + HALO, D), f32)],
        compiler_params=_cparams(("parallel",)), name=name,
    )(p, p, cw)


def b_mid_bwd(p, dy, cw, wt, x, g, tok, dres, *, tm, S, name):
    T = p.shape[0]
    nst = S // tm

    def body(p_ref, ph_ref, pn_ref, dy_ref, dyn_ref, cw_ref, wt_ref, x_ref, g_ref, tok_ref, dres_ref,
             dp_ref, dcw_ref, dx_ref, dgn_ref, ext, ext2):
        i = pl.program_id(0)
        first = (i % nst) == 0
        last = (i % nst) == nst - 1

        @pl.when(i == 0)
        def _():
            dgn_ref[...] = jnp.zeros(dgn_ref.shape, f32)
            dcw_ref[...] = jnp.zeros(dcw_ref.shape, f32)

        for c0 in range(0, D, COLS):
            cs, cc, cx = slice(c0, c0 + COLS), slice(D + c0, D + c0 + COLS), slice(2 * D + c0, 2 * D + c0 + COLS)
            gb, gc, xt = p_ref[:, cs].astype(f32), p_ref[:, cc].astype(f32), p_ref[:, cx].astype(f32)
            q = gc * xt
            ext[0:HALO, cs] = jnp.where(first, 0.0, ph_ref[:, cc].astype(f32) * ph_ref[:, cx].astype(f32))
            ext[HALO:, cs] = q
            q2 = ext[pl.ds(HALO - 2, tm), cs]
            q1 = ext[pl.ds(HALO - 1, tm), cs]
            y = cw_ref[0:1, cs] * q2 + cw_ref[1:2, cs] * q1 + cw_ref[2:3, cs] * q
            dyo = dy_ref[:, cs].astype(f32)
            dp_ref[:, cs] = (dyo * y).astype(bf16)
            dyc = dyo * gb
            ext2[0:tm, cs] = dyc
            ext2[tm:, cs] = jnp.where(last, 0.0, dyn_ref[:, cs].astype(f32) * pn_ref[:, cs].astype(f32))
            dq = cw_ref[2:3, cs] * dyc + cw_ref[1:2, cs] * ext2[pl.ds(1, tm), cs] + cw_ref[0:1, cs] * ext2[pl.ds(2, tm), cs]
            dp_ref[:, cc] = (dq * xt).astype(bf16)
            dp_ref[:, cx] = (dq * gc).astype(bf16)
            dcw_ref[0:1, cs] += jnp.sum(dyc * q2, axis=0, keepdims=True)
            dcw_ref[1:2, cs] += jnp.sum(dyc * q1, axis=0, keepdims=True)
            dcw_ref[2:3, cs] += jnp.sum(dyc * q, axis=0, keepdims=True)
        _normbwd_tail(_dot(dp_ref[...], wt_ref[...]), x_ref, g_ref, tok_ref, dres_ref, dx_ref, dgn_ref)

    tail_in, tail_out = _normbwd_specs(tm, wt)
    return pl.pallas_call(
        body, grid=(T // tm,),
        in_specs=[pl.BlockSpec((tm, 3 * D), lambda i: (i, 0)), _prev_halo_spec(tm, 3 * D), _next_halo_spec(tm, 3 * D, T),
                  pl.BlockSpec((tm, D), lambda i: (i, 0)), _next_halo_spec(tm, D, T),
                  pl.BlockSpec((3, D), lambda i: (0, 0))] + tail_in,
        out_specs=[pl.BlockSpec((tm, 3 * D), lambda i: (i, 0)), pl.BlockSpec((3, D), lambda i: (0, 0))] + tail_out,
        out_shape=[jax.ShapeDtypeStruct((T, 3 * D), bf16), jax.ShapeDtypeStruct((3, D), f32),
                   jax.ShapeDtypeStruct((T, D), f32), jax.ShapeDtypeStruct((1, D), f32)],
        scratch_shapes=[pltpu.VMEM((tm + HALO, D), f32), pltpu.VMEM((tm + HALO, D), f32)],
        compiler_params=_cparams(("arbitrary",)), name=name,
    )(p, p, p, dy, dy, cw, wt, x, g, tok, dres)


def _pool_counts(i, nst, tm, rows, row0, w):
    t = (i % nst) * tm + row0 + lax.broadcasted_iota(jnp.int32, (rows, 1), 0)
    return jnp.minimum(t + 1, w).astype(f32)


def _pool_diff(p_ref, ext, g, i, nst, tm):
    w = POOL_WINDOWS[g]
    cs = slice(g * C_GDIM, (g + 1) * C_GDIM)
    pg = p_ref[:, cs].astype(f32)
    s = pg
    for k in range(1, w):
        s = s + ext[pl.ds(HALO - k, tm), cs]
    return s / _pool_counts(i, nst, tm, tm, 0, w) - pg


def c_mid_fwd(p, wg, scale, *, tm, S, name):
    T = p.shape[0]
    nst = S // tm

    def body(p_ref, ph_ref, wg_ref, sc_ref, y_ref, ext):
        i = pl.program_id(0)
        first = (i % nst) == 0
        ext[0:HALO, :] = jnp.where(first, 0.0, ph_ref[...].astype(f32))
        ext[HALO:, :] = p_ref[...].astype(f32)
        for g in range(C_GROUPS):
            cs = slice(g * C_GDIM, (g + 1) * C_GDIM)
            dg = _pool_diff(p_ref, ext, g, i, nst, tm).astype(bf16)
            wv = wg_ref[:, g].reshape(C_GDIM, C_GDIM)
            y_ref[:, cs] = (_dot(dg, wv) * sc_ref[:, cs]).astype(bf16)

    return pl.pallas_call(
        body, grid=(T // tm,),
        in_specs=[pl.BlockSpec((tm, D), lambda i: (i, 0)), _prev_halo_spec(tm, D),
                  pl.BlockSpec((NDEV, C_GROUPS, C_GDIM // NDEV, C_GDIM), lambda i: (0, 0, 0, 0)),
                  pl.BlockSpec((1, D), lambda i: (0, 0))],
        out_specs=pl.BlockSpec((tm, D), lambda i: (i, 0)),
        out_shape=jax.ShapeDtypeStruct((T, D), bf16),
        scratch_shapes=[pltpu.VMEM((tm + HALO, D), f32)],
        compiler_params=_cparams(("parallel",)), name=name,
    )(p, p, wg, scale)


def c_mid_bwd(p, dy, wg, scale, w, x, g, tok, dres, *, tm, S, name):
    T = p.shape[0]
    nst = S // tm

    def body(p_ref, ph_ref, dy_ref, dyn_ref, wg_ref, sc_ref, w_ref, x_ref, g_ref, tok_ref, dres_ref,
             dp_ref, dsc_ref, dwg_ref, dx_ref, dgn_ref, ext, ext2):
        i = pl.program_id(0)
        first = (i % nst) == 0
        last = (i % nst) == nst - 1

        @pl.when(i == 0)
        def _():
            dgn_ref[...] = jnp.zeros(dgn_ref.shape, f32)
            dsc_ref[...] = jnp.zeros(dsc_ref.shape, f32)
            dwg_ref[...] = jnp.zeros(dwg_ref.shape, f32)

        ext[0:HALO, :] = jnp.where(first, 0.0, ph_ref[...].astype(f32))
        ext[HALO:, :] = p_ref[...].astype(f32)
        for g in range(C_GROUPS):
            w = POOL_WINDOWS[g]
            cs = slice(g * C_GDIM, (g + 1) * C_GDIM)
            dg = _pool_diff(p_ref, ext, g, i, nst, tm).astype(bf16)
            wv = wg_ref[:, g].reshape(C_GDIM, C_GDIM)
            dyo = dy_ref[:, cs].astype(f32)
            dsc_ref[:, cs] += jnp.sum(dyo * _dot(dg, wv), axis=0, keepdims=True)
            dyp = (dyo * sc_ref[:, cs]).astype(bf16)
            dypn = (dyn_ref[:, cs].astype(f32) * sc_ref[:, cs]).astype(bf16)
            dwg_ref[g] += _dot_tn(dg, dyp)
            dd = _dot_nt(dyp, wv)
            ddn = _dot_nt(dypn, wv)
            ext2[0:tm, cs] = dd / _pool_counts(i, nst, tm, tm, 0, w)
            ext2[tm:, cs] = jnp.where(last, 0.0, ddn / _pool_counts(i, nst, tm, HALO, tm, w))
            s = -dd
            for k in range(w):
                s = s + ext2[pl.ds(k, tm), cs]
            dp_ref[:, cs] = s.astype(bf16)
        _normbwd_tail(_dot_nt(dp_ref[...], w_ref[...]), x_ref, g_ref, tok_ref, dres_ref, dx_ref, dgn_ref)

    tail_in, tail_out = _normbwd_specs(tm, w)
    return pl.pallas_call(
        body, grid=(T // tm,),
        in_specs=[pl.BlockSpec((tm, D), lambda i: (i, 0)), _prev_halo_spec(tm, D),
                  pl.BlockSpec((tm, D), lambda i: (i, 0)), _next_halo_spec(tm, D, T),
                  pl.BlockSpec((NDEV, C_GROUPS, C_GDIM // NDEV, C_GDIM), lambda i: (0, 0, 0, 0)),
                  pl.BlockSpec((1, D), lambda i: (0, 0))] + tail_in,
        out_specs=[pl.BlockSpec((tm, D), lambda i: (i, 0)), pl.BlockSpec((1, D), lambda i: (0, 0)),
                   pl.BlockSpec((C_GROUPS, C_GDIM, C_GDIM), lambda i: (0, 0, 0))] + tail_out,
        out_shape=[jax.ShapeDtypeStruct((T, D), bf16), jax.ShapeDtypeStruct((1, D), f32),
                   jax.ShapeDtypeStruct((C_GROUPS, C_GDIM, C_GDIM), f32),
                   jax.ShapeDtypeStruct((T, D), f32), jax.ShapeDtypeStruct((1, D), f32)],
        scratch_shapes=[pltpu.VMEM((tm + HALO, D), f32), pltpu.VMEM((tm + HALO, D), f32)],
        compiler_params=_cparams(("arbitrary",)), name=name,
    )(p, p, dy, dy, wg, scale, w, x, g, tok, dres)


def final_loss(x, g, tgt, *, tm, name):
    T = x.shape[0]

    def body(x_ref, g_ref, t_ref, loss_ref, dx_ref, dg_ref):
        @pl.when(pl.program_id(0) == 0)
        def _():
            loss_ref[...] = jnp.zeros(loss_ref.shape, f32)
            dg_ref[...] = jnp.zeros(dg_ref.shape, f32)

        xv = x_ref[...]
        r = lax.rsqrt(jnp.mean(xv * xv, axis=-1, keepdims=True) + EPS)
        xhat = xv * r
        err = xhat * g_ref[...] - t_ref[...]
        loss_ref[...] += 0.5 * jnp.sum(jnp.mean(err * err, axis=-1, keepdims=True))
        dy = err * (1.0 / D)
        dg_ref[...] += jnp.sum(dy * xhat, axis=0, keepdims=True)
        dxhat = dy * g_ref[...]
        dx_ref[...] = r * (dxhat - xhat * jnp.mean(dxhat * xhat, axis=-1, keepdims=True))

    return pl.pallas_call(
        body, grid=(T // tm,),
        in_specs=[pl.BlockSpec((tm, D), lambda i: (i, 0)), pl.BlockSpec((1, D), lambda i: (0, 0)),
                  pl.BlockSpec((tm, D), lambda i: (i, 0))],
        out_specs=[pl.BlockSpec((8, 128), lambda i: (0, 0)), pl.BlockSpec((tm, D), lambda i: (i, 0)),
                   pl.BlockSpec((1, D), lambda i: (0, 0))],
        out_shape=[jax.ShapeDtypeStruct((8, 128), f32), jax.ShapeDtypeStruct((T, D), f32),
                   jax.ShapeDtypeStruct((1, D), f32)],
        compiler_params=_cparams(("arbitrary",)), name=name,
    )(x, g, tgt)


def _slot(px, py, pc):
    return 4 * px + 2 * py + pc


def _with_own_block(s, me):
    zone = lax.empty((NDEV,) + s.shape, s.dtype)
    return lax.dynamic_update_slice(zone, s[None], (me,) + (0,) * s.ndim)


def all_gather(arrs, me, *, name):
    n = len(arrs)

    def body(*refs):
        ins, outs = refs[:n], refs[2 * n:3 * n]
        send_sems, recv_sems = refs[3 * n:]
        x, y, c = lax.axis_index("x"), lax.axis_index("y"), lax.axis_index("c")
        me, sibling = (x, y, c), (x, y, 1 - c)
        chips = [(1 - x, y), (x, 1 - y), (1 - x, 1 - y)]

        def copy(a, k, block, to, src=None):
            dst = outs[a].at[_slot(*block)]
            return pltpu.make_async_remote_copy(
                src_ref=dst if src is None else src, dst_ref=dst,
                send_sem=send_sems.at[a, k], recv_sem=recv_sems.at[a, k], device_id=to, device_id_type=MESH)

        first = []
        for a in range(n):
            first.append(copy(a, 0, me, sibling, src=ins[a]))
            first += [copy(a, 1 + j, me, (*chip, c), src=ins[a]) for j, chip in enumerate(chips)]
        for cp in first:
            cp.start()
        passed = []
        for j, chip in enumerate(chips):
            for a in range(n):
                copy(a, 1 + j, (*chip, c), me).wait_recv()
                fwd = copy(a, 4 + j, (*chip, c), sibling)
                fwd.start()
                passed.append(fwd)
        for a in range(n):
            copy(a, 0, sibling, me).wait_recv()
        for j, chip in enumerate(chips):
            for a in range(n):
                copy(a, 4 + j, (*chip, 1 - c), me).wait_recv()
        for cp in first + passed:
            cp.wait_send()

    any_spec = pl.BlockSpec(memory_space=pl.ANY)
    return pl.pallas_call(
        body,
        in_specs=[any_spec] * (2 * n), out_specs=[any_spec] * n,
        out_shape=[jax.ShapeDtypeStruct((NDEV,) + a.shape, a.dtype) for a in arrs],
        input_output_aliases={n + i: i for i in range(n)},
        scratch_shapes=[pltpu.SemaphoreType.DMA((n, 7)), pltpu.SemaphoreType.DMA((n, 7))],
        compiler_params=pltpu.CompilerParams(has_side_effects=True), name=name,
    )(*arrs, *[_with_own_block(a, me) for a in arrs])


HBM_SPEC = pl.BlockSpec(memory_space=pltpu.HBM)
SEM_SPEC = pl.BlockSpec(memory_space=pltpu.SEMAPHORE)
ANY_SPEC = pl.BlockSpec(memory_space=pl.ANY)
TOKEN_SHAPE = jax.ShapeDtypeStruct((8, 128), f32)
DATAFLOW_EFFECT = pltpu.SideEffectType.DATAFLOW_SIDE_EFFECTING


def _in_hbm(a):
    return pltpu.with_memory_space_constraint(a, pltpu.HBM)


def _hbm_like(a):
    return pltpu.HBM(a.shape, a.dtype)


def _mesh_pos():
    return lax.axis_index("x"), lax.axis_index("y"), lax.axis_index("c")


def _gather_targets(x, y, c):
    return [(x, y, 1 - c), (1 - x, y, c), (x, 1 - y, c), (1 - x, 1 - y, c)]


def gather_start(shards, me, after, *, name):
    n = len(shards)
    extra = [] if after is None else [after]

    def body(*refs):
        srcs, lands = refs[:n], refs[n:2 * n]
        send_sems, recv_sems = refs[2 * n + len(extra)], refs[2 * n + len(extra) + 1]
        token = refs[-1]
        x, y, c = _mesh_pos()
        me = _slot(x, y, c)
        for a in range(n):
            for k, to in enumerate(_gather_targets(x, y, c)):
                pltpu.make_async_remote_copy(
                    src_ref=srcs[a], dst_ref=lands[a].at[me], send_sem=send_sems.at[4 * a + k], recv_sem=recv_sems.at[4 * a + k],
                    device_id=to, device_id_type=MESH).start()
        token[...] = jnp.zeros(token.shape, f32)

    lands = [_with_own_block(s, me) for s in shards]
    sems = pltpu.SemaphoreType.DMA((4 * n,))
    out = pl.pallas_call(
        body, name=name,
        in_specs=[HBM_SPEC] * (2 * n) + [ANY_SPEC] * len(extra),
        out_specs=[SEM_SPEC, SEM_SPEC] + [HBM_SPEC] * (2 * n) + [pl.BlockSpec(memory_space=pltpu.VMEM)],
        out_shape=[sems, sems] + [_hbm_like(s) for s in shards] + [_hbm_like(l) for l in lands] + [TOKEN_SHAPE],
        input_output_aliases={i: 2 + i for i in range(2 * n)},
        compiler_params=pltpu.CompilerParams(has_side_effects=DATAFLOW_EFFECT),
    )(*[_in_hbm(s) for s in shards], *[_in_hbm(l) for l in lands], *extra)
    return out[0], out[1], out[2:2 + n], out[2 + n:2 + 2 * n], out[-1]


def gather_wait(send_sems, recv_sems, shards, lands, after, *, name):
    n = len(shards)
    after = list(after) if isinstance(after, (list, tuple)) else [after]

    def body(*refs):
        srcs, lands_in = refs[:n], refs[n:2 * n]
        send_sems, recv_sems = refs[2 * n], refs[2 * n + 1]
        x, y, c = _mesh_pos()
        for a in range(n):
            for k, frm in enumerate(_gather_targets(x, y, c)):
                cp = pltpu.make_async_remote_copy(
                    src_ref=srcs[a], dst_ref=lands_in[a].at[_slot(*frm)], send_sem=send_sems.at[4 * a + k],
                    recv_sem=recv_sems.at[4 * a + k], device_id=frm, device_id_type=MESH)
                cp.wait_send()
                cp.wait_recv()

    out = pl.pallas_call(
        body, name=name,
        in_specs=[HBM_SPEC] * (2 * n) + [SEM_SPEC, SEM_SPEC] + [ANY_SPEC] * len(after),
        out_specs=[HBM_SPEC] * (2 * n),
        out_shape=[_hbm_like(s) for s in shards] + [_hbm_like(l) for l in lands],
        input_output_aliases={i: i for i in range(2 * n)},
        compiler_params=pltpu.CompilerParams(has_side_effects=DATAFLOW_EFFECT),
    )(*shards, *lands, send_sems, recv_sems, *after)
    return out[:n], out[n:]


def gather_finish(lands, *, name):
    n = len(lands)

    def body(*refs):
        lands_in, lands_out = refs[:n], refs[n:2 * n]
        send_sems, recv_sems = refs[2 * n:]
        x, y, c = _mesh_pos()
        sibling = (x, y, 1 - c)
        chips = [(1 - x, y), (x, 1 - y), (1 - x, 1 - y)]
        sent = []
        for a in range(n):
            for j, chip in enumerate(chips):
                s = _slot(*chip, c)
                cp = pltpu.make_async_remote_copy(
                    src_ref=lands_in[a].at[s], dst_ref=lands_out[a].at[s], send_sem=send_sems.at[a, j],
                    recv_sem=recv_sems.at[a, j], device_id=sibling, device_id_type=MESH)
                cp.start()
                sent.append(cp)
        for a in range(n):
            for j, chip in enumerate(chips):
                s = _slot(*chip, 1 - c)
                pltpu.make_async_remote_copy(
                    src_ref=lands_in[a].at[s], dst_ref=lands_out[a].at[s], send_sem=send_sems.at[a, j],
                    recv_sem=recv_sems.at[a, j], device_id=sibling, device_id_type=MESH).wait_recv()
        for cp in sent:
            cp.wait_send()

    return pl.pallas_call(
        body, name=name,
        in_specs=[ANY_SPEC] * n, out_specs=[ANY_SPEC] * n,
        out_shape=[jax.ShapeDtypeStruct(l.shape, l.dtype) for l in lands],
        input_output_aliases={i: i for i in range(n)},
        scratch_shapes=[pltpu.SemaphoreType.DMA((n, 3)), pltpu.SemaphoreType.DMA((n, 3))],
        compiler_params=pltpu.CompilerParams(has_side_effects=True),
    )(*lands)


def _peer(x, y, c, r):
    return (x ^ ((r >> 2) & 1), y ^ ((r >> 1) & 1), c ^ (r & 1))


def exchange_start(items, lands, after, *, name):
    n, m = len(items), len(lands)
    extra = [] if after is None else [after]

    def body(*refs):
        gs, zones = refs[:n], refs[n:n + m]
        send_sems, recv_sems = refs[n + m + len(extra)], refs[n + m + len(extra) + 1]
        token = refs[-1]
        x, y, c = _mesh_pos()
        me = _slot(x, y, c)
        for r in (1, 2, 4, 3, 5, 6, 7):
            to = _peer(x, y, c, r)
            for a, (_, zi, l) in enumerate(items):
                pltpu.make_async_remote_copy(
                    src_ref=gs[a].at[_slot(*to)], dst_ref=zones[zi].at[me, l], send_sem=send_sems.at[7 * a + r - 1],
                    recv_sem=recv_sems.at[7 * a + r - 1], device_id=to, device_id_type=MESH).start()
        token[...] = jnp.zeros(token.shape, f32)

    gs = [g for g, _, _ in items]
    sems = pltpu.SemaphoreType.DMA((7 * n,))
    out = pl.pallas_call(
        body, name=name,
        in_specs=[HBM_SPEC] * (n + m) + [ANY_SPEC] * len(extra),
        out_specs=[SEM_SPEC, SEM_SPEC] + [HBM_SPEC] * (n + m) + [pl.BlockSpec(memory_space=pltpu.VMEM)],
        out_shape=[sems, sems] + [_hbm_like(g) for g in gs] + [_hbm_like(z) for z in lands] + [TOKEN_SHAPE],
        input_output_aliases={i: 2 + i for i in range(n + m)},
        compiler_params=pltpu.CompilerParams(has_side_effects=DATAFLOW_EFFECT),
    )(*[_in_hbm(g) for g in gs], *[_in_hbm(z) for z in lands], *extra)
    return out[0], out[1], out[2:2 + n], out[2 + n:2 + n + m], out[-1]


def exchange_wait(parts, lands, after, *, name):
    m = len(lands)
    flat_gs = [g for _, _, _, gs in parts for g in gs]
    ng = len(flat_gs)
    after = list(after) if isinstance(after, (list, tuple)) else [after]

    def body(*refs):
        gs, zones = refs[:ng], refs[ng:ng + m]
        sem_refs = refs[ng + m:ng + m + 2 * len(parts)]
        x, y, c = _mesh_pos()
        base = 0
        for pi, (items, _, _, _) in enumerate(parts):
            send_sems, recv_sems = sem_refs[2 * pi], sem_refs[2 * pi + 1]
            for r in range(1, NDEV):
                frm = _peer(x, y, c, r)
                for a, (_, zi, l) in enumerate(items):
                    cp = pltpu.make_async_remote_copy(
                        src_ref=gs[base + a].at[_slot(*frm)], dst_ref=zones[zi].at[_slot(*frm), l],
                        send_sem=send_sems.at[7 * a + r - 1], recv_sem=recv_sems.at[7 * a + r - 1],
                        device_id=frm, device_id_type=MESH)
                    cp.wait_send()
                    cp.wait_recv()
            base += len(items)

    sem_args = [s for _, ss, rs, _ in parts for s in (ss, rs)]
    out = pl.pallas_call(
        body, name=name,
        in_specs=[HBM_SPEC] * (ng + m) + [SEM_SPEC] * len(sem_args) + [ANY_SPEC] * len(after),
        out_specs=[HBM_SPEC] * (ng + m),
        out_shape=[_hbm_like(g) for g in flat_gs] + [_hbm_like(z) for z in lands],
        input_output_aliases={i: i for i in range(ng + m)},
        compiler_params=pltpu.CompilerParams(has_side_effects=DATAFLOW_EFFECT),
    )(*flat_gs, *lands, *sem_args, *after)
    return out[ng:]


def sum8(g, *, name):
    _, R, C = g.shape
    tr = R

    def body(g_ref, o_ref):
        s = g_ref[0]
        for k in range(1, NDEV):
            s = s + g_ref[k]
        o_ref[...] = s

    return pl.pallas_call(
        body, grid=(R // tr,),
        in_specs=[pl.BlockSpec((NDEV, tr, C), lambda i: (0, i, 0))],
        out_specs=pl.BlockSpec((tr, C), lambda i: (i, 0)),
        out_shape=jax.ShapeDtypeStruct((R, C), f32),
        compiler_params=_cparams(("parallel",)), name=name,
    )(g)


def _adam_math(w, g, m, v):
    m = ADAM_B1 * m + (1.0 - ADAM_B1) * g
    v = ADAM_B2 * v + (1.0 - ADAM_B2) * (g * g)
    m_hat = m / (1.0 - ADAM_B1 ** ADAM_STEP)
    v_hat = v / (1.0 - ADAM_B2 ** ADAM_STEP)
    delta = -ADAM_LR * (m_hat / (jnp.sqrt(v_hat) + ADAM_EPS) + ADAM_WD * w)
    return delta, m, v


def adamw_slices(w, m, v, recv, *, name):
    return adamw_families([w], [m], [v], [recv], tc=w.shape[2], name=name)


def adamw_families(ws, ms, vs, recvs, *, tc, l0=0, nl=None, prev=None, name):
    nf = len(ws)
    L, R, C = ws[0].shape
    nl = L - l0 if nl is None else nl
    prevs = [] if prev is None else list(prev)

    def body(*refs):
        outs = refs[4 * nf + len(prevs):]
        for q in range(nf):
            w_ref, m_ref, v_ref, r_ref = refs[q], refs[nf + q], refs[2 * nf + q], refs[3 * nf + q]
            g = r_ref[0].astype(f32)
            for k in range(1, NDEV):
                g = g + r_ref[k].astype(f32)
            delta, nm, nv = _adam_math(w_ref[...], g, m_ref[...], v_ref[...])
            outs[4 * q][...] = g
            outs[4 * q + 1][...] = delta
            outs[4 * q + 2][...] = nm
            outs[4 * q + 3][...] = nv

    wspec = pl.BlockSpec((None, R, tc), lambda l, j: (l0 + l, 0, j))
    rspec = pl.BlockSpec((NDEV, None, R, tc), lambda l, j: (0, l0 + l, 0, j))
    return pl.pallas_call(
        body, grid=(nl, C // tc),
        in_specs=[wspec] * (3 * nf) + [rspec] * nf + [pl.BlockSpec(memory_space=pl.ANY)] * len(prevs),
        out_specs=[wspec] * (4 * nf), out_shape=[jax.ShapeDtypeStruct((L, R, C), f32)] * (4 * nf),
        input_output_aliases={4 * nf + q: q for q in range(len(prevs))},
        compiler_params=_cparams(("parallel", "parallel")), name=name,
    )(*ws, *ms, *vs, *recvs, *prevs)


def adamw_slices_t(w, m, v, recv, *, name):
    L, R, C = w.shape

    def body(w_ref, m_ref, v_ref, r_ref, g_ref, d_ref, nm_ref, nv_ref):
        gt = r_ref[0].astype(f32)
        for k in range(1, NDEV):
            gt = gt + r_ref[k].astype(f32)
        g = gt.T
        delta, nm, nv = _adam_math(w_ref[...], g, m_ref[...], v_ref[...])
        g_ref[...] = g
        d_ref[...] = delta
        nm_ref[...] = nm
        nv_ref[...] = nv

    wspec = pl.BlockSpec((None, R, C), lambda l: (l, 0, 0))
    return pl.pallas_call(
        body, grid=(L,),
        in_specs=[wspec, wspec, wspec, pl.BlockSpec((NDEV, None, C, R), lambda l: (0, l, 0, 0))],
        out_specs=[wspec] * 4, out_shape=[jax.ShapeDtypeStruct((L, R, C), f32)] * 4,
        compiler_params=_cparams(("parallel",)), name=name,
    )(w, m, v, recv)


def adamw_plain(w, m, v, g, *, name):
    R, C = w.shape

    def body(w_ref, m_ref, v_ref, g_ref, d_ref, nm_ref, nv_ref):
        delta, nm, nv = _adam_math(w_ref[...], g_ref[...], m_ref[...], v_ref[...])
        d_ref[...] = delta
        nm_ref[...] = nm
        nv_ref[...] = nv

    return pl.pallas_call(
        body, out_shape=[jax.ShapeDtypeStruct((R, C), f32)] * 3, name=name,
    )(w, m, v, g)


def _chunk_causal_mask():
    pos = jnp.arange(GBLK)
    return (pos[None, :] // CHUNK) <= (pos[:, None] // CHUNK)


def kernel(x, norm_mix_g, norm_ffn_g, final_norm_g, a_w_in, a_v_norm_g, a_w_s, a_b_s, a_w_out, b_w_in, b_conv_w, b_w_out, c_w_in, c_w_grp, c_scale, c_w_out, f_w_gate, f_w_up, f_w_down, loss_target, m_norm_mix_g, m_norm_ffn_g, m_final_norm_g, m_a_w_in, m_a_v_norm_g, m_a_w_s, m_a_b_s, m_a_w_out, m_b_w_in, m_b_conv_w, m_b_w_out, m_c_w_in, m_c_w_grp, m_c_scale, m_c_w_out, m_f_w_gate, m_f_w_up, m_f_w_down, v_norm_mix_g, v_norm_ffn_g, v_final_norm_g, v_a_w_in, v_a_v_norm_g, v_a_w_s, v_a_b_s, v_a_w_out, v_b_w_in, v_b_conv_w, v_b_w_out, v_c_w_in, v_c_w_grp, v_c_scale, v_c_w_out, v_f_w_gate, v_f_w_up, v_f_w_down):
    B, S, _ = x.shape
    T = B * S
    tm = min(512, S)
    tms = min(512, T)
    tma = min(256, S)
    tmf = min(256, T)
    me = _slot(lax.axis_index("x"), lax.axis_index("y"), lax.axis_index("c"))

    def tb(w):
        return jnp.transpose(w, (0, 2, 1)).astype(bf16)

    small = jnp.concatenate([a_v_norm_g, b_conv_w[0], c_scale, jnp.zeros((2, GBLK), f32)], axis=0)
    started = [gather_start([tb(a_w_in[0:1])[0], a_w_out[0].astype(bf16), small], me, None, name="gather_start_0")]
    tok = started[0][4]
    z = tok[0, 0]
    f_g, f_u, f_d = tb(f_w_gate + z), tb(f_w_up + z), (f_w_down + z).astype(bf16)
    stage_shards = [
        None,
        [f_g[0], f_u[0], f_d[0]],
        [tb(b_w_in + z)[0], (b_w_out[0] + z).astype(bf16), f_g[1], f_u[1], f_d[1]],
        [(c_w_in[0] + z).astype(bf16), (c_w_grp[0] + z).astype(bf16), (c_w_out[0] + z).astype(bf16), f_g[2], f_u[2], f_d[2]],
        [tb(a_w_in[1:2] + z)[0], (a_w_out[1] + z).astype(bf16), f_g[3], f_u[3], f_d[3]],
    ]
    for s in range(1, len(stage_shards)):
        started.append(gather_start(stage_shards[s], me, tok, name=f"gather_start_{s}"))
        tok = started[-1][4]

    def stage_weights(s, after):
        send_sems, recv_sems, shards, lands, _ = started[s]
        _, lands = gather_wait(send_sems, recv_sems, shards, lands, after, name=f"gather_wait_{s}")
        lands = gather_finish(lands, name=f"gather_finish_{s}")
        return [t.reshape(-1, t.shape[-1]) if t.ndim == 3 else t for t in lands]

    wm_masked = jnp.where(_chunk_causal_mask()[None, None], a_w_s, 0.0).astype(bf16)
    bsb = jnp.broadcast_to(a_b_s[:, :, :, None], a_b_s.shape + (GBLK,))

    xc = x.reshape(T, D)
    saved, weights = [], []
    for i in range(DEPTH):
        kind, j = i % 3, i // 3
        gmix = norm_mix_g[i][None]
        if i == 0:
            w_in, w_out, small_g = stage_weights(0, tok)
            small_full = jnp.transpose(small_g.reshape(NDEV, 8, GBLK), (1, 0, 2)).reshape(8, D)
            gv_full, cw_full, cs_full = small_full[0:2], small_full[2:5], small_full[5:6]
            w = [w_in, w_out]
        else:
            w = stage_weights(i + 1, xc)
        if kind == 0:
            w_in = w[0]
            p, h, y = a_fwd(xc, gmix, w_in, gv_full[j][None], wm_masked[j], bsb[j], tm=tma, name=f"a_fwd_{i}")
        elif kind == 1:
            w_in, w_out = w[:2]
            p, h = norm_mm(xc, gmix, w_in, tm=tms, transposed=True, name=f"b_in_{i}")
            y = b_mid_fwd(p, cw_full, tm=tm, S=S, name=f"b_mid_{i}")
        else:
            w_in, w_grp, w_out = w[:3]
            p, h = norm_mm(xc, gmix, w_in, tm=tms, transposed=False, name=f"c_in_{i}")
            y = c_mid_fwd(p, w_grp, cs_full, tm=tm, S=S, name=f"c_mid_{i}")
        if i == 0:
            w = w + stage_weights(1, y)
        weights.append(w)
        w_out, w_fg, w_fu, w_fd = w[-4:]
        x1, x2, pg, pu, a, h2 = ffn_fwd(y, w_out, xc, norm_ffn_g[i][None], w_fg, w_fu, w_fd, tm=tmf, name=f"f_fwd_{i}")
        saved.append((xc, p, h, y, x1, pg, pu, h2, a))
        xc = x2

    loss_blk, dx, d_final = final_loss(xc, final_norm_g[None], loss_target.reshape(T, D), tm=tms, name="final_loss")
    loss = lax.psum(loss_blk[0, 0], AXES)

    zone_shapes = [(2, 256, D), (2, GBLK, D), (1, 384, D), (1, GBLK, D), (1, GBLK, D), (1, GBLK, C_GDIM), (1, GBLK, D),
                   (DEPTH, F_LOC, D), (DEPTH, F_LOC, D), (DEPTH, F_LOC, D)]
    Z_AIN, Z_AOUT, Z_BIN, Z_BOUT, Z_CIN, Z_CGRP, Z_COUT, Z_FG, Z_FU, Z_FD = range(10)
    lands = [lax.empty((NDEV,) + s, bf16) for s in zone_shapes]
    parts = []

    def exchange(items, after, name):
        nonlocal lands
        for g, zi, l in items:
            own = lax.dynamic_index_in_dim(g, me, axis=0, keepdims=True)[None]
            lands[zi] = lax.dynamic_update_slice(lands[zi], own, (me, l) + (0,) * (g.ndim - 1))
        idx = sorted({zi for _, zi, _ in items})
        local = [(g, idx.index(zi), l) for g, zi, l in items]
        send_sems, recv_sems, gs, zs, token = exchange_start(local, [lands[zi] for zi in idx], after, name=name)
        for q, zi in enumerate(idx):
            lands[zi] = zs[q]
        parts.append(([(None, zi, l) for _, zi, l in items], send_sems, recv_sems, gs))
        return token

    d_mix, d_ffn = [None] * DEPTH, [None] * DEPTH
    d_gv, d_wm, d_bs = [None] * 2, [None] * 2, [None] * 2
    tok = None
    for i in reversed(range(DEPTH)):
        kind, j = i % 3, i // 3
        xin, p, h, y, x1, pg, pu, h2, a = saved[i]
        w = weights[i]
        w_in, w_out = w[0], w[-4]
        w_fg, w_fu, w_fd = w[-3:]
        dpg, dpu, dxb, dx1, d_ffn[i], dy, dx1b = ffn_bwd(dx, w_fd, pg, pu, w_fg, w_fu, x1, norm_ffn_g[i][None], tok, w_out,
                                                         tm=tmf, name=f"f_bwd_{i}")
        gw_g, gw_u, gw_d = ffn_dw(dpg, dpu, a, h2, dxb, name=f"f_dw_{i}")
        tok = exchange([(gw_g, Z_FG, i), (gw_u, Z_FU, i), (gw_d, Z_FD, i)], tok, f"exchange_start_ffn_{i}")
        gmix = norm_mix_g[i][None]
        if kind == 0:
            dp, d_gv[j], d_wm[j], d_bs[j], dx, d_mix[i] = a_mid_bwd(
                p, dy, gv_full[j][None], wm_masked[j], bsb[j], w_in, xin, gmix, tok, dx1, tm=tma, name=f"a_bwd_{i}")
            gw_in, gw_out = mm_tn2(dp, h, y, dx1b, name=f"a_dw_{i}")
            items = [(gw_in, Z_AIN, j), (gw_out, Z_AOUT, j)]
        elif kind == 1:
            dp, d_cw, dx, d_mix[i] = b_mid_bwd(p, dy, cw_full, w_in, xin, gmix, tok, dx1, tm=tm, S=S, name=f"b_bwd_{i}")
            gw_in, gw_out = mm_tn2(dp, h, y, dx1b, name=f"b_dw_{i}")
            items = [(gw_in, Z_BIN, 0), (gw_out, Z_BOUT, 0)]
        else:
            dp, d_cs, d_wgrp, dx, d_mix[i] = c_mid_bwd(p, dy, w[1], cs_full, w_in, xin, gmix, tok, dx1, tm=tm, S=S,
                                                       name=f"c_bwd_{i}")
            gw_cgrp = jnp.transpose(d_wgrp.reshape(C_GROUPS, NDEV, C_GDIM // NDEV, C_GDIM), (1, 0, 2, 3)).astype(bf16)
            gw_in, gw_out = mm_tn2(h, dp, y, dx1b, name=f"c_dw_{i}")
            items = [(gw_in, Z_CIN, 0), (gw_cgrp.reshape(NDEV, GBLK, C_GDIM), Z_CGRP, 0), (gw_out, Z_COUT, 0)]
        tok = exchange(items, tok, f"exchange_start_mix_{i}")
    grad_x = dx.reshape(B, S, D)

    small_parts = [
        jnp.concatenate(d_mix, axis=0).reshape(-1, GBLK),
        jnp.concatenate(d_ffn, axis=0).reshape(-1, GBLK),
        d_final.reshape(-1, GBLK),
        jnp.concatenate(d_gv, axis=0).reshape(-1, GBLK),
        jnp.where(_chunk_causal_mask()[None, None], jnp.stack(d_wm), 0.0).reshape(-1, GBLK),
        jnp.stack(d_bs).reshape(-1, GBLK),
        d_cw.reshape(-1, GBLK),
        d_cs.reshape(-1, GBLK),
    ]
    small_rows = [q.shape[0] for q in small_parts]
    sg_send, sg_recv, sg_shards, sg_lands, _ = gather_start(
        [jnp.concatenate(small_parts, axis=0)], me, tok, name="gather_small_start")

    def plain(w, m, v, g, name):
        shp = w.shape
        w2, m2, v2, g2 = (t.reshape(-1, shp[-1]) for t in (w, m, v, g))
        return tuple(t.reshape(shp) for t in adamw_plain(w2, m2, v2, g2, name=name))

    def sliced(w, m, v, recv, name):
        shp = w.shape
        w3, m3, v3 = (t.reshape((shp[0], -1, shp[-1])) for t in (w, m, v))
        return tuple(t.reshape(shp) for t in adamw_slices(w3, m3, v3, recv, name=name))

    lands = list(exchange_wait(parts[:6], lands, dx, name="exchange_wait"))
    _, _, r_bin, r_bout, r_cin, r_cgrp, r_cout, r_fg, r_fu, r_fd = lands
    res = {}
    res["b_w_in"] = adamw_slices_t(b_w_in, m_b_w_in, v_b_w_in, r_bin, name="adam_b_w_in")
    res["b_w_out"] = sliced(b_w_out, m_b_w_out, v_b_w_out, r_bout, "adam_b_w_out")
    res["c_w_in"] = sliced(c_w_in, m_c_w_in, v_c_w_in, r_cin, "adam_c_w_in")
    res["c_w_grp"] = sliced(c_w_grp, m_c_w_grp, v_c_w_grp, r_cgrp, "adam_c_w_grp")
    res["c_w_out"] = sliced(c_w_out, m_c_w_out, v_c_w_out, r_cout, "adam_c_w_out")
    def tr(t):
        return jnp.transpose(t, (0, 2, 1))

    f_ws = [tr(f_w_gate), tr(f_w_up), f_w_down]
    f_ms = [tr(m_f_w_gate), tr(m_f_w_up), m_f_w_down]
    f_vs = [tr(v_f_w_gate), tr(v_f_w_up), v_f_w_down]
    ff = adamw_families(f_ws, f_ms, f_vs, [r_fg, r_fu, r_fd], tc=256, l0=1, name="adam_ffn_123")
    big = [res[n][1] for n in ("b_w_in", "b_w_out", "c_w_in", "c_w_grp", "c_w_out")] + [ff[1], ff[5], ff[9]]

    ffn0_items, ffn0_send, ffn0_recv, ffn0_gs = parts[6]
    ffn0_zones = [Z_FG, Z_FU, Z_FD]
    ffn0_part = ([(None, ffn0_zones.index(zi), l) for _, zi, l in ffn0_items], ffn0_send, ffn0_recv, ffn0_gs)
    r_fg, r_fu, r_fd = exchange_wait([ffn0_part], [lands[zi] for zi in ffn0_zones], big, name="exchange_wait_ffn0")
    ff = adamw_families(f_ws, f_ms, f_vs, [r_fg, r_fu, r_fd], tc=256, l0=0, nl=1, prev=ff, name="adam_ffn_0")
    res["f_w_gate"], res["f_w_up"], res["f_w_down"] = tuple(tr(t) for t in ff[0:4]), tuple(tr(t) for t in ff[4:8]), tuple(ff[8:12])

    last_items, last_send, last_recv, last_gs = parts[7]
    last_zones = [Z_AIN, Z_AOUT]
    last_part = ([(None, last_zones.index(zi), l) for _, zi, l in last_items], last_send, last_recv, last_gs)
    r_ain, r_aout = exchange_wait([last_part], [lands[zi] for zi in last_zones], [ff[1], ff[5], ff[9]], name="exchange_wait_last")
    res["a_w_in"] = adamw_slices_t(a_w_in, m_a_w_in, v_a_w_in, r_ain, name="adam_a_w_in")
    res["a_w_out"] = sliced(a_w_out, m_a_w_out, v_a_w_out, r_aout, "adam_a_w_out")

    _, sg_lands = gather_wait(sg_send, sg_recv, sg_shards, sg_lands, [res["a_w_in"][1], res["a_w_out"][1]], name="gather_small_wait")
    (gs_all,) = gather_finish(sg_lands, name="gather_small_finish")
    gs = sum8(gs_all, name="sum_small_grads")
    offs = [0]
    for r in small_rows:
        offs.append(offs[-1] + r)
    sp = [gs[offs[q]:offs[q + 1]] for q in range(len(small_parts))]
    grad_norm_mix_g = sp[0].reshape(DEPTH, D)
    grad_norm_ffn_g = sp[1].reshape(DEPTH, D)
    grad_final_norm_g = sp[2].reshape(D)
    grad_a_w_s = sp[4].reshape(a_w_s.shape)
    grad_a_b_s = sp[5].reshape(a_b_s.shape)

    def my_cols(full):
        return lax.dynamic_slice_in_dim(full, me * GBLK, GBLK, axis=1)

    grad_a_v_norm_g = my_cols(sp[3].reshape(2, D))
    grad_b_conv_w = my_cols(sp[6].reshape(3, D))[None]
    grad_c_scale = my_cols(sp[7].reshape(1, D))
    res["norm_mix_g"] = (grad_norm_mix_g,) + plain(norm_mix_g, m_norm_mix_g, v_norm_mix_g, grad_norm_mix_g, "adam_norm_mix")
    res["norm_ffn_g"] = (grad_norm_ffn_g,) + plain(norm_ffn_g, m_norm_ffn_g, v_norm_ffn_g, grad_norm_ffn_g, "adam_norm_ffn")
    res["final_norm_g"] = (grad_final_norm_g,) + tuple(
        t.reshape(D) for t in plain(final_norm_g[None], m_final_norm_g[None], v_final_norm_g[None], grad_final_norm_g[None], "adam_final"))
    res["a_v_norm_g"] = (grad_a_v_norm_g,) + plain(a_v_norm_g, m_a_v_norm_g, v_a_v_norm_g, grad_a_v_norm_g, "adam_a_v_norm")
    res["a_w_s"] = (grad_a_w_s,) + plain(a_w_s, m_a_w_s, v_a_w_s, grad_a_w_s, "adam_a_w_s")
    res["a_b_s"] = (grad_a_b_s,) + plain(a_b_s, m_a_b_s, v_a_b_s, grad_a_b_s, "adam_a_b_s")
    res["b_conv_w"] = (grad_b_conv_w,) + plain(b_conv_w, m_b_conv_w, v_b_conv_w, grad_b_conv_w, "adam_b_conv")
    res["c_scale"] = (grad_c_scale,) + plain(c_scale, m_c_scale, v_c_scale, grad_c_scale, "adam_c_scale")

    order = ["norm_mix_g", "norm_ffn_g", "final_norm_g", "a_w_in", "a_v_norm_g", "a_w_s", "a_b_s", "a_w_out", "b_w_in",
             "b_conv_w", "b_w_out", "c_w_in", "c_w_grp", "c_scale", "c_w_out", "f_w_gate", "f_w_up", "f_w_down"]
    return (loss, grad_x, *[res[n][0] for n in order], *[res[n][1] for n in order],
            *[res[n][2] for n in order], *[res[n][3] for n in order])
```

```python
import jax
import jax.numpy as jnp
from jax import lax
from jax.experimental import pallas as pl
from jax.experimental.pallas import tpu as pltpu

f32 = jnp.float32
bf16 = jnp.bfloat16

NDEV = 8
D = 1024
EPS = 1e-6
CHUNK = 64
GBLK = 128
A_GROUPS = 8
C_GROUPS = 4
C_GDIM = 256
POOL_WINDOWS = (2, 4, 8, 16)
HALO = 16
COLS = 256
F_LOC = 352
MXU_COLS = 256
DEPTH = 4
AXES = ("x", "y", "c")
MESH = pl.DeviceIdType.MESH

ADAM_LR = 0.001
ADAM_B1 = 0.9
ADAM_B2 = 0.999
ADAM_EPS = 1e-08
ADAM_WD = 0.01
ADAM_STEP = 10

VMEM_LIMIT = 56 * 1024 * 1024


def _cparams(sem):
    return pltpu.CompilerParams(dimension_semantics=sem, vmem_limit_bytes=VMEM_LIMIT)


def _gelu(z):
    return 0.5 * z * (1.0 + lax.erf(z * 0.7071067811865476))


def _gelu_cdf(z):
    return 0.5 * (1.0 + lax.erf(z * 0.7071067811865476))


def _gelu_grad(z, cdf):
    return cdf + z * jnp.exp(-0.5 * z * z) * 0.3989422804014327


def _dot(a, b):
    return jnp.dot(a, b, preferred_element_type=f32)


def _dot_nt(a, b):
    return lax.dot_general(a, b, (((1,), (1,)), ((), ())), preferred_element_type=f32)


def _dot_tn(a, b):
    return lax.dot_general(a, b, (((0,), (0,)), ((), ())), preferred_element_type=f32)


def _col_chunks(n, width=1024):
    return [(c, min(c + width, n)) for c in range(0, n, width)]


def norm_mm(x, g, w, *, tm, transposed, name):
    T = x.shape[0]
    n = w.shape[0] if transposed else w.shape[1]

    def body(x_ref, g_ref, w_ref, p_ref, h_ref):
        xv = x_ref[...]
        r = lax.rsqrt(jnp.mean(xv * xv, axis=-1, keepdims=True) + EPS)
        h = (xv * r * g_ref[...]).astype(bf16)
        h_ref[...] = h
        p = _dot_nt(h, w_ref[...]) if transposed else _dot(h, w_ref[...])
        p_ref[...] = p.astype(bf16)

    return pl.pallas_call(
        body, grid=(T // tm,),
        in_specs=[pl.BlockSpec((tm, D), lambda i: (i, 0)), pl.BlockSpec((1, D), lambda i: (0, 0)),
                  pl.BlockSpec(w.shape, lambda i: (0, 0), pipeline_mode=pl.Buffered(1))],
        out_specs=[pl.BlockSpec((tm, n), lambda i: (i, 0)), pl.BlockSpec((tm, D), lambda i: (i, 0))],
        out_shape=[jax.ShapeDtypeStruct((T, n), bf16), jax.ShapeDtypeStruct((T, D), bf16)],
        compiler_params=_cparams(("parallel",)), name=name,
    )(x, g, w)


def ffn_fwd(y, w_out, xin, g, wtg, wtu, wd, *, tm, name):
    T, K = y.shape
    H = wtg.shape[0]

    def body(y_ref, wo_ref, xin_ref, g_ref, wg_ref, wu_ref, wd_ref, x_ref, x2_ref, pg_ref, pu_ref, a_ref, h_ref):
        xv = xin_ref[...] + _dot(y_ref[...], wo_ref[...])
        x_ref[...] = xv
        r = lax.rsqrt(jnp.mean(xv * xv, axis=-1, keepdims=True) + EPS)
        h = (xv * r * g_ref[...]).astype(bf16)
        h_ref[...] = h
        out = None
        for c0, c1 in _col_chunks(H):
            gate = _dot_nt(h, wg_ref[c0:c1, :])
            up = _dot_nt(h, wu_ref[c0:c1, :])
            pg_ref[:, c0:c1] = gate.astype(bf16)
            pu_ref[:, c0:c1] = up.astype(bf16)
            ac = (gate * jax.nn.sigmoid(gate) * up).astype(bf16)
            a_ref[:, c0:c1] = ac
            part = _dot(ac, wd_ref[c0:c1, :])
            out = part if out is None else out + part
        x2_ref[...] = xv + out

    wspec = pl.BlockSpec((H, D), lambda i: (0, 0), pipeline_mode=pl.Buffered(1))
    hspec = pl.BlockSpec((tm, H), lambda i: (i, 0))
    dspec = pl.BlockSpec((tm, D), lambda i: (i, 0))
    return pl.pallas_call(
        body, grid=(T // tm,),
        in_specs=[pl.BlockSpec((tm, K), lambda i: (i, 0)),
                  pl.BlockSpec((K, D), lambda i: (0, 0), pipeline_mode=pl.Buffered(1)),
                  dspec, pl.BlockSpec((1, D), lambda i: (0, 0)), wspec, wspec, wspec],
        out_specs=[dspec, dspec, hspec, hspec, hspec, dspec],
        out_shape=[jax.ShapeDtypeStruct((T, D), f32)] * 2 + [jax.ShapeDtypeStruct((T, H), bf16)] * 3
        + [jax.ShapeDtypeStruct((T, D), bf16)],
        compiler_params=_cparams(("parallel",)), name=name,
    )(y, w_out, xin, g, wtg, wtu, wd)


def ffn_bwd(dx, wd, pg, pu, wtg, wtu, x, g, tok, w_next, *, tm, name):
    T = dx.shape[0]
    H = wd.shape[0]
    kn = w_next.shape[0]
    toks = [] if tok is None else [tok]

    def body(dx_ref, wd_ref, pg_ref, pu_ref, wg_ref, wu_ref, x_ref, g_ref, *rest):
        wn_ref = rest[len(toks)]
        dg_ref, du_ref, dxb_ref, dx1_ref, dgn_ref, dy_ref, dx1b_ref = rest[len(toks) + 1:]

        @pl.when(pl.program_id(0) == 0)
        def _():
            dgn_ref[...] = jnp.zeros(dgn_ref.shape, f32)

        dxv = dx_ref[...]
        dxb = dxv.astype(bf16)
        dxb_ref[...] = dxb
        dh = None
        for c0, c1 in _col_chunks(H):
            da = _dot_nt(dxb, wd_ref[c0:c1, :])
            gate = pg_ref[:, c0:c1].astype(f32)
            up = pu_ref[:, c0:c1].astype(f32)
            sg = jax.nn.sigmoid(gate)
            dgc = (da * up * (sg * (1.0 + gate * (1.0 - sg)))).astype(bf16)
            duc = (da * gate * sg).astype(bf16)
            dg_ref[:, c0:c1] = dgc
            du_ref[:, c0:c1] = duc
            part = _dot(dgc, wg_ref[c0:c1, :]) + _dot(duc, wu_ref[c0:c1, :])
            dh = part if dh is None else dh + part
        gv = g_ref[...] + rest[0][0:1, 0:1] if toks else g_ref[...]
        xv = x_ref[...]
        r = lax.rsqrt(jnp.mean(xv * xv, axis=-1, keepdims=True) + EPS)
        xhat = xv * r
        dgn_ref[...] += jnp.sum(dh * xhat, axis=0, keepdims=True)
        dxhat = dh * gv
        dx1 = dxv + r * (dxhat - xhat * jnp.mean(dxhat * xhat, axis=-1, keepdims=True))
        dx1_ref[...] = dx1
        dx1b = dx1.astype(bf16)
        dx1b_ref[...] = dx1b
        dy_ref[...] = _dot_nt(dx1b, wn_ref[...]).astype(bf16)

    dspec = pl.BlockSpec((tm, D), lambda i: (i, 0))
    hspec = pl.BlockSpec((tm, H), lambda i: (i, 0))
    wspec = pl.BlockSpec((H, D), lambda i: (0, 0), pipeline_mode=pl.Buffered(1))
    gspec = pl.BlockSpec((1, D), lambda i: (0, 0))
    return pl.pallas_call(
        body, grid=(T // tm,),
        in_specs=[dspec, wspec, hspec, hspec, wspec, wspec, dspec, gspec]
        + [pl.BlockSpec((8, 128), lambda i: (0, 0))] * len(toks)
        + [pl.BlockSpec((kn, D), lambda i: (0, 0), pipeline_mode=pl.Buffered(1))],
        out_specs=[hspec, hspec, dspec, dspec, gspec, pl.BlockSpec((tm, kn), lambda i: (i, 0)), dspec],
        out_shape=[jax.ShapeDtypeStruct((T, H), bf16)] * 2 + [jax.ShapeDtypeStruct((T, D), bf16),
                   jax.ShapeDtypeStruct((T, D), f32), jax.ShapeDtypeStruct((1, D), f32),
                   jax.ShapeDtypeStruct((T, kn), bf16), jax.ShapeDtypeStruct((T, D), bf16)],
        compiler_params=_cparams(("arbitrary",)), name=name,
    )(dx, wd, pg, pu, wtg, wtu, x, g, *toks, w_next)


def mm_tn2(a1, b1, a2, b2, *, name):
    T, K1 = a1.shape
    K2 = a2.shape[1]
    n1, n2 = b1.shape[1], b2.shape[1]
    last2 = K2 // MXU_COLS - 1

    def body(a1_ref, b1_ref, a2_ref, b2_ref, o1_ref, o2_ref):
        o1_ref[...] = _dot_tn(a1_ref[...], b1_ref[...]).astype(bf16)

        @pl.when(pl.program_id(0) <= last2)
        def _():
            o2_ref[...] = _dot_tn(a2_ref[...], b2_ref[...]).astype(bf16)

    o1, o2 = pl.pallas_call(
        body, grid=(K1 // MXU_COLS,),
        in_specs=[pl.BlockSpec((T, MXU_COLS), lambda d: (0, d)),
                  pl.BlockSpec((T, n1), lambda d: (0, 0), pipeline_mode=pl.Buffered(1)),
                  pl.BlockSpec((T, MXU_COLS), lambda d: (0, jnp.minimum(d, last2))),
                  pl.BlockSpec((T, n2), lambda d: (0, 0), pipeline_mode=pl.Buffered(1))],
        out_specs=[pl.BlockSpec((MXU_COLS, n1), lambda d: (d, 0)),
                   pl.BlockSpec((MXU_COLS, n2), lambda d: (jnp.minimum(d, last2), 0))],
        out_shape=[jax.ShapeDtypeStruct((K1, n1), bf16), jax.ShapeDtypeStruct((K2, n2), bf16)],
        compiler_params=_cparams(("arbitrary",)), name=name,
    )(a1, b1, a2, b2)
    return o1.reshape(NDEV, K1 // NDEV, n1), o2.reshape(NDEV, K2 // NDEV, n2)


def ffn_dw(dpg, dpu, a, h2, dxb, *, name):
    T, H = dpg.shape

    def body(g_ref, u_ref, a_ref, h_ref, x_ref, og_ref, ou_ref, od_ref):
        h = h_ref[...]
        og_ref[...] = _dot_tn(g_ref[...], h).astype(bf16)
        ou_ref[...] = _dot_tn(u_ref[...], h).astype(bf16)
        od_ref[...] = _dot_tn(a_ref[...], x_ref[...]).astype(bf16)

    blk = pl.BlockSpec((T, MXU_COLS), lambda d: (0, d))
    full = pl.BlockSpec((T, D), lambda d: (0, 0), pipeline_mode=pl.Buffered(1))
    out = pl.BlockSpec((MXU_COLS, D), lambda d: (d, 0))
    outs = pl.pallas_call(
        body, grid=(H // MXU_COLS,),
        in_specs=[blk, blk, blk, full, full], out_specs=[out] * 3,
        out_shape=[jax.ShapeDtypeStruct((H, D), bf16)] * 3,
        compiler_params=_cparams(("parallel",)), name=name,
    )(dpg, dpu, a, h2, dxb)
    return [o.reshape(NDEV, H // NDEV, D) for o in outs]


def _normbwd_tail(dh, x_ref, g_ref, tok_ref, dres_ref, dx_ref, dgn_ref):
    gv = g_ref[...] + tok_ref[0:1, 0:1]
    xv = x_ref[...]
    r = lax.rsqrt(jnp.mean(xv * xv, axis=-1, keepdims=True) + EPS)
    xhat = xv * r
    dgn_ref[...] += jnp.sum(dh * xhat, axis=0, keepdims=True)
    dxhat = dh * gv
    dx_ref[...] = dres_ref[...] + r * (dxhat - xhat * jnp.mean(dxhat * xhat, axis=-1, keepdims=True))


def _normbwd_specs(tm, w):
    dspec = pl.BlockSpec((tm, D), lambda i: (i, 0))
    gspec = pl.BlockSpec((1, D), lambda i: (0, 0))
    ins = [pl.BlockSpec(w.shape, lambda i: (0, 0), pipeline_mode=pl.Buffered(1)), dspec, gspec,
           pl.BlockSpec((8, 128), lambda i: (0, 0)), dspec]
    return ins, [dspec, gspec]


def _prev_halo_spec(tm, ncol):
    return pl.BlockSpec((HALO, ncol), lambda i: (jnp.maximum(i * (tm // HALO) - 1, 0), 0))


def _next_halo_spec(tm, ncol, T):
    return pl.BlockSpec((HALO, ncol), lambda i: (jnp.minimum((i + 1) * (tm // HALO), T // HALO - 1), 0))


def a_fwd(x, g, wt, gv, wm, bsb, *, tm, name):
    T = x.shape[0]

    def body(x_ref, g_ref, wt_ref, gv_ref, wm_ref, bs_ref, p_ref, h_ref, y_ref, tok_ref, vn_sc):
        tok_ref[...] = jnp.zeros(tok_ref.shape, f32)
        xv = x_ref[...]
        r = lax.rsqrt(jnp.mean(xv * xv, axis=-1, keepdims=True) + EPS)
        h = (xv * r * g_ref[...]).astype(bf16)
        h_ref[...] = h
        p_ref[...] = _dot_nt(h, wt_ref[...]).astype(bf16)
        v = _gelu(p_ref[:, D:].astype(f32))
        vc = v - jnp.mean(v, axis=-1, keepdims=True)
        var = jnp.mean(vc * vc, axis=-1, keepdims=True)
        vn_sc[...] = (vc * lax.rsqrt(var + EPS) * gv_ref[...]).astype(bf16)
        for g in range(A_GROUPS):
            cs = slice(g * GBLK, (g + 1) * GBLK)
            for r in range(tm // GBLK):
                rs = slice(r * GBLK, (r + 1) * GBLK)
                sv = _dot(wm_ref[g], vn_sc[rs, cs]) + bs_ref[g]
                y_ref[rs, cs] = (_gelu(p_ref[rs, cs].astype(f32)) * sv).astype(bf16)

    dspec = pl.BlockSpec((tm, D), lambda i: (i, 0))
    gspec = pl.BlockSpec((1, D), lambda i: (0, 0))
    return pl.pallas_call(
        body, grid=(T // tm,),
        in_specs=[dspec, gspec, pl.BlockSpec(wt.shape, lambda i: (0, 0), pipeline_mode=pl.Buffered(1)), gspec,
                  pl.BlockSpec((A_GROUPS, GBLK, GBLK), lambda i: (0, 0, 0)),
                  pl.BlockSpec((A_GROUPS, GBLK, GBLK), lambda i: (0, 0, 0))],
        out_specs=[pl.BlockSpec((tm, 2 * D), lambda i: (i, 0)), dspec, dspec, pl.BlockSpec((8, 128), lambda i: (0, 0))],
        out_shape=[jax.ShapeDtypeStruct((T, 2 * D), bf16), jax.ShapeDtypeStruct((T, D), bf16),
                   jax.ShapeDtypeStruct((T, D), bf16), jax.ShapeDtypeStruct((8, 128), f32)],
        scratch_shapes=[pltpu.VMEM((tm, D), bf16)],
        compiler_params=_cparams(("arbitrary",)), name=name,
    )(x, g, wt, gv, wm, bsb)


def a_mid_bwd(p, dy, gv, wm, bsb, wt, x, g, tok, dres, *, tm, name):
    T = p.shape[0]

    def body(p_ref, dy_ref, gv_ref, wm_ref, bs_ref, wt_ref, x_ref, g_ref, tok_ref, dres_ref,
             dp_ref, dgv_ref, dwm_ref, dbs_ref, dx_ref, dgn_ref, vn_sc, dvn_sc):
        @pl.when(pl.program_id(0) == 0)
        def _():
            dgn_ref[...] = jnp.zeros(dgn_ref.shape, f32)
            dgv_ref[...] = jnp.zeros(dgv_ref.shape, f32)
            dwm_ref[...] = jnp.zeros(dwm_ref.shape, f32)
            dbs_ref[...] = jnp.zeros(dbs_ref.shape, f32)

        zv = p_ref[:, D:].astype(f32)
        cdf_v = _gelu_cdf(zv)
        v = zv * cdf_v
        vc = v - jnp.mean(v, axis=-1, keepdims=True)
        rstd = lax.rsqrt(jnp.mean(vc * vc, axis=-1, keepdims=True) + EPS)
        vhat = vc * rstd
        vn_sc[...] = (vhat * gv_ref[...]).astype(bf16)
        for g in range(A_GROUPS):
            cs = slice(g * GBLK, (g + 1) * GBLK)
            dwm = jnp.zeros((GBLK, GBLK), f32)
            dbs = jnp.zeros((GBLK, 1), f32)
            for r in range(tm // GBLK):
                rs = slice(r * GBLK, (r + 1) * GBLK)
                zu = p_ref[rs, cs].astype(f32)
                cdf_u = _gelu_cdf(zu)
                vn = vn_sc[rs, cs]
                sv = _dot(wm_ref[g], vn) + bs_ref[g]
                dyb = dy_ref[rs, cs].astype(f32)
                dsv = dyb * (zu * cdf_u)
                dsvb = dsv.astype(bf16)
                dp_ref[rs, cs] = (dyb * sv * _gelu_grad(zu, cdf_u)).astype(bf16)
                dwm += _dot_nt(dsvb, vn)
                dbs += jnp.sum(dsv, axis=1, keepdims=True)
                dvn_sc[rs, cs] = _dot_tn(wm_ref[g], dsvb)
            dwm_ref[g] += dwm
            dbs_ref[g] += dbs
        dvn = dvn_sc[...]
        dgv_ref[...] += jnp.sum(dvn * vhat, axis=0, keepdims=True)
        dvhat = dvn * gv_ref[...]
        dv = rstd * (dvhat - jnp.mean(dvhat, axis=-1, keepdims=True)
                     - vhat * jnp.mean(dvhat * vhat, axis=-1, keepdims=True))
        dp_ref[:, D:] = (dv * _gelu_grad(zv, cdf_v)).astype(bf16)
        _normbwd_tail(_dot(dp_ref[...], wt_ref[...]), x_ref, g_ref, tok_ref, dres_ref, dx_ref, dgn_ref)

    tail_in, tail_out = _normbwd_specs(tm, wt)
    return pl.pallas_call(
        body, grid=(T // tm,),
        in_specs=[pl.BlockSpec((tm, 2 * D), lambda i: (i, 0)), pl.BlockSpec((tm, D), lambda i: (i, 0)),
                  pl.BlockSpec((1, D), lambda i: (0, 0)),
                  pl.BlockSpec((A_GROUPS, GBLK, GBLK), lambda i: (0, 0, 0)),
                  pl.BlockSpec((A_GROUPS, GBLK, GBLK), lambda i: (0, 0, 0))] + tail_in,
        out_specs=[pl.BlockSpec((tm, 2 * D), lambda i: (i, 0)), pl.BlockSpec((1, D), lambda i: (0, 0)),
                   pl.BlockSpec((A_GROUPS, GBLK, GBLK), lambda i: (0, 0, 0)),
                   pl.BlockSpec((A_GROUPS, GBLK, 1), lambda i: (0, 0, 0))] + tail_out,
        out_shape=[jax.ShapeDtypeStruct((T, 2 * D), bf16), jax.ShapeDtypeStruct((1, D), f32),
                   jax.ShapeDtypeStruct((A_GROUPS, GBLK, GBLK), f32), jax.ShapeDtypeStruct((A_GROUPS, GBLK, 1), f32),
                   jax.ShapeDtypeStruct((T, D), f32), jax.ShapeDtypeStruct((1, D), f32)],
        scratch_shapes=[pltpu.VMEM((tm, D), bf16), pltpu.VMEM((tm, D), f32)],
        compiler_params=_cparams(("arbitrary",)), name=name,
    )(p, dy, gv, wm, bsb, wt, x, g, tok, dres)


def b_mid_fwd(p, cw, *, tm, S, name):
    T = p.shape[0]
    nst = S // tm

    def body(p_ref, ph_ref, cw_ref, y_ref, ext):
        first = (pl.program_id(0) % nst) == 0
        for c0 in range(0, D, COLS):
            cs, cc, cx = slice(c0, c0 + COLS), slice(D + c0, D + c0 + COLS), slice(2 * D + c0, 2 * D + c0 + COLS)
            q = p_ref[:, cc].astype(f32) * p_ref[:, cx].astype(f32)
            qh = ph_ref[:, cc].astype(f32) * ph_ref[:, cx].astype(f32)
            ext[0:HALO, cs] = jnp.where(first, 0.0, qh)
            ext[HALO:, cs] = q
            y = (cw_ref[0:1, cs] * ext[pl.ds(HALO - 2, tm), cs] + cw_ref[1:2, cs] * ext[pl.ds(HALO - 1, tm), cs]
                 + cw_ref[2:3, cs] * q)
            y_ref[:, cs] = (p_ref[:, cs].astype(f32) * y).astype(bf16)

    return pl.pallas_call(
        body, grid=(T // tm,),
        in_specs=[pl.BlockSpec((tm, 3 * D), lambda i: (i, 0)), _prev_halo_spec(tm, 3 * D),
                  pl.BlockSpec((3, D), lambda i: (0, 0))],
        out_specs=pl.BlockSpec((tm, D), lambda i: (i, 0)),
        out_shape=jax.ShapeDtypeStruct((T, D), bf16),
        scratch_shapes=[pltpu.VMEM((tm + HALO, D), f32)],
        compiler_params=_cparams(("parallel",)), name=name,
    )(p, p, cw)


def b_mid_bwd(p, dy, cw, wt, x, g, tok, dres, *, tm, S, name):
    T = p.shape[0]
    nst = S // tm

    def body(p_ref, ph_ref, pn_ref, dy_ref, dyn_ref, cw_ref, wt_ref, x_ref, g_ref, tok_ref, dres_ref,
             dp_ref, dcw_ref, dx_ref, dgn_ref, ext, ext2):
        i = pl.program_id(0)
        first = (i % nst) == 0
        last = (i % nst) == nst - 1

        @pl.when(i == 0)
        def _():
            dgn_ref[...] = jnp.zeros(dgn_ref.shape, f32)
            dcw_ref[...] = jnp.zeros(dcw_ref.shape, f32)

        for c0 in range(0, D, COLS):
            cs, cc, cx = slice(c0, c0 + COLS), slice(D + c0, D + c0 + COLS), slice(2 * D + c0, 2 * D + c0 + COLS)
            gb, gc, xt = p_ref[:, cs].astype(f32), p_ref[:, cc].astype(f32), p_ref[:, cx].astype(f32)
            q = gc * xt
            ext[0:HALO, cs] = jnp.where(first, 0.0, ph_ref[:, cc].astype(f32) * ph_ref[:, cx].astype(f32))
            ext[HALO:, cs] = q
            q2 = ext[pl.ds(HALO - 2, tm), cs]
            q1 = ext[pl.ds(HALO - 1, tm), cs]
            y = cw_ref[0:1, cs] * q2 + cw_ref[1:2, cs] * q1 + cw_ref[2:3, cs] * q
            dyo = dy_ref[:, cs].astype(f32)
            dp_ref[:, cs] = (dyo * y).astype(bf16)
            dyc = dyo * gb
            ext2[0:tm, cs] = dyc
            ext2[tm:, cs] = jnp.where(last, 0.0, dyn_ref[:, cs].astype(f32) * pn_ref[:, cs].astype(f32))
            dq = cw_ref[2:3, cs] * dyc + cw_ref[1:2, cs] * ext2[pl.ds(1, tm), cs] + cw_ref[0:1, cs] * ext2[pl.ds(2, tm), cs]
            dp_ref[:, cc] = (dq * xt).astype(bf16)
            dp_ref[:, cx] = (dq * gc).astype(bf16)
            dcw_ref[0:1, cs] += jnp.sum(dyc * q2, axis=0, keepdims=True)
            dcw_ref[1:2, cs] += jnp.sum(dyc * q1, axis=0, keepdims=True)
            dcw_ref[2:3, cs] += jnp.sum(dyc * q, axis=0, keepdims=True)
        _normbwd_tail(_dot(dp_ref[...], wt_ref[...]), x_ref, g_ref, tok_ref, dres_ref, dx_ref, dgn_ref)

    tail_in, tail_out = _normbwd_specs(tm, wt)
    return pl.pallas_call(
        body, grid=(T // tm,),
        in_specs=[pl.BlockSpec((tm, 3 * D), lambda i: (i, 0)), _prev_halo_spec(tm, 3 * D), _next_halo_spec(tm, 3 * D, T),
                  pl.BlockSpec((tm, D), lambda i: (i, 0)), _next_halo_spec(tm, D, T),
                  pl.BlockSpec((3, D), lambda i: (0, 0))] + tail_in,
        out_specs=[pl.BlockSpec((tm, 3 * D), lambda i: (i, 0)), pl.BlockSpec((3, D), lambda i: (0, 0))] + tail_out,
        out_shape=[jax.ShapeDtypeStruct((T, 3 * D), bf16), jax.ShapeDtypeStruct((3, D), f32),
                   jax.ShapeDtypeStruct((T, D), f32), jax.ShapeDtypeStruct((1, D), f32)],
        scratch_shapes=[pltpu.VMEM((tm + HALO, D), f32), pltpu.VMEM((tm + HALO, D), f32)],
        compiler_params=_cparams(("arbitrary",)), name=name,
    )(p, p, p, dy, dy, cw, wt, x, g, tok, dres)


def _pool_counts(i, nst, tm, rows, row0, w):
    t = (i % nst) * tm + row0 + lax.broadcasted_iota(jnp.int32, (rows, 1), 0)
    return jnp.minimum(t + 1, w).astype(f32)


def _pool_diff(p_ref, ext, g, i, nst, tm):
    w = POOL_WINDOWS[g]
    cs = slice(g * C_GDIM, (g + 1) * C_GDIM)
    pg = p_ref[:, cs].astype(f32)
    s = pg
    for k in range(1, w):
        s = s + ext[pl.ds(HALO - k, tm), cs]
    return s / _pool_counts(i, nst, tm, tm, 0, w) - pg


def c_mid_fwd(p, wg, scale, *, tm, S, name):
    T = p.shape[0]
    nst = S // tm

    def body(p_ref, ph_ref, wg_ref, sc_ref, y_ref, ext):
        i = pl.program_id(0)
        first = (i % nst) == 0
        ext[0:HALO, :] = jnp.where(first, 0.0, ph_ref[...].astype(f32))
        ext[HALO:, :] = p_ref[...].astype(f32)
        for g in range(C_GROUPS):
            cs = slice(g * C_GDIM, (g + 1) * C_GDIM)
            dg = _pool_diff(p_ref, ext, g, i, nst, tm).astype(bf16)
            wv = wg_ref[:, g].reshape(C_GDIM, C_GDIM)
            y_ref[:, cs] = (_dot(dg, wv) * sc_ref[:, cs]).astype(bf16)

    return pl.pallas_call(
        body, grid=(T // tm,),
        in_specs=[pl.BlockSpec((tm, D), lambda i: (i, 0)), _prev_halo_spec(tm, D),
                  pl.BlockSpec((NDEV, C_GROUPS, C_GDIM // NDEV, C_GDIM), lambda i: (0, 0, 0, 0)),
                  pl.BlockSpec((1, D), lambda i: (0, 0))],
        out_specs=pl.BlockSpec((tm, D), lambda i: (i, 0)),
        out_shape=jax.ShapeDtypeStruct((T, D), bf16),
        scratch_shapes=[pltpu.VMEM((tm + HALO, D), f32)],
        compiler_params=_cparams(("parallel",)), name=name,
    )(p, p, wg, scale)


def c_mid_bwd(p, dy, wg, scale, w, x, g, tok, dres, *, tm, S, name):
    T = p.shape[0]
    nst = S // tm

    def body(p_ref, ph_ref, dy_ref, dyn_ref, wg_ref, sc_ref, w_ref, x_ref, g_ref, tok_ref, dres_ref,
             dp_ref, dsc_ref, dwg_ref, dx_ref, dgn_ref, ext, ext2):
        i = pl.program_id(0)
        first = (i % nst) == 0
        last = (i % nst) == nst - 1

        @pl.when(i == 0)
        def _():
            dgn_ref[...] = jnp.zeros(dgn_ref.shape, f32)
            dsc_ref[...] = jnp.zeros(dsc_ref.shape, f32)
            dwg_ref[...] = jnp.zeros(dwg_ref.shape, f32)

        ext[0:HALO, :] = jnp.where(first, 0.0, ph_ref[...].astype(f32))
        ext[HALO:, :] = p_ref[...].astype(f32)
        for g in range(C_GROUPS):
            w = POOL_WINDOWS[g]
            cs = slice(g * C_GDIM, (g + 1) * C_GDIM)
            dg = _pool_diff(p_ref, ext, g, i, nst, tm).astype(bf16)
            wv = wg_ref[:, g].reshape(C_GDIM, C_GDIM)
            dyo = dy_ref[:, cs].astype(f32)
            dsc_ref[:, cs] += jnp.sum(dyo * _dot(dg, wv), axis=0, keepdims=True)
            dyp = (dyo * sc_ref[:, cs]).astype(bf16)
            dypn = (dyn_ref[:, cs].astype(f32) * sc_ref[:, cs]).astype(bf16)
            dwg_ref[g] += _dot_tn(dg, dyp)
            dd = _dot_nt(dyp, wv)
            ddn = _dot_nt(dypn, wv)
            ext2[0:tm, cs] = dd / _pool_counts(i, nst, tm, tm, 0, w)
            ext2[tm:, cs] = jnp.where(last, 0.0, ddn / _pool_counts(i, nst, tm, HALO, tm, w))
            s = -dd
            for k in range(w):
                s = s + ext2[pl.ds(k, tm), cs]
            dp_ref[:, cs] = s.astype(bf16)
        _normbwd_tail(_dot_nt(dp_ref[...], w_ref[...]), x_ref, g_ref, tok_ref, dres_ref, dx_ref, dgn_ref)

    tail_in, tail_out = _normbwd_specs(tm, w)
    return pl.pallas_call(
        body, grid=(T // tm,),
        in_specs=[pl.BlockSpec((tm, D), lambda i: (i, 0)), _prev_halo_spec(tm, D),
                  pl.BlockSpec((tm, D), lambda i: (i, 0)), _next_halo_spec(tm, D, T),
                  pl.BlockSpec((NDEV, C_GROUPS, C_GDIM // NDEV, C_GDIM), lambda i: (0, 0, 0, 0)),
                  pl.BlockSpec((1, D), lambda i: (0, 0))] + tail_in,
        out_specs=[pl.BlockSpec((tm, D), lambda i: (i, 0)), pl.BlockSpec((1, D), lambda i: (0, 0)),
                   pl.BlockSpec((C_GROUPS, C_GDIM, C_GDIM), lambda i: (0, 0, 0))] + tail_out,
        out_shape=[jax.ShapeDtypeStruct((T, D), bf16), jax.ShapeDtypeStruct((1, D), f32),
                   jax.ShapeDtypeStruct((C_GROUPS, C_GDIM, C_GDIM), f32),
                   jax.ShapeDtypeStruct((T, D), f32), jax.ShapeDtypeStruct((1, D), f32)],
        scratch_shapes=[pltpu.VMEM((tm + HALO, D), f32), pltpu.VMEM((tm + HALO, D), f32)],
        compiler_params=_cparams(("arbitrary",)), name=name,
    )(p, p, dy, dy, wg, scale, w, x, g, tok, dres)


def final_loss(x, g, tgt, *, tm, name):
    T = x.shape[0]

    def body(x_ref, g_ref, t_ref, loss_ref, dx_ref, dg_ref):
        @pl.when(pl.program_id(0) == 0)
        def _():
            loss_ref[...] = jnp.zeros(loss_ref.shape, f32)
            dg_ref[...] = jnp.zeros(dg_ref.shape, f32)

        xv = x_ref[...]
        r = lax.rsqrt(jnp.mean(xv * xv, axis=-1, keepdims=True) + EPS)
        xhat = xv * r
        err = xhat * g_ref[...] - t_ref[...]
        loss_ref[...] += 0.5 * jnp.sum(jnp.mean(err * err, axis=-1, keepdims=True))
        dy = err * (1.0 / D)
        dg_ref[...] += jnp.sum(dy * xhat, axis=0, keepdims=True)
        dxhat = dy * g_ref[...]
        dx_ref[...] = r * (dxhat - xhat * jnp.mean(dxhat * xhat, axis=-1, keepdims=True))

    return pl.pallas_call(
        body, grid=(T // tm,),
        in_specs=[pl.BlockSpec((tm, D), lambda i: (i, 0)), pl.BlockSpec((1, D), lambda i: (0, 0)),
                  pl.BlockSpec((tm, D), lambda i: (i, 0))],
        out_specs=[pl.BlockSpec((8, 128), lambda i: (0, 0)), pl.BlockSpec((tm, D), lambda i: (i, 0)),
                   pl.BlockSpec((1, D), lambda i: (0, 0))],
        out_shape=[jax.ShapeDtypeStruct((8, 128), f32), jax.ShapeDtypeStruct((T, D), f32),
                   jax.ShapeDtypeStruct((1, D), f32)],
        compiler_params=_cparams(("arbitrary",)), name=name,
    )(x, g, tgt)


def _slot(px, py, pc):
    return 4 * px + 2 * py + pc


def _with_own_block(s, me):
    zone = lax.empty((NDEV,) + s.shape, s.dtype)
    return lax.dynamic_update_slice(zone, s[None], (me,) + (0,) * s.ndim)


def all_gather(arrs, me, *, name):
    n = len(arrs)

    def body(*refs):
        ins, outs = refs[:n], refs[2 * n:3 * n]
        send_sems, recv_sems = refs[3 * n:]
        x, y, c = lax.axis_index("x"), lax.axis_index("y"), lax.axis_index("c")
        me, sibling = (x, y, c), (x, y, 1 - c)
        chips = [(1 - x, y), (x, 1 - y), (1 - x, 1 - y)]

        def copy(a, k, block, to, src=None):
            dst = outs[a].at[_slot(*block)]
            return pltpu.make_async_remote_copy(
                src_ref=dst if src is None else src, dst_ref=dst,
                send_sem=send_sems.at[a, k], recv_sem=recv_sems.at[a, k], device_id=to, device_id_type=MESH)

        first = []
        for a in range(n):
            first.append(copy(a, 0, me, sibling, src=ins[a]))
            first += [copy(a, 1 + j, me, (*chip, c), src=ins[a]) for j, chip in enumerate(chips)]
        for cp in first:
            cp.start()
        passed = []
        for j, chip in enumerate(chips):
            for a in range(n):
                copy(a, 1 + j, (*chip, c), me).wait_recv()
                fwd = copy(a, 4 + j, (*chip, c), sibling)
                fwd.start()
                passed.append(fwd)
        for a in range(n):
            copy(a, 0, sibling, me).wait_recv()
        for j, chip in enumerate(chips):
            for a in range(n):
                copy(a, 4 + j, (*chip, 1 - c), me).wait_recv()
        for cp in first + passed:
            cp.wait_send()

    any_spec = pl.BlockSpec(memory_space=pl.ANY)
    return pl.pallas_call(
        body,
        in_specs=[any_spec] * (2 * n), out_specs=[any_spec] * n,
        out_shape=[jax.ShapeDtypeStruct((NDEV,) + a.shape, a.dtype) for a in arrs],
        input_output_aliases={n + i: i for i in range(n)},
        scratch_shapes=[pltpu.SemaphoreType.DMA((n, 7)), pltpu.SemaphoreType.DMA((n, 7))],
        compiler_params=pltpu.CompilerParams(has_side_effects=True), name=name,
    )(*arrs, *[_with_own_block(a, me) for a in arrs])


HBM_SPEC = pl.BlockSpec(memory_space=pltpu.HBM)
SEM_SPEC = pl.BlockSpec(memory_space=pltpu.SEMAPHORE)
ANY_SPEC = pl.BlockSpec(memory_space=pl.ANY)
TOKEN_SHAPE = jax.ShapeDtypeStruct((8, 128), f32)
DATAFLOW_EFFECT = pltpu.SideEffectType.DATAFLOW_SIDE_EFFECTING


def _in_hbm(a):
    return pltpu.with_memory_space_constraint(a, pltpu.HBM)


def _hbm_like(a):
    return pltpu.HBM(a.shape, a.dtype)


def _mesh_pos():
    return lax.axis_index("x"), lax.axis_index("y"), lax.axis_index("c")


def _gather_targets(x, y, c):
    return [(x, y, 1 - c), (1 - x, y, c), (x, 1 - y, c), (1 - x, 1 - y, c)]


def gather_start(shards, me, after, *, name):
    n = len(shards)
    extra = [] if after is None else [after]

    def body(*refs):
        srcs, lands = refs[:n], refs[n:2 * n]
        send_sems, recv_sems = refs[2 * n + len(extra)], refs[2 * n + len(extra) + 1]
        token = refs[-1]
        x, y, c = _mesh_pos()
        me = _slot(x, y, c)
        for a in range(n):
            for k, to in enumerate(_gather_targets(x, y, c)):
                pltpu.make_async_remote_copy(
                    src_ref=srcs[a], dst_ref=lands[a].at[me], send_sem=send_sems.at[4 * a + k], recv_sem=recv_sems.at[4 * a + k],
                    device_id=to, device_id_type=MESH).start()
        token[...] = jnp.zeros(token.shape, f32)

    lands = [_with_own_block(s, me) for s in shards]
    sems = pltpu.SemaphoreType.DMA((4 * n,))
    out = pl.pallas_call(
        body, name=name,
        in_specs=[HBM_SPEC] * (2 * n) + [ANY_SPEC] * len(extra),
        out_specs=[SEM_SPEC, SEM_SPEC] + [HBM_SPEC] * (2 * n) + [pl.BlockSpec(memory_space=pltpu.VMEM)],
        out_shape=[sems, sems] + [_hbm_like(s) for s in shards] + [_hbm_like(l) for l in lands] + [TOKEN_SHAPE],
        input_output_aliases={i: 2 + i for i in range(2 * n)},
        compiler_params=pltpu.CompilerParams(has_side_effects=DATAFLOW_EFFECT),
    )(*[_in_hbm(s) for s in shards], *[_in_hbm(l) for l in lands], *extra)
    return out[0], out[1], out[2:2 + n], out[2 + n:2 + 2 * n], out[-1]


def gather_wait(send_sems, recv_sems, shards, lands, after, *, name):
    n = len(shards)
    after = list(after) if isinstance(after, (list, tuple)) else [after]

    def body(*refs):
        srcs, lands_in = refs[:n], refs[n:2 * n]
        send_sems, recv_sems = refs[2 * n], refs[2 * n + 1]
        x, y, c = _mesh_pos()
        for a in range(n):
            for k, frm in enumerate(_gather_targets(x, y, c)):
                cp = pltpu.make_async_remote_copy(
                    src_ref=srcs[a], dst_ref=lands_in[a].at[_slot(*frm)], send_sem=send_sems.at[4 * a + k],
                    recv_sem=recv_sems.at[4 * a + k], device_id=frm, device_id_type=MESH)
                cp.wait_send()
                cp.wait_recv()

    out = pl.pallas_call(
        body, name=name,
        in_specs=[HBM_SPEC] * (2 * n) + [SEM_SPEC, SEM_SPEC] + [ANY_SPEC] * len(after),
        out_specs=[HBM_SPEC] * (2 * n),
        out_shape=[_hbm_like(s) for s in shards] + [_hbm_like(l) for l in lands],
        input_output_aliases={i: i for i in range(2 * n)},
        compiler_params=pltpu.CompilerParams(has_side_effects=DATAFLOW_EFFECT),
    )(*shards, *lands, send_sems, recv_sems, *after)
    return out[:n], out[n:]


def gather_finish(lands, *, name):
    n = len(lands)

    def body(*refs):
        lands_in, lands_out = refs[:n], refs[n:2 * n]
        send_sems, recv_sems = refs[2 * n:]
        x, y, c = _mesh_pos()
        sibling = (x, y, 1 - c)
        chips = [(1 - x, y), (x, 1 - y), (1 - x, 1 - y)]
        sent = []
        for a in range(n):
            for j, chip in enumerate(chips):
                s = _slot(*chip, c)
                cp = pltpu.make_async_remote_copy(
                    src_ref=lands_in[a].at[s], dst_ref=lands_out[a].at[s], send_sem=send_sems.at[a, j],
                    recv_sem=recv_sems.at[a, j], device_id=sibling, device_id_type=MESH)
                cp.start()
                sent.append(cp)
        for a in range(n):
            for j, chip in enumerate(chips):
                s = _slot(*chip, 1 - c)
                pltpu.make_async_remote_copy(
                    src_ref=lands_in[a].at[s], dst_ref=lands_out[a].at[s], send_sem=send_sems.at[a, j],
                    recv_sem=recv_sems.at[a, j], device_id=sibling, device_id_type=MESH).wait_recv()
        for cp in sent:
            cp.wait_send()

    return pl.pallas_call(
        body, name=name,
        in_specs=[ANY_SPEC] * n, out_specs=[ANY_SPEC] * n,
        out_shape=[jax.ShapeDtypeStruct(l.shape, l.dtype) for l in lands],
        input_output_aliases={i: i for i in range(n)},
        scratch_shapes=[pltpu.SemaphoreType.DMA((n, 3)), pltpu.SemaphoreType.DMA((n, 3))],
        compiler_params=pltpu.CompilerParams(has_side_effects=True),
    )(*lands)


def _peer(x, y, c, r):
    return (x ^ ((r >> 2) & 1), y ^ ((r >> 1) & 1), c ^ (r & 1))


def exchange_start(items, lands, after, *, name):
    n, m = len(items), len(lands)
    extra = [] if after is None else [after]

    def body(*refs):
        gs, zones = refs[:n], refs[n:n + m]
        send_sems, recv_sems = refs[n + m + len(extra)], refs[n + m + len(extra) + 1]
        token = refs[-1]
        x, y, c = _mesh_pos()
        me = _slot(x, y, c)
        for r in (1, 2, 4, 3, 5, 6, 7):
            to = _peer(x, y, c, r)
            for a, (_, zi, l) in enumerate(items):
                pltpu.make_async_remote_copy(
                    src_ref=gs[a].at[_slot(*to)], dst_ref=zones[zi].at[me, l], send_sem=send_sems.at[7 * a + r - 1],
                    recv_sem=recv_sems.at[7 * a + r - 1], device_id=to, device_id_type=MESH).start()
        token[...] = jnp.zeros(token.shape, f32)

    gs = [g for g, _, _ in items]
    sems = pltpu.SemaphoreType.DMA((7 * n,))
    out = pl.pallas_call(
        body, name=name,
        in_specs=[HBM_SPEC] * (n + m) + [ANY_SPEC] * len(extra),
        out_specs=[SEM_SPEC, SEM_SPEC] + [HBM_SPEC] * (n + m) + [pl.BlockSpec(memory_space=pltpu.VMEM)],
        out_shape=[sems, sems] + [_hbm_like(g) for g in gs] + [_hbm_like(z) for z in lands] + [TOKEN_SHAPE],
        input_output_aliases={i: 2 + i for i in range(n + m)},
        compiler_params=pltpu.CompilerParams(has_side_effects=DATAFLOW_EFFECT),
    )(*[_in_hbm(g) for g in gs], *[_in_hbm(z) for z in lands], *extra)
    return out[0], out[1], out[2:2 + n], out[2 + n:2 + n + m], out[-1]


def exchange_wait(parts, lands, after, *, name):
    m = len(lands)
    flat_gs = [g for _, _, _, gs in parts for g in gs]
    ng = len(flat_gs)
    after = list(after) if isinstance(after, (list, tuple)) else [after]

    def body(*refs):
        gs, zones = refs[:ng], refs[ng:ng + m]
        sem_refs = refs[ng + m:ng + m + 2 * len(parts)]
        x, y, c = _mesh_pos()
        base = 0
        for pi, (items, _, _, _) in enumerate(parts):
            send_sems, recv_sems = sem_refs[2 * pi], sem_refs[2 * pi + 1]
            for r in range(1, NDEV):
                frm = _peer(x, y, c, r)
                for a, (_, zi, l) in enumerate(items):
                    cp = pltpu.make_async_remote_copy(
                        src_ref=gs[base + a].at[_slot(*frm)], dst_ref=zones[zi].at[_slot(*frm), l],
                        send_sem=send_sems.at[7 * a + r - 1], recv_sem=recv_sems.at[7 * a + r - 1],
                        device_id=frm, device_id_type=MESH)
                    cp.wait_send()
                    cp.wait_recv()
            base += len(items)

    sem_args = [s for _, ss, rs, _ in parts for s in (ss, rs)]
    out = pl.pallas_call(
        body, name=name,
        in_specs=[HBM_SPEC] * (ng + m) + [SEM_SPEC] * len(sem_args) + [ANY_SPEC] * len(after),
        out_specs=[HBM_SPEC] * (ng + m),
        out_shape=[_hbm_like(g) for g in flat_gs] + [_hbm_like(z) for z in lands],
        input_output_aliases={i: i for i in range(ng + m)},
        compiler_params=pltpu.CompilerParams(has_side_effects=DATAFLOW_EFFECT),
    )(*flat_gs, *lands, *sem_args, *after)
    return out[ng:]


def sum8(g, *, name):
    _, R, C = g.shape
    tr = R

    def body(g_ref, o_ref):
        s = g_ref[0]
        for k in range(1, NDEV):
            s = s + g_ref[k]
        o_ref[...] = s

    return pl.pallas_call(
        body, grid=(R // tr,),
        in_specs=[pl.BlockSpec((NDEV, tr, C), lambda i: (0, i, 0))],
        out_specs=pl.BlockSpec((tr, C), lambda i: (i, 0)),
        out_shape=jax.ShapeDtypeStruct((R, C), f32),
        compiler_params=_cparams(("parallel",)), name=name,
    )(g)


def _adam_math(w, g, m, v):
    m = ADAM_B1 * m + (1.0 - ADAM_B1) * g
    v = ADAM_B2 * v + (1.0 - ADAM_B2) * (g * g)
    m_hat = m / (1.0 - ADAM_B1 ** ADAM_STEP)
    v_hat = v / (1.0 - ADAM_B2 ** ADAM_STEP)
    delta = -ADAM_LR * (m_hat / (jnp.sqrt(v_hat) + ADAM_EPS) + ADAM_WD * w)
    return delta, m, v


def adamw_slices(w, m, v, recv, *, name):
    return adamw_families([w], [m], [v], [recv], tc=w.shape[2], name=name)


def adamw_families(ws, ms, vs, recvs, *, tc, l0=0, nl=None, prev=None, name):
    nf = len(ws)
    L, R, C = ws[0].shape
    nl = L - l0 if nl is None else nl
    prevs = [] if prev is None else list(prev)

    def body(*refs):
        outs = refs[4 * nf + len(prevs):]
        for q in range(nf):
            w_ref, m_ref, v_ref, r_ref = refs[q], refs[nf + q], refs[2 * nf + q], refs[3 * nf + q]
            g = r_ref[0].astype(f32)
            for k in range(1, NDEV):
                g = g + r_ref[k].astype(f32)
            delta, nm, nv = _adam_math(w_ref[...], g, m_ref[...], v_ref[...])
            outs[4 * q][...] = g
            outs[4 * q + 1][...] = delta
            outs[4 * q + 2][...] = nm
            outs[4 * q + 3][...] = nv

    wspec = pl.BlockSpec((None, R, tc), lambda l, j: (l0 + l, 0, j))
    rspec = pl.BlockSpec((NDEV, None, R, tc), lambda l, j: (0, l0 + l, 0, j))
    return pl.pallas_call(
        body, grid=(nl, C // tc),
        in_specs=[wspec] * (3 * nf) + [rspec] * nf + [pl.BlockSpec(memory_space=pl.ANY)] * len(prevs),
        out_specs=[wspec] * (4 * nf), out_shape=[jax.ShapeDtypeStruct((L, R, C), f32)] * (4 * nf),
        input_output_aliases={4 * nf + q: q for q in range(len(prevs))},
        compiler_params=_cparams(("parallel", "parallel")), name=name,
    )(*ws, *ms, *vs, *recvs, *prevs)


def adamw_slices_t(w, m, v, recv, *, name):
    L, R, C = w.shape

    def body(w_ref, m_ref, v_ref, r_ref, g_ref, d_ref, nm_ref, nv_ref):
        gt = r_ref[0].astype(f32)
        for k in range(1, NDEV):
            gt = gt + r_ref[k].astype(f32)
        g = gt.T
        delta, nm, nv = _adam_math(w_ref[...], g, m_ref[...], v_ref[...])
        g_ref[...] = g
        d_ref[...] = delta
        nm_ref[...] = nm
        nv_ref[...] = nv

    wspec = pl.BlockSpec((None, R, C), lambda l: (l, 0, 0))
    return pl.pallas_call(
        body, grid=(L,),
        in_specs=[wspec, wspec, wspec, pl.BlockSpec((NDEV, None, C, R), lambda l: (0, l, 0, 0))],
        out_specs=[wspec] * 4, out_shape=[jax.ShapeDtypeStruct((L, R, C), f32)] * 4,
        compiler_params=_cparams(("parallel",)), name=name,
    )(w, m, v, recv)


def adamw_plain(w, m, v, g, *, name):
    R, C = w.shape

    def body(w_ref, m_ref, v_ref, g_ref, d_ref, nm_ref, nv_ref):
        delta, nm, nv = _adam_math(w_ref[...], g_ref[...], m_ref[...], v_ref[...])
        d_ref[...] = delta
        nm_ref[...] = nm
        nv_ref[...] = nv

    return pl.pallas_call(
        body, out_shape=[jax.ShapeDtypeStruct((R, C), f32)] * 3, name=name,
    )(w, m, v, g)


def _chunk_causal_mask():
    pos = jnp.arange(GBLK)
    return (pos[None, :] // CHUNK) <= (pos[:, None] // CHUNK)


def kernel(x, norm_mix_g, norm_ffn_g, final_norm_g, a_w_in, a_v_norm_g, a_w_s, a_b_s, a_w_out, b_w_in, b_conv_w, b_w_out, c_w_in, c_w_grp, c_scale, c_w_out, f_w_gate, f_w_up, f_w_down, loss_target, m_norm_mix_g, m_norm_ffn_g, m_final_norm_g, m_a_w_in, m_a_v_norm_g, m_a_w_s, m_a_b_s, m_a_w_out, m_b_w_in, m_b_conv_w, m_b_w_out, m_c_w_in, m_c_w_grp, m_c_scale, m_c_w_out, m_f_w_gate, m_f_w_up, m_f_w_down, v_norm_mix_g, v_norm_ffn_g, v_final_norm_g, v_a_w_in, v_a_v_norm_g, v_a_w_s, v_a_b_s, v_a_w_out, v_b_w_in, v_b_conv_w, v_b_w_out, v_c_w_in, v_c_w_grp, v_c_scale, v_c_w_out, v_f_w_gate, v_f_w_up, v_f_w_down):
    B, S, _ = x.shape
    T = B * S
    tm = min(512, S)
    tms = min(512, T)
    tma = min(256, S)
    tmf = min(256, T)
    me = _slot(lax.axis_index("x"), lax.axis_index("y"), lax.axis_index("c"))

    def tb(w):
        return jnp.transpose(w, (0, 2, 1)).astype(bf16)

    small = jnp.concatenate([a_v_norm_g, b_conv_w[0], c_scale, jnp.zeros((2, GBLK), f32)], axis=0)
    started = [gather_start([tb(a_w_in[0:1])[0], a_w_out[0].astype(bf16), small], me, None, name="gather_start_0")]
    tok = started[0][4]

    def ffn_shards(l, z):
        return [tb(f_w_gate[l:l + 1] + z)[0], tb(f_w_up[l:l + 1] + z)[0], (f_w_down[l] + z).astype(bf16)]

    started.append(gather_start(ffn_shards(0, tok[0, 0]), me, tok, name="gather_start_1"))
    tok = started[-1][4]

    def start_later_stages(z, after):
        stage_shards = [
            [tb(b_w_in + z)[0], (b_w_out[0] + z).astype(bf16)] + ffn_shards(1, z),
            [(c_w_in[0] + z).astype(bf16), (c_w_grp[0] + z).astype(bf16), (c_w_out[0] + z).astype(bf16)] + ffn_shards(2, z),
            [tb(a_w_in[1:2] + z)[0], (a_w_out[1] + z).astype(bf16)] + ffn_shards(3, z),
        ]
        for q, shards in enumerate(stage_shards):
            started.append(gather_start(shards, me, after, name=f"gather_start_{q + 2}"))
            after = started[-1][4]

    def stage_weights(s, after):
        send_sems, recv_sems, shards, lands, _ = started[s]
        _, lands = gather_wait(send_sems, recv_sems, shards, lands, after, name=f"gather_wait_{s}")
        lands = gather_finish(lands, name=f"gather_finish_{s}")
        return [t.reshape(-1, t.shape[-1]) if t.ndim == 3 else t for t in lands]

    wm_masked = jnp.where(_chunk_causal_mask()[None, None], a_w_s, 0.0).astype(bf16)
    bsb = jnp.broadcast_to(a_b_s[:, :, :, None], a_b_s.shape + (GBLK,))

    xc = x.reshape(T, D)
    saved, weights = [], []
    for i in range(DEPTH):
        kind, j = i % 3, i // 3
        gmix = norm_mix_g[i][None]
        if i == 0:
            w_in, w_out, small_g = stage_weights(0, tok)
            small_full = jnp.transpose(small_g.reshape(NDEV, 8, GBLK), (1, 0, 2)).reshape(8, D)
            gv_full, cw_full, cs_full = small_full[0:2], small_full[2:5], small_full[5:6]
            w = [w_in, w_out]
        else:
            w = stage_weights(i + 1, xc)
        if kind == 0:
            w_in = w[0]
            p, h, y, tok_a = a_fwd(xc, gmix, w_in, gv_full[j][None], wm_masked[j], bsb[j], tm=tma, name=f"a_fwd_{i}")
            if i == 0:
                start_later_stages(tok_a[0, 0], tok_a)
        elif kind == 1:
            w_in, w_out = w[:2]
            p, h = norm_mm(xc, gmix, w_in, tm=tms, transposed=True, name=f"b_in_{i}")
            y = b_mid_fwd(p, cw_full, tm=tm, S=S, name=f"b_mid_{i}")
        else:
            w_in, w_grp, w_out = w[:3]
            p, h = norm_mm(xc, gmix, w_in, tm=tms, transposed=False, name=f"c_in_{i}")
            y = c_mid_fwd(p, w_grp, cs_full, tm=tm, S=S, name=f"c_mid_{i}")
        if i == 0:
            w = w + stage_weights(1, y)
        weights.append(w)
        w_out, w_fg, w_fu, w_fd = w[-4:]
        x1, x2, pg, pu, a, h2 = ffn_fwd(y, w_out, xc, norm_ffn_g[i][None], w_fg, w_fu, w_fd, tm=tmf, name=f"f_fwd_{i}")
        saved.append((xc, p, h, y, x1, pg, pu, h2, a))
        xc = x2

    loss_blk, dx, d_final = final_loss(xc, final_norm_g[None], loss_target.reshape(T, D), tm=tms, name="final_loss")
    loss = lax.psum(loss_blk[0, 0], AXES)

    zone_shapes = [(2, 256, D), (2, GBLK, D), (1, 384, D), (1, GBLK, D), (1, GBLK, D), (1, GBLK, C_GDIM), (1, GBLK, D),
                   (DEPTH, F_LOC, D), (DEPTH, F_LOC, D), (DEPTH, F_LOC, D)]
    Z_AIN, Z_AOUT, Z_BIN, Z_BOUT, Z_CIN, Z_CGRP, Z_COUT, Z_FG, Z_FU, Z_FD = range(10)
    lands = [lax.empty((NDEV,) + s, bf16) for s in zone_shapes]
    parts = []

    def exchange(items, after, name):
        nonlocal lands
        for g, zi, l in items:
            own = lax.dynamic_index_in_dim(g, me, axis=0, keepdims=True)[None]
            lands[zi] = lax.dynamic_update_slice(lands[zi], own, (me, l) + (0,) * (g.ndim - 1))
        idx = sorted({zi for _, zi, _ in items})
        local = [(g, idx.index(zi), l) for g, zi, l in items]
        send_sems, recv_sems, gs, zs, token = exchange_start(local, [lands[zi] for zi in idx], after, name=name)
        for q, zi in enumerate(idx):
            lands[zi] = zs[q]
        parts.append(([(None, zi, l) for _, zi, l in items], send_sems, recv_sems, gs))
        return token

    d_mix, d_ffn = [None] * DEPTH, [None] * DEPTH
    d_gv, d_wm, d_bs = [None] * 2, [None] * 2, [None] * 2
    tok = None
    for i in reversed(range(DEPTH)):
        kind, j = i % 3, i // 3
        xin, p, h, y, x1, pg, pu, h2, a = saved[i]
        w = weights[i]
        w_in, w_out = w[0], w[-4]
        w_fg, w_fu, w_fd = w[-3:]
        dpg, dpu, dxb, dx1, d_ffn[i], dy, dx1b = ffn_bwd(dx, w_fd, pg, pu, w_fg, w_fu, x1, norm_ffn_g[i][None], tok, w_out,
                                                         tm=tmf, name=f"f_bwd_{i}")
        gw_g, gw_u, gw_d = ffn_dw(dpg, dpu, a, h2, dxb, name=f"f_dw_{i}")
        tok = exchange([(gw_g, Z_FG, i), (gw_u, Z_FU, i), (gw_d, Z_FD, i)], tok, f"exchange_start_ffn_{i}")
        gmix = norm_mix_g[i][None]
        if kind == 0:
            dp, d_gv[j], d_wm[j], d_bs[j], dx, d_mix[i] = a_mid_bwd(
                p, dy, gv_full[j][None], wm_masked[j], bsb[j], w_in, xin, gmix, tok, dx1, tm=tma, name=f"a_bwd_{i}")
            gw_in, gw_out = mm_tn2(dp, h, y, dx1b, name=f"a_dw_{i}")
            items = [(gw_in, Z_AIN, j), (gw_out, Z_AOUT, j)]
        elif kind == 1:
            dp, d_cw, dx, d_mix[i] = b_mid_bwd(p, dy, cw_full, w_in, xin, gmix, tok, dx1, tm=tm, S=S, name=f"b_bwd_{i}")
            gw_in, gw_out = mm_tn2(dp, h, y, dx1b, name=f"b_dw_{i}")
            items = [(gw_in, Z_BIN, 0), (gw_out, Z_BOUT, 0)]
        else:
            dp, d_cs, d_wgrp, dx, d_mix[i] = c_mid_bwd(p, dy, w[1], cs_full, w_in, xin, gmix, tok, dx1, tm=tm, S=S,
                                                       name=f"c_bwd_{i}")
            gw_cgrp = jnp.transpose(d_wgrp.reshape(C_GROUPS, NDEV, C_GDIM // NDEV, C_GDIM), (1, 0, 2, 3)).astype(bf16)
            gw_in, gw_out = mm_tn2(h, dp, y, dx1b, name=f"c_dw_{i}")
            items = [(gw_in, Z_CIN, 0), (gw_cgrp.reshape(NDEV, GBLK, C_GDIM), Z_CGRP, 0), (gw_out, Z_COUT, 0)]
        tok = exchange(items, tok, f"exchange_start_mix_{i}")
    grad_x = dx.reshape(B, S, D)

    small_parts = [
        jnp.concatenate(d_mix, axis=0).reshape(-1, GBLK),
        jnp.concatenate(d_ffn, axis=0).reshape(-1, GBLK),
        d_final.reshape(-1, GBLK),
        jnp.concatenate(d_gv, axis=0).reshape(-1, GBLK),
        jnp.where(_chunk_causal_mask()[None, None], jnp.stack(d_wm), 0.0).reshape(-1, GBLK),
        jnp.stack(d_bs).reshape(-1, GBLK),
        d_cw.reshape(-1, GBLK),
        d_cs.reshape(-1, GBLK),
    ]
    small_rows = [q.shape[0] for q in small_parts]
    sg_send, sg_recv, sg_shards, sg_lands, _ = gather_start(
        [jnp.concatenate(small_parts, axis=0)], me, tok, name="gather_small_start")

    def plain(w, m, v, g, name):
        shp = w.shape
        w2, m2, v2, g2 = (t.reshape(-1, shp[-1]) for t in (w, m, v, g))
        return tuple(t.reshape(shp) for t in adamw_plain(w2, m2, v2, g2, name=name))

    def sliced(w, m, v, recv, name):
        shp = w.shape
        w3, m3, v3 = (t.reshape((shp[0], -1, shp[-1])) for t in (w, m, v))
        return tuple(t.reshape(shp) for t in adamw_slices(w3, m3, v3, recv, name=name))

    lands = list(exchange_wait(parts[:6], lands, dx, name="exchange_wait"))
    _, _, r_bin, r_bout, r_cin, r_cgrp, r_cout, r_fg, r_fu, r_fd = lands
    res = {}
    res["b_w_in"] = adamw_slices_t(b_w_in, m_b_w_in, v_b_w_in, r_bin, name="adam_b_w_in")
    res["b_w_out"] = sliced(b_w_out, m_b_w_out, v_b_w_out, r_bout, "adam_b_w_out")
    res["c_w_in"] = sliced(c_w_in, m_c_w_in, v_c_w_in, r_cin, "adam_c_w_in")
    res["c_w_grp"] = sliced(c_w_grp, m_c_w_grp, v_c_w_grp, r_cgrp, "adam_c_w_grp")
    res["c_w_out"] = sliced(c_w_out, m_c_w_out, v_c_w_out, r_cout, "adam_c_w_out")
    def tr(t):
        return jnp.transpose(t, (0, 2, 1))

    f_ws = [tr(f_w_gate), tr(f_w_up), f_w_down]
    f_ms = [tr(m_f_w_gate), tr(m_f_w_up), m_f_w_down]
    f_vs = [tr(v_f_w_gate), tr(v_f_w_up), v_f_w_down]
    ff = adamw_families(f_ws, f_ms, f_vs, [r_fg, r_fu, r_fd], tc=256, l0=1, name="adam_ffn_123")
    big = [res[n][1] for n in ("b_w_in", "b_w_out", "c_w_in", "c_w_grp", "c_w_out")] + [ff[1], ff[5], ff[9]]

    ffn0_items, ffn0_send, ffn0_recv, ffn0_gs = parts[6]
    ffn0_zones = [Z_FG, Z_FU, Z_FD]
    ffn0_part = ([(None, ffn0_zones.index(zi), l) for _, zi, l in ffn0_items], ffn0_send, ffn0_recv, ffn0_gs)
    r_fg, r_fu, r_fd = exchange_wait([ffn0_part], [lands[zi] for zi in ffn0_zones], big, name="exchange_wait_ffn0")
    ff = adamw_families(f_ws, f_ms, f_vs, [r_fg, r_fu, r_fd], tc=256, l0=0, nl=1, prev=ff, name="adam_ffn_0")
    res["f_w_gate"], res["f_w_up"], res["f_w_down"] = tuple(tr(t) for t in ff[0:4]), tuple(tr(t) for t in ff[4:8]), tuple(ff[8:12])

    last_items, last_send, last_recv, last_gs = parts[7]
    last_zones = [Z_AIN, Z_AOUT]
    last_part = ([(None, last_zones.index(zi), l) for _, zi, l in last_items], last_send, last_recv, last_gs)
    r_ain, r_aout = exchange_wait([last_part], [lands[zi] for zi in last_zones], [ff[1], ff[5], ff[9]], name="exchange_wait_last")
    res["a_w_in"] = adamw_slices_t(a_w_in, m_a_w_in, v_a_w_in, r_ain, name="adam_a_w_in")
    res["a_w_out"] = sliced(a_w_out, m_a_w_out, v_a_w_out, r_aout, "adam_a_w_out")

    _, sg_lands = gather_wait(sg_send, sg_recv, sg_shards, sg_lands, [res["a_w_in"][1], res["a_w_out"][1]], name="gather_small_wait")
    (gs_all,) = gather_finish(sg_lands, name="gather_small_finish")
    gs = sum8(gs_all, name="sum_small_grads")
    offs = [0]
    for r in small_rows:
        offs.append(offs[-1] + r)
    sp = [gs[offs[q]:offs[q + 1]] for q in range(len(small_parts))]
    grad_norm_mix_g = sp[0].reshape(DEPTH, D)
    grad_norm_ffn_g = sp[1].reshape(DEPTH, D)
    grad_final_norm_g = sp[2].reshape(D)
    grad_a_w_s = sp[4].reshape(a_w_s.shape)
    grad_a_b_s = sp[5].reshape(a_b_s.shape)

    def my_cols(full):
        return lax.dynamic_slice_in_dim(full, me * GBLK, GBLK, axis=1)

    grad_a_v_norm_g = my_cols(sp[3].reshape(2, D))
    grad_b_conv_w = my_cols(sp[6].reshape(3, D))[None]
    grad_c_scale = my_cols(sp[7].reshape(1, D))
    res["norm_mix_g"] = (grad_norm_mix_g,) + plain(norm_mix_g, m_norm_mix_g, v_norm_mix_g, grad_norm_mix_g, "adam_norm_mix")
    res["norm_ffn_g"] = (grad_norm_ffn_g,) + plain(norm_ffn_g, m_norm_ffn_g, v_norm_ffn_g, grad_norm_ffn_g, "adam_norm_ffn")
    res["final_norm_g"] = (grad_final_norm_g,) + tuple(
        t.reshape(D) for t in plain(final_norm_g[None], m_final_norm_g[None], v_final_norm_g[None], grad_final_norm_g[None], "adam_final"))
    res["a_v_norm_g"] = (grad_a_v_norm_g,) + plain(a_v_norm_g, m_a_v_norm_g, v_a_v_norm_g, grad_a_v_norm_g, "adam_a_v_norm")
    res["a_w_s"] = (grad_a_w_s,) + plain(a_w_s, m_a_w_s, v_a_w_s, grad_a_w_s, "adam_a_w_s")
    res["a_b_s"] = (grad_a_b_s,) + plain(a_b_s, m_a_b_s, v_a_b_s, grad_a_b_s, "adam_a_b_s")
    res["b_conv_w"] = (grad_b_conv_w,) + plain(b_conv_w, m_b_conv_w, v_b_conv_w, grad_b_conv_w, "adam_b_conv")
    res["c_scale"] = (grad_c_scale,) + plain(c_scale, m_c_scale, v_c_scale, grad_c_scale, "adam_c_scale")

    order = ["norm_mix_g", "norm_ffn_g", "final_norm_g", "a_w_in", "a_v_norm_g", "a_w_s", "a_b_s", "a_w_out", "b_w_in",
             "b_conv_w", "b_w_out", "c_w_in", "c_w_grp", "c_scale", "c_w_out", "f_w_gate", "f_w_up", "f_w_down"]
    return (loss, grad_x, *[res[n][0] for n in order], *[res[n][1] for n in order],
            *[res[n][2] for n in order], *[res[n][3] for n in order])
```

```python
import jax
import jax.numpy as jnp
from jax import lax
from jax.experimental import pallas as pl
from jax.experimental.pallas import tpu as pltpu

f32 = jnp.float32
bf16 = jnp.bfloat16

NDEV = 8
D = 1024
EPS = 1e-6
CHUNK = 64
GBLK = 128
A_GROUPS = 8
C_GROUPS = 4
C_GDIM = 256
POOL_WINDOWS = (2, 4, 8, 16)
HALO = 16
COLS = 256
F_LOC = 352
MXU_COLS = 256
DEPTH = 4
AXES = ("x", "y", "c")
MESH = pl.DeviceIdType.MESH

ADAM_LR = 0.001
ADAM_B1 = 0.9
ADAM_B2 = 0.999
ADAM_EPS = 1e-08
ADAM_WD = 0.01
ADAM_STEP = 10

VMEM_LIMIT = 56 * 1024 * 1024


def _cparams(sem):
    return pltpu.CompilerParams(dimension_semantics=sem, vmem_limit_bytes=VMEM_LIMIT)


def _gelu(z):
    return 0.5 * z * (1.0 + lax.erf(z * 0.7071067811865476))


def _gelu_cdf(z):
    return 0.5 * (1.0 + lax.erf(z * 0.7071067811865476))


def _gelu_grad(z, cdf):
    return cdf + z * jnp.exp(-0.5 * z * z) * 0.3989422804014327


def _dot(a, b):
    return jnp.dot(a, b, preferred_element_type=f32)


def _dot_nt(a, b):
    return lax.dot_general(a, b, (((1,), (1,)), ((), ())), preferred_element_type=f32)


def _dot_tn(a, b):
    return lax.dot_general(a, b, (((0,), (0,)), ((), ())), preferred_element_type=f32)


def _col_chunks(n, width=1024):
    return [(c, min(c + width, n)) for c in range(0, n, width)]


def norm_mm(x, g, w, *, tm, transposed, name):
    T = x.shape[0]
    n = w.shape[0] if transposed else w.shape[1]

    def body(x_ref, g_ref, w_ref, p_ref, h_ref):
        xv = x_ref[...]
        r = lax.rsqrt(jnp.mean(xv * xv, axis=-1, keepdims=True) + EPS)
        h = (xv * r * g_ref[...]).astype(bf16)
        h_ref[...] = h
        p = _dot_nt(h, w_ref[...]) if transposed else _dot(h, w_ref[...])
        p_ref[...] = p.astype(bf16)

    return pl.pallas_call(
        body, grid=(T // tm,),
        in_specs=[pl.BlockSpec((tm, D), lambda i: (i, 0)), pl.BlockSpec((1, D), lambda i: (0, 0)),
                  pl.BlockSpec(w.shape, lambda i: (0, 0), pipeline_mode=pl.Buffered(1))],
        out_specs=[pl.BlockSpec((tm, n), lambda i: (i, 0)), pl.BlockSpec((tm, D), lambda i: (i, 0))],
        out_shape=[jax.ShapeDtypeStruct((T, n), bf16), jax.ShapeDtypeStruct((T, D), bf16)],
        compiler_params=_cparams(("parallel",)), name=name,
    )(x, g, w)


def ffn_fwd(y, w_out, xin, g, wtg, wtu, wd, *, tm, name):
    T, K = y.shape
    H = wtg.shape[0]

    def body(y_ref, wo_ref, xin_ref, g_ref, wg_ref, wu_ref, wd_ref, x_ref, x2_ref, pg_ref, pu_ref, a_ref, h_ref):
        xv = xin_ref[...] + _dot(y_ref[...], wo_ref[...])
        x_ref[...] = xv
        r = lax.rsqrt(jnp.mean(xv * xv, axis=-1, keepdims=True) + EPS)
        h = (xv * r * g_ref[...]).astype(bf16)
        h_ref[...] = h
        out = None
        for c0, c1 in _col_chunks(H):
            gate = _dot_nt(h, wg_ref[c0:c1, :])
            up = _dot_nt(h, wu_ref[c0:c1, :])
            pg_ref[:, c0:c1] = gate.astype(bf16)
            pu_ref[:, c0:c1] = up.astype(bf16)
            ac = (gate * jax.nn.sigmoid(gate) * up).astype(bf16)
            a_ref[:, c0:c1] = ac
            part = _dot(ac, wd_ref[c0:c1, :])
            out = part if out is None else out + part
        x2_ref[...] = xv + out

    wspec = pl.BlockSpec((H, D), lambda i: (0, 0), pipeline_mode=pl.Buffered(1))
    hspec = pl.BlockSpec((tm, H), lambda i: (i, 0))
    dspec = pl.BlockSpec((tm, D), lambda i: (i, 0))
    return pl.pallas_call(
        body, grid=(T // tm,),
        in_specs=[pl.BlockSpec((tm, K), lambda i: (i, 0)),
                  pl.BlockSpec((K, D), lambda i: (0, 0), pipeline_mode=pl.Buffered(1)),
                  dspec, pl.BlockSpec((1, D), lambda i: (0, 0)), wspec, wspec, wspec],
        out_specs=[dspec, dspec, hspec, hspec, hspec, dspec],
        out_shape=[jax.ShapeDtypeStruct((T, D), f32)] * 2 + [jax.ShapeDtypeStruct((T, H), bf16)] * 3
        + [jax.ShapeDtypeStruct((T, D), bf16)],
        compiler_params=_cparams(("parallel",)), name=name,
    )(y, w_out, xin, g, wtg, wtu, wd)


def ffn_bwd(dx, wd, pg, pu, wtg, wtu, x, g, tok, w_next, *, tm, name):
    T = dx.shape[0]
    H = wd.shape[0]
    kn = w_next.shape[0]
    toks = [] if tok is None else [tok]

    def body(dx_ref, wd_ref, pg_ref, pu_ref, wg_ref, wu_ref, x_ref, g_ref, *rest):
        wn_ref = rest[len(toks)]
        dg_ref, du_ref, dxb_ref, dx1_ref, dgn_ref, dy_ref, dx1b_ref = rest[len(toks) + 1:]

        @pl.when(pl.program_id(0) == 0)
        def _():
            dgn_ref[...] = jnp.zeros(dgn_ref.shape, f32)

        dxv = dx_ref[...]
        dxb = dxv.astype(bf16)
        dxb_ref[...] = dxb
        dh = None
        for c0, c1 in _col_chunks(H):
            da = _dot_nt(dxb, wd_ref[c0:c1, :])
            gate = pg_ref[:, c0:c1].astype(f32)
            up = pu_ref[:, c0:c1].astype(f32)
            sg = jax.nn.sigmoid(gate)
            dgc = (da * up * (sg * (1.0 + gate * (1.0 - sg)))).astype(bf16)
            duc = (da * gate * sg).astype(bf16)
            dg_ref[:, c0:c1] = dgc
            du_ref[:, c0:c1] = duc
            part = _dot(dgc, wg_ref[c0:c1, :]) + _dot(duc, wu_ref[c0:c1, :])
            dh = part if dh is None else dh + part
        gv = g_ref[...] + rest[0][0:1, 0:1] if toks else g_ref[...]
        xv = x_ref[...]
        r = lax.rsqrt(jnp.mean(xv * xv, axis=-1, keepdims=True) + EPS)
        xhat = xv * r
        dgn_ref[...] += jnp.sum(dh * xhat, axis=0, keepdims=True)
        dxhat = dh * gv
        dx1 = dxv + r * (dxhat - xhat * jnp.mean(dxhat * xhat, axis=-1, keepdims=True))
        dx1_ref[...] = dx1
        dx1b = dx1.astype(bf16)
        dx1b_ref[...] = dx1b
        dy_ref[...] = _dot_nt(dx1b, wn_ref[...]).astype(bf16)

    dspec = pl.BlockSpec((tm, D), lambda i: (i, 0))
    hspec = pl.BlockSpec((tm, H), lambda i: (i, 0))
    wspec = pl.BlockSpec((H, D), lambda i: (0, 0), pipeline_mode=pl.Buffered(1))
    gspec = pl.BlockSpec((1, D), lambda i: (0, 0))
    return pl.pallas_call(
        body, grid=(T // tm,),
        in_specs=[dspec, wspec, hspec, hspec, wspec, wspec, dspec, gspec]
        + [pl.BlockSpec((8, 128), lambda i: (0, 0))] * len(toks)
        + [pl.BlockSpec((kn, D), lambda i: (0, 0), pipeline_mode=pl.Buffered(1))],
        out_specs=[hspec, hspec, dspec, dspec, gspec, pl.BlockSpec((tm, kn), lambda i: (i, 0)), dspec],
        out_shape=[jax.ShapeDtypeStruct((T, H), bf16)] * 2 + [jax.ShapeDtypeStruct((T, D), bf16),
                   jax.ShapeDtypeStruct((T, D), f32), jax.ShapeDtypeStruct((1, D), f32),
                   jax.ShapeDtypeStruct((T, kn), bf16), jax.ShapeDtypeStruct((T, D), bf16)],
        compiler_params=_cparams(("arbitrary",)), name=name,
    )(dx, wd, pg, pu, wtg, wtu, x, g, *toks, w_next)


def mm_tn2(a1, b1, a2, b2, *, name):
    T, K1 = a1.shape
    K2 = a2.shape[1]
    n1, n2 = b1.shape[1], b2.shape[1]
    last2 = K2 // MXU_COLS - 1

    def body(a1_ref, b1_ref, a2_ref, b2_ref, o1_ref, o2_ref):
        o1_ref[...] = _dot_tn(a1_ref[...], b1_ref[...]).astype(bf16)

        @pl.when(pl.program_id(0) <= last2)
        def _():
            o2_ref[...] = _dot_tn(a2_ref[...], b2_ref[...]).astype(bf16)

    o1, o2 = pl.pallas_call(
        body, grid=(K1 // MXU_COLS,),
        in_specs=[pl.BlockSpec((T, MXU_COLS), lambda d: (0, d)),
                  pl.BlockSpec((T, n1), lambda d: (0, 0), pipeline_mode=pl.Buffered(1)),
                  pl.BlockSpec((T, MXU_COLS), lambda d: (0, jnp.minimum(d, last2))),
                  pl.BlockSpec((T, n2), lambda d: (0, 0), pipeline_mode=pl.Buffered(1))],
        out_specs=[pl.BlockSpec((MXU_COLS, n1), lambda d: (d, 0)),
                   pl.BlockSpec((MXU_COLS, n2), lambda d: (jnp.minimum(d, last2), 0))],
        out_shape=[jax.ShapeDtypeStruct((K1, n1), bf16), jax.ShapeDtypeStruct((K2, n2), bf16)],
        compiler_params=_cparams(("arbitrary",)), name=name,
    )(a1, b1, a2, b2)
    return o1.reshape(NDEV, K1 // NDEV, n1), o2.reshape(NDEV, K2 // NDEV, n2)


def ffn_dw(dpg, dpu, a, h2, dxb, *, name):
    T, H = dpg.shape

    def body(g_ref, u_ref, a_ref, h_ref, x_ref, og_ref, ou_ref, od_ref):
        h = h_ref[...]
        og_ref[...] = _dot_tn(g_ref[...], h).astype(bf16)
        ou_ref[...] = _dot_tn(u_ref[...], h).astype(bf16)
        od_ref[...] = _dot_tn(a_ref[...], x_ref[...]).astype(bf16)

    blk = pl.BlockSpec((T, MXU_COLS), lambda d: (0, d))
    full = pl.BlockSpec((T, D), lambda d: (0, 0), pipeline_mode=pl.Buffered(1))
    out = pl.BlockSpec((MXU_COLS, D), lambda d: (d, 0))
    outs = pl.pallas_call(
        body, grid=(H // MXU_COLS,),
        in_specs=[blk, blk, blk, full, full], out_specs=[out] * 3,
        out_shape=[jax.ShapeDtypeStruct((H, D), bf16)] * 3,
        compiler_params=_cparams(("parallel",)), name=name,
    )(dpg, dpu, a, h2, dxb)
    return [o.reshape(NDEV, H // NDEV, D) for o in outs]


def _normbwd_tail(dh, x_ref, g_ref, tok_ref, dres_ref, dx_ref, dgn_ref):
    gv = g_ref[...] + tok_ref[0:1, 0:1]
    xv = x_ref[...]
    r = lax.rsqrt(jnp.mean(xv * xv, axis=-1, keepdims=True) + EPS)
    xhat = xv * r
    dgn_ref[...] += jnp.sum(dh * xhat, axis=0, keepdims=True)
    dxhat = dh * gv
    dx_ref[...] = dres_ref[...] + r * (dxhat - xhat * jnp.mean(dxhat * xhat, axis=-1, keepdims=True))


def _normbwd_specs(tm, w):
    dspec = pl.BlockSpec((tm, D), lambda i: (i, 0))
    gspec = pl.BlockSpec((1, D), lambda i: (0, 0))
    ins = [pl.BlockSpec(w.shape, lambda i: (0, 0), pipeline_mode=pl.Buffered(1)), dspec, gspec,
           pl.BlockSpec((8, 128), lambda i: (0, 0)), dspec]
    return ins, [dspec, gspec]


def _prev_halo_spec(tm, ncol):
    return pl.BlockSpec((HALO, ncol), lambda i: (jnp.maximum(i * (tm // HALO) - 1, 0), 0))


def _next_halo_spec(tm, ncol, T):
    return pl.BlockSpec((HALO, ncol), lambda i: (jnp.minimum((i + 1) * (tm // HALO), T // HALO - 1), 0))


def a_fwd(x, g, wt, gv, wm, bsb, *, tm, name):
    T = x.shape[0]

    def body(x_ref, g_ref, wt_ref, gv_ref, wm_ref, bs_ref, p_ref, h_ref, y_ref, tok_ref, vn_sc):
        tok_ref[...] = jnp.zeros(tok_ref.shape, f32)
        xv = x_ref[...]
        r = lax.rsqrt(jnp.mean(xv * xv, axis=-1, keepdims=True) + EPS)
        h = (xv * r * g_ref[...]).astype(bf16)
        h_ref[...] = h
        p_ref[...] = _dot_nt(h, wt_ref[...]).astype(bf16)
        v = _gelu(p_ref[:, D:].astype(f32))
        vc = v - jnp.mean(v, axis=-1, keepdims=True)
        var = jnp.mean(vc * vc, axis=-1, keepdims=True)
        vn_sc[...] = (vc * lax.rsqrt(var + EPS) * gv_ref[...]).astype(bf16)
        for g in range(A_GROUPS):
            cs = slice(g * GBLK, (g + 1) * GBLK)
            for r in range(tm // GBLK):
                rs = slice(r * GBLK, (r + 1) * GBLK)
                sv = _dot(wm_ref[g], vn_sc[rs, cs]) + bs_ref[g]
                y_ref[rs, cs] = (_gelu(p_ref[rs, cs].astype(f32)) * sv).astype(bf16)

    dspec = pl.BlockSpec((tm, D), lambda i: (i, 0))
    gspec = pl.BlockSpec((1, D), lambda i: (0, 0))
    return pl.pallas_call(
        body, grid=(T // tm,),
        in_specs=[dspec, gspec, pl.BlockSpec(wt.shape, lambda i: (0, 0), pipeline_mode=pl.Buffered(1)), gspec,
                  pl.BlockSpec((A_GROUPS, GBLK, GBLK), lambda i: (0, 0, 0)),
                  pl.BlockSpec((A_GROUPS, GBLK, GBLK), lambda i: (0, 0, 0))],
        out_specs=[pl.BlockSpec((tm, 2 * D), lambda i: (i, 0)), dspec, dspec, pl.BlockSpec((8, 128), lambda i: (0, 0))],
        out_shape=[jax.ShapeDtypeStruct((T, 2 * D), bf16), jax.ShapeDtypeStruct((T, D), bf16),
                   jax.ShapeDtypeStruct((T, D), bf16), jax.ShapeDtypeStruct((8, 128), f32)],
        scratch_shapes=[pltpu.VMEM((tm, D), bf16)],
        compiler_params=_cparams(("arbitrary",)), name=name,
    )(x, g, wt, gv, wm, bsb)


def a_mid_bwd(p, dy, gv, wm, bsb, wt, x, g, tok, dres, *, tm, name):
    T = p.shape[0]

    def body(p_ref, dy_ref, gv_ref, wm_ref, bs_ref, wt_ref, x_ref, g_ref, tok_ref, dres_ref,
             dp_ref, dgv_ref, dwm_ref, dbs_ref, dx_ref, dgn_ref, vn_sc, dvn_sc):
        @pl.when(pl.program_id(0) == 0)
        def _():
            dgn_ref[...] = jnp.zeros(dgn_ref.shape, f32)
            dgv_ref[...] = jnp.zeros(dgv_ref.shape, f32)
            dwm_ref[...] = jnp.zeros(dwm_ref.shape, f32)
            dbs_ref[...] = jnp.zeros(dbs_ref.shape, f32)

        zv = p_ref[:, D:].astype(f32)
        cdf_v = _gelu_cdf(zv)
        v = zv * cdf_v
        vc = v - jnp.mean(v, axis=-1, keepdims=True)
        rstd = lax.rsqrt(jnp.mean(vc * vc, axis=-1, keepdims=True) + EPS)
        vhat = vc * rstd
        vn_sc[...] = (vhat * gv_ref[...]).astype(bf16)
        for g in range(A_GROUPS):
            cs = slice(g * GBLK, (g + 1) * GBLK)
            dwm = jnp.zeros((GBLK, GBLK), f32)
            dbs = jnp.zeros((GBLK, 1), f32)
            for r in range(tm // GBLK):
                rs = slice(r * GBLK, (r + 1) * GBLK)
                zu = p_ref[rs, cs].astype(f32)
                cdf_u = _gelu_cdf(zu)
                vn = vn_sc[rs, cs]
                sv = _dot(wm_ref[g], vn) + bs_ref[g]
                dyb = dy_ref[rs, cs].astype(f32)
                dsv = dyb * (zu * cdf_u)
                dsvb = dsv.astype(bf16)
                dp_ref[rs, cs] = (dyb * sv * _gelu_grad(zu, cdf_u)).astype(bf16)
                dwm += _dot_nt(dsvb, vn)
                dbs += jnp.sum(dsv, axis=1, keepdims=True)
                dvn_sc[rs, cs] = _dot_tn(wm_ref[g], dsvb)
            dwm_ref[g] += dwm
            dbs_ref[g] += dbs
        dvn = dvn_sc[...]
        dgv_ref[...] += jnp.sum(dvn * vhat, axis=0, keepdims=True)
        dvhat = dvn * gv_ref[...]
        dv = rstd * (dvhat - jnp.mean(dvhat, axis=-1, keepdims=True)
                     - vhat * jnp.mean(dvhat * vhat, axis=-1, keepdims=True))
        dp_ref[:, D:] = (dv * _gelu_grad(zv, cdf_v)).astype(bf16)
        _normbwd_tail(_dot(dp_ref[...], wt_ref[...]), x_ref, g_ref, tok_ref, dres_ref, dx_ref, dgn_ref)

    tail_in, tail_out = _normbwd_specs(tm, wt)
    return pl.pallas_call(
        body, grid=(T // tm,),
        in_specs=[pl.BlockSpec((tm, 2 * D), lambda i: (i, 0)), pl.BlockSpec((tm, D), lambda i: (i, 0)),
                  pl.BlockSpec((1, D), lambda i: (0, 0)),
                  pl.BlockSpec((A_GROUPS, GBLK, GBLK), lambda i: (0, 0, 0)),
                  pl.BlockSpec((A_GROUPS, GBLK, GBLK), lambda i: (0, 0, 0))] + tail_in,
        out_specs=[pl.BlockSpec((tm, 2 * D), lambda i: (i, 0)), pl.BlockSpec((1, D), lambda i: (0, 0)),
                   pl.BlockSpec((A_GROUPS, GBLK, GBLK), lambda i: (0, 0, 0)),
                   pl.BlockSpec((A_GROUPS, GBLK, 1), lambda i: (0, 0, 0))] + tail_out,
        out_shape=[jax.ShapeDtypeStruct((T, 2 * D), bf16), jax.ShapeDtypeStruct((1, D), f32),
                   jax.ShapeDtypeStruct((A_GROUPS, GBLK, GBLK), f32), jax.ShapeDtypeStruct((A_GROUPS, GBLK, 1), f32),
                   jax.ShapeDtypeStruct((T, D), f32), jax.ShapeDtypeStruct((1, D), f32)],
        scratch_shapes=[pltpu.VMEM((tm, D), bf16), pltpu.VMEM((tm, D), f32)],
        compiler_params=_cparams(("arbitrary",)), name=name,
    )(p, dy, gv, wm, bsb, wt, x, g, tok, dres)


def b_mid_fwd(p, cw, *, tm, S, name):
    T = p.shape[0]
    nst = S // tm

    def body(p_ref, ph_ref, cw_ref, y_ref, ext):
        first = (pl.program_id(0) % nst) == 0
        for c0 in range(0, D, COLS):
            cs, cc, cx = slice(c0, c0 + COLS), slice(D + c0, D + c0 + COLS), slice(2 * D + c0, 2 * D + c0 + COLS)
            q = p_ref[:, cc].astype(f32) * p_ref[:, cx].astype(f32)
            qh = ph_ref[:, cc].astype(f32) * ph_ref[:, cx].astype(f32)
            ext[0:HALO, cs] = jnp.where(first, 0.0, qh)
            ext[HALO:, cs] = q
            y = (cw_ref[0:1, cs] * ext[pl.ds(HALO - 2, tm), cs] + cw_ref[1:2, cs] * ext[pl.ds(HALO - 1, tm), cs]
                 + cw_ref[2:3, cs] * q)
            y_ref[:, cs] = (p_ref[:, cs].astype(f32) * y).astype(bf16)

    return pl.pallas_call(
        body, grid=(T // tm,),
        in_specs=[pl.BlockSpec((tm, 3 * D), lambda i: (i, 0)), _prev_halo_spec(tm, 3 * D),
                  pl.BlockSpec((3, D), lambda i: (0, 0))],
        out_specs=pl.BlockSpec((tm, D), lambda i: (i, 0)),
        out_shape=jax.ShapeDtypeStruct((T, D), bf16),
        scratch_shapes=[pltpu.VMEM((tm + HALO, D), f32)],
        compiler_params=_cparams(("parallel",)), name=name,
    )(p, p, cw)


def b_mid_bwd(p, dy, cw, wt, x, g, tok, dres, *, tm, S, name):
    T = p.shape[0]
    nst = S // tm

    def body(p_ref, ph_ref, pn_ref, dy_ref, dyn_ref, cw_ref, wt_ref, x_ref, g_ref, tok_ref, dres_ref,
             dp_ref, dcw_ref, dx_ref, dgn_ref, ext, ext2):
        i = pl.program_id(0)
        first = (i % nst) == 0
        last = (i % nst) == nst - 1

        @pl.when(i == 0)
        def _():
            dgn_ref[...] = jnp.zeros(dgn_ref.shape, f32)
            dcw_ref[...] = jnp.zeros(dcw_ref.shape, f32)

        for c0 in range(0, D, COLS):
            cs, cc, cx = slice(c0, c0 + COLS), slice(D + c0, D + c0 + COLS), slice(2 * D + c0, 2 * D + c0 + COLS)
            gb, gc, xt = p_ref[:, cs].astype(f32), p_ref[:, cc].astype(f32), p_ref[:, cx].astype(f32)
            q = gc * xt
            ext[0:HALO, cs] = jnp.where(first, 0.0, ph_ref[:, cc].astype(f32) * ph_ref[:, cx].astype(f32))
            ext[HALO:, cs] = q
            q2 = ext[pl.ds(HALO - 2, tm), cs]
            q1 = ext[pl.ds(HALO - 1, tm), cs]
            y = cw_ref[0:1, cs] * q2 + cw_ref[1:2, cs] * q1 + cw_ref[2:3, cs] * q
            dyo = dy_ref[:, cs].astype(f32)
            dp_ref[:, cs] = (dyo * y).astype(bf16)
            dyc = dyo * gb
            ext2[0:tm, cs] = dyc
            ext2[tm:, cs] = jnp.where(last, 0.0, dyn_ref[:, cs].astype(f32) * pn_ref[:, cs].astype(f32))
            dq = cw_ref[2:3, cs] * dyc + cw_ref[1:2, cs] * ext2[pl.ds(1, tm), cs] + cw_ref[0:1, cs] * ext2[pl.ds(2, tm), cs]
            dp_ref[:, cc] = (dq * xt).astype(bf16)
            dp_ref[:, cx] = (dq * gc).astype(bf16)
            dcw_ref[0:1, cs] += jnp.sum(dyc * q2, axis=0, keepdims=True)
            dcw_ref[1:2, cs] += jnp.sum(dyc * q1, axis=0, keepdims=True)
            dcw_ref[2:3, cs] += jnp.sum(dyc * q, axis=0, keepdims=True)
        _normbwd_tail(_dot(dp_ref[...], wt_ref[...]), x_ref, g_ref, tok_ref, dres_ref, dx_ref, dgn_ref)

    tail_in, tail_out = _normbwd_specs(tm, wt)
    return pl.pallas_call(
        body, grid=(T // tm,),
        in_specs=[pl.BlockSpec((tm, 3 * D), lambda i: (i, 0)), _prev_halo_spec(tm, 3 * D), _next_halo_spec(tm, 3 * D, T),
                  pl.BlockSpec((tm, D), lambda i: (i, 0)), _next_halo_spec(tm, D, T),
                  pl.BlockSpec((3, D), lambda i: (0, 0))] + tail_in,
        out_specs=[pl.BlockSpec((tm, 3 * D), lambda i: (i, 0)), pl.BlockSpec((3, D), lambda i: (0, 0))] + tail_out,
        out_shape=[jax.ShapeDtypeStruct((T, 3 * D), bf16), jax.ShapeDtypeStruct((3, D), f32),
                   jax.ShapeDtypeStruct((T, D), f32), jax.ShapeDtypeStruct((1, D), f32)],
        scratch_shapes=[pltpu.VMEM((tm + HALO, D), f32), pltpu.VMEM((tm + HALO, D), f32)],
        compiler_params=_cparams(("arbitrary",)), name=name,
    )(p, p, p, dy, dy, cw, wt, x, g, tok, dres)


def _pool_counts(i, nst, tm, rows, row0, w):
    t = (i % nst) * tm + row0 + lax.broadcasted_iota(jnp.int32, (rows, 1), 0)
    return jnp.minimum(t + 1, w).astype(f32)


def _pool_diff(p_ref, ext, g, i, nst, tm):
    w = POOL_WINDOWS[g]
    cs = slice(g * C_GDIM, (g + 1) * C_GDIM)
    pg = p_ref[:, cs].astype(f32)
    s = pg
    for k in range(1, w):
        s = s + ext[pl.ds(HALO - k, tm), cs]
    return s / _pool_counts(i, nst, tm, tm, 0, w) - pg


def c_mid_fwd(p, wg, scale, *, tm, S, name):
    T = p.shape[0]
    nst = S // tm

    def body(p_ref, ph_ref, wg_ref, sc_ref, y_ref, ext):
        i = pl.program_id(0)
        first = (i % nst) == 0
        ext[0:HALO, :] = jnp.where(first, 0.0, ph_ref[...].astype(f32))
        ext[HALO:, :] = p_ref[...].astype(f32)
        for g in range(C_GROUPS):
            cs = slice(g * C_GDIM, (g + 1) * C_GDIM)
            dg = _pool_diff(p_ref, ext, g, i, nst, tm).astype(bf16)
            wv = wg_ref[:, g].reshape(C_GDIM, C_GDIM)
            y_ref[:, cs] = (_dot(dg, wv) * sc_ref[:, cs]).astype(bf16)

    return pl.pallas_call(
        body, grid=(T // tm,),
        in_specs=[pl.BlockSpec((tm, D), lambda i: (i, 0)), _prev_halo_spec(tm, D),
                  pl.BlockSpec((NDEV, C_GROUPS, C_GDIM // NDEV, C_GDIM), lambda i: (0, 0, 0, 0)),
                  pl.BlockSpec((1, D), lambda i: (0, 0))],
        out_specs=pl.BlockSpec((tm, D), lambda i: (i, 0)),
        out_shape=jax.ShapeDtypeStruct((T, D), bf16),
        scratch_shapes=[pltpu.VMEM((tm + HALO, D), f32)],
        compiler_params=_cparams(("parallel",)), name=name,
    )(p, p, wg, scale)


def c_mid_bwd(p, dy, wg, scale, w, x, g, tok, dres, *, tm, S, name):
    T = p.shape[0]
    nst = S // tm

    def body(p_ref, ph_ref, dy_ref, dyn_ref, wg_ref, sc_ref, w_ref, x_ref, g_ref, tok_ref, dres_ref,
             dp_ref, dsc_ref, dwg_ref, dx_ref, dgn_ref, ext, ext2):
        i = pl.program_id(0)
        first = (i % nst) == 0
        last = (i % nst) == nst - 1

        @pl.when(i == 0)
        def _():
            dgn_ref[...] = jnp.zeros(dgn_ref.shape, f32)
            dsc_ref[...] = jnp.zeros(dsc_ref.shape, f32)
            dwg_ref[...] = jnp.zeros(dwg_ref.shape, f32)

        ext[0:HALO, :] = jnp.where(first, 0.0, ph_ref[...].astype(f32))
        ext[HALO:, :] = p_ref[...].astype(f32)
        for g in range(C_GROUPS):
            w = POOL_WINDOWS[g]
            cs = slice(g * C_GDIM, (g + 1) * C_GDIM)
            dg = _pool_diff(p_ref, ext, g, i, nst, tm).astype(bf16)
            wv = wg_ref[:, g].reshape(C_GDIM, C_GDIM)
            dyo = dy_ref[:, cs].astype(f32)
            dsc_ref[:, cs] += jnp.sum(dyo * _dot(dg, wv), axis=0, keepdims=True)
            dyp = (dyo * sc_ref[:, cs]).astype(bf16)
            dypn = (dyn_ref[:, cs].astype(f32) * sc_ref[:, cs]).astype(bf16)
            dwg_ref[g] += _dot_tn(dg, dyp)
            dd = _dot_nt(dyp, wv)
            ddn = _dot_nt(dypn, wv)
            ext2[0:tm, cs] = dd / _pool_counts(i, nst, tm, tm, 0, w)
            ext2[tm:, cs] = jnp.where(last, 0.0, ddn / _pool_counts(i, nst, tm, HALO, tm, w))
            s = -dd
            for k in range(w):
                s = s + ext2[pl.ds(k, tm), cs]
            dp_ref[:, cs] = s.astype(bf16)
        _normbwd_tail(_dot_nt(dp_ref[...], w_ref[...]), x_ref, g_ref, tok_ref, dres_ref, dx_ref, dgn_ref)

    tail_in, tail_out = _normbwd_specs(tm, w)
    return pl.pallas_call(
        body, grid=(T // tm,),
        in_specs=[pl.BlockSpec((tm, D), lambda i: (i, 0)), _prev_halo_spec(tm, D),
                  pl.BlockSpec((tm, D), lambda i: (i, 0)), _next_halo_spec(tm, D, T),
                  pl.BlockSpec((NDEV, C_GROUPS, C_GDIM // NDEV, C_GDIM), lambda i: (0, 0, 0, 0)),
                  pl.BlockSpec((1, D), lambda i: (0, 0))] + tail_in,
        out_specs=[pl.BlockSpec((tm, D), lambda i: (i, 0)), pl.BlockSpec((1, D), lambda i: (0, 0)),
                   pl.BlockSpec((C_GROUPS, C_GDIM, C_GDIM), lambda i: (0, 0, 0))] + tail_out,
        out_shape=[jax.ShapeDtypeStruct((T, D), bf16), jax.ShapeDtypeStruct((1, D), f32),
                   jax.ShapeDtypeStruct((C_GROUPS, C_GDIM, C_GDIM), f32),
                   jax.ShapeDtypeStruct((T, D), f32), jax.ShapeDtypeStruct((1, D), f32)],
        scratch_shapes=[pltpu.VMEM((tm + HALO, D), f32), pltpu.VMEM((tm + HALO, D), f32)],
        compiler_params=_cparams(("arbitrary",)), name=name,
    )(p, p, dy, dy, wg, scale, w, x, g, tok, dres)


def final_loss(x, g, tgt, *, tm, name):
    T = x.shape[0]

    def body(x_ref, g_ref, t_ref, loss_ref, dx_ref, dg_ref):
        @pl.when(pl.program_id(0) == 0)
        def _():
            loss_ref[...] = jnp.zeros(loss_ref.shape, f32)
            dg_ref[...] = jnp.zeros(dg_ref.shape, f32)

        xv = x_ref[...]
        r = lax.rsqrt(jnp.mean(xv * xv, axis=-1, keepdims=True) + EPS)
        xhat = xv * r
        err = xhat * g_ref[...] - t_ref[...]
        loss_ref[...] += 0.5 * jnp.sum(jnp.mean(err * err, axis=-1, keepdims=True))
        dy = err * (1.0 / D)
        dg_ref[...] += jnp.sum(dy * xhat, axis=0, keepdims=True)
        dxhat = dy * g_ref[...]
        dx_ref[...] = r * (dxhat - xhat * jnp.mean(dxhat * xhat, axis=-1, keepdims=True))

    return pl.pallas_call(
        body, grid=(T // tm,),
        in_specs=[pl.BlockSpec((tm, D), lambda i: (i, 0)), pl.BlockSpec((1, D), lambda i: (0, 0)),
                  pl.BlockSpec((tm, D), lambda i: (i, 0))],
        out_specs=[pl.BlockSpec((8, 128), lambda i: (0, 0)), pl.BlockSpec((tm, D), lambda i: (i, 0)),
                   pl.BlockSpec((1, D), lambda i: (0, 0))],
        out_shape=[jax.ShapeDtypeStruct((8, 128), f32), jax.ShapeDtypeStruct((T, D), f32),
                   jax.ShapeDtypeStruct((1, D), f32)],
        compiler_params=_cparams(("arbitrary",)), name=name,
    )(x, g, tgt)


def _slot(px, py, pc):
    return 4 * px + 2 * py + pc


def _with_own_block(s, me):
    zone = lax.empty((NDEV,) + s.shape, s.dtype)
    return lax.dynamic_update_slice(zone, s[None], (me,) + (0,) * s.ndim)


def all_gather(arrs, me, *, name):
    n = len(arrs)

    def body(*refs):
        ins, outs = refs[:n], refs[2 * n:3 * n]
        send_sems, recv_sems = refs[3 * n:]
        x, y, c = lax.axis_index("x"), lax.axis_index("y"), lax.axis_index("c")
        me, sibling = (x, y, c), (x, y, 1 - c)
        chips = [(1 - x, y), (x, 1 - y), (1 - x, 1 - y)]

        def copy(a, k, block, to, src=None):
            dst = outs[a].at[_slot(*block)]
            return pltpu.make_async_remote_copy(
                src_ref=dst if src is None else src, dst_ref=dst,
                send_sem=send_sems.at[a, k], recv_sem=recv_sems.at[a, k], device_id=to, device_id_type=MESH)

        first = []
        for a in range(n):
            first.append(copy(a, 0, me, sibling, src=ins[a]))
            first += [copy(a, 1 + j, me, (*chip, c), src=ins[a]) for j, chip in enumerate(chips)]
        for cp in first:
            cp.start()
        passed = []
        for j, chip in enumerate(chips):
            for a in range(n):
                copy(a, 1 + j, (*chip, c), me).wait_recv()
                fwd = copy(a, 4 + j, (*chip, c), sibling)
                fwd.start()
                passed.append(fwd)
        for a in range(n):
            copy(a, 0, sibling, me).wait_recv()
        for j, chip in enumerate(chips):
            for a in range(n):
                copy(a, 4 + j, (*chip, 1 - c), me).wait_recv()
        for cp in first + passed:
            cp.wait_send()

    any_spec = pl.BlockSpec(memory_space=pl.ANY)
    return pl.pallas_call(
        body,
        in_specs=[any_spec] * (2 * n), out_specs=[any_spec] * n,
        out_shape=[jax.ShapeDtypeStruct((NDEV,) + a.shape, a.dtype) for a in arrs],
        input_output_aliases={n + i: i for i in range(n)},
        scratch_shapes=[pltpu.SemaphoreType.DMA((n, 7)), pltpu.SemaphoreType.DMA((n, 7))],
        compiler_params=pltpu.CompilerParams(has_side_effects=True), name=name,
    )(*arrs, *[_with_own_block(a, me) for a in arrs])


HBM_SPEC = pl.BlockSpec(memory_space=pltpu.HBM)
SEM_SPEC = pl.BlockSpec(memory_space=pltpu.SEMAPHORE)
ANY_SPEC = pl.BlockSpec(memory_space=pl.ANY)
TOKEN_SHAPE = jax.ShapeDtypeStruct((8, 128), f32)
DATAFLOW_EFFECT = pltpu.SideEffectType.DATAFLOW_SIDE_EFFECTING


def _in_hbm(a):
    return pltpu.with_memory_space_constraint(a, pltpu.HBM)


def _hbm_like(a):
    return pltpu.HBM(a.shape, a.dtype)


def _mesh_pos():
    return lax.axis_index("x"), lax.axis_index("y"), lax.axis_index("c")


def _gather_targets(x, y, c):
    return [(x, y, 1 - c), (1 - x, y, c), (x, 1 - y, c), (1 - x, 1 - y, c)]


def gather_start(shards, me, after, *, name):
    n = len(shards)
    extra = [] if after is None else [after]

    def body(*refs):
        srcs, lands = refs[:n], refs[n:2 * n]
        send_sems, recv_sems = refs[2 * n + len(extra)], refs[2 * n + len(extra) + 1]
        token = refs[-1]
        x, y, c = _mesh_pos()
        me = _slot(x, y, c)
        for a in range(n):
            for k, to in enumerate(_gather_targets(x, y, c)):
                pltpu.make_async_remote_copy(
                    src_ref=srcs[a], dst_ref=lands[a].at[me], send_sem=send_sems.at[4 * a + k], recv_sem=recv_sems.at[4 * a + k],
                    device_id=to, device_id_type=MESH).start()
        token[...] = jnp.zeros(token.shape, f32)

    lands = [_with_own_block(s, me) for s in shards]
    sems = pltpu.SemaphoreType.DMA((4 * n,))
    out = pl.pallas_call(
        body, name=name,
        in_specs=[HBM_SPEC] * (2 * n) + [ANY_SPEC] * len(extra),
        out_specs=[SEM_SPEC, SEM_SPEC] + [HBM_SPEC] * (2 * n) + [pl.BlockSpec(memory_space=pltpu.VMEM)],
        out_shape=[sems, sems] + [_hbm_like(s) for s in shards] + [_hbm_like(l) for l in lands] + [TOKEN_SHAPE],
        input_output_aliases={i: 2 + i for i in range(2 * n)},
        compiler_params=pltpu.CompilerParams(has_side_effects=DATAFLOW_EFFECT),
    )(*[_in_hbm(s) for s in shards], *[_in_hbm(l) for l in lands], *extra)
    return out[0], out[1], out[2:2 + n], out[2 + n:2 + 2 * n], out[-1]


def gather_wait(send_sems, recv_sems, shards, lands, after, *, name):
    n = len(shards)
    after = list(after) if isinstance(after, (list, tuple)) else [after]

    def body(*refs):
        srcs, lands_in = refs[:n], refs[n:2 * n]
        send_sems, recv_sems = refs[2 * n], refs[2 * n + 1]
        x, y, c = _mesh_pos()
        for a in range(n):
            for k, frm in enumerate(_gather_targets(x, y, c)):
                cp = pltpu.make_async_remote_copy(
                    src_ref=srcs[a], dst_ref=lands_in[a].at[_slot(*frm)], send_sem=send_sems.at[4 * a + k],
                    recv_sem=recv_sems.at[4 * a + k], device_id=frm, device_id_type=MESH)
                cp.wait_send()
                cp.wait_recv()

    out = pl.pallas_call(
        body, name=name,
        in_specs=[HBM_SPEC] * (2 * n) + [SEM_SPEC, SEM_SPEC] + [ANY_SPEC] * len(after),
        out_specs=[HBM_SPEC] * (2 * n),
        out_shape=[_hbm_like(s) for s in shards] + [_hbm_like(l) for l in lands],
        input_output_aliases={i: i for i in range(2 * n)},
        compiler_params=pltpu.CompilerParams(has_side_effects=DATAFLOW_EFFECT),
    )(*shards, *lands, send_sems, recv_sems, *after)
    return out[:n], out[n:]


def gather_finish(lands, *, name):
    n = len(lands)

    def body(*refs):
        lands_in, lands_out = refs[:n], refs[n:2 * n]
        send_sems, recv_sems = refs[2 * n:]
        x, y, c = _mesh_pos()
        sibling = (x, y, 1 - c)
        chips = [(1 - x, y), (x, 1 - y), (1 - x, 1 - y)]
        sent = []
        for a in range(n):
            for j, chip in enumerate(chips):
                s = _slot(*chip, c)
                cp = pltpu.make_async_remote_copy(
                    src_ref=lands_in[a].at[s], dst_ref=lands_out[a].at[s], send_sem=send_sems.at[a, j],
                    recv_sem=recv_sems.at[a, j], device_id=sibling, device_id_type=MESH)
                cp.start()
                sent.append(cp)
        for a in range(n):
            for j, chip in enumerate(chips):
                s = _slot(*chip, 1 - c)
                pltpu.make_async_remote_copy(
                    src_ref=lands_in[a].at[s], dst_ref=lands_out[a].at[s], send_sem=send_sems.at[a, j],
                    recv_sem=recv_sems.at[a, j], device_id=sibling, device_id_type=MESH).wait_recv()
        for cp in sent:
            cp.wait_send()

    return pl.pallas_call(
        body, name=name,
        in_specs=[ANY_SPEC] * n, out_specs=[ANY_SPEC] * n,
        out_shape=[jax.ShapeDtypeStruct(l.shape, l.dtype) for l in lands],
        input_output_aliases={i: i for i in range(n)},
        scratch_shapes=[pltpu.SemaphoreType.DMA((n, 3)), pltpu.SemaphoreType.DMA((n, 3))],
        compiler_params=pltpu.CompilerParams(has_side_effects=True),
    )(*lands)


def _peer(x, y, c, r):
    return (x ^ ((r >> 2) & 1), y ^ ((r >> 1) & 1), c ^ (r & 1))


def exchange_start(items, lands, after, *, name):
    n, m = len(items), len(lands)
    extra = [] if after is None else [after]

    def body(*refs):
        gs, zones = refs[:n], refs[n:n + m]
        send_sems, recv_sems = refs[n + m + len(extra)], refs[n + m + len(extra) + 1]
        token = refs[-1]
        x, y, c = _mesh_pos()
        me = _slot(x, y, c)
        for r in (1, 2, 4, 3, 5, 6, 7):
            to = _peer(x, y, c, r)
            for a, (_, zi, l) in enumerate(items):
                pltpu.make_async_remote_copy(
                    src_ref=gs[a].at[_slot(*to)], dst_ref=zones[zi].at[me, l], send_sem=send_sems.at[7 * a + r - 1],
                    recv_sem=recv_sems.at[7 * a + r - 1], device_id=to, device_id_type=MESH).start()
        token[...] = jnp.zeros(token.shape, f32)

    gs = [g for g, _, _ in items]
    sems = pltpu.SemaphoreType.DMA((7 * n,))
    out = pl.pallas_call(
        body, name=name,
        in_specs=[HBM_SPEC] * (n + m) + [ANY_SPEC] * len(extra),
        out_specs=[SEM_SPEC, SEM_SPEC] + [HBM_SPEC] * (n + m) + [pl.BlockSpec(memory_space=pltpu.VMEM)],
        out_shape=[sems, sems] + [_hbm_like(g) for g in gs] + [_hbm_like(z) for z in lands] + [TOKEN_SHAPE],
        input_output_aliases={i: 2 + i for i in range(n + m)},
        compiler_params=pltpu.CompilerParams(has_side_effects=DATAFLOW_EFFECT),
    )(*[_in_hbm(g) for g in gs], *[_in_hbm(z) for z in lands], *extra)
    return out[0], out[1], out[2:2 + n], out[2 + n:2 + n + m], out[-1]


def exchange_wait(parts, lands, after, *, name):
    m = len(lands)
    flat_gs = [g for _, _, _, gs in parts for g in gs]
    ng = len(flat_gs)
    after = list(after) if isinstance(after, (list, tuple)) else [after]

    def body(*refs):
        gs, zones = refs[:ng], refs[ng:ng + m]
        sem_refs = refs[ng + m:ng + m + 2 * len(parts)]
        x, y, c = _mesh_pos()
        base = 0
        for pi, (items, _, _, _) in enumerate(parts):
            send_sems, recv_sems = sem_refs[2 * pi], sem_refs[2 * pi + 1]
            for r in range(1, NDEV):
                frm = _peer(x, y, c, r)
                for a, (_, zi, l) in enumerate(items):
                    cp = pltpu.make_async_remote_copy(
                        src_ref=gs[base + a].at[_slot(*frm)], dst_ref=zones[zi].at[_slot(*frm), l],
                        send_sem=send_sems.at[7 * a + r - 1], recv_sem=recv_sems.at[7 * a + r - 1],
                        device_id=frm, device_id_type=MESH)
                    cp.wait_send()
                    cp.wait_recv()
            base += len(items)

    sem_args = [s for _, ss, rs, _ in parts for s in (ss, rs)]
    out = pl.pallas_call(
        body, name=name,
        in_specs=[HBM_SPEC] * (ng + m) + [SEM_SPEC] * len(sem_args) + [ANY_SPEC] * len(after),
        out_specs=[HBM_SPEC] * (ng + m),
        out_shape=[_hbm_like(g) for g in flat_gs] + [_hbm_like(z) for z in lands],
        input_output_aliases={i: i for i in range(ng + m)},
        compiler_params=pltpu.CompilerParams(has_side_effects=DATAFLOW_EFFECT),
    )(*flat_gs, *lands, *sem_args, *after)
    return out[ng:]


def sum8(g, *, name):
    _, R, C = g.shape
    tr = R

    def body(g_ref, o_ref):
        s = g_ref[0]
        for k in range(1, NDEV):
            s = s + g_ref[k]
        o_ref[...] = s

    return pl.pallas_call(
        body, grid=(R // tr,),
        in_specs=[pl.BlockSpec((NDEV, tr, C), lambda i: (0, i, 0))],
        out_specs=pl.BlockSpec((tr, C), lambda i: (i, 0)),
        out_shape=jax.ShapeDtypeStruct((R, C), f32),
        compiler_params=_cparams(("parallel",)), name=name,
    )(g)


def _adam_math(w, g, m, v):
    m = ADAM_B1 * m + (1.0 - ADAM_B1) * g
    v = ADAM_B2 * v + (1.0 - ADAM_B2) * (g * g)
    m_hat = m / (1.0 - ADAM_B1 ** ADAM_STEP)
    v_hat = v / (1.0 - ADAM_B2 ** ADAM_STEP)
    delta = -ADAM_LR * (m_hat / (jnp.sqrt(v_hat) + ADAM_EPS) + ADAM_WD * w)
    return delta, m, v


def adamw_slices(w, m, v, recv, *, name):
    return adamw_families([w], [m], [v], [recv], tc=w.shape[2], name=name)


def adamw_families(ws, ms, vs, recvs, *, tc, l0=0, nl=None, prev=None, name):
    nf = len(ws)
    L, R, C = ws[0].shape
    nl = L - l0 if nl is None else nl
    prevs = [] if prev is None else list(prev)

    def body(*refs):
        outs = refs[4 * nf + len(prevs):]
        for q in range(nf):
            w_ref, m_ref, v_ref, r_ref = refs[q], refs[nf + q], refs[2 * nf + q], refs[3 * nf + q]
            g = r_ref[0].astype(f32)
            for k in range(1, NDEV):
                g = g + r_ref[k].astype(f32)
            delta, nm, nv = _adam_math(w_ref[...], g, m_ref[...], v_ref[...])
            outs[4 * q][...] = g
            outs[4 * q + 1][...] = delta
            outs[4 * q + 2][...] = nm
            outs[4 * q + 3][...] = nv

    wspec = pl.BlockSpec((None, R, tc), lambda l, j: (l0 + l, 0, j))
    rspec = pl.BlockSpec((NDEV, None, R, tc), lambda l, j: (0, l0 + l, 0, j))
    return pl.pallas_call(
        body, grid=(nl, C // tc),
        in_specs=[wspec] * (3 * nf) + [rspec] * nf + [pl.BlockSpec(memory_space=pl.ANY)] * len(prevs),
        out_specs=[wspec] * (4 * nf), out_shape=[jax.ShapeDtypeStruct((L, R, C), f32)] * (4 * nf),
        input_output_aliases={4 * nf + q: q for q in range(len(prevs))},
        compiler_params=_cparams(("parallel", "parallel")), name=name,
    )(*ws, *ms, *vs, *recvs, *prevs)


def adamw_slices_t(w, m, v, recv, *, name):
    L, R, C = w.shape

    def body(w_ref, m_ref, v_ref, r_ref, g_ref, d_ref, nm_ref, nv_ref):
        gt = r_ref[0].astype(f32)
        for k in range(1, NDEV):
            gt = gt + r_ref[k].astype(f32)
        g = gt.T
        delta, nm, nv = _adam_math(w_ref[...], g, m_ref[...], v_ref[...])
        g_ref[...] = g
        d_ref[...] = delta
        nm_ref[...] = nm
        nv_ref[...] = nv

    wspec = pl.BlockSpec((None, R, C), lambda l: (l, 0, 0))
    return pl.pallas_call(
        body, grid=(L,),
        in_specs=[wspec, wspec, wspec, pl.BlockSpec((NDEV, None, C, R), lambda l: (0, l, 0, 0))],
        out_specs=[wspec] * 4, out_shape=[jax.ShapeDtypeStruct((L, R, C), f32)] * 4,
        compiler_params=_cparams(("parallel",)), name=name,
    )(w, m, v, recv)


def adamw_plain(w, m, v, g, *, name):
    R, C = w.shape

    def body(w_ref, m_ref, v_ref, g_ref, d_ref, nm_ref, nv_ref):
        delta, nm, nv = _adam_math(w_ref[...], g_ref[...], m_ref[...], v_ref[...])
        d_ref[...] = delta
        nm_ref[...] = nm
        nv_ref[...] = nv

    return pl.pallas_call(
        body, out_shape=[jax.ShapeDtypeStruct((R, C), f32)] * 3, name=name,
    )(w, m, v, g)


def _chunk_causal_mask():
    pos = jnp.arange(GBLK)
    return (pos[None, :] // CHUNK) <= (pos[:, None] // CHUNK)


def kernel(x, norm_mix_g, norm_ffn_g, final_norm_g, a_w_in, a_v_norm_g, a_w_s, a_b_s, a_w_out, b_w_in, b_conv_w, b_w_out, c_w_in, c_w_grp, c_scale, c_w_out, f_w_gate, f_w_up, f_w_down, loss_target, m_norm_mix_g, m_norm_ffn_g, m_final_norm_g, m_a_w_in, m_a_v_norm_g, m_a_w_s, m_a_b_s, m_a_w_out, m_b_w_in, m_b_conv_w, m_b_w_out, m_c_w_in, m_c_w_grp, m_c_scale, m_c_w_out, m_f_w_gate, m_f_w_up, m_f_w_down, v_norm_mix_g, v_norm_ffn_g, v_final_norm_g, v_a_w_in, v_a_v_norm_g, v_a_w_s, v_a_b_s, v_a_w_out, v_b_w_in, v_b_conv_w, v_b_w_out, v_c_w_in, v_c_w_grp, v_c_scale, v_c_w_out, v_f_w_gate, v_f_w_up, v_f_w_down):
    B, S, _ = x.shape
    T = B * S
    tm = min(512, S)
    tms = min(512, T)
    tma = min(256, S)
    tmf = min(256, T)
    me = _slot(lax.axis_index("x"), lax.axis_index("y"), lax.axis_index("c"))

    def tb(w):
        return jnp.transpose(w, (0, 2, 1)).astype(bf16)

    small = jnp.concatenate([a_v_norm_g, b_conv_w[0], c_scale, jnp.zeros((2, GBLK), f32)], axis=0)
    started = [gather_start([tb(a_w_in[0:1])[0], a_w_out[0].astype(bf16), small], me, None, name="gather_start_0")]
    tok = started[0][4]

    def ffn_shards(l, z):
        return [tb(f_w_gate[l:l + 1] + z)[0], tb(f_w_up[l:l + 1] + z)[0], (f_w_down[l] + z).astype(bf16)]

    started.append(gather_start(ffn_shards(0, tok[0, 0]), me, tok, name="gather_start_1"))
    tok = started[-1][4]

    def start_later_stages(z, after):
        stage_shards = [
            [tb(b_w_in + z)[0], (b_w_out[0] + z).astype(bf16)] + ffn_shards(1, z),
            [(c_w_in[0] + z).astype(bf16), (c_w_grp[0] + z).astype(bf16), (c_w_out[0] + z).astype(bf16)] + ffn_shards(2, z),
            [tb(a_w_in[1:2] + z)[0], (a_w_out[1] + z).astype(bf16)] + ffn_shards(3, z),
        ]
        for q, shards in enumerate(stage_shards):
            started.append(gather_start(shards, me, after, name=f"gather_start_{q + 2}"))
            after = started[-1][4]
        return after

    def stage_weights(s, after):
        send_sems, recv_sems, shards, lands, _ = started[s]
        _, lands = gather_wait(send_sems, recv_sems, shards, lands, after, name=f"gather_wait_{s}")
        lands = gather_finish(lands, name=f"gather_finish_{s}")
        return [t.reshape(-1, t.shape[-1]) if t.ndim == 3 else t for t in lands]

    wm_masked = jnp.where(_chunk_causal_mask()[None, None], a_w_s, 0.0).astype(bf16)
    bsb = jnp.broadcast_to(a_b_s[:, :, :, None], a_b_s.shape + (GBLK,))

    xc = x.reshape(T, D)
    saved, weights = [], []
    for i in range(DEPTH):
        kind, j = i % 3, i // 3
        gmix = norm_mix_g[i][None]
        if i == 0:
            w_in, w_out, small_g = stage_weights(0, tok)
            small_full = jnp.transpose(small_g.reshape(NDEV, 8, GBLK), (1, 0, 2)).reshape(8, D)
            gv_full, cw_full, cs_full = small_full[0:2], small_full[2:5], small_full[5:6]
            w = [w_in, w_out]
        else:
            w = stage_weights(i + 1, xc)
        if kind == 0:
            w_in = w[0]
            p, h, y, tok_a = a_fwd(xc, gmix, w_in, gv_full[j][None], wm_masked[j], bsb[j], tm=tma, name=f"a_fwd_{i}")
            if i == 0:
                tok_late = start_later_stages(tok_a[0, 0], tok_a)
        elif kind == 1:
            w_in, w_out = w[:2]
            p, h = norm_mm(xc, gmix, w_in, tm=tms, transposed=True, name=f"b_in_{i}")
            y = b_mid_fwd(p, cw_full, tm=tm, S=S, name=f"b_mid_{i}")
        else:
            w_in, w_grp, w_out = w[:3]
            p, h = norm_mm(xc, gmix, w_in, tm=tms, transposed=False, name=f"c_in_{i}")
            y = c_mid_fwd(p, w_grp, cs_full, tm=tm, S=S, name=f"c_mid_{i}")
        if i == 0:
            w = w + stage_weights(1, [y, tok_late])
        weights.append(w)
        w_out, w_fg, w_fu, w_fd = w[-4:]
        x1, x2, pg, pu, a, h2 = ffn_fwd(y, w_out, xc, norm_ffn_g[i][None], w_fg, w_fu, w_fd, tm=tmf, name=f"f_fwd_{i}")
        saved.append((xc, p, h, y, x1, pg, pu, h2, a))
        xc = x2

    loss_blk, dx, d_final = final_loss(xc, final_norm_g[None], loss_target.reshape(T, D), tm=tms, name="final_loss")
    loss = lax.psum(loss_blk[0, 0], AXES)

    zone_shapes = [(2, 256, D), (2, GBLK, D), (1, 384, D), (1, GBLK, D), (1, GBLK, D), (1, GBLK, C_GDIM), (1, GBLK, D),
                   (DEPTH, F_LOC, D), (DEPTH, F_LOC, D), (DEPTH, F_LOC, D)]
    Z_AIN, Z_AOUT, Z_BIN, Z_BOUT, Z_CIN, Z_CGRP, Z_COUT, Z_FG, Z_FU, Z_FD = range(10)
    lands = [lax.empty((NDEV,) + s, bf16) for s in zone_shapes]
    parts = []

    def exchange(items, after, name):
        nonlocal lands
        for g, zi, l in items:
            own = lax.dynamic_index_in_dim(g, me, axis=0, keepdims=True)[None]
            lands[zi] = lax.dynamic_update_slice(lands[zi], own, (me, l) + (0,) * (g.ndim - 1))
        idx = sorted({zi for _, zi, _ in items})
        local = [(g, idx.index(zi), l) for g, zi, l in items]
        send_sems, recv_sems, gs, zs, token = exchange_start(local, [lands[zi] for zi in idx], after, name=name)
        for q, zi in enumerate(idx):
            lands[zi] = zs[q]
        parts.append(([(None, zi, l) for _, zi, l in items], send_sems, recv_sems, gs))
        return token

    d_mix, d_ffn = [None] * DEPTH, [None] * DEPTH
    d_gv, d_wm, d_bs = [None] * 2, [None] * 2, [None] * 2
    tok = None
    for i in reversed(range(DEPTH)):
        kind, j = i % 3, i // 3
        xin, p, h, y, x1, pg, pu, h2, a = saved[i]
        w = weights[i]
        w_in, w_out = w[0], w[-4]
        w_fg, w_fu, w_fd = w[-3:]
        dpg, dpu, dxb, dx1, d_ffn[i], dy, dx1b = ffn_bwd(dx, w_fd, pg, pu, w_fg, w_fu, x1, norm_ffn_g[i][None], tok, w_out,
                                                         tm=tmf, name=f"f_bwd_{i}")
        gw_g, gw_u, gw_d = ffn_dw(dpg, dpu, a, h2, dxb, name=f"f_dw_{i}")
        tok = exchange([(gw_g, Z_FG, i), (gw_u, Z_FU, i), (gw_d, Z_FD, i)], tok, f"exchange_start_ffn_{i}")
        gmix = norm_mix_g[i][None]
        if kind == 0:
            dp, d_gv[j], d_wm[j], d_bs[j], dx, d_mix[i] = a_mid_bwd(
                p, dy, gv_full[j][None], wm_masked[j], bsb[j], w_in, xin, gmix, tok, dx1, tm=tma, name=f"a_bwd_{i}")
            gw_in, gw_out = mm_tn2(dp, h, y, dx1b, name=f"a_dw_{i}")
            items = [(gw_in, Z_AIN, j), (gw_out, Z_AOUT, j)]
        elif kind == 1:
            dp, d_cw, dx, d_mix[i] = b_mid_bwd(p, dy, cw_full, w_in, xin, gmix, tok, dx1, tm=tm, S=S, name=f"b_bwd_{i}")
            gw_in, gw_out = mm_tn2(dp, h, y, dx1b, name=f"b_dw_{i}")
            items = [(gw_in, Z_BIN, 0), (gw_out, Z_BOUT, 0)]
        else:
            dp, d_cs, d_wgrp, dx, d_mix[i] = c_mid_bwd(p, dy, w[1], cs_full, w_in, xin, gmix, tok, dx1, tm=tm, S=S,
                                                       name=f"c_bwd_{i}")
            gw_cgrp = jnp.transpose(d_wgrp.reshape(C_GROUPS, NDEV, C_GDIM // NDEV, C_GDIM), (1, 0, 2, 3)).astype(bf16)
            gw_in, gw_out = mm_tn2(h, dp, y, dx1b, name=f"c_dw_{i}")
            items = [(gw_in, Z_CIN, 0), (gw_cgrp.reshape(NDEV, GBLK, C_GDIM), Z_CGRP, 0), (gw_out, Z_COUT, 0)]
        tok = exchange(items, tok, f"exchange_start_mix_{i}")
    grad_x = dx.reshape(B, S, D)

    small_parts = [
        jnp.concatenate(d_mix, axis=0).reshape(-1, GBLK),
        jnp.concatenate(d_ffn, axis=0).reshape(-1, GBLK),
        d_final.reshape(-1, GBLK),
        jnp.concatenate(d_gv, axis=0).reshape(-1, GBLK),
        jnp.where(_chunk_causal_mask()[None, None], jnp.stack(d_wm), 0.0).reshape(-1, GBLK),
        jnp.stack(d_bs).reshape(-1, GBLK),
        d_cw.reshape(-1, GBLK),
        d_cs.reshape(-1, GBLK),
    ]
    small_rows = [q.shape[0] for q in small_parts]
    sg_send, sg_recv, sg_shards, sg_lands, _ = gather_start(
        [jnp.concatenate(small_parts, axis=0)], me, tok, name="gather_small_start")

    def plain(w, m, v, g, name):
        shp = w.shape
        w2, m2, v2, g2 = (t.reshape(-1, shp[-1]) for t in (w, m, v, g))
        return tuple(t.reshape(shp) for t in adamw_plain(w2, m2, v2, g2, name=name))

    def sliced(w, m, v, recv, name):
        shp = w.shape
        w3, m3, v3 = (t.reshape((shp[0], -1, shp[-1])) for t in (w, m, v))
        return tuple(t.reshape(shp) for t in adamw_slices(w3, m3, v3, recv, name=name))

    lands = list(exchange_wait(parts[:6], lands, dx, name="exchange_wait"))
    _, _, r_bin, r_bout, r_cin, r_cgrp, r_cout, r_fg, r_fu, r_fd = lands
    res = {}
    res["b_w_in"] = adamw_slices_t(b_w_in, m_b_w_in, v_b_w_in, r_bin, name="adam_b_w_in")
    res["b_w_out"] = sliced(b_w_out, m_b_w_out, v_b_w_out, r_bout, "adam_b_w_out")
    res["c_w_in"] = sliced(c_w_in, m_c_w_in, v_c_w_in, r_cin, "adam_c_w_in")
    res["c_w_grp"] = sliced(c_w_grp, m_c_w_grp, v_c_w_grp, r_cgrp, "adam_c_w_grp")
    res["c_w_out"] = sliced(c_w_out, m_c_w_out, v_c_w_out, r_cout, "adam_c_w_out")
    def tr(t):
        return jnp.transpose(t, (0, 2, 1))

    f_ws = [tr(f_w_gate), tr(f_w_up), f_w_down]
    f_ms = [tr(m_f_w_gate), tr(m_f_w_up), m_f_w_down]
    f_vs = [tr(v_f_w_gate), tr(v_f_w_up), v_f_w_down]
    ff = adamw_families(f_ws, f_ms, f_vs, [r_fg, r_fu, r_fd], tc=256, l0=1, name="adam_ffn_123")
    big = [res[n][1] for n in ("b_w_in", "b_w_out", "c_w_in", "c_w_grp", "c_w_out")] + [ff[1], ff[5], ff[9]]

    ffn0_items, ffn0_send, ffn0_recv, ffn0_gs = parts[6]
    ffn0_zones = [Z_FG, Z_FU, Z_FD]
    ffn0_part = ([(None, ffn0_zones.index(zi), l) for _, zi, l in ffn0_items], ffn0_send, ffn0_recv, ffn0_gs)
    r_fg, r_fu, r_fd = exchange_wait([ffn0_part], [lands[zi] for zi in ffn0_zones], big, name="exchange_wait_ffn0")
    ff = adamw_families(f_ws, f_ms, f_vs, [r_fg, r_fu, r_fd], tc=256, l0=0, nl=1, prev=ff, name="adam_ffn_0")
    res["f_w_gate"], res["f_w_up"], res["f_w_down"] = tuple(tr(t) for t in ff[0:4]), tuple(tr(t) for t in ff[4:8]), tuple(ff[8:12])

    last_items, last_send, last_recv, last_gs = parts[7]
    last_zones = [Z_AIN, Z_AOUT]
    last_part = ([(None, last_zones.index(zi), l) for _, zi, l in last_items], last_send, last_recv, last_gs)
    r_ain, r_aout = exchange_wait([last_part], [lands[zi] for zi in last_zones], [ff[1], ff[5], ff[9]], name="exchange_wait_last")
    res["a_w_in"] = adamw_slices_t(a_w_in, m_a_w_in, v_a_w_in, r_ain, name="adam_a_w_in")
    res["a_w_out"] = sliced(a_w_out, m_a_w_out, v_a_w_out, r_aout, "adam_a_w_out")

    _, sg_lands = gather_wait(sg_send, sg_recv, sg_shards, sg_lands, [res["a_w_in"][1], res["a_w_out"][1]], name="gather_small_wait")
    (gs_all,) = gather_finish(sg_lands, name="gather_small_finish")
    gs = sum8(gs_all, name="sum_small_grads")
    offs = [0]
    for r in small_rows:
        offs.append(offs[-1] + r)
    sp = [gs[offs[q]:offs[q + 1]] for q in range(len(small_parts))]
    grad_norm_mix_g = sp[0].reshape(DEPTH, D)
    grad_norm_ffn_g = sp[1].reshape(DEPTH, D)
    grad_final_norm_g = sp[2].reshape(D)
    grad_a_w_s = sp[4].reshape(a_w_s.shape)
    grad_a_b_s = sp[5].reshape(a_b_s.shape)

    def my_cols(full):
        return lax.dynamic_slice_in_dim(full, me * GBLK, GBLK, axis=1)

    grad_a_v_norm_g = my_cols(sp[3].reshape(2, D))
    grad_b_conv_w = my_cols(sp[6].reshape(3, D))[None]
    grad_c_scale = my_cols(sp[7].reshape(1, D))
    res["norm_mix_g"] = (grad_norm_mix_g,) + plain(norm_mix_g, m_norm_mix_g, v_norm_mix_g, grad_norm_mix_g, "adam_norm_mix")
    res["norm_ffn_g"] = (grad_norm_ffn_g,) + plain(norm_ffn_g, m_norm_ffn_g, v_norm_ffn_g, grad_norm_ffn_g, "adam_norm_ffn")
    res["final_norm_g"] = (grad_final_norm_g,) + tuple(
        t.reshape(D) for t in plain(final_norm_g[None], m_final_norm_g[None], v_final_norm_g[None], grad_final_norm_g[None], "adam_final"))
    res["a_v_norm_g"] = (grad_a_v_norm_g,) + plain(a_v_norm_g, m_a_v_norm_g, v_a_v_norm_g, grad_a_v_norm_g, "adam_a_v_norm")
    res["a_w_s"] = (grad_a_w_s,) + plain(a_w_s, m_a_w_s, v_a_w_s, grad_a_w_s, "adam_a_w_s")
    res["a_b_s"] = (grad_a_b_s,) + plain(a_b_s, m_a_b_s, v_a_b_s, grad_a_b_s, "adam_a_b_s")
    res["b_conv_w"] = (grad_b_conv_w,) + plain(b_conv_w, m_b_conv_w, v_b_conv_w, grad_b_conv_w, "adam_b_conv")
    res["c_scale"] = (grad_c_scale,) + plain(c_scale, m_c_scale, v_c_scale, grad_c_scale, "adam_c_scale")

    order = ["norm_mix_g", "norm_ffn_g", "final_norm_g", "a_w_in", "a_v_norm_g", "a_w_s", "a_b_s", "a_w_out", "b_w_in",
             "b_conv_w", "b_w_out", "c_w_in", "c_w_grp", "c_scale", "c_w_out", "f_w_gate", "f_w_up", "f_w_down"]
    return (loss, grad_x, *[res[n][0] for n in order], *[res[n][1] for n in order],
            *[res[n][2] for n in order], *[res[n][3] for n in order])
```

```python
import jax
import jax.numpy as jnp
from jax import lax
from jax.experimental import pallas as pl
from jax.experimental.pallas import tpu as pltpu

f32 = jnp.float32
bf16 = jnp.bfloat16

NDEV = 8
D = 1024
EPS = 1e-6
CHUNK = 64
GBLK = 128
A_GROUPS = 8
C_GROUPS = 4
C_GDIM = 256
POOL_WINDOWS = (2, 4, 8, 16)
HALO = 16
COLS = 256
F_LOC = 352
MXU_COLS = 256
DEPTH = 4
AXES = ("x", "y", "c")
MESH = pl.DeviceIdType.MESH

ADAM_LR = 0.001
ADAM_B1 = 0.9
ADAM_B2 = 0.999
ADAM_EPS = 1e-08
ADAM_WD = 0.01
ADAM_STEP = 10

VMEM_LIMIT = 56 * 1024 * 1024


def _cparams(sem):
    return pltpu.CompilerParams(dimension_semantics=sem, vmem_limit_bytes=VMEM_LIMIT)


def _gelu(z):
    return 0.5 * z * (1.0 + lax.erf(z * 0.7071067811865476))


def _gelu_cdf(z):
    return 0.5 * (1.0 + lax.erf(z * 0.7071067811865476))


def _gelu_grad(z, cdf):
    return cdf + z * jnp.exp(-0.5 * z * z) * 0.3989422804014327


def _dot(a, b):
    return jnp.dot(a, b, preferred_element_type=f32)


def _dot_nt(a, b):
    return lax.dot_general(a, b, (((1,), (1,)), ((), ())), preferred_element_type=f32)


def _dot_tn(a, b):
    return lax.dot_general(a, b, (((0,), (0,)), ((), ())), preferred_element_type=f32)


def _col_chunks(n, width=1024):
    return [(c, min(c + width, n)) for c in range(0, n, width)]


def ffn_fwd(y, w_out, xin, g, wtg, wtu, wd, *, tm, name):
    T, K = y.shape
    H = wtg.shape[0]

    def body(y_ref, wo_ref, xin_ref, g_ref, wg_ref, wu_ref, wd_ref, x_ref, x2_ref, pg_ref, pu_ref, a_ref, h_ref):
        xv = xin_ref[...] + _dot(y_ref[...], wo_ref[...])
        x_ref[...] = xv
        r = lax.rsqrt(jnp.mean(xv * xv, axis=-1, keepdims=True) + EPS)
        h = (xv * r * g_ref[...]).astype(bf16)
        h_ref[...] = h
        out = None
        for c0, c1 in _col_chunks(H):
            gate = _dot_nt(h, wg_ref[c0:c1, :])
            up = _dot_nt(h, wu_ref[c0:c1, :])
            pg_ref[:, c0:c1] = gate.astype(bf16)
            pu_ref[:, c0:c1] = up.astype(bf16)
            ac = (gate * jax.nn.sigmoid(gate) * up).astype(bf16)
            a_ref[:, c0:c1] = ac
            part = _dot(ac, wd_ref[c0:c1, :])
            out = part if out is None else out + part
        x2_ref[...] = xv + out

    wspec = pl.BlockSpec((H, D), lambda i: (0, 0), pipeline_mode=pl.Buffered(1))
    hspec = pl.BlockSpec((tm, H), lambda i: (i, 0))
    dspec = pl.BlockSpec((tm, D), lambda i: (i, 0))
    return pl.pallas_call(
        body, grid=(T // tm,),
        in_specs=[pl.BlockSpec((tm, K), lambda i: (i, 0)),
                  pl.BlockSpec((K, D), lambda i: (0, 0), pipeline_mode=pl.Buffered(1)),
                  dspec, pl.BlockSpec((1, D), lambda i: (0, 0)), wspec, wspec, wspec],
        out_specs=[dspec, dspec, hspec, hspec, hspec, dspec],
        out_shape=[jax.ShapeDtypeStruct((T, D), f32)] * 2 + [jax.ShapeDtypeStruct((T, H), bf16)] * 3
        + [jax.ShapeDtypeStruct((T, D), bf16)],
        compiler_params=_cparams(("parallel",)), name=name,
    )(y, w_out, xin, g, wtg, wtu, wd)


def ffn_bwd(dx, wd, pg, pu, wtg, wtu, x, g, tok, w_next, *, tm, name):
    T = dx.shape[0]
    H = wd.shape[0]
    kn = w_next.shape[0]
    toks = [] if tok is None else [tok]

    def body(dx_ref, wd_ref, pg_ref, pu_ref, wg_ref, wu_ref, x_ref, g_ref, *rest):
        wn_ref = rest[len(toks)]
        dg_ref, du_ref, dxb_ref, dx1_ref, dgn_ref, dy_ref, dx1b_ref = rest[len(toks) + 1:]

        @pl.when(pl.program_id(0) == 0)
        def _():
            dgn_ref[...] = jnp.zeros(dgn_ref.shape, f32)

        dxv = dx_ref[...]
        dxb = dxv.astype(bf16)
        dxb_ref[...] = dxb
        dh = None
        for c0, c1 in _col_chunks(H):
            da = _dot_nt(dxb, wd_ref[c0:c1, :])
            gate = pg_ref[:, c0:c1].astype(f32)
            up = pu_ref[:, c0:c1].astype(f32)
            sg = jax.nn.sigmoid(gate)
            dgc = (da * up * (sg * (1.0 + gate * (1.0 - sg)))).astype(bf16)
            duc = (da * gate * sg).astype(bf16)
            dg_ref[:, c0:c1] = dgc
            du_ref[:, c0:c1] = duc
            part = _dot(dgc, wg_ref[c0:c1, :]) + _dot(duc, wu_ref[c0:c1, :])
            dh = part if dh is None else dh + part
        gv = g_ref[...] + rest[0][0:1, 0:1] if toks else g_ref[...]
        xv = x_ref[...]
        r = lax.rsqrt(jnp.mean(xv * xv, axis=-1, keepdims=True) + EPS)
        xhat = xv * r
        dgn_ref[...] += jnp.sum(dh * xhat, axis=0, keepdims=True)
        dxhat = dh * gv
        dx1 = dxv + r * (dxhat - xhat * jnp.mean(dxhat * xhat, axis=-1, keepdims=True))
        dx1_ref[...] = dx1
        dx1b = dx1.astype(bf16)
        dx1b_ref[...] = dx1b
        dy_ref[...] = _dot_nt(dx1b, wn_ref[...]).astype(bf16)

    dspec = pl.BlockSpec((tm, D), lambda i: (i, 0))
    hspec = pl.BlockSpec((tm, H), lambda i: (i, 0))
    wspec = pl.BlockSpec((H, D), lambda i: (0, 0), pipeline_mode=pl.Buffered(1))
    gspec = pl.BlockSpec((1, D), lambda i: (0, 0))
    return pl.pallas_call(
        body, grid=(T // tm,),
        in_specs=[dspec, wspec, hspec, hspec, wspec, wspec, dspec, gspec]
        + [pl.BlockSpec((8, 128), lambda i: (0, 0))] * len(toks)
        + [pl.BlockSpec((kn, D), lambda i: (0, 0), pipeline_mode=pl.Buffered(1))],
        out_specs=[hspec, hspec, dspec, dspec, gspec, pl.BlockSpec((tm, kn), lambda i: (i, 0)), dspec],
        out_shape=[jax.ShapeDtypeStruct((T, H), bf16)] * 2 + [jax.ShapeDtypeStruct((T, D), bf16),
                   jax.ShapeDtypeStruct((T, D), f32), jax.ShapeDtypeStruct((1, D), f32),
                   jax.ShapeDtypeStruct((T, kn), bf16), jax.ShapeDtypeStruct((T, D), bf16)],
        compiler_params=_cparams(("arbitrary",)), name=name,
    )(dx, wd, pg, pu, wtg, wtu, x, g, *toks, w_next)


def mm_tn2(a1, b1, a2, b2, *, name):
    T, K1 = a1.shape
    K2 = a2.shape[1]
    n1, n2 = b1.shape[1], b2.shape[1]
    last2 = K2 // MXU_COLS - 1

    def body(a1_ref, b1_ref, a2_ref, b2_ref, o1_ref, o2_ref):
        o1_ref[...] = _dot_tn(a1_ref[...], b1_ref[...]).astype(bf16)

        @pl.when(pl.program_id(0) <= last2)
        def _():
            o2_ref[...] = _dot_tn(a2_ref[...], b2_ref[...]).astype(bf16)

    o1, o2 = pl.pallas_call(
        body, grid=(K1 // MXU_COLS,),
        in_specs=[pl.BlockSpec((T, MXU_COLS), lambda d: (0, d)),
                  pl.BlockSpec((T, n1), lambda d: (0, 0), pipeline_mode=pl.Buffered(1)),
                  pl.BlockSpec((T, MXU_COLS), lambda d: (0, jnp.minimum(d, last2))),
                  pl.BlockSpec((T, n2), lambda d: (0, 0), pipeline_mode=pl.Buffered(1))],
        out_specs=[pl.BlockSpec((MXU_COLS, n1), lambda d: (d, 0)),
                   pl.BlockSpec((MXU_COLS, n2), lambda d: (jnp.minimum(d, last2), 0))],
        out_shape=[jax.ShapeDtypeStruct((K1, n1), bf16), jax.ShapeDtypeStruct((K2, n2), bf16)],
        compiler_params=_cparams(("arbitrary",)), name=name,
    )(a1, b1, a2, b2)
    return o1.reshape(NDEV, K1 // NDEV, n1), o2.reshape(NDEV, K2 // NDEV, n2)


def ffn_dw(dpg, dpu, a, h2, dxb, *, name):
    T, H = dpg.shape

    def body(g_ref, u_ref, a_ref, h_ref, x_ref, og_ref, ou_ref, od_ref):
        h = h_ref[...]
        og_ref[...] = _dot_tn(g_ref[...], h).astype(bf16)
        ou_ref[...] = _dot_tn(u_ref[...], h).astype(bf16)
        od_ref[...] = _dot_tn(a_ref[...], x_ref[...]).astype(bf16)

    blk = pl.BlockSpec((T, MXU_COLS), lambda d: (0, d))
    full = pl.BlockSpec((T, D), lambda d: (0, 0), pipeline_mode=pl.Buffered(1))
    out = pl.BlockSpec((MXU_COLS, D), lambda d: (d, 0))
    outs = pl.pallas_call(
        body, grid=(H // MXU_COLS,),
        in_specs=[blk, blk, blk, full, full], out_specs=[out] * 3,
        out_shape=[jax.ShapeDtypeStruct((H, D), bf16)] * 3,
        compiler_params=_cparams(("parallel",)), name=name,
    )(dpg, dpu, a, h2, dxb)
    return [o.reshape(NDEV, H // NDEV, D) for o in outs]


def _normbwd_tail(dh, x_ref, g_ref, tok_ref, dres_ref, dx_ref, dgn_ref):
    gv = g_ref[...] + tok_ref[0:1, 0:1]
    xv = x_ref[...]
    r = lax.rsqrt(jnp.mean(xv * xv, axis=-1, keepdims=True) + EPS)
    xhat = xv * r
    dgn_ref[...] += jnp.sum(dh * xhat, axis=0, keepdims=True)
    dxhat = dh * gv
    dx_ref[...] = dres_ref[...] + r * (dxhat - xhat * jnp.mean(dxhat * xhat, axis=-1, keepdims=True))


def _normbwd_specs(tm, w):
    dspec = pl.BlockSpec((tm, D), lambda i: (i, 0))
    gspec = pl.BlockSpec((1, D), lambda i: (0, 0))
    ins = [pl.BlockSpec(w.shape, lambda i: (0, 0), pipeline_mode=pl.Buffered(1)), dspec, gspec,
           pl.BlockSpec((8, 128), lambda i: (0, 0)), dspec]
    return ins, [dspec, gspec]


def _prev_halo_spec(tm, ncol):
    return pl.BlockSpec((HALO, ncol), lambda i: (jnp.maximum(i * (tm // HALO) - 1, 0), 0))


def _next_halo_spec(tm, ncol, T):
    return pl.BlockSpec((HALO, ncol), lambda i: (jnp.minimum((i + 1) * (tm // HALO), T // HALO - 1), 0))


def a_fwd(x, g, wt, gv, wm, bsb, *, tm, name):
    T = x.shape[0]

    def body(x_ref, g_ref, wt_ref, gv_ref, wm_ref, bs_ref, p_ref, h_ref, y_ref, tok_ref, vn_sc):
        tok_ref[...] = jnp.zeros(tok_ref.shape, f32)
        xv = x_ref[...]
        r = lax.rsqrt(jnp.mean(xv * xv, axis=-1, keepdims=True) + EPS)
        h = (xv * r * g_ref[...]).astype(bf16)
        h_ref[...] = h
        p_ref[...] = _dot_nt(h, wt_ref[...]).astype(bf16)
        v = _gelu(p_ref[:, D:].astype(f32))
        vc = v - jnp.mean(v, axis=-1, keepdims=True)
        var = jnp.mean(vc * vc, axis=-1, keepdims=True)
        vn_sc[...] = (vc * lax.rsqrt(var + EPS) * gv_ref[...]).astype(bf16)
        for g in range(A_GROUPS):
            cs = slice(g * GBLK, (g + 1) * GBLK)
            for r in range(tm // GBLK):
                rs = slice(r * GBLK, (r + 1) * GBLK)
                sv = _dot(wm_ref[g], vn_sc[rs, cs]) + bs_ref[g]
                y_ref[rs, cs] = (_gelu(p_ref[rs, cs].astype(f32)) * sv).astype(bf16)

    dspec = pl.BlockSpec((tm, D), lambda i: (i, 0))
    gspec = pl.BlockSpec((1, D), lambda i: (0, 0))
    return pl.pallas_call(
        body, grid=(T // tm,),
        in_specs=[dspec, gspec, pl.BlockSpec(wt.shape, lambda i: (0, 0), pipeline_mode=pl.Buffered(1)), gspec,
                  pl.BlockSpec((A_GROUPS, GBLK, GBLK), lambda i: (0, 0, 0)),
                  pl.BlockSpec((A_GROUPS, GBLK, GBLK), lambda i: (0, 0, 0))],
        out_specs=[pl.BlockSpec((tm, 2 * D), lambda i: (i, 0)), dspec, dspec, pl.BlockSpec((8, 128), lambda i: (0, 0))],
        out_shape=[jax.ShapeDtypeStruct((T, 2 * D), bf16), jax.ShapeDtypeStruct((T, D), bf16),
                   jax.ShapeDtypeStruct((T, D), bf16), jax.ShapeDtypeStruct((8, 128), f32)],
        scratch_shapes=[pltpu.VMEM((tm, D), bf16)],
        compiler_params=_cparams(("arbitrary",)), name=name,
    )(x, g, wt, gv, wm, bsb)


def a_mid_bwd(p, dy, gv, wm, bsb, wt, x, g, tok, dres, *, tm, name):
    T = p.shape[0]

    def body(p_ref, dy_ref, gv_ref, wm_ref, bs_ref, wt_ref, x_ref, g_ref, tok_ref, dres_ref,
             dp_ref, dgv_ref, dwm_ref, dbs_ref, dx_ref, dgn_ref, vn_sc, dvn_sc):
        @pl.when(pl.program_id(0) == 0)
        def _():
            dgn_ref[...] = jnp.zeros(dgn_ref.shape, f32)
            dgv_ref[...] = jnp.zeros(dgv_ref.shape, f32)
            dwm_ref[...] = jnp.zeros(dwm_ref.shape, f32)
            dbs_ref[...] = jnp.zeros(dbs_ref.shape, f32)

        zv = p_ref[:, D:].astype(f32)
        cdf_v = _gelu_cdf(zv)
        v = zv * cdf_v
        vc = v - jnp.mean(v, axis=-1, keepdims=True)
        rstd = lax.rsqrt(jnp.mean(vc * vc, axis=-1, keepdims=True) + EPS)
        vhat = vc * rstd
        vn_sc[...] = (vhat * gv_ref[...]).astype(bf16)
        for g in range(A_GROUPS):
            cs = slice(g * GBLK, (g + 1) * GBLK)
            dwm = jnp.zeros((GBLK, GBLK), f32)
            dbs = jnp.zeros((GBLK, 1), f32)
            for r in range(tm // GBLK):
                rs = slice(r * GBLK, (r + 1) * GBLK)
                zu = p_ref[rs, cs].astype(f32)
                cdf_u = _gelu_cdf(zu)
                vn = vn_sc[rs, cs]
                sv = _dot(wm_ref[g], vn) + bs_ref[g]
                dyb = dy_ref[rs, cs].astype(f32)
                dsv = dyb * (zu * cdf_u)
                dsvb = dsv.astype(bf16)
                dp_ref[rs, cs] = (dyb * sv * _gelu_grad(zu, cdf_u)).astype(bf16)
                dwm += _dot_nt(dsvb, vn)
                dbs += jnp.sum(dsv, axis=1, keepdims=True)
                dvn_sc[rs, cs] = _dot_tn(wm_ref[g], dsvb)
            dwm_ref[g] += dwm
            dbs_ref[g] += dbs
        dvn = dvn_sc[...]
        dgv_ref[...] += jnp.sum(dvn * vhat, axis=0, keepdims=True)
        dvhat = dvn * gv_ref[...]
        dv = rstd * (dvhat - jnp.mean(dvhat, axis=-1, keepdims=True)
                     - vhat * jnp.mean(dvhat * vhat, axis=-1, keepdims=True))
        dp_ref[:, D:] = (dv * _gelu_grad(zv, cdf_v)).astype(bf16)
        _normbwd_tail(_dot(dp_ref[...], wt_ref[...]), x_ref, g_ref, tok_ref, dres_ref, dx_ref, dgn_ref)

    tail_in, tail_out = _normbwd_specs(tm, wt)
    return pl.pallas_call(
        body, grid=(T // tm,),
        in_specs=[pl.BlockSpec((tm, 2 * D), lambda i: (i, 0)), pl.BlockSpec((tm, D), lambda i: (i, 0)),
                  pl.BlockSpec((1, D), lambda i: (0, 0)),
                  pl.BlockSpec((A_GROUPS, GBLK, GBLK), lambda i: (0, 0, 0)),
                  pl.BlockSpec((A_GROUPS, GBLK, GBLK), lambda i: (0, 0, 0))] + tail_in,
        out_specs=[pl.BlockSpec((tm, 2 * D), lambda i: (i, 0)), pl.BlockSpec((1, D), lambda i: (0, 0)),
                   pl.BlockSpec((A_GROUPS, GBLK, GBLK), lambda i: (0, 0, 0)),
                   pl.BlockSpec((A_GROUPS, GBLK, 1), lambda i: (0, 0, 0))] + tail_out,
        out_shape=[jax.ShapeDtypeStruct((T, 2 * D), bf16), jax.ShapeDtypeStruct((1, D), f32),
                   jax.ShapeDtypeStruct((A_GROUPS, GBLK, GBLK), f32), jax.ShapeDtypeStruct((A_GROUPS, GBLK, 1), f32),
                   jax.ShapeDtypeStruct((T, D), f32), jax.ShapeDtypeStruct((1, D), f32)],
        scratch_shapes=[pltpu.VMEM((tm, D), bf16), pltpu.VMEM((tm, D), f32)],
        compiler_params=_cparams(("arbitrary",)), name=name,
    )(p, dy, gv, wm, bsb, wt, x, g, tok, dres)


def b_fwd(x, g, wt, cw, *, tm, S, name):
    T = x.shape[0]
    nst = S // tm

    def body(x_ref, g_ref, wt_ref, cw_ref, p_ref, h_ref, y_ref, ext):
        first = (pl.program_id(0) % nst) == 0
        xv = x_ref[...]
        r = lax.rsqrt(jnp.mean(xv * xv, axis=-1, keepdims=True) + EPS)
        h = (xv * r * g_ref[...]).astype(bf16)
        h_ref[...] = h
        p_ref[...] = _dot_nt(h, wt_ref[...]).astype(bf16)

        @pl.when(first)
        def _():
            ext[0:HALO, :] = jnp.zeros((HALO, D), f32)

        for c0 in range(0, D, COLS):
            cs, cc, cx = slice(c0, c0 + COLS), slice(D + c0, D + c0 + COLS), slice(2 * D + c0, 2 * D + c0 + COLS)
            q = p_ref[:, cc].astype(f32) * p_ref[:, cx].astype(f32)
            ext[HALO:, cs] = q
            y = (cw_ref[0:1, cs] * ext[pl.ds(HALO - 2, tm), cs] + cw_ref[1:2, cs] * ext[pl.ds(HALO - 1, tm), cs]
                 + cw_ref[2:3, cs] * q)
            y_ref[:, cs] = (p_ref[:, cs].astype(f32) * y).astype(bf16)
            ext[0:HALO, cs] = q[tm - HALO:, :]

    dspec = pl.BlockSpec((tm, D), lambda i: (i, 0))
    return pl.pallas_call(
        body, grid=(T // tm,),
        in_specs=[dspec, pl.BlockSpec((1, D), lambda i: (0, 0)),
                  pl.BlockSpec(wt.shape, lambda i: (0, 0), pipeline_mode=pl.Buffered(1)),
                  pl.BlockSpec((3, D), lambda i: (0, 0))],
        out_specs=[pl.BlockSpec((tm, 3 * D), lambda i: (i, 0)), dspec, dspec],
        out_shape=[jax.ShapeDtypeStruct((T, 3 * D), bf16), jax.ShapeDtypeStruct((T, D), bf16),
                   jax.ShapeDtypeStruct((T, D), bf16)],
        scratch_shapes=[pltpu.VMEM((tm + HALO, D), f32)],
        compiler_params=_cparams(("arbitrary",)), name=name,
    )(x, g, wt, cw)


def b_mid_bwd(p, dy, cw, wt, x, g, tok, dres, *, tm, S, name):
    T = p.shape[0]
    nst = S // tm

    def body(p_ref, ph_ref, pn_ref, dy_ref, dyn_ref, cw_ref, wt_ref, x_ref, g_ref, tok_ref, dres_ref,
             dp_ref, dcw_ref, dx_ref, dgn_ref, ext, ext2):
        i = pl.program_id(0)
        first = (i % nst) == 0
        last = (i % nst) == nst - 1

        @pl.when(i == 0)
        def _():
            dgn_ref[...] = jnp.zeros(dgn_ref.shape, f32)
            dcw_ref[...] = jnp.zeros(dcw_ref.shape, f32)

        for c0 in range(0, D, COLS):
            cs, cc, cx = slice(c0, c0 + COLS), slice(D + c0, D + c0 + COLS), slice(2 * D + c0, 2 * D + c0 + COLS)
            gb, gc, xt = p_ref[:, cs].astype(f32), p_ref[:, cc].astype(f32), p_ref[:, cx].astype(f32)
            q = gc * xt
            ext[0:HALO, cs] = jnp.where(first, 0.0, ph_ref[:, cc].astype(f32) * ph_ref[:, cx].astype(f32))
            ext[HALO:, cs] = q
            q2 = ext[pl.ds(HALO - 2, tm), cs]
            q1 = ext[pl.ds(HALO - 1, tm), cs]
            y = cw_ref[0:1, cs] * q2 + cw_ref[1:2, cs] * q1 + cw_ref[2:3, cs] * q
            dyo = dy_ref[:, cs].astype(f32)
            dp_ref[:, cs] = (dyo * y).astype(bf16)
            dyc = dyo * gb
            ext2[0:tm, cs] = dyc
            ext2[tm:, cs] = jnp.where(last, 0.0, dyn_ref[:, cs].astype(f32) * pn_ref[:, cs].astype(f32))
            dq = cw_ref[2:3, cs] * dyc + cw_ref[1:2, cs] * ext2[pl.ds(1, tm), cs] + cw_ref[0:1, cs] * ext2[pl.ds(2, tm), cs]
            dp_ref[:, cc] = (dq * xt).astype(bf16)
            dp_ref[:, cx] = (dq * gc).astype(bf16)
            dcw_ref[0:1, cs] += jnp.sum(dyc * q2, axis=0, keepdims=True)
            dcw_ref[1:2, cs] += jnp.sum(dyc * q1, axis=0, keepdims=True)
            dcw_ref[2:3, cs] += jnp.sum(dyc * q, axis=0, keepdims=True)
        _normbwd_tail(_dot(dp_ref[...], wt_ref[...]), x_ref, g_ref, tok_ref, dres_ref, dx_ref, dgn_ref)

    tail_in, tail_out = _normbwd_specs(tm, wt)
    return pl.pallas_call(
        body, grid=(T // tm,),
        in_specs=[pl.BlockSpec((tm, 3 * D), lambda i: (i, 0)), _prev_halo_spec(tm, 3 * D), _next_halo_spec(tm, 3 * D, T),
                  pl.BlockSpec((tm, D), lambda i: (i, 0)), _next_halo_spec(tm, D, T),
                  pl.BlockSpec((3, D), lambda i: (0, 0))] + tail_in,
        out_specs=[pl.BlockSpec((tm, 3 * D), lambda i: (i, 0)), pl.BlockSpec((3, D), lambda i: (0, 0))] + tail_out,
        out_shape=[jax.ShapeDtypeStruct((T, 3 * D), bf16), jax.ShapeDtypeStruct((3, D), f32),
                   jax.ShapeDtypeStruct((T, D), f32), jax.ShapeDtypeStruct((1, D), f32)],
        scratch_shapes=[pltpu.VMEM((tm + HALO, D), f32), pltpu.VMEM((tm + HALO, D), f32)],
        compiler_params=_cparams(("arbitrary",)), name=name,
    )(p, p, p, dy, dy, cw, wt, x, g, tok, dres)


def _pool_counts(i, nst, tm, rows, row0, w):
    t = (i % nst) * tm + row0 + lax.broadcasted_iota(jnp.int32, (rows, 1), 0)
    return jnp.minimum(t + 1, w).astype(f32)


def _pool_diff(p_ref, ext, g, i, nst, tm):
    w = POOL_WINDOWS[g]
    cs = slice(g * C_GDIM, (g + 1) * C_GDIM)
    pg = p_ref[:, cs].astype(f32)
    s = pg
    for k in range(1, w):
        s = s + ext[pl.ds(HALO - k, tm), cs]
    return s / _pool_counts(i, nst, tm, tm, 0, w) - pg


def c_fwd(x, g, w, wg, scale, *, tm, S, name):
    T = x.shape[0]
    nst = S // tm

    def body(x_ref, g_ref, w_ref, wg_ref, sc_ref, p_ref, h_ref, y_ref, ext):
        i = pl.program_id(0)
        xv = x_ref[...]
        r = lax.rsqrt(jnp.mean(xv * xv, axis=-1, keepdims=True) + EPS)
        h = (xv * r * g_ref[...]).astype(bf16)
        h_ref[...] = h
        p_ref[...] = _dot(h, w_ref[...]).astype(bf16)

        @pl.when((i % nst) == 0)
        def _():
            ext[0:HALO, :] = jnp.zeros((HALO, D), f32)

        ext[HALO:, :] = p_ref[...].astype(f32)
        for g_ in range(C_GROUPS):
            cs = slice(g_ * C_GDIM, (g_ + 1) * C_GDIM)
            dg = _pool_diff(p_ref, ext, g_, i, nst, tm).astype(bf16)
            wv = wg_ref[:, g_].reshape(C_GDIM, C_GDIM)
            y_ref[:, cs] = (_dot(dg, wv) * sc_ref[:, cs]).astype(bf16)
        ext[0:HALO, :] = ext[tm:tm + HALO, :]

    dspec = pl.BlockSpec((tm, D), lambda i: (i, 0))
    gspec = pl.BlockSpec((1, D), lambda i: (0, 0))
    return pl.pallas_call(
        body, grid=(T // tm,),
        in_specs=[dspec, gspec, pl.BlockSpec(w.shape, lambda i: (0, 0), pipeline_mode=pl.Buffered(1)),
                  pl.BlockSpec((NDEV, C_GROUPS, C_GDIM // NDEV, C_GDIM), lambda i: (0, 0, 0, 0)), gspec],
        out_specs=[dspec, dspec, dspec],
        out_shape=[jax.ShapeDtypeStruct((T, D), bf16)] * 3,
        scratch_shapes=[pltpu.VMEM((tm + HALO, D), f32)],
        compiler_params=_cparams(("arbitrary",)), name=name,
    )(x, g, w, wg, scale)


def c_mid_bwd(p, dy, wg, scale, w, x, g, tok, dres, *, tm, S, name):
    T = p.shape[0]
    nst = S // tm

    def body(p_ref, ph_ref, dy_ref, dyn_ref, wg_ref, sc_ref, w_ref, x_ref, g_ref, tok_ref, dres_ref,
             dp_ref, dsc_ref, dwg_ref, dx_ref, dgn_ref, ext, ext2):
        i = pl.program_id(0)
        first = (i % nst) == 0
        last = (i % nst) == nst - 1

        @pl.when(i == 0)
        def _():
            dgn_ref[...] = jnp.zeros(dgn_ref.shape, f32)
            dsc_ref[...] = jnp.zeros(dsc_ref.shape, f32)
            dwg_ref[...] = jnp.zeros(dwg_ref.shape, f32)

        ext[0:HALO, :] = jnp.where(first, 0.0, ph_ref[...].astype(f32))
        ext[HALO:, :] = p_ref[...].astype(f32)
        for g in range(C_GROUPS):
            w = POOL_WINDOWS[g]
            cs = slice(g * C_GDIM, (g + 1) * C_GDIM)
            dg = _pool_diff(p_ref, ext, g, i, nst, tm).astype(bf16)
            wv = wg_ref[:, g].reshape(C_GDIM, C_GDIM)
            dyo = dy_ref[:, cs].astype(f32)
            dsc_ref[:, cs] += jnp.sum(dyo * _dot(dg, wv), axis=0, keepdims=True)
            dyp = (dyo * sc_ref[:, cs]).astype(bf16)
            dypn = (dyn_ref[:, cs].astype(f32) * sc_ref[:, cs]).astype(bf16)
            dwg_ref[g] += _dot_tn(dg, dyp)
            dd = _dot_nt(dyp, wv)
            ddn = _dot_nt(dypn, wv)
            ext2[0:tm, cs] = dd / _pool_counts(i, nst, tm, tm, 0, w)
            ext2[tm:, cs] = jnp.where(last, 0.0, ddn / _pool_counts(i, nst, tm, HALO, tm, w))
            s = -dd
            for k in range(w):
                s = s + ext2[pl.ds(k, tm), cs]
            dp_ref[:, cs] = s.astype(bf16)
        _normbwd_tail(_dot_nt(dp_ref[...], w_ref[...]), x_ref, g_ref, tok_ref, dres_ref, dx_ref, dgn_ref)

    tail_in, tail_out = _normbwd_specs(tm, w)
    return pl.pallas_call(
        body, grid=(T // tm,),
        in_specs=[pl.BlockSpec((tm, D), lambda i: (i, 0)), _prev_halo_spec(tm, D),
                  pl.BlockSpec((tm, D), lambda i: (i, 0)), _next_halo_spec(tm, D, T),
                  pl.BlockSpec((NDEV, C_GROUPS, C_GDIM // NDEV, C_GDIM), lambda i: (0, 0, 0, 0)),
                  pl.BlockSpec((1, D), lambda i: (0, 0))] + tail_in,
        out_specs=[pl.BlockSpec((tm, D), lambda i: (i, 0)), pl.BlockSpec((1, D), lambda i: (0, 0)),
                   pl.BlockSpec((C_GROUPS, C_GDIM, C_GDIM), lambda i: (0, 0, 0))] + tail_out,
        out_shape=[jax.ShapeDtypeStruct((T, D), bf16), jax.ShapeDtypeStruct((1, D), f32),
                   jax.ShapeDtypeStruct((C_GROUPS, C_GDIM, C_GDIM), f32),
                   jax.ShapeDtypeStruct((T, D), f32), jax.ShapeDtypeStruct((1, D), f32)],
        scratch_shapes=[pltpu.VMEM((tm + HALO, D), f32), pltpu.VMEM((tm + HALO, D), f32)],
        compiler_params=_cparams(("arbitrary",)), name=name,
    )(p, p, dy, dy, wg, scale, w, x, g, tok, dres)


def final_loss(x, g, tgt, *, tm, name):
    T = x.shape[0]

    def body(x_ref, g_ref, t_ref, loss_ref, dx_ref, dg_ref):
        @pl.when(pl.program_id(0) == 0)
        def _():
            loss_ref[...] = jnp.zeros(loss_ref.shape, f32)
            dg_ref[...] = jnp.zeros(dg_ref.shape, f32)

        xv = x_ref[...]
        r = lax.rsqrt(jnp.mean(xv * xv, axis=-1, keepdims=True) + EPS)
        xhat = xv * r
        err = xhat * g_ref[...] - t_ref[...]
        loss_ref[...] += 0.5 * jnp.sum(jnp.mean(err * err, axis=-1, keepdims=True))
        dy = err * (1.0 / D)
        dg_ref[...] += jnp.sum(dy * xhat, axis=0, keepdims=True)
        dxhat = dy * g_ref[...]
        dx_ref[...] = r * (dxhat - xhat * jnp.mean(dxhat * xhat, axis=-1, keepdims=True))

    return pl.pallas_call(
        body, grid=(T // tm,),
        in_specs=[pl.BlockSpec((tm, D), lambda i: (i, 0)), pl.BlockSpec((1, D), lambda i: (0, 0)),
                  pl.BlockSpec((tm, D), lambda i: (i, 0))],
        out_specs=[pl.BlockSpec((8, 128), lambda i: (0, 0)), pl.BlockSpec((tm, D), lambda i: (i, 0)),
                   pl.BlockSpec((1, D), lambda i: (0, 0))],
        out_shape=[jax.ShapeDtypeStruct((8, 128), f32), jax.ShapeDtypeStruct((T, D), f32),
                   jax.ShapeDtypeStruct((1, D), f32)],
        compiler_params=_cparams(("arbitrary",)), name=name,
    )(x, g, tgt)


def _slot(px, py, pc):
    return 4 * px + 2 * py + pc


def _with_own_block(s, me):
    zone = lax.empty((NDEV,) + s.shape, s.dtype)
    return lax.dynamic_update_slice(zone, s[None], (me,) + (0,) * s.ndim)


def all_gather(arrs, me, *, name):
    n = len(arrs)

    def body(*refs):
        ins, outs = refs[:n], refs[2 * n:3 * n]
        send_sems, recv_sems = refs[3 * n:]
        x, y, c = lax.axis_index("x"), lax.axis_index("y"), lax.axis_index("c")
        me, sibling = (x, y, c), (x, y, 1 - c)
        chips = [(1 - x, y), (x, 1 - y), (1 - x, 1 - y)]

        def copy(a, k, block, to, src=None):
            dst = outs[a].at[_slot(*block)]
            return pltpu.make_async_remote_copy(
                src_ref=dst if src is None else src, dst_ref=dst,
                send_sem=send_sems.at[a, k], recv_sem=recv_sems.at[a, k], device_id=to, device_id_type=MESH)

        first = []
        for a in range(n):
            first.append(copy(a, 0, me, sibling, src=ins[a]))
            first += [copy(a, 1 + j, me, (*chip, c), src=ins[a]) for j, chip in enumerate(chips)]
        for cp in first:
            cp.start()
        passed = []
        for j, chip in enumerate(chips):
            for a in range(n):
                copy(a, 1 + j, (*chip, c), me).wait_recv()
                fwd = copy(a, 4 + j, (*chip, c), sibling)
                fwd.start()
                passed.append(fwd)
        for a in range(n):
            copy(a, 0, sibling, me).wait_recv()
        for j, chip in enumerate(chips):
            for a in range(n):
                copy(a, 4 + j, (*chip, 1 - c), me).wait_recv()
        for cp in first + passed:
            cp.wait_send()

    any_spec = pl.BlockSpec(memory_space=pl.ANY)
    return pl.pallas_call(
        body,
        in_specs=[any_spec] * (2 * n), out_specs=[any_spec] * n,
        out_shape=[jax.ShapeDtypeStruct((NDEV,) + a.shape, a.dtype) for a in arrs],
        input_output_aliases={n + i: i for i in range(n)},
        scratch_shapes=[pltpu.SemaphoreType.DMA((n, 7)), pltpu.SemaphoreType.DMA((n, 7))],
        compiler_params=pltpu.CompilerParams(has_side_effects=True), name=name,
    )(*arrs, *[_with_own_block(a, me) for a in arrs])


HBM_SPEC = pl.BlockSpec(memory_space=pltpu.HBM)
SEM_SPEC = pl.BlockSpec(memory_space=pltpu.SEMAPHORE)
ANY_SPEC = pl.BlockSpec(memory_space=pl.ANY)
TOKEN_SHAPE = jax.ShapeDtypeStruct((8, 128), f32)
DATAFLOW_EFFECT = pltpu.SideEffectType.DATAFLOW_SIDE_EFFECTING


def _in_hbm(a):
    return pltpu.with_memory_space_constraint(a, pltpu.HBM)


def _hbm_like(a):
    return pltpu.HBM(a.shape, a.dtype)


def _mesh_pos():
    return lax.axis_index("x"), lax.axis_index("y"), lax.axis_index("c")


def _gather_targets(x, y, c):
    return [(x, y, 1 - c), (1 - x, y, c), (x, 1 - y, c), (1 - x, 1 - y, c)]


def gather_start(shards, me, after, *, name):
    n = len(shards)
    extra = [] if after is None else [after]

    def body(*refs):
        srcs, lands = refs[:n], refs[n:2 * n]
        send_sems, recv_sems = refs[2 * n + len(extra)], refs[2 * n + len(extra) + 1]
        token = refs[-1]
        x, y, c = _mesh_pos()
        me = _slot(x, y, c)
        for a in range(n):
            for k, to in enumerate(_gather_targets(x, y, c)):
                pltpu.make_async_remote_copy(
                    src_ref=srcs[a], dst_ref=lands[a].at[me], send_sem=send_sems.at[4 * a + k], recv_sem=recv_sems.at[4 * a + k],
                    device_id=to, device_id_type=MESH).start()
        token[...] = jnp.zeros(token.shape, f32)

    lands = [_with_own_block(s, me) for s in shards]
    sems = pltpu.SemaphoreType.DMA((4 * n,))
    out = pl.pallas_call(
        body, name=name,
        in_specs=[HBM_SPEC] * (2 * n) + [ANY_SPEC] * len(extra),
        out_specs=[SEM_SPEC, SEM_SPEC] + [HBM_SPEC] * (2 * n) + [pl.BlockSpec(memory_space=pltpu.VMEM)],
        out_shape=[sems, sems] + [_hbm_like(s) for s in shards] + [_hbm_like(l) for l in lands] + [TOKEN_SHAPE],
        input_output_aliases={i: 2 + i for i in range(2 * n)},
        compiler_params=pltpu.CompilerParams(has_side_effects=DATAFLOW_EFFECT),
    )(*[_in_hbm(s) for s in shards], *[_in_hbm(l) for l in lands], *extra)
    return out[0], out[1], out[2:2 + n], out[2 + n:2 + 2 * n], out[-1]


def gather_wait(send_sems, recv_sems, shards, lands, after, *, name):
    n = len(shards)
    after = list(after) if isinstance(after, (list, tuple)) else [after]

    def body(*refs):
        srcs, lands_in = refs[:n], refs[n:2 * n]
        send_sems, recv_sems = refs[2 * n], refs[2 * n + 1]
        x, y, c = _mesh_pos()
        for a in range(n):
            for k, frm in enumerate(_gather_targets(x, y, c)):
                cp = pltpu.make_async_remote_copy(
                    src_ref=srcs[a], dst_ref=lands_in[a].at[_slot(*frm)], send_sem=send_sems.at[4 * a + k],
                    recv_sem=recv_sems.at[4 * a + k], device_id=frm, device_id_type=MESH)
                cp.wait_send()
                cp.wait_recv()

    out = pl.pallas_call(
        body, name=name,
        in_specs=[HBM_SPEC] * (2 * n) + [SEM_SPEC, SEM_SPEC] + [ANY_SPEC] * len(after),
        out_specs=[HBM_SPEC] * (2 * n),
        out_shape=[_hbm_like(s) for s in shards] + [_hbm_like(l) for l in lands],
        input_output_aliases={i: i for i in range(2 * n)},
        compiler_params=pltpu.CompilerParams(has_side_effects=DATAFLOW_EFFECT),
    )(*shards, *lands, send_sems, recv_sems, *after)
    return out[:n], out[n:]


def gather_finish(lands, *, name):
    n = len(lands)

    def body(*refs):
        lands_in, lands_out = refs[:n], refs[n:2 * n]
        send_sems, recv_sems = refs[2 * n:]
        x, y, c = _mesh_pos()
        sibling = (x, y, 1 - c)
        chips = [(1 - x, y), (x, 1 - y), (1 - x, 1 - y)]
        sent = []
        for a in range(n):
            for j, chip in enumerate(chips):
                s = _slot(*chip, c)
                cp = pltpu.make_async_remote_copy(
                    src_ref=lands_in[a].at[s], dst_ref=lands_out[a].at[s], send_sem=send_sems.at[a, j],
                    recv_sem=recv_sems.at[a, j], device_id=sibling, device_id_type=MESH)
                cp.start()
                sent.append(cp)
        for a in range(n):
            for j, chip in enumerate(chips):
                s = _slot(*chip, 1 - c)
                pltpu.make_async_remote_copy(
                    src_ref=lands_in[a].at[s], dst_ref=lands_out[a].at[s], send_sem=send_sems.at[a, j],
                    recv_sem=recv_sems.at[a, j], device_id=sibling, device_id_type=MESH).wait_recv()
        for cp in sent:
            cp.wait_send()

    return pl.pallas_call(
        body, name=name,
        in_specs=[ANY_SPEC] * n, out_specs=[ANY_SPEC] * n,
        out_shape=[jax.ShapeDtypeStruct(l.shape, l.dtype) for l in lands],
        input_output_aliases={i: i for i in range(n)},
        scratch_shapes=[pltpu.SemaphoreType.DMA((n, 3)), pltpu.SemaphoreType.DMA((n, 3))],
        compiler_params=pltpu.CompilerParams(has_side_effects=True),
    )(*lands)


def _peer(x, y, c, r):
    return (x ^ ((r >> 2) & 1), y ^ ((r >> 1) & 1), c ^ (r & 1))


def exchange_start(items, lands, after, *, name):
    n, m = len(items), len(lands)
    extra = [] if after is None else [after]

    def body(*refs):
        gs, zones = refs[:n], refs[n:n + m]
        send_sems, recv_sems = refs[n + m + len(extra)], refs[n + m + len(extra) + 1]
        token = refs[-1]
        x, y, c = _mesh_pos()
        me = _slot(x, y, c)
        for r in (1, 2, 4, 3, 5, 6, 7):
            to = _peer(x, y, c, r)
            for a, (_, zi, l) in enumerate(items):
                pltpu.make_async_remote_copy(
                    src_ref=gs[a].at[_slot(*to)], dst_ref=zones[zi].at[me, l], send_sem=send_sems.at[7 * a + r - 1],
                    recv_sem=recv_sems.at[7 * a + r - 1], device_id=to, device_id_type=MESH).start()
        token[...] = jnp.zeros(token.shape, f32)

    gs = [g for g, _, _ in items]
    sems = pltpu.SemaphoreType.DMA((7 * n,))
    out = pl.pallas_call(
        body, name=name,
        in_specs=[HBM_SPEC] * (n + m) + [ANY_SPEC] * len(extra),
        out_specs=[SEM_SPEC, SEM_SPEC] + [HBM_SPEC] * (n + m) + [pl.BlockSpec(memory_space=pltpu.VMEM)],
        out_shape=[sems, sems] + [_hbm_like(g) for g in gs] + [_hbm_like(z) for z in lands] + [TOKEN_SHAPE],
        input_output_aliases={i: 2 + i for i in range(n + m)},
        compiler_params=pltpu.CompilerParams(has_side_effects=DATAFLOW_EFFECT),
    )(*[_in_hbm(g) for g in gs], *[_in_hbm(z) for z in lands], *extra)
    return out[0], out[1], out[2:2 + n], out[2 + n:2 + n + m], out[-1]


def exchange_wait(parts, lands, after, *, name):
    m = len(lands)
    flat_gs = [g for _, _, _, gs in parts for g in gs]
    ng = len(flat_gs)
    after = list(after) if isinstance(after, (list, tuple)) else [after]

    def body(*refs):
        gs, zones = refs[:ng], refs[ng:ng + m]
        sem_refs = refs[ng + m:ng + m + 2 * len(parts)]
        x, y, c = _mesh_pos()
        base = 0
        for pi, (items, _, _, _) in enumerate(parts):
            send_sems, recv_sems = sem_refs[2 * pi], sem_refs[2 * pi + 1]
            for r in range(1, NDEV):
                frm = _peer(x, y, c, r)
                for a, (_, zi, l) in enumerate(items):
                    cp = pltpu.make_async_remote_copy(
                        src_ref=gs[base + a].at[_slot(*frm)], dst_ref=zones[zi].at[_slot(*frm), l],
                        send_sem=send_sems.at[7 * a + r - 1], recv_sem=recv_sems.at[7 * a + r - 1],
                        device_id=frm, device_id_type=MESH)
                    cp.wait_send()
                    cp.wait_recv()
            base += len(items)

    sem_args = [s for _, ss, rs, _ in parts for s in (ss, rs)]
    out = pl.pallas_call(
        body, name=name,
        in_specs=[HBM_SPEC] * (ng + m) + [SEM_SPEC] * len(sem_args) + [ANY_SPEC] * len(after),
        out_specs=[HBM_SPEC] * (ng + m),
        out_shape=[_hbm_like(g) for g in flat_gs] + [_hbm_like(z) for z in lands],
        input_output_aliases={i: i for i in range(ng + m)},
        compiler_params=pltpu.CompilerParams(has_side_effects=DATAFLOW_EFFECT),
    )(*flat_gs, *lands, *sem_args, *after)
    return out[ng:]


def sum8(g, *, name):
    _, R, C = g.shape
    tr = R

    def body(g_ref, o_ref):
        s = g_ref[0]
        for k in range(1, NDEV):
            s = s + g_ref[k]
        o_ref[...] = s

    return pl.pallas_call(
        body, grid=(R // tr,),
        in_specs=[pl.BlockSpec((NDEV, tr, C), lambda i: (0, i, 0))],
        out_specs=pl.BlockSpec((tr, C), lambda i: (i, 0)),
        out_shape=jax.ShapeDtypeStruct((R, C), f32),
        compiler_params=_cparams(("parallel",)), name=name,
    )(g)


def _adam_math(w, g, m, v):
    m = ADAM_B1 * m + (1.0 - ADAM_B1) * g
    v = ADAM_B2 * v + (1.0 - ADAM_B2) * (g * g)
    m_hat = m / (1.0 - ADAM_B1 ** ADAM_STEP)
    v_hat = v / (1.0 - ADAM_B2 ** ADAM_STEP)
    delta = -ADAM_LR * (m_hat / (jnp.sqrt(v_hat) + ADAM_EPS) + ADAM_WD * w)
    return delta, m, v


def adamw_slices(w, m, v, recv, *, name):
    return adamw_families([w], [m], [v], [recv], tc=w.shape[2], name=name)


def adamw_families(ws, ms, vs, recvs, *, tc, l0=0, nl=None, prev=None, name):
    nf = len(ws)
    L, R, C = ws[0].shape
    nl = L - l0 if nl is None else nl
    prevs = [] if prev is None else list(prev)

    def body(*refs):
        outs = refs[4 * nf + len(prevs):]
        for q in range(nf):
            w_ref, m_ref, v_ref, r_ref = refs[q], refs[nf + q], refs[2 * nf + q], refs[3 * nf + q]
            g = r_ref[0].astype(f32)
            for k in range(1, NDEV):
                g = g + r_ref[k].astype(f32)
            delta, nm, nv = _adam_math(w_ref[...], g, m_ref[...], v_ref[...])
            outs[4 * q][...] = g
            outs[4 * q + 1][...] = delta
            outs[4 * q + 2][...] = nm
            outs[4 * q + 3][...] = nv

    wspec = pl.BlockSpec((None, R, tc), lambda l, j: (l0 + l, 0, j))
    rspec = pl.BlockSpec((NDEV, None, R, tc), lambda l, j: (0, l0 + l, 0, j))
    return pl.pallas_call(
        body, grid=(nl, C // tc),
        in_specs=[wspec] * (3 * nf) + [rspec] * nf + [pl.BlockSpec(memory_space=pl.ANY)] * len(prevs),
        out_specs=[wspec] * (4 * nf), out_shape=[jax.ShapeDtypeStruct((L, R, C), f32)] * (4 * nf),
        input_output_aliases={4 * nf + q: q for q in range(len(prevs))},
        compiler_params=_cparams(("parallel", "parallel")), name=name,
    )(*ws, *ms, *vs, *recvs, *prevs)


def adamw_slices_t(w, m, v, recv, *, name):
    L, R, C = w.shape

    def body(w_ref, m_ref, v_ref, r_ref, g_ref, d_ref, nm_ref, nv_ref):
        gt = r_ref[0].astype(f32)
        for k in range(1, NDEV):
            gt = gt + r_ref[k].astype(f32)
        g = gt.T
        delta, nm, nv = _adam_math(w_ref[...], g, m_ref[...], v_ref[...])
        g_ref[...] = g
        d_ref[...] = delta
        nm_ref[...] = nm
        nv_ref[...] = nv

    wspec = pl.BlockSpec((None, R, C), lambda l: (l, 0, 0))
    return pl.pallas_call(
        body, grid=(L,),
        in_specs=[wspec, wspec, wspec, pl.BlockSpec((NDEV, None, C, R), lambda l: (0, l, 0, 0))],
        out_specs=[wspec] * 4, out_shape=[jax.ShapeDtypeStruct((L, R, C), f32)] * 4,
        compiler_params=_cparams(("parallel",)), name=name,
    )(w, m, v, recv)


def adamw_plain(w, m, v, g, *, name):
    R, C = w.shape

    def body(w_ref, m_ref, v_ref, g_ref, d_ref, nm_ref, nv_ref):
        delta, nm, nv = _adam_math(w_ref[...], g_ref[...], m_ref[...], v_ref[...])
        d_ref[...] = delta
        nm_ref[...] = nm
        nv_ref[...] = nv

    return pl.pallas_call(
        body, out_shape=[jax.ShapeDtypeStruct((R, C), f32)] * 3, name=name,
    )(w, m, v, g)


def _chunk_causal_mask():
    pos = jnp.arange(GBLK)
    return (pos[None, :] // CHUNK) <= (pos[:, None] // CHUNK)


def kernel(x, norm_mix_g, norm_ffn_g, final_norm_g, a_w_in, a_v_norm_g, a_w_s, a_b_s, a_w_out, b_w_in, b_conv_w, b_w_out, c_w_in, c_w_grp, c_scale, c_w_out, f_w_gate, f_w_up, f_w_down, loss_target, m_norm_mix_g, m_norm_ffn_g, m_final_norm_g, m_a_w_in, m_a_v_norm_g, m_a_w_s, m_a_b_s, m_a_w_out, m_b_w_in, m_b_conv_w, m_b_w_out, m_c_w_in, m_c_w_grp, m_c_scale, m_c_w_out, m_f_w_gate, m_f_w_up, m_f_w_down, v_norm_mix_g, v_norm_ffn_g, v_final_norm_g, v_a_w_in, v_a_v_norm_g, v_a_w_s, v_a_b_s, v_a_w_out, v_b_w_in, v_b_conv_w, v_b_w_out, v_c_w_in, v_c_w_grp, v_c_scale, v_c_w_out, v_f_w_gate, v_f_w_up, v_f_w_down):
    B, S, _ = x.shape
    T = B * S
    tm = min(512, S)
    tms = min(512, T)
    tma = min(256, S)
    tmf = min(256, T)
    me = _slot(lax.axis_index("x"), lax.axis_index("y"), lax.axis_index("c"))

    def tb(w):
        return jnp.transpose(w, (0, 2, 1)).astype(bf16)

    small = jnp.concatenate([a_v_norm_g, b_conv_w[0], c_scale, jnp.zeros((2, GBLK), f32)], axis=0)
    started = [gather_start([tb(a_w_in[0:1])[0], a_w_out[0].astype(bf16), small], me, None, name="gather_start_0")]
    tok = started[0][4]

    def ffn_shards(l, z):
        return [tb(f_w_gate[l:l + 1] + z)[0], tb(f_w_up[l:l + 1] + z)[0], (f_w_down[l] + z).astype(bf16)]

    started.append(gather_start(ffn_shards(0, tok[0, 0]), me, tok, name="gather_start_1"))
    tok = started[-1][4]

    def start_later_stages(z, after):
        stage_shards = [
            [tb(b_w_in + z)[0], (b_w_out[0] + z).astype(bf16)] + ffn_shards(1, z),
            [(c_w_in[0] + z).astype(bf16), (c_w_grp[0] + z).astype(bf16), (c_w_out[0] + z).astype(bf16)] + ffn_shards(2, z),
            [tb(a_w_in[1:2] + z)[0], (a_w_out[1] + z).astype(bf16)] + ffn_shards(3, z),
        ]
        for q, shards in enumerate(stage_shards):
            started.append(gather_start(shards, me, after, name=f"gather_start_{q + 2}"))
            after = started[-1][4]
        return after

    def stage_weights(s, after):
        send_sems, recv_sems, shards, lands, _ = started[s]
        _, lands = gather_wait(send_sems, recv_sems, shards, lands, after, name=f"gather_wait_{s}")
        lands = gather_finish(lands, name=f"gather_finish_{s}")
        return [t.reshape(-1, t.shape[-1]) if t.ndim == 3 else t for t in lands]

    wm_masked = jnp.where(_chunk_causal_mask()[None, None], a_w_s, 0.0).astype(bf16)
    bsb = jnp.broadcast_to(a_b_s[:, :, :, None], a_b_s.shape + (GBLK,))

    xc = x.reshape(T, D)
    saved, weights = [], []
    for i in range(DEPTH):
        kind, j = i % 3, i // 3
        gmix = norm_mix_g[i][None]
        if i == 0:
            w_in, w_out, small_g = stage_weights(0, tok)
            small_full = jnp.transpose(small_g.reshape(NDEV, 8, GBLK), (1, 0, 2)).reshape(8, D)
            gv_full, cw_full, cs_full = small_full[0:2], small_full[2:5], small_full[5:6]
            w = [w_in, w_out]
        else:
            w = stage_weights(i + 1, xc)
        if kind == 0:
            w_in = w[0]
            p, h, y, tok_a = a_fwd(xc, gmix, w_in, gv_full[j][None], wm_masked[j], bsb[j], tm=tma, name=f"a_fwd_{i}")
            if i == 0:
                tok_late = start_later_stages(tok_a[0, 0], tok_a)
        elif kind == 1:
            w_in, w_out = w[:2]
            p, h, y = b_fwd(xc, gmix, w_in, cw_full, tm=tm, S=S, name=f"b_fwd_{i}")
        else:
            w_in, w_grp, w_out = w[:3]
            p, h, y = c_fwd(xc, gmix, w_in, w_grp, cs_full, tm=tm, S=S, name=f"c_fwd_{i}")
        if i == 0:
            w = w + stage_weights(1, [y, tok_late])
        weights.append(w)
        w_out, w_fg, w_fu, w_fd = w[-4:]
        x1, x2, pg, pu, a, h2 = ffn_fwd(y, w_out, xc, norm_ffn_g[i][None], w_fg, w_fu, w_fd, tm=tmf, name=f"f_fwd_{i}")
        saved.append((xc, p, h, y, x1, pg, pu, h2, a))
        xc = x2

    loss_blk, dx, d_final = final_loss(xc, final_norm_g[None], loss_target.reshape(T, D), tm=tms, name="final_loss")
    loss = lax.psum(loss_blk[0, 0], AXES)

    zone_shapes = [(2, 256, D), (2, GBLK, D), (1, 384, D), (1, GBLK, D), (1, GBLK, D), (1, GBLK, C_GDIM), (1, GBLK, D),
                   (DEPTH, F_LOC, D), (DEPTH, F_LOC, D), (DEPTH, F_LOC, D)]
    Z_AIN, Z_AOUT, Z_BIN, Z_BOUT, Z_CIN, Z_CGRP, Z_COUT, Z_FG, Z_FU, Z_FD = range(10)
    lands = [lax.empty((NDEV,) + s, bf16) for s in zone_shapes]
    parts = []

    def exchange(items, after, name):
        nonlocal lands
        for g, zi, l in items:
            own = lax.dynamic_index_in_dim(g, me, axis=0, keepdims=True)[None]
            lands[zi] = lax.dynamic_update_slice(lands[zi], own, (me, l) + (0,) * (g.ndim - 1))
        idx = sorted({zi for _, zi, _ in items})
        local = [(g, idx.index(zi), l) for g, zi, l in items]
        send_sems, recv_sems, gs, zs, token = exchange_start(local, [lands[zi] for zi in idx], after, name=name)
        for q, zi in enumerate(idx):
            lands[zi] = zs[q]
        parts.append(([(None, zi, l) for _, zi, l in items], send_sems, recv_sems, gs))
        return token

    d_mix, d_ffn = [None] * DEPTH, [None] * DEPTH
    d_gv, d_wm, d_bs = [None] * 2, [None] * 2, [None] * 2
    tok = None
    for i in reversed(range(DEPTH)):
        kind, j = i % 3, i // 3
        xin, p, h, y, x1, pg, pu, h2, a = saved[i]
        w = weights[i]
        w_in, w_out = w[0], w[-4]
        w_fg, w_fu, w_fd = w[-3:]
        dpg, dpu, dxb, dx1, d_ffn[i], dy, dx1b = ffn_bwd(dx, w_fd, pg, pu, w_fg, w_fu, x1, norm_ffn_g[i][None], tok, w_out,
                                                         tm=tmf, name=f"f_bwd_{i}")
        gw_g, gw_u, gw_d = ffn_dw(dpg, dpu, a, h2, dxb, name=f"f_dw_{i}")
        tok = exchange([(gw_g, Z_FG, i), (gw_u, Z_FU, i), (gw_d, Z_FD, i)], tok, f"exchange_start_ffn_{i}")
        gmix = norm_mix_g[i][None]
        if kind == 0:
            dp, d_gv[j], d_wm[j], d_bs[j], dx, d_mix[i] = a_mid_bwd(
                p, dy, gv_full[j][None], wm_masked[j], bsb[j], w_in, xin, gmix, tok, dx1, tm=tma, name=f"a_bwd_{i}")
            gw_in, gw_out = mm_tn2(dp, h, y, dx1b, name=f"a_dw_{i}")
            items = [(gw_in, Z_AIN, j), (gw_out, Z_AOUT, j)]
        elif kind == 1:
            dp, d_cw, dx, d_mix[i] = b_mid_bwd(p, dy, cw_full, w_in, xin, gmix, tok, dx1, tm=tm, S=S, name=f"b_bwd_{i}")
            gw_in, gw_out = mm_tn2(dp, h, y, dx1b, name=f"b_dw_{i}")
            items = [(gw_in, Z_BIN, 0), (gw_out, Z_BOUT, 0)]
        else:
            dp, d_cs, d_wgrp, dx, d_mix[i] = c_mid_bwd(p, dy, w[1], cs_full, w_in, xin, gmix, tok, dx1, tm=tm, S=S,
                                                       name=f"c_bwd_{i}")
            gw_cgrp = jnp.transpose(d_wgrp.reshape(C_GROUPS, NDEV, C_GDIM // NDEV, C_GDIM), (1, 0, 2, 3)).astype(bf16)
            gw_in, gw_out = mm_tn2(h, dp, y, dx1b, name=f"c_dw_{i}")
            items = [(gw_in, Z_CIN, 0), (gw_cgrp.reshape(NDEV, GBLK, C_GDIM), Z_CGRP, 0), (gw_out, Z_COUT, 0)]
        tok = exchange(items, tok, f"exchange_start_mix_{i}")
    grad_x = dx.reshape(B, S, D)

    small_parts = [
        jnp.concatenate(d_mix, axis=0).reshape(-1, GBLK),
        jnp.concatenate(d_ffn, axis=0).reshape(-1, GBLK),
        d_final.reshape(-1, GBLK),
        jnp.concatenate(d_gv, axis=0).reshape(-1, GBLK),
        jnp.where(_chunk_causal_mask()[None, None], jnp.stack(d_wm), 0.0).reshape(-1, GBLK),
        jnp.stack(d_bs).reshape(-1, GBLK),
        d_cw.reshape(-1, GBLK),
        d_cs.reshape(-1, GBLK),
    ]
    small_rows = [q.shape[0] for q in small_parts]
    sg_send, sg_recv, sg_shards, sg_lands, _ = gather_start(
        [jnp.concatenate(small_parts, axis=0)], me, tok, name="gather_small_start")

    def plain(w, m, v, g, name):
        shp = w.shape
        w2, m2, v2, g2 = (t.reshape(-1, shp[-1]) for t in (w, m, v, g))
        return tuple(t.reshape(shp) for t in adamw_plain(w2, m2, v2, g2, name=name))

    def sliced(w, m, v, recv, name):
        shp = w.shape
        w3, m3, v3 = (t.reshape((shp[0], -1, shp[-1])) for t in (w, m, v))
        return tuple(t.reshape(shp) for t in adamw_slices(w3, m3, v3, recv, name=name))

    lands = list(exchange_wait(parts[:6], lands, dx, name="exchange_wait"))
    _, _, r_bin, r_bout, r_cin, r_cgrp, r_cout, r_fg, r_fu, r_fd = lands
    res = {}
    res["b_w_in"] = adamw_slices_t(b_w_in, m_b_w_in, v_b_w_in, r_bin, name="adam_b_w_in")
    res["b_w_out"] = sliced(b_w_out, m_b_w_out, v_b_w_out, r_bout, "adam_b_w_out")
    res["c_w_in"] = sliced(c_w_in, m_c_w_in, v_c_w_in, r_cin, "adam_c_w_in")
    res["c_w_grp"] = sliced(c_w_grp, m_c_w_grp, v_c_w_grp, r_cgrp, "adam_c_w_grp")
    res["c_w_out"] = sliced(c_w_out, m_c_w_out, v_c_w_out, r_cout, "adam_c_w_out")
    def tr(t):
        return jnp.transpose(t, (0, 2, 1))

    f_ws = [tr(f_w_gate), tr(f_w_up), f_w_down]
    f_ms = [tr(m_f_w_gate), tr(m_f_w_up), m_f_w_down]
    f_vs = [tr(v_f_w_gate), tr(v_f_w_up), v_f_w_down]
    ff = adamw_families(f_ws, f_ms, f_vs, [r_fg, r_fu, r_fd], tc=256, l0=1, name="adam_ffn_123")
    big = [res[n][1] for n in ("b_w_in", "b_w_out", "c_w_in", "c_w_grp", "c_w_out")] + [ff[1], ff[5], ff[9]]

    ffn0_items, ffn0_send, ffn0_recv, ffn0_gs = parts[6]
    ffn0_zones = [Z_FG, Z_FU, Z_FD]
    ffn0_part = ([(None, ffn0_zones.index(zi), l) for _, zi, l in ffn0_items], ffn0_send, ffn0_recv, ffn0_gs)
    r_fg, r_fu, r_fd = exchange_wait([ffn0_part], [lands[zi] for zi in ffn0_zones], big, name="exchange_wait_ffn0")
    ff = adamw_families(f_ws, f_ms, f_vs, [r_fg, r_fu, r_fd], tc=256, l0=0, nl=1, prev=ff, name="adam_ffn_0")
    res["f_w_gate"], res["f_w_up"], res["f_w_down"] = tuple(tr(t) for t in ff[0:4]), tuple(tr(t) for t in ff[4:8]), tuple(ff[8:12])

    last_items, last_send, last_recv, last_gs = parts[7]
    last_zones = [Z_AIN, Z_AOUT]
    last_part = ([(None, last_zones.index(zi), l) for _, zi, l in last_items], last_send, last_recv, last_gs)
    r_ain, r_aout = exchange_wait([last_part], [lands[zi] for zi in last_zones], [ff[1], ff[5], ff[9]], name="exchange_wait_last")
    res["a_w_in"] = adamw_slices_t(a_w_in, m_a_w_in, v_a_w_in, r_ain, name="adam_a_w_in")
    res["a_w_out"] = sliced(a_w_out, m_a_w_out, v_a_w_out, r_aout, "adam_a_w_out")

    _, sg_lands = gather_wait(sg_send, sg_recv, sg_shards, sg_lands, [res["a_w_in"][1], res["a_w_out"][1]], name="gather_small_wait")
    (gs_all,) = gather_finish(sg_lands, name="gather_small_finish")
    gs = sum8(gs_all, name="sum_small_grads")
    offs = [0]
    for r in small_rows:
        offs.append(offs[-1] + r)
    sp = [gs[offs[q]:offs[q + 1]] for q in range(len(small_parts))]
    grad_norm_mix_g = sp[0].reshape(DEPTH, D)
    grad_norm_ffn_g = sp[1].reshape(DEPTH, D)
    grad_final_norm_g = sp[2].reshape(D)
    grad_a_w_s = sp[4].reshape(a_w_s.shape)
    grad_a_b_s = sp[5].reshape(a_b_s.shape)

    def my_cols(full):
        return lax.dynamic_slice_in_dim(full, me * GBLK, GBLK, axis=1)

    grad_a_v_norm_g = my_cols(sp[3].reshape(2, D))
    grad_b_conv_w = my_cols(sp[6].reshape(3, D))[None]
    grad_c_scale = my_cols(sp[7].reshape(1, D))
    res["norm_mix_g"] = (grad_norm_mix_g,) + plain(norm_mix_g, m_norm_mix_g, v_norm_mix_g, grad_norm_mix_g, "adam_norm_mix")
    res["norm_ffn_g"] = (grad_norm_ffn_g,) + plain(norm_ffn_g, m_norm_ffn_g, v_norm_ffn_g, grad_norm_ffn_g, "adam_norm_ffn")
    res["final_norm_g"] = (grad_final_norm_g,) + tuple(
        t.reshape(D) for t in plain(final_norm_g[None], m_final_norm_g[None], v_final_norm_g[None], grad_final_norm_g[None], "adam_final"))
    res["a_v_norm_g"] = (grad_a_v_norm_g,) + plain(a_v_norm_g, m_a_v_norm_g, v_a_v_norm_g, grad_a_v_norm_g, "adam_a_v_norm")
    res["a_w_s"] = (grad_a_w_s,) + plain(a_w_s, m_a_w_s, v_a_w_s, grad_a_w_s, "adam_a_w_s")
    res["a_b_s"] = (grad_a_b_s,) + plain(a_b_s, m_a_b_s, v_a_b_s, grad_a_b_s, "adam_a_b_s")
    res["b_conv_w"] = (grad_b_conv_w,) + plain(b_conv_w, m_b_conv_w, v_b_conv_w, grad_b_conv_w, "adam_b_conv")
    res["c_scale"] = (grad_c_scale,) + plain(c_scale, m_c_scale, v_c_scale, grad_c_scale, "adam_c_scale")

    order = ["norm_mix_g", "norm_ffn_g", "final_norm_g", "a_w_in", "a_v_norm_g", "a_w_s", "a_b_s", "a_w_out", "b_w_in",
             "b_conv_w", "b_w_out", "c_w_in", "c_w_grp", "c_scale", "c_w_out", "f_w_gate", "f_w_up", "f_w_down"]
    return (loss, grad_x, *[res[n][0] for n in order], *[res[n][1] for n in order],
            *[res[n][2] for n in order], *[res[n][3] for n in order])
```

```python
import jax
import jax.numpy as jnp
from jax import lax
from jax.experimental import pallas as pl
from jax.experimental.pallas import tpu as pltpu

f32 = jnp.float32
bf16 = jnp.bfloat16

NDEV = 8
D = 1024
EPS = 1e-6
CHUNK = 64
GBLK = 128
A_GROUPS = 8
C_GROUPS = 4
C_GDIM = 256
POOL_WINDOWS = (2, 4, 8, 16)
HALO = 16
COLS = 256
F_LOC = 352
MXU_COLS = 256
DEPTH = 4
AXES = ("x", "y", "c")
MESH = pl.DeviceIdType.MESH

ADAM_LR = 0.001
ADAM_B1 = 0.9
ADAM_B2 = 0.999
ADAM_EPS = 1e-08
ADAM_WD = 0.01
ADAM_STEP = 10

VMEM_LIMIT = 56 * 1024 * 1024


def _cparams(sem):
    return pltpu.CompilerParams(dimension_semantics=sem, vmem_limit_bytes=VMEM_LIMIT)


def _gelu(z):
    return 0.5 * z * (1.0 + lax.erf(z * 0.7071067811865476))


def _gelu_cdf(z):
    return 0.5 * (1.0 + lax.erf(z * 0.7071067811865476))


def _gelu_grad(z, cdf):
    return cdf + z * jnp.exp(-0.5 * z * z) * 0.3989422804014327


def _dot(a, b):
    return jnp.dot(a, b, preferred_element_type=f32)


def _dot_nt(a, b):
    return lax.dot_general(a, b, (((1,), (1,)), ((), ())), preferred_element_type=f32)


def _dot_tn(a, b):
    return lax.dot_general(a, b, (((0,), (0,)), ((), ())), preferred_element_type=f32)


def _col_chunks(n, width=1024):
    return [(c, min(c + width, n)) for c in range(0, n, width)]


def ffn_fwd(y, w_out, xin, g, wtg, wtu, wd, *, tm, name):
    T, K = y.shape
    H = wtg.shape[0]

    def body(y_ref, wo_ref, xin_ref, g_ref, wg_ref, wu_ref, wd_ref, x_ref, x2_ref, pg_ref, pu_ref, a_ref, h_ref):
        xv = xin_ref[...] + _dot(y_ref[...], wo_ref[...])
        x_ref[...] = xv
        r = lax.rsqrt(jnp.mean(xv * xv, axis=-1, keepdims=True) + EPS)
        h = (xv * r * g_ref[...]).astype(bf16)
        h_ref[...] = h
        out = None
        for c0, c1 in _col_chunks(H):
            gate = _dot_nt(h, wg_ref[c0:c1, :])
            up = _dot_nt(h, wu_ref[c0:c1, :])
            pg_ref[:, c0:c1] = gate.astype(bf16)
            pu_ref[:, c0:c1] = up.astype(bf16)
            ac = (gate * jax.nn.sigmoid(gate) * up).astype(bf16)
            a_ref[:, c0:c1] = ac
            part = _dot(ac, wd_ref[c0:c1, :])
            out = part if out is None else out + part
        x2_ref[...] = xv + out

    wspec = pl.BlockSpec((H, D), lambda i: (0, 0), pipeline_mode=pl.Buffered(1))
    hspec = pl.BlockSpec((tm, H), lambda i: (i, 0))
    dspec = pl.BlockSpec((tm, D), lambda i: (i, 0))
    return pl.pallas_call(
        body, grid=(T // tm,),
        in_specs=[pl.BlockSpec((tm, K), lambda i: (i, 0)),
                  pl.BlockSpec((K, D), lambda i: (0, 0), pipeline_mode=pl.Buffered(1)),
                  dspec, pl.BlockSpec((1, D), lambda i: (0, 0)), wspec, wspec, wspec],
        out_specs=[dspec, dspec, hspec, hspec, hspec, dspec],
        out_shape=[jax.ShapeDtypeStruct((T, D), f32)] * 2 + [jax.ShapeDtypeStruct((T, H), bf16)] * 3
        + [jax.ShapeDtypeStruct((T, D), bf16)],
        compiler_params=_cparams(("parallel",)), name=name,
    )(y, w_out, xin, g, wtg, wtu, wd)


def ffn_bwd(dx, wd, pg, pu, wtg, wtu, x, g, tok, w_next, *, tm, name):
    T = dx.shape[0]
    H = wd.shape[0]
    kn = w_next.shape[0]
    toks = [] if tok is None else [tok]

    def body(dx_ref, wd_ref, pg_ref, pu_ref, wg_ref, wu_ref, x_ref, g_ref, *rest):
        wn_ref = rest[len(toks)]
        dg_ref, du_ref, dxb_ref, dx1_ref, dgn_ref, dy_ref, dx1b_ref = rest[len(toks) + 1:]

        @pl.when(pl.program_id(0) == 0)
        def _():
            dgn_ref[...] = jnp.zeros(dgn_ref.shape, f32)

        dxv = dx_ref[...]
        dxb = dxv.astype(bf16)
        dxb_ref[...] = dxb
        dh = None
        for c0, c1 in _col_chunks(H):
            da = _dot_nt(dxb, wd_ref[c0:c1, :])
            gate = pg_ref[:, c0:c1].astype(f32)
            up = pu_ref[:, c0:c1].astype(f32)
            sg = jax.nn.sigmoid(gate)
            dgc = (da * up * (sg * (1.0 + gate * (1.0 - sg)))).astype(bf16)
            duc = (da * gate * sg).astype(bf16)
            dg_ref[:, c0:c1] = dgc
            du_ref[:, c0:c1] = duc
            part = _dot(dgc, wg_ref[c0:c1, :]) + _dot(duc, wu_ref[c0:c1, :])
            dh = part if dh is None else dh + part
        gv = g_ref[...] + rest[0][0:1, 0:1] if toks else g_ref[...]
        xv = x_ref[...]
        r = lax.rsqrt(jnp.mean(xv * xv, axis=-1, keepdims=True) + EPS)
        xhat = xv * r
        dgn_ref[...] += jnp.sum(dh * xhat, axis=0, keepdims=True)
        dxhat = dh * gv
        dx1 = dxv + r * (dxhat - xhat * jnp.mean(dxhat * xhat, axis=-1, keepdims=True))
        dx1_ref[...] = dx1
        dx1b = dx1.astype(bf16)
        dx1b_ref[...] = dx1b
        dy_ref[...] = _dot_nt(dx1b, wn_ref[...]).astype(bf16)

    dspec = pl.BlockSpec((tm, D), lambda i: (i, 0))
    hspec = pl.BlockSpec((tm, H), lambda i: (i, 0))
    wspec = pl.BlockSpec((H, D), lambda i: (0, 0), pipeline_mode=pl.Buffered(1))
    gspec = pl.BlockSpec((1, D), lambda i: (0, 0))
    return pl.pallas_call(
        body, grid=(T // tm,),
        in_specs=[dspec, wspec, hspec, hspec, wspec, wspec, dspec, gspec]
        + [pl.BlockSpec((8, 128), lambda i: (0, 0))] * len(toks)
        + [pl.BlockSpec((kn, D), lambda i: (0, 0), pipeline_mode=pl.Buffered(1))],
        out_specs=[hspec, hspec, dspec, dspec, gspec, pl.BlockSpec((tm, kn), lambda i: (i, 0)), dspec],
        out_shape=[jax.ShapeDtypeStruct((T, H), bf16)] * 2 + [jax.ShapeDtypeStruct((T, D), bf16),
                   jax.ShapeDtypeStruct((T, D), f32), jax.ShapeDtypeStruct((1, D), f32),
                   jax.ShapeDtypeStruct((T, kn), bf16), jax.ShapeDtypeStruct((T, D), bf16)],
        compiler_params=_cparams(("arbitrary",)), name=name,
    )(dx, wd, pg, pu, wtg, wtu, x, g, *toks, w_next)


def mm_tn2(a1, b1, a2, b2, *, name):
    T, K1 = a1.shape
    K2 = a2.shape[1]
    n1, n2 = b1.shape[1], b2.shape[1]
    last2 = K2 // MXU_COLS - 1

    def body(a1_ref, b1_ref, a2_ref, b2_ref, o1_ref, o2_ref):
        o1_ref[...] = _dot_tn(a1_ref[...], b1_ref[...]).astype(bf16)

        @pl.when(pl.program_id(0) <= last2)
        def _():
            o2_ref[...] = _dot_tn(a2_ref[...], b2_ref[...]).astype(bf16)

    o1, o2 = pl.pallas_call(
        body, grid=(K1 // MXU_COLS,),
        in_specs=[pl.BlockSpec((T, MXU_COLS), lambda d: (0, d)),
                  pl.BlockSpec((T, n1), lambda d: (0, 0), pipeline_mode=pl.Buffered(1)),
                  pl.BlockSpec((T, MXU_COLS), lambda d: (0, jnp.minimum(d, last2))),
                  pl.BlockSpec((T, n2), lambda d: (0, 0), pipeline_mode=pl.Buffered(1))],
        out_specs=[pl.BlockSpec((MXU_COLS, n1), lambda d: (d, 0)),
                   pl.BlockSpec((MXU_COLS, n2), lambda d: (jnp.minimum(d, last2), 0))],
        out_shape=[jax.ShapeDtypeStruct((K1, n1), bf16), jax.ShapeDtypeStruct((K2, n2), bf16)],
        compiler_params=_cparams(("arbitrary",)), name=name,
    )(a1, b1, a2, b2)
    return o1.reshape(NDEV, K1 // NDEV, n1), o2.reshape(NDEV, K2 // NDEV, n2)


def ffn_dw(dpg, dpu, a, h2, dxb, *, name):
    T, H = dpg.shape

    def body(g_ref, u_ref, a_ref, h_ref, x_ref, og_ref, ou_ref, od_ref):
        h = h_ref[...]
        og_ref[...] = _dot_tn(g_ref[...], h).astype(bf16)
        ou_ref[...] = _dot_tn(u_ref[...], h).astype(bf16)
        od_ref[...] = _dot_tn(a_ref[...], x_ref[...]).astype(bf16)

    blk = pl.BlockSpec((T, MXU_COLS), lambda d: (0, d))
    full = pl.BlockSpec((T, D), lambda d: (0, 0), pipeline_mode=pl.Buffered(1))
    out = pl.BlockSpec((MXU_COLS, D), lambda d: (d, 0))
    outs = pl.pallas_call(
        body, grid=(H // MXU_COLS,),
        in_specs=[blk, blk, blk, full, full], out_specs=[out] * 3,
        out_shape=[jax.ShapeDtypeStruct((H, D), bf16)] * 3,
        compiler_params=_cparams(("parallel",)), name=name,
    )(dpg, dpu, a, h2, dxb)
    return [o.reshape(NDEV, H // NDEV, D) for o in outs]


def _normbwd_tail(dh, x_ref, g_ref, tok_ref, dres_ref, dx_ref, dgn_ref):
    gv = g_ref[...] + tok_ref[0:1, 0:1]
    xv = x_ref[...]
    r = lax.rsqrt(jnp.mean(xv * xv, axis=-1, keepdims=True) + EPS)
    xhat = xv * r
    dgn_ref[...] += jnp.sum(dh * xhat, axis=0, keepdims=True)
    dxhat = dh * gv
    dx_ref[...] = dres_ref[...] + r * (dxhat - xhat * jnp.mean(dxhat * xhat, axis=-1, keepdims=True))


def _normbwd_specs(tm, w):
    dspec = pl.BlockSpec((tm, D), lambda i: (i, 0))
    gspec = pl.BlockSpec((1, D), lambda i: (0, 0))
    ins = [pl.BlockSpec(w.shape, lambda i: (0, 0), pipeline_mode=pl.Buffered(1)), dspec, gspec,
           pl.BlockSpec((8, 128), lambda i: (0, 0)), dspec]
    return ins, [dspec, gspec]


def _prev_halo_spec(tm, ncol):
    return pl.BlockSpec((HALO, ncol), lambda i: (jnp.maximum(i * (tm // HALO) - 1, 0), 0))


def _next_halo_spec(tm, ncol, T):
    return pl.BlockSpec((HALO, ncol), lambda i: (jnp.minimum((i + 1) * (tm // HALO), T // HALO - 1), 0))


def a_fwd(x, g, wt, gv, wm, bsb, *, tm, name):
    T = x.shape[0]

    def body(x_ref, g_ref, wt_ref, gv_ref, wm_ref, bs_ref, p_ref, h_ref, y_ref, tok_ref, vn_sc):
        tok_ref[...] = jnp.zeros(tok_ref.shape, f32)
        xv = x_ref[...]
        r = lax.rsqrt(jnp.mean(xv * xv, axis=-1, keepdims=True) + EPS)
        h = (xv * r * g_ref[...]).astype(bf16)
        h_ref[...] = h
        p_ref[...] = _dot_nt(h, wt_ref[...]).astype(bf16)
        v = _gelu(p_ref[:, D:].astype(f32))
        vc = v - jnp.mean(v, axis=-1, keepdims=True)
        var = jnp.mean(vc * vc, axis=-1, keepdims=True)
        vn_sc[...] = (vc * lax.rsqrt(var + EPS) * gv_ref[...]).astype(bf16)
        for g in range(A_GROUPS):
            cs = slice(g * GBLK, (g + 1) * GBLK)
            for r in range(tm // GBLK):
                rs = slice(r * GBLK, (r + 1) * GBLK)
                sv = _dot(wm_ref[g], vn_sc[rs, cs]) + bs_ref[g]
                y_ref[rs, cs] = (_gelu(p_ref[rs, cs].astype(f32)) * sv).astype(bf16)

    dspec = pl.BlockSpec((tm, D), lambda i: (i, 0))
    gspec = pl.BlockSpec((1, D), lambda i: (0, 0))
    return pl.pallas_call(
        body, grid=(T // tm,),
        in_specs=[dspec, gspec, pl.BlockSpec(wt.shape, lambda i: (0, 0), pipeline_mode=pl.Buffered(1)), gspec,
                  pl.BlockSpec((A_GROUPS, GBLK, GBLK), lambda i: (0, 0, 0)),
                  pl.BlockSpec((A_GROUPS, GBLK, GBLK), lambda i: (0, 0, 0))],
        out_specs=[pl.BlockSpec((tm, 2 * D), lambda i: (i, 0)), dspec, dspec, pl.BlockSpec((8, 128), lambda i: (0, 0))],
        out_shape=[jax.ShapeDtypeStruct((T, 2 * D), bf16), jax.ShapeDtypeStruct((T, D), bf16),
                   jax.ShapeDtypeStruct((T, D), bf16), jax.ShapeDtypeStruct((8, 128), f32)],
        scratch_shapes=[pltpu.VMEM((tm, D), bf16)],
        compiler_params=_cparams(("arbitrary",)), name=name,
    )(x, g, wt, gv, wm, bsb)


def a_mid_bwd(p, dy, gv, wm, bsb, wt, x, g, tok, dres, *, tm, name):
    T = p.shape[0]

    def body(p_ref, dy_ref, gv_ref, wm_ref, bs_ref, wt_ref, x_ref, g_ref, tok_ref, dres_ref,
             dp_ref, dgv_ref, dwm_ref, dbs_ref, dx_ref, dgn_ref, vn_sc, dvn_sc):
        @pl.when(pl.program_id(0) == 0)
        def _():
            dgn_ref[...] = jnp.zeros(dgn_ref.shape, f32)
            dgv_ref[...] = jnp.zeros(dgv_ref.shape, f32)
            dwm_ref[...] = jnp.zeros(dwm_ref.shape, f32)
            dbs_ref[...] = jnp.zeros(dbs_ref.shape, f32)

        zv = p_ref[:, D:].astype(f32)
        cdf_v = _gelu_cdf(zv)
        v = zv * cdf_v
        vc = v - jnp.mean(v, axis=-1, keepdims=True)
        rstd = lax.rsqrt(jnp.mean(vc * vc, axis=-1, keepdims=True) + EPS)
        vhat = vc * rstd
        vn_sc[...] = (vhat * gv_ref[...]).astype(bf16)
        for g in range(A_GROUPS):
            cs = slice(g * GBLK, (g + 1) * GBLK)
            dwm = jnp.zeros((GBLK, GBLK), f32)
            dbs = jnp.zeros((GBLK, 1), f32)
            for r in range(tm // GBLK):
                rs = slice(r * GBLK, (r + 1) * GBLK)
                zu = p_ref[rs, cs].astype(f32)
                cdf_u = _gelu_cdf(zu)
                vn = vn_sc[rs, cs]
                sv = _dot(wm_ref[g], vn) + bs_ref[g]
                dyb = dy_ref[rs, cs].astype(f32)
                dsv = dyb * (zu * cdf_u)
                dsvb = dsv.astype(bf16)
                dp_ref[rs, cs] = (dyb * sv * _gelu_grad(zu, cdf_u)).astype(bf16)
                dwm += _dot_nt(dsvb, vn)
                dbs += jnp.sum(dsv, axis=1, keepdims=True)
                dvn_sc[rs, cs] = _dot_tn(wm_ref[g], dsvb)
            dwm_ref[g] += dwm
            dbs_ref[g] += dbs
        dvn = dvn_sc[...]
        dgv_ref[...] += jnp.sum(dvn * vhat, axis=0, keepdims=True)
        dvhat = dvn * gv_ref[...]
        dv = rstd * (dvhat - jnp.mean(dvhat, axis=-1, keepdims=True)
                     - vhat * jnp.mean(dvhat * vhat, axis=-1, keepdims=True))
        dp_ref[:, D:] = (dv * _gelu_grad(zv, cdf_v)).astype(bf16)
        _normbwd_tail(_dot(dp_ref[...], wt_ref[...]), x_ref, g_ref, tok_ref, dres_ref, dx_ref, dgn_ref)

    tail_in, tail_out = _normbwd_specs(tm, wt)
    return pl.pallas_call(
        body, grid=(T // tm,),
        in_specs=[pl.BlockSpec((tm, 2 * D), lambda i: (i, 0)), pl.BlockSpec((tm, D), lambda i: (i, 0)),
                  pl.BlockSpec((1, D), lambda i: (0, 0)),
                  pl.BlockSpec((A_GROUPS, GBLK, GBLK), lambda i: (0, 0, 0)),
                  pl.BlockSpec((A_GROUPS, GBLK, GBLK), lambda i: (0, 0, 0))] + tail_in,
        out_specs=[pl.BlockSpec((tm, 2 * D), lambda i: (i, 0)), pl.BlockSpec((1, D), lambda i: (0, 0)),
                   pl.BlockSpec((A_GROUPS, GBLK, GBLK), lambda i: (0, 0, 0)),
                   pl.BlockSpec((A_GROUPS, GBLK, 1), lambda i: (0, 0, 0))] + tail_out,
        out_shape=[jax.ShapeDtypeStruct((T, 2 * D), bf16), jax.ShapeDtypeStruct((1, D), f32),
                   jax.ShapeDtypeStruct((A_GROUPS, GBLK, GBLK), f32), jax.ShapeDtypeStruct((A_GROUPS, GBLK, 1), f32),
                   jax.ShapeDtypeStruct((T, D), f32), jax.ShapeDtypeStruct((1, D), f32)],
        scratch_shapes=[pltpu.VMEM((tm, D), bf16), pltpu.VMEM((tm, D), f32)],
        compiler_params=_cparams(("arbitrary",)), name=name,
    )(p, dy, gv, wm, bsb, wt, x, g, tok, dres)


def b_fwd(x, g, wt, cw, *, tm, S, name):
    T = x.shape[0]
    nst = S // tm

    def body(x_ref, g_ref, wt_ref, cw_ref, p_ref, h_ref, y_ref, ext):
        first = (pl.program_id(0) % nst) == 0
        xv = x_ref[...]
        r = lax.rsqrt(jnp.mean(xv * xv, axis=-1, keepdims=True) + EPS)
        h = (xv * r * g_ref[...]).astype(bf16)
        h_ref[...] = h
        p_ref[...] = _dot_nt(h, wt_ref[...]).astype(bf16)

        @pl.when(first)
        def _():
            ext[0:HALO, :] = jnp.zeros((HALO, D), f32)

        for c0 in range(0, D, COLS):
            cs, cc, cx = slice(c0, c0 + COLS), slice(D + c0, D + c0 + COLS), slice(2 * D + c0, 2 * D + c0 + COLS)
            q = p_ref[:, cc].astype(f32) * p_ref[:, cx].astype(f32)
            ext[HALO:, cs] = q
            y = (cw_ref[0:1, cs] * ext[pl.ds(HALO - 2, tm), cs] + cw_ref[1:2, cs] * ext[pl.ds(HALO - 1, tm), cs]
                 + cw_ref[2:3, cs] * q)
            y_ref[:, cs] = (p_ref[:, cs].astype(f32) * y).astype(bf16)
            ext[0:HALO, cs] = q[tm - HALO:, :]

    dspec = pl.BlockSpec((tm, D), lambda i: (i, 0))
    return pl.pallas_call(
        body, grid=(T // tm,),
        in_specs=[dspec, pl.BlockSpec((1, D), lambda i: (0, 0)),
                  pl.BlockSpec(wt.shape, lambda i: (0, 0), pipeline_mode=pl.Buffered(1)),
                  pl.BlockSpec((3, D), lambda i: (0, 0))],
        out_specs=[pl.BlockSpec((tm, 3 * D), lambda i: (i, 0)), dspec, dspec],
        out_shape=[jax.ShapeDtypeStruct((T, 3 * D), bf16), jax.ShapeDtypeStruct((T, D), bf16),
                   jax.ShapeDtypeStruct((T, D), bf16)],
        scratch_shapes=[pltpu.VMEM((tm + HALO, D), f32)],
        compiler_params=_cparams(("arbitrary",)), name=name,
    )(x, g, wt, cw)


def b_mid_bwd(p, dy, cw, wt, x, g, tok, dres, *, tm, S, name):
    T = p.shape[0]
    nst = S // tm

    def body(p_ref, ph_ref, pn_ref, dy_ref, dyn_ref, cw_ref, wt_ref, x_ref, g_ref, tok_ref, dres_ref,
             dp_ref, dcw_ref, dx_ref, dgn_ref, ext, ext2):
        i = pl.program_id(0)
        first = (i % nst) == 0
        last = (i % nst) == nst - 1

        @pl.when(i == 0)
        def _():
            dgn_ref[...] = jnp.zeros(dgn_ref.shape, f32)
            dcw_ref[...] = jnp.zeros(dcw_ref.shape, f32)

        for c0 in range(0, D, COLS):
            cs, cc, cx = slice(c0, c0 + COLS), slice(D + c0, D + c0 + COLS), slice(2 * D + c0, 2 * D + c0 + COLS)
            gb, gc, xt = p_ref[:, cs].astype(f32), p_ref[:, cc].astype(f32), p_ref[:, cx].astype(f32)
            q = gc * xt
            ext[0:HALO, cs] = jnp.where(first, 0.0, ph_ref[:, cc].astype(f32) * ph_ref[:, cx].astype(f32))
            ext[HALO:, cs] = q
            q2 = ext[pl.ds(HALO - 2, tm), cs]
            q1 = ext[pl.ds(HALO - 1, tm), cs]
            y = cw_ref[0:1, cs] * q2 + cw_ref[1:2, cs] * q1 + cw_ref[2:3, cs] * q
            dyo = dy_ref[:, cs].astype(f32)
            dp_ref[:, cs] = (dyo * y).astype(bf16)
            dyc = dyo * gb
            ext2[0:tm, cs] = dyc
            ext2[tm:, cs] = jnp.where(last, 0.0, dyn_ref[:, cs].astype(f32) * pn_ref[:, cs].astype(f32))
            dq = cw_ref[2:3, cs] * dyc + cw_ref[1:2, cs] * ext2[pl.ds(1, tm), cs] + cw_ref[0:1, cs] * ext2[pl.ds(2, tm), cs]
            dp_ref[:, cc] = (dq * xt).astype(bf16)
            dp_ref[:, cx] = (dq * gc).astype(bf16)
            dcw_ref[0:1, cs] += jnp.sum(dyc * q2, axis=0, keepdims=True)
            dcw_ref[1:2, cs] += jnp.sum(dyc * q1, axis=0, keepdims=True)
            dcw_ref[2:3, cs] += jnp.sum(dyc * q, axis=0, keepdims=True)
        _normbwd_tail(_dot(dp_ref[...], wt_ref[...]), x_ref, g_ref, tok_ref, dres_ref, dx_ref, dgn_ref)

    tail_in, tail_out = _normbwd_specs(tm, wt)
    return pl.pallas_call(
        body, grid=(T // tm,),
        in_specs=[pl.BlockSpec((tm, 3 * D), lambda i: (i, 0)), _prev_halo_spec(tm, 3 * D), _next_halo_spec(tm, 3 * D, T),
                  pl.BlockSpec((tm, D), lambda i: (i, 0)), _next_halo_spec(tm, D, T),
                  pl.BlockSpec((3, D), lambda i: (0, 0))] + tail_in,
        out_specs=[pl.BlockSpec((tm, 3 * D), lambda i: (i, 0)), pl.BlockSpec((3, D), lambda i: (0, 0))] + tail_out,
        out_shape=[jax.ShapeDtypeStruct((T, 3 * D), bf16), jax.ShapeDtypeStruct((3, D), f32),
                   jax.ShapeDtypeStruct((T, D), f32), jax.ShapeDtypeStruct((1, D), f32)],
        scratch_shapes=[pltpu.VMEM((tm + HALO, D), f32), pltpu.VMEM((tm + HALO, D), f32)],
        compiler_params=_cparams(("arbitrary",)), name=name,
    )(p, p, p, dy, dy, cw, wt, x, g, tok, dres)


def _pool_counts(i, nst, tm, rows, row0, w):
    t = (i % nst) * tm + row0 + lax.broadcasted_iota(jnp.int32, (rows, 1), 0)
    return jnp.minimum(t + 1, w).astype(f32)


def _pool_diff(p_ref, ext, g, i, nst, tm):
    w = POOL_WINDOWS[g]
    cs = slice(g * C_GDIM, (g + 1) * C_GDIM)
    pg = p_ref[:, cs].astype(f32)
    s = pg
    for k in range(1, w):
        s = s + ext[pl.ds(HALO - k, tm), cs]
    return s / _pool_counts(i, nst, tm, tm, 0, w) - pg


def c_fwd(x, g, w, wg, scale, *, tm, S, name):
    T = x.shape[0]
    nst = S // tm

    def body(x_ref, g_ref, w_ref, wg_ref, sc_ref, p_ref, h_ref, y_ref, ext):
        i = pl.program_id(0)
        xv = x_ref[...]
        r = lax.rsqrt(jnp.mean(xv * xv, axis=-1, keepdims=True) + EPS)
        h = (xv * r * g_ref[...]).astype(bf16)
        h_ref[...] = h
        p_ref[...] = _dot(h, w_ref[...]).astype(bf16)

        @pl.when((i % nst) == 0)
        def _():
            ext[0:HALO, :] = jnp.zeros((HALO, D), f32)

        ext[HALO:, :] = p_ref[...].astype(f32)
        for g_ in range(C_GROUPS):
            cs = slice(g_ * C_GDIM, (g_ + 1) * C_GDIM)
            dg = _pool_diff(p_ref, ext, g_, i, nst, tm).astype(bf16)
            wv = wg_ref[:, g_].reshape(C_GDIM, C_GDIM)
            y_ref[:, cs] = (_dot(dg, wv) * sc_ref[:, cs]).astype(bf16)
        ext[0:HALO, :] = ext[tm:tm + HALO, :]

    dspec = pl.BlockSpec((tm, D), lambda i: (i, 0))
    gspec = pl.BlockSpec((1, D), lambda i: (0, 0))
    return pl.pallas_call(
        body, grid=(T // tm,),
        in_specs=[dspec, gspec, pl.BlockSpec(w.shape, lambda i: (0, 0), pipeline_mode=pl.Buffered(1)),
                  pl.BlockSpec((NDEV, C_GROUPS, C_GDIM // NDEV, C_GDIM), lambda i: (0, 0, 0, 0)), gspec],
        out_specs=[dspec, dspec, dspec],
        out_shape=[jax.ShapeDtypeStruct((T, D), bf16)] * 3,
        scratch_shapes=[pltpu.VMEM((tm + HALO, D), f32)],
        compiler_params=_cparams(("arbitrary",)), name=name,
    )(x, g, w, wg, scale)


def c_mid_bwd(p, dy, wg, scale, w, x, g, tok, dres, *, tm, S, name):
    T = p.shape[0]
    nst = S // tm

    def body(p_ref, ph_ref, dy_ref, dyn_ref, wg_ref, sc_ref, w_ref, x_ref, g_ref, tok_ref, dres_ref,
             dp_ref, dsc_ref, dwg_ref, dx_ref, dgn_ref, ext, ext2):
        i = pl.program_id(0)
        first = (i % nst) == 0
        last = (i % nst) == nst - 1

        @pl.when(i == 0)
        def _():
            dgn_ref[...] = jnp.zeros(dgn_ref.shape, f32)
            dsc_ref[...] = jnp.zeros(dsc_ref.shape, f32)
            dwg_ref[...] = jnp.zeros(dwg_ref.shape, f32)

        ext[0:HALO, :] = jnp.where(first, 0.0, ph_ref[...].astype(f32))
        ext[HALO:, :] = p_ref[...].astype(f32)
        for g in range(C_GROUPS):
            w = POOL_WINDOWS[g]
            cs = slice(g * C_GDIM, (g + 1) * C_GDIM)
            dg = _pool_diff(p_ref, ext, g, i, nst, tm).astype(bf16)
            wv = wg_ref[:, g].reshape(C_GDIM, C_GDIM)
            dyo = dy_ref[:, cs].astype(f32)
            dsc_ref[:, cs] += jnp.sum(dyo * _dot(dg, wv), axis=0, keepdims=True)
            dyp = (dyo * sc_ref[:, cs]).astype(bf16)
            dypn = (dyn_ref[:, cs].astype(f32) * sc_ref[:, cs]).astype(bf16)
            dwg_ref[g] += _dot_tn(dg, dyp)
            dd = _dot_nt(dyp, wv)
            ddn = _dot_nt(dypn, wv)
            ext2[0:tm, cs] = dd / _pool_counts(i, nst, tm, tm, 0, w)
            ext2[tm:, cs] = jnp.where(last, 0.0, ddn / _pool_counts(i, nst, tm, HALO, tm, w))
            s = -dd
            for k in range(w):
                s = s + ext2[pl.ds(k, tm), cs]
            dp_ref[:, cs] = s.astype(bf16)
        _normbwd_tail(_dot_nt(dp_ref[...], w_ref[...]), x_ref, g_ref, tok_ref, dres_ref, dx_ref, dgn_ref)

    tail_in, tail_out = _normbwd_specs(tm, w)
    return pl.pallas_call(
        body, grid=(T // tm,),
        in_specs=[pl.BlockSpec((tm, D), lambda i: (i, 0)), _prev_halo_spec(tm, D),
                  pl.BlockSpec((tm, D), lambda i: (i, 0)), _next_halo_spec(tm, D, T),
                  pl.BlockSpec((NDEV, C_GROUPS, C_GDIM // NDEV, C_GDIM), lambda i: (0, 0, 0, 0)),
                  pl.BlockSpec((1, D), lambda i: (0, 0))] + tail_in,
        out_specs=[pl.BlockSpec((tm, D), lambda i: (i, 0)), pl.BlockSpec((1, D), lambda i: (0, 0)),
                   pl.BlockSpec((C_GROUPS, C_GDIM, C_GDIM), lambda i: (0, 0, 0))] + tail_out,
        out_shape=[jax.ShapeDtypeStruct((T, D), bf16), jax.ShapeDtypeStruct((1, D), f32),
                   jax.ShapeDtypeStruct((C_GROUPS, C_GDIM, C_GDIM), f32),
                   jax.ShapeDtypeStruct((T, D), f32), jax.ShapeDtypeStruct((1, D), f32)],
        scratch_shapes=[pltpu.VMEM((tm + HALO, D), f32), pltpu.VMEM((tm + HALO, D), f32)],
        compiler_params=_cparams(("arbitrary",)), name=name,
    )(p, p, dy, dy, wg, scale, w, x, g, tok, dres)


def final_loss(x, g, tgt, *, tm, name):
    T = x.shape[0]

    def body(x_ref, g_ref, t_ref, loss_ref, dx_ref, dg_ref):
        @pl.when(pl.program_id(0) == 0)
        def _():
            loss_ref[...] = jnp.zeros(loss_ref.shape, f32)
            dg_ref[...] = jnp.zeros(dg_ref.shape, f32)

        xv = x_ref[...]
        r = lax.rsqrt(jnp.mean(xv * xv, axis=-1, keepdims=True) + EPS)
        xhat = xv * r
        err = xhat * g_ref[...] - t_ref[...]
        loss_ref[...] += 0.5 * jnp.sum(jnp.mean(err * err, axis=-1, keepdims=True))
        dy = err * (1.0 / D)
        dg_ref[...] += jnp.sum(dy * xhat, axis=0, keepdims=True)
        dxhat = dy * g_ref[...]
        dx_ref[...] = r * (dxhat - xhat * jnp.mean(dxhat * xhat, axis=-1, keepdims=True))

    return pl.pallas_call(
        body, grid=(T // tm,),
        in_specs=[pl.BlockSpec((tm, D), lambda i: (i, 0)), pl.BlockSpec((1, D), lambda i: (0, 0)),
                  pl.BlockSpec((tm, D), lambda i: (i, 0))],
        out_specs=[pl.BlockSpec((8, 128), lambda i: (0, 0)), pl.BlockSpec((tm, D), lambda i: (i, 0)),
                   pl.BlockSpec((1, D), lambda i: (0, 0))],
        out_shape=[jax.ShapeDtypeStruct((8, 128), f32), jax.ShapeDtypeStruct((T, D), f32),
                   jax.ShapeDtypeStruct((1, D), f32)],
        compiler_params=_cparams(("arbitrary",)), name=name,
    )(x, g, tgt)


def _slot(px, py, pc):
    return 4 * px + 2 * py + pc


def _with_own_block(s, me):
    zone = lax.empty((NDEV,) + s.shape, s.dtype)
    return lax.dynamic_update_slice(zone, s[None], (me,) + (0,) * s.ndim)


def all_gather(arrs, me, *, name):
    n = len(arrs)

    def body(*refs):
        ins, outs = refs[:n], refs[2 * n:3 * n]
        send_sems, recv_sems = refs[3 * n:]
        x, y, c = lax.axis_index("x"), lax.axis_index("y"), lax.axis_index("c")
        me, sibling = (x, y, c), (x, y, 1 - c)
        chips = [(1 - x, y), (x, 1 - y), (1 - x, 1 - y)]

        def copy(a, k, block, to, src=None):
            dst = outs[a].at[_slot(*block)]
            return pltpu.make_async_remote_copy(
                src_ref=dst if src is None else src, dst_ref=dst,
                send_sem=send_sems.at[a, k], recv_sem=recv_sems.at[a, k], device_id=to, device_id_type=MESH)

        first = []
        for a in range(n):
            first.append(copy(a, 0, me, sibling, src=ins[a]))
            first += [copy(a, 1 + j, me, (*chip, c), src=ins[a]) for j, chip in enumerate(chips)]
        for cp in first:
            cp.start()
        passed = []
        for j, chip in enumerate(chips):
            for a in range(n):
                copy(a, 1 + j, (*chip, c), me).wait_recv()
                fwd = copy(a, 4 + j, (*chip, c), sibling)
                fwd.start()
                passed.append(fwd)
        for a in range(n):
            copy(a, 0, sibling, me).wait_recv()
        for j, chip in enumerate(chips):
            for a in range(n):
                copy(a, 4 + j, (*chip, 1 - c), me).wait_recv()
        for cp in first + passed:
            cp.wait_send()

    any_spec = pl.BlockSpec(memory_space=pl.ANY)
    return pl.pallas_call(
        body,
        in_specs=[any_spec] * (2 * n), out_specs=[any_spec] * n,
        out_shape=[jax.ShapeDtypeStruct((NDEV,) + a.shape, a.dtype) for a in arrs],
        input_output_aliases={n + i: i for i in range(n)},
        scratch_shapes=[pltpu.SemaphoreType.DMA((n, 7)), pltpu.SemaphoreType.DMA((n, 7))],
        compiler_params=pltpu.CompilerParams(has_side_effects=True), name=name,
    )(*arrs, *[_with_own_block(a, me) for a in arrs])


HBM_SPEC = pl.BlockSpec(memory_space=pltpu.HBM)
SEM_SPEC = pl.BlockSpec(memory_space=pltpu.SEMAPHORE)
ANY_SPEC = pl.BlockSpec(memory_space=pl.ANY)
TOKEN_SHAPE = jax.ShapeDtypeStruct((8, 128), f32)
DATAFLOW_EFFECT = pltpu.SideEffectType.DATAFLOW_SIDE_EFFECTING


def _in_hbm(a):
    return pltpu.with_memory_space_constraint(a, pltpu.HBM)


def _hbm_like(a):
    return pltpu.HBM(a.shape, a.dtype)


def _mesh_pos():
    return lax.axis_index("x"), lax.axis_index("y"), lax.axis_index("c")


def _gather_targets(x, y, c):
    return [(x, y, 1 - c), (1 - x, y, c), (x, 1 - y, c), (1 - x, 1 - y, c)]


def gather_start(shards, me, after, *, name):
    n = len(shards)
    extra = [] if after is None else [after]

    def body(*refs):
        srcs, lands = refs[:n], refs[n:2 * n]
        send_sems, recv_sems = refs[2 * n + len(extra)], refs[2 * n + len(extra) + 1]
        token = refs[-1]
        x, y, c = _mesh_pos()
        me = _slot(x, y, c)
        for a in range(n):
            for k, to in enumerate(_gather_targets(x, y, c)):
                pltpu.make_async_remote_copy(
                    src_ref=srcs[a], dst_ref=lands[a].at[me], send_sem=send_sems.at[4 * a + k], recv_sem=recv_sems.at[4 * a + k],
                    device_id=to, device_id_type=MESH).start()
        token[...] = jnp.zeros(token.shape, f32)

    lands = [_with_own_block(s, me) for s in shards]
    sems = pltpu.SemaphoreType.DMA((4 * n,))
    out = pl.pallas_call(
        body, name=name,
        in_specs=[HBM_SPEC] * (2 * n) + [ANY_SPEC] * len(extra),
        out_specs=[SEM_SPEC, SEM_SPEC] + [HBM_SPEC] * (2 * n) + [pl.BlockSpec(memory_space=pltpu.VMEM)],
        out_shape=[sems, sems] + [_hbm_like(s) for s in shards] + [_hbm_like(l) for l in lands] + [TOKEN_SHAPE],
        input_output_aliases={i: 2 + i for i in range(2 * n)},
        compiler_params=pltpu.CompilerParams(has_side_effects=DATAFLOW_EFFECT),
    )(*[_in_hbm(s) for s in shards], *[_in_hbm(l) for l in lands], *extra)
    return out[0], out[1], out[2:2 + n], out[2 + n:2 + 2 * n], out[-1]


def gather_wait(send_sems, recv_sems, shards, lands, after, *, name):
    n = len(shards)
    after = list(after) if isinstance(after, (list, tuple)) else [after]

    def body(*refs):
        srcs, lands_in = refs[:n], refs[n:2 * n]
        send_sems, recv_sems = refs[2 * n], refs[2 * n + 1]
        x, y, c = _mesh_pos()
        for a in range(n):
            for k, frm in enumerate(_gather_targets(x, y, c)):
                cp = pltpu.make_async_remote_copy(
                    src_ref=srcs[a], dst_ref=lands_in[a].at[_slot(*frm)], send_sem=send_sems.at[4 * a + k],
                    recv_sem=recv_sems.at[4 * a + k], device_id=frm, device_id_type=MESH)
                cp.wait_send()
                cp.wait_recv()

    out = pl.pallas_call(
        body, name=name,
        in_specs=[HBM_SPEC] * (2 * n) + [SEM_SPEC, SEM_SPEC] + [ANY_SPEC] * len(after),
        out_specs=[HBM_SPEC] * (2 * n),
        out_shape=[_hbm_like(s) for s in shards] + [_hbm_like(l) for l in lands],
        input_output_aliases={i: i for i in range(2 * n)},
        compiler_params=pltpu.CompilerParams(has_side_effects=DATAFLOW_EFFECT),
    )(*shards, *lands, send_sems, recv_sems, *after)
    return out[:n], out[n:]


def gather_finish(lands, *, name):
    n = len(lands)

    def body(*refs):
        lands_in, lands_out = refs[:n], refs[n:2 * n]
        send_sems, recv_sems = refs[2 * n:]
        x, y, c = _mesh_pos()
        sibling = (x, y, 1 - c)
        chips = [(1 - x, y), (x, 1 - y), (1 - x, 1 - y)]
        sent = []
        for a in range(n):
            for j, chip in enumerate(chips):
                s = _slot(*chip, c)
                cp = pltpu.make_async_remote_copy(
                    src_ref=lands_in[a].at[s], dst_ref=lands_out[a].at[s], send_sem=send_sems.at[a, j],
                    recv_sem=recv_sems.at[a, j], device_id=sibling, device_id_type=MESH)
                cp.start()
                sent.append(cp)
        for a in range(n):
            for j, chip in enumerate(chips):
                s = _slot(*chip, 1 - c)
                pltpu.make_async_remote_copy(
                    src_ref=lands_in[a].at[s], dst_ref=lands_out[a].at[s], send_sem=send_sems.at[a, j],
                    recv_sem=recv_sems.at[a, j], device_id=sibling, device_id_type=MESH).wait_recv()
        for cp in sent:
            cp.wait_send()

    return pl.pallas_call(
        body, name=name,
        in_specs=[ANY_SPEC] * n, out_specs=[ANY_SPEC] * n,
        out_shape=[jax.ShapeDtypeStruct(l.shape, l.dtype) for l in lands],
        input_output_aliases={i: i for i in range(n)},
        scratch_shapes=[pltpu.SemaphoreType.DMA((n, 3)), pltpu.SemaphoreType.DMA((n, 3))],
        compiler_params=pltpu.CompilerParams(has_side_effects=True),
    )(*lands)


def forward_start(lands, after, *, name):
    n = len(lands)

    def body(*refs):
        lands_in = refs[:n]
        send_sems, recv_sems = refs[n + 1], refs[n + 2]
        lands_out = refs[n + 3:2 * n + 3]
        token = refs[-1]
        x, y, c = _mesh_pos()
        for a in range(n):
            for j, chip in enumerate([(1 - x, y), (x, 1 - y), (1 - x, 1 - y)]):
                s = _slot(*chip, c)
                pltpu.make_async_remote_copy(
                    src_ref=lands_in[a].at[s], dst_ref=lands_out[a].at[s], send_sem=send_sems.at[3 * a + j],
                    recv_sem=recv_sems.at[3 * a + j], device_id=(x, y, 1 - c), device_id_type=MESH).start()
        token[...] = jnp.zeros(token.shape, f32)

    sems = pltpu.SemaphoreType.DMA((3 * n,))
    out = pl.pallas_call(
        body, name=name,
        in_specs=[HBM_SPEC] * n + [ANY_SPEC],
        out_specs=[SEM_SPEC, SEM_SPEC] + [HBM_SPEC] * n + [pl.BlockSpec(memory_space=pltpu.VMEM)],
        out_shape=[sems, sems] + [_hbm_like(l) for l in lands] + [TOKEN_SHAPE],
        input_output_aliases={i: 2 + i for i in range(n)},
        compiler_params=pltpu.CompilerParams(has_side_effects=DATAFLOW_EFFECT),
    )(*lands, after)
    return out[0], out[1], list(out[2:2 + n]), out[-1]


def forward_wait(send_sems, recv_sems, lands, after, *, name):
    n = len(lands)

    def body(*refs):
        lands_in = refs[:n]
        send_sems, recv_sems = refs[n], refs[n + 1]
        x, y, c = _mesh_pos()
        sibling = (x, y, 1 - c)
        for a in range(n):
            for j, chip in enumerate([(1 - x, y), (x, 1 - y), (1 - x, 1 - y)]):
                k = 3 * a + j
                cp = pltpu.make_async_remote_copy(
                    src_ref=lands_in[a].at[_slot(*chip, c)], dst_ref=lands_in[a].at[_slot(*chip, 1 - c)],
                    send_sem=send_sems.at[k], recv_sem=recv_sems.at[k], device_id=sibling,
                    device_id_type=MESH)
                cp.wait_send()
                cp.wait_recv()

    out = pl.pallas_call(
        body, name=name,
        in_specs=[HBM_SPEC] * n + [SEM_SPEC, SEM_SPEC] + [ANY_SPEC] * len(after),
        out_specs=[HBM_SPEC] * n,
        out_shape=[_hbm_like(l) for l in lands],
        input_output_aliases={i: i for i in range(n)},
        compiler_params=pltpu.CompilerParams(has_side_effects=DATAFLOW_EFFECT),
    )(*lands, send_sems, recv_sems, *after)
    return list(out)


def _peer(x, y, c, r):
    return (x ^ ((r >> 2) & 1), y ^ ((r >> 1) & 1), c ^ (r & 1))


def exchange_start(items, lands, after, *, name):
    n, m = len(items), len(lands)
    extra = [] if after is None else [after]

    def body(*refs):
        gs, zones = refs[:n], refs[n:n + m]
        send_sems, recv_sems = refs[n + m + len(extra)], refs[n + m + len(extra) + 1]
        token = refs[-1]
        x, y, c = _mesh_pos()
        me = _slot(x, y, c)
        for r in (1, 2, 4, 3, 5, 6, 7):
            to = _peer(x, y, c, r)
            for a, (_, zi, l) in enumerate(items):
                pltpu.make_async_remote_copy(
                    src_ref=gs[a].at[_slot(*to)], dst_ref=zones[zi].at[me, l], send_sem=send_sems.at[7 * a + r - 1],
                    recv_sem=recv_sems.at[7 * a + r - 1], device_id=to, device_id_type=MESH).start()
        token[...] = jnp.zeros(token.shape, f32)

    gs = [g for g, _, _ in items]
    sems = pltpu.SemaphoreType.DMA((7 * n,))
    out = pl.pallas_call(
        body, name=name,
        in_specs=[HBM_SPEC] * (n + m) + [ANY_SPEC] * len(extra),
        out_specs=[SEM_SPEC, SEM_SPEC] + [HBM_SPEC] * (n + m) + [pl.BlockSpec(memory_space=pltpu.VMEM)],
        out_shape=[sems, sems] + [_hbm_like(g) for g in gs] + [_hbm_like(z) for z in lands] + [TOKEN_SHAPE],
        input_output_aliases={i: 2 + i for i in range(n + m)},
        compiler_params=pltpu.CompilerParams(has_side_effects=DATAFLOW_EFFECT),
    )(*[_in_hbm(g) for g in gs], *[_in_hbm(z) for z in lands], *extra)
    return out[0], out[1], out[2:2 + n], out[2 + n:2 + n + m], out[-1]


def exchange_wait(parts, lands, after, *, name):
    m = len(lands)
    flat_gs = [g for _, _, _, gs in parts for g in gs]
    ng = len(flat_gs)
    after = list(after) if isinstance(after, (list, tuple)) else [after]

    def body(*refs):
        gs, zones = refs[:ng], refs[ng:ng + m]
        sem_refs = refs[ng + m:ng + m + 2 * len(parts)]
        x, y, c = _mesh_pos()
        base = 0
        for pi, (items, _, _, _) in enumerate(parts):
            send_sems, recv_sems = sem_refs[2 * pi], sem_refs[2 * pi + 1]
            for r in range(1, NDEV):
                frm = _peer(x, y, c, r)
                for a, (_, zi, l) in enumerate(items):
                    cp = pltpu.make_async_remote_copy(
                        src_ref=gs[base + a].at[_slot(*frm)], dst_ref=zones[zi].at[_slot(*frm), l],
                        send_sem=send_sems.at[7 * a + r - 1], recv_sem=recv_sems.at[7 * a + r - 1],
                        device_id=frm, device_id_type=MESH)
                    cp.wait_send()
                    cp.wait_recv()
            base += len(items)

    sem_args = [s for _, ss, rs, _ in parts for s in (ss, rs)]
    out = pl.pallas_call(
        body, name=name,
        in_specs=[HBM_SPEC] * (ng + m) + [SEM_SPEC] * len(sem_args) + [ANY_SPEC] * len(after),
        out_specs=[HBM_SPEC] * (ng + m),
        out_shape=[_hbm_like(g) for g in flat_gs] + [_hbm_like(z) for z in lands],
        input_output_aliases={i: i for i in range(ng + m)},
        compiler_params=pltpu.CompilerParams(has_side_effects=DATAFLOW_EFFECT),
    )(*flat_gs, *lands, *sem_args, *after)
    return out[ng:]


def sum8(g, *, name):
    _, R, C = g.shape
    tr = R

    def body(g_ref, o_ref):
        s = g_ref[0]
        for k in range(1, NDEV):
            s = s + g_ref[k]
        o_ref[...] = s

    return pl.pallas_call(
        body, grid=(R // tr,),
        in_specs=[pl.BlockSpec((NDEV, tr, C), lambda i: (0, i, 0))],
        out_specs=pl.BlockSpec((tr, C), lambda i: (i, 0)),
        out_shape=jax.ShapeDtypeStruct((R, C), f32),
        compiler_params=_cparams(("parallel",)), name=name,
    )(g)


def _adam_math(w, g, m, v):
    m = ADAM_B1 * m + (1.0 - ADAM_B1) * g
    v = ADAM_B2 * v + (1.0 - ADAM_B2) * (g * g)
    m_hat = m / (1.0 - ADAM_B1 ** ADAM_STEP)
    v_hat = v / (1.0 - ADAM_B2 ** ADAM_STEP)
    delta = -ADAM_LR * (m_hat / (jnp.sqrt(v_hat) + ADAM_EPS) + ADAM_WD * w)
    return delta, m, v


def adamw_slices(w, m, v, recv, *, name):
    return adamw_families([w], [m], [v], [recv], tc=w.shape[2], name=name)


def adamw_families(ws, ms, vs, recvs, *, tc, l0=0, nl=None, prev=None, name):
    nf = len(ws)
    L, R, C = ws[0].shape
    nl = L - l0 if nl is None else nl
    prevs = [] if prev is None else list(prev)

    def body(*refs):
        outs = refs[4 * nf + len(prevs):]
        for q in range(nf):
            w_ref, m_ref, v_ref, r_ref = refs[q], refs[nf + q], refs[2 * nf + q], refs[3 * nf + q]
            g = r_ref[0].astype(f32)
            for k in range(1, NDEV):
                g = g + r_ref[k].astype(f32)
            delta, nm, nv = _adam_math(w_ref[...], g, m_ref[...], v_ref[...])
            outs[4 * q][...] = g
            outs[4 * q + 1][...] = delta
            outs[4 * q + 2][...] = nm
            outs[4 * q + 3][...] = nv

    wspec = pl.BlockSpec((None, R, tc), lambda l, j: (l0 + l, 0, j))
    rspec = pl.BlockSpec((NDEV, None, R, tc), lambda l, j: (0, l0 + l, 0, j))
    return pl.pallas_call(
        body, grid=(nl, C // tc),
        in_specs=[wspec] * (3 * nf) + [rspec] * nf + [pl.BlockSpec(memory_space=pl.ANY)] * len(prevs),
        out_specs=[wspec] * (4 * nf), out_shape=[jax.ShapeDtypeStruct((L, R, C), f32)] * (4 * nf),
        input_output_aliases={4 * nf + q: q for q in range(len(prevs))},
        compiler_params=_cparams(("parallel", "parallel")), name=name,
    )(*ws, *ms, *vs, *recvs, *prevs)


def adamw_slices_t(w, m, v, recv, *, name):
    L, R, C = w.shape

    def body(w_ref, m_ref, v_ref, r_ref, g_ref, d_ref, nm_ref, nv_ref):
        gt = r_ref[0].astype(f32)
        for k in range(1, NDEV):
            gt = gt + r_ref[k].astype(f32)
        g = gt.T
        delta, nm, nv = _adam_math(w_ref[...], g, m_ref[...], v_ref[...])
        g_ref[...] = g
        d_ref[...] = delta
        nm_ref[...] = nm
        nv_ref[...] = nv

    wspec = pl.BlockSpec((None, R, C), lambda l: (l, 0, 0))
    return pl.pallas_call(
        body, grid=(L,),
        in_specs=[wspec, wspec, wspec, pl.BlockSpec((NDEV, None, C, R), lambda l: (0, l, 0, 0))],
        out_specs=[wspec] * 4, out_shape=[jax.ShapeDtypeStruct((L, R, C), f32)] * 4,
        compiler_params=_cparams(("parallel",)), name=name,
    )(w, m, v, recv)


def adamw_plain(w, m, v, g, *, name):
    R, C = w.shape

    def body(w_ref, m_ref, v_ref, g_ref, d_ref, nm_ref, nv_ref):
        delta, nm, nv = _adam_math(w_ref[...], g_ref[...], m_ref[...], v_ref[...])
        d_ref[...] = delta
        nm_ref[...] = nm
        nv_ref[...] = nv

    return pl.pallas_call(
        body, out_shape=[jax.ShapeDtypeStruct((R, C), f32)] * 3, name=name,
    )(w, m, v, g)


def _chunk_causal_mask():
    pos = jnp.arange(GBLK)
    return (pos[None, :] // CHUNK) <= (pos[:, None] // CHUNK)


def kernel(x, norm_mix_g, norm_ffn_g, final_norm_g, a_w_in, a_v_norm_g, a_w_s, a_b_s, a_w_out, b_w_in, b_conv_w, b_w_out, c_w_in, c_w_grp, c_scale, c_w_out, f_w_gate, f_w_up, f_w_down, loss_target, m_norm_mix_g, m_norm_ffn_g, m_final_norm_g, m_a_w_in, m_a_v_norm_g, m_a_w_s, m_a_b_s, m_a_w_out, m_b_w_in, m_b_conv_w, m_b_w_out, m_c_w_in, m_c_w_grp, m_c_scale, m_c_w_out, m_f_w_gate, m_f_w_up, m_f_w_down, v_norm_mix_g, v_norm_ffn_g, v_final_norm_g, v_a_w_in, v_a_v_norm_g, v_a_w_s, v_a_b_s, v_a_w_out, v_b_w_in, v_b_conv_w, v_b_w_out, v_c_w_in, v_c_w_grp, v_c_scale, v_c_w_out, v_f_w_gate, v_f_w_up, v_f_w_down):
    B, S, _ = x.shape
    T = B * S
    tm = min(512, S)
    tms = min(512, T)
    tma = min(256, S)
    tmf = min(256, T)
    me = _slot(lax.axis_index("x"), lax.axis_index("y"), lax.axis_index("c"))

    def tb(w):
        return jnp.transpose(w, (0, 2, 1)).astype(bf16)

    small = jnp.concatenate([a_v_norm_g, b_conv_w[0], c_scale, jnp.zeros((2, GBLK), f32)], axis=0)
    started = [gather_start([tb(a_w_in[0:1])[0], a_w_out[0].astype(bf16), small], me, None, name="gather_start_0")]
    tok = started[0][4]

    def ffn_shards(l, z):
        return [tb(f_w_gate[l:l + 1] + z)[0], tb(f_w_up[l:l + 1] + z)[0], (f_w_down[l] + z).astype(bf16)]

    started.append(gather_start(ffn_shards(0, tok[0, 0]), me, tok, name="gather_start_1"))
    tok = started[-1][4]

    def start_later_stages(z, after):
        stage_shards = [
            [tb(b_w_in + z)[0], (b_w_out[0] + z).astype(bf16)] + ffn_shards(1, z),
            [(c_w_in[0] + z).astype(bf16), (c_w_grp[0] + z).astype(bf16), (c_w_out[0] + z).astype(bf16)] + ffn_shards(2, z),
            [tb(a_w_in[1:2] + z)[0], (a_w_out[1] + z).astype(bf16)] + ffn_shards(3, z),
        ]
        for q, shards in enumerate(stage_shards):
            started.append(gather_start(shards, me, after, name=f"gather_start_{q + 2}"))
            after = started[-1][4]
        return after

    def as_matrices(lands):
        return [t.reshape(-1, t.shape[-1]) if t.ndim == 3 else t for t in lands]

    def stage_weights(s, after):
        send_sems, recv_sems, shards, lands, _ = started[s]
        _, lands = gather_wait(send_sems, recv_sems, shards, lands, after, name=f"gather_wait_{s}")
        return as_matrices(gather_finish(lands, name=f"gather_finish_{s}"))

    def mixer_weights(s, after, n_mix):
        send_sems, recv_sems, shards, lands, _ = started[s]
        _, lands = gather_wait(send_sems, recv_sems, shards, lands, after, name=f"gather_wait_{s}")
        mix_lands = gather_finish(lands[:n_mix], name=f"gather_finish_{s}")
        fsend, frecv, ffn_lands, ftok = forward_start(lands[n_mix:], mix_lands[0], name=f"forward_start_{s}")
        return as_matrices(mix_lands), (fsend, frecv, ffn_lands), ftok

    def ffn_weights(s, pending, after):
        fsend, frecv, lands = pending
        return as_matrices(forward_wait(fsend, frecv, lands, after, name=f"forward_wait_{s}"))

    wm_masked = jnp.where(_chunk_causal_mask()[None, None], a_w_s, 0.0).astype(bf16)
    bsb = jnp.broadcast_to(a_b_s[:, :, :, None], a_b_s.shape + (GBLK,))

    xc = x.reshape(T, D)
    saved, weights = [], []
    for i in range(DEPTH):
        kind, j = i % 3, i // 3
        gmix = norm_mix_g[i][None]
        if i == 0:
            w_in, w_out, small_g = stage_weights(0, tok)
            small_full = jnp.transpose(small_g.reshape(NDEV, 8, GBLK), (1, 0, 2)).reshape(8, D)
            gv_full, cw_full, cs_full = small_full[0:2], small_full[2:5], small_full[5:6]
            w = [w_in, w_out]
        else:
            w, pending, ftok = mixer_weights(i + 1, xc, 3 if kind == 2 else 2)
            gmix = gmix + ftok[0:1, 0:1]
        if kind == 0:
            w_in = w[0]
            p, h, y, tok_a = a_fwd(xc, gmix, w_in, gv_full[j][None], wm_masked[j], bsb[j], tm=tma, name=f"a_fwd_{i}")
            if i == 0:
                tok_late = start_later_stages(tok_a[0, 0], tok_a)
        elif kind == 1:
            w_in, w_out = w[:2]
            p, h, y = b_fwd(xc, gmix, w_in, cw_full, tm=tm, S=S, name=f"b_fwd_{i}")
        else:
            w_in, w_grp, w_out = w[:3]
            p, h, y = c_fwd(xc, gmix, w_in, w_grp, cs_full, tm=tm, S=S, name=f"c_fwd_{i}")
        if i == 0:
            w = w + stage_weights(1, [y, tok_late])
        else:
            w = w + ffn_weights(i + 1, pending, [y])
        weights.append(w)
        w_out, w_fg, w_fu, w_fd = w[-4:]
        x1, x2, pg, pu, a, h2 = ffn_fwd(y, w_out, xc, norm_ffn_g[i][None], w_fg, w_fu, w_fd, tm=tmf, name=f"f_fwd_{i}")
        saved.append((xc, p, h, y, x1, pg, pu, h2, a))
        xc = x2

    loss_blk, dx, d_final = final_loss(xc, final_norm_g[None], loss_target.reshape(T, D), tm=tms, name="final_loss")
    loss = lax.psum(loss_blk[0, 0], AXES)

    zone_shapes = [(2, 256, D), (2, GBLK, D), (1, 384, D), (1, GBLK, D), (1, GBLK, D), (1, GBLK, C_GDIM), (1, GBLK, D),
                   (DEPTH, F_LOC, D), (DEPTH, F_LOC, D), (DEPTH, F_LOC, D)]
    Z_AIN, Z_AOUT, Z_BIN, Z_BOUT, Z_CIN, Z_CGRP, Z_COUT, Z_FG, Z_FU, Z_FD = range(10)
    lands = [lax.empty((NDEV,) + s, bf16) for s in zone_shapes]
    parts = []

    def exchange(items, after, name):
        nonlocal lands
        for g, zi, l in items:
            own = lax.dynamic_index_in_dim(g, me, axis=0, keepdims=True)[None]
            lands[zi] = lax.dynamic_update_slice(lands[zi], own, (me, l) + (0,) * (g.ndim - 1))
        idx = sorted({zi for _, zi, _ in items})
        local = [(g, idx.index(zi), l) for g, zi, l in items]
        send_sems, recv_sems, gs, zs, token = exchange_start(local, [lands[zi] for zi in idx], after, name=name)
        for q, zi in enumerate(idx):
            lands[zi] = zs[q]
        parts.append(([(None, zi, l) for _, zi, l in items], send_sems, recv_sems, gs))
        return token

    d_mix, d_ffn = [None] * DEPTH, [None] * DEPTH
    d_gv, d_wm, d_bs = [None] * 2, [None] * 2, [None] * 2
    tok = None
    for i in reversed(range(DEPTH)):
        kind, j = i % 3, i // 3
        xin, p, h, y, x1, pg, pu, h2, a = saved[i]
        w = weights[i]
        w_in, w_out = w[0], w[-4]
        w_fg, w_fu, w_fd = w[-3:]
        dpg, dpu, dxb, dx1, d_ffn[i], dy, dx1b = ffn_bwd(dx, w_fd, pg, pu, w_fg, w_fu, x1, norm_ffn_g[i][None], tok, w_out,
                                                         tm=tmf, name=f"f_bwd_{i}")
        gw_g, gw_u, gw_d = ffn_dw(dpg, dpu, a, h2, dxb, name=f"f_dw_{i}")
        tok = exchange([(gw_g, Z_FG, i), (gw_u, Z_FU, i), (gw_d, Z_FD, i)], tok, f"exchange_start_ffn_{i}")
        gmix = norm_mix_g[i][None]
        if kind == 0:
            dp, d_gv[j], d_wm[j], d_bs[j], dx, d_mix[i] = a_mid_bwd(
                p, dy, gv_full[j][None], wm_masked[j], bsb[j], w_in, xin, gmix, tok, dx1, tm=tma, name=f"a_bwd_{i}")
            gw_in, gw_out = mm_tn2(dp, h, y, dx1b, name=f"a_dw_{i}")
            items = [(gw_in, Z_AIN, j), (gw_out, Z_AOUT, j)]
        elif kind == 1:
            dp, d_cw, dx, d_mix[i] = b_mid_bwd(p, dy, cw_full, w_in, xin, gmix, tok, dx1, tm=tm, S=S, name=f"b_bwd_{i}")
            gw_in, gw_out = mm_tn2(dp, h, y, dx1b, name=f"b_dw_{i}")
            items = [(gw_in, Z_BIN, 0), (gw_out, Z_BOUT, 0)]
        else:
            dp, d_cs, d_wgrp, dx, d_mix[i] = c_mid_bwd(p, dy, w[1], cs_full, w_in, xin, gmix, tok, dx1, tm=tm, S=S,
                                                       name=f"c_bwd_{i}")
            gw_cgrp = jnp.transpose(d_wgrp.reshape(C_GROUPS, NDEV, C_GDIM // NDEV, C_GDIM), (1, 0, 2, 3)).astype(bf16)
            gw_in, gw_out = mm_tn2(h, dp, y, dx1b, name=f"c_dw_{i}")
            items = [(gw_in, Z_CIN, 0), (gw_cgrp.reshape(NDEV, GBLK, C_GDIM), Z_CGRP, 0), (gw_out, Z_COUT, 0)]
        tok = exchange(items, tok, f"exchange_start_mix_{i}")
    grad_x = dx.reshape(B, S, D)

    small_parts = [
        jnp.concatenate(d_mix, axis=0).reshape(-1, GBLK),
        jnp.concatenate(d_ffn, axis=0).reshape(-1, GBLK),
        d_final.reshape(-1, GBLK),
        jnp.concatenate(d_gv, axis=0).reshape(-1, GBLK),
        jnp.where(_chunk_causal_mask()[None, None], jnp.stack(d_wm), 0.0).reshape(-1, GBLK),
        jnp.stack(d_bs).reshape(-1, GBLK),
        d_cw.reshape(-1, GBLK),
        d_cs.reshape(-1, GBLK),
    ]
    small_rows = [q.shape[0] for q in small_parts]
    sg_send, sg_recv, sg_shards, sg_lands, _ = gather_start(
        [jnp.concatenate(small_parts, axis=0)], me, tok, name="gather_small_start")

    def plain(w, m, v, g, name):
        shp = w.shape
        w2, m2, v2, g2 = (t.reshape(-1, shp[-1]) for t in (w, m, v, g))
        return tuple(t.reshape(shp) for t in adamw_plain(w2, m2, v2, g2, name=name))

    def sliced(w, m, v, recv, name):
        shp = w.shape
        w3, m3, v3 = (t.reshape((shp[0], -1, shp[-1])) for t in (w, m, v))
        return tuple(t.reshape(shp) for t in adamw_slices(w3, m3, v3, recv, name=name))

    lands = list(exchange_wait(parts[:6], lands, dx, name="exchange_wait"))
    _, _, r_bin, r_bout, r_cin, r_cgrp, r_cout, r_fg, r_fu, r_fd = lands
    res = {}
    res["b_w_in"] = adamw_slices_t(b_w_in, m_b_w_in, v_b_w_in, r_bin, name="adam_b_w_in")
    res["b_w_out"] = sliced(b_w_out, m_b_w_out, v_b_w_out, r_bout, "adam_b_w_out")
    res["c_w_in"] = sliced(c_w_in, m_c_w_in, v_c_w_in, r_cin, "adam_c_w_in")
    res["c_w_grp"] = sliced(c_w_grp, m_c_w_grp, v_c_w_grp, r_cgrp, "adam_c_w_grp")
    res["c_w_out"] = sliced(c_w_out, m_c_w_out, v_c_w_out, r_cout, "adam_c_w_out")
    def tr(t):
        return jnp.transpose(t, (0, 2, 1))

    f_ws = [tr(f_w_gate), tr(f_w_up), f_w_down]
    f_ms = [tr(m_f_w_gate), tr(m_f_w_up), m_f_w_down]
    f_vs = [tr(v_f_w_gate), tr(v_f_w_up), v_f_w_down]
    ff = adamw_families(f_ws, f_ms, f_vs, [r_fg, r_fu, r_fd], tc=256, l0=1, name="adam_ffn_123")
    big = [res[n][1] for n in ("b_w_in", "b_w_out", "c_w_in", "c_w_grp", "c_w_out")] + [ff[1], ff[5], ff[9]]

    ffn0_items, ffn0_send, ffn0_recv, ffn0_gs = parts[6]
    ffn0_zones = [Z_FG, Z_FU, Z_FD]
    ffn0_part = ([(None, ffn0_zones.index(zi), l) for _, zi, l in ffn0_items], ffn0_send, ffn0_recv, ffn0_gs)
    r_fg, r_fu, r_fd = exchange_wait([ffn0_part], [lands[zi] for zi in ffn0_zones], big, name="exchange_wait_ffn0")
    ff = adamw_families(f_ws, f_ms, f_vs, [r_fg, r_fu, r_fd], tc=256, l0=0, nl=1, prev=ff, name="adam_ffn_0")
    res["f_w_gate"], res["f_w_up"], res["f_w_down"] = tuple(tr(t) for t in ff[0:4]), tuple(tr(t) for t in ff[4:8]), tuple(ff[8:12])

    last_items, last_send, last_recv, last_gs = parts[7]
    last_zones = [Z_AIN, Z_AOUT]
    last_part = ([(None, last_zones.index(zi), l) for _, zi, l in last_items], last_send, last_recv, last_gs)
    r_ain, r_aout = exchange_wait([last_part], [lands[zi] for zi in last_zones], [ff[1], ff[5], ff[9]], name="exchange_wait_last")
    res["a_w_in"] = adamw_slices_t(a_w_in, m_a_w_in, v_a_w_in, r_ain, name="adam_a_w_in")
    res["a_w_out"] = sliced(a_w_out, m_a_w_out, v_a_w_out, r_aout, "adam_a_w_out")

    _, sg_lands = gather_wait(sg_send, sg_recv, sg_shards, sg_lands, [res["a_w_in"][1], res["a_w_out"][1]], name="gather_small_wait")
    (gs_all,) = gather_finish(sg_lands, name="gather_small_finish")
    gs = sum8(gs_all, name="sum_small_grads")
    offs = [0]
    for r in small_rows:
        offs.append(offs[-1] + r)
    sp = [gs[offs[q]:offs[q + 1]] for q in range(len(small_parts))]
    grad_norm_mix_g = sp[0].reshape(DEPTH, D)
    grad_norm_ffn_g = sp[1].reshape(DEPTH, D)
    grad_final_norm_g = sp[2].reshape(D)
    grad_a_w_s = sp[4].reshape(a_w_s.shape)
    grad_a_b_s = sp[5].reshape(a_b_s.shape)

    def my_cols(full):
        return lax.dynamic_slice_in_dim(full, me * GBLK, GBLK, axis=1)

    grad_a_v_norm_g = my_cols(sp[3].reshape(2, D))
    grad_b_conv_w = my_cols(sp[6].reshape(3, D))[None]
    grad_c_scale = my_cols(sp[7].reshape(1, D))
    res["norm_mix_g"] = (grad_norm_mix_g,) + plain(norm_mix_g, m_norm_mix_g, v_norm_mix_g, grad_norm_mix_g, "adam_norm_mix")
    res["norm_ffn_g"] = (grad_norm_ffn_g,) + plain(norm_ffn_g, m_norm_ffn_g, v_norm_ffn_g, grad_norm_ffn_g, "adam_norm_ffn")
    res["final_norm_g"] = (grad_final_norm_g,) + tuple(
        t.reshape(D) for t in plain(final_norm_g[None], m_final_norm_g[None], v_final_norm_g[None], grad_final_norm_g[None], "adam_final"))
    res["a_v_norm_g"] = (grad_a_v_norm_g,) + plain(a_v_norm_g, m_a_v_norm_g, v_a_v_norm_g, grad_a_v_norm_g, "adam_a_v_norm")
    res["a_w_s"] = (grad_a_w_s,) + plain(a_w_s, m_a_w_s, v_a_w_s, grad_a_w_s, "adam_a_w_s")
    res["a_b_s"] = (grad_a_b_s,) + plain(a_b_s, m_a_b_s, v_a_b_s, grad_a_b_s, "adam_a_b_s")
    res["b_conv_w"] = (grad_b_conv_w,) + plain(b_conv_w, m_b_conv_w, v_b_conv_w, grad_b_conv_w, "adam_b_conv")
    res["c_scale"] = (grad_c_scale,) + plain(c_scale, m_c_scale, v_c_scale, grad_c_scale, "adam_c_scale")

    order = ["norm_mix_g", "norm_ffn_g", "final_norm_g", "a_w_in", "a_v_norm_g", "a_w_s", "a_b_s", "a_w_out", "b_w_in",
             "b_conv_w", "b_w_out", "c_w_in", "c_w_grp", "c_scale", "c_w_out", "f_w_gate", "f_w_up", "f_w_down"]
    return (loss, grad_x, *[res[n][0] for n in order], *[res[n][1] for n in order],
            *[res[n][2] for n in order], *[res[n][3] for n in order])
```

```python
import jax
import jax.numpy as jnp
from jax import lax
from jax.experimental import pallas as pl
from jax.experimental.pallas import tpu as pltpu

f32 = jnp.float32
bf16 = jnp.bfloat16

NDEV = 8
D = 1024
EPS = 1e-6
CHUNK = 64
GBLK = 128
A_GROUPS = 8
C_GROUPS = 4
C_GDIM = 256
POOL_WINDOWS = (2, 4, 8, 16)
HALO = 16
COLS = 256
F_LOC = 352
MXU_COLS = 256
DEPTH = 4
AXES = ("x", "y", "c")
MESH = pl.DeviceIdType.MESH

ADAM_LR = 0.001
ADAM_B1 = 0.9
ADAM_B2 = 0.999
ADAM_EPS = 1e-08
ADAM_WD = 0.01
ADAM_STEP = 10

VMEM_LIMIT = 56 * 1024 * 1024


def _cparams(sem):
    return pltpu.CompilerParams(dimension_semantics=sem, vmem_limit_bytes=VMEM_LIMIT)


def _gelu(z):
    return 0.5 * z * (1.0 + lax.erf(z * 0.7071067811865476))


def _gelu_cdf(z):
    return 0.5 * (1.0 + lax.erf(z * 0.7071067811865476))


def _gelu_grad(z, cdf):
    return cdf + z * jnp.exp(-0.5 * z * z) * 0.3989422804014327


def _dot(a, b):
    return jnp.dot(a, b, preferred_element_type=f32)


def _dot_nt(a, b):
    return lax.dot_general(a, b, (((1,), (1,)), ((), ())), preferred_element_type=f32)


def _dot_tn(a, b):
    return lax.dot_general(a, b, (((0,), (0,)), ((), ())), preferred_element_type=f32)


def _col_chunks(n, width=1024):
    return [(c, min(c + width, n)) for c in range(0, n, width)]


def ffn_fwd(y, w_out, xin, g, wtg, wtu, wd, *, tm, name):
    T, K = y.shape
    H = wtg.shape[0]

    def body(y_ref, wo_ref, xin_ref, g_ref, wg_ref, wu_ref, wd_ref, x_ref, x2_ref, pg_ref, pu_ref, a_ref, h_ref):
        xv = xin_ref[...] + _dot(y_ref[...], wo_ref[...])
        x_ref[...] = xv
        r = lax.rsqrt(jnp.mean(xv * xv, axis=-1, keepdims=True) + EPS)
        h = (xv * r * g_ref[...]).astype(bf16)
        h_ref[...] = h
        out = None
        for c0, c1 in _col_chunks(H):
            gate = _dot_nt(h, wg_ref[c0:c1, :])
            up = _dot_nt(h, wu_ref[c0:c1, :])
            pg_ref[:, c0:c1] = gate.astype(bf16)
            pu_ref[:, c0:c1] = up.astype(bf16)
            ac = (gate * jax.nn.sigmoid(gate) * up).astype(bf16)
            a_ref[:, c0:c1] = ac
            part = _dot(ac, wd_ref[c0:c1, :])
            out = part if out is None else out + part
        x2_ref[...] = xv + out

    wspec = pl.BlockSpec((H, D), lambda i: (0, 0), pipeline_mode=pl.Buffered(1))
    hspec = pl.BlockSpec((tm, H), lambda i: (i, 0))
    dspec = pl.BlockSpec((tm, D), lambda i: (i, 0))
    return pl.pallas_call(
        body, grid=(T // tm,),
        in_specs=[pl.BlockSpec((tm, K), lambda i: (i, 0)),
                  pl.BlockSpec((K, D), lambda i: (0, 0), pipeline_mode=pl.Buffered(1)),
                  dspec, pl.BlockSpec((1, D), lambda i: (0, 0)), wspec, wspec, wspec],
        out_specs=[dspec, dspec, hspec, hspec, hspec, dspec],
        out_shape=[jax.ShapeDtypeStruct((T, D), f32)] * 2 + [jax.ShapeDtypeStruct((T, H), bf16)] * 3
        + [jax.ShapeDtypeStruct((T, D), bf16)],
        compiler_params=_cparams(("parallel",)), name=name,
    )(y, w_out, xin, g, wtg, wtu, wd)


def ffn_bwd(dx, wd, pg, pu, wtg, wtu, x, g, tok, w_next, *, tm, name):
    T = dx.shape[0]
    H = wd.shape[0]
    kn = w_next.shape[0]
    toks = [] if tok is None else [tok]

    def body(dx_ref, wd_ref, pg_ref, pu_ref, wg_ref, wu_ref, x_ref, g_ref, *rest):
        wn_ref = rest[len(toks)]
        dg_ref, du_ref, dxb_ref, dx1_ref, dgn_ref, dy_ref, dx1b_ref = rest[len(toks) + 1:]

        @pl.when(pl.program_id(0) == 0)
        def _():
            dgn_ref[...] = jnp.zeros(dgn_ref.shape, f32)

        dxv = dx_ref[...]
        dxb = dxv.astype(bf16)
        dxb_ref[...] = dxb
        dh = None
        for c0, c1 in _col_chunks(H):
            da = _dot_nt(dxb, wd_ref[c0:c1, :])
            gate = pg_ref[:, c0:c1].astype(f32)
            up = pu_ref[:, c0:c1].astype(f32)
            sg = jax.nn.sigmoid(gate)
            dgc = (da * up * (sg * (1.0 + gate * (1.0 - sg)))).astype(bf16)
            duc = (da * gate * sg).astype(bf16)
            dg_ref[:, c0:c1] = dgc
            du_ref[:, c0:c1] = duc
            part = _dot(dgc, wg_ref[c0:c1, :]) + _dot(duc, wu_ref[c0:c1, :])
            dh = part if dh is None else dh + part
        gv = g_ref[...] + rest[0][0:1, 0:1] if toks else g_ref[...]
        xv = x_ref[...]
        r = lax.rsqrt(jnp.mean(xv * xv, axis=-1, keepdims=True) + EPS)
        xhat = xv * r
        dgn_ref[...] += jnp.sum(dh * xhat, axis=0, keepdims=True)
        dxhat = dh * gv
        dx1 = dxv + r * (dxhat - xhat * jnp.mean(dxhat * xhat, axis=-1, keepdims=True))
        dx1_ref[...] = dx1
        dx1b = dx1.astype(bf16)
        dx1b_ref[...] = dx1b
        dy_ref[...] = _dot_nt(dx1b, wn_ref[...]).astype(bf16)

    dspec = pl.BlockSpec((tm, D), lambda i: (i, 0))
    hspec = pl.BlockSpec((tm, H), lambda i: (i, 0))
    wspec = pl.BlockSpec((H, D), lambda i: (0, 0), pipeline_mode=pl.Buffered(1))
    gspec = pl.BlockSpec((1, D), lambda i: (0, 0))
    return pl.pallas_call(
        body, grid=(T // tm,),
        in_specs=[dspec, wspec, hspec, hspec, wspec, wspec, dspec, gspec]
        + [pl.BlockSpec((8, 128), lambda i: (0, 0))] * len(toks)
        + [pl.BlockSpec((kn, D), lambda i: (0, 0), pipeline_mode=pl.Buffered(1))],
        out_specs=[hspec, hspec, dspec, dspec, gspec, pl.BlockSpec((tm, kn), lambda i: (i, 0)), dspec],
        out_shape=[jax.ShapeDtypeStruct((T, H), bf16)] * 2 + [jax.ShapeDtypeStruct((T, D), bf16),
                   jax.ShapeDtypeStruct((T, D), f32), jax.ShapeDtypeStruct((1, D), f32),
                   jax.ShapeDtypeStruct((T, kn), bf16), jax.ShapeDtypeStruct((T, D), bf16)],
        compiler_params=_cparams(("arbitrary",)), name=name,
    )(dx, wd, pg, pu, wtg, wtu, x, g, *toks, w_next)


def mm_tn2(a1, b1, a2, b2, *, name):
    T, K1 = a1.shape
    K2 = a2.shape[1]
    n1, n2 = b1.shape[1], b2.shape[1]
    last2 = K2 // MXU_COLS - 1

    def body(a1_ref, b1_ref, a2_ref, b2_ref, o1_ref, o2_ref):
        o1_ref[...] = _dot_tn(a1_ref[...], b1_ref[...]).astype(bf16)

        @pl.when(pl.program_id(0) <= last2)
        def _():
            o2_ref[...] = _dot_tn(a2_ref[...], b2_ref[...]).astype(bf16)

    o1, o2 = pl.pallas_call(
        body, grid=(K1 // MXU_COLS,),
        in_specs=[pl.BlockSpec((T, MXU_COLS), lambda d: (0, d)),
                  pl.BlockSpec((T, n1), lambda d: (0, 0), pipeline_mode=pl.Buffered(1)),
                  pl.BlockSpec((T, MXU_COLS), lambda d: (0, jnp.minimum(d, last2))),
                  pl.BlockSpec((T, n2), lambda d: (0, 0), pipeline_mode=pl.Buffered(1))],
        out_specs=[pl.BlockSpec((MXU_COLS, n1), lambda d: (d, 0)),
                   pl.BlockSpec((MXU_COLS, n2), lambda d: (jnp.minimum(d, last2), 0))],
        out_shape=[jax.ShapeDtypeStruct((K1, n1), bf16), jax.ShapeDtypeStruct((K2, n2), bf16)],
        compiler_params=_cparams(("arbitrary",)), name=name,
    )(a1, b1, a2, b2)
    return o1.reshape(NDEV, K1 // NDEV, n1), o2.reshape(NDEV, K2 // NDEV, n2)


def ffn_dw(dpg, dpu, a, h2, dxb, *, name):
    T, H = dpg.shape

    def body(g_ref, u_ref, a_ref, h_ref, x_ref, og_ref, ou_ref, od_ref):
        h = h_ref[...]
        og_ref[...] = _dot_tn(g_ref[...], h).astype(bf16)
        ou_ref[...] = _dot_tn(u_ref[...], h).astype(bf16)
        od_ref[...] = _dot_tn(a_ref[...], x_ref[...]).astype(bf16)

    blk = pl.BlockSpec((T, MXU_COLS), lambda d: (0, d))
    full = pl.BlockSpec((T, D), lambda d: (0, 0), pipeline_mode=pl.Buffered(1))
    out = pl.BlockSpec((MXU_COLS, D), lambda d: (d, 0))
    outs = pl.pallas_call(
        body, grid=(H // MXU_COLS,),
        in_specs=[blk, blk, blk, full, full], out_specs=[out] * 3,
        out_shape=[jax.ShapeDtypeStruct((H, D), bf16)] * 3,
        compiler_params=_cparams(("parallel",)), name=name,
    )(dpg, dpu, a, h2, dxb)
    return [o.reshape(NDEV, H // NDEV, D) for o in outs]


def _normbwd_tail(dh, x_ref, g_ref, tok_ref, dres_ref, dx_ref, dgn_ref):
    gv = g_ref[...] + tok_ref[0:1, 0:1]
    xv = x_ref[...]
    r = lax.rsqrt(jnp.mean(xv * xv, axis=-1, keepdims=True) + EPS)
    xhat = xv * r
    dgn_ref[...] += jnp.sum(dh * xhat, axis=0, keepdims=True)
    dxhat = dh * gv
    dx_ref[...] = dres_ref[...] + r * (dxhat - xhat * jnp.mean(dxhat * xhat, axis=-1, keepdims=True))


def _normbwd_specs(tm, w):
    dspec = pl.BlockSpec((tm, D), lambda i: (i, 0))
    gspec = pl.BlockSpec((1, D), lambda i: (0, 0))
    ins = [pl.BlockSpec(w.shape, lambda i: (0, 0), pipeline_mode=pl.Buffered(1)), dspec, gspec,
           pl.BlockSpec((8, 128), lambda i: (0, 0)), dspec]
    return ins, [dspec, gspec]


def _prev_halo_spec(tm, ncol):
    return pl.BlockSpec((HALO, ncol), lambda i: (jnp.maximum(i * (tm // HALO) - 1, 0), 0))


def _next_halo_spec(tm, ncol, T):
    return pl.BlockSpec((HALO, ncol), lambda i: (jnp.minimum((i + 1) * (tm // HALO), T // HALO - 1), 0))


def a_fwd(x, g, wt, gv, wm, bsb, *, tm, name):
    T = x.shape[0]

    def body(x_ref, g_ref, wt_ref, gv_ref, wm_ref, bs_ref, p_ref, h_ref, y_ref, tok_ref, vn_sc):
        tok_ref[...] = jnp.zeros(tok_ref.shape, f32)
        xv = x_ref[...]
        r = lax.rsqrt(jnp.mean(xv * xv, axis=-1, keepdims=True) + EPS)
        h = (xv * r * g_ref[...]).astype(bf16)
        h_ref[...] = h
        p_ref[...] = _dot_nt(h, wt_ref[...]).astype(bf16)
        v = _gelu(p_ref[:, D:].astype(f32))
        vc = v - jnp.mean(v, axis=-1, keepdims=True)
        var = jnp.mean(vc * vc, axis=-1, keepdims=True)
        vn_sc[...] = (vc * lax.rsqrt(var + EPS) * gv_ref[...]).astype(bf16)
        for g in range(A_GROUPS):
            cs = slice(g * GBLK, (g + 1) * GBLK)
            for r in range(tm // GBLK):
                rs = slice(r * GBLK, (r + 1) * GBLK)
                sv = _dot(wm_ref[g], vn_sc[rs, cs]) + bs_ref[g]
                y_ref[rs, cs] = (_gelu(p_ref[rs, cs].astype(f32)) * sv).astype(bf16)

    dspec = pl.BlockSpec((tm, D), lambda i: (i, 0))
    gspec = pl.BlockSpec((1, D), lambda i: (0, 0))
    return pl.pallas_call(
        body, grid=(T // tm,),
        in_specs=[dspec, gspec, pl.BlockSpec(wt.shape, lambda i: (0, 0), pipeline_mode=pl.Buffered(1)), gspec,
                  pl.BlockSpec((A_GROUPS, GBLK, GBLK), lambda i: (0, 0, 0)),
                  pl.BlockSpec((A_GROUPS, GBLK, GBLK), lambda i: (0, 0, 0))],
        out_specs=[pl.BlockSpec((tm, 2 * D), lambda i: (i, 0)), dspec, dspec, pl.BlockSpec((8, 128), lambda i: (0, 0))],
        out_shape=[jax.ShapeDtypeStruct((T, 2 * D), bf16), jax.ShapeDtypeStruct((T, D), bf16),
                   jax.ShapeDtypeStruct((T, D), bf16), jax.ShapeDtypeStruct((8, 128), f32)],
        scratch_shapes=[pltpu.VMEM((tm, D), bf16)],
        compiler_params=_cparams(("arbitrary",)), name=name,
    )(x, g, wt, gv, wm, bsb)


def a_mid_bwd(p, dy, gv, wm, bsb, wt, x, g, tok, dres, *, tm, name):
    T = p.shape[0]

    def body(p_ref, dy_ref, gv_ref, wm_ref, bs_ref, wt_ref, x_ref, g_ref, tok_ref, dres_ref,
             dp_ref, dgv_ref, dwm_ref, dbs_ref, dx_ref, dgn_ref, vn_sc, dvn_sc):
        @pl.when(pl.program_id(0) == 0)
        def _():
            dgn_ref[...] = jnp.zeros(dgn_ref.shape, f32)
            dgv_ref[...] = jnp.zeros(dgv_ref.shape, f32)
            dwm_ref[...] = jnp.zeros(dwm_ref.shape, f32)
            dbs_ref[...] = jnp.zeros(dbs_ref.shape, f32)

        zv = p_ref[:, D:].astype(f32)
        cdf_v = _gelu_cdf(zv)
        v = zv * cdf_v
        vc = v - jnp.mean(v, axis=-1, keepdims=True)
        rstd = lax.rsqrt(jnp.mean(vc * vc, axis=-1, keepdims=True) + EPS)
        vhat = vc * rstd
        vn_sc[...] = (vhat * gv_ref[...]).astype(bf16)
        for g in range(A_GROUPS):
            cs = slice(g * GBLK, (g + 1) * GBLK)
            dwm = jnp.zeros((GBLK, GBLK), f32)
            dbs = jnp.zeros((GBLK, 1), f32)
            for r in range(tm // GBLK):
                rs = slice(r * GBLK, (r + 1) * GBLK)
                zu = p_ref[rs, cs].astype(f32)
                cdf_u = _gelu_cdf(zu)
                vn = vn_sc[rs, cs]
                sv = _dot(wm_ref[g], vn) + bs_ref[g]
                dyb = dy_ref[rs, cs].astype(f32)
                dsv = dyb * (zu * cdf_u)
                dsvb = dsv.astype(bf16)
                dp_ref[rs, cs] = (dyb * sv * _gelu_grad(zu, cdf_u)).astype(bf16)
                dwm += _dot_nt(dsvb, vn)
                dbs += jnp.sum(dsv, axis=1, keepdims=True)
                dvn_sc[rs, cs] = _dot_tn(wm_ref[g], dsvb)
            dwm_ref[g] += dwm
            dbs_ref[g] += dbs
        dvn = dvn_sc[...]
        dgv_ref[...] += jnp.sum(dvn * vhat, axis=0, keepdims=True)
        dvhat = dvn * gv_ref[...]
        dv = rstd * (dvhat - jnp.mean(dvhat, axis=-1, keepdims=True)
                     - vhat * jnp.mean(dvhat * vhat, axis=-1, keepdims=True))
        dp_ref[:, D:] = (dv * _gelu_grad(zv, cdf_v)).astype(bf16)
        _normbwd_tail(_dot(dp_ref[...], wt_ref[...]), x_ref, g_ref, tok_ref, dres_ref, dx_ref, dgn_ref)

    tail_in, tail_out = _normbwd_specs(tm, wt)
    return pl.pallas_call(
        body, grid=(T // tm,),
        in_specs=[pl.BlockSpec((tm, 2 * D), lambda i: (i, 0)), pl.BlockSpec((tm, D), lambda i: (i, 0)),
                  pl.BlockSpec((1, D), lambda i: (0, 0)),
                  pl.BlockSpec((A_GROUPS, GBLK, GBLK), lambda i: (0, 0, 0)),
                  pl.BlockSpec((A_GROUPS, GBLK, GBLK), lambda i: (0, 0, 0))] + tail_in,
        out_specs=[pl.BlockSpec((tm, 2 * D), lambda i: (i, 0)), pl.BlockSpec((1, D), lambda i: (0, 0)),
                   pl.BlockSpec((A_GROUPS, GBLK, GBLK), lambda i: (0, 0, 0)),
                   pl.BlockSpec((A_GROUPS, GBLK, 1), lambda i: (0, 0, 0))] + tail_out,
        out_shape=[jax.ShapeDtypeStruct((T, 2 * D), bf16), jax.ShapeDtypeStruct((1, D), f32),
                   jax.ShapeDtypeStruct((A_GROUPS, GBLK, GBLK), f32), jax.ShapeDtypeStruct((A_GROUPS, GBLK, 1), f32),
                   jax.ShapeDtypeStruct((T, D), f32), jax.ShapeDtypeStruct((1, D), f32)],
        scratch_shapes=[pltpu.VMEM((tm, D), bf16), pltpu.VMEM((tm, D), f32)],
        compiler_params=_cparams(("arbitrary",)), name=name,
    )(p, dy, gv, wm, bsb, wt, x, g, tok, dres)


def b_fwd(x, g, wt, cw, *, tm, S, name):
    T = x.shape[0]
    nst = S // tm

    def body(x_ref, g_ref, wt_ref, cw_ref, p_ref, h_ref, y_ref, ext):
        first = (pl.program_id(0) % nst) == 0
        xv = x_ref[...]
        r = lax.rsqrt(jnp.mean(xv * xv, axis=-1, keepdims=True) + EPS)
        h = (xv * r * g_ref[...]).astype(bf16)
        h_ref[...] = h
        p_ref[...] = _dot_nt(h, wt_ref[...]).astype(bf16)

        @pl.when(first)
        def _():
            ext[0:HALO, :] = jnp.zeros((HALO, D), f32)

        for c0 in range(0, D, COLS):
            cs, cc, cx = slice(c0, c0 + COLS), slice(D + c0, D + c0 + COLS), slice(2 * D + c0, 2 * D + c0 + COLS)
            q = p_ref[:, cc].astype(f32) * p_ref[:, cx].astype(f32)
            ext[HALO:, cs] = q
            y = (cw_ref[0:1, cs] * ext[pl.ds(HALO - 2, tm), cs] + cw_ref[1:2, cs] * ext[pl.ds(HALO - 1, tm), cs]
                 + cw_ref[2:3, cs] * q)
            y_ref[:, cs] = (p_ref[:, cs].astype(f32) * y).astype(bf16)
            ext[0:HALO, cs] = q[tm - HALO:, :]

    dspec = pl.BlockSpec((tm, D), lambda i: (i, 0))
    return pl.pallas_call(
        body, grid=(T // tm,),
        in_specs=[dspec, pl.BlockSpec((1, D), lambda i: (0, 0)),
                  pl.BlockSpec(wt.shape, lambda i: (0, 0), pipeline_mode=pl.Buffered(1)),
                  pl.BlockSpec((3, D), lambda i: (0, 0))],
        out_specs=[pl.BlockSpec((tm, 3 * D), lambda i: (i, 0)), dspec, dspec],
        out_shape=[jax.ShapeDtypeStruct((T, 3 * D), bf16), jax.ShapeDtypeStruct((T, D), bf16),
                   jax.ShapeDtypeStruct((T, D), bf16)],
        scratch_shapes=[pltpu.VMEM((tm + HALO, D), f32)],
        compiler_params=_cparams(("arbitrary",)), name=name,
    )(x, g, wt, cw)


def b_mid_bwd(p, dy, cw, wt, x, g, tok, dres, *, tm, S, name):
    T = p.shape[0]
    nst = S // tm

    def body(p_ref, ph_ref, pn_ref, dy_ref, dyn_ref, cw_ref, wt_ref, x_ref, g_ref, tok_ref, dres_ref,
             dp_ref, dcw_ref, dx_ref, dgn_ref, ext, ext2):
        i = pl.program_id(0)
        first = (i % nst) == 0
        last = (i % nst) == nst - 1

        @pl.when(i == 0)
        def _():
            dgn_ref[...] = jnp.zeros(dgn_ref.shape, f32)
            dcw_ref[...] = jnp.zeros(dcw_ref.shape, f32)

        for c0 in range(0, D, COLS):
            cs, cc, cx = slice(c0, c0 + COLS), slice(D + c0, D + c0 + COLS), slice(2 * D + c0, 2 * D + c0 + COLS)
            gb, gc, xt = p_ref[:, cs].astype(f32), p_ref[:, cc].astype(f32), p_ref[:, cx].astype(f32)
            q = gc * xt
            ext[0:HALO, cs] = jnp.where(first, 0.0, ph_ref[:, cc].astype(f32) * ph_ref[:, cx].astype(f32))
            ext[HALO:, cs] = q
            q2 = ext[pl.ds(HALO - 2, tm), cs]
            q1 = ext[pl.ds(HALO - 1, tm), cs]
            y = cw_ref[0:1, cs] * q2 + cw_ref[1:2, cs] * q1 + cw_ref[2:3, cs] * q
            dyo = dy_ref[:, cs].astype(f32)
            dp_ref[:, cs] = (dyo * y).astype(bf16)
            dyc = dyo * gb
            ext2[0:tm, cs] = dyc
            ext2[tm:, cs] = jnp.where(last, 0.0, dyn_ref[:, cs].astype(f32) * pn_ref[:, cs].astype(f32))
            dq = cw_ref[2:3, cs] * dyc + cw_ref[1:2, cs] * ext2[pl.ds(1, tm), cs] + cw_ref[0:1, cs] * ext2[pl.ds(2, tm), cs]
            dp_ref[:, cc] = (dq * xt).astype(bf16)
            dp_ref[:, cx] = (dq * gc).astype(bf16)
            dcw_ref[0:1, cs] += jnp.sum(dyc * q2, axis=0, keepdims=True)
            dcw_ref[1:2, cs] += jnp.sum(dyc * q1, axis=0, keepdims=True)
            dcw_ref[2:3, cs] += jnp.sum(dyc * q, axis=0, keepdims=True)
        _normbwd_tail(_dot(dp_ref[...], wt_ref[...]), x_ref, g_ref, tok_ref, dres_ref, dx_ref, dgn_ref)

    tail_in, tail_out = _normbwd_specs(tm, wt)
    return pl.pallas_call(
        body, grid=(T // tm,),
        in_specs=[pl.BlockSpec((tm, 3 * D), lambda i: (i, 0)), _prev_halo_spec(tm, 3 * D), _next_halo_spec(tm, 3 * D, T),
                  pl.BlockSpec((tm, D), lambda i: (i, 0)), _next_halo_spec(tm, D, T),
                  pl.BlockSpec((3, D), lambda i: (0, 0))] + tail_in,
        out_specs=[pl.BlockSpec((tm, 3 * D), lambda i: (i, 0)), pl.BlockSpec((3, D), lambda i: (0, 0))] + tail_out,
        out_shape=[jax.ShapeDtypeStruct((T, 3 * D), bf16), jax.ShapeDtypeStruct((3, D), f32),
                   jax.ShapeDtypeStruct((T, D), f32), jax.ShapeDtypeStruct((1, D), f32)],
        scratch_shapes=[pltpu.VMEM((tm + HALO, D), f32), pltpu.VMEM((tm + HALO, D), f32)],
        compiler_params=_cparams(("arbitrary",)), name=name,
    )(p, p, p, dy, dy, cw, wt, x, g, tok, dres)


def _pool_counts(i, nst, tm, rows, row0, w):
    t = (i % nst) * tm + row0 + lax.broadcasted_iota(jnp.int32, (rows, 1), 0)
    return jnp.minimum(t + 1, w).astype(f32)


def _pool_diff(p_ref, ext, g, i, nst, tm):
    w = POOL_WINDOWS[g]
    cs = slice(g * C_GDIM, (g + 1) * C_GDIM)
    pg = p_ref[:, cs].astype(f32)
    s = pg
    for k in range(1, w):
        s = s + ext[pl.ds(HALO - k, tm), cs]
    return s / _pool_counts(i, nst, tm, tm, 0, w) - pg


def c_fwd(x, g, w, wg, scale, *, tm, S, name):
    T = x.shape[0]
    nst = S // tm

    def body(x_ref, g_ref, w_ref, wg_ref, sc_ref, p_ref, h_ref, y_ref, ext):
        i = pl.program_id(0)
        xv = x_ref[...]
        r = lax.rsqrt(jnp.mean(xv * xv, axis=-1, keepdims=True) + EPS)
        h = (xv * r * g_ref[...]).astype(bf16)
        h_ref[...] = h
        p_ref[...] = _dot(h, w_ref[...]).astype(bf16)

        @pl.when((i % nst) == 0)
        def _():
            ext[0:HALO, :] = jnp.zeros((HALO, D), f32)

        ext[HALO:, :] = p_ref[...].astype(f32)
        for g_ in range(C_GROUPS):
            cs = slice(g_ * C_GDIM, (g_ + 1) * C_GDIM)
            dg = _pool_diff(p_ref, ext, g_, i, nst, tm).astype(bf16)
            wv = wg_ref[:, g_].reshape(C_GDIM, C_GDIM)
            y_ref[:, cs] = (_dot(dg, wv) * sc_ref[:, cs]).astype(bf16)
        ext[0:HALO, :] = ext[tm:tm + HALO, :]

    dspec = pl.BlockSpec((tm, D), lambda i: (i, 0))
    gspec = pl.BlockSpec((1, D), lambda i: (0, 0))
    return pl.pallas_call(
        body, grid=(T // tm,),
        in_specs=[dspec, gspec, pl.BlockSpec(w.shape, lambda i: (0, 0), pipeline_mode=pl.Buffered(1)),
                  pl.BlockSpec((NDEV, C_GROUPS, C_GDIM // NDEV, C_GDIM), lambda i: (0, 0, 0, 0)), gspec],
        out_specs=[dspec, dspec, dspec],
        out_shape=[jax.ShapeDtypeStruct((T, D), bf16)] * 3,
        scratch_shapes=[pltpu.VMEM((tm + HALO, D), f32)],
        compiler_params=_cparams(("arbitrary",)), name=name,
    )(x, g, w, wg, scale)


def c_mid_bwd(p, dy, wg, scale, w, x, g, tok, dres, *, tm, S, name):
    T = p.shape[0]
    nst = S // tm

    def body(p_ref, ph_ref, dy_ref, dyn_ref, wg_ref, sc_ref, w_ref, x_ref, g_ref, tok_ref, dres_ref,
             dp_ref, dsc_ref, dwg_ref, dx_ref, dgn_ref, ext, ext2):
        i = pl.program_id(0)
        first = (i % nst) == 0
        last = (i % nst) == nst - 1

        @pl.when(i == 0)
        def _():
            dgn_ref[...] = jnp.zeros(dgn_ref.shape, f32)
            dsc_ref[...] = jnp.zeros(dsc_ref.shape, f32)
            dwg_ref[...] = jnp.zeros(dwg_ref.shape, f32)

        ext[0:HALO, :] = jnp.where(first, 0.0, ph_ref[...].astype(f32))
        ext[HALO:, :] = p_ref[...].astype(f32)
        for g in range(C_GROUPS):
            w = POOL_WINDOWS[g]
            cs = slice(g * C_GDIM, (g + 1) * C_GDIM)
            dg = _pool_diff(p_ref, ext, g, i, nst, tm).astype(bf16)
            wv = wg_ref[:, g].reshape(C_GDIM, C_GDIM)
            dyo = dy_ref[:, cs].astype(f32)
            dsc_ref[:, cs] += jnp.sum(dyo * _dot(dg, wv), axis=0, keepdims=True)
            dyp = (dyo * sc_ref[:, cs]).astype(bf16)
            dypn = (dyn_ref[:, cs].astype(f32) * sc_ref[:, cs]).astype(bf16)
            dwg_ref[g] += _dot_tn(dg, dyp)
            dd = _dot_nt(dyp, wv)
            ddn = _dot_nt(dypn, wv)
            ext2[0:tm, cs] = dd / _pool_counts(i, nst, tm, tm, 0, w)
            ext2[tm:, cs] = jnp.where(last, 0.0, ddn / _pool_counts(i, nst, tm, HALO, tm, w))
            s = -dd
            for k in range(w):
                s = s + ext2[pl.ds(k, tm), cs]
            dp_ref[:, cs] = s.astype(bf16)
        _normbwd_tail(_dot_nt(dp_ref[...], w_ref[...]), x_ref, g_ref, tok_ref, dres_ref, dx_ref, dgn_ref)

    tail_in, tail_out = _normbwd_specs(tm, w)
    return pl.pallas_call(
        body, grid=(T // tm,),
        in_specs=[pl.BlockSpec((tm, D), lambda i: (i, 0)), _prev_halo_spec(tm, D),
                  pl.BlockSpec((tm, D), lambda i: (i, 0)), _next_halo_spec(tm, D, T),
                  pl.BlockSpec((NDEV, C_GROUPS, C_GDIM // NDEV, C_GDIM), lambda i: (0, 0, 0, 0)),
                  pl.BlockSpec((1, D), lambda i: (0, 0))] + tail_in,
        out_specs=[pl.BlockSpec((tm, D), lambda i: (i, 0)), pl.BlockSpec((1, D), lambda i: (0, 0)),
                   pl.BlockSpec((C_GROUPS, C_GDIM, C_GDIM), lambda i: (0, 0, 0))] + tail_out,
        out_shape=[jax.ShapeDtypeStruct((T, D), bf16), jax.ShapeDtypeStruct((1, D), f32),
                   jax.ShapeDtypeStruct((C_GROUPS, C_GDIM, C_GDIM), f32),
                   jax.ShapeDtypeStruct((T, D), f32), jax.ShapeDtypeStruct((1, D), f32)],
        scratch_shapes=[pltpu.VMEM((tm + HALO, D), f32), pltpu.VMEM((tm + HALO, D), f32)],
        compiler_params=_cparams(("arbitrary",)), name=name,
    )(p, p, dy, dy, wg, scale, w, x, g, tok, dres)


def final_loss(x, g, tgt, *, tm, name):
    T = x.shape[0]

    def body(x_ref, g_ref, t_ref, loss_ref, dx_ref, dg_ref):
        @pl.when(pl.program_id(0) == 0)
        def _():
            loss_ref[...] = jnp.zeros(loss_ref.shape, f32)
            dg_ref[...] = jnp.zeros(dg_ref.shape, f32)

        xv = x_ref[...]
        r = lax.rsqrt(jnp.mean(xv * xv, axis=-1, keepdims=True) + EPS)
        xhat = xv * r
        err = xhat * g_ref[...] - t_ref[...]
        loss_ref[...] += 0.5 * jnp.sum(jnp.mean(err * err, axis=-1, keepdims=True))
        dy = err * (1.0 / D)
        dg_ref[...] += jnp.sum(dy * xhat, axis=0, keepdims=True)
        dxhat = dy * g_ref[...]
        dx_ref[...] = r * (dxhat - xhat * jnp.mean(dxhat * xhat, axis=-1, keepdims=True))

    return pl.pallas_call(
        body, grid=(T // tm,),
        in_specs=[pl.BlockSpec((tm, D), lambda i: (i, 0)), pl.BlockSpec((1, D), lambda i: (0, 0)),
                  pl.BlockSpec((tm, D), lambda i: (i, 0))],
        out_specs=[pl.BlockSpec((8, 128), lambda i: (0, 0)), pl.BlockSpec((tm, D), lambda i: (i, 0)),
                   pl.BlockSpec((1, D), lambda i: (0, 0))],
        out_shape=[jax.ShapeDtypeStruct((8, 128), f32), jax.ShapeDtypeStruct((T, D), f32),
                   jax.ShapeDtypeStruct((1, D), f32)],
        compiler_params=_cparams(("arbitrary",)), name=name,
    )(x, g, tgt)


def _slot(px, py, pc):
    return 4 * px + 2 * py + pc


def _with_own_block(s, me):
    zone = lax.empty((NDEV,) + s.shape, s.dtype)
    return lax.dynamic_update_slice(zone, s[None], (me,) + (0,) * s.ndim)


def all_gather(arrs, me, *, name):
    n = len(arrs)

    def body(*refs):
        ins, outs = refs[:n], refs[2 * n:3 * n]
        send_sems, recv_sems = refs[3 * n:]
        x, y, c = lax.axis_index("x"), lax.axis_index("y"), lax.axis_index("c")
        me, sibling = (x, y, c), (x, y, 1 - c)
        chips = [(1 - x, y), (x, 1 - y), (1 - x, 1 - y)]

        def copy(a, k, block, to, src=None):
            dst = outs[a].at[_slot(*block)]
            return pltpu.make_async_remote_copy(
                src_ref=dst if src is None else src, dst_ref=dst,
                send_sem=send_sems.at[a, k], recv_sem=recv_sems.at[a, k], device_id=to, device_id_type=MESH)

        first = []
        for a in range(n):
            first.append(copy(a, 0, me, sibling, src=ins[a]))
            first += [copy(a, 1 + j, me, (*chip, c), src=ins[a]) for j, chip in enumerate(chips)]
        for cp in first:
            cp.start()
        passed = []
        for j, chip in enumerate(chips):
            for a in range(n):
                copy(a, 1 + j, (*chip, c), me).wait_recv()
                fwd = copy(a, 4 + j, (*chip, c), sibling)
                fwd.start()
                passed.append(fwd)
        for a in range(n):
            copy(a, 0, sibling, me).wait_recv()
        for j, chip in enumerate(chips):
            for a in range(n):
                copy(a, 4 + j, (*chip, 1 - c), me).wait_recv()
        for cp in first + passed:
            cp.wait_send()

    any_spec = pl.BlockSpec(memory_space=pl.ANY)
    return pl.pallas_call(
        body,
        in_specs=[any_spec] * (2 * n), out_specs=[any_spec] * n,
        out_shape=[jax.ShapeDtypeStruct((NDEV,) + a.shape, a.dtype) for a in arrs],
        input_output_aliases={n + i: i for i in range(n)},
        scratch_shapes=[pltpu.SemaphoreType.DMA((n, 7)), pltpu.SemaphoreType.DMA((n, 7))],
        compiler_params=pltpu.CompilerParams(has_side_effects=True), name=name,
    )(*arrs, *[_with_own_block(a, me) for a in arrs])


HBM_SPEC = pl.BlockSpec(memory_space=pltpu.HBM)
SEM_SPEC = pl.BlockSpec(memory_space=pltpu.SEMAPHORE)
ANY_SPEC = pl.BlockSpec(memory_space=pl.ANY)
TOKEN_SHAPE = jax.ShapeDtypeStruct((8, 128), f32)
DATAFLOW_EFFECT = pltpu.SideEffectType.DATAFLOW_SIDE_EFFECTING


def _in_hbm(a):
    return pltpu.with_memory_space_constraint(a, pltpu.HBM)


def _hbm_like(a):
    return pltpu.HBM(a.shape, a.dtype)


def _mesh_pos():
    return lax.axis_index("x"), lax.axis_index("y"), lax.axis_index("c")


def _gather_targets(x, y, c):
    return [(x, y, 1 - c), (1 - x, y, c), (x, 1 - y, c), (1 - x, 1 - y, c)]


def gather_start(shards, me, after, *, name):
    n = len(shards)
    extra = [] if after is None else [after]

    def body(*refs):
        srcs, lands = refs[:n], refs[n:2 * n]
        send_sems, recv_sems = refs[2 * n + len(extra)], refs[2 * n + len(extra) + 1]
        token = refs[-1]
        x, y, c = _mesh_pos()
        me = _slot(x, y, c)
        for a in range(n):
            for k, to in enumerate(_gather_targets(x, y, c)):
                pltpu.make_async_remote_copy(
                    src_ref=srcs[a], dst_ref=lands[a].at[me], send_sem=send_sems.at[4 * a + k], recv_sem=recv_sems.at[4 * a + k],
                    device_id=to, device_id_type=MESH).start()
        token[...] = jnp.zeros(token.shape, f32)

    lands = [_with_own_block(s, me) for s in shards]
    sems = pltpu.SemaphoreType.DMA((4 * n,))
    out = pl.pallas_call(
        body, name=name,
        in_specs=[HBM_SPEC] * (2 * n) + [ANY_SPEC] * len(extra),
        out_specs=[SEM_SPEC, SEM_SPEC] + [HBM_SPEC] * (2 * n) + [pl.BlockSpec(memory_space=pltpu.VMEM)],
        out_shape=[sems, sems] + [_hbm_like(s) for s in shards] + [_hbm_like(l) for l in lands] + [TOKEN_SHAPE],
        input_output_aliases={i: 2 + i for i in range(2 * n)},
        compiler_params=pltpu.CompilerParams(has_side_effects=DATAFLOW_EFFECT),
    )(*[_in_hbm(s) for s in shards], *[_in_hbm(l) for l in lands], *extra)
    return out[0], out[1], out[2:2 + n], out[2 + n:2 + 2 * n], out[-1]


def gather_wait(send_sems, recv_sems, shards, lands, after, *, name, forward=None):
    n = len(shards)
    after = list(after) if isinstance(after, (list, tuple)) else [after]
    groups = [] if forward is None else [forward, n - forward]

    def body(*refs):
        srcs, lands_in = refs[:n], refs[n:2 * n]
        send_sems, recv_sems = refs[2 * n], refs[2 * n + 1]
        x, y, c = _mesh_pos()
        for a in range(n):
            for k, frm in enumerate(_gather_targets(x, y, c)):
                cp = pltpu.make_async_remote_copy(
                    src_ref=srcs[a], dst_ref=lands_in[a].at[_slot(*frm)], send_sem=send_sems.at[4 * a + k],
                    recv_sem=recv_sems.at[4 * a + k], device_id=frm, device_id_type=MESH)
                cp.wait_send()
                cp.wait_recv()
        if groups:
            fsems = refs[len(refs) - 4:]
            for a in range(n):
                g, b = (0, a) if a < forward else (1, a - forward)
                for j, chip in enumerate([(1 - x, y), (x, 1 - y), (1 - x, 1 - y)]):
                    s = _slot(*chip, c)
                    pltpu.make_async_remote_copy(
                        src_ref=lands_in[a].at[s], dst_ref=lands_in[a].at[s], send_sem=fsems[2 * g].at[3 * b + j],
                        recv_sem=fsems[2 * g + 1].at[3 * b + j], device_id=(x, y, 1 - c), device_id_type=MESH).start()

    fshapes = [pltpu.SemaphoreType.DMA((3 * k,)) for k in groups for _ in range(2)]
    out = pl.pallas_call(
        body, name=name,
        in_specs=[HBM_SPEC] * (2 * n) + [SEM_SPEC, SEM_SPEC] + [ANY_SPEC] * len(after),
        out_specs=[HBM_SPEC] * (2 * n) + [SEM_SPEC] * len(fshapes),
        out_shape=[_hbm_like(s) for s in shards] + [_hbm_like(l) for l in lands] + fshapes,
        input_output_aliases={i: i for i in range(2 * n)},
        compiler_params=pltpu.CompilerParams(has_side_effects=DATAFLOW_EFFECT),
    )(*shards, *lands, send_sems, recv_sems, *after)
    if groups:
        return out[:n], out[n:2 * n], out[2 * n:]
    return out[:n], out[n:]


def gather_finish(lands, *, name):
    n = len(lands)

    def body(*refs):
        lands_in, lands_out = refs[:n], refs[n:2 * n]
        send_sems, recv_sems = refs[2 * n:]
        x, y, c = _mesh_pos()
        sibling = (x, y, 1 - c)
        chips = [(1 - x, y), (x, 1 - y), (1 - x, 1 - y)]
        sent = []
        for a in range(n):
            for j, chip in enumerate(chips):
                s = _slot(*chip, c)
                cp = pltpu.make_async_remote_copy(
                    src_ref=lands_in[a].at[s], dst_ref=lands_out[a].at[s], send_sem=send_sems.at[a, j],
                    recv_sem=recv_sems.at[a, j], device_id=sibling, device_id_type=MESH)
                cp.start()
                sent.append(cp)
        for a in range(n):
            for j, chip in enumerate(chips):
                s = _slot(*chip, 1 - c)
                pltpu.make_async_remote_copy(
                    src_ref=lands_in[a].at[s], dst_ref=lands_out[a].at[s], send_sem=send_sems.at[a, j],
                    recv_sem=recv_sems.at[a, j], device_id=sibling, device_id_type=MESH).wait_recv()
        for cp in sent:
            cp.wait_send()

    return pl.pallas_call(
        body, name=name,
        in_specs=[ANY_SPEC] * n, out_specs=[ANY_SPEC] * n,
        out_shape=[jax.ShapeDtypeStruct(l.shape, l.dtype) for l in lands],
        input_output_aliases={i: i for i in range(n)},
        scratch_shapes=[pltpu.SemaphoreType.DMA((n, 3)), pltpu.SemaphoreType.DMA((n, 3))],
        compiler_params=pltpu.CompilerParams(has_side_effects=True),
    )(*lands)


def forward_wait(send_sems, recv_sems, lands, after, *, name):
    n = len(lands)

    def body(*refs):
        lands_in = refs[:n]
        send_sems, recv_sems = refs[n], refs[n + 1]
        x, y, c = _mesh_pos()
        sibling = (x, y, 1 - c)
        for a in range(n):
            for j, chip in enumerate([(1 - x, y), (x, 1 - y), (1 - x, 1 - y)]):
                k = 3 * a + j
                cp = pltpu.make_async_remote_copy(
                    src_ref=lands_in[a].at[_slot(*chip, c)], dst_ref=lands_in[a].at[_slot(*chip, 1 - c)],
                    send_sem=send_sems.at[k], recv_sem=recv_sems.at[k], device_id=sibling,
                    device_id_type=MESH)
                cp.wait_send()
                cp.wait_recv()

    out = pl.pallas_call(
        body, name=name,
        in_specs=[HBM_SPEC] * n + [SEM_SPEC, SEM_SPEC] + [ANY_SPEC] * len(after),
        out_specs=[HBM_SPEC] * n,
        out_shape=[_hbm_like(l) for l in lands],
        input_output_aliases={i: i for i in range(n)},
        compiler_params=pltpu.CompilerParams(has_side_effects=DATAFLOW_EFFECT),
    )(*lands, send_sems, recv_sems, *after)
    return list(out)


def _peer(x, y, c, r):
    return (x ^ ((r >> 2) & 1), y ^ ((r >> 1) & 1), c ^ (r & 1))


def exchange_start(items, lands, after, *, name):
    n, m = len(items), len(lands)
    extra = [] if after is None else [after]

    def body(*refs):
        gs, zones = refs[:n], refs[n:n + m]
        send_sems, recv_sems = refs[n + m + len(extra)], refs[n + m + len(extra) + 1]
        token = refs[-1]
        x, y, c = _mesh_pos()
        me = _slot(x, y, c)
        for r in (1, 2, 4, 3, 5, 6, 7):
            to = _peer(x, y, c, r)
            for a, (_, zi, l) in enumerate(items):
                pltpu.make_async_remote_copy(
                    src_ref=gs[a].at[_slot(*to)], dst_ref=zones[zi].at[me, l], send_sem=send_sems.at[7 * a + r - 1],
                    recv_sem=recv_sems.at[7 * a + r - 1], device_id=to, device_id_type=MESH).start()
        token[...] = jnp.zeros(token.shape, f32)

    gs = [g for g, _, _ in items]
    sems = pltpu.SemaphoreType.DMA((7 * n,))
    out = pl.pallas_call(
        body, name=name,
        in_specs=[HBM_SPEC] * (n + m) + [ANY_SPEC] * len(extra),
        out_specs=[SEM_SPEC, SEM_SPEC] + [HBM_SPEC] * (n + m) + [pl.BlockSpec(memory_space=pltpu.VMEM)],
        out_shape=[sems, sems] + [_hbm_like(g) for g in gs] + [_hbm_like(z) for z in lands] + [TOKEN_SHAPE],
        input_output_aliases={i: 2 + i for i in range(n + m)},
        compiler_params=pltpu.CompilerParams(has_side_effects=DATAFLOW_EFFECT),
    )(*[_in_hbm(g) for g in gs], *[_in_hbm(z) for z in lands], *extra)
    return out[0], out[1], out[2:2 + n], out[2 + n:2 + n + m], out[-1]


def exchange_wait(parts, lands, after, *, name):
    m = len(lands)
    flat_gs = [g for _, _, _, gs in parts for g in gs]
    ng = len(flat_gs)
    after = list(after) if isinstance(after, (list, tuple)) else [after]

    def body(*refs):
        gs, zones = refs[:ng], refs[ng:ng + m]
        sem_refs = refs[ng + m:ng + m + 2 * len(parts)]
        x, y, c = _mesh_pos()
        base = 0
        for pi, (items, _, _, _) in enumerate(parts):
            send_sems, recv_sems = sem_refs[2 * pi], sem_refs[2 * pi + 1]
            for r in range(1, NDEV):
                frm = _peer(x, y, c, r)
                for a, (_, zi, l) in enumerate(items):
                    cp = pltpu.make_async_remote_copy(
                        src_ref=gs[base + a].at[_slot(*frm)], dst_ref=zones[zi].at[_slot(*frm), l],
                        send_sem=send_sems.at[7 * a + r - 1], recv_sem=recv_sems.at[7 * a + r - 1],
                        device_id=frm, device_id_type=MESH)
                    cp.wait_send()
                    cp.wait_recv()
            base += len(items)

    sem_args = [s for _, ss, rs, _ in parts for s in (ss, rs)]
    out = pl.pallas_call(
        body, name=name,
        in_specs=[HBM_SPEC] * (ng + m) + [SEM_SPEC] * len(sem_args) + [ANY_SPEC] * len(after),
        out_specs=[HBM_SPEC] * (ng + m),
        out_shape=[_hbm_like(g) for g in flat_gs] + [_hbm_like(z) for z in lands],
        input_output_aliases={i: i for i in range(ng + m)},
        compiler_params=pltpu.CompilerParams(has_side_effects=DATAFLOW_EFFECT),
    )(*flat_gs, *lands, *sem_args, *after)
    return out[ng:]


def sum8(g, *, name):
    _, R, C = g.shape
    tr = R

    def body(g_ref, o_ref):
        s = g_ref[0]
        for k in range(1, NDEV):
            s = s + g_ref[k]
        o_ref[...] = s

    return pl.pallas_call(
        body, grid=(R // tr,),
        in_specs=[pl.BlockSpec((NDEV, tr, C), lambda i: (0, i, 0))],
        out_specs=pl.BlockSpec((tr, C), lambda i: (i, 0)),
        out_shape=jax.ShapeDtypeStruct((R, C), f32),
        compiler_params=_cparams(("parallel",)), name=name,
    )(g)


def _adam_math(w, g, m, v):
    m = ADAM_B1 * m + (1.0 - ADAM_B1) * g
    v = ADAM_B2 * v + (1.0 - ADAM_B2) * (g * g)
    m_hat = m / (1.0 - ADAM_B1 ** ADAM_STEP)
    v_hat = v / (1.0 - ADAM_B2 ** ADAM_STEP)
    delta = -ADAM_LR * (m_hat / (jnp.sqrt(v_hat) + ADAM_EPS) + ADAM_WD * w)
    return delta, m, v


def adamw_slices(w, m, v, recv, *, name):
    return adamw_families([w], [m], [v], [recv], tc=w.shape[2], name=name)


def adamw_families(ws, ms, vs, recvs, *, tc, l0=0, nl=None, prev=None, name):
    nf = len(ws)
    L, R, C = ws[0].shape
    nl = L - l0 if nl is None else nl
    prevs = [] if prev is None else list(prev)

    def body(*refs):
        outs = refs[4 * nf + len(prevs):]
        for q in range(nf):
            w_ref, m_ref, v_ref, r_ref = refs[q], refs[nf + q], refs[2 * nf + q], refs[3 * nf + q]
            g = r_ref[0].astype(f32)
            for k in range(1, NDEV):
                g = g + r_ref[k].astype(f32)
            delta, nm, nv = _adam_math(w_ref[...], g, m_ref[...], v_ref[...])
            outs[4 * q][...] = g
            outs[4 * q + 1][...] = delta
            outs[4 * q + 2][...] = nm
            outs[4 * q + 3][...] = nv

    wspec = pl.BlockSpec((None, R, tc), lambda l, j: (l0 + l, 0, j))
    rspec = pl.BlockSpec((NDEV, None, R, tc), lambda l, j: (0, l0 + l, 0, j))
    return pl.pallas_call(
        body, grid=(nl, C // tc),
        in_specs=[wspec] * (3 * nf) + [rspec] * nf + [pl.BlockSpec(memory_space=pl.ANY)] * len(prevs),
        out_specs=[wspec] * (4 * nf), out_shape=[jax.ShapeDtypeStruct((L, R, C), f32)] * (4 * nf),
        input_output_aliases={4 * nf + q: q for q in range(len(prevs))},
        compiler_params=_cparams(("parallel", "parallel")), name=name,
    )(*ws, *ms, *vs, *recvs, *prevs)


def adamw_slices_t(w, m, v, recv, *, name):
    L, R, C = w.shape

    def body(w_ref, m_ref, v_ref, r_ref, g_ref, d_ref, nm_ref, nv_ref):
        gt = r_ref[0].astype(f32)
        for k in range(1, NDEV):
            gt = gt + r_ref[k].astype(f32)
        g = gt.T
        delta, nm, nv = _adam_math(w_ref[...], g, m_ref[...], v_ref[...])
        g_ref[...] = g
        d_ref[...] = delta
        nm_ref[...] = nm
        nv_ref[...] = nv

    wspec = pl.BlockSpec((None, R, C), lambda l: (l, 0, 0))
    return pl.pallas_call(
        body, grid=(L,),
        in_specs=[wspec, wspec, wspec, pl.BlockSpec((NDEV, None, C, R), lambda l: (0, l, 0, 0))],
        out_specs=[wspec] * 4, out_shape=[jax.ShapeDtypeStruct((L, R, C), f32)] * 4,
        compiler_params=_cparams(("parallel",)), name=name,
    )(w, m, v, recv)


def adamw_plain(w, m, v, g, *, name):
    R, C = w.shape

    def body(w_ref, m_ref, v_ref, g_ref, d_ref, nm_ref, nv_ref):
        delta, nm, nv = _adam_math(w_ref[...], g_ref[...], m_ref[...], v_ref[...])
        d_ref[...] = delta
        nm_ref[...] = nm
        nv_ref[...] = nv

    return pl.pallas_call(
        body, out_shape=[jax.ShapeDtypeStruct((R, C), f32)] * 3, name=name,
    )(w, m, v, g)


def _chunk_causal_mask():
    pos = jnp.arange(GBLK)
    return (pos[None, :] // CHUNK) <= (pos[:, None] // CHUNK)


def kernel(x, norm_mix_g, norm_ffn_g, final_norm_g, a_w_in, a_v_norm_g, a_w_s, a_b_s, a_w_out, b_w_in, b_conv_w, b_w_out, c_w_in, c_w_grp, c_scale, c_w_out, f_w_gate, f_w_up, f_w_down, loss_target, m_norm_mix_g, m_norm_ffn_g, m_final_norm_g, m_a_w_in, m_a_v_norm_g, m_a_w_s, m_a_b_s, m_a_w_out, m_b_w_in, m_b_conv_w, m_b_w_out, m_c_w_in, m_c_w_grp, m_c_scale, m_c_w_out, m_f_w_gate, m_f_w_up, m_f_w_down, v_norm_mix_g, v_norm_ffn_g, v_final_norm_g, v_a_w_in, v_a_v_norm_g, v_a_w_s, v_a_b_s, v_a_w_out, v_b_w_in, v_b_conv_w, v_b_w_out, v_c_w_in, v_c_w_grp, v_c_scale, v_c_w_out, v_f_w_gate, v_f_w_up, v_f_w_down):
    B, S, _ = x.shape
    T = B * S
    tm = min(512, S)
    tms = min(512, T)
    tma = min(256, S)
    tmf = min(256, T)
    me = _slot(lax.axis_index("x"), lax.axis_index("y"), lax.axis_index("c"))

    def tb(w):
        return jnp.transpose(w, (0, 2, 1)).astype(bf16)

    small = jnp.concatenate([a_v_norm_g, b_conv_w[0], c_scale, jnp.zeros((2, GBLK), f32)], axis=0)
    started = [gather_start([tb(a_w_in[0:1])[0], a_w_out[0].astype(bf16), small], me, None, name="gather_start_0")]
    tok = started[0][4]

    def ffn_shards(l, z):
        return [tb(f_w_gate[l:l + 1] + z)[0], tb(f_w_up[l:l + 1] + z)[0], (f_w_down[l] + z).astype(bf16)]

    started.append(gather_start(ffn_shards(0, tok[0, 0]), me, tok, name="gather_start_1"))
    tok = started[-1][4]

    def start_later_stages(z, after):
        stage_shards = [
            [tb(b_w_in + z)[0], (b_w_out[0] + z).astype(bf16)] + ffn_shards(1, z),
            [(c_w_in[0] + z).astype(bf16), (c_w_grp[0] + z).astype(bf16), (c_w_out[0] + z).astype(bf16)] + ffn_shards(2, z),
            [tb(a_w_in[1:2] + z)[0], (a_w_out[1] + z).astype(bf16)] + ffn_shards(3, z),
        ]
        for q, shards in enumerate(stage_shards):
            started.append(gather_start(shards, me, after, name=f"gather_start_{q + 2}"))
            after = started[-1][4]
        return after

    def as_matrices(lands):
        return [t.reshape(-1, t.shape[-1]) if t.ndim == 3 else t for t in lands]

    def stage_weights(s, after):
        send_sems, recv_sems, shards, lands, _ = started[s]
        _, lands = gather_wait(send_sems, recv_sems, shards, lands, after, name=f"gather_wait_{s}")
        return as_matrices(gather_finish(lands, name=f"gather_finish_{s}"))

    def mixer_weights(s, after, n_mix):
        send_sems, recv_sems, shards, lands, _ = started[s]
        _, lands, fsems = gather_wait(send_sems, recv_sems, shards, lands, after, name=f"gather_wait_{s}", forward=n_mix)
        mix_lands = forward_wait(fsems[0], fsems[1], lands[:n_mix], [], name=f"forward_wait_mix_{s}")
        return as_matrices(mix_lands), (fsems[2], fsems[3], lands[n_mix:])

    def ffn_weights(s, pending, after):
        fsend, frecv, lands = pending
        return as_matrices(forward_wait(fsend, frecv, lands, after, name=f"forward_wait_{s}"))

    wm_masked = jnp.where(_chunk_causal_mask()[None, None], a_w_s, 0.0).astype(bf16)
    bsb = jnp.broadcast_to(a_b_s[:, :, :, None], a_b_s.shape + (GBLK,))

    xc = x.reshape(T, D)
    saved, weights = [], []
    for i in range(DEPTH):
        kind, j = i % 3, i // 3
        gmix = norm_mix_g[i][None]
        if i == 0:
            w_in, w_out, small_g = stage_weights(0, tok)
            small_full = jnp.transpose(small_g.reshape(NDEV, 8, GBLK), (1, 0, 2)).reshape(8, D)
            gv_full, cw_full, cs_full = small_full[0:2], small_full[2:5], small_full[5:6]
            w = [w_in, w_out]
        else:
            w, pending = mixer_weights(i + 1, xc, 3 if kind == 2 else 2)
        if kind == 0:
            w_in = w[0]
            p, h, y, tok_a = a_fwd(xc, gmix, w_in, gv_full[j][None], wm_masked[j], bsb[j], tm=tma, name=f"a_fwd_{i}")
            if i == 0:
                tok_late = start_later_stages(tok_a[0, 0], tok_a)
        elif kind == 1:
            w_in, w_out = w[:2]
            p, h, y = b_fwd(xc, gmix, w_in, cw_full, tm=tm, S=S, name=f"b_fwd_{i}")
        else:
            w_in, w_grp, w_out = w[:3]
            p, h, y = c_fwd(xc, gmix, w_in, w_grp, cs_full, tm=tm, S=S, name=f"c_fwd_{i}")
        if i == 0:
            w = w + stage_weights(1, [y, tok_late])
        else:
            w = w + ffn_weights(i + 1, pending, [y])
        weights.append(w)
        w_out, w_fg, w_fu, w_fd = w[-4:]
        x1, x2, pg, pu, a, h2 = ffn_fwd(y, w_out, xc, norm_ffn_g[i][None], w_fg, w_fu, w_fd, tm=tmf, name=f"f_fwd_{i}")
        saved.append((xc, p, h, y, x1, pg, pu, h2, a))
        xc = x2

    loss_blk, dx, d_final = final_loss(xc, final_norm_g[None], loss_target.reshape(T, D), tm=tms, name="final_loss")
    loss = lax.psum(loss_blk[0, 0], AXES)

    zone_shapes = [(2, 256, D), (2, GBLK, D), (1, 384, D), (1, GBLK, D), (1, GBLK, D), (1, GBLK, C_GDIM), (1, GBLK, D),
                   (DEPTH, F_LOC, D), (DEPTH, F_LOC, D), (DEPTH, F_LOC, D)]
    Z_AIN, Z_AOUT, Z_BIN, Z_BOUT, Z_CIN, Z_CGRP, Z_COUT, Z_FG, Z_FU, Z_FD = range(10)
    lands = [lax.empty((NDEV,) + s, bf16) for s in zone_shapes]
    parts = []

    def exchange(items, after, name):
        nonlocal lands
        for g, zi, l in items:
            own = lax.dynamic_index_in_dim(g, me, axis=0, keepdims=True)[None]
            lands[zi] = lax.dynamic_update_slice(lands[zi], own, (me, l) + (0,) * (g.ndim - 1))
        idx = sorted({zi for _, zi, _ in items})
        local = [(g, idx.index(zi), l) for g, zi, l in items]
        send_sems, recv_sems, gs, zs, token = exchange_start(local, [lands[zi] for zi in idx], after, name=name)
        for q, zi in enumerate(idx):
            lands[zi] = zs[q]
        parts.append(([(None, zi, l) for _, zi, l in items], send_sems, recv_sems, gs))
        return token

    d_mix, d_ffn = [None] * DEPTH, [None] * DEPTH
    d_gv, d_wm, d_bs = [None] * 2, [None] * 2, [None] * 2
    tok = None
    for i in reversed(range(DEPTH)):
        kind, j = i % 3, i // 3
        xin, p, h, y, x1, pg, pu, h2, a = saved[i]
        w = weights[i]
        w_in, w_out = w[0], w[-4]
        w_fg, w_fu, w_fd = w[-3:]
        dpg, dpu, dxb, dx1, d_ffn[i], dy, dx1b = ffn_bwd(dx, w_fd, pg, pu, w_fg, w_fu, x1, norm_ffn_g[i][None], tok, w_out,
                                                         tm=tmf, name=f"f_bwd_{i}")
        gw_g, gw_u, gw_d = ffn_dw(dpg, dpu, a, h2, dxb, name=f"f_dw_{i}")
        tok = exchange([(gw_g, Z_FG, i), (gw_u, Z_FU, i), (gw_d, Z_FD, i)], tok, f"exchange_start_ffn_{i}")
        gmix = norm_mix_g[i][None]
        if kind == 0:
            dp, d_gv[j], d_wm[j], d_bs[j], dx, d_mix[i] = a_mid_bwd(
                p, dy, gv_full[j][None], wm_masked[j], bsb[j], w_in, xin, gmix, tok, dx1, tm=tma, name=f"a_bwd_{i}")
            gw_in, gw_out = mm_tn2(dp, h, y, dx1b, name=f"a_dw_{i}")
            items = [(gw_in, Z_AIN, j), (gw_out, Z_AOUT, j)]
        elif kind == 1:
            dp, d_cw, dx, d_mix[i] = b_mid_bwd(p, dy, cw_full, w_in, xin, gmix, tok, dx1, tm=tm, S=S, name=f"b_bwd_{i}")
            gw_in, gw_out = mm_tn2(dp, h, y, dx1b, name=f"b_dw_{i}")
            items = [(gw_in, Z_BIN, 0), (gw_out, Z_BOUT, 0)]
        else:
            dp, d_cs, d_wgrp, dx, d_mix[i] = c_mid_bwd(p, dy, w[1], cs_full, w_in, xin, gmix, tok, dx1, tm=tm, S=S,
                                                       name=f"c_bwd_{i}")
            gw_cgrp = jnp.transpose(d_wgrp.reshape(C_GROUPS, NDEV, C_GDIM // NDEV, C_GDIM), (1, 0, 2, 3)).astype(bf16)
            gw_in, gw_out = mm_tn2(h, dp, y, dx1b, name=f"c_dw_{i}")
            items = [(gw_in, Z_CIN, 0), (gw_cgrp.reshape(NDEV, GBLK, C_GDIM), Z_CGRP, 0), (gw_out, Z_COUT, 0)]
        tok = exchange(items, tok, f"exchange_start_mix_{i}")
    grad_x = dx.reshape(B, S, D)

    small_parts = [
        jnp.concatenate(d_mix, axis=0).reshape(-1, GBLK),
        jnp.concatenate(d_ffn, axis=0).reshape(-1, GBLK),
        d_final.reshape(-1, GBLK),
        jnp.concatenate(d_gv, axis=0).reshape(-1, GBLK),
        jnp.where(_chunk_causal_mask()[None, None], jnp.stack(d_wm), 0.0).reshape(-1, GBLK),
        jnp.stack(d_bs).reshape(-1, GBLK),
        d_cw.reshape(-1, GBLK),
        d_cs.reshape(-1, GBLK),
    ]
    small_rows = [q.shape[0] for q in small_parts]
    sg_send, sg_recv, sg_shards, sg_lands, _ = gather_start(
        [jnp.concatenate(small_parts, axis=0)], me, tok, name="gather_small_start")

    def plain(w, m, v, g, name):
        shp = w.shape
        w2, m2, v2, g2 = (t.reshape(-1, shp[-1]) for t in (w, m, v, g))
        return tuple(t.reshape(shp) for t in adamw_plain(w2, m2, v2, g2, name=name))

    def sliced(w, m, v, recv, name):
        shp = w.shape
        w3, m3, v3 = (t.reshape((shp[0], -1, shp[-1])) for t in (w, m, v))
        return tuple(t.reshape(shp) for t in adamw_slices(w3, m3, v3, recv, name=name))

    lands = list(exchange_wait(parts[:6], lands, dx, name="exchange_wait"))
    _, _, r_bin, r_bout, r_cin, r_cgrp, r_cout, r_fg, r_fu, r_fd = lands
    res = {}
    res["b_w_in"] = adamw_slices_t(b_w_in, m_b_w_in, v_b_w_in, r_bin, name="adam_b_w_in")
    res["b_w_out"] = sliced(b_w_out, m_b_w_out, v_b_w_out, r_bout, "adam_b_w_out")
    res["c_w_in"] = sliced(c_w_in, m_c_w_in, v_c_w_in, r_cin, "adam_c_w_in")
    res["c_w_grp"] = sliced(c_w_grp, m_c_w_grp, v_c_w_grp, r_cgrp, "adam_c_w_grp")
    res["c_w_out"] = sliced(c_w_out, m_c_w_out, v_c_w_out, r_cout, "adam_c_w_out")
    def tr(t):
        return jnp.transpose(t, (0, 2, 1))

    f_ws = [tr(f_w_gate), tr(f_w_up), f_w_down]
    f_ms = [tr(m_f_w_gate), tr(m_f_w_up), m_f_w_down]
    f_vs = [tr(v_f_w_gate), tr(v_f_w_up), v_f_w_down]
    ff = adamw_families(f_ws, f_ms, f_vs, [r_fg, r_fu, r_fd], tc=256, l0=1, name="adam_ffn_123")
    big = [res[n][1] for n in ("b_w_in", "b_w_out", "c_w_in", "c_w_grp", "c_w_out")] + [ff[1], ff[5], ff[9]]

    ffn0_items, ffn0_send, ffn0_recv, ffn0_gs = parts[6]
    ffn0_zones = [Z_FG, Z_FU, Z_FD]
    ffn0_part = ([(None, ffn0_zones.index(zi), l) for _, zi, l in ffn0_items], ffn0_send, ffn0_recv, ffn0_gs)
    r_fg, r_fu, r_fd = exchange_wait([ffn0_part], [lands[zi] for zi in ffn0_zones], big, name="exchange_wait_ffn0")
    ff = adamw_families(f_ws, f_ms, f_vs, [r_fg, r_fu, r_fd], tc=256, l0=0, nl=1, prev=ff, name="adam_ffn_0")
    res["f_w_gate"], res["f_w_up"], res["f_w_down"] = tuple(tr(t) for t in ff[0:4]), tuple(tr(t) for t in ff[4:8]), tuple(ff[8:12])

    last_items, last_send, last_recv, last_gs = parts[7]
    last_zones = [Z_AIN, Z_AOUT]
    last_part = ([(None, last_zones.index(zi), l) for _, zi, l in last_items], last_send, last_recv, last_gs)
    r_ain, r_aout = exchange_wait([last_part], [lands[zi] for zi in last_zones], [ff[1], ff[5], ff[9]], name="exchange_wait_last")
    res["a_w_in"] = adamw_slices_t(a_w_in, m_a_w_in, v_a_w_in, r_ain, name="adam_a_w_in")
    res["a_w_out"] = sliced(a_w_out, m_a_w_out, v_a_w_out, r_aout, "adam_a_w_out")

    _, sg_lands = gather_wait(sg_send, sg_recv, sg_shards, sg_lands, [res["a_w_in"][1], res["a_w_out"][1]], name="gather_small_wait")
    (gs_all,) = gather_finish(sg_lands, name="gather_small_finish")
    gs = sum8(gs_all, name="sum_small_grads")
    offs = [0]
    for r in small_rows:
        offs.append(offs[-1] + r)
    sp = [gs[offs[q]:offs[q + 1]] for q in range(len(small_parts))]
    grad_norm_mix_g = sp[0].reshape(DEPTH, D)
    grad_norm_ffn_g = sp[1].reshape(DEPTH, D)
    grad_final_norm_g = sp[2].reshape(D)
    grad_a_w_s = sp[4].reshape(a_w_s.shape)
    grad_a_b_s = sp[5].reshape(a_b_s.shape)

    def my_cols(full):
        return lax.dynamic_slice_in_dim(full, me * GBLK, GBLK, axis=1)

    grad_a_v_norm_g = my_cols(sp[3].reshape(2, D))
    grad_b_conv_w = my_cols(sp[6].reshape(3, D))[None]
    grad_c_scale = my_cols(sp[7].reshape(1, D))
    res["norm_mix_g"] = (grad_norm_mix_g,) + plain(norm_mix_g, m_norm_mix_g, v_norm_mix_g, grad_norm_mix_g, "adam_norm_mix")
    res["norm_ffn_g"] = (grad_norm_ffn_g,) + plain(norm_ffn_g, m_norm_ffn_g, v_norm_ffn_g, grad_norm_ffn_g, "adam_norm_ffn")
    res["final_norm_g"] = (grad_final_norm_g,) + tuple(
        t.reshape(D) for t in plain(final_norm_g[None], m_final_norm_g[None], v_final_norm_g[None], grad_final_norm_g[None], "adam_final"))
    res["a_v_norm_g"] = (grad_a_v_norm_g,) + plain(a_v_norm_g, m_a_v_norm_g, v_a_v_norm_g, grad_a_v_norm_g, "adam_a_v_norm")
    res["a_w_s"] = (grad_a_w_s,) + plain(a_w_s, m_a_w_s, v_a_w_s, grad_a_w_s, "adam_a_w_s")
    res["a_b_s"] = (grad_a_b_s,) + plain(a_b_s, m_a_b_s, v_a_b_s, grad_a_b_s, "adam_a_b_s")
    res["b_conv_w"] = (grad_b_conv_w,) + plain(b_conv_w, m_b_conv_w, v_b_conv_w, grad_b_conv_w, "adam_b_conv")
    res["c_scale"] = (grad_c_scale,) + plain(c_scale, m_c_scale, v_c_scale, grad_c_scale, "adam_c_scale")

    order = ["norm_mix_g", "norm_ffn_g", "final_norm_g", "a_w_in", "a_v_norm_g", "a_w_s", "a_b_s", "a_w_out", "b_w_in",
             "b_conv_w", "b_w_out", "c_w_in", "c_w_grp", "c_scale", "c_w_out", "f_w_gate", "f_w_up", "f_w_down"]
    return (loss, grad_x, *[res[n][0] for n in order], *[res[n][1] for n in order],
            *[res[n][2] for n in order], *[res[n][3] for n in order])
```
